```python
import math
import jax, jax.numpy as jnp
from jax import lax
import numpy as np

D_MODEL = 1024
BATCH = 8
SEQ = 8192
DEPTH = 4

ATTN_HEADS_PER_GROUP = 8
ATTN_HEAD_DIM = 128
ATTN_WINDOWS = (128, 512, 2048)
ATTN_DILATIONS = (1, 4, 16)
N_ATTN_GROUPS = 3
ATTN_BLOCK = 128
ROPE_THETA = 500000.0
ROPE_DIM = ATTN_HEAD_DIM // 4
ATTN_QKV_WIDTH = N_ATTN_GROUPS * 3 * ATTN_HEADS_PER_GROUP * ATTN_HEAD_DIM
ATTN_OUT_WIDTH = ATTN_HEADS_PER_GROUP * ATTN_HEAD_DIM

SSM_EXPAND = 2
SSM_D_INNER = SSM_EXPAND * D_MODEL
SSM_HEAD_DIM = 64
SSM_HEADS = SSM_D_INNER // SSM_HEAD_DIM
SSM_STATE = 128
SSM_GROUPS = 8
SSM_CONV = 4
SSM_CHUNK = 128
SSM_CONV_DIM = SSM_D_INNER + 2 * SSM_GROUPS * SSM_STATE
SSM_IN_WIDTH = SSM_D_INNER + SSM_CONV_DIM + SSM_HEADS

D_FF = 2816
FFN_CONV = 3

NORM_EPS = 1e-5

kernel_name = "hybrid_dilated_attn_mamba2_convffn"


def rmsnorm(x, w):
    xf = x.astype(jnp.float32)
    y = xf * lax.rsqrt(jnp.mean(xf * xf, axis=-1, keepdims=True) + NORM_EPS)
    return (y * w.astype(jnp.float32)).astype(x.dtype)


def gated_group_rmsnorm(y, z, w, groups):
    g = y.astype(jnp.float32) * jax.nn.silu(z.astype(jnp.float32))
    shp = g.shape
    g = g.reshape(shp[:-1] + (groups, shp[-1] // groups))
    g = g * lax.rsqrt(jnp.mean(g * g, axis=-1, keepdims=True) + NORM_EPS)
    return g.reshape(shp) * w.astype(jnp.float32)


def causal_depthwise_conv(x, w, b):
    k = w.shape[0]
    y = lax.conv_general_dilated(
        x, w[:, None, :].astype(x.dtype), window_strides=(1,), padding=[(k - 1, 0)],
        dimension_numbers=("NWC", "WIO", "NWC"), feature_group_count=x.shape[-1])
    return y + b.astype(x.dtype)


def rope_tables(seq):
    pos = jnp.arange(seq, dtype=jnp.float32)
    inv_freq = ROPE_THETA ** (-jnp.arange(0, ROPE_DIM, 2, dtype=jnp.float32) / ROPE_DIM)
    ang = pos[:, None] * inv_freq[None, :]
    return jnp.cos(ang), jnp.sin(ang)


def apply_partial_rope(t, cos, sin):
    t = t.astype(jnp.float32)
    half = ROPE_DIM // 2
    c = cos[:, None, None, :]
    s = sin[:, None, None, :]
    x1 = t[..., :half]
    x2 = t[..., half:ROPE_DIM]
    return jnp.concatenate([x1 * c - x2 * s, x2 * c + x1 * s, t[..., ROPE_DIM:]], axis=-1)


def dilated_window_attention(q, k, v, dilation, steps):
    bsz, s, h, hd = q.shape
    length = s // dilation
    nb = -(-length // ATTN_BLOCK)
    lp = nb * ATTN_BLOCK

    def to_strided(t):
        t = t.reshape(bsz, length, dilation, h, hd).transpose(0, 2, 3, 1, 4)
        t = jnp.pad(t, ((0, 0), (0, 0), (0, 0), (0, lp - length), (0, 0)))
        return t.reshape(bsz, dilation, h, nb, ATTN_BLOCK, hd)

    def with_prev(t):
        prev = jnp.pad(t, ((0, 0), (0, 0), (0, 0), (1, 0), (0, 0), (0, 0)))[:, :, :, :-1]
        return jnp.concatenate([prev, t], axis=-2)

    qb = to_strided(q)
    kk = with_prev(to_strided(k))
    vv = with_prev(to_strided(v))
    scores = jnp.einsum("brhnqe,brhnke->brhnqk", qb, kk) * (hd ** -0.5)
    n_idx = jnp.arange(nb)[:, None, None]
    i_idx = jnp.arange(ATTN_BLOCK)[None, :, None]
    j_idx = jnp.arange(2 * ATTN_BLOCK)[None, None, :]
    delta = ATTN_BLOCK + i_idx - j_idx
    key_pos = (n_idx - 1) * ATTN_BLOCK + j_idx
    allowed = (delta >= 0) & (delta <= steps) & (key_pos >= 0)
    scores = jnp.where(allowed, scores, -jnp.inf)
    m = jnp.max(scores, axis=-1, keepdims=True)
    p = jnp.exp(scores - m)
    den = jnp.sum(p, axis=-1, keepdims=True)
    o = jnp.einsum("brhnqk,brhnke->brhnqe", p, vv) / den
    lse = (m + jnp.log(den))[..., 0]
    o = o.reshape(bsz, dilation, h, lp, hd)[:, :, :, :length]
    o = o.transpose(0, 3, 1, 2, 4).reshape(bsz, s, h, hd)
    lse = lse.reshape(bsz, dilation, h, lp)[:, :, :, :length]
    lse = lse.transpose(0, 3, 1, 2).reshape(bsz, s, h)
    return o, lse


def dilated_attention_mixer(h, w_qkv, w_o, cos, sin):
    bsz, s, _ = h.shape
    qkv = (h @ w_qkv).reshape(bsz, s, N_ATTN_GROUPS, 3, ATTN_HEADS_PER_GROUP, ATTN_HEAD_DIM)
    q = apply_partial_rope(qkv[:, :, :, 0], cos, sin)
    k = apply_partial_rope(qkv[:, :, :, 1], cos, sin)
    v = qkv[:, :, :, 2].astype(jnp.float32)
    outs, lses = [], []
    for g in range(N_ATTN_GROUPS):
        dil = ATTN_DILATIONS[g]
        o_g, lse_g = dilated_window_attention(q[:, :, g], k[:, :, g], v[:, :, g], dil,
                                              ATTN_WINDOWS[g] // dil)
        outs.append(o_g)
        lses.append(lse_g)
    wts = jax.nn.softmax(jnp.stack(lses, axis=2), axis=2)
    o = jnp.einsum("bsgh,bsghe->bshe", wts, jnp.stack(outs, axis=2))
    return o.reshape(bsz, s, ATTN_OUT_WIDTH).astype(h.dtype) @ w_o


def ssd_chunked_scan(x, dt, a, bm, cm):
    b, s, h, p = x.shape
    g, n = bm.shape[2], bm.shape[3]
    r = h // g
    c = s // SSM_CHUNK
    q = SSM_CHUNK
    x = x.reshape(b, c, q, g, r, p)
    dt = dt.reshape(b, c, q, g, r)
    bm = bm.reshape(b, c, q, g, n)
    cm = cm.reshape(b, c, q, g, n)
    a_dt = dt * a.reshape(g, r)
    a_cs = jnp.cumsum(a_dt, axis=2)
    xdt = x * dt[..., None]
    seg = a_cs[:, :, :, None] - a_cs[:, :, None, :]
    causal = jnp.tril(jnp.ones((q, q), dtype=bool))[:, :, None, None]
    lmat = jnp.exp(jnp.where(causal, seg, -jnp.inf))
    cb = jnp.einsum("bcign,bcjgn->bcijg", cm, bm)
    y_diag = jnp.einsum("bcijgr,bcjgrp->bcigrp", cb[..., None] * lmat, xdt)
    decay = jnp.exp(a_cs[:, :, -1:] - a_cs)
    states = jnp.einsum("bcjgn,bcjgr,bcjgrp->bcgrpn", bm, decay, xdt)
    chunk_decay = jnp.exp(a_cs[:, :, -1])

    def step(state, inp):
        st_c, dec_c = inp
        return state * dec_c[..., None, None] + st_c, state

    init = jnp.zeros((b, g, r, p, n), dtype=x.dtype)
    _, prev = lax.scan(step, init, (jnp.moveaxis(states, 1, 0), jnp.moveaxis(chunk_decay, 1, 0)))
    prev = jnp.moveaxis(prev, 0, 1)
    y_off = jnp.einsum("bcign,bcgrpn->bcigrp", cm, prev) * jnp.exp(a_cs)[..., None]
    return (y_diag + y_off).reshape(b, s, h, p)


def ssd_mixer(h, w_in, conv_w, conv_b, dt_bias, a_log, d_skip, norm_w, w_out):
    bsz, s, _ = h.shape
    gn = SSM_GROUPS * SSM_STATE
    zxbcdt = h @ w_in
    z = zxbcdt[..., :SSM_D_INNER]
    xbc = zxbcdt[..., SSM_D_INNER:SSM_D_INNER + SSM_CONV_DIM]
    dt_raw = zxbcdt[..., SSM_D_INNER + SSM_CONV_DIM:]
    xbc = jax.nn.silu(causal_depthwise_conv(xbc, conv_w, conv_b))
    xs = xbc[..., :SSM_D_INNER].reshape(bsz, s, SSM_HEADS, SSM_HEAD_DIM).astype(jnp.float32)
    bm = xbc[..., SSM_D_INNER:SSM_D_INNER + gn].reshape(bsz, s, SSM_GROUPS, SSM_STATE)
    cm = xbc[..., SSM_D_INNER + gn:].reshape(bsz, s, SSM_GROUPS, SSM_STATE)
    dt = jax.nn.softplus(dt_raw.astype(jnp.float32) + dt_bias.astype(jnp.float32))
    a = -jnp.exp(a_log.astype(jnp.float32))
    y = ssd_chunked_scan(xs, dt, a, bm.astype(jnp.float32), cm.astype(jnp.float32))
    y = y + d_skip.astype(jnp.float32)[:, None] * xs
    y = gated_group_rmsnorm(y.reshape(bsz, s, SSM_D_INNER), z, norm_w, SSM_GROUPS)
    return y.astype(h.dtype) @ w_out


def conv_ffn(h, w_up, conv_w, conv_b, w_down):
    u = causal_depthwise_conv(h @ w_up, conv_w, conv_b)
    gate, up = u[..., :D_FF], u[..., D_FF:]
    return (jax.nn.silu(gate) * up) @ w_down


def _fwd_setup_inputs(seed: int = 0) -> dict:
    key = jax.random.key(seed)
    ks = jax.random.split(key, 20)
    n_attn = (DEPTH + 1) // 2
    n_ssm = DEPTH // 2
    f32 = jnp.float32

    def nrm(k, shape, scale):
        return jax.random.normal(k, shape, dtype=f32) * scale

    u = jax.random.uniform(ks[7], (n_ssm, SSM_HEADS), dtype=f32)
    dt0 = jnp.exp(u * (math.log(0.1) - math.log(0.001)) + math.log(0.001))
    dt0 = jnp.maximum(dt0, 1e-4)
    dt_bias = dt0 + jnp.log(-jnp.expm1(-dt0))
    a_log = jnp.log(jax.random.uniform(ks[8], (n_ssm, SSM_HEADS), dtype=f32, minval=1.0, maxval=16.0))
    return {
        "x": nrm(ks[0], (BATCH, SEQ, D_MODEL), 1.0),
        "mix_norm_w": 1.0 + nrm(ks[1], (DEPTH, D_MODEL), 0.02),
        "attn_w_qkv": nrm(ks[2], (n_attn, D_MODEL, ATTN_QKV_WIDTH), D_MODEL ** -0.5),
        "attn_w_o": nrm(ks[3], (n_attn, ATTN_OUT_WIDTH, D_MODEL), ATTN_OUT_WIDTH ** -0.5),
        "ssm_w_in": nrm(ks[4], (n_ssm, D_MODEL, SSM_IN_WIDTH), D_MODEL ** -0.5),
        "ssm_conv_w": nrm(ks[5], (n_ssm, SSM_CONV, SSM_CONV_DIM), SSM_CONV ** -0.5),
        "ssm_conv_b": nrm(ks[6], (n_ssm, SSM_CONV_DIM), 0.01),
        "ssm_dt_bias": dt_bias,
        "ssm_a_log": a_log,
        "ssm_d": 1.0 + nrm(ks[9], (n_ssm, SSM_HEADS), 0.1),
        "ssm_norm_w": 1.0 + nrm(ks[10], (n_ssm, SSM_D_INNER), 0.02),
        "ssm_w_out": nrm(ks[11], (n_ssm, SSM_D_INNER, D_MODEL), SSM_D_INNER ** -0.5),
        "ffn_norm_w": 1.0 + nrm(ks[12], (DEPTH, D_MODEL), 0.02),
        "ffn_w_up": nrm(ks[13], (DEPTH, D_MODEL, 2 * D_FF), D_MODEL ** -0.5),
        "ffn_conv_w": nrm(ks[14], (DEPTH, FFN_CONV, 2 * D_FF), FFN_CONV ** -0.5),
        "ffn_conv_b": nrm(ks[15], (DEPTH, 2 * D_FF), 0.01),
        "ffn_w_down": nrm(ks[16], (DEPTH, D_FF, D_MODEL), D_FF ** -0.5),
        "final_norm_w": 1.0 + nrm(ks[17], (D_MODEL,), 0.02),
    }


def _fwd_reference(x, mix_norm_w, attn_w_qkv, attn_w_o, ssm_w_in, ssm_conv_w, ssm_conv_b,
              ssm_dt_bias, ssm_a_log, ssm_d, ssm_norm_w, ssm_w_out, ffn_norm_w, ffn_w_up,
              ffn_conv_w, ffn_conv_b, ffn_w_down, final_norm_w):
    cos, sin = rope_tables(x.shape[1])
    for i in range(DEPTH):
        h = rmsnorm(x, mix_norm_w[i])
        j = i // 2
        if i % 2 == 0:
            x = x + dilated_attention_mixer(h, attn_w_qkv[j], attn_w_o[j], cos, sin)
        else:
            x = x + ssd_mixer(h, ssm_w_in[j], ssm_conv_w[j], ssm_conv_b[j], ssm_dt_bias[j],
                              ssm_a_log[j], ssm_d[j], ssm_norm_w[j], ssm_w_out[j])
        h = rmsnorm(x, ffn_norm_w[i])
        x = x + conv_ffn(h, ffn_w_up[i], ffn_conv_w[i], ffn_conv_b[i], ffn_w_down[i])
    return rmsnorm(x, final_norm_w)


import jax as _jax
import jax.numpy as _jnp

TWIN_FORMAT = 'train_step'
FWD_PARAMS = ['x', 'mix_norm_w', 'attn_w_qkv', 'attn_w_o', 'ssm_w_in', 'ssm_conv_w', 'ssm_conv_b', 'ssm_dt_bias', 'ssm_a_log', 'ssm_d', 'ssm_norm_w', 'ssm_w_out', 'ffn_norm_w', 'ffn_w_up', 'ffn_conv_w', 'ffn_conv_b', 'ffn_w_down', 'final_norm_w']
TWIN_WEIGHTS = ['mix_norm_w', 'attn_w_qkv', 'attn_w_o', 'ssm_w_in', 'ssm_conv_w', 'ssm_conv_b', 'ssm_dt_bias', 'ssm_a_log', 'ssm_d', 'ssm_norm_w', 'ssm_w_out', 'ffn_norm_w', 'ffn_w_up', 'ffn_conv_w', 'ffn_conv_b', 'ffn_w_down', 'final_norm_w']
TWIN_DIFF_INPUT = 'x'
TWIN_INPUTS = ['x', 'mix_norm_w', 'attn_w_qkv', 'attn_w_o', 'ssm_w_in', 'ssm_conv_w', 'ssm_conv_b', 'ssm_dt_bias', 'ssm_a_log', 'ssm_d', 'ssm_norm_w', 'ssm_w_out', 'ffn_norm_w', 'ffn_w_up', 'ffn_conv_w', 'ffn_conv_b', 'ffn_w_down', 'final_norm_w', 'loss_target', 'm_mix_norm_w', 'm_attn_w_qkv', 'm_attn_w_o', 'm_ssm_w_in', 'm_ssm_conv_w', 'm_ssm_conv_b', 'm_ssm_dt_bias', 'm_ssm_a_log', 'm_ssm_d', 'm_ssm_norm_w', 'm_ssm_w_out', 'm_ffn_norm_w', 'm_ffn_w_up', 'm_ffn_conv_w', 'm_ffn_conv_b', 'm_ffn_w_down', 'm_final_norm_w', 'v_mix_norm_w', 'v_attn_w_qkv', 'v_attn_w_o', 'v_ssm_w_in', 'v_ssm_conv_w', 'v_ssm_conv_b', 'v_ssm_dt_bias', 'v_ssm_a_log', 'v_ssm_d', 'v_ssm_norm_w', 'v_ssm_w_out', 'v_ffn_norm_w', 'v_ffn_w_up', 'v_ffn_conv_w', 'v_ffn_conv_b', 'v_ffn_w_down', 'v_final_norm_w']
TWIN_OUTPUTS = ['loss', 'grad_x', 'grad_mix_norm_w', 'grad_attn_w_qkv', 'grad_attn_w_o', 'grad_ssm_w_in', 'grad_ssm_conv_w', 'grad_ssm_conv_b', 'grad_ssm_dt_bias', 'grad_ssm_a_log', 'grad_ssm_d', 'grad_ssm_norm_w', 'grad_ssm_w_out', 'grad_ffn_norm_w', 'grad_ffn_w_up', 'grad_ffn_conv_w', 'grad_ffn_conv_b', 'grad_ffn_w_down', 'grad_final_norm_w', 'delta_mix_norm_w', 'delta_attn_w_qkv', 'delta_attn_w_o', 'delta_ssm_w_in', 'delta_ssm_conv_w', 'delta_ssm_conv_b', 'delta_ssm_dt_bias', 'delta_ssm_a_log', 'delta_ssm_d', 'delta_ssm_norm_w', 'delta_ssm_w_out', 'delta_ffn_norm_w', 'delta_ffn_w_up', 'delta_ffn_conv_w', 'delta_ffn_conv_b', 'delta_ffn_w_down', 'delta_final_norm_w', 'new_m_mix_norm_w', 'new_m_attn_w_qkv', 'new_m_attn_w_o', 'new_m_ssm_w_in', 'new_m_ssm_conv_w', 'new_m_ssm_conv_b', 'new_m_ssm_dt_bias', 'new_m_ssm_a_log', 'new_m_ssm_d', 'new_m_ssm_norm_w', 'new_m_ssm_w_out', 'new_m_ffn_norm_w', 'new_m_ffn_w_up', 'new_m_ffn_conv_w', 'new_m_ffn_conv_b', 'new_m_ffn_w_down', 'new_m_final_norm_w', 'new_v_mix_norm_w', 'new_v_attn_w_qkv', 'new_v_attn_w_o', 'new_v_ssm_w_in', 'new_v_ssm_conv_w', 'new_v_ssm_conv_b', 'new_v_ssm_dt_bias', 'new_v_ssm_a_log', 'new_v_ssm_d', 'new_v_ssm_norm_w', 'new_v_ssm_w_out', 'new_v_ffn_norm_w', 'new_v_ffn_w_up', 'new_v_ffn_conv_w', 'new_v_ffn_conv_b', 'new_v_ffn_w_down', 'new_v_final_norm_w']
TWIN_LEAF_KINDS = {'loss': 'loss', 'grad_x': 'grad_x', 'grad_mix_norm_w': 'grad_w', 'grad_attn_w_qkv': 'grad_w', 'grad_attn_w_o': 'grad_w', 'grad_ssm_w_in': 'grad_w', 'grad_ssm_conv_w': 'grad_w', 'grad_ssm_conv_b': 'grad_w', 'grad_ssm_dt_bias': 'grad_w', 'grad_ssm_a_log': 'grad_w', 'grad_ssm_d': 'grad_w', 'grad_ssm_norm_w': 'grad_w', 'grad_ssm_w_out': 'grad_w', 'grad_ffn_norm_w': 'grad_w', 'grad_ffn_w_up': 'grad_w', 'grad_ffn_conv_w': 'grad_w', 'grad_ffn_conv_b': 'grad_w', 'grad_ffn_w_down': 'grad_w', 'grad_final_norm_w': 'grad_w', 'delta_mix_norm_w': 'delta_w', 'delta_attn_w_qkv': 'delta_w', 'delta_attn_w_o': 'delta_w', 'delta_ssm_w_in': 'delta_w', 'delta_ssm_conv_w': 'delta_w', 'delta_ssm_conv_b': 'delta_w', 'delta_ssm_dt_bias': 'delta_w', 'delta_ssm_a_log': 'delta_w', 'delta_ssm_d': 'delta_w', 'delta_ssm_norm_w': 'delta_w', 'delta_ssm_w_out': 'delta_w', 'delta_ffn_norm_w': 'delta_w', 'delta_ffn_w_up': 'delta_w', 'delta_ffn_conv_w': 'delta_w', 'delta_ffn_conv_b': 'delta_w', 'delta_ffn_w_down': 'delta_w', 'delta_final_norm_w': 'delta_w', 'new_m_mix_norm_w': 'new_m', 'new_m_attn_w_qkv': 'new_m', 'new_m_attn_w_o': 'new_m', 'new_m_ssm_w_in': 'new_m', 'new_m_ssm_conv_w': 'new_m', 'new_m_ssm_conv_b': 'new_m', 'new_m_ssm_dt_bias': 'new_m', 'new_m_ssm_a_log': 'new_m', 'new_m_ssm_d': 'new_m', 'new_m_ssm_norm_w': 'new_m', 'new_m_ssm_w_out': 'new_m', 'new_m_ffn_norm_w': 'new_m', 'new_m_ffn_w_up': 'new_m', 'new_m_ffn_conv_w': 'new_m', 'new_m_ffn_conv_b': 'new_m', 'new_m_ffn_w_down': 'new_m', 'new_m_final_norm_w': 'new_m', 'new_v_mix_norm_w': 'new_v', 'new_v_attn_w_qkv': 'new_v', 'new_v_attn_w_o': 'new_v', 'new_v_ssm_w_in': 'new_v', 'new_v_ssm_conv_w': 'new_v', 'new_v_ssm_conv_b': 'new_v', 'new_v_ssm_dt_bias': 'new_v', 'new_v_ssm_a_log': 'new_v', 'new_v_ssm_d': 'new_v', 'new_v_ssm_norm_w': 'new_v', 'new_v_ssm_w_out': 'new_v', 'new_v_ffn_norm_w': 'new_v', 'new_v_ffn_w_up': 'new_v', 'new_v_ffn_conv_w': 'new_v', 'new_v_ffn_conv_b': 'new_v', 'new_v_ffn_w_down': 'new_v', 'new_v_final_norm_w': 'new_v'}


def _forward(args):
    return _fwd_reference(*[args[k] for k in FWD_PARAMS])


def _output_shape():
    def fwd():
        inp = _fwd_setup_inputs(0)
        return _fwd_reference(*[inp[k] for k in FWD_PARAMS])
    out = _jax.eval_shape(fwd)
    return out.shape, out.dtype

N_MICROBATCH = 1
ADAM_LR = 0.001
ADAM_B1 = 0.9
ADAM_B2 = 0.999
ADAM_EPS = 1e-08
ADAM_WD = 0.01
ADAM_STEP = 10
PER_EXAMPLE_BATCH_AXIS = {'x': 0, 'loss_target': 0}
SHARED_INPUTS = []
_WEIGHT_DTYPES = {'mix_norm_w': _jnp.float32, 'attn_w_qkv': _jnp.float32, 'attn_w_o': _jnp.float32, 'ssm_w_in': _jnp.float32, 'ssm_conv_w': _jnp.float32, 'ssm_conv_b': _jnp.float32, 'ssm_dt_bias': _jnp.float32, 'ssm_a_log': _jnp.float32, 'ssm_d': _jnp.float32, 'ssm_norm_w': _jnp.float32, 'ssm_w_out': _jnp.float32, 'ffn_norm_w': _jnp.float32, 'ffn_w_up': _jnp.float32, 'ffn_conv_w': _jnp.float32, 'ffn_conv_b': _jnp.float32, 'ffn_w_down': _jnp.float32, 'final_norm_w': _jnp.float32}
MOMENT_SCALE = {'mix_norm_w': 2.053202e-01, 'attn_w_qkv': 3.618497e-02, 'attn_w_o': 7.017477e-02, 'ssm_w_in': 1.093179e-01, 'ssm_conv_w': 9.775130e-02, 'ssm_conv_b': 1.509090e-01, 'ssm_dt_bias': 2.483129e-01, 'ssm_a_log': 3.469728e-01, 'ssm_d': 7.656358e-01, 'ssm_norm_w': 1.356317e-01, 'ssm_w_out': 1.838720e-01, 'ffn_norm_w': 2.047840e-01, 'ffn_w_up': 8.470000e-02, 'ffn_conv_w': 8.600983e-02, 'ffn_conv_b': 8.871279e-02, 'ffn_w_down': 1.385139e-01, 'final_norm_w': 6.404798e+01}


def _to_microbatches(a, axis):
    t = _jnp.moveaxis(a, axis, 0)
    t = t.reshape((N_MICROBATCH, t.shape[0] // N_MICROBATCH) + t.shape[1:])
    return _jnp.moveaxis(t, 1, axis + 1)


def setup_inputs(seed: int = 0) -> dict:
    inp = _fwd_setup_inputs(seed)
    key = _jax.random.fold_in(_jax.random.key(seed), 7919)
    shape, _ = _output_shape()
    out = dict(inp)
    out["loss_target"] = _jax.random.normal(_jax.random.fold_in(key, 0), shape, _jnp.float32)
    for i, name in enumerate(TWIN_WEIGHTS):
        w = inp[name].astype(_jnp.float32)
        if MOMENT_SCALE is None:
            s = _jnp.sqrt(_jnp.mean(_jnp.square(w)) + 1e-30)
        else:
            s = MOMENT_SCALE[name]
        km, kv = _jax.random.split(_jax.random.fold_in(key, i + 1))
        out[name] = w
        out["m_" + name] = s * _jax.random.normal(km, w.shape, _jnp.float32)
        out["v_" + name] = (s * s) * _jax.random.uniform(kv, w.shape, _jnp.float32, 0.5, 1.5)
    if N_MICROBATCH > 1:
        for name, axis in PER_EXAMPLE_BATCH_AXIS.items():
            out[name] = _to_microbatches(out[name], axis)
    return {'x': out['x'], 'mix_norm_w': out['mix_norm_w'], 'attn_w_qkv': out['attn_w_qkv'], 'attn_w_o': out['attn_w_o'], 'ssm_w_in': out['ssm_w_in'], 'ssm_conv_w': out['ssm_conv_w'], 'ssm_conv_b': out['ssm_conv_b'], 'ssm_dt_bias': out['ssm_dt_bias'], 'ssm_a_log': out['ssm_a_log'], 'ssm_d': out['ssm_d'], 'ssm_norm_w': out['ssm_norm_w'], 'ssm_w_out': out['ssm_w_out'], 'ffn_norm_w': out['ffn_norm_w'], 'ffn_w_up': out['ffn_w_up'], 'ffn_conv_w': out['ffn_conv_w'], 'ffn_conv_b': out['ffn_conv_b'], 'ffn_w_down': out['ffn_w_down'], 'final_norm_w': out['final_norm_w'], 'loss_target': out['loss_target'], 'm_mix_norm_w': out['m_mix_norm_w'], 'm_attn_w_qkv': out['m_attn_w_qkv'], 'm_attn_w_o': out['m_attn_w_o'], 'm_ssm_w_in': out['m_ssm_w_in'], 'm_ssm_conv_w': out['m_ssm_conv_w'], 'm_ssm_conv_b': out['m_ssm_conv_b'], 'm_ssm_dt_bias': out['m_ssm_dt_bias'], 'm_ssm_a_log': out['m_ssm_a_log'], 'm_ssm_d': out['m_ssm_d'], 'm_ssm_norm_w': out['m_ssm_norm_w'], 'm_ssm_w_out': out['m_ssm_w_out'], 'm_ffn_norm_w': out['m_ffn_norm_w'], 'm_ffn_w_up': out['m_ffn_w_up'], 'm_ffn_conv_w': out['m_ffn_conv_w'], 'm_ffn_conv_b': out['m_ffn_conv_b'], 'm_ffn_w_down': out['m_ffn_w_down'], 'm_final_norm_w': out['m_final_norm_w'], 'v_mix_norm_w': out['v_mix_norm_w'], 'v_attn_w_qkv': out['v_attn_w_qkv'], 'v_attn_w_o': out['v_attn_w_o'], 'v_ssm_w_in': out['v_ssm_w_in'], 'v_ssm_conv_w': out['v_ssm_conv_w'], 'v_ssm_conv_b': out['v_ssm_conv_b'], 'v_ssm_dt_bias': out['v_ssm_dt_bias'], 'v_ssm_a_log': out['v_ssm_a_log'], 'v_ssm_d': out['v_ssm_d'], 'v_ssm_norm_w': out['v_ssm_norm_w'], 'v_ssm_w_out': out['v_ssm_w_out'], 'v_ffn_norm_w': out['v_ffn_norm_w'], 'v_ffn_w_up': out['v_ffn_w_up'], 'v_ffn_conv_w': out['v_ffn_conv_w'], 'v_ffn_conv_b': out['v_ffn_conv_b'], 'v_ffn_w_down': out['v_ffn_w_down'], 'v_final_norm_w': out['v_final_norm_w']}


def _loss(weights, diff, rest, loss_target):
    with _jax.named_scope("forward"):
        args = {**rest, TWIN_DIFF_INPUT: diff, **{k: w.astype(_WEIGHT_DTYPES[k]) for k, w in weights.items()}}
        y = _forward(args)
    with _jax.named_scope("loss_head"):
        err = _jnp.square(y.astype(_jnp.float32) - loss_target)
        return 0.5 * _jnp.sum(_jnp.mean(err, axis=-1)) if err.ndim else 0.5 * err


def _adamw(w, g, m, v):
    m = ADAM_B1 * m + (1.0 - ADAM_B1) * g
    v = ADAM_B2 * v + (1.0 - ADAM_B2) * _jnp.square(g)
    m_hat = m / (1.0 - ADAM_B1 ** ADAM_STEP)
    v_hat = v / (1.0 - ADAM_B2 ** ADAM_STEP)
    delta = -ADAM_LR * (m_hat / (_jnp.sqrt(v_hat) + ADAM_EPS) + ADAM_WD * w)
    return delta, m, v


def reference(x, mix_norm_w, attn_w_qkv, attn_w_o, ssm_w_in, ssm_conv_w, ssm_conv_b, ssm_dt_bias, ssm_a_log, ssm_d, ssm_norm_w, ssm_w_out, ffn_norm_w, ffn_w_up, ffn_conv_w, ffn_conv_b, ffn_w_down, final_norm_w, loss_target, m_mix_norm_w, m_attn_w_qkv, m_attn_w_o, m_ssm_w_in, m_ssm_conv_w, m_ssm_conv_b, m_ssm_dt_bias, m_ssm_a_log, m_ssm_d, m_ssm_norm_w, m_ssm_w_out, m_ffn_norm_w, m_ffn_w_up, m_ffn_conv_w, m_ffn_conv_b, m_ffn_w_down, m_final_norm_w, v_mix_norm_w, v_attn_w_qkv, v_attn_w_o, v_ssm_w_in, v_ssm_conv_w, v_ssm_conv_b, v_ssm_dt_bias, v_ssm_a_log, v_ssm_d, v_ssm_norm_w, v_ssm_w_out, v_ffn_norm_w, v_ffn_w_up, v_ffn_conv_w, v_ffn_conv_b, v_ffn_w_down, v_final_norm_w):
    given = dict(x=x, mix_norm_w=mix_norm_w, attn_w_qkv=attn_w_qkv, attn_w_o=attn_w_o, ssm_w_in=ssm_w_in, ssm_conv_w=ssm_conv_w, ssm_conv_b=ssm_conv_b, ssm_dt_bias=ssm_dt_bias, ssm_a_log=ssm_a_log, ssm_d=ssm_d, ssm_norm_w=ssm_norm_w, ssm_w_out=ssm_w_out, ffn_norm_w=ffn_norm_w, ffn_w_up=ffn_w_up, ffn_conv_w=ffn_conv_w, ffn_conv_b=ffn_conv_b, ffn_w_down=ffn_w_down, final_norm_w=final_norm_w, loss_target=loss_target, m_mix_norm_w=m_mix_norm_w, m_attn_w_qkv=m_attn_w_qkv, m_attn_w_o=m_attn_w_o, m_ssm_w_in=m_ssm_w_in, m_ssm_conv_w=m_ssm_conv_w, m_ssm_conv_b=m_ssm_conv_b, m_ssm_dt_bias=m_ssm_dt_bias, m_ssm_a_log=m_ssm_a_log, m_ssm_d=m_ssm_d, m_ssm_norm_w=m_ssm_norm_w, m_ssm_w_out=m_ssm_w_out, m_ffn_norm_w=m_ffn_norm_w, m_ffn_w_up=m_ffn_w_up, m_ffn_conv_w=m_ffn_conv_w, m_ffn_conv_b=m_ffn_conv_b, m_ffn_w_down=m_ffn_w_down, m_final_norm_w=m_final_norm_w, v_mix_norm_w=v_mix_norm_w, v_attn_w_qkv=v_attn_w_qkv, v_attn_w_o=v_attn_w_o, v_ssm_w_in=v_ssm_w_in, v_ssm_conv_w=v_ssm_conv_w, v_ssm_conv_b=v_ssm_conv_b, v_ssm_dt_bias=v_ssm_dt_bias, v_ssm_a_log=v_ssm_a_log, v_ssm_d=v_ssm_d, v_ssm_norm_w=v_ssm_norm_w, v_ssm_w_out=v_ssm_w_out, v_ffn_norm_w=v_ffn_norm_w, v_ffn_w_up=v_ffn_w_up, v_ffn_conv_w=v_ffn_conv_w, v_ffn_conv_b=v_ffn_conv_b, v_ffn_w_down=v_ffn_w_down, v_final_norm_w=v_final_norm_w)
    weights = {n: given[n] for n in TWIN_WEIGHTS}
    shared = {n: given[n] for n in SHARED_INPUTS}
    per_example = {n: given[n] for n in ['x']}
    grad_fn = _jax.value_and_grad(_loss, argnums=(0, 1))

    def one_microbatch(ex, loss_target):
        ex = dict(ex)
        diff = ex.pop(TWIN_DIFF_INPUT)
        return grad_fn(weights, diff, {**shared, **ex}, loss_target)

    if N_MICROBATCH == 1:
        loss, (grad_w, grad_x) = one_microbatch(per_example, given["loss_target"])
    else:
        def body(carry, xs):
            loss_sum, grad_sum = carry
            l_k, (gw_k, gx_k) = one_microbatch(xs[0], xs[1])
            with _jax.named_scope("update"):
                return (loss_sum + l_k, _jax.tree.map(_jnp.add, grad_sum, gw_k)), gx_k

        init = (_jnp.zeros((), _jnp.float32), _jax.tree.map(_jnp.zeros_like, weights))
        (loss, grad_w), grad_x = _jax.lax.scan(body, init, (per_example, given["loss_target"]))
    with _jax.named_scope("update"):
        delta_w, new_m, new_v = {}, {}, {}
        for n in TWIN_WEIGHTS:
            delta_w[n], new_m[n], new_v[n] = _adamw(weights[n], grad_w[n], given["m_" + n], given["v_" + n])
    return (loss, grad_x, *[grad_w[n] for n in TWIN_WEIGHTS], *[delta_w[n] for n in TWIN_WEIGHTS],
            *[new_m[n] for n in TWIN_WEIGHTS], *[new_v[n] for n in TWIN_WEIGHTS])
```

```python
import functools
import math

import jax
import jax.numpy as jnp
from jax import lax
from jax.experimental import pallas as pl
from jax.experimental.pallas import tpu as pltpu

F32 = jnp.float32
BF16 = jnp.bfloat16
MESH = pl.DeviceIdType.MESH

NORM_EPS = 1e-5
HEAD_DIM = 128
ATTN_BLOCK = 128
ATTN_DILATIONS = (1, 4, 16)
ATTN_WINDOWS = (128, 512, 2048)
ROPE_THETA = 500000.0
ROPE_HALF = HEAD_DIM // 8
SSM_HEAD_DIM = 64
SSM_STATE = 128
SSM_GROUPS = 8
SSM_CHUNK = 128
NEG = -1e30

ADAM_LR = 0.001
ADAM_B1 = 0.9
ADAM_B2 = 0.999
ADAM_EPS = 1e-08
ADAM_WD = 0.01
ADAM_STEP = 10

VMEM_LIMIT_BYTES = 48 * 1024 * 1024
N_CHIPS = 4
N_DEV = 8


def _cp(*sem):
    return pltpu.CompilerParams(dimension_semantics=sem, vmem_limit_bytes=VMEM_LIMIT_BYTES)


def _pick(n, pref, mult=128):
    best = None
    t = mult
    while t <= min(n, pref):
        if n % t == 0:
            best = t
        t += mult
    return n if best is None else best


def _sigmoid(x):
    return 1.0 / (1.0 + jnp.exp(-x))


def _softplus(x):
    u = jnp.exp(-jnp.abs(x))
    w = 1.0 + u
    log1p = jnp.where(w == 1.0, u, jnp.log(w) * (u / jnp.where(w == 1.0, 1.0, w - 1.0)))
    return jnp.maximum(x, 0.0) + log1p


def _dot(a, b, dims):
    return lax.dot_general(a.astype(BF16), b.astype(BF16), (dims, ((), ())),
                           preferred_element_type=F32)


def _dot_nn(a, b):
    return _dot(a, b, ((1,), (0,)))


def _dot_nt(a, b):
    return _dot(a, b, ((1,), (1,)))


def _dot_tn(a, b):
    return _dot(a, b, ((0,), (0,)))


def _matmul(a, b, mode, out_dtype, name, resid=None, tm=512, tn=1408, tk=1024):
    if mode == "nn":
        (M, K), (K2, N) = a.shape, b.shape
    elif mode == "nt":
        (M, K), (N, K2) = a.shape, b.shape
    else:
        (K, M), (K2, N) = a.shape, b.shape
    assert K == K2, (a.shape, b.shape, mode)
    tm, tn, tk = _pick(M, tm), _pick(N, tn), _pick(K, tk)
    nk = K // tk
    dims = {"nn": ((1,), (0,)), "nt": ((1,), (1,)), "tn": ((0,), (0,))}[mode]
    has_resid = resid is not None

    def body(a_ref, b_ref, *rest):
        if has_resid:
            r_ref, o_ref, acc_ref = rest
        else:
            o_ref, acc_ref = rest
        k = pl.program_id(2)

        @pl.when(k == 0)
        def _():
            acc_ref[...] = jnp.zeros_like(acc_ref)

        acc_ref[...] += _dot(a_ref[...], b_ref[...], dims)

        @pl.when(k == nk - 1)
        def _():
            r = acc_ref[...]
            if has_resid:
                r = r + r_ref[...]
            o_ref[...] = r.astype(o_ref.dtype)

    if mode == "nn":
        a_spec = pl.BlockSpec((tm, tk), lambda i, j, k: (i, k))
        b_spec = pl.BlockSpec((tk, tn), lambda i, j, k: (k, j))
    elif mode == "nt":
        a_spec = pl.BlockSpec((tm, tk), lambda i, j, k: (i, k))
        b_spec = pl.BlockSpec((tn, tk), lambda i, j, k: (j, k))
    else:
        a_spec = pl.BlockSpec((tk, tm), lambda i, j, k: (k, i))
        b_spec = pl.BlockSpec((tk, tn), lambda i, j, k: (k, j))
    o_spec = pl.BlockSpec((tm, tn), lambda i, j, k: (i, j))
    in_specs = [a_spec, b_spec] + ([o_spec] if has_resid else [])
    args = (a, b) + ((resid,) if has_resid else ())
    return pl.pallas_call(
        body, grid=(M // tm, N // tn, nk), in_specs=in_specs, out_specs=o_spec,
        out_shape=jax.ShapeDtypeStruct((M, N), out_dtype),
        scratch_shapes=[pltpu.VMEM((tm, tn), F32)],
        compiler_params=_cp("parallel", "parallel", "arbitrary"), name=name)(*args)


def _rms_fwd(x, w, name):
    S, D = x.shape
    t = _pick(S, 512, 8)

    def body(x_ref, w_ref, o_ref):
        xv = x_ref[...]
        r = lax.rsqrt(jnp.mean(xv * xv, axis=-1, keepdims=True) + NORM_EPS)
        o_ref[...] = (xv * r * w_ref[...]).astype(o_ref.dtype)

    row = pl.BlockSpec((t, D), lambda i: (i, 0))
    vec = pl.BlockSpec((1, D), lambda i: (0, 0))
    return pl.pallas_call(body, grid=(S // t,), in_specs=[row, vec], out_specs=row,
                          out_shape=jax.ShapeDtypeStruct((S, D), BF16),
                          compiler_params=_cp("parallel"), name=name)(x, w.reshape(1, D))


def _rms_bwd(x, w, dh, dres, name):
    S, D = x.shape
    t = _pick(S, 512, 8)

    def body(x_ref, w_ref, dh_ref, dr_ref, dx_ref, dw_ref):
        @pl.when(pl.program_id(0) == 0)
        def _():
            dw_ref[...] = jnp.zeros_like(dw_ref)

        xv = x_ref[...]
        r = lax.rsqrt(jnp.mean(xv * xv, axis=-1, keepdims=True) + NORM_EPS)
        xh = xv * r
        dh_v = dh_ref[...]
        g = dh_v * w_ref[...]
        dx_ref[...] = dr_ref[...] + r * (g - xh * jnp.mean(g * xh, axis=-1, keepdims=True))
        dw_ref[...] += jnp.sum(dh_v * xh, axis=0, keepdims=True)

    row = pl.BlockSpec((t, D), lambda i: (i, 0))
    vec = pl.BlockSpec((1, D), lambda i: (0, 0))
    return pl.pallas_call(
        body, grid=(S // t,), in_specs=[row, vec, row, row], out_specs=[row, vec],
        out_shape=[jax.ShapeDtypeStruct((S, D), F32), jax.ShapeDtypeStruct((1, D), F32)],
        compiler_params=_cp("arbitrary"), name=name)(x, w.reshape(1, D), dh, dres)


def _loss_head(x, w, tgt, name):
    S, D = x.shape
    t = _pick(S, 512, 8)

    def body(x_ref, w_ref, t_ref, dx_ref, dw_ref, l_ref):
        @pl.when(pl.program_id(0) == 0)
        def _():
            dw_ref[...] = jnp.zeros_like(dw_ref)
            l_ref[...] = jnp.zeros_like(l_ref)

        xv = x_ref[...]
        wv = w_ref[...]
        r = lax.rsqrt(jnp.mean(xv * xv, axis=-1, keepdims=True) + NORM_EPS)
        xh = xv * r
        err = xh * wv - t_ref[...]
        per_tok = jnp.mean(err * err, axis=-1, keepdims=True)
        l_ref[...] += 0.5 * jnp.sum(per_tok, axis=0, keepdims=True)
        dy = err * (1.0 / D)
        g = dy * wv
        dx_ref[...] = r * (g - xh * jnp.mean(g * xh, axis=-1, keepdims=True))
        dw_ref[...] += jnp.sum(dy * xh, axis=0, keepdims=True)

    row = pl.BlockSpec((t, D), lambda i: (i, 0))
    vec = pl.BlockSpec((1, D), lambda i: (0, 0))
    lspec = pl.BlockSpec((1, 128), lambda i: (0, 0))
    return pl.pallas_call(
        body, grid=(S // t,), in_specs=[row, vec, row], out_specs=[row, vec, lspec],
        out_shape=[jax.ShapeDtypeStruct((S, D), F32), jax.ShapeDtypeStruct((1, D), F32),
                   jax.ShapeDtypeStruct((1, 128), F32)],
        compiler_params=_cp("arbitrary"), name=name)(x, w.reshape(1, D), tgt)


def _rope_table(seq):
    pos = jnp.arange(seq, dtype=F32)
    inv_freq = ROPE_THETA ** (-jnp.arange(0, 2 * ROPE_HALF, 2, dtype=F32) / (2 * ROPE_HALF))
    ang = pos[:, None] * inv_freq[None, :]
    cos, sin = jnp.cos(ang), jnp.sin(ang)
    pad = HEAD_DIM - 2 * ROPE_HALF
    cos_p = jnp.concatenate([cos, cos, jnp.ones((seq, pad), F32)], axis=1)
    sin_a = jnp.concatenate([-sin, jnp.zeros((seq, HEAD_DIM - ROPE_HALF), F32)], axis=1)
    sin_b = jnp.concatenate([jnp.zeros((seq, ROPE_HALF), F32), sin, jnp.zeros((seq, pad), F32)], axis=1)
    return jnp.concatenate([cos_p, sin_a, sin_b], axis=1)


def _rope(t, tab, sign):
    cos_p = tab[:, 0:HEAD_DIM]
    sin_a = tab[:, HEAD_DIM:2 * HEAD_DIM]
    sin_b = tab[:, 2 * HEAD_DIM:3 * HEAD_DIM]
    up = pltpu.roll(t, HEAD_DIM - ROPE_HALF, 1)
    down = pltpu.roll(t, ROPE_HALF, 1)
    return t * cos_p + sign * (up * sin_a + down * sin_b)


def _attn_fwd(qkv, tab, g, heads, name):
    S = qkv.shape[0]
    W = heads * HEAD_DIM
    dil = ATTN_DILATIONS[g]
    steps = ATTN_WINDOWS[g] // dil
    L = S // dil
    nb = L // ATTN_BLOCK
    B = ATTN_BLOCK
    scale = HEAD_DIM ** -0.5
    qv = qkv.reshape(L, dil * 9 * W)
    tv = tab.reshape(L, dil * 3 * HEAD_DIM)

    def body(q_ref, k_ref, kp_ref, v_ref, vp_ref, t_ref, tp_ref, o_ref, l_ref):
        n = pl.program_id(1)
        ii = lax.broadcasted_iota(jnp.int32, (B, 2 * B), 0)
        jj = lax.broadcasted_iota(jnp.int32, (B, 2 * B), 1)
        delta = B + ii - jj
        ok = (delta >= 0) & (delta <= steps) & ((jj >= B) | (n > 0))
        tb = t_ref[...]
        tpv = tp_ref[...]
        for h in range(heads):
            sl = slice(h * HEAD_DIM, (h + 1) * HEAD_DIM)
            q = _rope(q_ref[:, sl], tb, 1.0)
            kc = jnp.concatenate([_rope(kp_ref[:, sl], tpv, 1.0), _rope(k_ref[:, sl], tb, 1.0)], axis=0)
            vc = jnp.concatenate([vp_ref[:, sl], v_ref[:, sl]], axis=0)
            s = jnp.where(ok, _dot_nt(q, kc) * scale, NEG)
            m = jnp.max(s, axis=-1, keepdims=True)
            p = jnp.exp(s - m)
            den = jnp.sum(p, axis=-1, keepdims=True)
            o_ref[:, sl] = _dot_nn(p, vc) / den
            l_ref[:, sl] = jnp.broadcast_to(m + jnp.log(den), (B, HEAD_DIM))

    def cur(j):
        return pl.BlockSpec((B, W), lambda r, n: (n, r * 9 + g * 3 + j))

    def prev(j):
        return pl.BlockSpec((B, W), lambda r, n: (jnp.maximum(n - 1, 0), r * 9 + g * 3 + j))

    t_cur = pl.BlockSpec((B, 3 * HEAD_DIM), lambda r, n: (n, r))
    t_prev = pl.BlockSpec((B, 3 * HEAD_DIM), lambda r, n: (jnp.maximum(n - 1, 0), r))
    o_spec = pl.BlockSpec((B, W), lambda r, n: (n, r))
    o, lse = pl.pallas_call(
        body, grid=(dil, nb),
        in_specs=[cur(0), cur(1), prev(1), cur(2), prev(2), t_cur, t_prev],
        out_specs=[o_spec, o_spec],
        out_shape=[jax.ShapeDtypeStruct((L, dil * W), F32)] * 2,
        compiler_params=_cp("parallel", "parallel"), name=name)(qv, qv, qv, qv, qv, tv, tv)
    return o.reshape(S, W), lse.reshape(S, W)


def _attn_combine(os_, ls_, name):
    S, W = os_[0].shape
    t = _pick(S, 256, 8)

    def body(o0, o1, o2, l0, l1, l2, o_ref, l_ref):
        a, b, c = l0[...], l1[...], l2[...]
        m = jnp.maximum(jnp.maximum(a, b), c)
        ea, eb, ec = jnp.exp(a - m), jnp.exp(b - m), jnp.exp(c - m)
        tot = ea + eb + ec
        o_ref[...] = (ea * o0[...] + eb * o1[...] + ec * o2[...]) / tot
        l_ref[...] = m + jnp.log(tot)

    row = pl.BlockSpec((t, W), lambda i: (i, 0))
    return pl.pallas_call(body, grid=(S // t,), in_specs=[row] * 6, out_specs=[row, row],
                          out_shape=[jax.ShapeDtypeStruct((S, W), F32)] * 2,
                          compiler_params=_cp("parallel"), name=name)(*os_, *ls_)


def _attn_bwd(qkv, tab, o, lse, do, dqkv_prev, g, heads, name):
    S = qkv.shape[0]
    W = heads * HEAD_DIM
    dil = ATTN_DILATIONS[g]
    steps = ATTN_WINDOWS[g] // dil
    L = S // dil
    nb = L // ATTN_BLOCK
    B = ATTN_BLOCK
    scale = HEAD_DIM ** -0.5
    qv = qkv.reshape(L, dil * 9 * W)
    tv = tab.reshape(L, dil * 3 * HEAD_DIM)
    ov, lv, dov = (a.reshape(L, dil * W) for a in (o, lse, do))
    aliased = dqkv_prev is not None

    def body(q_ref, qn_ref, k_ref, kp_ref, v_ref, vp_ref, do_ref, don_ref, o_ref, on_ref,
             l_ref, ln_ref, t_ref, tp_ref, tn_ref, *rest):
        out_ref = rest[-1]
        n = pl.program_id(1)
        has_next = n < nb - 1
        ia = lax.broadcasted_iota(jnp.int32, (B, 2 * B), 0)
        ja = lax.broadcasted_iota(jnp.int32, (B, 2 * B), 1)
        da = B + ia - ja
        ok_a = (da >= 0) & (da <= steps) & ((ja >= B) | (n > 0))
        ib = lax.broadcasted_iota(jnp.int32, (2 * B, B), 0)
        jb = lax.broadcasted_iota(jnp.int32, (2 * B, B), 1)
        db = ib - jb
        ok_b = (db >= 0) & (db <= steps) & ((ib < B) | has_next)
        tb, tpv, tnv = t_ref[...], tp_ref[...], tn_ref[...]
        for h in range(heads):
            sl = slice(h * HEAD_DIM, (h + 1) * HEAD_DIM)
            qr = _rope(q_ref[:, sl], tb, 1.0)
            qnr = _rope(qn_ref[:, sl], tnv, 1.0)
            kr = _rope(k_ref[:, sl], tb, 1.0)
            kpr = _rope(kp_ref[:, sl], tpv, 1.0)
            v = v_ref[:, sl]
            dov_ = do_ref[:, sl]
            donv = don_ref[:, sl]
            dl = jnp.sum(dov_ * o_ref[:, sl], axis=-1, keepdims=True)
            dln = jnp.sum(donv * on_ref[:, sl], axis=-1, keepdims=True)
            ls = l_ref[:, sl]
            kc = jnp.concatenate([kpr, kr], axis=0)
            vc = jnp.concatenate([vp_ref[:, sl], v], axis=0)
            s = _dot_nt(qr, kc) * scale
            p = jnp.where(ok_a, jnp.exp(jnp.minimum(s - jnp.concatenate([ls, ls], axis=1), 30.0)), 0.0)
            ds = p * (_dot_nt(dov_, vc) - dl) * scale
            out_ref[:, sl] = _rope(_dot_nn(ds, kc), tb, -1.0)
            qc = jnp.concatenate([qr, qnr], axis=0)
            doc = jnp.concatenate([dov_, donv], axis=0)
            lc = jnp.concatenate([ls, ln_ref[:, sl]], axis=0)
            dlc = jnp.concatenate([dl, dln], axis=0)
            s2 = _dot_nt(qc, kr) * scale
            p2 = jnp.where(ok_b, jnp.exp(jnp.minimum(s2 - lc, 30.0)), 0.0)
            ds2 = p2 * (_dot_nt(doc, v) - dlc) * scale
            out_ref[:, slice(W + h * HEAD_DIM, W + (h + 1) * HEAD_DIM)] = _rope(_dot_tn(ds2, qc), tb, -1.0)
            out_ref[:, slice(2 * W + h * HEAD_DIM, 2 * W + (h + 1) * HEAD_DIM)] = _dot_tn(p2, doc)

    def nxt(n):
        return jnp.minimum(n + 1, nb - 1)

    def prv(n):
        return jnp.maximum(n - 1, 0)

    def qspec(j, f):
        return pl.BlockSpec((B, W), lambda r, n: (f(n), r * 9 + g * 3 + j))

    def ospec(f):
        return pl.BlockSpec((B, W), lambda r, n: (f(n), r))

    def tspec(f):
        return pl.BlockSpec((B, 3 * HEAD_DIM), lambda r, n: (f(n), r))

    same = lambda n: n
    in_specs = [qspec(0, same), qspec(0, nxt), qspec(1, same), qspec(1, prv), qspec(2, same), qspec(2, prv),
                ospec(same), ospec(nxt), ospec(same), ospec(nxt), ospec(same), ospec(nxt),
                tspec(same), tspec(prv), tspec(nxt)]
    args = [qv, qv, qv, qv, qv, qv, dov, dov, ov, ov, lv, lv, tv, tv, tv]
    kwargs = {}
    if aliased:
        in_specs.append(pl.BlockSpec(memory_space=pl.ANY))
        args.append(dqkv_prev.reshape(L, dil * 9 * W))
        kwargs["input_output_aliases"] = {len(args) - 1: 0}
    out = pl.pallas_call(
        body, grid=(dil, nb), in_specs=in_specs,
        out_specs=pl.BlockSpec((B, 3 * W), lambda r, n: (n, r * 3 + g)),
        out_shape=jax.ShapeDtypeStruct((L, dil * 9 * W), F32),
        compiler_params=_cp("parallel", "parallel"), name=name, **kwargs)(*args)
    return out.reshape(S, 9 * W)


def _shift_down(x, halo, s):
    if s == 0:
        return x
    T = x.shape[0]
    xs = pltpu.roll(x, s, 0)
    hs = pltpu.roll(halo, s, 0)
    row8 = lax.broadcasted_iota(jnp.int32, hs.shape, 0)
    top = jnp.where(row8 < s, hs, xs[0:8])
    return jnp.concatenate([top, xs[8:T]], axis=0)


def _shift_up(x, halo, s):
    if s == 0:
        return x
    T = x.shape[0]
    xs = pltpu.roll(x, T - s, 0)
    hs = pltpu.roll(halo, 8 - s, 0)
    row8 = lax.broadcasted_iota(jnp.int32, hs.shape, 0)
    bot = jnp.where(row8 >= 8 - s, hs, xs[T - 8:T])
    return jnp.concatenate([xs[0:T - 8], bot], axis=0)


def _conv_apply(x, halo, w_ref, b, K):
    acc = x * w_ref[K - 1] + b
    for s in range(1, K):
        acc = acc + _shift_down(x, halo, s) * w_ref[K - 1 - s]
    return acc


def _halo_prev(T, cb, col):
    return pl.BlockSpec((8, cb), lambda j, i: (jnp.maximum(i * (T // 8) - 1, 0), col(j)))


def _ssm_conv_fwd(zx, w, b, d_inner, conv_dim, name):
    S = zx.shape[0]
    K = w.shape[0]
    T = _pick(S, 256, 8)
    cb = _pick(math.gcd(conv_dim, d_inner), 512)
    off = d_inner // cb

    def body(x_ref, h_ref, w_ref, b_ref, c_ref, a_ref):
        i = pl.program_id(1)
        halo = jnp.where(i > 0, h_ref[...], 0.0)
        y = _conv_apply(x_ref[...], halo, w_ref, b_ref[...], K)
        c_ref[...] = y
        a_ref[...] = y * _sigmoid(y)

    main = pl.BlockSpec((T, cb), lambda j, i: (i, j + off))
    out = pl.BlockSpec((T, cb), lambda j, i: (i, j))
    wspec = pl.BlockSpec((K, 1, cb), lambda j, i: (0, 0, j))
    bspec = pl.BlockSpec((1, cb), lambda j, i: (0, j))
    return pl.pallas_call(
        body, grid=(conv_dim // cb, S // T),
        in_specs=[main, _halo_prev(T, cb, lambda j: j + off), wspec, bspec], out_specs=[out, out],
        out_shape=[jax.ShapeDtypeStruct((S, conv_dim), F32)] * 2,
        compiler_params=_cp("parallel", "parallel"), name=name)(
            zx, zx, w.reshape(K, 1, conv_dim), b.reshape(1, conv_dim))


def _ffn_conv_fwd(up, w, b, name):
    S, C = up.shape
    F = C // 2
    K = w.shape[0]
    T = _pick(S, 256, 8)
    cb = _pick(F, 256)
    nf = F // cb

    def body(xg_ref, xu_ref, hg_ref, hu_ref, wg_ref, wu_ref, bg_ref, bu_ref, u_ref, a_ref):
        i = pl.program_id(1)
        first = i > 0
        gate = _conv_apply(xg_ref[...], jnp.where(first, hg_ref[...], 0.0), wg_ref, bg_ref[...], K)
        upv = _conv_apply(xu_ref[...], jnp.where(first, hu_ref[...], 0.0), wu_ref, bu_ref[...], K)
        u_ref[0] = gate
        u_ref[1] = upv
        a_ref[...] = (gate * _sigmoid(gate) * upv).astype(a_ref.dtype)

    def main(o):
        return pl.BlockSpec((T, cb), lambda j, i: (i, j + o))

    def wspec(o):
        return pl.BlockSpec((K, 1, cb), lambda j, i: (0, 0, j + o))

    def bspec(o):
        return pl.BlockSpec((1, cb), lambda j, i: (0, j + o))

    w3, b2 = w.reshape(K, 1, C), b.reshape(1, C)
    return pl.pallas_call(
        body, grid=(nf, S // T),
        in_specs=[main(0), main(nf), _halo_prev(T, cb, lambda j: j), _halo_prev(T, cb, lambda j: j + nf),
                  wspec(0), wspec(nf), bspec(0), bspec(nf)],
        out_specs=[pl.BlockSpec((2, T, cb), lambda j, i: (0, i, j)), pl.BlockSpec((T, cb), lambda j, i: (i, j))],
        out_shape=[jax.ShapeDtypeStruct((2, S, F), F32), jax.ShapeDtypeStruct((S, F), BF16)],
        compiler_params=_cp("parallel", "parallel"), name=name)(up, up, up, up, w3, w3, b2, b2)


def _ffn_gate_bwd(u2, dact, name):
    _, S, F = u2.shape
    T = _pick(S, 256, 8)
    cb = _pick(F, 256)

    def body(u_ref, d_ref, o_ref):
        gate, upv, d = u_ref[0], u_ref[1], d_ref[...]
        sg = _sigmoid(gate)
        o_ref[0] = d * upv * sg * (1.0 + gate * (1.0 - sg))
        o_ref[1] = d * gate * sg

    u_spec = pl.BlockSpec((2, T, cb), lambda i, j: (0, i, j))
    return pl.pallas_call(
        body, grid=(S // T, F // cb), in_specs=[u_spec, pl.BlockSpec((T, cb), lambda i, j: (i, j))],
        out_specs=u_spec, out_shape=jax.ShapeDtypeStruct((2, S, F), F32),
        compiler_params=_cp("parallel", "parallel"), name=name)(u2, dact)


def _conv_bwd(dy, dy_split, x, x_off, w, name):
    K, C = w.shape
    S = x.shape[0]
    T = _pick(S, 256, 8)
    cb = _pick(math.gcd(C // 2, x_off) if x_off else C // 2, 256)
    nhalf = (C // 2) // cb
    xo = x_off // cb
    nrow = S // T

    def body(dy_ref, dyn_ref, x_ref, xp_ref, w_ref, dx_ref, dw_ref, db_ref):
        i = pl.program_id(1)

        @pl.when(i == 0)
        def _():
            dw_ref[...] = jnp.zeros_like(dw_ref)
            db_ref[...] = jnp.zeros_like(db_ref)

        d = dy_ref[...]
        dn = jnp.where(i < nrow - 1, dyn_ref[...], 0.0)
        xv = x_ref[...]
        xp = jnp.where(i > 0, xp_ref[...], 0.0)
        acc = d * w_ref[K - 1]
        dw_ref[K - 1] += jnp.sum(d * xv, axis=0, keepdims=True)
        for s in range(1, K):
            acc = acc + _shift_up(d, dn, s) * w_ref[K - 1 - s]
            dw_ref[K - 1 - s] += jnp.sum(d * _shift_down(xv, xp, s), axis=0, keepdims=True)
        dx_ref[...] = acc
        db_ref[...] += jnp.sum(d, axis=0, keepdims=True)

    if dy_split:
        dy_main = pl.BlockSpec((None, T, cb), lambda j, i: (j // nhalf, i, j % nhalf))
        dy_next = pl.BlockSpec((None, 8, cb),
                               lambda j, i: (j // nhalf, jnp.minimum((i + 1) * (T // 8), S // 8 - 1), j % nhalf))
    else:
        dy_main = pl.BlockSpec((T, cb), lambda j, i: (i, j))
        dy_next = pl.BlockSpec((8, cb), lambda j, i: (jnp.minimum((i + 1) * (T // 8), S // 8 - 1), j))
    x_main = pl.BlockSpec((T, cb), lambda j, i: (i, j + xo))
    x_prev = _halo_prev(T, cb, lambda j: j + xo)
    wspec = pl.BlockSpec((K, 1, cb), lambda j, i: (0, 0, j))
    dx, dw, db = pl.pallas_call(
        body, grid=(C // cb, nrow),
        in_specs=[dy_main, dy_next, x_main, x_prev, wspec],
        out_specs=[pl.BlockSpec((T, cb), lambda j, i: (i, j)), wspec, pl.BlockSpec((1, cb), lambda j, i: (0, j))],
        out_shape=[jax.ShapeDtypeStruct((S, C), F32), jax.ShapeDtypeStruct((K, 1, C), F32),
                   jax.ShapeDtypeStruct((1, C), F32)],
        compiler_params=_cp("parallel", "arbitrary"), name=name)(dy, dy, x, x, w.reshape(K, 1, C))
    return dx, dw.reshape(K, C), db


def _ssm_act_bwd(dxs, dbm, dcm, conv, name):
    S, C = conv.shape
    d_inner, gn = dxs.shape[1], dbm.shape[1]
    T = _pick(S, 256, 8)
    cb = _pick(math.gcd(d_inner, gn), 512)
    n1, n2 = d_inner // cb, gn // cb

    def body(a_ref, b_ref, c_ref, y_ref, o_ref):
        j = pl.program_id(1)
        d = jnp.where(j < n1, a_ref[...], jnp.where(j < n1 + n2, b_ref[...], c_ref[...]))
        y = y_ref[...]
        sg = _sigmoid(y)
        o_ref[...] = d * sg * (1.0 + y * (1.0 - sg))

    a_spec = pl.BlockSpec((T, cb), lambda i, j: (i, jnp.minimum(j, n1 - 1)))
    b_spec = pl.BlockSpec((T, cb), lambda i, j: (i, jnp.clip(j - n1, 0, n2 - 1)))
    c_spec = pl.BlockSpec((T, cb), lambda i, j: (i, jnp.clip(j - n1 - n2, 0, n2 - 1)))
    full = pl.BlockSpec((T, cb), lambda i, j: (i, j))
    return pl.pallas_call(
        body, grid=(S // T, C // cb), in_specs=[a_spec, b_spec, c_spec, full], out_specs=full,
        out_shape=jax.ShapeDtypeStruct((S, C), F32),
        compiler_params=_cp("parallel", "parallel"), name=name)(dxs, dbm, dcm, conv)


def _cumsum_rows(v):
    n = v.shape[0]
    row = lax.broadcasted_iota(jnp.int32, v.shape, 0)
    k = 1
    while k < n:
        v = v + jnp.where(row >= k, pltpu.roll(v, k, 0), 0.0)
        k *= 2
    return v


def _rev_cumsum_rows(v):
    n = v.shape[0]
    row = lax.broadcasted_iota(jnp.int32, v.shape, 0)
    k = 1
    while k < n:
        v = v + jnp.where(row < n - k, pltpu.roll(v, n - k, 0), 0.0)
        k *= 2
    return v


def _ssd_common(x_ref, dtr_ref, bias_ref, alog_ref, gw):
    Q = SSM_CHUNK
    X = x_ref[...]
    pre = dtr_ref[...] + bias_ref[...]
    dt = _softplus(pre)
    a = -jnp.exp(alog_ref[...])
    cs = _cumsum_rows(dt * a)
    row = lax.broadcasted_iota(jnp.int32, (Q, gw), 0)
    cs_last = jnp.sum(jnp.where(row == Q - 1, cs, 0.0), axis=0, keepdims=True)
    return X, pre, dt, a, cs, cs_last, row


def _head_decay(cs, head_mask):
    Q = SSM_CHUNK
    col = jnp.max(jnp.where(head_mask, cs, NEG), axis=1, keepdims=True)
    acol = jnp.broadcast_to(col, (Q, Q))
    arow = acol.T
    ii = lax.broadcasted_iota(jnp.int32, (Q, Q), 0)
    jj = lax.broadcasted_iota(jnp.int32, (Q, Q), 1)
    tril = ii >= jj
    return jnp.where(tril, jnp.exp(jnp.where(tril, acol - arow, 0.0)), 0.0), tril


def _ssd_specs(S, d_inner, gw, nc, rev):
    Q, N, G = SSM_CHUNK, SSM_STATE, SSM_GROUPS
    ch = (lambda c: nc - 1 - c) if rev else (lambda c: c)
    x_spec = pl.BlockSpec((Q, gw), lambda g, c: (ch(c), g))
    b_spec = pl.BlockSpec((Q, N), lambda g, c: (ch(c), d_inner // N + g))
    c_spec = pl.BlockSpec((Q, N), lambda g, c: (ch(c), d_inner // N + G + g))
    p_spec = pl.BlockSpec((None, 1, gw), lambda g, c: (g, 0, 0))
    s_spec = pl.BlockSpec((None, None, gw, N), lambda g, c: (ch(c), g, 0, 0))
    return x_spec, b_spec, c_spec, p_spec, s_spec


def _ssd_fwd(xbc, dtr, bias, alog, dsk, d_inner, name):
    S = xbc.shape[0]
    Q, N, G, P = SSM_CHUNK, SSM_STATE, SSM_GROUPS, SSM_HEAD_DIM
    gw = d_inner // G
    R = gw // P
    nc = S // Q

    def body(x_ref, b_ref, c_ref, dtr_ref, bias_ref, alog_ref, d_ref, y_ref, sp_ref, s_scr):
        @pl.when(pl.program_id(1) == 0)
        def _():
            s_scr[...] = jnp.zeros_like(s_scr)

        X, _, dt, a, cs, cs_last, row = _ssd_common(x_ref, dtr_ref, bias_ref, alog_ref, gw)
        Bm, Cm = b_ref[...], c_ref[...]
        xdt = X * dt
        lane = lax.broadcasted_iota(jnp.int32, (Q, gw), 1)
        sprev = s_scr[...]
        sp_ref[...] = sprev
        cb = _dot_nt(Cm, Bm)
        y = jnp.exp(cs) * _dot_nt(Cm, sprev)
        for r in range(R):
            hm = (lane >= r * P) & (lane < (r + 1) * P)
            dec_l, _ = _head_decay(cs, hm)
            y = y + _dot_nn(cb * dec_l, jnp.where(hm, xdt, 0.0))
        dec = jnp.exp(cs_last - cs)
        cd = jnp.exp(jnp.broadcast_to(cs_last, (Q, gw)).T)
        s_scr[...] = sprev * cd + _dot_tn(xdt * dec, Bm)
        y_ref[...] = y + d_ref[...] * X

    x_spec, b_spec, c_spec, p_spec, s_spec = _ssd_specs(S, d_inner, gw, nc, False)
    return pl.pallas_call(
        body, grid=(G, nc), in_specs=[x_spec, b_spec, c_spec, x_spec, p_spec, p_spec, p_spec],
        out_specs=[x_spec, s_spec],
        out_shape=[jax.ShapeDtypeStruct((S, d_inner), F32), jax.ShapeDtypeStruct((nc, G, gw, N), F32)],
        scratch_shapes=[pltpu.VMEM((gw, N), F32)],
        compiler_params=_cp("parallel", "arbitrary"), name=name)(xbc, xbc, xbc, dtr, bias, alog, dsk)


def _ssd_bwd(xbc, dtr, bias, alog, dsk, sprev_all, dy, d_inner, name):
    S = xbc.shape[0]
    Q, N, G, P = SSM_CHUNK, SSM_STATE, SSM_GROUPS, SSM_HEAD_DIM
    gw = d_inner // G
    R = gw // P
    nc = S // Q

    def body(x_ref, b_ref, c_ref, dtr_ref, bias_ref, alog_ref, d_ref, sp_ref, dy_ref,
             dx_ref, db_ref, dc_ref, ddt_ref, dbias_ref, dalog_ref, dd_ref, ds_scr):
        @pl.when(pl.program_id(1) == 0)
        def _():
            ds_scr[...] = jnp.zeros_like(ds_scr)
            dbias_ref[...] = jnp.zeros_like(dbias_ref)
            dalog_ref[...] = jnp.zeros_like(dalog_ref)
            dd_ref[...] = jnp.zeros_like(dd_ref)

        X, pre, dt, a, cs, cs_last, row = _ssd_common(x_ref, dtr_ref, bias_ref, alog_ref, gw)
        Bm, Cm = b_ref[...], c_ref[...]
        dY = dy_ref[...]
        sprev = sp_ref[...]
        dsn = ds_scr[...]
        xdt = X * dt
        lane = lax.broadcasted_iota(jnp.int32, (Q, gw), 1)
        lane1 = lax.broadcasted_iota(jnp.int32, (1, gw), 1)
        srow = lax.broadcasted_iota(jnp.int32, (gw, N), 0)
        ecs = jnp.exp(cs)
        dec = jnp.exp(cs_last - cs)
        cd = jnp.exp(jnp.broadcast_to(cs_last, (Q, gw)).T)
        dd_ref[...] += jnp.sum(dY * X, axis=0, keepdims=True)
        dX = d_ref[...] * dY
        ey = ecs * dY
        dcs = ey * _dot_nt(Cm, sprev)
        dC = _dot_nn(ey, sprev)
        ds_scr[...] = cd * dsn + _dot_tn(ey, Cm)
        wmat = _dot_nt(Bm, dsn)
        dxdt = dec * wmat
        xd = xdt * dec
        dB = _dot_nn(xd, dsn)
        ddec = xdt * wmat * dec
        dcs = dcs - ddec
        dlast = jnp.sum(ddec, axis=0, keepdims=True)
        qmat = dsn * sprev * cd
        cb = _dot_nt(Cm, Bm)
        dcb = jnp.zeros((Q, Q), F32)
        dcs_rep = jnp.zeros((Q, gw), F32)
        dtx_rep = jnp.zeros((Q, gw), F32)
        for r in range(R):
            hm = (lane >= r * P) & (lane < (r + 1) * P)
            dec_l, tril = _head_decay(cs, hm)
            dyr = jnp.where(hm, dY, 0.0)
            gmat = jnp.where(tril, _dot_nt(dyr, xdt), 0.0)
            dcb = dcb + gmat * dec_l
            e = gmat * cb * dec_l
            v = (jnp.sum(e, axis=1, keepdims=True) - jnp.sum(e.T, axis=1, keepdims=True)
                 + jnp.sum(jnp.where(hm, dcs, 0.0), axis=1, keepdims=True))
            dxdt = dxdt + _dot_tn(cb * dec_l, dyr)
            hm1 = (lane1 >= r * P) & (lane1 < (r + 1) * P)
            t_last = (jnp.sum(jnp.where(hm1, dlast, 0.0), axis=1, keepdims=True)
                      + jnp.sum(jnp.where((srow >= r * P) & (srow < (r + 1) * P), qmat, 0.0), keepdims=True))
            dcs_rep = dcs_rep + jnp.where(hm, v, 0.0) + jnp.where(hm & (row == Q - 1), t_last, 0.0)
        for r in range(R):
            hm = (lane >= r * P) & (lane < (r + 1) * P)
            w_r = jnp.sum(jnp.where(hm, dxdt * X, 0.0), axis=1, keepdims=True)
            dtx_rep = dtx_rep + jnp.where(hm, w_r, 0.0)
        dadt = _rev_cumsum_rows(dcs_rep)
        ddt = a * dadt + dtx_rep
        dalog_ref[...] += jnp.sum(dt * dadt, axis=0, keepdims=True) * a
        draw = ddt * _sigmoid(pre)
        ddt_ref[...] = draw
        dbias_ref[...] += jnp.sum(draw, axis=0, keepdims=True)
        dx_ref[...] = dX + dxdt * dt
        db_ref[...] = dB + _dot_tn(dcb, Cm)
        dc_ref[...] = dC + _dot_nn(dcb, Bm)

    x_spec, b_spec, c_spec, p_spec, s_spec = _ssd_specs(S, d_inner, gw, nc, True)
    n_spec = pl.BlockSpec((Q, N), lambda g, c: (nc - 1 - c, g))
    gshape = jax.ShapeDtypeStruct((G, 1, gw), F32)
    return pl.pallas_call(
        body, grid=(G, nc),
        in_specs=[x_spec, b_spec, c_spec, x_spec, p_spec, p_spec, p_spec, s_spec, x_spec],
        out_specs=[x_spec, n_spec, n_spec, x_spec, p_spec, p_spec, p_spec],
        out_shape=[jax.ShapeDtypeStruct((S, d_inner), F32), jax.ShapeDtypeStruct((S, G * N), F32),
                   jax.ShapeDtypeStruct((S, G * N), F32), jax.ShapeDtypeStruct((S, d_inner), F32),
                   gshape, gshape, gshape],
        scratch_shapes=[pltpu.VMEM((gw, N), F32)],
        compiler_params=_cp("parallel", "arbitrary"), name=name)(
            xbc, xbc, xbc, dtr, bias, alog, dsk, sprev_all, dy)


def _gnorm_fwd(y, zx, w, name):
    S, d_inner = y.shape
    G = SSM_GROUPS
    gw = d_inner // G
    T = _pick(S, 256, 8)

    def body(y_ref, z_ref, w_ref, o_ref):
        for k in range(G):
            sl = slice(k * gw, (k + 1) * gw)
            z = z_ref[:, sl]
            gk = y_ref[:, sl] * z * _sigmoid(z)
            r = lax.rsqrt(jnp.mean(gk * gk, axis=-1, keepdims=True) + NORM_EPS)
            o_ref[:, sl] = (gk * r * w_ref[:, sl]).astype(o_ref.dtype)

    row = pl.BlockSpec((T, d_inner), lambda i: (i, 0))
    vec = pl.BlockSpec((1, d_inner), lambda i: (0, 0))
    return pl.pallas_call(body, grid=(S // T,), in_specs=[row, row, vec], out_specs=row,
                          out_shape=jax.ShapeDtypeStruct((S, d_inner), BF16),
                          compiler_params=_cp("parallel"), name=name)(y, zx, w.reshape(1, d_inner))


def _gnorm_bwd(y, zx, w, dout, name):
    S, d_inner = y.shape
    G = SSM_GROUPS
    gw = d_inner // G
    T = _pick(S, 256, 8)

    def body(y_ref, z_ref, w_ref, d_ref, dy_ref, dz_ref, dw_ref):
        @pl.when(pl.program_id(0) == 0)
        def _():
            dw_ref[...] = jnp.zeros_like(dw_ref)

        for k in range(G):
            sl = slice(k * gw, (k + 1) * gw)
            z, yv, d = z_ref[:, sl], y_ref[:, sl], d_ref[:, sl]
            sg = _sigmoid(z)
            sz = z * sg
            gk = yv * sz
            r = lax.rsqrt(jnp.mean(gk * gk, axis=-1, keepdims=True) + NORM_EPS)
            gh = gk * r
            dw_ref[:, sl] += jnp.sum(d * gh, axis=0, keepdims=True)
            dg = d * w_ref[:, sl]
            dgk = r * (dg - gh * jnp.mean(dg * gh, axis=-1, keepdims=True))
            dy_ref[:, sl] = dgk * sz
            dz_ref[:, sl] = dgk * yv * sg * (1.0 + z * (1.0 - sg))

    row = pl.BlockSpec((T, d_inner), lambda i: (i, 0))
    vec = pl.BlockSpec((1, d_inner), lambda i: (0, 0))
    return pl.pallas_call(
        body, grid=(S // T,), in_specs=[row, row, vec, row], out_specs=[row, row, vec],
        out_shape=[jax.ShapeDtypeStruct((S, d_inner), F32)] * 2 + [jax.ShapeDtypeStruct((1, d_inner), F32)],
        compiler_params=_cp("arbitrary"), name=name)(y, zx, w.reshape(1, d_inner), dout)


def _adam_math(g, w, m, v):
    m = ADAM_B1 * m + (1.0 - ADAM_B1) * g
    v = ADAM_B2 * v + (1.0 - ADAM_B2) * (g * g)
    m_hat = m / (1.0 - ADAM_B1 ** ADAM_STEP)
    v_hat = v / (1.0 - ADAM_B2 ** ADAM_STEP)
    delta = -ADAM_LR * (m_hat / (jnp.sqrt(v_hat) + ADAM_EPS) + ADAM_WD * w)
    return delta, m, v


def _adamw_big(own, sib, w, m, v, name):
    L, A, Bc = w.shape
    T = _pick(A, max(8, (1 << 19) // (4 * Bc)), 16)

    def body(o_ref, s_ref, w_ref, m_ref, v_ref, g_ref, d_ref, nm_ref, nv_ref):
        so = o_ref[0].astype(F32)
        ss = s_ref[0].astype(F32)
        for k in range(1, N_CHIPS):
            so = so + o_ref[k].astype(F32)
            ss = ss + s_ref[k].astype(F32)
        g = so + ss
        delta, nm, nv = _adam_math(g, w_ref[...], m_ref[...], v_ref[...])
        g_ref[...] = g
        d_ref[...] = delta
        nm_ref[...] = nm
        nv_ref[...] = nv

    part = pl.BlockSpec((N_CHIPS, None, T, Bc), lambda l, i: (0, l, i, 0))
    blk = pl.BlockSpec((None, T, Bc), lambda l, i: (l, i, 0))
    shp = jax.ShapeDtypeStruct(w.shape, F32)
    return pl.pallas_call(body, grid=(L, A // T), in_specs=[part, part, blk, blk, blk],
                          out_specs=[blk] * 4, out_shape=[shp] * 4,
                          compiler_params=_cp("parallel", "parallel"), name=name)(own, sib, w, m, v)


def _sum_devices(parts, name):
    _, R, C = parts.shape

    def body(p_ref, o_ref):
        acc = p_ref[0]
        for k in range(1, N_DEV):
            acc = acc + p_ref[k]
        o_ref[...] = acc

    return pl.pallas_call(body, out_shape=jax.ShapeDtypeStruct((R, C), F32), name=name)(parts)


def _adamw_small(g, w, m, v, name):
    def body(g_ref, w_ref, m_ref, v_ref, d_ref, nm_ref, nv_ref):
        delta, nm, nv = _adam_math(g_ref[...], w_ref[...], m_ref[...], v_ref[...])
        d_ref[...] = delta
        nm_ref[...] = nm
        nv_ref[...] = nv

    shp = jax.ShapeDtypeStruct(g.shape, F32)
    return pl.pallas_call(body, out_shape=[shp] * 3, name=name)(g, w, m, v)


PACK_COLS = 1024


def _pack(arrs):
    flat = jnp.concatenate([a.reshape(-1).astype(F32) for a in arrs])
    n = flat.shape[0]
    rows = -(-n // (8 * PACK_COLS)) * 8
    return jnp.pad(flat, (0, rows * PACK_COLS - n)).reshape(rows, PACK_COLS)


def _unpack(packed, shapes):
    flat = packed.reshape(-1)
    out, off = [], 0
    for s in shapes:
        n = math.prod(s)
        out.append(flat[off:off + n].reshape(s))
        off += n
    return out


def _shard_ref(ref, kind, k, n):
    if kind == "col":
        return ref.at[:, pl.ds(pl.multiple_of(k * n, 128), n)]
    if kind == "row":
        return ref.at[pl.ds(pl.multiple_of(k * n, 16), n), :]
    return ref.at[k]


def _chip_peers():
    x, y, c = lax.axis_index("x"), lax.axis_index("y"), lax.axis_index("c")
    return x, y, c, [(1 - x, y), (x, 1 - y), (1 - x, 1 - y)]


def _all_gather_chips(items, name):
    arrays = []
    for it in items:
        if not any(it[0] is a for a in arrays):
            arrays.append(it[0])
    src_idx = [next(i for i, a in enumerate(arrays) if a is it[0]) for it in items]
    nin, nit = len(arrays), len(items)

    def body(*refs):
        ins, outs = refs[:nin], refs[nin:nin + nit]
        send_sems, recv_sems, loc_sems = refs[nin + nit:]
        x, y, c, peers = _chip_peers()
        me = 2 * x + y
        started = []
        for t, (_, layer, kind, n, _) in enumerate(items):
            src = ins[src_idx[t]] if layer is None else ins[src_idx[t]].at[layer]
            mine = _shard_ref(outs[t], kind, me, n)
            lc = pltpu.make_async_copy(src, mine, loc_sems.at[t])
            lc.start()
            started.append(lc)
            for j, (px, py) in enumerate(peers):
                rc = pltpu.make_async_remote_copy(
                    src_ref=src, dst_ref=mine, send_sem=send_sems.at[3 * t + j],
                    recv_sem=recv_sems.at[3 * t + j], device_id=(px, py, c), device_id_type=MESH)
                rc.start()
                started.append(rc)
        for t, (_, layer, kind, n, _) in enumerate(items):
            src = ins[src_idx[t]] if layer is None else ins[src_idx[t]].at[layer]
            for j, (px, py) in enumerate(peers):
                theirs = _shard_ref(outs[t], kind, 2 * px + py, n)
                pltpu.make_async_remote_copy(
                    src_ref=src, dst_ref=theirs, send_sem=send_sems.at[3 * t + j],
                    recv_sem=recv_sems.at[3 * t + j], device_id=(px, py, c), device_id_type=MESH).wait_recv()
        for cp in started:
            if cp.is_remote:
                cp.wait_send()
            else:
                cp.wait()

    anyspec = pl.BlockSpec(memory_space=pl.ANY)
    return pl.pallas_call(
        body, in_specs=[anyspec] * nin, out_specs=[anyspec] * nit,
        out_shape=[jax.ShapeDtypeStruct(it[4], it[0].dtype) for it in items],
        scratch_shapes=[pltpu.SemaphoreType.DMA((3 * nit,)), pltpu.SemaphoreType.DMA((3 * nit,)),
                        pltpu.SemaphoreType.DMA((nit,))],
        name=name)(*arrays)


def _reduce_scatter_chips(items, name):
    grads = [it[0] for it in items]
    nit = len(items)
    bufs = []
    for it in items:
        if it[3] not in [b[0] for b in bufs]:
            bufs.append((it[3], it[5], it[0].dtype))
    nbuf = len(bufs)

    def body(*refs):
        ins, outs = refs[:nit], refs[nit:nit + nbuf]
        send_sems, recv_sems, loc_sems = refs[nit + nbuf:]
        x, y, c, peers = _chip_peers()
        me = 2 * x + y
        started = []
        for t, (_, kind, n, bi, layer, _) in enumerate(items):
            slot = lambda k: outs[bi].at[k, layer]
            lc = pltpu.make_async_copy(_shard_ref(ins[t], kind, me, n), slot(me), loc_sems.at[t])
            lc.start()
            started.append(lc)
            for j, (px, py) in enumerate(peers):
                rc = pltpu.make_async_remote_copy(
                    src_ref=_shard_ref(ins[t], kind, 2 * px + py, n), dst_ref=slot(me),
                    send_sem=send_sems.at[3 * t + j], recv_sem=recv_sems.at[3 * t + j],
                    device_id=(px, py, c), device_id_type=MESH)
                rc.start()
                started.append(rc)
        for t, (_, kind, n, bi, layer, _) in enumerate(items):
            for j, (px, py) in enumerate(peers):
                pk = 2 * px + py
                pltpu.make_async_remote_copy(
                    src_ref=_shard_ref(ins[t], kind, pk, n), dst_ref=outs[bi].at[pk, layer],
                    send_sem=send_sems.at[3 * t + j], recv_sem=recv_sems.at[3 * t + j],
                    device_id=(px, py, c), device_id_type=MESH).wait_recv()
        for cp in started:
            if cp.is_remote:
                cp.wait_send()
            else:
                cp.wait()

    anyspec = pl.BlockSpec(memory_space=pl.ANY)
    return pl.pallas_call(
        body, in_specs=[anyspec] * nit, out_specs=[anyspec] * nbuf,
        out_shape=[jax.ShapeDtypeStruct(shape, dt) for _, shape, dt in bufs],
        scratch_shapes=[pltpu.SemaphoreType.DMA((3 * nit,)), pltpu.SemaphoreType.DMA((3 * nit,)),
                        pltpu.SemaphoreType.DMA((nit,))],
        name=name)(*grads)


def _swap_with_sibling(arrs, name):
    n = len(arrs)

    def body(*refs):
        ins, outs = refs[:n], refs[n:2 * n]
        send_sems, recv_sems = refs[2 * n:]
        x, y, c = lax.axis_index("x"), lax.axis_index("y"), lax.axis_index("c")
        copies = [pltpu.make_async_remote_copy(
            src_ref=ins[t], dst_ref=outs[t], send_sem=send_sems.at[t], recv_sem=recv_sems.at[t],
            device_id=(x, y, 1 - c), device_id_type=MESH) for t in range(n)]
        for cp in copies:
            cp.start()
        for cp in copies:
            cp.wait_recv()
        for cp in copies:
            cp.wait_send()

    anyspec = pl.BlockSpec(memory_space=pl.ANY)
    return pl.pallas_call(
        body, in_specs=[anyspec] * n, out_specs=[anyspec] * n,
        out_shape=[jax.ShapeDtypeStruct(a.shape, a.dtype) for a in arrs],
        scratch_shapes=[pltpu.SemaphoreType.DMA((n,)), pltpu.SemaphoreType.DMA((n,))],
        name=name)(*arrs)


def _all_gather_devices(v, name):
    def body(v_ref, o_ref, send_sems, recv_sems, loc_sem):
        x, y, c = lax.axis_index("x"), lax.axis_index("y"), lax.axis_index("c")
        me = 4 * x + 2 * y + c
        lc = pltpu.make_async_copy(v_ref, o_ref.at[me], loc_sem)
        lc.start()
        rel = [(bx, by, bc) for bx in (0, 1) for by in (0, 1) for bc in (0, 1)][1:]
        copies = []
        for j, (bx, by, bc) in enumerate(rel):
            px, py, pc = x ^ bx, y ^ by, c ^ bc
            copies.append((pltpu.make_async_remote_copy(
                src_ref=v_ref, dst_ref=o_ref.at[me], send_sem=send_sems.at[j], recv_sem=recv_sems.at[j],
                device_id=(px, py, pc), device_id_type=MESH), 4 * px + 2 * py + pc))
        for cp, _ in copies:
            cp.start()
        for j, (cp, pid) in enumerate(copies):
            pltpu.make_async_remote_copy(
                src_ref=v_ref, dst_ref=o_ref.at[pid], send_sem=send_sems.at[j], recv_sem=recv_sems.at[j],
                device_id=(x, y, c), device_id_type=MESH).wait_recv()
        for cp, _ in copies:
            cp.wait_send()
        lc.wait()

    anyspec = pl.BlockSpec(memory_space=pl.ANY)
    return pl.pallas_call(
        body, in_specs=[anyspec], out_specs=anyspec,
        out_shape=jax.ShapeDtypeStruct((N_DEV,) + v.shape, v.dtype),
        scratch_shapes=[pltpu.SemaphoreType.DMA((N_DEV - 1,)), pltpu.SemaphoreType.DMA((N_DEV - 1,)),
                        pltpu.SemaphoreType.DMA(())],
        name=name)(v)


BIG = ("attn_w_qkv", "attn_w_o", "ssm_w_in", "ssm_w_out", "ffn_w_up", "ffn_w_down")
BIG_KIND = {"attn_w_qkv": "col", "attn_w_o": "row", "ssm_w_in": "lead", "ssm_w_out": "row",
            "ffn_w_up": "col", "ffn_w_down": "row"}
SMALL_SHARDED = {"ssm_conv_w": 2, "ssm_conv_b": 1, "ssm_norm_w": 1, "ffn_conv_w": 2}
SMALL = ("mix_norm_w", "ssm_conv_w", "ssm_conv_b", "ssm_dt_bias", "ssm_a_log", "ssm_d", "ssm_norm_w",
         "ffn_norm_w", "ffn_conv_w", "ffn_conv_b", "final_norm_w")
WEIGHTS = ("mix_norm_w", "attn_w_qkv", "attn_w_o", "ssm_w_in", "ssm_conv_w", "ssm_conv_b", "ssm_dt_bias",
           "ssm_a_log", "ssm_d", "ssm_norm_w", "ssm_w_out", "ffn_norm_w", "ffn_w_up", "ffn_conv_w",
           "ffn_conv_b", "ffn_w_down", "final_norm_w")


def _gather_weights(W):
    items, keys = [], []
    for name in BIG:
        w16 = W[name].astype(BF16)
        L, a, b = w16.shape
        kind = BIG_KIND[name]
        full = {"col": (a, N_CHIPS * b), "row": (N_CHIPS * a, b), "lead": (N_CHIPS, a, b)}[kind]
        n = {"col": b, "row": a, "lead": 1}[kind]
        for l in range(L):
            items.append((w16, l, kind, n, full))
            keys.append((name, l))
    sm_names = list(SMALL_SHARDED)
    packed = _pack([W[n] for n in sm_names])
    items.append((packed, None, "lead", 1, (N_CHIPS,) + packed.shape))
    outs = _all_gather_chips(items, "gather_weights")
    full = {}
    for (name, l), o in zip(keys, outs[:-1]):
        full.setdefault(name, []).append(o)
    per_chip = [_unpack(outs[-1][k], [W[n].shape for n in sm_names]) for k in range(N_CHIPS)]
    for i, n in enumerate(sm_names):
        full[n] = jnp.concatenate([per_chip[k][i] for k in range(N_CHIPS)], axis=SMALL_SHARDED[n])
    return full


def kernel(x, mix_norm_w, attn_w_qkv, attn_w_o, ssm_w_in, ssm_conv_w, ssm_conv_b, ssm_dt_bias, ssm_a_log, ssm_d, ssm_norm_w, ssm_w_out, ffn_norm_w, ffn_w_up, ffn_conv_w, ffn_conv_b, ffn_w_down, final_norm_w, loss_target, m_mix_norm_w, m_attn_w_qkv, m_attn_w_o, m_ssm_w_in, m_ssm_conv_w, m_ssm_conv_b, m_ssm_dt_bias, m_ssm_a_log, m_ssm_d, m_ssm_norm_w, m_ssm_w_out, m_ffn_norm_w, m_ffn_w_up, m_ffn_conv_w, m_ffn_conv_b, m_ffn_w_down, m_final_norm_w, v_mix_norm_w, v_attn_w_qkv, v_attn_w_o, v_ssm_w_in, v_ssm_conv_w, v_ssm_conv_b, v_ssm_dt_bias, v_ssm_a_log, v_ssm_d, v_ssm_norm_w, v_ssm_w_out, v_ffn_norm_w, v_ffn_w_up, v_ffn_conv_w, v_ffn_conv_b, v_ffn_w_down, v_final_norm_w):
    W = dict(mix_norm_w=mix_norm_w, attn_w_qkv=attn_w_qkv, attn_w_o=attn_w_o, ssm_w_in=ssm_w_in,
             ssm_conv_w=ssm_conv_w, ssm_conv_b=ssm_conv_b, ssm_dt_bias=ssm_dt_bias, ssm_a_log=ssm_a_log,
             ssm_d=ssm_d, ssm_norm_w=ssm_norm_w, ssm_w_out=ssm_w_out, ffn_norm_w=ffn_norm_w, ffn_w_up=ffn_w_up,
             ffn_conv_w=ffn_conv_w, ffn_conv_b=ffn_conv_b, ffn_w_down=ffn_w_down, final_norm_w=final_norm_w)
    M = dict(mix_norm_w=m_mix_norm_w, attn_w_qkv=m_attn_w_qkv, attn_w_o=m_attn_w_o, ssm_w_in=m_ssm_w_in,
             ssm_conv_w=m_ssm_conv_w, ssm_conv_b=m_ssm_conv_b, ssm_dt_bias=m_ssm_dt_bias, ssm_a_log=m_ssm_a_log,
             ssm_d=m_ssm_d, ssm_norm_w=m_ssm_norm_w, ssm_w_out=m_ssm_w_out, ffn_norm_w=m_ffn_norm_w,
             ffn_w_up=m_ffn_w_up, ffn_conv_w=m_ffn_conv_w, ffn_conv_b=m_ffn_conv_b, ffn_w_down=m_ffn_w_down,
             final_norm_w=m_final_norm_w)
    V = dict(mix_norm_w=v_mix_norm_w, attn_w_qkv=v_attn_w_qkv, attn_w_o=v_attn_w_o, ssm_w_in=v_ssm_w_in,
             ssm_conv_w=v_ssm_conv_w, ssm_conv_b=v_ssm_conv_b, ssm_dt_bias=v_ssm_dt_bias, ssm_a_log=v_ssm_a_log,
             ssm_d=v_ssm_d, ssm_norm_w=v_ssm_norm_w, ssm_w_out=v_ssm_w_out, ffn_norm_w=v_ffn_norm_w,
             ffn_w_up=v_ffn_w_up, ffn_conv_w=v_ffn_conv_w, ffn_conv_b=v_ffn_conv_b, ffn_w_down=v_ffn_w_down,
             final_norm_w=v_final_norm_w)

    S, D = x.shape[1], x.shape[2]
    xs = x.reshape(S, D)
    tgt = loss_target.reshape(S, D)
    depth = mix_norm_w.shape[0]
    heads = attn_w_o.shape[1] * N_CHIPS // HEAD_DIM
    AW = heads * HEAD_DIM
    d_inner = ssm_w_out.shape[1] * N_CHIPS
    ssm_heads = d_inner // SSM_HEAD_DIM
    G, P, N = SSM_GROUPS, SSM_HEAD_DIM, SSM_STATE
    gw = d_inner // G
    conv_dim = d_inner + 2 * G * N
    in_w = d_inner + conv_dim + ssm_heads
    in_pad = -(-in_w // 128) * 128
    shard_in = ssm_w_in.shape[2]
    xi, yi = lax.axis_index("x"), lax.axis_index("y")
    chip = 2 * xi + yi

    full = _gather_weights(W)
    w_in_full = [jnp.pad(jnp.concatenate([g[k] for k in range(N_CHIPS)], axis=1), ((0, 0), (0, in_pad - in_w)))
                 for g in full["ssm_w_in"]]
    tab = _rope_table(S)

    def rep_heads(p):
        return jnp.repeat(p, P).reshape(G, 1, gw)

    saved = []
    cur = xs
    for i in range(depth):
        j = i // 2
        sv = {"x_in": cur}
        h = _rms_fwd(cur, mix_norm_w[i], f"mix_norm_fwd_{i}")
        sv["h"] = h
        if i % 2 == 0:
            qkv = _matmul(h, full["attn_w_qkv"][j], "nn", F32, f"qkv_fwd_{i}")
            og = [_attn_fwd(qkv, tab, g, heads, f"attn_fwd_{i}_{g}") for g in range(3)]
            o, lse = _attn_combine([a for a, _ in og], [b for _, b in og], f"attn_combine_{i}")
            cur = _matmul(o, full["attn_w_o"][j], "nn", F32, f"attn_out_fwd_{i}", resid=cur)
            sv.update(qkv=qkv, o=o, lse=lse)
        else:
            zx = _matmul(h, w_in_full[j], "nn", F32, f"ssm_in_fwd_{i}")
            conv, xbc = _ssm_conv_fwd(zx, full["ssm_conv_w"][j], full["ssm_conv_b"][j], d_inner, conv_dim,
                                      f"ssm_conv_fwd_{i}")
            dtr = jnp.repeat(zx[:, d_inner + conv_dim:in_w], P, axis=1)
            prm = [rep_heads(p[j]) for p in (ssm_dt_bias, ssm_a_log, ssm_d)]
            y, sprev = _ssd_fwd(xbc, dtr, *prm, d_inner, f"ssd_fwd_{i}")
            gated = _gnorm_fwd(y, zx, full["ssm_norm_w"][j], f"ssm_norm_fwd_{i}")
            cur = _matmul(gated, full["ssm_w_out"][j], "nn", F32, f"ssm_out_fwd_{i}", resid=cur)
            sv.update(zx=zx, conv=conv, xbc=xbc, dtr=dtr, prm=prm, y=y, sprev=sprev, gated=gated)
        sv["x_mid"] = cur
        h2 = _rms_fwd(cur, ffn_norm_w[i], f"ffn_norm_fwd_{i}")
        up = _matmul(h2, full["ffn_w_up"][i], "nn", F32, f"ffn_up_fwd_{i}")
        u2, act = _ffn_conv_fwd(up, full["ffn_conv_w"][i], ffn_conv_b[i], f"ffn_conv_fwd_{i}")
        cur = _matmul(act, full["ffn_w_down"][i], "nn", F32, f"ffn_down_fwd_{i}", resid=cur)
        sv.update(h2=h2, up=up, u2=u2, act=act)
        saved.append(sv)

    dx, d_final, loss_part = _loss_head(cur, final_norm_w, tgt, "loss_head")
    gbig = {n: [None] * W[n].shape[0] for n in BIG}
    gs = {n: [None] * W[n].shape[0] for n in SMALL if n != "final_norm_w"}
    for i in reversed(range(depth)):
        j = i // 2
        sv = saved[i]
        dact = _matmul(dx, full["ffn_w_down"][i], "nt", F32, f"ffn_down_dgrad_{i}")
        gbig["ffn_w_down"][i] = _matmul(sv["act"], dx, "tn", BF16, f"ffn_down_wgrad_{i}")
        du2 = _ffn_gate_bwd(sv["u2"], dact, f"ffn_gate_bwd_{i}")
        dup, dcw, dcb = _conv_bwd(du2, True, sv["up"], 0, full["ffn_conv_w"][i], f"ffn_conv_bwd_{i}")
        gs["ffn_conv_w"][i], gs["ffn_conv_b"][i] = dcw, dcb[0]
        dh2 = _matmul(dup, full["ffn_w_up"][i], "nt", F32, f"ffn_up_dgrad_{i}")
        gbig["ffn_w_up"][i] = _matmul(sv["h2"], dup, "tn", BF16, f"ffn_up_wgrad_{i}")
        dx, dnw = _rms_bwd(sv["x_mid"], ffn_norm_w[i], dh2, dx, f"ffn_norm_bwd_{i}")
        gs["ffn_norm_w"][i] = dnw[0]
        if i % 2 == 0:
            do = _matmul(dx, full["attn_w_o"][j], "nt", F32, f"attn_out_dgrad_{i}")
            gbig["attn_w_o"][j] = _matmul(sv["o"], dx, "tn", BF16, f"attn_out_wgrad_{i}")
            dqkv = None
            for g in range(3):
                dqkv = _attn_bwd(sv["qkv"], tab, sv["o"], sv["lse"], do, dqkv, g, heads, f"attn_bwd_{i}_{g}")
            dh = _matmul(dqkv, full["attn_w_qkv"][j], "nt", F32, f"qkv_dgrad_{i}")
            gbig["attn_w_qkv"][j] = _matmul(sv["h"], dqkv, "tn", BF16, f"qkv_wgrad_{i}")
        else:
            dgated = _matmul(dx, full["ssm_w_out"][j], "nt", F32, f"ssm_out_dgrad_{i}")
            gbig["ssm_w_out"][j] = _matmul(sv["gated"], dx, "tn", BF16, f"ssm_out_wgrad_{i}")
            dy, dz, dgw = _gnorm_bwd(sv["y"], sv["zx"], full["ssm_norm_w"][j], dgated, f"ssm_norm_bwd_{i}")
            gs["ssm_norm_w"][j] = dgw[0]
            dxs_, dbm, dcm, ddtr, dbias, dalog, ddsk = _ssd_bwd(
                sv["xbc"], sv["dtr"], *sv["prm"], sv["sprev"], dy, d_inner, f"ssd_bwd_{i}")
            gs["ssm_dt_bias"][j] = dbias.reshape(-1)[::P]
            gs["ssm_a_log"][j] = dalog.reshape(-1)[::P]
            gs["ssm_d"][j] = ddsk.reshape(ssm_heads, P).sum(axis=1)
            dconv = _ssm_act_bwd(dxs_, dbm, dcm, sv["conv"], f"ssm_act_bwd_{i}")
            dxbc, dcw, dcb = _conv_bwd(dconv, False, sv["zx"], d_inner, full["ssm_conv_w"][j], f"ssm_conv_bwd_{i}")
            gs["ssm_conv_w"][j], gs["ssm_conv_b"][j] = dcw, dcb[0]
            dzx = jnp.concatenate([dz, dxbc, ddtr[:, ::P], jnp.zeros((S, in_pad - in_w), F32)], axis=1)
            dh = _matmul(dzx, w_in_full[j], "nt", F32, f"ssm_in_dgrad_{i}")
            dwin = _matmul(sv["h"], dzx, "tn", BF16, f"ssm_in_wgrad_{i}")
            gbig["ssm_w_in"][j] = jnp.stack([dwin[:, k * shard_in:(k + 1) * shard_in] for k in range(N_CHIPS)])
        dx, dnw = _rms_bwd(sv["x_in"], mix_norm_w[i], dh, dx, f"mix_norm_bwd_{i}")
        gs["mix_norm_w"][i] = dnw[0]
    grad_x = dx.reshape(x.shape)

    rs_items = []
    for bi, name in enumerate(BIG):
        L, a, b = W[name].shape
        kind = BIG_KIND[name]
        n = {"col": b, "row": a, "lead": 1}[kind]
        for l in range(L):
            rs_items.append((gbig[name][l], kind, n, bi, l, (N_CHIPS, L, a, b)))
    own = _reduce_scatter_chips(rs_items, "scatter_grads")
    sib = _swap_with_sibling(own, "swap_grads")

    small_full = [jnp.stack(gs[n]) if n != "final_norm_w" else d_final[0] for n in SMALL]
    small_full.append(loss_part[0, 0:1])
    small_shapes = [a.shape for a in small_full]
    summed = _sum_devices(_all_gather_devices(_pack(small_full), "gather_small_grads"), "sum_small_grads")
    small_g = _unpack(summed, small_shapes)
    loss = small_g[-1][0]
    gsm = {}
    for n, g in zip(SMALL, small_g[:-1]):
        if n in SMALL_SHARDED:
            ax = SMALL_SHARDED[n]
            ext = W[n].shape[ax]
            g = lax.dynamic_slice_in_dim(g, chip * ext, ext, axis=ax)
        gsm[n] = g

    out_g, out_d, out_m, out_v = {}, {}, {}, {}
    for bi, name in enumerate(BIG):
        out_g[name], out_d[name], out_m[name], out_v[name] = _adamw_big(
            own[bi], sib[bi], W[name], M[name], V[name], f"adamw_{name}")
    shapes = [W[n].shape for n in SMALL]
    pd, pm, pv = _adamw_small(_pack([gsm[n] for n in SMALL]), _pack([W[n] for n in SMALL]),
                              _pack([M[n] for n in SMALL]), _pack([V[n] for n in SMALL]), "adamw_small")
    for n, d_, m_, v_ in zip(SMALL, _unpack(pd, shapes), _unpack(pm, shapes), _unpack(pv, shapes)):
        out_g[n], out_d[n], out_m[n], out_v[n] = gsm[n], d_, m_, v_

    return (loss, grad_x, *[out_g[n] for n in WEIGHTS], *[out_d[n] for n in WEIGHTS],
            *[out_m[n] for n in WEIGHTS], *[out_v[n] for n in WEIGHTS])
```

```python
import functools
import math

import jax
import jax.numpy as jnp
from jax import lax
from jax.experimental import pallas as pl
from jax.experimental.pallas import tpu as pltpu

F32 = jnp.float32
BF16 = jnp.bfloat16
MESH = pl.DeviceIdType.MESH

NORM_EPS = 1e-5
HEAD_DIM = 128
ATTN_BLOCK = 128
ATTN_DILATIONS = (1, 4, 16)
ATTN_WINDOWS = (128, 512, 2048)
PERM = 16
ROPE_THETA = 500000.0
ROPE_HALF = HEAD_DIM // 8
SSM_HEAD_DIM = 64
SSM_STATE = 128
SSM_GROUPS = 8
SSM_CHUNK = 128
NEG = -1e30

ADAM_LR = 0.001
ADAM_B1 = 0.9
ADAM_B2 = 0.999
ADAM_EPS = 1e-08
ADAM_WD = 0.01
ADAM_STEP = 10

VMEM_LIMIT_BYTES = 48 * 1024 * 1024
N_CHIPS = 4
N_DEV = 8


def _cp(*sem):
    return pltpu.CompilerParams(dimension_semantics=sem, vmem_limit_bytes=VMEM_LIMIT_BYTES)


def _pick(n, pref, mult=128):
    best = None
    t = mult
    while t <= min(n, pref):
        if n % t == 0:
            best = t
        t += mult
    return n if best is None else best


def _sigmoid(x):
    return 1.0 / (1.0 + jnp.exp(-x))


def _silu(x):
    return x * _sigmoid(x)


def _softplus(x):
    u = jnp.exp(-jnp.abs(x))
    w = 1.0 + u
    log1p = jnp.where(w == 1.0, u, jnp.log(w) * (u / jnp.where(w == 1.0, 1.0, w - 1.0)))
    return jnp.maximum(x, 0.0) + log1p


def _dot(a, b, dims):
    return lax.dot_general(a.astype(BF16), b.astype(BF16), (dims, ((), ())),
                           preferred_element_type=F32)


def _dot_nn(a, b):
    return _dot(a, b, ((1,), (0,)))


def _dot_nt(a, b):
    return _dot(a, b, ((1,), (1,)))


def _dot_tn(a, b):
    return _dot(a, b, ((0,), (0,)))


def _matmul(a, b, mode, out_dtype, name, resid=None, tm=512, tn=1408, tk=1024):
    if mode == "nn":
        (M, K), (K2, N) = a.shape, b.shape
    elif mode == "nt":
        (M, K), (N, K2) = a.shape, b.shape
    else:
        (K, M), (K2, N) = a.shape, b.shape
    assert K == K2, (a.shape, b.shape, mode)
    tm, tn, tk = _pick(M, tm), _pick(N, tn), _pick(K, tk)
    nk = K // tk
    dims = {"nn": ((1,), (0,)), "nt": ((1,), (1,)), "tn": ((0,), (0,))}[mode]
    has_resid = resid is not None

    def body_single(a_ref, b_ref, *rest):
        r = _dot(a_ref[...], b_ref[...], dims)
        if has_resid:
            r = r + rest[0][...]
        rest[-1][...] = r.astype(rest[-1].dtype)

    def body(a_ref, b_ref, *rest):
        if has_resid:
            r_ref, o_ref, acc_ref = rest
        else:
            o_ref, acc_ref = rest
        k = pl.program_id(2)

        @pl.when(k == 0)
        def _():
            acc_ref[...] = jnp.zeros_like(acc_ref)

        acc_ref[...] += _dot(a_ref[...], b_ref[...], dims)

        @pl.when(k == nk - 1)
        def _():
            r = acc_ref[...]
            if has_resid:
                r = r + r_ref[...]
            o_ref[...] = r.astype(o_ref.dtype)

    if mode == "nn":
        a_spec = pl.BlockSpec((tm, tk), lambda i, j, k: (i, k))
        b_spec = pl.BlockSpec((tk, tn), lambda i, j, k: (k, j))
    elif mode == "nt":
        a_spec = pl.BlockSpec((tm, tk), lambda i, j, k: (i, k))
        b_spec = pl.BlockSpec((tn, tk), lambda i, j, k: (j, k))
    else:
        a_spec = pl.BlockSpec((tk, tm), lambda i, j, k: (k, i))
        b_spec = pl.BlockSpec((tk, tn), lambda i, j, k: (k, j))
    o_spec = pl.BlockSpec((tm, tn), lambda i, j, k: (i, j))
    in_specs = [a_spec, b_spec] + ([o_spec] if has_resid else [])
    args = (a, b) + ((resid,) if has_resid else ())
    return pl.pallas_call(
        body_single if nk == 1 else body, grid=(M // tm, N // tn, nk), in_specs=in_specs, out_specs=o_spec,
        out_shape=jax.ShapeDtypeStruct((M, N), out_dtype),
        scratch_shapes=[] if nk == 1 else [pltpu.VMEM((tm, tn), F32)],
        compiler_params=_cp("parallel", "parallel", "arbitrary"), name=name)(*args)


def _rms_fwd(x, w, name, add=None):
    S, D = x.shape
    t = _pick(S, 512, 8)
    has_add = add is not None

    def body(x_ref, w_ref, *rest):
        xv = x_ref[...]
        if has_add:
            xv = xv + rest[0][...]
            rest[1][...] = xv
        r = lax.rsqrt(jnp.mean(xv * xv, axis=-1, keepdims=True) + NORM_EPS)
        rest[-1][...] = (xv * r * w_ref[...]).astype(rest[-1].dtype)

    row = pl.BlockSpec((t, D), lambda i: (i, 0))
    vec = pl.BlockSpec((1, D), lambda i: (0, 0))
    normed = jax.ShapeDtypeStruct((S, D), BF16)
    if not has_add:
        return pl.pallas_call(body, grid=(S // t,), in_specs=[row, vec], out_specs=row, out_shape=normed,
                              compiler_params=_cp("parallel"), name=name)(x, w.reshape(1, D))
    return pl.pallas_call(body, grid=(S // t,), in_specs=[row, vec, row], out_specs=[row, row],
                          out_shape=[jax.ShapeDtypeStruct((S, D), F32), normed],
                          compiler_params=_cp("parallel"), name=name)(x, w.reshape(1, D), add)


def _rms_bwd(x, w, dh, dres, name):
    S, D = x.shape
    t = _pick(S, 512, 8)

    def body(x_ref, w_ref, dh_ref, dr_ref, dx_ref, dw_ref):
        @pl.when(pl.program_id(0) == 0)
        def _():
            dw_ref[...] = jnp.zeros_like(dw_ref)

        xv = x_ref[...]
        r = lax.rsqrt(jnp.mean(xv * xv, axis=-1, keepdims=True) + NORM_EPS)
        xh = xv * r
        dh_v = dh_ref[...]
        g = dh_v * w_ref[...]
        dx_ref[...] = dr_ref[...] + r * (g - xh * jnp.mean(g * xh, axis=-1, keepdims=True))
        dw_ref[...] += jnp.sum(dh_v * xh, axis=0, keepdims=True)

    row = pl.BlockSpec((t, D), lambda i: (i, 0))
    vec = pl.BlockSpec((1, D), lambda i: (0, 0))
    return pl.pallas_call(
        body, grid=(S // t,), in_specs=[row, vec, row, row], out_specs=[row, vec],
        out_shape=[jax.ShapeDtypeStruct((S, D), F32), jax.ShapeDtypeStruct((1, D), F32)],
        compiler_params=_cp("arbitrary"), name=name)(x, w.reshape(1, D), dh, dres)


def _loss_head(x, w, tgt, name):
    S, D = x.shape
    t = _pick(S, 512, 8)

    def body(x_ref, w_ref, t_ref, dx_ref, dw_ref, l_ref):
        @pl.when(pl.program_id(0) == 0)
        def _():
            dw_ref[...] = jnp.zeros_like(dw_ref)
            l_ref[...] = jnp.zeros_like(l_ref)

        xv = x_ref[...]
        wv = w_ref[...]
        r = lax.rsqrt(jnp.mean(xv * xv, axis=-1, keepdims=True) + NORM_EPS)
        xh = xv * r
        err = xh * wv - t_ref[...]
        per_tok = jnp.mean(err * err, axis=-1, keepdims=True)
        l_ref[...] += 0.5 * jnp.sum(per_tok, axis=0, keepdims=True)
        dy = err * (1.0 / D)
        g = dy * wv
        dx_ref[...] = r * (g - xh * jnp.mean(g * xh, axis=-1, keepdims=True))
        dw_ref[...] += jnp.sum(dy * xh, axis=0, keepdims=True)

    row = pl.BlockSpec((t, D), lambda i: (i, 0))
    vec = pl.BlockSpec((1, D), lambda i: (0, 0))
    lspec = pl.BlockSpec((1, 128), lambda i: (0, 0))
    return pl.pallas_call(
        body, grid=(S // t,), in_specs=[row, vec, row], out_specs=[row, vec, lspec],
        out_shape=[jax.ShapeDtypeStruct((S, D), F32), jax.ShapeDtypeStruct((1, D), F32),
                   jax.ShapeDtypeStruct((1, 128), F32)],
        compiler_params=_cp("arbitrary"), name=name)(x, w.reshape(1, D), tgt)


def _rope_table(seq):
    pos = jnp.arange(seq, dtype=F32)
    inv_freq = ROPE_THETA ** (-jnp.arange(0, 2 * ROPE_HALF, 2, dtype=F32) / (2 * ROPE_HALF))
    ang = pos[:, None] * inv_freq[None, :]
    cos, sin = jnp.cos(ang), jnp.sin(ang)
    pad = HEAD_DIM - 2 * ROPE_HALF
    cos_p = jnp.concatenate([cos, cos, jnp.ones((seq, pad), F32)], axis=1)
    sin_a = jnp.concatenate([-sin, jnp.zeros((seq, HEAD_DIM - ROPE_HALF), F32)], axis=1)
    sin_b = jnp.concatenate([jnp.zeros((seq, ROPE_HALF), F32), sin, jnp.zeros((seq, pad), F32)], axis=1)
    return jnp.concatenate([cos_p, sin_a, sin_b], axis=1)


def _rope(t, tab, sign):
    cos_p = tab[:, 0:HEAD_DIM]
    sin_a = tab[:, HEAD_DIM:2 * HEAD_DIM]
    sin_b = tab[:, 2 * HEAD_DIM:3 * HEAD_DIM]
    up = pltpu.roll(t, HEAD_DIM - ROPE_HALF, 1)
    down = pltpu.roll(t, ROPE_HALF, 1)
    return t * cos_p + sign * (up * sin_a + down * sin_b)


def _perm_tokens(a):
    S = a.shape[0]
    return a.reshape(S // PERM, PERM, -1).transpose(1, 0, 2).reshape(S, -1)


def _unperm_tokens(a):
    S = a.shape[0]
    return a.reshape(PERM, S // PERM, -1).transpose(1, 0, 2).reshape(S, -1)


class _Strided:
    def __init__(self, S, dil):
        self.dil, self.m = dil, PERM // dil
        self.c = ATTN_BLOCK // self.m
        self.rows = S // PERM
        self.nb = S // (dil * ATTN_BLOCK)

    def view(self, a):
        return a.reshape(self.m, self.dil, self.rows, a.shape[-1])

    def spec(self, width, col, f=lambda n: n):
        return pl.BlockSpec((self.m, None, self.c, width), lambda r, n: (0, r, f(n), col))

    def load(self, ref, sl=slice(None)):
        if self.m == 1:
            return ref[0, :, sl]
        return jnp.concatenate([ref[q, :, sl] for q in range(self.m)], axis=0)

    def store(self, ref, sl, val):
        for q in range(self.m):
            ref[q, :, sl] = val[q * self.c:(q + 1) * self.c, :]

    def member(self, i):
        shift = self.c.bit_length() - 1
        return (i & (self.c - 1)) * self.m + (i >> shift)


def _attn_fwd(qkv, tab, g, heads, name):
    S = qkv.shape[0]
    W = heads * HEAD_DIM
    dil = ATTN_DILATIONS[g]
    steps = ATTN_WINDOWS[g] // dil
    B = ATTN_BLOCK
    scale = HEAD_DIM ** -0.5
    st = _Strided(S, dil)

    def body(q_ref, k_ref, kp_ref, v_ref, vp_ref, t_ref, tp_ref, o_ref, l_ref):
        n = pl.program_id(1)
        ii = lax.broadcasted_iota(jnp.int32, (B, 2 * B), 0)
        jj = lax.broadcasted_iota(jnp.int32, (B, 2 * B), 1)
        delta = st.member(ii) - st.member(jj & (B - 1)) + jnp.where(jj >= B, 0, B)
        ok = (delta >= 0) & (delta <= steps) & ((jj >= B) | (n > 0))
        tb = st.load(t_ref)
        tpv = st.load(tp_ref)
        for h in range(heads):
            sl = slice(h * HEAD_DIM, (h + 1) * HEAD_DIM)
            q = _rope(st.load(q_ref, sl), tb, 1.0)
            kc = jnp.concatenate([_rope(st.load(kp_ref, sl), tpv, 1.0), _rope(st.load(k_ref, sl), tb, 1.0)], axis=0)
            vc = jnp.concatenate([st.load(vp_ref, sl), st.load(v_ref, sl)], axis=0)
            s = jnp.where(ok, _dot_nt(q, kc) * scale, NEG)
            m = jnp.max(s, axis=-1, keepdims=True)
            p = jnp.exp(s - m)
            den = jnp.sum(p, axis=-1, keepdims=True)
            st.store(o_ref, sl, _dot_nn(p, vc) / den)
            st.store(l_ref, sl, jnp.broadcast_to(m + jnp.log(den), (B, HEAD_DIM)))

    prv = lambda n: jnp.maximum(n - 1, 0)
    qv, tv = st.view(qkv), st.view(tab)
    o_spec = st.spec(W, 0)
    o, lse = pl.pallas_call(
        body, grid=(dil, st.nb),
        in_specs=[st.spec(W, g * 3), st.spec(W, g * 3 + 1), st.spec(W, g * 3 + 1, prv),
                  st.spec(W, g * 3 + 2), st.spec(W, g * 3 + 2, prv),
                  st.spec(3 * HEAD_DIM, 0), st.spec(3 * HEAD_DIM, 0, prv)],
        out_specs=[o_spec, o_spec],
        out_shape=[jax.ShapeDtypeStruct((st.m, dil, st.rows, W), F32)] * 2,
        compiler_params=_cp("parallel", "parallel"), name=name)(qv, qv, qv, qv, qv, tv, tv)
    return o.reshape(S, W), lse.reshape(S, W)


def _attn_combine(os_, ls_, name):
    S, W = os_[0].shape
    t = _pick(S, 256, 8)

    def body(o0, o1, o2, l0, l1, l2, o_ref, l_ref):
        a, b, c = l0[...], l1[...], l2[...]
        m = jnp.maximum(jnp.maximum(a, b), c)
        ea, eb, ec = jnp.exp(a - m), jnp.exp(b - m), jnp.exp(c - m)
        tot = ea + eb + ec
        o_ref[...] = (ea * o0[...] + eb * o1[...] + ec * o2[...]) / tot
        l_ref[...] = m + jnp.log(tot)

    row = pl.BlockSpec((t, W), lambda i: (i, 0))
    return pl.pallas_call(body, grid=(S // t,), in_specs=[row] * 6, out_specs=[row, row],
                          out_shape=[jax.ShapeDtypeStruct((S, W), F32)] * 2,
                          compiler_params=_cp("parallel"), name=name)(*os_, *ls_)


def _attn_bwd(qkv, tab, o, lse, do, dqkv_prev, g, heads, name):
    S = qkv.shape[0]
    W = heads * HEAD_DIM
    dil = ATTN_DILATIONS[g]
    steps = ATTN_WINDOWS[g] // dil
    B = ATTN_BLOCK
    scale = HEAD_DIM ** -0.5
    st = _Strided(S, dil)
    nb = st.nb
    aliased = dqkv_prev is not None

    def body(q_ref, qn_ref, k_ref, kp_ref, v_ref, vp_ref, do_ref, don_ref, o_ref, on_ref,
             l_ref, ln_ref, t_ref, tp_ref, tn_ref, *rest):
        out_ref = rest[-1]
        n = pl.program_id(1)
        has_next = n < nb - 1
        ia = lax.broadcasted_iota(jnp.int32, (B, 2 * B), 0)
        ja = lax.broadcasted_iota(jnp.int32, (B, 2 * B), 1)
        da = st.member(ia) - st.member(ja & (B - 1)) + jnp.where(ja >= B, 0, B)
        ok_a = (da >= 0) & (da <= steps) & ((ja >= B) | (n > 0))
        ib = lax.broadcasted_iota(jnp.int32, (2 * B, B), 0)
        jb = lax.broadcasted_iota(jnp.int32, (2 * B, B), 1)
        db = st.member(ib & (B - 1)) + jnp.where(ib >= B, B, 0) - st.member(jb)
        ok_b = (db >= 0) & (db <= steps) & ((ib < B) | has_next)
        tb, tpv, tnv = st.load(t_ref), st.load(tp_ref), st.load(tn_ref)
        for h in range(heads):
            sl = slice(h * HEAD_DIM, (h + 1) * HEAD_DIM)
            qr = _rope(st.load(q_ref, sl), tb, 1.0)
            qnr = _rope(st.load(qn_ref, sl), tnv, 1.0)
            kr = _rope(st.load(k_ref, sl), tb, 1.0)
            kpr = _rope(st.load(kp_ref, sl), tpv, 1.0)
            v = st.load(v_ref, sl)
            dov_ = st.load(do_ref, sl)
            donv = st.load(don_ref, sl)
            dl = jnp.sum(dov_ * st.load(o_ref, sl), axis=-1, keepdims=True)
            dln = jnp.sum(donv * st.load(on_ref, sl), axis=-1, keepdims=True)
            ls = st.load(l_ref, sl)
            kc = jnp.concatenate([kpr, kr], axis=0)
            vc = jnp.concatenate([st.load(vp_ref, sl), v], axis=0)
            s = _dot_nt(qr, kc) * scale
            p = jnp.where(ok_a, jnp.exp(jnp.minimum(s - jnp.concatenate([ls, ls], axis=1), 30.0)), 0.0)
            ds = p * (_dot_nt(dov_, vc) - dl) * scale
            st.store(out_ref, sl, _rope(_dot_nn(ds, kc), tb, -1.0))
            qc = jnp.concatenate([qr, qnr], axis=0)
            doc = jnp.concatenate([dov_, donv], axis=0)
            lc = jnp.concatenate([ls, st.load(ln_ref, sl)], axis=0)
            dlc = jnp.concatenate([dl, dln], axis=0)
            s2 = _dot_nt(qc, kr) * scale
            p2 = jnp.where(ok_b, jnp.exp(jnp.minimum(s2 - lc, 30.0)), 0.0)
            ds2 = p2 * (_dot_nt(doc, v) - dlc) * scale
            st.store(out_ref, slice(W + h * HEAD_DIM, W + (h + 1) * HEAD_DIM), _rope(_dot_tn(ds2, qc), tb, -1.0))
            st.store(out_ref, slice(2 * W + h * HEAD_DIM, 2 * W + (h + 1) * HEAD_DIM), _dot_tn(p2, doc))

    nxt = lambda n: jnp.minimum(n + 1, nb - 1)
    prv = lambda n: jnp.maximum(n - 1, 0)
    same = lambda n: n
    q0, q1, q2, tw = g * 3, g * 3 + 1, g * 3 + 2, 3 * HEAD_DIM
    in_specs = [st.spec(W, q0), st.spec(W, q0, nxt), st.spec(W, q1), st.spec(W, q1, prv),
                st.spec(W, q2), st.spec(W, q2, prv)]
    in_specs += [st.spec(W, 0, f) for f in (same, nxt, same, nxt, same, nxt)]
    in_specs += [st.spec(tw, 0, f) for f in (same, prv, nxt)]
    qv, tv, ov, lv, dov = (st.view(a) for a in (qkv, tab, o, lse, do))
    args = [qv, qv, qv, qv, qv, qv, dov, dov, ov, ov, lv, lv, tv, tv, tv]
    kwargs = {}
    if aliased:
        in_specs.append(pl.BlockSpec(memory_space=pl.ANY))
        args.append(st.view(dqkv_prev))
        kwargs["input_output_aliases"] = {len(args) - 1: 0}
    out = pl.pallas_call(
        body, grid=(dil, nb), in_specs=in_specs, out_specs=st.spec(3 * W, g),
        out_shape=jax.ShapeDtypeStruct((st.m, dil, st.rows, 9 * W), F32),
        compiler_params=_cp("parallel", "parallel"), name=name, **kwargs)(*args)
    return out.reshape(S, 9 * W)


def _shift_down(x, halo, s):
    if s == 0:
        return x
    T = x.shape[0]
    xs = pltpu.roll(x, s, 0)
    hs = pltpu.roll(halo, s, 0)
    row8 = lax.broadcasted_iota(jnp.int32, hs.shape, 0)
    top = jnp.where(row8 < s, hs, xs[0:8])
    return jnp.concatenate([top, xs[8:T]], axis=0)


def _shift_up(x, halo, s):
    if s == 0:
        return x
    T = x.shape[0]
    xs = pltpu.roll(x, T - s, 0)
    hs = pltpu.roll(halo, 8 - s, 0)
    row8 = lax.broadcasted_iota(jnp.int32, hs.shape, 0)
    bot = jnp.where(row8 >= 8 - s, hs, xs[T - 8:T])
    return jnp.concatenate([xs[0:T - 8], bot], axis=0)


CONV_ROWS = 128
CONV_LANES = 512


def _conv_apply(x, halo, w_ref, wsl, b, K):
    acc = x * w_ref[K - 1, :, wsl] + b
    for s in range(1, K):
        acc = acc + _shift_down(x, halo, s) * w_ref[K - 1 - s, :, wsl]
    return acc


def _conv_accum(dy, dyn, xv, xp, w_ref, dw_ref, db_ref, wsl, K):
    acc = dy * w_ref[K - 1, :, wsl]
    dw_ref[K - 1, :, wsl] += jnp.sum(dy * xv, axis=0, keepdims=True)
    for s in range(1, K):
        acc = acc + _shift_up(dy, dyn, s) * w_ref[K - 1 - s, :, wsl]
        dw_ref[K - 1 - s, :, wsl] += jnp.sum(dy * _shift_down(xv, xp, s), axis=0, keepdims=True)
    db_ref[:, wsl] += jnp.sum(dy, axis=0, keepdims=True)
    return acc


def _row_specs(T, S, width):
    main = pl.BlockSpec((T, width), lambda i: (i, 0))
    prev = pl.BlockSpec((8, width), lambda i: (jnp.maximum(i * (T // 8) - 1, 0), 0))
    nxt = pl.BlockSpec((8, width), lambda i: (jnp.minimum((i + 1) * (T // 8), S // 8 - 1), 0))
    return main, prev, nxt


def _full(shape):
    return pl.BlockSpec(shape, lambda i: (0,) * len(shape))


def _silu_grad(y):
    sg = _sigmoid(y)
    return sg * (1.0 + y * (1.0 - sg))


def _ssm_conv_fwd(zx, w, b, d_inner, conv_dim, name):
    S, wz = zx.shape
    K = w.shape[0]
    T = _pick(S, CONV_ROWS, 8)
    cw = _pick(conv_dim, CONV_LANES)

    def body(x_ref, h_ref, w_ref, b_ref, c_ref):
        has_prev = pl.program_id(0) > 0
        for cs in range(0, conv_dim, cw):
            so, sx = slice(cs, cs + cw), slice(d_inner + cs, d_inner + cs + cw)
            halo = jnp.where(has_prev, h_ref[:, sx], 0.0)
            c_ref[:, so] = _conv_apply(x_ref[:, sx], halo, w_ref, so, b_ref[:, so], K)

    main, prev, _ = _row_specs(T, S, wz)
    return pl.pallas_call(
        body, grid=(S // T,), in_specs=[main, prev, _full((K, 1, conv_dim)), _full((1, conv_dim))],
        out_specs=pl.BlockSpec((T, conv_dim), lambda i: (i, 0)),
        out_shape=jax.ShapeDtypeStruct((S, conv_dim), F32),
        compiler_params=_cp("parallel"), name=name)(zx, zx, w.reshape(K, 1, conv_dim), b.reshape(1, conv_dim))


def _ssm_conv_bwd(dxs, dbm, dcm, conv, zx, dz, ddt, w, d_inner, name):
    S, wz = zx.shape
    K, conv_dim = w.shape
    gn = dbm.shape[1]
    T = _pick(S, CONV_ROWS, 8)
    cw = _pick(math.gcd(d_inner, gn), CONV_LANES)
    nrow = S // T
    tail = wz - d_inner - conv_dim
    assert ddt.shape[1] == tail

    def body(dx_ref, dxn_ref, db_ref_, dbn_ref, dc_ref, dcn_ref, y_ref, yn_ref, x_ref, xp_ref, dz_ref, ddt_ref,
             w_ref, o_ref, dw_ref, dbias_ref):
        i = pl.program_id(0)

        @pl.when(i == 0)
        def _():
            dw_ref[...] = jnp.zeros_like(dw_ref)
            dbias_ref[...] = jnp.zeros_like(dbias_ref)

        has_prev, has_next = i > 0, i < nrow - 1
        for cs in range(0, d_inner, cw):
            o_ref[:, cs:cs + cw] = dz_ref[:, cs:cs + cw]
        o_ref[:, d_inner + conv_dim:wz] = ddt_ref[...]
        for cs in range(0, conv_dim, cw):
            so, sx = slice(cs, cs + cw), slice(d_inner + cs, d_inner + cs + cw)
            if cs < d_inner:
                src, srcn, ss = dx_ref, dxn_ref, slice(cs, cs + cw)
            elif cs < d_inner + gn:
                src, srcn, ss = db_ref_, dbn_ref, slice(cs - d_inner, cs - d_inner + cw)
            else:
                src, srcn, ss = dc_ref, dcn_ref, slice(cs - d_inner - gn, cs - d_inner - gn + cw)
            dy = src[:, ss] * _silu_grad(y_ref[:, so])
            dyn = jnp.where(has_next, srcn[:, ss] * _silu_grad(yn_ref[:, so]), 0.0)
            xp = jnp.where(has_prev, xp_ref[:, sx], 0.0)
            o_ref[:, sx] = _conv_accum(dy, dyn, x_ref[:, sx], xp, w_ref, dw_ref, dbias_ref, so, K)

    xm, _, xn = _row_specs(T, S, d_inner)
    gm, _, gnx = _row_specs(T, S, gn)
    cm, _, cn = _row_specs(T, S, conv_dim)
    zm, zp, _ = _row_specs(T, S, wz)
    tm_, _, _ = _row_specs(T, S, tail)
    dzx, dw, db = pl.pallas_call(
        body, grid=(nrow,),
        in_specs=[xm, xn, gm, gnx, gm, gnx, cm, cn, zm, zp, xm, tm_, _full((K, 1, conv_dim))],
        out_specs=[zm, _full((K, 1, conv_dim)), _full((1, conv_dim))],
        out_shape=[jax.ShapeDtypeStruct((S, wz), F32), jax.ShapeDtypeStruct((K, 1, conv_dim), F32),
                   jax.ShapeDtypeStruct((1, conv_dim), F32)],
        compiler_params=_cp("arbitrary"), name=name)(
            dxs, dxs, dbm, dbm, dcm, dcm, conv, conv, zx, zx, dz, ddt, w.reshape(K, 1, conv_dim))
    return dzx, dw.reshape(K, conv_dim), db


def _ffn_conv_fwd(up, w, b, name):
    S, C = up.shape
    F = C // 2
    K = w.shape[0]
    T = _pick(S, CONV_ROWS, 8)
    cw = _pick(F, CONV_LANES)

    def body(x_ref, h_ref, w_ref, b_ref, u_ref, a_ref):
        has_prev = pl.program_id(0) > 0
        for cs in range(0, F, cw):
            sg, su = slice(cs, cs + cw), slice(F + cs, F + cs + cw)
            gate = _conv_apply(x_ref[:, sg], jnp.where(has_prev, h_ref[:, sg], 0.0), w_ref, sg, b_ref[:, sg], K)
            upv = _conv_apply(x_ref[:, su], jnp.where(has_prev, h_ref[:, su], 0.0), w_ref, su, b_ref[:, su], K)
            u_ref[0, :, sg] = gate
            u_ref[1, :, sg] = upv
            a_ref[:, sg] = (gate * _sigmoid(gate) * upv).astype(a_ref.dtype)

    main, prev, _ = _row_specs(T, S, C)
    return pl.pallas_call(
        body, grid=(S // T,), in_specs=[main, prev, _full((K, 1, C)), _full((1, C))],
        out_specs=[pl.BlockSpec((2, T, F), lambda i: (0, i, 0)), pl.BlockSpec((T, F), lambda i: (i, 0))],
        out_shape=[jax.ShapeDtypeStruct((2, S, F), F32), jax.ShapeDtypeStruct((S, F), BF16)],
        compiler_params=_cp("parallel"), name=name)(up, up, w.reshape(K, 1, C), b.reshape(1, C))


def _ffn_conv_bwd(u2, dact, up, w, name):
    S, C = up.shape
    F = C // 2
    K = w.shape[0]
    T = _pick(S, CONV_ROWS, 8)
    cw = _pick(F, CONV_LANES)
    nrow = S // T

    def du(half, gate, upv, d):
        sg = _sigmoid(gate)
        return d * upv * sg * (1.0 + gate * (1.0 - sg)) if half == 0 else d * gate * sg

    def body(u_ref, un_ref, d_ref, dn_ref, x_ref, xp_ref, w_ref, dx_ref, dw_ref, db_ref):
        i = pl.program_id(0)

        @pl.when(i == 0)
        def _():
            dw_ref[...] = jnp.zeros_like(dw_ref)
            db_ref[...] = jnp.zeros_like(db_ref)

        has_prev, has_next = i > 0, i < nrow - 1
        for half in range(2):
            for cs in range(0, F, cw):
                sf, sc = slice(cs, cs + cw), slice(half * F + cs, half * F + cs + cw)
                dy = du(half, u_ref[0, :, sf], u_ref[1, :, sf], d_ref[:, sf])
                dyn = jnp.where(has_next, du(half, un_ref[0, :, sf], un_ref[1, :, sf], dn_ref[:, sf]), 0.0)
                xp = jnp.where(has_prev, xp_ref[:, sc], 0.0)
                dx_ref[:, sc] = _conv_accum(dy, dyn, x_ref[:, sc], xp, w_ref, dw_ref, db_ref, sc, K)

    am, _, an = _row_specs(T, S, F)
    xm, xp_, _ = _row_specs(T, S, C)
    u_main = pl.BlockSpec((2, T, F), lambda i: (0, i, 0))
    u_next = pl.BlockSpec((2, 8, F), lambda i: (0, jnp.minimum((i + 1) * (T // 8), S // 8 - 1), 0))
    dx, dw, db = pl.pallas_call(
        body, grid=(nrow,), in_specs=[u_main, u_next, am, an, xm, xp_, _full((K, 1, C))],
        out_specs=[xm, _full((K, 1, C)), _full((1, C))],
        out_shape=[jax.ShapeDtypeStruct((S, C), F32), jax.ShapeDtypeStruct((K, 1, C), F32),
                   jax.ShapeDtypeStruct((1, C), F32)],
        compiler_params=_cp("arbitrary"), name=name)(u2, u2, dact, dact, up, up, w.reshape(K, 1, C))
    return dx, dw.reshape(K, C), db


def _cumsum_rows(v):
    n = v.shape[0]
    row = lax.broadcasted_iota(jnp.int32, v.shape, 0)
    k = 1
    while k < n:
        v = v + jnp.where(row >= k, pltpu.roll(v, k, 0), 0.0)
        k *= 2
    return v


def _rev_cumsum_rows(v):
    n = v.shape[0]
    row = lax.broadcasted_iota(jnp.int32, v.shape, 0)
    k = 1
    while k < n:
        v = v + jnp.where(row < n - k, pltpu.roll(v, n - k, 0), 0.0)
        k *= 2
    return v


def _ssd_common(x_ref, dtr_ref, bias_ref, alog_ref, gw):
    Q = SSM_CHUNK
    X = _silu(x_ref[...])
    pre = dtr_ref[...] + bias_ref[...]
    dt = _softplus(pre)
    a = -jnp.exp(alog_ref[...])
    cs = _cumsum_rows(dt * a)
    row = lax.broadcasted_iota(jnp.int32, (Q, gw), 0)
    cs_last = jnp.sum(jnp.where(row == Q - 1, cs, 0.0), axis=0, keepdims=True)
    return X, pre, dt, a, cs, cs_last, row


def _head_decay(cs, head_mask):
    Q = SSM_CHUNK
    col = jnp.max(jnp.where(head_mask, cs, NEG), axis=1, keepdims=True)
    acol = jnp.broadcast_to(col, (Q, Q))
    arow = acol.T
    ii = lax.broadcasted_iota(jnp.int32, (Q, Q), 0)
    jj = lax.broadcasted_iota(jnp.int32, (Q, Q), 1)
    tril = ii >= jj
    return jnp.where(tril, jnp.exp(jnp.where(tril, acol - arow, 0.0)), 0.0), tril


def _ssd_specs(S, d_inner, gw, nc, rev):
    Q, N, G = SSM_CHUNK, SSM_STATE, SSM_GROUPS
    ch = (lambda c: nc - 1 - c) if rev else (lambda c: c)
    x_spec = pl.BlockSpec((Q, gw), lambda g, c: (ch(c), g))
    b_spec = pl.BlockSpec((Q, N), lambda g, c: (ch(c), d_inner // N + g))
    c_spec = pl.BlockSpec((Q, N), lambda g, c: (ch(c), d_inner // N + G + g))
    p_spec = pl.BlockSpec((None, 1, gw), lambda g, c: (g, 0, 0))
    s_spec = pl.BlockSpec((None, None, gw, N), lambda g, c: (ch(c), g, 0, 0))
    return x_spec, b_spec, c_spec, p_spec, s_spec


def _ssd_fwd(xbc, dtr, bias, alog, dsk, d_inner, name):
    S = xbc.shape[0]
    Q, N, G, P = SSM_CHUNK, SSM_STATE, SSM_GROUPS, SSM_HEAD_DIM
    gw = d_inner // G
    R = gw // P
    nc = S // Q

    def body(x_ref, b_ref, c_ref, dtr_ref, bias_ref, alog_ref, d_ref, y_ref, sp_ref, s_scr):
        @pl.when(pl.program_id(1) == 0)
        def _():
            s_scr[...] = jnp.zeros_like(s_scr)

        X, _, dt, a, cs, cs_last, row = _ssd_common(x_ref, dtr_ref, bias_ref, alog_ref, gw)
        Bm, Cm = _silu(b_ref[...]), _silu(c_ref[...])
        xdt = X * dt
        lane = lax.broadcasted_iota(jnp.int32, (Q, gw), 1)
        sprev = s_scr[...]
        sp_ref[...] = sprev
        cb = _dot_nt(Cm, Bm)
        y = jnp.exp(cs) * _dot_nt(Cm, sprev)
        for r in range(R):
            hm = (lane >= r * P) & (lane < (r + 1) * P)
            dec_l, _ = _head_decay(cs, hm)
            y = y + _dot_nn(cb * dec_l, jnp.where(hm, xdt, 0.0))
        dec = jnp.exp(cs_last - cs)
        cd = jnp.exp(jnp.broadcast_to(cs_last, (Q, gw)).T)
        s_scr[...] = sprev * cd + _dot_tn(xdt * dec, Bm)
        y_ref[...] = y + d_ref[...] * X

    x_spec, b_spec, c_spec, p_spec, s_spec = _ssd_specs(S, d_inner, gw, nc, False)
    return pl.pallas_call(
        body, grid=(G, nc), in_specs=[x_spec, b_spec, c_spec, x_spec, p_spec, p_spec, p_spec],
        out_specs=[x_spec, s_spec],
        out_shape=[jax.ShapeDtypeStruct((S, d_inner), F32), jax.ShapeDtypeStruct((nc, G, gw, N), F32)],
        scratch_shapes=[pltpu.VMEM((gw, N), F32)],
        compiler_params=_cp("parallel", "arbitrary"), name=name)(xbc, xbc, xbc, dtr, bias, alog, dsk)


def _ssd_bwd(xbc, dtr, bias, alog, dsk, sprev_all, dy, d_inner, name):
    S = xbc.shape[0]
    Q, N, G, P = SSM_CHUNK, SSM_STATE, SSM_GROUPS, SSM_HEAD_DIM
    gw = d_inner // G
    R = gw // P
    nc = S // Q

    def body(x_ref, b_ref, c_ref, dtr_ref, bias_ref, alog_ref, d_ref, sp_ref, dy_ref,
             dx_ref, db_ref, dc_ref, ddt_ref, dbias_ref, dalog_ref, dd_ref, ds_scr):
        @pl.when(pl.program_id(1) == 0)
        def _():
            ds_scr[...] = jnp.zeros_like(ds_scr)
            dbias_ref[...] = jnp.zeros_like(dbias_ref)
            dalog_ref[...] = jnp.zeros_like(dalog_ref)
            dd_ref[...] = jnp.zeros_like(dd_ref)

        X, pre, dt, a, cs, cs_last, row = _ssd_common(x_ref, dtr_ref, bias_ref, alog_ref, gw)
        Bm, Cm = _silu(b_ref[...]), _silu(c_ref[...])
        dY = dy_ref[...]
        sprev = sp_ref[...]
        dsn = ds_scr[...]
        xdt = X * dt
        lane = lax.broadcasted_iota(jnp.int32, (Q, gw), 1)
        lane1 = lax.broadcasted_iota(jnp.int32, (1, gw), 1)
        srow = lax.broadcasted_iota(jnp.int32, (gw, N), 0)
        ecs = jnp.exp(cs)
        dec = jnp.exp(cs_last - cs)
        cd = jnp.exp(jnp.broadcast_to(cs_last, (Q, gw)).T)
        dd_ref[...] += jnp.sum(dY * X, axis=0, keepdims=True)
        dX = d_ref[...] * dY
        ey = ecs * dY
        dcs = ey * _dot_nt(Cm, sprev)
        dC = _dot_nn(ey, sprev)
        ds_scr[...] = cd * dsn + _dot_tn(ey, Cm)
        wmat = _dot_nt(Bm, dsn)
        dxdt = dec * wmat
        xd = xdt * dec
        dB = _dot_nn(xd, dsn)
        ddec = xdt * wmat * dec
        dcs = dcs - ddec
        dlast = jnp.sum(ddec, axis=0, keepdims=True)
        qmat = dsn * sprev * cd
        cb = _dot_nt(Cm, Bm)
        dcb = jnp.zeros((Q, Q), F32)
        dcs_rep = jnp.zeros((Q, gw), F32)
        dtx_rep = jnp.zeros((Q, gw), F32)
        for r in range(R):
            hm = (lane >= r * P) & (lane < (r + 1) * P)
            dec_l, tril = _head_decay(cs, hm)
            dyr = jnp.where(hm, dY, 0.0)
            gmat = jnp.where(tril, _dot_nt(dyr, xdt), 0.0)
            dcb = dcb + gmat * dec_l
            e = gmat * cb * dec_l
            v = (jnp.sum(e, axis=1, keepdims=True) - jnp.sum(e.T, axis=1, keepdims=True)
                 + jnp.sum(jnp.where(hm, dcs, 0.0), axis=1, keepdims=True))
            dxdt = dxdt + _dot_tn(cb * dec_l, dyr)
            hm1 = (lane1 >= r * P) & (lane1 < (r + 1) * P)
            t_last = (jnp.sum(jnp.where(hm1, dlast, 0.0), axis=1, keepdims=True)
                      + jnp.sum(jnp.where((srow >= r * P) & (srow < (r + 1) * P), qmat, 0.0), keepdims=True))
            dcs_rep = dcs_rep + jnp.where(hm, v, 0.0) + jnp.where(hm & (row == Q - 1), t_last, 0.0)
        for r in range(R):
            hm = (lane >= r * P) & (lane < (r + 1) * P)
            w_r = jnp.sum(jnp.where(hm, dxdt * X, 0.0), axis=1, keepdims=True)
            dtx_rep = dtx_rep + jnp.where(hm, w_r, 0.0)
        dadt = _rev_cumsum_rows(dcs_rep)
        ddt = a * dadt + dtx_rep
        dalog_ref[...] += jnp.sum(dt * dadt, axis=0, keepdims=True) * a
        draw = ddt * _sigmoid(pre)
        ddt_ref[...] = draw
        dbias_ref[...] += jnp.sum(draw, axis=0, keepdims=True)
        dx_ref[...] = dX + dxdt * dt
        db_ref[...] = dB + _dot_tn(dcb, Cm)
        dc_ref[...] = dC + _dot_nn(dcb, Bm)

    x_spec, b_spec, c_spec, p_spec, s_spec = _ssd_specs(S, d_inner, gw, nc, True)
    n_spec = pl.BlockSpec((Q, N), lambda g, c: (nc - 1 - c, g))
    gshape = jax.ShapeDtypeStruct((G, 1, gw), F32)
    return pl.pallas_call(
        body, grid=(G, nc),
        in_specs=[x_spec, b_spec, c_spec, x_spec, p_spec, p_spec, p_spec, s_spec, x_spec],
        out_specs=[x_spec, n_spec, n_spec, x_spec, p_spec, p_spec, p_spec],
        out_shape=[jax.ShapeDtypeStruct((S, d_inner), F32), jax.ShapeDtypeStruct((S, G * N), F32),
                   jax.ShapeDtypeStruct((S, G * N), F32), jax.ShapeDtypeStruct((S, d_inner), F32),
                   gshape, gshape, gshape],
        scratch_shapes=[pltpu.VMEM((gw, N), F32)],
        compiler_params=_cp("parallel", "arbitrary"), name=name)(
            xbc, xbc, xbc, dtr, bias, alog, dsk, sprev_all, dy)


def _gnorm_fwd(y, zx, w, name):
    S, d_inner = y.shape
    G = SSM_GROUPS
    gw = d_inner // G
    T = _pick(S, 256, 8)

    def body(y_ref, z_ref, w_ref, o_ref):
        for k in range(G):
            sl = slice(k * gw, (k + 1) * gw)
            z = z_ref[:, sl]
            gk = y_ref[:, sl] * z * _sigmoid(z)
            r = lax.rsqrt(jnp.mean(gk * gk, axis=-1, keepdims=True) + NORM_EPS)
            o_ref[:, sl] = (gk * r * w_ref[:, sl]).astype(o_ref.dtype)

    row = pl.BlockSpec((T, d_inner), lambda i: (i, 0))
    vec = pl.BlockSpec((1, d_inner), lambda i: (0, 0))
    return pl.pallas_call(body, grid=(S // T,), in_specs=[row, row, vec], out_specs=row,
                          out_shape=jax.ShapeDtypeStruct((S, d_inner), BF16),
                          compiler_params=_cp("parallel"), name=name)(y, zx, w.reshape(1, d_inner))


def _gnorm_bwd(y, zx, w, dout, name):
    S, d_inner = y.shape
    G = SSM_GROUPS
    gw = d_inner // G
    T = _pick(S, 256, 8)

    def body(y_ref, z_ref, w_ref, d_ref, dy_ref, dz_ref, dw_ref):
        @pl.when(pl.program_id(0) == 0)
        def _():
            dw_ref[...] = jnp.zeros_like(dw_ref)

        for k in range(G):
            sl = slice(k * gw, (k + 1) * gw)
            z, yv, d = z_ref[:, sl], y_ref[:, sl], d_ref[:, sl]
            sg = _sigmoid(z)
            sz = z * sg
            gk = yv * sz
            r = lax.rsqrt(jnp.mean(gk * gk, axis=-1, keepdims=True) + NORM_EPS)
            gh = gk * r
            dw_ref[:, sl] += jnp.sum(d * gh, axis=0, keepdims=True)
            dg = d * w_ref[:, sl]
            dgk = r * (dg - gh * jnp.mean(dg * gh, axis=-1, keepdims=True))
            dy_ref[:, sl] = dgk * sz
            dz_ref[:, sl] = dgk * yv * sg * (1.0 + z * (1.0 - sg))

    row = pl.BlockSpec((T, d_inner), lambda i: (i, 0))
    vec = pl.BlockSpec((1, d_inner), lambda i: (0, 0))
    return pl.pallas_call(
        body, grid=(S // T,), in_specs=[row, row, vec, row], out_specs=[row, row, vec],
        out_shape=[jax.ShapeDtypeStruct((S, d_inner), F32)] * 2 + [jax.ShapeDtypeStruct((1, d_inner), F32)],
        compiler_params=_cp("arbitrary"), name=name)(y, zx, w.reshape(1, d_inner), dout)


def _adam_math(g, w, m, v):
    m = ADAM_B1 * m + (1.0 - ADAM_B1) * g
    v = ADAM_B2 * v + (1.0 - ADAM_B2) * (g * g)
    m_hat = m / (1.0 - ADAM_B1 ** ADAM_STEP)
    v_hat = v / (1.0 - ADAM_B2 ** ADAM_STEP)
    delta = -ADAM_LR * (m_hat / (jnp.sqrt(v_hat) + ADAM_EPS) + ADAM_WD * w)
    return delta, m, v


def _adamw_big(own, sib, w, m, v, name):
    L, A, Bc = w.shape
    T = _pick(A, max(8, (1 << 19) // (4 * Bc)), 16)

    def body(o_ref, s_ref, w_ref, m_ref, v_ref, g_ref, d_ref, nm_ref, nv_ref):
        so = o_ref[0].astype(F32)
        ss = s_ref[0].astype(F32)
        for k in range(1, N_CHIPS):
            so = so + o_ref[k].astype(F32)
            ss = ss + s_ref[k].astype(F32)
        g = so + ss
        delta, nm, nv = _adam_math(g, w_ref[...], m_ref[...], v_ref[...])
        g_ref[...] = g
        d_ref[...] = delta
        nm_ref[...] = nm
        nv_ref[...] = nv

    part = pl.BlockSpec((N_CHIPS, None, T, Bc), lambda l, i: (0, l, i, 0))
    blk = pl.BlockSpec((None, T, Bc), lambda l, i: (l, i, 0))
    shp = jax.ShapeDtypeStruct(w.shape, F32)
    return pl.pallas_call(body, grid=(L, A // T), in_specs=[part, part, blk, blk, blk],
                          out_specs=[blk] * 4, out_shape=[shp] * 4,
                          compiler_params=_cp("parallel", "parallel"), name=name)(own, sib, w, m, v)


def _sum_devices(parts, name):
    _, R, C = parts.shape

    def body(p_ref, o_ref):
        acc = p_ref[0]
        for k in range(1, N_DEV):
            acc = acc + p_ref[k]
        o_ref[...] = acc

    return pl.pallas_call(body, out_shape=jax.ShapeDtypeStruct((R, C), F32), name=name)(parts)


def _adamw_small(g, w, m, v, name):
    def body(g_ref, w_ref, m_ref, v_ref, d_ref, nm_ref, nv_ref):
        delta, nm, nv = _adam_math(g_ref[...], w_ref[...], m_ref[...], v_ref[...])
        d_ref[...] = delta
        nm_ref[...] = nm
        nv_ref[...] = nv

    shp = jax.ShapeDtypeStruct(g.shape, F32)
    return pl.pallas_call(body, out_shape=[shp] * 3, name=name)(g, w, m, v)


PACK_COLS = 1024


def _pack(arrs):
    flat = jnp.concatenate([a.reshape(-1).astype(F32) for a in arrs])
    n = flat.shape[0]
    rows = -(-n // (8 * PACK_COLS)) * 8
    return jnp.pad(flat, (0, rows * PACK_COLS - n)).reshape(rows, PACK_COLS)


def _unpack(packed, shapes):
    flat = packed.reshape(-1)
    out, off = [], 0
    for s in shapes:
        n = math.prod(s)
        out.append(flat[off:off + n].reshape(s))
        off += n
    return out


def _shard_ref(ref, kind, k, n):
    if kind == "col":
        return ref.at[:, pl.ds(pl.multiple_of(k * n, 128), n)]
    if kind == "row":
        return ref.at[pl.ds(pl.multiple_of(k * n, 16), n), :]
    return ref.at[k]


def _chip_peers():
    x, y, c = lax.axis_index("x"), lax.axis_index("y"), lax.axis_index("c")
    return x, y, c, [(1 - x, y), (x, 1 - y), (1 - x, 1 - y)]


def _all_gather_chips(items, name):
    arrays = []
    for it in items:
        if not any(it[0] is a for a in arrays):
            arrays.append(it[0])
    src_idx = [next(i for i, a in enumerate(arrays) if a is it[0]) for it in items]
    nin, nit = len(arrays), len(items)

    def body(*refs):
        ins, outs = refs[:nin], refs[nin:nin + nit]
        send_sems, recv_sems, loc_sems = refs[nin + nit:]
        x, y, c, peers = _chip_peers()
        me = 2 * x + y
        started = []
        for t, (_, layer, kind, n, _) in enumerate(items):
            src = ins[src_idx[t]] if layer is None else ins[src_idx[t]].at[layer]
            mine = _shard_ref(outs[t], kind, me, n)
            lc = pltpu.make_async_copy(src, mine, loc_sems.at[t])
            lc.start()
            started.append(lc)
            for j, (px, py) in enumerate(peers):
                rc = pltpu.make_async_remote_copy(
                    src_ref=src, dst_ref=mine, send_sem=send_sems.at[3 * t + j],
                    recv_sem=recv_sems.at[3 * t + j], device_id=(px, py, c), device_id_type=MESH)
                rc.start()
                started.append(rc)
        for t, (_, layer, kind, n, _) in enumerate(items):
            src = ins[src_idx[t]] if layer is None else ins[src_idx[t]].at[layer]
            for j, (px, py) in enumerate(peers):
                theirs = _shard_ref(outs[t], kind, 2 * px + py, n)
                pltpu.make_async_remote_copy(
                    src_ref=src, dst_ref=theirs, send_sem=send_sems.at[3 * t + j],
                    recv_sem=recv_sems.at[3 * t + j], device_id=(px, py, c), device_id_type=MESH).wait_recv()
        for cp in started:
            if cp.is_remote:
                cp.wait_send()
            else:
                cp.wait()

    anyspec = pl.BlockSpec(memory_space=pl.ANY)
    return pl.pallas_call(
        body, in_specs=[anyspec] * nin, out_specs=[anyspec] * nit,
        out_shape=[jax.ShapeDtypeStruct(it[4], it[0].dtype) for it in items],
        scratch_shapes=[pltpu.SemaphoreType.DMA((3 * nit,)), pltpu.SemaphoreType.DMA((3 * nit,)),
                        pltpu.SemaphoreType.DMA((nit,))],
        name=name)(*arrays)


def _reduce_scatter_chips(items, name):
    grads = [it[0] for it in items]
    nit = len(items)
    bufs = []
    for it in items:
        if it[3] not in [b[0] for b in bufs]:
            bufs.append((it[3], it[5], it[0].dtype))
    nbuf = len(bufs)

    def body(*refs):
        ins, outs = refs[:nit], refs[nit:nit + nbuf]
        send_sems, recv_sems, loc_sems = refs[nit + nbuf:]
        x, y, c, peers = _chip_peers()
        me = 2 * x + y
        started = []
        for t, (_, kind, n, bi, layer, _) in enumerate(items):
            slot = lambda k: outs[bi].at[k, layer]
            lc = pltpu.make_async_copy(_shard_ref(ins[t], kind, me, n), slot(me), loc_sems.at[t])
            lc.start()
            started.append(lc)
            for j, (px, py) in enumerate(peers):
                rc = pltpu.make_async_remote_copy(
                    src_ref=_shard_ref(ins[t], kind, 2 * px + py, n), dst_ref=slot(me),
                    send_sem=send_sems.at[3 * t + j], recv_sem=recv_sems.at[3 * t + j],
                    device_id=(px, py, c), device_id_type=MESH)
                rc.start()
                started.append(rc)
        for t, (_, kind, n, bi, layer, _) in enumerate(items):
            for j, (px, py) in enumerate(peers):
                pk = 2 * px + py
                pltpu.make_async_remote_copy(
                    src_ref=_shard_ref(ins[t], kind, pk, n), dst_ref=outs[bi].at[pk, layer],
                    send_sem=send_sems.at[3 * t + j], recv_sem=recv_sems.at[3 * t + j],
                    device_id=(px, py, c), device_id_type=MESH).wait_recv()
        for cp in started:
            if cp.is_remote:
                cp.wait_send()
            else:
                cp.wait()

    anyspec = pl.BlockSpec(memory_space=pl.ANY)
    return pl.pallas_call(
        body, in_specs=[anyspec] * nit, out_specs=[anyspec] * nbuf,
        out_shape=[jax.ShapeDtypeStruct(shape, dt) for _, shape, dt in bufs],
        scratch_shapes=[pltpu.SemaphoreType.DMA((3 * nit,)), pltpu.SemaphoreType.DMA((3 * nit,)),
                        pltpu.SemaphoreType.DMA((nit,))],
        name=name)(*grads)


def _swap_with_sibling(arrs, name):
    n = len(arrs)

    def body(*refs):
        ins, outs = refs[:n], refs[n:2 * n]
        send_sems, recv_sems = refs[2 * n:]
        x, y, c = lax.axis_index("x"), lax.axis_index("y"), lax.axis_index("c")
        copies = [pltpu.make_async_remote_copy(
            src_ref=ins[t], dst_ref=outs[t], send_sem=send_sems.at[t], recv_sem=recv_sems.at[t],
            device_id=(x, y, 1 - c), device_id_type=MESH) for t in range(n)]
        for cp in copies:
            cp.start()
        for cp in copies:
            cp.wait_recv()
        for cp in copies:
            cp.wait_send()

    anyspec = pl.BlockSpec(memory_space=pl.ANY)
    return pl.pallas_call(
        body, in_specs=[anyspec] * n, out_specs=[anyspec] * n,
        out_shape=[jax.ShapeDtypeStruct(a.shape, a.dtype) for a in arrs],
        scratch_shapes=[pltpu.SemaphoreType.DMA((n,)), pltpu.SemaphoreType.DMA((n,))],
        name=name)(*arrs)


def _all_gather_devices(v, name):
    def body(v_ref, o_ref, send_sems, recv_sems, loc_sem):
        x, y, c = lax.axis_index("x"), lax.axis_index("y"), lax.axis_index("c")
        me = 4 * x + 2 * y + c
        lc = pltpu.make_async_copy(v_ref, o_ref.at[me], loc_sem)
        lc.start()
        rel = [(bx, by, bc) for bx in (0, 1) for by in (0, 1) for bc in (0, 1)][1:]
        copies = []
        for j, (bx, by, bc) in enumerate(rel):
            px, py, pc = x ^ bx, y ^ by, c ^ bc
            copies.append((pltpu.make_async_remote_copy(
                src_ref=v_ref, dst_ref=o_ref.at[me], send_sem=send_sems.at[j], recv_sem=recv_sems.at[j],
                device_id=(px, py, pc), device_id_type=MESH), 4 * px + 2 * py + pc))
        for cp, _ in copies:
            cp.start()
        for j, (cp, pid) in enumerate(copies):
            pltpu.make_async_remote_copy(
                src_ref=v_ref, dst_ref=o_ref.at[pid], send_sem=send_sems.at[j], recv_sem=recv_sems.at[j],
                device_id=(x, y, c), device_id_type=MESH).wait_recv()
        for cp, _ in copies:
            cp.wait_send()
        lc.wait()

    anyspec = pl.BlockSpec(memory_space=pl.ANY)
    return pl.pallas_call(
        body, in_specs=[anyspec], out_specs=anyspec,
        out_shape=jax.ShapeDtypeStruct((N_DEV,) + v.shape, v.dtype),
        scratch_shapes=[pltpu.SemaphoreType.DMA((N_DEV - 1,)), pltpu.SemaphoreType.DMA((N_DEV - 1,)),
                        pltpu.SemaphoreType.DMA(())],
        name=name)(v)


BIG = ("attn_w_qkv", "attn_w_o", "ssm_w_in", "ssm_w_out", "ffn_w_up", "ffn_w_down")
BIG_KIND = {"attn_w_qkv": "col", "attn_w_o": "row", "ssm_w_in": "lead", "ssm_w_out": "row",
            "ffn_w_up": "col", "ffn_w_down": "row"}
SMALL_SHARDED = {"ssm_conv_w": 2, "ssm_conv_b": 1, "ssm_norm_w": 1, "ffn_conv_w": 2}
SMALL = ("mix_norm_w", "ssm_conv_w", "ssm_conv_b", "ssm_dt_bias", "ssm_a_log", "ssm_d", "ssm_norm_w",
         "ffn_norm_w", "ffn_conv_w", "ffn_conv_b", "final_norm_w")
WEIGHTS = ("mix_norm_w", "attn_w_qkv", "attn_w_o", "ssm_w_in", "ssm_conv_w", "ssm_conv_b", "ssm_dt_bias",
           "ssm_a_log", "ssm_d", "ssm_norm_w", "ssm_w_out", "ffn_norm_w", "ffn_w_up", "ffn_conv_w",
           "ffn_conv_b", "ffn_w_down", "final_norm_w")


def _gather_weights(W):
    items, keys = [], []
    for name in BIG:
        w16 = W[name].astype(BF16)
        L, a, b = w16.shape
        kind = BIG_KIND[name]
        full = {"col": (a, N_CHIPS * b), "row": (N_CHIPS * a, b), "lead": (N_CHIPS, a, b)}[kind]
        n = {"col": b, "row": a, "lead": 1}[kind]
        for l in range(L):
            items.append((w16, l, kind, n, full))
            keys.append((name, l))
    sm_names = list(SMALL_SHARDED)
    packed = _pack([W[n] for n in sm_names])
    items.append((packed, None, "lead", 1, (N_CHIPS,) + packed.shape))
    outs = _all_gather_chips(items, "gather_weights")
    full = {}
    for (name, l), o in zip(keys, outs[:-1]):
        full.setdefault(name, []).append(o)
    per_chip = [_unpack(outs[-1][k], [W[n].shape for n in sm_names]) for k in range(N_CHIPS)]
    for i, n in enumerate(sm_names):
        full[n] = jnp.concatenate([per_chip[k][i] for k in range(N_CHIPS)], axis=SMALL_SHARDED[n])
    return full


def kernel(x, mix_norm_w, attn_w_qkv, attn_w_o, ssm_w_in, ssm_conv_w, ssm_conv_b, ssm_dt_bias, ssm_a_log, ssm_d, ssm_norm_w, ssm_w_out, ffn_norm_w, ffn_w_up, ffn_conv_w, ffn_conv_b, ffn_w_down, final_norm_w, loss_target, m_mix_norm_w, m_attn_w_qkv, m_attn_w_o, m_ssm_w_in, m_ssm_conv_w, m_ssm_conv_b, m_ssm_dt_bias, m_ssm_a_log, m_ssm_d, m_ssm_norm_w, m_ssm_w_out, m_ffn_norm_w, m_ffn_w_up, m_ffn_conv_w, m_ffn_conv_b, m_ffn_w_down, m_final_norm_w, v_mix_norm_w, v_attn_w_qkv, v_attn_w_o, v_ssm_w_in, v_ssm_conv_w, v_ssm_conv_b, v_ssm_dt_bias, v_ssm_a_log, v_ssm_d, v_ssm_norm_w, v_ssm_w_out, v_ffn_norm_w, v_ffn_w_up, v_ffn_conv_w, v_ffn_conv_b, v_ffn_w_down, v_final_norm_w):
    W = dict(mix_norm_w=mix_norm_w, attn_w_qkv=attn_w_qkv, attn_w_o=attn_w_o, ssm_w_in=ssm_w_in,
             ssm_conv_w=ssm_conv_w, ssm_conv_b=ssm_conv_b, ssm_dt_bias=ssm_dt_bias, ssm_a_log=ssm_a_log,
             ssm_d=ssm_d, ssm_norm_w=ssm_norm_w, ssm_w_out=ssm_w_out, ffn_norm_w=ffn_norm_w, ffn_w_up=ffn_w_up,
             ffn_conv_w=ffn_conv_w, ffn_conv_b=ffn_conv_b, ffn_w_down=ffn_w_down, final_norm_w=final_norm_w)
    M = dict(mix_norm_w=m_mix_norm_w, attn_w_qkv=m_attn_w_qkv, attn_w_o=m_attn_w_o, ssm_w_in=m_ssm_w_in,
             ssm_conv_w=m_ssm_conv_w, ssm_conv_b=m_ssm_conv_b, ssm_dt_bias=m_ssm_dt_bias, ssm_a_log=m_ssm_a_log,
             ssm_d=m_ssm_d, ssm_norm_w=m_ssm_norm_w, ssm_w_out=m_ssm_w_out, ffn_norm_w=m_ffn_norm_w,
             ffn_w_up=m_ffn_w_up, ffn_conv_w=m_ffn_conv_w, ffn_conv_b=m_ffn_conv_b, ffn_w_down=m_ffn_w_down,
             final_norm_w=m_final_norm_w)
    V = dict(mix_norm_w=v_mix_norm_w, attn_w_qkv=v_attn_w_qkv, attn_w_o=v_attn_w_o, ssm_w_in=v_ssm_w_in,
             ssm_conv_w=v_ssm_conv_w, ssm_conv_b=v_ssm_conv_b, ssm_dt_bias=v_ssm_dt_bias, ssm_a_log=v_ssm_a_log,
             ssm_d=v_ssm_d, ssm_norm_w=v_ssm_norm_w, ssm_w_out=v_ssm_w_out, ffn_norm_w=v_ffn_norm_w,
             ffn_w_up=v_ffn_w_up, ffn_conv_w=v_ffn_conv_w, ffn_conv_b=v_ffn_conv_b, ffn_w_down=v_ffn_w_down,
             final_norm_w=v_final_norm_w)

    S, D = x.shape[1], x.shape[2]
    xs = x.reshape(S, D)
    tgt = loss_target.reshape(S, D)
    depth = mix_norm_w.shape[0]
    heads = attn_w_o.shape[1] * N_CHIPS // HEAD_DIM
    AW = heads * HEAD_DIM
    d_inner = ssm_w_out.shape[1] * N_CHIPS
    ssm_heads = d_inner // SSM_HEAD_DIM
    G, P, N = SSM_GROUPS, SSM_HEAD_DIM, SSM_STATE
    gw = d_inner // G
    conv_dim = d_inner + 2 * G * N
    in_w = d_inner + conv_dim + ssm_heads
    in_pad = -(-in_w // 128) * 128
    shard_in = ssm_w_in.shape[2]
    xi, yi = lax.axis_index("x"), lax.axis_index("y")
    chip = 2 * xi + yi

    full = _gather_weights(W)
    w_in_full = [jnp.pad(jnp.concatenate([g[k] for k in range(N_CHIPS)], axis=1), ((0, 0), (0, in_pad - in_w)))
                 for g in full["ssm_w_in"]]
    tab = _perm_tokens(_rope_table(S))

    def rep_heads(p):
        return jnp.repeat(p, P).reshape(G, 1, gw)

    saved = []
    cur = xs
    for i in range(depth):
        j = i // 2
        sv = {"x_in": cur}
        h = _rms_fwd(cur, mix_norm_w[i], f"mix_norm_fwd_{i}")
        sv["h"] = h
        if i % 2 == 0:
            h = _perm_tokens(h)
            sv["h"] = h
            qkv = _matmul(h, full["attn_w_qkv"][j], "nn", F32, f"qkv_fwd_{i}")
            og = [_attn_fwd(qkv, tab, g, heads, f"attn_fwd_{i}_{g}") for g in range(3)]
            o, lse = _attn_combine([a for a, _ in og], [b for _, b in og], f"attn_combine_{i}")
            mixed = _unperm_tokens(_matmul(o, full["attn_w_o"][j], "nn", F32, f"attn_out_fwd_{i}"))
            sv.update(qkv=qkv, o=o, lse=lse)
        else:
            zx = _matmul(h, w_in_full[j], "nn", F32, f"ssm_in_fwd_{i}")
            conv = _ssm_conv_fwd(zx, full["ssm_conv_w"][j], full["ssm_conv_b"][j], d_inner, conv_dim,
                                 f"ssm_conv_fwd_{i}")
            dtr = jnp.repeat(zx[:, d_inner + conv_dim:in_w], P, axis=1)
            prm = [rep_heads(p[j]) for p in (ssm_dt_bias, ssm_a_log, ssm_d)]
            y, sprev = _ssd_fwd(conv, dtr, *prm, d_inner, f"ssd_fwd_{i}")
            gated = _gnorm_fwd(y, zx, full["ssm_norm_w"][j], f"ssm_norm_fwd_{i}")
            cur = _matmul(gated, full["ssm_w_out"][j], "nn", F32, f"ssm_out_fwd_{i}", resid=cur)
            sv.update(zx=zx, conv=conv, dtr=dtr, prm=prm, y=y, sprev=sprev, gated=gated)
            mixed = None
        if mixed is None:
            h2 = _rms_fwd(cur, ffn_norm_w[i], f"ffn_norm_fwd_{i}")
        else:
            cur, h2 = _rms_fwd(cur, ffn_norm_w[i], f"ffn_norm_fwd_{i}", add=mixed)
        sv["x_mid"] = cur
        up = _matmul(h2, full["ffn_w_up"][i], "nn", F32, f"ffn_up_fwd_{i}")
        u2, act = _ffn_conv_fwd(up, full["ffn_conv_w"][i], ffn_conv_b[i], f"ffn_conv_fwd_{i}")
        cur = _matmul(act, full["ffn_w_down"][i], "nn", F32, f"ffn_down_fwd_{i}", resid=cur)
        sv.update(h2=h2, up=up, u2=u2, act=act)
        saved.append(sv)

    dx, d_final, loss_part = _loss_head(cur, final_norm_w, tgt, "loss_head")
    gbig = {n: [None] * W[n].shape[0] for n in BIG}
    gs = {n: [None] * W[n].shape[0] for n in SMALL if n != "final_norm_w"}
    for i in reversed(range(depth)):
        j = i // 2
        sv = saved[i]
        dact = _matmul(dx, full["ffn_w_down"][i], "nt", F32, f"ffn_down_dgrad_{i}")
        gbig["ffn_w_down"][i] = _matmul(sv["act"], dx, "tn", BF16, f"ffn_down_wgrad_{i}")
        dup, dcw, dcb = _ffn_conv_bwd(sv["u2"], dact, sv["up"], full["ffn_conv_w"][i], f"ffn_conv_bwd_{i}")
        gs["ffn_conv_w"][i], gs["ffn_conv_b"][i] = dcw, dcb[0]
        dh2 = _matmul(dup, full["ffn_w_up"][i], "nt", F32, f"ffn_up_dgrad_{i}")
        gbig["ffn_w_up"][i] = _matmul(sv["h2"], dup, "tn", BF16, f"ffn_up_wgrad_{i}")
        dx, dnw = _rms_bwd(sv["x_mid"], ffn_norm_w[i], dh2, dx, f"ffn_norm_bwd_{i}")
        gs["ffn_norm_w"][i] = dnw[0]
        if i % 2 == 0:
            dxp = _perm_tokens(dx)
            do = _matmul(dxp, full["attn_w_o"][j], "nt", F32, f"attn_out_dgrad_{i}")
            gbig["attn_w_o"][j] = _matmul(sv["o"], dxp, "tn", BF16, f"attn_out_wgrad_{i}")
            dqkv = None
            for g in range(3):
                dqkv = _attn_bwd(sv["qkv"], tab, sv["o"], sv["lse"], do, dqkv, g, heads, f"attn_bwd_{i}_{g}")
            dh = _unperm_tokens(_matmul(dqkv, full["attn_w_qkv"][j], "nt", F32, f"qkv_dgrad_{i}"))
            gbig["attn_w_qkv"][j] = _matmul(sv["h"], dqkv, "tn", BF16, f"qkv_wgrad_{i}")
        else:
            dgated = _matmul(dx, full["ssm_w_out"][j], "nt", F32, f"ssm_out_dgrad_{i}")
            gbig["ssm_w_out"][j] = _matmul(sv["gated"], dx, "tn", BF16, f"ssm_out_wgrad_{i}")
            dy, dz, dgw = _gnorm_bwd(sv["y"], sv["zx"], full["ssm_norm_w"][j], dgated, f"ssm_norm_bwd_{i}")
            gs["ssm_norm_w"][j] = dgw[0]
            dxs_, dbm, dcm, ddtr, dbias, dalog, ddsk = _ssd_bwd(
                sv["conv"], sv["dtr"], *sv["prm"], sv["sprev"], dy, d_inner, f"ssd_bwd_{i}")
            gs["ssm_dt_bias"][j] = dbias.reshape(-1)[::P]
            gs["ssm_a_log"][j] = dalog.reshape(-1)[::P]
            gs["ssm_d"][j] = ddsk.reshape(ssm_heads, P).sum(axis=1)
            ddt = jnp.pad(ddtr[:, ::P], ((0, 0), (0, in_pad - in_w)))
            dzx, dcw, dcb = _ssm_conv_bwd(dxs_, dbm, dcm, sv["conv"], sv["zx"], dz, ddt, full["ssm_conv_w"][j],
                                          d_inner, f"ssm_conv_bwd_{i}")
            gs["ssm_conv_w"][j], gs["ssm_conv_b"][j] = dcw, dcb[0]
            dh = _matmul(dzx, w_in_full[j], "nt", F32, f"ssm_in_dgrad_{i}")
            dwin = _matmul(sv["h"], dzx, "tn", BF16, f"ssm_in_wgrad_{i}")
            gbig["ssm_w_in"][j] = jnp.stack([dwin[:, k * shard_in:(k + 1) * shard_in] for k in range(N_CHIPS)])
        dx, dnw = _rms_bwd(sv["x_in"], mix_norm_w[i], dh, dx, f"mix_norm_bwd_{i}")
        gs["mix_norm_w"][i] = dnw[0]
    grad_x = dx.reshape(x.shape)

    rs_items = []
    for bi, name in enumerate(BIG):
        L, a, b = W[name].shape
        kind = BIG_KIND[name]
        n = {"col": b, "row": a, "lead": 1}[kind]
        for l in range(L):
            rs_items.append((gbig[name][l], kind, n, bi, l, (N_CHIPS, L, a, b)))
    own = _reduce_scatter_chips(rs_items, "scatter_grads")
    sib = _swap_with_sibling(own, "swap_grads")

    small_full = [jnp.stack(gs[n]) if n != "final_norm_w" else d_final[0] for n in SMALL]
    small_full.append(loss_part[0, 0:1])
    small_shapes = [a.shape for a in small_full]
    summed = _sum_devices(_all_gather_devices(_pack(small_full), "gather_small_grads"), "sum_small_grads")
    small_g = _unpack(summed, small_shapes)
    loss = small_g[-1][0]
    gsm = {}
    for n, g in zip(SMALL, small_g[:-1]):
        if n in SMALL_SHARDED:
            ax = SMALL_SHARDED[n]
            ext = W[n].shape[ax]
            g = lax.dynamic_slice_in_dim(g, chip * ext, ext, axis=ax)
        gsm[n] = g

    out_g, out_d, out_m, out_v = {}, {}, {}, {}
    for bi, name in enumerate(BIG):
        out_g[name], out_d[name], out_m[name], out_v[name] = _adamw_big(
            own[bi], sib[bi], W[name], M[name], V[name], f"adamw_{name}")
    shapes = [W[n].shape for n in SMALL]
    pd, pm, pv = _adamw_small(_pack([gsm[n] for n in SMALL]), _pack([W[n] for n in SMALL]),
                              _pack([M[n] for n in SMALL]), _pack([V[n] for n in SMALL]), "adamw_small")
    for n, d_, m_, v_ in zip(SMALL, _unpack(pd, shapes), _unpack(pm, shapes), _unpack(pv, shapes)):
        out_g[n], out_d[n], out_m[n], out_v[n] = gsm[n], d_, m_, v_

    return (loss, grad_x, *[out_g[n] for n in WEIGHTS], *[out_d[n] for n in WEIGHTS],
            *[out_m[n] for n in WEIGHTS], *[out_v[n] for n in WEIGHTS])
```

```python
import functools
import math

import jax
import jax.numpy as jnp
from jax import lax
from jax.experimental import pallas as pl
from jax.experimental.pallas import tpu as pltpu

F32 = jnp.float32
BF16 = jnp.bfloat16
MESH = pl.DeviceIdType.MESH

NORM_EPS = 1e-5
HEAD_DIM = 128
ATTN_BLOCK = 128
ATTN_DILATIONS = (1, 4, 16)
ATTN_WINDOWS = (128, 512, 2048)
PERM = 16
ROPE_THETA = 500000.0
ROPE_HALF = HEAD_DIM // 8
SSM_HEAD_DIM = 64
SSM_STATE = 128
SSM_GROUPS = 8
SSM_CHUNK = 128
NEG = -1e30

ADAM_LR = 0.001
ADAM_B1 = 0.9
ADAM_B2 = 0.999
ADAM_EPS = 1e-08
ADAM_WD = 0.01
ADAM_STEP = 10

VMEM_LIMIT_BYTES = 48 * 1024 * 1024
N_CHIPS = 4
N_DEV = 8


def _cp(*sem):
    return pltpu.CompilerParams(dimension_semantics=sem, vmem_limit_bytes=VMEM_LIMIT_BYTES)


def _pick(n, pref, mult=128):
    best = None
    t = mult
    while t <= min(n, pref):
        if n % t == 0:
            best = t
        t += mult
    return n if best is None else best


def _sigmoid(x):
    return 1.0 / (1.0 + jnp.exp(-x))


def _silu(x):
    return x * _sigmoid(x)


def _softplus(x):
    u = jnp.exp(-jnp.abs(x))
    w = 1.0 + u
    log1p = jnp.where(w == 1.0, u, jnp.log(w) * (u / jnp.where(w == 1.0, 1.0, w - 1.0)))
    return jnp.maximum(x, 0.0) + log1p


def _dot(a, b, dims):
    return lax.dot_general(a.astype(BF16), b.astype(BF16), (dims, ((), ())),
                           preferred_element_type=F32)


def _dot_nn(a, b):
    return _dot(a, b, ((1,), (0,)))


def _dot_nt(a, b):
    return _dot(a, b, ((1,), (1,)))


def _dot_tn(a, b):
    return _dot(a, b, ((0,), (0,)))


def _matmul(a, b, mode, out_dtype, name, resid=None, carry=None, tm=512, tn=1408, tk=1024):
    if mode == "nn":
        (M, K), (K2, N) = a.shape, b.shape
    elif mode == "nt":
        (M, K), (N, K2) = a.shape, b.shape
    else:
        (K, M), (K2, N) = a.shape, b.shape
    assert K == K2, (a.shape, b.shape, mode)
    tm, tn, tk = _pick(M, tm), _pick(N, tn), _pick(K, tk)
    nk = K // tk
    gm, gn = M // tm, N // tn
    dims = {"nn": ((1,), (0,)), "nt": ((1,), (1,)), "tn": ((0,), (0,))}[mode]
    has_resid = resid is not None
    nci = len(carry.arrays) if carry else 0
    nco = len(carry.out_shapes) if carry else 0

    def body(a_ref, b_ref, *rest):
        r_ref = rest[0] if has_resid else None
        rest = rest[has_resid:]
        c_ins, o_ref, c_outs, scratch = rest[:nci], rest[nci], rest[nci + 1:nci + 1 + nco], rest[nci + 1 + nco:]
        acc_ref = scratch[0] if nk > 1 else None
        sems = scratch[nk > 1:]
        i, j, k = pl.program_id(0), pl.program_id(1), pl.program_id(2)
        if carry:
            @pl.when((i == 0) & (j == 0) & (k == 0))
            def _():
                carry.start(c_ins, c_outs, sems)

        if nk == 1:
            r = _dot(a_ref[...], b_ref[...], dims)
            if has_resid:
                r = r + r_ref[...]
            o_ref[...] = r.astype(o_ref.dtype)
        else:
            @pl.when(k == 0)
            def _():
                acc_ref[...] = jnp.zeros_like(acc_ref)

            acc_ref[...] += _dot(a_ref[...], b_ref[...], dims)

            @pl.when(k == nk - 1)
            def _():
                r = acc_ref[...]
                if has_resid:
                    r = r + r_ref[...]
                o_ref[...] = r.astype(o_ref.dtype)

        if carry:
            @pl.when((i == gm - 1) & (j == gn - 1) & (k == nk - 1))
            def _():
                carry.wait(c_ins, c_outs, sems)

    if mode == "nn":
        a_spec = pl.BlockSpec((tm, tk), lambda i, j, k: (i, k))
        b_spec = pl.BlockSpec((tk, tn), lambda i, j, k: (k, j))
    elif mode == "nt":
        a_spec = pl.BlockSpec((tm, tk), lambda i, j, k: (i, k))
        b_spec = pl.BlockSpec((tn, tk), lambda i, j, k: (j, k))
    else:
        a_spec = pl.BlockSpec((tk, tm), lambda i, j, k: (k, i))
        b_spec = pl.BlockSpec((tk, tn), lambda i, j, k: (k, j))
    o_spec = pl.BlockSpec((tm, tn), lambda i, j, k: (i, j))
    anyspec = pl.BlockSpec(memory_space=pl.ANY)
    in_specs = [a_spec, b_spec] + ([o_spec] if has_resid else []) + [anyspec] * nci
    args = (a, b) + ((resid,) if has_resid else ()) + (tuple(carry.arrays) if carry else ())
    out_shape = [jax.ShapeDtypeStruct((M, N), out_dtype)] + (carry.out_shapes if carry else [])
    scratch = ([] if nk == 1 else [pltpu.VMEM((tm, tn), F32)]) + (carry.scratch if carry else [])
    sem = ("arbitrary",) * 3 if carry else ("parallel", "parallel", "arbitrary")
    outs = pl.pallas_call(
        body, grid=(gm, gn, nk), in_specs=in_specs, out_specs=[o_spec] + [anyspec] * nco,
        out_shape=out_shape, scratch_shapes=scratch, compiler_params=_cp(*sem), name=name)(*args)
    return (outs[0], outs[1:]) if carry else outs[0]


def _rms_fwd(x, w, name, add=None):
    S, D = x.shape
    t = _pick(S, 512, 8)
    has_add = add is not None

    def body(x_ref, w_ref, *rest):
        xv = x_ref[...]
        if has_add:
            xv = xv + rest[0][...]
            rest[1][...] = xv
        r = lax.rsqrt(jnp.mean(xv * xv, axis=-1, keepdims=True) + NORM_EPS)
        rest[-1][...] = (xv * r * w_ref[...]).astype(rest[-1].dtype)

    row = pl.BlockSpec((t, D), lambda i: (i, 0))
    vec = pl.BlockSpec((1, D), lambda i: (0, 0))
    normed = jax.ShapeDtypeStruct((S, D), BF16)
    if not has_add:
        return pl.pallas_call(body, grid=(S // t,), in_specs=[row, vec], out_specs=row, out_shape=normed,
                              compiler_params=_cp("parallel"), name=name)(x, w.reshape(1, D))
    return pl.pallas_call(body, grid=(S // t,), in_specs=[row, vec, row], out_specs=[row, row],
                          out_shape=[jax.ShapeDtypeStruct((S, D), F32), normed],
                          compiler_params=_cp("parallel"), name=name)(x, w.reshape(1, D), add)


def _rms_bwd(x, w, dh, dres, name):
    S, D = x.shape
    t = _pick(S, 512, 8)

    def body(x_ref, w_ref, dh_ref, dr_ref, dx_ref, dw_ref):
        @pl.when(pl.program_id(0) == 0)
        def _():
            dw_ref[...] = jnp.zeros_like(dw_ref)

        xv = x_ref[...]
        r = lax.rsqrt(jnp.mean(xv * xv, axis=-1, keepdims=True) + NORM_EPS)
        xh = xv * r
        dh_v = dh_ref[...]
        g = dh_v * w_ref[...]
        dx_ref[...] = dr_ref[...] + r * (g - xh * jnp.mean(g * xh, axis=-1, keepdims=True))
        dw_ref[...] += jnp.sum(dh_v * xh, axis=0, keepdims=True)

    row = pl.BlockSpec((t, D), lambda i: (i, 0))
    vec = pl.BlockSpec((1, D), lambda i: (0, 0))
    return pl.pallas_call(
        body, grid=(S // t,), in_specs=[row, vec, row, row], out_specs=[row, vec],
        out_shape=[jax.ShapeDtypeStruct((S, D), F32), jax.ShapeDtypeStruct((1, D), F32)],
        compiler_params=_cp("arbitrary"), name=name)(x, w.reshape(1, D), dh, dres)


def _loss_head(x, w, tgt, name):
    S, D = x.shape
    t = _pick(S, 512, 8)

    def body(x_ref, w_ref, t_ref, dx_ref, dw_ref, l_ref):
        @pl.when(pl.program_id(0) == 0)
        def _():
            dw_ref[...] = jnp.zeros_like(dw_ref)
            l_ref[...] = jnp.zeros_like(l_ref)

        xv = x_ref[...]
        wv = w_ref[...]
        r = lax.rsqrt(jnp.mean(xv * xv, axis=-1, keepdims=True) + NORM_EPS)
        xh = xv * r
        err = xh * wv - t_ref[...]
        per_tok = jnp.mean(err * err, axis=-1, keepdims=True)
        l_ref[...] += 0.5 * jnp.sum(per_tok, axis=0, keepdims=True)
        dy = err * (1.0 / D)
        g = dy * wv
        dx_ref[...] = r * (g - xh * jnp.mean(g * xh, axis=-1, keepdims=True))
        dw_ref[...] += jnp.sum(dy * xh, axis=0, keepdims=True)

    row = pl.BlockSpec((t, D), lambda i: (i, 0))
    vec = pl.BlockSpec((1, D), lambda i: (0, 0))
    lspec = pl.BlockSpec((1, 128), lambda i: (0, 0))
    return pl.pallas_call(
        body, grid=(S // t,), in_specs=[row, vec, row], out_specs=[row, vec, lspec],
        out_shape=[jax.ShapeDtypeStruct((S, D), F32), jax.ShapeDtypeStruct((1, D), F32),
                   jax.ShapeDtypeStruct((1, 128), F32)],
        compiler_params=_cp("arbitrary"), name=name)(x, w.reshape(1, D), tgt)


def _rope_table(seq):
    pos = jnp.arange(seq, dtype=F32)
    inv_freq = ROPE_THETA ** (-jnp.arange(0, 2 * ROPE_HALF, 2, dtype=F32) / (2 * ROPE_HALF))
    ang = pos[:, None] * inv_freq[None, :]
    cos, sin = jnp.cos(ang), jnp.sin(ang)
    pad = HEAD_DIM - 2 * ROPE_HALF
    cos_p = jnp.concatenate([cos, cos, jnp.ones((seq, pad), F32)], axis=1)
    sin_a = jnp.concatenate([-sin, jnp.zeros((seq, HEAD_DIM - ROPE_HALF), F32)], axis=1)
    sin_b = jnp.concatenate([jnp.zeros((seq, ROPE_HALF), F32), sin, jnp.zeros((seq, pad), F32)], axis=1)
    return jnp.concatenate([cos_p, sin_a, sin_b], axis=1)


def _rope(t, tab, sign):
    cos_p = tab[:, 0:HEAD_DIM]
    sin_a = tab[:, HEAD_DIM:2 * HEAD_DIM]
    sin_b = tab[:, 2 * HEAD_DIM:3 * HEAD_DIM]
    up = pltpu.roll(t, HEAD_DIM - ROPE_HALF, 1)
    down = pltpu.roll(t, ROPE_HALF, 1)
    return t * cos_p + sign * (up * sin_a + down * sin_b)


def _perm_tokens(a):
    S = a.shape[0]
    return a.reshape(S // PERM, PERM, -1).transpose(1, 0, 2).reshape(S, -1)


def _unperm_tokens(a):
    S = a.shape[0]
    return a.reshape(PERM, S // PERM, -1).transpose(1, 0, 2).reshape(S, -1)


class _Strided:
    def __init__(self, S, dil):
        self.dil, self.m = dil, PERM // dil
        self.c = ATTN_BLOCK // self.m
        self.rows = S // PERM
        self.nb = S // (dil * ATTN_BLOCK)

    def view(self, a):
        return a.reshape(self.m, self.dil, self.rows, a.shape[-1])

    def spec(self, width, col, f=lambda n: n):
        return pl.BlockSpec((self.m, None, self.c, width), lambda r, n: (0, r, f(n), col))

    def load(self, ref, sl=slice(None)):
        if self.m == 1:
            return ref[0, :, sl]
        return jnp.concatenate([ref[q, :, sl] for q in range(self.m)], axis=0)

    def store(self, ref, sl, val):
        for q in range(self.m):
            ref[q, :, sl] = val[q * self.c:(q + 1) * self.c, :]

    def member(self, i):
        shift = self.c.bit_length() - 1
        return (i & (self.c - 1)) * self.m + (i >> shift)


def _attn_fwd(qkv, tab, g, heads, name):
    S = qkv.shape[0]
    W = heads * HEAD_DIM
    dil = ATTN_DILATIONS[g]
    steps = ATTN_WINDOWS[g] // dil
    B = ATTN_BLOCK
    scale = HEAD_DIM ** -0.5
    st = _Strided(S, dil)

    def body(q_ref, k_ref, kp_ref, v_ref, vp_ref, t_ref, tp_ref, o_ref, l_ref):
        n = pl.program_id(1)
        ii = lax.broadcasted_iota(jnp.int32, (B, 2 * B), 0)
        jj = lax.broadcasted_iota(jnp.int32, (B, 2 * B), 1)
        delta = st.member(ii) - st.member(jj & (B - 1)) + jnp.where(jj >= B, 0, B)
        ok = (delta >= 0) & (delta <= steps) & ((jj >= B) | (n > 0))
        tb = st.load(t_ref)
        tpv = st.load(tp_ref)
        for h in range(heads):
            sl = slice(h * HEAD_DIM, (h + 1) * HEAD_DIM)
            q = _rope(st.load(q_ref, sl), tb, 1.0)
            kc = jnp.concatenate([_rope(st.load(kp_ref, sl), tpv, 1.0), _rope(st.load(k_ref, sl), tb, 1.0)], axis=0)
            vc = jnp.concatenate([st.load(vp_ref, sl), st.load(v_ref, sl)], axis=0)
            s = jnp.where(ok, _dot_nt(q, kc) * scale, NEG)
            m = jnp.max(s, axis=-1, keepdims=True)
            p = jnp.exp(s - m)
            den = jnp.sum(p, axis=-1, keepdims=True)
            st.store(o_ref, sl, _dot_nn(p, vc) / den)
            st.store(l_ref, sl, jnp.broadcast_to(m + jnp.log(den), (B, HEAD_DIM)))

    prv = lambda n: jnp.maximum(n - 1, 0)
    qv, tv = st.view(qkv), st.view(tab)
    o_spec = st.spec(W, 0)
    o, lse = pl.pallas_call(
        body, grid=(dil, st.nb),
        in_specs=[st.spec(W, g * 3), st.spec(W, g * 3 + 1), st.spec(W, g * 3 + 1, prv),
                  st.spec(W, g * 3 + 2), st.spec(W, g * 3 + 2, prv),
                  st.spec(3 * HEAD_DIM, 0), st.spec(3 * HEAD_DIM, 0, prv)],
        out_specs=[o_spec, o_spec],
        out_shape=[jax.ShapeDtypeStruct((st.m, dil, st.rows, W), F32)] * 2,
        compiler_params=_cp("parallel", "parallel"), name=name)(qv, qv, qv, qv, qv, tv, tv)
    return o.reshape(S, W), lse.reshape(S, W)


def _attn_combine(os_, ls_, name):
    S, W = os_[0].shape
    t = _pick(S, 256, 8)

    def body(o0, o1, o2, l0, l1, l2, o_ref, l_ref):
        a, b, c = l0[...], l1[...], l2[...]
        m = jnp.maximum(jnp.maximum(a, b), c)
        ea, eb, ec = jnp.exp(a - m), jnp.exp(b - m), jnp.exp(c - m)
        tot = ea + eb + ec
        o_ref[...] = (ea * o0[...] + eb * o1[...] + ec * o2[...]) / tot
        l_ref[...] = m + jnp.log(tot)

    row = pl.BlockSpec((t, W), lambda i: (i, 0))
    return pl.pallas_call(body, grid=(S // t,), in_specs=[row] * 6, out_specs=[row, row],
                          out_shape=[jax.ShapeDtypeStruct((S, W), F32)] * 2,
                          compiler_params=_cp("parallel"), name=name)(*os_, *ls_)


def _attn_bwd(qkv, tab, o, lse, do, dqkv_prev, g, heads, name):
    S = qkv.shape[0]
    W = heads * HEAD_DIM
    dil = ATTN_DILATIONS[g]
    steps = ATTN_WINDOWS[g] // dil
    B = ATTN_BLOCK
    scale = HEAD_DIM ** -0.5
    st = _Strided(S, dil)
    nb = st.nb
    aliased = dqkv_prev is not None

    def body(q_ref, qn_ref, k_ref, kp_ref, v_ref, vp_ref, do_ref, don_ref, o_ref, on_ref,
             l_ref, ln_ref, t_ref, tp_ref, tn_ref, *rest):
        out_ref = rest[-1]
        n = pl.program_id(1)
        has_next = n < nb - 1
        ia = lax.broadcasted_iota(jnp.int32, (B, 2 * B), 0)
        ja = lax.broadcasted_iota(jnp.int32, (B, 2 * B), 1)
        da = st.member(ia) - st.member(ja & (B - 1)) + jnp.where(ja >= B, 0, B)
        ok_a = (da >= 0) & (da <= steps) & ((ja >= B) | (n > 0))
        ib = lax.broadcasted_iota(jnp.int32, (2 * B, B), 0)
        jb = lax.broadcasted_iota(jnp.int32, (2 * B, B), 1)
        db = st.member(ib & (B - 1)) + jnp.where(ib >= B, B, 0) - st.member(jb)
        ok_b = (db >= 0) & (db <= steps) & ((ib < B) | has_next)
        tb, tpv, tnv = st.load(t_ref), st.load(tp_ref), st.load(tn_ref)
        for h in range(heads):
            sl = slice(h * HEAD_DIM, (h + 1) * HEAD_DIM)
            qr = _rope(st.load(q_ref, sl), tb, 1.0)
            qnr = _rope(st.load(qn_ref, sl), tnv, 1.0)
            kr = _rope(st.load(k_ref, sl), tb, 1.0)
            kpr = _rope(st.load(kp_ref, sl), tpv, 1.0)
            v = st.load(v_ref, sl)
            dov_ = st.load(do_ref, sl)
            donv = st.load(don_ref, sl)
            dl = jnp.sum(dov_ * st.load(o_ref, sl), axis=-1, keepdims=True)
            dln = jnp.sum(donv * st.load(on_ref, sl), axis=-1, keepdims=True)
            ls = st.load(l_ref, sl)
            kc = jnp.concatenate([kpr, kr], axis=0)
            vc = jnp.concatenate([st.load(vp_ref, sl), v], axis=0)
            s = _dot_nt(qr, kc) * scale
            p = jnp.where(ok_a, jnp.exp(jnp.minimum(s - jnp.concatenate([ls, ls], axis=1), 30.0)), 0.0)
            ds = p * (_dot_nt(dov_, vc) - dl) * scale
            st.store(out_ref, sl, _rope(_dot_nn(ds, kc), tb, -1.0))
            qc = jnp.concatenate([qr, qnr], axis=0)
            doc = jnp.concatenate([dov_, donv], axis=0)
            lc = jnp.concatenate([ls, st.load(ln_ref, sl)], axis=0)
            dlc = jnp.concatenate([dl, dln], axis=0)
            s2 = _dot_nt(qc, kr) * scale
            p2 = jnp.where(ok_b, jnp.exp(jnp.minimum(s2 - lc, 30.0)), 0.0)
            ds2 = p2 * (_dot_nt(doc, v) - dlc) * scale
            st.store(out_ref, slice(W + h * HEAD_DIM, W + (h + 1) * HEAD_DIM), _rope(_dot_tn(ds2, qc), tb, -1.0))
            st.store(out_ref, slice(2 * W + h * HEAD_DIM, 2 * W + (h + 1) * HEAD_DIM), _dot_tn(p2, doc))

    nxt = lambda n: jnp.minimum(n + 1, nb - 1)
    prv = lambda n: jnp.maximum(n - 1, 0)
    same = lambda n: n
    q0, q1, q2, tw = g * 3, g * 3 + 1, g * 3 + 2, 3 * HEAD_DIM
    in_specs = [st.spec(W, q0), st.spec(W, q0, nxt), st.spec(W, q1), st.spec(W, q1, prv),
                st.spec(W, q2), st.spec(W, q2, prv)]
    in_specs += [st.spec(W, 0, f) for f in (same, nxt, same, nxt, same, nxt)]
    in_specs += [st.spec(tw, 0, f) for f in (same, prv, nxt)]
    qv, tv, ov, lv, dov = (st.view(a) for a in (qkv, tab, o, lse, do))
    args = [qv, qv, qv, qv, qv, qv, dov, dov, ov, ov, lv, lv, tv, tv, tv]
    kwargs = {}
    if aliased:
        in_specs.append(pl.BlockSpec(memory_space=pl.ANY))
        args.append(st.view(dqkv_prev))
        kwargs["input_output_aliases"] = {len(args) - 1: 0}
    out = pl.pallas_call(
        body, grid=(dil, nb), in_specs=in_specs, out_specs=st.spec(3 * W, g),
        out_shape=jax.ShapeDtypeStruct((st.m, dil, st.rows, 9 * W), F32),
        compiler_params=_cp("parallel", "parallel"), name=name, **kwargs)(*args)
    return out.reshape(S, 9 * W)


def _shift_down(x, halo, s):
    if s == 0:
        return x
    T = x.shape[0]
    xs = pltpu.roll(x, s, 0)
    hs = pltpu.roll(halo, s, 0)
    row8 = lax.broadcasted_iota(jnp.int32, hs.shape, 0)
    top = jnp.where(row8 < s, hs, xs[0:8])
    return jnp.concatenate([top, xs[8:T]], axis=0)


def _shift_up(x, halo, s):
    if s == 0:
        return x
    T = x.shape[0]
    xs = pltpu.roll(x, T - s, 0)
    hs = pltpu.roll(halo, 8 - s, 0)
    row8 = lax.broadcasted_iota(jnp.int32, hs.shape, 0)
    bot = jnp.where(row8 >= 8 - s, hs, xs[T - 8:T])
    return jnp.concatenate([xs[0:T - 8], bot], axis=0)


CONV_ROWS = 128
CONV_LANES = 512


def _conv_apply(x, halo, w_ref, wsl, b, K):
    acc = x * w_ref[K - 1, :, wsl] + b
    for s in range(1, K):
        acc = acc + _shift_down(x, halo, s) * w_ref[K - 1 - s, :, wsl]
    return acc


def _conv_accum(dy, dyn, xv, xp, w_ref, dw_ref, db_ref, wsl, K):
    acc = dy * w_ref[K - 1, :, wsl]
    dw_ref[K - 1, :, wsl] += jnp.sum(dy * xv, axis=0, keepdims=True)
    for s in range(1, K):
        acc = acc + _shift_up(dy, dyn, s) * w_ref[K - 1 - s, :, wsl]
        dw_ref[K - 1 - s, :, wsl] += jnp.sum(dy * _shift_down(xv, xp, s), axis=0, keepdims=True)
    db_ref[:, wsl] += jnp.sum(dy, axis=0, keepdims=True)
    return acc


def _row_specs(T, S, width):
    main = pl.BlockSpec((T, width), lambda i: (i, 0))
    prev = pl.BlockSpec((8, width), lambda i: (jnp.maximum(i * (T // 8) - 1, 0), 0))
    nxt = pl.BlockSpec((8, width), lambda i: (jnp.minimum((i + 1) * (T // 8), S // 8 - 1), 0))
    return main, prev, nxt


def _full(shape):
    return pl.BlockSpec(shape, lambda i: (0,) * len(shape))


def _silu_grad(y):
    sg = _sigmoid(y)
    return sg * (1.0 + y * (1.0 - sg))


def _ssm_conv_fwd(zx, w, b, d_inner, conv_dim, name):
    S, wz = zx.shape
    K = w.shape[0]
    T = _pick(S, CONV_ROWS, 8)
    cw = _pick(conv_dim, CONV_LANES)

    def body(x_ref, h_ref, w_ref, b_ref, c_ref):
        has_prev = pl.program_id(0) > 0
        for cs in range(0, conv_dim, cw):
            so, sx = slice(cs, cs + cw), slice(d_inner + cs, d_inner + cs + cw)
            halo = jnp.where(has_prev, h_ref[:, sx], 0.0)
            c_ref[:, so] = _conv_apply(x_ref[:, sx], halo, w_ref, so, b_ref[:, so], K)

    main, prev, _ = _row_specs(T, S, wz)
    return pl.pallas_call(
        body, grid=(S // T,), in_specs=[main, prev, _full((K, 1, conv_dim)), _full((1, conv_dim))],
        out_specs=pl.BlockSpec((T, conv_dim), lambda i: (i, 0)),
        out_shape=jax.ShapeDtypeStruct((S, conv_dim), F32),
        compiler_params=_cp("parallel"), name=name)(zx, zx, w.reshape(K, 1, conv_dim), b.reshape(1, conv_dim))


def _ssm_conv_bwd(dxs, dbm, dcm, conv, zx, dz, ddt, w, d_inner, name):
    S, wz = zx.shape
    K, conv_dim = w.shape
    gn = dbm.shape[1]
    T = _pick(S, CONV_ROWS, 8)
    cw = _pick(math.gcd(d_inner, gn), CONV_LANES)
    nrow = S // T
    tail = wz - d_inner - conv_dim
    assert ddt.shape[1] == tail

    def body(dx_ref, dxn_ref, db_ref_, dbn_ref, dc_ref, dcn_ref, y_ref, yn_ref, x_ref, xp_ref, dz_ref, ddt_ref,
             w_ref, o_ref, dw_ref, dbias_ref):
        i = pl.program_id(0)

        @pl.when(i == 0)
        def _():
            dw_ref[...] = jnp.zeros_like(dw_ref)
            dbias_ref[...] = jnp.zeros_like(dbias_ref)

        has_prev, has_next = i > 0, i < nrow - 1
        for cs in range(0, d_inner, cw):
            o_ref[:, cs:cs + cw] = dz_ref[:, cs:cs + cw]
        o_ref[:, d_inner + conv_dim:wz] = ddt_ref[...]
        for cs in range(0, conv_dim, cw):
            so, sx = slice(cs, cs + cw), slice(d_inner + cs, d_inner + cs + cw)
            if cs < d_inner:
                src, srcn, ss = dx_ref, dxn_ref, slice(cs, cs + cw)
            elif cs < d_inner + gn:
                src, srcn, ss = db_ref_, dbn_ref, slice(cs - d_inner, cs - d_inner + cw)
            else:
                src, srcn, ss = dc_ref, dcn_ref, slice(cs - d_inner - gn, cs - d_inner - gn + cw)
            dy = src[:, ss] * _silu_grad(y_ref[:, so])
            dyn = jnp.where(has_next, srcn[:, ss] * _silu_grad(yn_ref[:, so]), 0.0)
            xp = jnp.where(has_prev, xp_ref[:, sx], 0.0)
            o_ref[:, sx] = _conv_accum(dy, dyn, x_ref[:, sx], xp, w_ref, dw_ref, dbias_ref, so, K)

    xm, _, xn = _row_specs(T, S, d_inner)
    gm, _, gnx = _row_specs(T, S, gn)
    cm, _, cn = _row_specs(T, S, conv_dim)
    zm, zp, _ = _row_specs(T, S, wz)
    tm_, _, _ = _row_specs(T, S, tail)
    dzx, dw, db = pl.pallas_call(
        body, grid=(nrow,),
        in_specs=[xm, xn, gm, gnx, gm, gnx, cm, cn, zm, zp, xm, tm_, _full((K, 1, conv_dim))],
        out_specs=[zm, _full((K, 1, conv_dim)), _full((1, conv_dim))],
        out_shape=[jax.ShapeDtypeStruct((S, wz), F32), jax.ShapeDtypeStruct((K, 1, conv_dim), F32),
                   jax.ShapeDtypeStruct((1, conv_dim), F32)],
        compiler_params=_cp("arbitrary"), name=name)(
            dxs, dxs, dbm, dbm, dcm, dcm, conv, conv, zx, zx, dz, ddt, w.reshape(K, 1, conv_dim))
    return dzx, dw.reshape(K, conv_dim), db


def _ffn_conv_fwd(up, w, b, name):
    S, C = up.shape
    F = C // 2
    K = w.shape[0]
    T = _pick(S, CONV_ROWS, 8)
    cw = _pick(F, CONV_LANES)

    def body(x_ref, h_ref, w_ref, b_ref, u_ref, a_ref):
        has_prev = pl.program_id(0) > 0
        for cs in range(0, F, cw):
            sg, su = slice(cs, cs + cw), slice(F + cs, F + cs + cw)
            gate = _conv_apply(x_ref[:, sg], jnp.where(has_prev, h_ref[:, sg], 0.0), w_ref, sg, b_ref[:, sg], K)
            upv = _conv_apply(x_ref[:, su], jnp.where(has_prev, h_ref[:, su], 0.0), w_ref, su, b_ref[:, su], K)
            u_ref[0, :, sg] = gate
            u_ref[1, :, sg] = upv
            a_ref[:, sg] = (gate * _sigmoid(gate) * upv).astype(a_ref.dtype)

    main, prev, _ = _row_specs(T, S, C)
    return pl.pallas_call(
        body, grid=(S // T,), in_specs=[main, prev, _full((K, 1, C)), _full((1, C))],
        out_specs=[pl.BlockSpec((2, T, F), lambda i: (0, i, 0)), pl.BlockSpec((T, F), lambda i: (i, 0))],
        out_shape=[jax.ShapeDtypeStruct((2, S, F), F32), jax.ShapeDtypeStruct((S, F), BF16)],
        compiler_params=_cp("parallel"), name=name)(up, up, w.reshape(K, 1, C), b.reshape(1, C))


def _ffn_conv_bwd(u2, dact, up, w, name):
    S, C = up.shape
    F = C // 2
    K = w.shape[0]
    T = _pick(S, CONV_ROWS, 8)
    cw = _pick(F, CONV_LANES)
    nrow = S // T

    def du(half, gate, upv, d):
        sg = _sigmoid(gate)
        return d * upv * sg * (1.0 + gate * (1.0 - sg)) if half == 0 else d * gate * sg

    def body(u_ref, un_ref, d_ref, dn_ref, x_ref, xp_ref, w_ref, dx_ref, dw_ref, db_ref):
        i = pl.program_id(0)

        @pl.when(i == 0)
        def _():
            dw_ref[...] = jnp.zeros_like(dw_ref)
            db_ref[...] = jnp.zeros_like(db_ref)

        has_prev, has_next = i > 0, i < nrow - 1
        for half in range(2):
            for cs in range(0, F, cw):
                sf, sc = slice(cs, cs + cw), slice(half * F + cs, half * F + cs + cw)
                dy = du(half, u_ref[0, :, sf], u_ref[1, :, sf], d_ref[:, sf])
                dyn = jnp.where(has_next, du(half, un_ref[0, :, sf], un_ref[1, :, sf], dn_ref[:, sf]), 0.0)
                xp = jnp.where(has_prev, xp_ref[:, sc], 0.0)
                dx_ref[:, sc] = _conv_accum(dy, dyn, x_ref[:, sc], xp, w_ref, dw_ref, db_ref, sc, K)

    am, _, an = _row_specs(T, S, F)
    xm, xp_, _ = _row_specs(T, S, C)
    u_main = pl.BlockSpec((2, T, F), lambda i: (0, i, 0))
    u_next = pl.BlockSpec((2, 8, F), lambda i: (0, jnp.minimum((i + 1) * (T // 8), S // 8 - 1), 0))
    dx, dw, db = pl.pallas_call(
        body, grid=(nrow,), in_specs=[u_main, u_next, am, an, xm, xp_, _full((K, 1, C))],
        out_specs=[xm, _full((K, 1, C)), _full((1, C))],
        out_shape=[jax.ShapeDtypeStruct((S, C), F32), jax.ShapeDtypeStruct((K, 1, C), F32),
                   jax.ShapeDtypeStruct((1, C), F32)],
        compiler_params=_cp("arbitrary"), name=name)(u2, u2, dact, dact, up, up, w.reshape(K, 1, C))
    return dx, dw.reshape(K, C), db


def _cumsum_rows(v):
    n = v.shape[0]
    row = lax.broadcasted_iota(jnp.int32, v.shape, 0)
    k = 1
    while k < n:
        v = v + jnp.where(row >= k, pltpu.roll(v, k, 0), 0.0)
        k *= 2
    return v


def _rev_cumsum_rows(v):
    n = v.shape[0]
    row = lax.broadcasted_iota(jnp.int32, v.shape, 0)
    k = 1
    while k < n:
        v = v + jnp.where(row < n - k, pltpu.roll(v, n - k, 0), 0.0)
        k *= 2
    return v


def _ssd_common(x_ref, dtr_ref, bias_ref, alog_ref, gw):
    Q = SSM_CHUNK
    X = _silu(x_ref[...])
    pre = dtr_ref[...] + bias_ref[...]
    dt = _softplus(pre)
    a = -jnp.exp(alog_ref[...])
    cs = _cumsum_rows(dt * a)
    row = lax.broadcasted_iota(jnp.int32, (Q, gw), 0)
    cs_last = jnp.sum(jnp.where(row == Q - 1, cs, 0.0), axis=0, keepdims=True)
    return X, pre, dt, a, cs, cs_last, row


def _head_decay(cs, head_mask):
    Q = SSM_CHUNK
    col = jnp.max(jnp.where(head_mask, cs, NEG), axis=1, keepdims=True)
    acol = jnp.broadcast_to(col, (Q, Q))
    arow = acol.T
    ii = lax.broadcasted_iota(jnp.int32, (Q, Q), 0)
    jj = lax.broadcasted_iota(jnp.int32, (Q, Q), 1)
    tril = ii >= jj
    return jnp.where(tril, jnp.exp(jnp.where(tril, acol - arow, 0.0)), 0.0), tril


def _ssd_specs(S, d_inner, gw, nc, rev):
    Q, N, G = SSM_CHUNK, SSM_STATE, SSM_GROUPS
    ch = (lambda c: nc - 1 - c) if rev else (lambda c: c)
    x_spec = pl.BlockSpec((Q, gw), lambda g, c: (ch(c), g))
    b_spec = pl.BlockSpec((Q, N), lambda g, c: (ch(c), d_inner // N + g))
    c_spec = pl.BlockSpec((Q, N), lambda g, c: (ch(c), d_inner // N + G + g))
    p_spec = pl.BlockSpec((None, 1, gw), lambda g, c: (g, 0, 0))
    s_spec = pl.BlockSpec((None, None, gw, N), lambda g, c: (ch(c), g, 0, 0))
    return x_spec, b_spec, c_spec, p_spec, s_spec


def _ssd_fwd(xbc, dtr, bias, alog, dsk, d_inner, name):
    S = xbc.shape[0]
    Q, N, G, P = SSM_CHUNK, SSM_STATE, SSM_GROUPS, SSM_HEAD_DIM
    gw = d_inner // G
    R = gw // P
    nc = S // Q

    def body(x_ref, b_ref, c_ref, dtr_ref, bias_ref, alog_ref, d_ref, y_ref, sp_ref, s_scr):
        @pl.when(pl.program_id(1) == 0)
        def _():
            s_scr[...] = jnp.zeros_like(s_scr)

        X, _, dt, a, cs, cs_last, row = _ssd_common(x_ref, dtr_ref, bias_ref, alog_ref, gw)
        Bm, Cm = _silu(b_ref[...]), _silu(c_ref[...])
        xdt = X * dt
        lane = lax.broadcasted_iota(jnp.int32, (Q, gw), 1)
        sprev = s_scr[...]
        sp_ref[...] = sprev
        cb = _dot_nt(Cm, Bm)
        y = jnp.exp(cs) * _dot_nt(Cm, sprev)
        for r in range(R):
            hm = (lane >= r * P) & (lane < (r + 1) * P)
            dec_l, _ = _head_decay(cs, hm)
            y = y + _dot_nn(cb * dec_l, jnp.where(hm, xdt, 0.0))
        dec = jnp.exp(cs_last - cs)
        cd = jnp.exp(jnp.broadcast_to(cs_last, (Q, gw)).T)
        s_scr[...] = sprev * cd + _dot_tn(xdt * dec, Bm)
        y_ref[...] = y + d_ref[...] * X

    x_spec, b_spec, c_spec, p_spec, s_spec = _ssd_specs(S, d_inner, gw, nc, False)
    return pl.pallas_call(
        body, grid=(G, nc), in_specs=[x_spec, b_spec, c_spec, x_spec, p_spec, p_spec, p_spec],
        out_specs=[x_spec, s_spec],
        out_shape=[jax.ShapeDtypeStruct((S, d_inner), F32), jax.ShapeDtypeStruct((nc, G, gw, N), F32)],
        scratch_shapes=[pltpu.VMEM((gw, N), F32)],
        compiler_params=_cp("parallel", "arbitrary"), name=name)(xbc, xbc, xbc, dtr, bias, alog, dsk)


def _ssd_bwd(xbc, dtr, bias, alog, dsk, sprev_all, dy, d_inner, name):
    S = xbc.shape[0]
    Q, N, G, P = SSM_CHUNK, SSM_STATE, SSM_GROUPS, SSM_HEAD_DIM
    gw = d_inner // G
    R = gw // P
    nc = S // Q

    def body(x_ref, b_ref, c_ref, dtr_ref, bias_ref, alog_ref, d_ref, sp_ref, dy_ref,
             dx_ref, db_ref, dc_ref, ddt_ref, dbias_ref, dalog_ref, dd_ref, ds_scr):
        @pl.when(pl.program_id(1) == 0)
        def _():
            ds_scr[...] = jnp.zeros_like(ds_scr)
            dbias_ref[...] = jnp.zeros_like(dbias_ref)
            dalog_ref[...] = jnp.zeros_like(dalog_ref)
            dd_ref[...] = jnp.zeros_like(dd_ref)

        X, pre, dt, a, cs, cs_last, row = _ssd_common(x_ref, dtr_ref, bias_ref, alog_ref, gw)
        Bm, Cm = _silu(b_ref[...]), _silu(c_ref[...])
        dY = dy_ref[...]
        sprev = sp_ref[...]
        dsn = ds_scr[...]
        xdt = X * dt
        lane = lax.broadcasted_iota(jnp.int32, (Q, gw), 1)
        lane1 = lax.broadcasted_iota(jnp.int32, (1, gw), 1)
        srow = lax.broadcasted_iota(jnp.int32, (gw, N), 0)
        ecs = jnp.exp(cs)
        dec = jnp.exp(cs_last - cs)
        cd = jnp.exp(jnp.broadcast_to(cs_last, (Q, gw)).T)
        dd_ref[...] += jnp.sum(dY * X, axis=0, keepdims=True)
        dX = d_ref[...] * dY
        ey = ecs * dY
        dcs = ey * _dot_nt(Cm, sprev)
        dC = _dot_nn(ey, sprev)
        ds_scr[...] = cd * dsn + _dot_tn(ey, Cm)
        wmat = _dot_nt(Bm, dsn)
        dxdt = dec * wmat
        xd = xdt * dec
        dB = _dot_nn(xd, dsn)
        ddec = xdt * wmat * dec
        dcs = dcs - ddec
        dlast = jnp.sum(ddec, axis=0, keepdims=True)
        qmat = dsn * sprev * cd
        cb = _dot_nt(Cm, Bm)
        dcb = jnp.zeros((Q, Q), F32)
        dcs_rep = jnp.zeros((Q, gw), F32)
        dtx_rep = jnp.zeros((Q, gw), F32)
        for r in range(R):
            hm = (lane >= r * P) & (lane < (r + 1) * P)
            dec_l, tril = _head_decay(cs, hm)
            dyr = jnp.where(hm, dY, 0.0)
            gmat = jnp.where(tril, _dot_nt(dyr, xdt), 0.0)
            dcb = dcb + gmat * dec_l
            e = gmat * cb * dec_l
            v = (jnp.sum(e, axis=1, keepdims=True) - jnp.sum(e.T, axis=1, keepdims=True)
                 + jnp.sum(jnp.where(hm, dcs, 0.0), axis=1, keepdims=True))
            dxdt = dxdt + _dot_tn(cb * dec_l, dyr)
            hm1 = (lane1 >= r * P) & (lane1 < (r + 1) * P)
            t_last = (jnp.sum(jnp.where(hm1, dlast, 0.0), axis=1, keepdims=True)
                      + jnp.sum(jnp.where((srow >= r * P) & (srow < (r + 1) * P), qmat, 0.0), keepdims=True))
            dcs_rep = dcs_rep + jnp.where(hm, v, 0.0) + jnp.where(hm & (row == Q - 1), t_last, 0.0)
        for r in range(R):
            hm = (lane >= r * P) & (lane < (r + 1) * P)
            w_r = jnp.sum(jnp.where(hm, dxdt * X, 0.0), axis=1, keepdims=True)
            dtx_rep = dtx_rep + jnp.where(hm, w_r, 0.0)
        dadt = _rev_cumsum_rows(dcs_rep)
        ddt = a * dadt + dtx_rep
        dalog_ref[...] += jnp.sum(dt * dadt, axis=0, keepdims=True) * a
        draw = ddt * _sigmoid(pre)
        ddt_ref[...] = draw
        dbias_ref[...] += jnp.sum(draw, axis=0, keepdims=True)
        dx_ref[...] = dX + dxdt * dt
        db_ref[...] = dB + _dot_tn(dcb, Cm)
        dc_ref[...] = dC + _dot_nn(dcb, Bm)

    x_spec, b_spec, c_spec, p_spec, s_spec = _ssd_specs(S, d_inner, gw, nc, True)
    n_spec = pl.BlockSpec((Q, N), lambda g, c: (nc - 1 - c, g))
    gshape = jax.ShapeDtypeStruct((G, 1, gw), F32)
    return pl.pallas_call(
        body, grid=(G, nc),
        in_specs=[x_spec, b_spec, c_spec, x_spec, p_spec, p_spec, p_spec, s_spec, x_spec],
        out_specs=[x_spec, n_spec, n_spec, x_spec, p_spec, p_spec, p_spec],
        out_shape=[jax.ShapeDtypeStruct((S, d_inner), F32), jax.ShapeDtypeStruct((S, G * N), F32),
                   jax.ShapeDtypeStruct((S, G * N), F32), jax.ShapeDtypeStruct((S, d_inner), F32),
                   gshape, gshape, gshape],
        scratch_shapes=[pltpu.VMEM((gw, N), F32)],
        compiler_params=_cp("parallel", "arbitrary"), name=name)(
            xbc, xbc, xbc, dtr, bias, alog, dsk, sprev_all, dy)


def _gnorm_fwd(y, zx, w, name):
    S, d_inner = y.shape
    G = SSM_GROUPS
    gw = d_inner // G
    T = _pick(S, 256, 8)

    def body(y_ref, z_ref, w_ref, o_ref):
        for k in range(G):
            sl = slice(k * gw, (k + 1) * gw)
            z = z_ref[:, sl]
            gk = y_ref[:, sl] * z * _sigmoid(z)
            r = lax.rsqrt(jnp.mean(gk * gk, axis=-1, keepdims=True) + NORM_EPS)
            o_ref[:, sl] = (gk * r * w_ref[:, sl]).astype(o_ref.dtype)

    row = pl.BlockSpec((T, d_inner), lambda i: (i, 0))
    vec = pl.BlockSpec((1, d_inner), lambda i: (0, 0))
    return pl.pallas_call(body, grid=(S // T,), in_specs=[row, row, vec], out_specs=row,
                          out_shape=jax.ShapeDtypeStruct((S, d_inner), BF16),
                          compiler_params=_cp("parallel"), name=name)(y, zx, w.reshape(1, d_inner))


def _gnorm_bwd(y, zx, w, dout, name):
    S, d_inner = y.shape
    G = SSM_GROUPS
    gw = d_inner // G
    T = _pick(S, 256, 8)

    def body(y_ref, z_ref, w_ref, d_ref, dy_ref, dz_ref, dw_ref):
        @pl.when(pl.program_id(0) == 0)
        def _():
            dw_ref[...] = jnp.zeros_like(dw_ref)

        for k in range(G):
            sl = slice(k * gw, (k + 1) * gw)
            z, yv, d = z_ref[:, sl], y_ref[:, sl], d_ref[:, sl]
            sg = _sigmoid(z)
            sz = z * sg
            gk = yv * sz
            r = lax.rsqrt(jnp.mean(gk * gk, axis=-1, keepdims=True) + NORM_EPS)
            gh = gk * r
            dw_ref[:, sl] += jnp.sum(d * gh, axis=0, keepdims=True)
            dg = d * w_ref[:, sl]
            dgk = r * (dg - gh * jnp.mean(dg * gh, axis=-1, keepdims=True))
            dy_ref[:, sl] = dgk * sz
            dz_ref[:, sl] = dgk * yv * sg * (1.0 + z * (1.0 - sg))

    row = pl.BlockSpec((T, d_inner), lambda i: (i, 0))
    vec = pl.BlockSpec((1, d_inner), lambda i: (0, 0))
    return pl.pallas_call(
        body, grid=(S // T,), in_specs=[row, row, vec, row], out_specs=[row, row, vec],
        out_shape=[jax.ShapeDtypeStruct((S, d_inner), F32)] * 2 + [jax.ShapeDtypeStruct((1, d_inner), F32)],
        compiler_params=_cp("arbitrary"), name=name)(y, zx, w.reshape(1, d_inner), dout)


def _adam_math(g, w, m, v):
    m = ADAM_B1 * m + (1.0 - ADAM_B1) * g
    v = ADAM_B2 * v + (1.0 - ADAM_B2) * (g * g)
    m_hat = m / (1.0 - ADAM_B1 ** ADAM_STEP)
    v_hat = v / (1.0 - ADAM_B2 ** ADAM_STEP)
    delta = -ADAM_LR * (m_hat / (jnp.sqrt(v_hat) + ADAM_EPS) + ADAM_WD * w)
    return delta, m, v


def _adamw_big(own, sib, w, m, v, layer, prev, name):
    L, A, Bc = w.shape
    T = _pick(A, max(8, (1 << 19) // (4 * Bc)), 16)

    def body(o_ref, s_ref, w_ref, m_ref, v_ref, *rest):
        g_ref, d_ref, nm_ref, nv_ref = rest[-4:]
        so = o_ref[0].astype(F32)
        ss = s_ref[0].astype(F32)
        for k in range(1, N_CHIPS):
            so = so + o_ref[k].astype(F32)
            ss = ss + s_ref[k].astype(F32)
        g = so + ss
        delta, nm, nv = _adam_math(g, w_ref[...], m_ref[...], v_ref[...])
        g_ref[...] = g
        d_ref[...] = delta
        nm_ref[...] = nm
        nv_ref[...] = nv

    part = pl.BlockSpec((N_CHIPS, T, Bc), lambda i: (0, i, 0))
    blk = pl.BlockSpec((None, T, Bc), lambda i: (layer, i, 0))
    shp = jax.ShapeDtypeStruct(w.shape, F32)
    in_specs, args, kwargs = [part, part, blk, blk, blk], [own, sib, w, m, v], {}
    if prev is not None:
        in_specs += [pl.BlockSpec(memory_space=pl.ANY)] * 4
        args += list(prev)
        kwargs["input_output_aliases"] = {5 + q: q for q in range(4)}
    return pl.pallas_call(body, grid=(A // T,), in_specs=in_specs, out_specs=[blk] * 4, out_shape=[shp] * 4,
                          compiler_params=_cp("parallel"), name=name, **kwargs)(*args)


def _sum_devices(parts, name):
    _, R, C = parts.shape

    def body(p_ref, o_ref):
        acc = p_ref[0]
        for k in range(1, N_DEV):
            acc = acc + p_ref[k]
        o_ref[...] = acc

    return pl.pallas_call(body, out_shape=jax.ShapeDtypeStruct((R, C), F32), name=name)(parts)


def _adamw_small(g, w, m, v, name):
    def body(g_ref, w_ref, m_ref, v_ref, d_ref, nm_ref, nv_ref):
        delta, nm, nv = _adam_math(g_ref[...], w_ref[...], m_ref[...], v_ref[...])
        d_ref[...] = delta
        nm_ref[...] = nm
        nv_ref[...] = nv

    shp = jax.ShapeDtypeStruct(g.shape, F32)
    return pl.pallas_call(body, out_shape=[shp] * 3, name=name)(g, w, m, v)


PACK_COLS = 1024


def _pack(arrs):
    flat = jnp.concatenate([a.reshape(-1).astype(F32) for a in arrs])
    n = flat.shape[0]
    rows = -(-n // (8 * PACK_COLS)) * 8
    return jnp.pad(flat, (0, rows * PACK_COLS - n)).reshape(rows, PACK_COLS)


def _unpack(packed, shapes):
    flat = packed.reshape(-1)
    out, off = [], 0
    for s in shapes:
        n = math.prod(s)
        out.append(flat[off:off + n].reshape(s))
        off += n
    return out


def _shard_ref(ref, kind, k, n):
    if kind == "col":
        return ref.at[:, pl.ds(pl.multiple_of(k * n, 128), n)]
    if kind == "row":
        return ref.at[pl.ds(pl.multiple_of(k * n, 16), n), :]
    return ref.at[k]


def _chip_peers():
    x, y, c = lax.axis_index("x"), lax.axis_index("y"), lax.axis_index("c")
    return x, y, c, [(1 - x, y), (x, 1 - y), (1 - x, 1 - y)]


class _Exchange:
    def __init__(self, mode, items):
        self.mode, self.items = mode, items
        self.arrays = []
        for it in items:
            if not any(it[0] is a for a in self.arrays):
                self.arrays.append(it[0])
        self.src_idx = [next(i for i, a in enumerate(self.arrays) if a is it[0]) for it in items]
        self.out_shapes = [jax.ShapeDtypeStruct(it[-1], it[0].dtype) for it in items]
        n = len(items)
        self.scratch = [pltpu.SemaphoreType.DMA((3 * n,)), pltpu.SemaphoreType.DMA((3 * n,)),
                        pltpu.SemaphoreType.DMA((n,))]

    def _copies(self, ins, outs, sems):
        send_sems, recv_sems, loc_sems = sems
        x, y, c, peers = _chip_peers()
        me = 2 * x + y
        local, sent, arriving = [], [], []
        for t, it in enumerate(self.items):
            src_arr = ins[self.src_idx[t]]
            if self.mode == "gather":
                _, layer, kind, n, _ = it
                src = src_arr if layer is None else src_arr.at[layer]
                src_for = lambda k: src
                dst_from = lambda k: _shard_ref(outs[t], kind, k, n)
            else:
                _, kind, n, _ = it
                src_for = lambda k: _shard_ref(src_arr, kind, k, n)
                dst_from = lambda k: outs[t].at[k]
            local.append(pltpu.make_async_copy(src_for(me), dst_from(me), loc_sems.at[t]))
            for j, (px, py) in enumerate(peers):
                pk = 2 * px + py
                args = dict(send_sem=send_sems.at[3 * t + j], recv_sem=recv_sems.at[3 * t + j],
                            device_id=(px, py, c), device_id_type=MESH)
                sent.append(pltpu.make_async_remote_copy(src_ref=src_for(pk), dst_ref=dst_from(me), **args))
                arriving.append(pltpu.make_async_remote_copy(src_ref=src_for(pk), dst_ref=dst_from(pk), **args))
        return local, sent, arriving

    def start(self, ins, outs, sems):
        local, sent, arriving = self._copies(ins, outs, sems)
        for cp in local + sent:
            cp.start()
        for cp in arriving:
            cp._used = True

    def wait(self, ins, outs, sems):
        local, sent, arriving = self._copies(ins, outs, sems)
        for cp in arriving:
            cp.wait_recv()
        for cp in sent:
            cp.wait_send()
        for cp in local:
            cp.wait()


def _run_exchange(ex, name):
    nin, nout = len(ex.arrays), len(ex.out_shapes)

    def body(*refs):
        ins, outs, sems = refs[:nin], refs[nin:nin + nout], refs[nin + nout:]
        ex.start(ins, outs, sems)
        ex.wait(ins, outs, sems)

    anyspec = pl.BlockSpec(memory_space=pl.ANY)
    return pl.pallas_call(body, in_specs=[anyspec] * nin, out_specs=[anyspec] * nout, out_shape=ex.out_shapes,
                          scratch_shapes=ex.scratch, name=name)(*ex.arrays)


def _swap_with_sibling(arrs, name):
    n = len(arrs)

    def body(*refs):
        ins, outs = refs[:n], refs[n:2 * n]
        send_sems, recv_sems = refs[2 * n:]
        x, y, c = lax.axis_index("x"), lax.axis_index("y"), lax.axis_index("c")
        copies = [pltpu.make_async_remote_copy(
            src_ref=ins[t], dst_ref=outs[t], send_sem=send_sems.at[t], recv_sem=recv_sems.at[t],
            device_id=(x, y, 1 - c), device_id_type=MESH) for t in range(n)]
        for cp in copies:
            cp.start()
        for cp in copies:
            cp.wait_recv()
        for cp in copies:
            cp.wait_send()

    anyspec = pl.BlockSpec(memory_space=pl.ANY)
    return pl.pallas_call(
        body, in_specs=[anyspec] * n, out_specs=[anyspec] * n,
        out_shape=[jax.ShapeDtypeStruct(a.shape, a.dtype) for a in arrs],
        scratch_shapes=[pltpu.SemaphoreType.DMA((n,)), pltpu.SemaphoreType.DMA((n,))],
        name=name)(*arrs)


def _all_gather_devices(v, name):
    def body(v_ref, o_ref, send_sems, recv_sems, loc_sem):
        x, y, c = lax.axis_index("x"), lax.axis_index("y"), lax.axis_index("c")
        me = 4 * x + 2 * y + c
        lc = pltpu.make_async_copy(v_ref, o_ref.at[me], loc_sem)
        lc.start()
        rel = [(bx, by, bc) for bx in (0, 1) for by in (0, 1) for bc in (0, 1)][1:]
        copies = []
        for j, (bx, by, bc) in enumerate(rel):
            px, py, pc = x ^ bx, y ^ by, c ^ bc
            copies.append((pltpu.make_async_remote_copy(
                src_ref=v_ref, dst_ref=o_ref.at[me], send_sem=send_sems.at[j], recv_sem=recv_sems.at[j],
                device_id=(px, py, pc), device_id_type=MESH), 4 * px + 2 * py + pc))
        for cp, _ in copies:
            cp.start()
        for j, (cp, pid) in enumerate(copies):
            pltpu.make_async_remote_copy(
                src_ref=v_ref, dst_ref=o_ref.at[pid], send_sem=send_sems.at[j], recv_sem=recv_sems.at[j],
                device_id=(x, y, c), device_id_type=MESH).wait_recv()
        for cp, _ in copies:
            cp.wait_send()
        lc.wait()

    anyspec = pl.BlockSpec(memory_space=pl.ANY)
    return pl.pallas_call(
        body, in_specs=[anyspec], out_specs=anyspec,
        out_shape=jax.ShapeDtypeStruct((N_DEV,) + v.shape, v.dtype),
        scratch_shapes=[pltpu.SemaphoreType.DMA((N_DEV - 1,)), pltpu.SemaphoreType.DMA((N_DEV - 1,)),
                        pltpu.SemaphoreType.DMA(())],
        name=name)(v)


BIG = ("attn_w_qkv", "attn_w_o", "ssm_w_in", "ssm_w_out", "ffn_w_up", "ffn_w_down")
BIG_KIND = {"attn_w_qkv": "col", "attn_w_o": "row", "ssm_w_in": "lead", "ssm_w_out": "row",
            "ffn_w_up": "col", "ffn_w_down": "row"}
SMALL_SHARDED = {"ssm_conv_w": 2, "ssm_conv_b": 1, "ssm_norm_w": 1, "ffn_conv_w": 2}
SMALL = ("mix_norm_w", "ssm_conv_w", "ssm_conv_b", "ssm_dt_bias", "ssm_a_log", "ssm_d", "ssm_norm_w",
         "ffn_norm_w", "ffn_conv_w", "ffn_conv_b", "final_norm_w")
WEIGHTS = ("mix_norm_w", "attn_w_qkv", "attn_w_o", "ssm_w_in", "ssm_conv_w", "ssm_conv_b", "ssm_dt_bias",
           "ssm_a_log", "ssm_d", "ssm_norm_w", "ssm_w_out", "ffn_norm_w", "ffn_w_up", "ffn_conv_w",
           "ffn_conv_b", "ffn_w_down", "final_norm_w")


def _shard_extent(name, shape):
    _, a, b = shape
    return {"col": b, "row": a, "lead": 1}[BIG_KIND[name]]


def _gather_item(w16, name, layer):
    _, a, b = w16.shape
    kind = BIG_KIND[name]
    full = {"col": (a, N_CHIPS * b), "row": (N_CHIPS * a, b), "lead": (N_CHIPS, a, b)}[kind]
    return (w16, layer, kind, _shard_extent(name, w16.shape), full)


def _layer_weights(i):
    j = i // 2
    mixer = [("attn_w_qkv", j), ("attn_w_o", j)] if i % 2 == 0 else [("ssm_w_in", j), ("ssm_w_out", j)]
    return mixer + [("ffn_w_up", i), ("ffn_w_down", i)]


def kernel(x, mix_norm_w, attn_w_qkv, attn_w_o, ssm_w_in, ssm_conv_w, ssm_conv_b, ssm_dt_bias, ssm_a_log, ssm_d, ssm_norm_w, ssm_w_out, ffn_norm_w, ffn_w_up, ffn_conv_w, ffn_conv_b, ffn_w_down, final_norm_w, loss_target, m_mix_norm_w, m_attn_w_qkv, m_attn_w_o, m_ssm_w_in, m_ssm_conv_w, m_ssm_conv_b, m_ssm_dt_bias, m_ssm_a_log, m_ssm_d, m_ssm_norm_w, m_ssm_w_out, m_ffn_norm_w, m_ffn_w_up, m_ffn_conv_w, m_ffn_conv_b, m_ffn_w_down, m_final_norm_w, v_mix_norm_w, v_attn_w_qkv, v_attn_w_o, v_ssm_w_in, v_ssm_conv_w, v_ssm_conv_b, v_ssm_dt_bias, v_ssm_a_log, v_ssm_d, v_ssm_norm_w, v_ssm_w_out, v_ffn_norm_w, v_ffn_w_up, v_ffn_conv_w, v_ffn_conv_b, v_ffn_w_down, v_final_norm_w):
    W = dict(mix_norm_w=mix_norm_w, attn_w_qkv=attn_w_qkv, attn_w_o=attn_w_o, ssm_w_in=ssm_w_in,
             ssm_conv_w=ssm_conv_w, ssm_conv_b=ssm_conv_b, ssm_dt_bias=ssm_dt_bias, ssm_a_log=ssm_a_log,
             ssm_d=ssm_d, ssm_norm_w=ssm_norm_w, ssm_w_out=ssm_w_out, ffn_norm_w=ffn_norm_w, ffn_w_up=ffn_w_up,
             ffn_conv_w=ffn_conv_w, ffn_conv_b=ffn_conv_b, ffn_w_down=ffn_w_down, final_norm_w=final_norm_w)
    M = dict(mix_norm_w=m_mix_norm_w, attn_w_qkv=m_attn_w_qkv, attn_w_o=m_attn_w_o, ssm_w_in=m_ssm_w_in,
             ssm_conv_w=m_ssm_conv_w, ssm_conv_b=m_ssm_conv_b, ssm_dt_bias=m_ssm_dt_bias, ssm_a_log=m_ssm_a_log,
             ssm_d=m_ssm_d, ssm_norm_w=m_ssm_norm_w, ssm_w_out=m_ssm_w_out, ffn_norm_w=m_ffn_norm_w,
             ffn_w_up=m_ffn_w_up, ffn_conv_w=m_ffn_conv_w, ffn_conv_b=m_ffn_conv_b, ffn_w_down=m_ffn_w_down,
             final_norm_w=m_final_norm_w)
    V = dict(mix_norm_w=v_mix_norm_w, attn_w_qkv=v_attn_w_qkv, attn_w_o=v_attn_w_o, ssm_w_in=v_ssm_w_in,
             ssm_conv_w=v_ssm_conv_w, ssm_conv_b=v_ssm_conv_b, ssm_dt_bias=v_ssm_dt_bias, ssm_a_log=v_ssm_a_log,
             ssm_d=v_ssm_d, ssm_norm_w=v_ssm_norm_w, ssm_w_out=v_ssm_w_out, ffn_norm_w=v_ffn_norm_w,
             ffn_w_up=v_ffn_w_up, ffn_conv_w=v_ffn_conv_w, ffn_conv_b=v_ffn_conv_b, ffn_w_down=v_ffn_w_down,
             final_norm_w=v_final_norm_w)

    S, D = x.shape[1], x.shape[2]
    xs = x.reshape(S, D)
    tgt = loss_target.reshape(S, D)
    depth = mix_norm_w.shape[0]
    heads = attn_w_o.shape[1] * N_CHIPS // HEAD_DIM
    AW = heads * HEAD_DIM
    d_inner = ssm_w_out.shape[1] * N_CHIPS
    ssm_heads = d_inner // SSM_HEAD_DIM
    G, P, N = SSM_GROUPS, SSM_HEAD_DIM, SSM_STATE
    gw = d_inner // G
    conv_dim = d_inner + 2 * G * N
    in_w = d_inner + conv_dim + ssm_heads
    in_pad = -(-in_w // 128) * 128
    shard_in = ssm_w_in.shape[2]
    xi, yi = lax.axis_index("x"), lax.axis_index("y")
    chip = 2 * xi + yi

    W16 = {n: W[n].astype(BF16) for n in BIG}
    full = {}

    def land(keys, outs):
        for (n, l), o in zip(keys, outs):
            if n == "ssm_w_in":
                o = jnp.pad(jnp.concatenate([o[k] for k in range(N_CHIPS)], axis=1), ((0, 0), (0, in_pad - in_w)))
            full[(n, l)] = o

    def fwd_mm(a, b, name, fetch=(), resid=None):
        if not fetch:
            return _matmul(a, b, "nn", F32, name, resid=resid)
        ex = _Exchange("gather", [_gather_item(W16[n], n, l) for n, l in fetch])
        out, got = _matmul(a, b, "nn", F32, name, resid=resid, carry=ex)
        land(fetch, got)
        return out

    sm_names = list(SMALL_SHARDED)
    packed = _pack([W[n] for n in sm_names])
    first = _layer_weights(0)
    got = _run_exchange(_Exchange("gather", [_gather_item(W16[n], n, l) for n, l in first]
                                  + [(packed, None, "lead", 1, (N_CHIPS,) + packed.shape)]), "gather_first")
    land(first, got[:-1])
    per_chip = [_unpack(got[-1][k], [W[n].shape for n in sm_names]) for k in range(N_CHIPS)]
    for q, n in enumerate(sm_names):
        full[n] = jnp.concatenate([per_chip[k][q] for k in range(N_CHIPS)], axis=SMALL_SHARDED[n])
    tab = _perm_tokens(_rope_table(S))

    def rep_heads(p):
        return jnp.repeat(p, P).reshape(G, 1, gw)

    saved = []
    cur = xs
    for i in range(depth):
        j = i // 2
        nxt = _layer_weights(i + 1) if i + 1 < depth else None
        sv = {"x_in": cur}
        h = _rms_fwd(cur, mix_norm_w[i], f"mix_norm_fwd_{i}")
        sv["h"] = h
        if i % 2 == 0:
            h = _perm_tokens(h)
            sv["h"] = h
            qkv = fwd_mm(h, full[("attn_w_qkv", j)], f"qkv_fwd_{i}", fetch=nxt[:1] if nxt else ())
            og = [_attn_fwd(qkv, tab, g, heads, f"attn_fwd_{i}_{g}") for g in range(3)]
            o, lse = _attn_combine([a for a, _ in og], [b for _, b in og], f"attn_combine_{i}")
            mixed = _unperm_tokens(fwd_mm(o, full[("attn_w_o", j)], f"attn_out_fwd_{i}"))
            sv.update(qkv=qkv, o=o, lse=lse)
        else:
            zx = fwd_mm(h, full[("ssm_w_in", j)], f"ssm_in_fwd_{i}", fetch=nxt[:1] if nxt else ())
            conv = _ssm_conv_fwd(zx, full["ssm_conv_w"][j], full["ssm_conv_b"][j], d_inner, conv_dim,
                                 f"ssm_conv_fwd_{i}")
            dtr = jnp.repeat(zx[:, d_inner + conv_dim:in_w], P, axis=1)
            prm = [rep_heads(p[j]) for p in (ssm_dt_bias, ssm_a_log, ssm_d)]
            y, sprev = _ssd_fwd(conv, dtr, *prm, d_inner, f"ssd_fwd_{i}")
            gated = _gnorm_fwd(y, zx, full["ssm_norm_w"][j], f"ssm_norm_fwd_{i}")
            cur = fwd_mm(gated, full[("ssm_w_out", j)], f"ssm_out_fwd_{i}", resid=cur)
            sv.update(zx=zx, conv=conv, dtr=dtr, prm=prm, y=y, sprev=sprev, gated=gated)
            mixed = None
        if mixed is None:
            h2 = _rms_fwd(cur, ffn_norm_w[i], f"ffn_norm_fwd_{i}")
        else:
            cur, h2 = _rms_fwd(cur, ffn_norm_w[i], f"ffn_norm_fwd_{i}", add=mixed)
        sv["x_mid"] = cur
        up = fwd_mm(h2, full[("ffn_w_up", i)], f"ffn_up_fwd_{i}", fetch=[nxt[2], nxt[1]] if nxt else ())
        u2, act = _ffn_conv_fwd(up, full["ffn_conv_w"][i], ffn_conv_b[i], f"ffn_conv_fwd_{i}")
        cur = fwd_mm(act, full[("ffn_w_down", i)], f"ffn_down_fwd_{i}", fetch=nxt[3:] if nxt else (), resid=cur)
        sv.update(h2=h2, up=up, u2=u2, act=act)
        saved.append(sv)

    dx, d_final, loss_part = _loss_head(cur, final_norm_w, tgt, "loss_head")
    gbig, recv = {}, {}
    gs = {n: [None] * W[n].shape[0] for n in SMALL if n != "final_norm_w"}

    def scatter_ex(keys):
        return _Exchange("scatter", [(gbig[(n, l)], BIG_KIND[n], _shard_extent(n, W[n].shape),
                                      (N_CHIPS,) + W[n].shape[1:]) for n, l in keys])

    def bwd_mm(a, b, mode, dtype, name, send=()):
        if not send:
            return _matmul(a, b, mode, dtype, name)
        out, got = _matmul(a, b, mode, dtype, name, carry=scatter_ex(send))
        recv.update(zip(send, got))
        return out

    late = []
    for i in reversed(range(depth)):
        j = i // 2
        sv = saved[i]
        k_in, k_out, k_up, k_down = _layer_weights(i)
        dact = _matmul(dx, full[k_down], "nt", F32, f"ffn_down_dgrad_{i}")
        gbig[k_down] = bwd_mm(sv["act"], dx, "tn", BF16, f"ffn_down_wgrad_{i}", send=late)
        dup, dcw, dcb = _ffn_conv_bwd(sv["u2"], dact, sv["up"], full["ffn_conv_w"][i], f"ffn_conv_bwd_{i}")
        gs["ffn_conv_w"][i], gs["ffn_conv_b"][i] = dcw, dcb[0]
        dh2 = bwd_mm(dup, full[k_up], "nt", F32, f"ffn_up_dgrad_{i}", send=[k_down])
        gbig[k_up] = _matmul(sv["h2"], dup, "tn", BF16, f"ffn_up_wgrad_{i}")
        dx, dnw = _rms_bwd(sv["x_mid"], ffn_norm_w[i], dh2, dx, f"ffn_norm_bwd_{i}")
        gs["ffn_norm_w"][i] = dnw[0]
        if i % 2 == 0:
            dxp = _perm_tokens(dx)
            do = _matmul(dxp, full[k_out], "nt", F32, f"attn_out_dgrad_{i}")
            gbig[k_out] = _matmul(sv["o"], dxp, "tn", BF16, f"attn_out_wgrad_{i}")
            dqkv = None
            for g in range(3):
                dqkv = _attn_bwd(sv["qkv"], tab, sv["o"], sv["lse"], do, dqkv, g, heads, f"attn_bwd_{i}_{g}")
            dh = _unperm_tokens(bwd_mm(dqkv, full[k_in], "nt", F32, f"qkv_dgrad_{i}", send=[k_up, k_out]))
            gbig[k_in] = _matmul(sv["h"], dqkv, "tn", BF16, f"qkv_wgrad_{i}")
        else:
            dgated = _matmul(dx, full[k_out], "nt", F32, f"ssm_out_dgrad_{i}")
            gbig[k_out] = _matmul(sv["gated"], dx, "tn", BF16, f"ssm_out_wgrad_{i}")
            dy, dz, dgw = _gnorm_bwd(sv["y"], sv["zx"], full["ssm_norm_w"][j], dgated, f"ssm_norm_bwd_{i}")
            gs["ssm_norm_w"][j] = dgw[0]
            dxs_, dbm, dcm, ddtr, dbias, dalog, ddsk = _ssd_bwd(
                sv["conv"], sv["dtr"], *sv["prm"], sv["sprev"], dy, d_inner, f"ssd_bwd_{i}")
            gs["ssm_dt_bias"][j] = dbias.reshape(-1)[::P]
            gs["ssm_a_log"][j] = dalog.reshape(-1)[::P]
            gs["ssm_d"][j] = ddsk.reshape(ssm_heads, P).sum(axis=1)
            ddt = jnp.pad(ddtr[:, ::P], ((0, 0), (0, in_pad - in_w)))
            dzx, dcw, dcb = _ssm_conv_bwd(dxs_, dbm, dcm, sv["conv"], sv["zx"], dz, ddt, full["ssm_conv_w"][j],
                                          d_inner, f"ssm_conv_bwd_{i}")
            gs["ssm_conv_w"][j], gs["ssm_conv_b"][j] = dcw, dcb[0]
            dh = bwd_mm(dzx, full[k_in], "nt", F32, f"ssm_in_dgrad_{i}", send=[k_up, k_out])
            dwin = _matmul(sv["h"], dzx, "tn", BF16, f"ssm_in_wgrad_{i}")
            gbig[k_in] = jnp.stack([dwin[:, k * shard_in:(k + 1) * shard_in] for k in range(N_CHIPS)])
        late = [k_in]
        dx, dnw = _rms_bwd(sv["x_in"], mix_norm_w[i], dh, dx, f"mix_norm_bwd_{i}")
        gs["mix_norm_w"][i] = dnw[0]
    grad_x = dx.reshape(x.shape)

    recv.update(zip(late, _run_exchange(scatter_ex(late), "scatter_last")))
    big_keys = [(n, l) for n in BIG for l in range(W[n].shape[0])]
    sib = dict(zip(big_keys, _swap_with_sibling([recv[k] for k in big_keys], "swap_grads")))

    small_full = [jnp.stack(gs[n]) if n != "final_norm_w" else d_final[0] for n in SMALL]
    small_full.append(loss_part[0, 0:1])
    small_shapes = [a.shape for a in small_full]
    summed = _sum_devices(_all_gather_devices(_pack(small_full), "gather_small_grads"), "sum_small_grads")
    small_g = _unpack(summed, small_shapes)
    loss = small_g[-1][0]
    gsm = {}
    for n, g in zip(SMALL, small_g[:-1]):
        if n in SMALL_SHARDED:
            ax = SMALL_SHARDED[n]
            ext = W[n].shape[ax]
            g = lax.dynamic_slice_in_dim(g, chip * ext, ext, axis=ax)
        gsm[n] = g

    out_g, out_d, out_m, out_v = {}, {}, {}, {}
    for name in BIG:
        outs = None
        for l in range(W[name].shape[0]):
            outs = _adamw_big(recv[(name, l)], sib[(name, l)], W[name], M[name], V[name], l, outs,
                              f"adamw_{name}_{l}")
        out_g[name], out_d[name], out_m[name], out_v[name] = outs
    shapes = [W[n].shape for n in SMALL]
    pd, pm, pv = _adamw_small(_pack([gsm[n] for n in SMALL]), _pack([W[n] for n in SMALL]),
                              _pack([M[n] for n in SMALL]), _pack([V[n] for n in SMALL]), "adamw_small")
    for n, d_, m_, v_ in zip(SMALL, _unpack(pd, shapes), _unpack(pm, shapes), _unpack(pv, shapes)):
        out_g[n], out_d[n], out_m[n], out_v[n] = gsm[n], d_, m_, v_

    return (loss, grad_x, *[out_g[n] for n in WEIGHTS], *[out_d[n] for n in WEIGHTS],
            *[out_m[n] for n in WEIGHTS], *[out_v[n] for n in WEIGHTS])
```

```python
import functools
import math

import jax
import jax.numpy as jnp
from jax import lax
from jax.experimental import pallas as pl
from jax.experimental.pallas import tpu as pltpu

F32 = jnp.float32
BF16 = jnp.bfloat16
MESH = pl.DeviceIdType.MESH

NORM_EPS = 1e-5
HEAD_DIM = 128
ATTN_BLOCK = 128
ATTN_DILATIONS = (1, 4, 16)
ATTN_WINDOWS = (128, 512, 2048)
PERM = 16
ROPE_THETA = 500000.0
ROPE_HALF = HEAD_DIM // 8
SSM_HEAD_DIM = 64
SSM_STATE = 128
SSM_GROUPS = 8
SSM_CHUNK = 128
NEG = -1e30

ADAM_LR = 0.001
ADAM_B1 = 0.9
ADAM_B2 = 0.999
ADAM_EPS = 1e-08
ADAM_WD = 0.01
ADAM_STEP = 10

VMEM_LIMIT_BYTES = 48 * 1024 * 1024
N_CHIPS = 4
N_DEV = 8


def _cp(*sem):
    return pltpu.CompilerParams(dimension_semantics=sem, vmem_limit_bytes=VMEM_LIMIT_BYTES)


def _pick(n, pref, mult=128):
    best = None
    t = mult
    while t <= min(n, pref):
        if n % t == 0:
            best = t
        t += mult
    return n if best is None else best


def _sigmoid(x):
    return 1.0 / (1.0 + jnp.exp(-x))


def _silu(x):
    return x * _sigmoid(x)


def _softplus(x):
    u = jnp.exp(-jnp.abs(x))
    w = 1.0 + u
    log1p = jnp.where(w == 1.0, u, jnp.log(w) * (u / jnp.where(w == 1.0, 1.0, w - 1.0)))
    return jnp.maximum(x, 0.0) + log1p


def _dot(a, b, dims):
    return lax.dot_general(a.astype(BF16), b.astype(BF16), (dims, ((), ())),
                           preferred_element_type=F32)


def _dot_nn(a, b):
    return _dot(a, b, ((1,), (0,)))


def _dot_nt(a, b):
    return _dot(a, b, ((1,), (1,)))


def _dot_tn(a, b):
    return _dot(a, b, ((0,), (0,)))


MATMUL_VMEM_BYTES = 36 * 1024 * 1024
MATMUL_TILES = (2048, 1536, 1408, 1152, 1024, 896, 768, 640, 512, 384, 256, 128)


def _matmul_tiles(M, N, K, a_bytes, b_bytes, o_bytes, has_resid):
    best = None
    for tm in [t for t in MATMUL_TILES if M % t == 0] or [M]:
        for tn in [t for t in MATMUL_TILES if N % t == 0] or [N]:
            for tk in [t for t in MATMUL_TILES if K % t == 0 and t <= 1408] or [K]:
                nk, gm, gn = K // tk, M // tm, N // tn
                vmem = 2 * (tm * tk * a_bytes + tk * tn * b_bytes + tm * tn * o_bytes)
                vmem += (2 * tm * tn * 4 if has_resid else 0) + (tm * tn * 4 if nk > 1 else 0)
                if vmem > MATMUL_VMEM_BYTES:
                    continue
                a_reads = 1 if nk == 1 else gn
                b_reads = 1 if (nk == 1 and gn == 1) else gm
                traffic = M * K * a_bytes * a_reads + K * N * b_bytes * b_reads + M * N * o_bytes
                key = (traffic, gm * gn * nk)
                if best is None or key < best[0]:
                    best = (key, (tm, tn, tk))
    assert best is not None, (M, N, K)
    return best[1]


def _matmul(a, b, mode, out_dtype, name, resid=None, carry=None):
    if mode == "nn":
        (M, K), (K2, N) = a.shape, b.shape
    elif mode == "nt":
        (M, K), (N, K2) = a.shape, b.shape
    else:
        (K, M), (K2, N) = a.shape, b.shape
    assert K == K2, (a.shape, b.shape, mode)
    tm, tn, tk = _matmul_tiles(M, N, K, a.dtype.itemsize, b.dtype.itemsize, jnp.dtype(out_dtype).itemsize,
                               resid is not None)
    nk = K // tk
    gm, gn = M // tm, N // tn
    dims = {"nn": ((1,), (0,)), "nt": ((1,), (1,)), "tn": ((0,), (0,))}[mode]
    has_resid = resid is not None
    nci = len(carry.arrays) if carry else 0
    nco = len(carry.out_shapes) if carry else 0

    def body(a_ref, b_ref, *rest):
        r_ref = rest[0] if has_resid else None
        rest = rest[has_resid:]
        c_ins, o_ref, c_outs, scratch = rest[:nci], rest[nci], rest[nci + 1:nci + 1 + nco], rest[nci + 1 + nco:]
        acc_ref = scratch[0] if nk > 1 else None
        sems = scratch[nk > 1:]
        i, j, k = pl.program_id(0), pl.program_id(1), pl.program_id(2)
        if carry:
            @pl.when((i == 0) & (j == 0) & (k == 0))
            def _():
                carry.start(c_ins, c_outs, sems)

        if nk == 1:
            r = _dot(a_ref[...], b_ref[...], dims)
            if has_resid:
                r = r + r_ref[...]
            o_ref[...] = r.astype(o_ref.dtype)
        else:
            @pl.when(k == 0)
            def _():
                acc_ref[...] = jnp.zeros_like(acc_ref)

            acc_ref[...] += _dot(a_ref[...], b_ref[...], dims)

            @pl.when(k == nk - 1)
            def _():
                r = acc_ref[...]
                if has_resid:
                    r = r + r_ref[...]
                o_ref[...] = r.astype(o_ref.dtype)

        if carry:
            @pl.when((i == gm - 1) & (j == gn - 1) & (k == nk - 1))
            def _():
                carry.wait(c_ins, c_outs, sems)

    if mode == "nn":
        a_spec = pl.BlockSpec((tm, tk), lambda i, j, k: (i, k))
        b_spec = pl.BlockSpec((tk, tn), lambda i, j, k: (k, j))
    elif mode == "nt":
        a_spec = pl.BlockSpec((tm, tk), lambda i, j, k: (i, k))
        b_spec = pl.BlockSpec((tn, tk), lambda i, j, k: (j, k))
    else:
        a_spec = pl.BlockSpec((tk, tm), lambda i, j, k: (k, i))
        b_spec = pl.BlockSpec((tk, tn), lambda i, j, k: (k, j))
    o_spec = pl.BlockSpec((tm, tn), lambda i, j, k: (i, j))
    anyspec = pl.BlockSpec(memory_space=pl.ANY)
    in_specs = [a_spec, b_spec] + ([o_spec] if has_resid else []) + [anyspec] * nci
    args = (a, b) + ((resid,) if has_resid else ()) + (tuple(carry.arrays) if carry else ())
    out_shape = [jax.ShapeDtypeStruct((M, N), out_dtype)] + (carry.out_shapes if carry else [])
    scratch = ([] if nk == 1 else [pltpu.VMEM((tm, tn), F32)]) + (carry.scratch if carry else [])
    sem = ("arbitrary",) * 3 if carry else ("parallel", "parallel", "arbitrary")
    outs = pl.pallas_call(
        body, grid=(gm, gn, nk), in_specs=in_specs, out_specs=[o_spec] + [anyspec] * nco,
        out_shape=out_shape, scratch_shapes=scratch, compiler_params=_cp(*sem), name=name)(*args)
    return (outs[0], outs[1:]) if carry else outs[0]


def _rms_fwd(x, w, name, add=None):
    S, D = x.shape
    t = _pick(S, 512, 8)
    has_add = add is not None

    def body(x_ref, w_ref, *rest):
        xv = x_ref[...]
        if has_add:
            xv = xv + rest[0][...]
            rest[1][...] = xv
        r = lax.rsqrt(jnp.mean(xv * xv, axis=-1, keepdims=True) + NORM_EPS)
        rest[-1][...] = (xv * r * w_ref[...]).astype(rest[-1].dtype)

    row = pl.BlockSpec((t, D), lambda i: (i, 0))
    vec = pl.BlockSpec((1, D), lambda i: (0, 0))
    normed = jax.ShapeDtypeStruct((S, D), BF16)
    if not has_add:
        return pl.pallas_call(body, grid=(S // t,), in_specs=[row, vec], out_specs=row, out_shape=normed,
                              compiler_params=_cp("parallel"), name=name)(x, w.reshape(1, D))
    return pl.pallas_call(body, grid=(S // t,), in_specs=[row, vec, row], out_specs=[row, row],
                          out_shape=[jax.ShapeDtypeStruct((S, D), F32), normed],
                          compiler_params=_cp("parallel"), name=name)(x, w.reshape(1, D), add)


def _rms_bwd(x, w, dh, dres, name):
    S, D = x.shape
    t = _pick(S, 512, 8)

    def body(x_ref, w_ref, dh_ref, dr_ref, dx_ref, dw_ref):
        @pl.when(pl.program_id(0) == 0)
        def _():
            dw_ref[...] = jnp.zeros_like(dw_ref)

        xv = x_ref[...]
        r = lax.rsqrt(jnp.mean(xv * xv, axis=-1, keepdims=True) + NORM_EPS)
        xh = xv * r
        dh_v = dh_ref[...]
        g = dh_v * w_ref[...]
        dx_ref[...] = dr_ref[...] + r * (g - xh * jnp.mean(g * xh, axis=-1, keepdims=True))
        dw_ref[...] += jnp.sum(dh_v * xh, axis=0, keepdims=True)

    row = pl.BlockSpec((t, D), lambda i: (i, 0))
    vec = pl.BlockSpec((1, D), lambda i: (0, 0))
    return pl.pallas_call(
        body, grid=(S // t,), in_specs=[row, vec, row, row], out_specs=[row, vec],
        out_shape=[jax.ShapeDtypeStruct((S, D), F32), jax.ShapeDtypeStruct((1, D), F32)],
        compiler_params=_cp("arbitrary"), name=name)(x, w.reshape(1, D), dh, dres)


def _loss_head(x, w, tgt, name):
    S, D = x.shape
    t = _pick(S, 512, 8)

    def body(x_ref, w_ref, t_ref, dx_ref, dw_ref, l_ref):
        @pl.when(pl.program_id(0) == 0)
        def _():
            dw_ref[...] = jnp.zeros_like(dw_ref)
            l_ref[...] = jnp.zeros_like(l_ref)

        xv = x_ref[...]
        wv = w_ref[...]
        r = lax.rsqrt(jnp.mean(xv * xv, axis=-1, keepdims=True) + NORM_EPS)
        xh = xv * r
        err = xh * wv - t_ref[...]
        per_tok = jnp.mean(err * err, axis=-1, keepdims=True)
        l_ref[...] += 0.5 * jnp.sum(per_tok, axis=0, keepdims=True)
        dy = err * (1.0 / D)
        g = dy * wv
        dx_ref[...] = r * (g - xh * jnp.mean(g * xh, axis=-1, keepdims=True))
        dw_ref[...] += jnp.sum(dy * xh, axis=0, keepdims=True)

    row = pl.BlockSpec((t, D), lambda i: (i, 0))
    vec = pl.BlockSpec((1, D), lambda i: (0, 0))
    lspec = pl.BlockSpec((1, 128), lambda i: (0, 0))
    return pl.pallas_call(
        body, grid=(S // t,), in_specs=[row, vec, row], out_specs=[row, vec, lspec],
        out_shape=[jax.ShapeDtypeStruct((S, D), F32), jax.ShapeDtypeStruct((1, D), F32),
                   jax.ShapeDtypeStruct((1, 128), F32)],
        compiler_params=_cp("arbitrary"), name=name)(x, w.reshape(1, D), tgt)


def _rope_table(seq):
    pos = jnp.arange(seq, dtype=F32)
    inv_freq = ROPE_THETA ** (-jnp.arange(0, 2 * ROPE_HALF, 2, dtype=F32) / (2 * ROPE_HALF))
    ang = pos[:, None] * inv_freq[None, :]
    cos, sin = jnp.cos(ang), jnp.sin(ang)
    pad = HEAD_DIM - 2 * ROPE_HALF
    cos_p = jnp.concatenate([cos, cos, jnp.ones((seq, pad), F32)], axis=1)
    sin_a = jnp.concatenate([-sin, jnp.zeros((seq, HEAD_DIM - ROPE_HALF), F32)], axis=1)
    sin_b = jnp.concatenate([jnp.zeros((seq, ROPE_HALF), F32), sin, jnp.zeros((seq, pad), F32)], axis=1)
    return jnp.concatenate([cos_p, sin_a, sin_b], axis=1)


def _rope(t, tab, sign):
    cos_p = tab[:, 0:HEAD_DIM]
    sin_a = tab[:, HEAD_DIM:2 * HEAD_DIM]
    sin_b = tab[:, 2 * HEAD_DIM:3 * HEAD_DIM]
    up = pltpu.roll(t, HEAD_DIM - ROPE_HALF, 1)
    down = pltpu.roll(t, ROPE_HALF, 1)
    return t * cos_p + sign * (up * sin_a + down * sin_b)


def _perm_tokens(a):
    S = a.shape[0]
    return a.reshape(S // PERM, PERM, -1).transpose(1, 0, 2).reshape(S, -1)


def _unperm_tokens(a):
    S = a.shape[0]
    return a.reshape(PERM, S // PERM, -1).transpose(1, 0, 2).reshape(S, -1)


class _Strided:
    def __init__(self, S, dil):
        self.dil, self.m = dil, PERM // dil
        self.c = ATTN_BLOCK // self.m
        self.rows = S // PERM
        self.nb = S // (dil * ATTN_BLOCK)

    def view(self, a):
        return a.reshape(self.m, self.dil, self.rows, a.shape[-1])

    def spec(self, width, col, f=lambda n: n):
        return pl.BlockSpec((self.m, None, self.c, width), lambda r, n: (0, r, f(n), col))

    def load(self, ref, sl=slice(None)):
        if self.m == 1:
            return ref[0, :, sl]
        return jnp.concatenate([ref[q, :, sl] for q in range(self.m)], axis=0)

    def store(self, ref, sl, val):
        for q in range(self.m):
            ref[q, :, sl] = val[q * self.c:(q + 1) * self.c, :]

    def member(self, i):
        shift = self.c.bit_length() - 1
        return (i & (self.c - 1)) * self.m + (i >> shift)


def _attn_fwd(qkv, tab, g, heads, name):
    S = qkv.shape[0]
    W = heads * HEAD_DIM
    dil = ATTN_DILATIONS[g]
    steps = ATTN_WINDOWS[g] // dil
    B = ATTN_BLOCK
    scale = HEAD_DIM ** -0.5
    st = _Strided(S, dil)

    def body(q_ref, k_ref, kp_ref, v_ref, vp_ref, t_ref, tp_ref, o_ref, l_ref):
        n = pl.program_id(1)
        ii = lax.broadcasted_iota(jnp.int32, (B, 2 * B), 0)
        jj = lax.broadcasted_iota(jnp.int32, (B, 2 * B), 1)
        delta = st.member(ii) - st.member(jj & (B - 1)) + jnp.where(jj >= B, 0, B)
        ok = (delta >= 0) & (delta <= steps) & ((jj >= B) | (n > 0))
        tb = st.load(t_ref)
        tpv = st.load(tp_ref)
        for h in range(heads):
            sl = slice(h * HEAD_DIM, (h + 1) * HEAD_DIM)
            q = _rope(st.load(q_ref, sl), tb, 1.0)
            kc = jnp.concatenate([_rope(st.load(kp_ref, sl), tpv, 1.0), _rope(st.load(k_ref, sl), tb, 1.0)], axis=0)
            vc = jnp.concatenate([st.load(vp_ref, sl), st.load(v_ref, sl)], axis=0)
            s = jnp.where(ok, _dot_nt(q, kc) * scale, NEG)
            m = jnp.max(s, axis=-1, keepdims=True)
            p = jnp.exp(s - m)
            den = jnp.sum(p, axis=-1, keepdims=True)
            st.store(o_ref, sl, _dot_nn(p, vc) / den)
            st.store(l_ref, sl, jnp.broadcast_to(m + jnp.log(den), (B, HEAD_DIM)))

    prv = lambda n: jnp.maximum(n - 1, 0)
    qv, tv = st.view(qkv), st.view(tab)
    o_spec = st.spec(W, 0)
    o, lse = pl.pallas_call(
        body, grid=(dil, st.nb),
        in_specs=[st.spec(W, g * 3), st.spec(W, g * 3 + 1), st.spec(W, g * 3 + 1, prv),
                  st.spec(W, g * 3 + 2), st.spec(W, g * 3 + 2, prv),
                  st.spec(3 * HEAD_DIM, 0), st.spec(3 * HEAD_DIM, 0, prv)],
        out_specs=[o_spec, o_spec],
        out_shape=[jax.ShapeDtypeStruct((st.m, dil, st.rows, W), F32)] * 2,
        compiler_params=_cp("parallel", "parallel"), name=name)(qv, qv, qv, qv, qv, tv, tv)
    return o.reshape(S, W), lse.reshape(S, W)


def _attn_combine(os_, ls_, name):
    S, W = os_[0].shape
    t = _pick(S, 256, 8)

    def body(o0, o1, o2, l0, l1, l2, o_ref, l_ref):
        a, b, c = l0[...], l1[...], l2[...]
        m = jnp.maximum(jnp.maximum(a, b), c)
        ea, eb, ec = jnp.exp(a - m), jnp.exp(b - m), jnp.exp(c - m)
        tot = ea + eb + ec
        o_ref[...] = (ea * o0[...] + eb * o1[...] + ec * o2[...]) / tot
        l_ref[...] = m + jnp.log(tot)

    row = pl.BlockSpec((t, W), lambda i: (i, 0))
    return pl.pallas_call(body, grid=(S // t,), in_specs=[row] * 6, out_specs=[row, row],
                          out_shape=[jax.ShapeDtypeStruct((S, W), F32)] * 2,
                          compiler_params=_cp("parallel"), name=name)(*os_, *ls_)


def _attn_bwd(qkv, tab, o, lse, do, dqkv_prev, g, heads, name):
    S = qkv.shape[0]
    W = heads * HEAD_DIM
    dil = ATTN_DILATIONS[g]
    steps = ATTN_WINDOWS[g] // dil
    B = ATTN_BLOCK
    scale = HEAD_DIM ** -0.5
    st = _Strided(S, dil)
    nb = st.nb
    aliased = dqkv_prev is not None

    def body(q_ref, qn_ref, k_ref, kp_ref, v_ref, vp_ref, do_ref, don_ref, o_ref, on_ref,
             l_ref, ln_ref, t_ref, tp_ref, tn_ref, *rest):
        out_ref = rest[-1]
        n = pl.program_id(1)
        has_next = n < nb - 1
        ia = lax.broadcasted_iota(jnp.int32, (B, 2 * B), 0)
        ja = lax.broadcasted_iota(jnp.int32, (B, 2 * B), 1)
        da = st.member(ia) - st.member(ja & (B - 1)) + jnp.where(ja >= B, 0, B)
        ok_a = (da >= 0) & (da <= steps) & ((ja >= B) | (n > 0))
        ib = lax.broadcasted_iota(jnp.int32, (2 * B, B), 0)
        jb = lax.broadcasted_iota(jnp.int32, (2 * B, B), 1)
        db = st.member(ib & (B - 1)) + jnp.where(ib >= B, B, 0) - st.member(jb)
        ok_b = (db >= 0) & (db <= steps) & ((ib < B) | has_next)
        tb, tpv, tnv = st.load(t_ref), st.load(tp_ref), st.load(tn_ref)
        for h in range(heads):
            sl = slice(h * HEAD_DIM, (h + 1) * HEAD_DIM)
            qr = _rope(st.load(q_ref, sl), tb, 1.0)
            qnr = _rope(st.load(qn_ref, sl), tnv, 1.0)
            kr = _rope(st.load(k_ref, sl), tb, 1.0)
            kpr = _rope(st.load(kp_ref, sl), tpv, 1.0)
            v = st.load(v_ref, sl)
            dov_ = st.load(do_ref, sl)
            donv = st.load(don_ref, sl)
            dl = jnp.sum(dov_ * st.load(o_ref, sl), axis=-1, keepdims=True)
            dln = jnp.sum(donv * st.load(on_ref, sl), axis=-1, keepdims=True)
            ls = st.load(l_ref, sl)
            kc = jnp.concatenate([kpr, kr], axis=0)
            vc = jnp.concatenate([st.load(vp_ref, sl), v], axis=0)
            s = _dot_nt(qr, kc) * scale
            p = jnp.where(ok_a, jnp.exp(jnp.minimum(s - jnp.concatenate([ls, ls], axis=1), 30.0)), 0.0)
            ds = p * (_dot_nt(dov_, vc) - dl) * scale
            st.store(out_ref, sl, _rope(_dot_nn(ds, kc), tb, -1.0))
            qc = jnp.concatenate([qr, qnr], axis=0)
            doc = jnp.concatenate([dov_, donv], axis=0)
            lc = jnp.concatenate([ls, st.load(ln_ref, sl)], axis=0)
            dlc = jnp.concatenate([dl, dln], axis=0)
            s2 = _dot_nt(qc, kr) * scale
            p2 = jnp.where(ok_b, jnp.exp(jnp.minimum(s2 - lc, 30.0)), 0.0)
            ds2 = p2 * (_dot_nt(doc, v) - dlc) * scale
            st.store(out_ref, slice(W + h * HEAD_DIM, W + (h + 1) * HEAD_DIM), _rope(_dot_tn(ds2, qc), tb, -1.0))
            st.store(out_ref, slice(2 * W + h * HEAD_DIM, 2 * W + (h + 1) * HEAD_DIM), _dot_tn(p2, doc))

    nxt = lambda n: jnp.minimum(n + 1, nb - 1)
    prv = lambda n: jnp.maximum(n - 1, 0)
    same = lambda n: n
    q0, q1, q2, tw = g * 3, g * 3 + 1, g * 3 + 2, 3 * HEAD_DIM
    in_specs = [st.spec(W, q0), st.spec(W, q0, nxt), st.spec(W, q1), st.spec(W, q1, prv),
                st.spec(W, q2), st.spec(W, q2, prv)]
    in_specs += [st.spec(W, 0, f) for f in (same, nxt, same, nxt, same, nxt)]
    in_specs += [st.spec(tw, 0, f) for f in (same, prv, nxt)]
    qv, tv, ov, lv, dov = (st.view(a) for a in (qkv, tab, o, lse, do))
    args = [qv, qv, qv, qv, qv, qv, dov, dov, ov, ov, lv, lv, tv, tv, tv]
    kwargs = {}
    if aliased:
        in_specs.append(pl.BlockSpec(memory_space=pl.ANY))
        args.append(st.view(dqkv_prev))
        kwargs["input_output_aliases"] = {len(args) - 1: 0}
    out = pl.pallas_call(
        body, grid=(dil, nb), in_specs=in_specs, out_specs=st.spec(3 * W, g),
        out_shape=jax.ShapeDtypeStruct((st.m, dil, st.rows, 9 * W), F32),
        compiler_params=_cp("parallel", "parallel"), name=name, **kwargs)(*args)
    return out.reshape(S, 9 * W)


def _shift_down(x, halo, s):
    if s == 0:
        return x
    T = x.shape[0]
    xs = pltpu.roll(x, s, 0)
    hs = pltpu.roll(halo, s, 0)
    row8 = lax.broadcasted_iota(jnp.int32, hs.shape, 0)
    top = jnp.where(row8 < s, hs, xs[0:8])
    return jnp.concatenate([top, xs[8:T]], axis=0)


def _shift_up(x, halo, s):
    if s == 0:
        return x
    T = x.shape[0]
    xs = pltpu.roll(x, T - s, 0)
    hs = pltpu.roll(halo, 8 - s, 0)
    row8 = lax.broadcasted_iota(jnp.int32, hs.shape, 0)
    bot = jnp.where(row8 >= 8 - s, hs, xs[T - 8:T])
    return jnp.concatenate([xs[0:T - 8], bot], axis=0)


CONV_ROWS = 128
CONV_LANES = 512


def _conv_apply(x, halo, w_ref, wsl, b, K):
    acc = x * w_ref[K - 1, :, wsl] + b
    for s in range(1, K):
        acc = acc + _shift_down(x, halo, s) * w_ref[K - 1 - s, :, wsl]
    return acc


def _conv_accum(dy, dyn, xv, xp, w_ref, dw_ref, db_ref, wsl, K):
    acc = dy * w_ref[K - 1, :, wsl]
    dw_ref[K - 1, :, wsl] += jnp.sum(dy * xv, axis=0, keepdims=True)
    for s in range(1, K):
        acc = acc + _shift_up(dy, dyn, s) * w_ref[K - 1 - s, :, wsl]
        dw_ref[K - 1 - s, :, wsl] += jnp.sum(dy * _shift_down(xv, xp, s), axis=0, keepdims=True)
    db_ref[:, wsl] += jnp.sum(dy, axis=0, keepdims=True)
    return acc


def _row_specs(T, S, width):
    main = pl.BlockSpec((T, width), lambda i: (i, 0))
    prev = pl.BlockSpec((8, width), lambda i: (jnp.maximum(i * (T // 8) - 1, 0), 0))
    nxt = pl.BlockSpec((8, width), lambda i: (jnp.minimum((i + 1) * (T // 8), S // 8 - 1), 0))
    return main, prev, nxt


def _full(shape):
    return pl.BlockSpec(shape, lambda i: (0,) * len(shape))


def _silu_grad(y):
    sg = _sigmoid(y)
    return sg * (1.0 + y * (1.0 - sg))


def _ssm_conv_fwd(zx, w, b, d_inner, conv_dim, name):
    S, wz = zx.shape
    K = w.shape[0]
    T = _pick(S, CONV_ROWS, 8)
    cw = _pick(conv_dim, CONV_LANES)

    def body(x_ref, h_ref, w_ref, b_ref, c_ref):
        has_prev = pl.program_id(0) > 0
        for cs in range(0, conv_dim, cw):
            so, sx = slice(cs, cs + cw), slice(d_inner + cs, d_inner + cs + cw)
            halo = jnp.where(has_prev, h_ref[:, sx], 0.0)
            c_ref[:, so] = _conv_apply(x_ref[:, sx], halo, w_ref, so, b_ref[:, so], K)

    main, prev, _ = _row_specs(T, S, wz)
    return pl.pallas_call(
        body, grid=(S // T,), in_specs=[main, prev, _full((K, 1, conv_dim)), _full((1, conv_dim))],
        out_specs=pl.BlockSpec((T, conv_dim), lambda i: (i, 0)),
        out_shape=jax.ShapeDtypeStruct((S, conv_dim), F32),
        compiler_params=_cp("parallel"), name=name)(zx, zx, w.reshape(K, 1, conv_dim), b.reshape(1, conv_dim))


def _ssm_conv_bwd(dxs, dbm, dcm, conv, zx, dz, ddt, w, d_inner, name):
    S, wz = zx.shape
    K, conv_dim = w.shape
    gn = dbm.shape[1]
    T = _pick(S, CONV_ROWS, 8)
    cw = _pick(math.gcd(d_inner, gn), CONV_LANES)
    nrow = S // T
    tail = wz - d_inner - conv_dim
    assert ddt.shape[1] == tail

    def body(dx_ref, dxn_ref, db_ref_, dbn_ref, dc_ref, dcn_ref, y_ref, yn_ref, x_ref, xp_ref, dz_ref, ddt_ref,
             w_ref, o_ref, dw_ref, dbias_ref):
        i = pl.program_id(0)

        @pl.when(i == 0)
        def _():
            dw_ref[...] = jnp.zeros_like(dw_ref)
            dbias_ref[...] = jnp.zeros_like(dbias_ref)

        has_prev, has_next = i > 0, i < nrow - 1
        for cs in range(0, d_inner, cw):
            o_ref[:, cs:cs + cw] = dz_ref[:, cs:cs + cw].astype(o_ref.dtype)
        o_ref[:, d_inner + conv_dim:wz] = ddt_ref[...].astype(o_ref.dtype)
        for cs in range(0, conv_dim, cw):
            so, sx = slice(cs, cs + cw), slice(d_inner + cs, d_inner + cs + cw)
            if cs < d_inner:
                src, srcn, ss = dx_ref, dxn_ref, slice(cs, cs + cw)
            elif cs < d_inner + gn:
                src, srcn, ss = db_ref_, dbn_ref, slice(cs - d_inner, cs - d_inner + cw)
            else:
                src, srcn, ss = dc_ref, dcn_ref, slice(cs - d_inner - gn, cs - d_inner - gn + cw)
            dy = src[:, ss] * _silu_grad(y_ref[:, so])
            dyn = jnp.where(has_next, srcn[:, ss] * _silu_grad(yn_ref[:, so]), 0.0)
            xp = jnp.where(has_prev, xp_ref[:, sx], 0.0)
            o_ref[:, sx] = _conv_accum(dy, dyn, x_ref[:, sx], xp, w_ref, dw_ref, dbias_ref, so, K).astype(o_ref.dtype)

    xm, _, xn = _row_specs(T, S, d_inner)
    gm, _, gnx = _row_specs(T, S, gn)
    cm, _, cn = _row_specs(T, S, conv_dim)
    zm, zp, _ = _row_specs(T, S, wz)
    tm_, _, _ = _row_specs(T, S, tail)
    dzx, dw, db = pl.pallas_call(
        body, grid=(nrow,),
        in_specs=[xm, xn, gm, gnx, gm, gnx, cm, cn, zm, zp, xm, tm_, _full((K, 1, conv_dim))],
        out_specs=[zm, _full((K, 1, conv_dim)), _full((1, conv_dim))],
        out_shape=[jax.ShapeDtypeStruct((S, wz), BF16), jax.ShapeDtypeStruct((K, 1, conv_dim), F32),
                   jax.ShapeDtypeStruct((1, conv_dim), F32)],
        compiler_params=_cp("arbitrary"), name=name)(
            dxs, dxs, dbm, dbm, dcm, dcm, conv, conv, zx, zx, dz, ddt, w.reshape(K, 1, conv_dim))
    return dzx, dw.reshape(K, conv_dim), db


def _ffn_conv_fwd(up, w, b, name):
    S, C = up.shape
    F = C // 2
    K = w.shape[0]
    T = _pick(S, CONV_ROWS, 8)
    cw = _pick(F, CONV_LANES)

    def body(x_ref, h_ref, w_ref, b_ref, u_ref, a_ref):
        has_prev = pl.program_id(0) > 0
        for cs in range(0, F, cw):
            sg, su = slice(cs, cs + cw), slice(F + cs, F + cs + cw)
            gate = _conv_apply(x_ref[:, sg], jnp.where(has_prev, h_ref[:, sg], 0.0), w_ref, sg, b_ref[:, sg], K)
            upv = _conv_apply(x_ref[:, su], jnp.where(has_prev, h_ref[:, su], 0.0), w_ref, su, b_ref[:, su], K)
            u_ref[0, :, sg] = gate
            u_ref[1, :, sg] = upv
            a_ref[:, sg] = (gate * _sigmoid(gate) * upv).astype(a_ref.dtype)

    main, prev, _ = _row_specs(T, S, C)
    return pl.pallas_call(
        body, grid=(S // T,), in_specs=[main, prev, _full((K, 1, C)), _full((1, C))],
        out_specs=[pl.BlockSpec((2, T, F), lambda i: (0, i, 0)), pl.BlockSpec((T, F), lambda i: (i, 0))],
        out_shape=[jax.ShapeDtypeStruct((2, S, F), F32), jax.ShapeDtypeStruct((S, F), BF16)],
        compiler_params=_cp("parallel"), name=name)(up, up, w.reshape(K, 1, C), b.reshape(1, C))


def _ffn_conv_bwd(u2, dact, up, w, name):
    S, C = up.shape
    F = C // 2
    K = w.shape[0]
    T = _pick(S, CONV_ROWS, 8)
    cw = _pick(F, CONV_LANES)
    nrow = S // T

    def du(half, gate, upv, d):
        sg = _sigmoid(gate)
        return d * upv * sg * (1.0 + gate * (1.0 - sg)) if half == 0 else d * gate * sg

    def body(u_ref, un_ref, d_ref, dn_ref, x_ref, xp_ref, w_ref, dx_ref, dw_ref, db_ref):
        i = pl.program_id(0)

        @pl.when(i == 0)
        def _():
            dw_ref[...] = jnp.zeros_like(dw_ref)
            db_ref[...] = jnp.zeros_like(db_ref)

        has_prev, has_next = i > 0, i < nrow - 1
        for half in range(2):
            for cs in range(0, F, cw):
                sf, sc = slice(cs, cs + cw), slice(half * F + cs, half * F + cs + cw)
                dy = du(half, u_ref[0, :, sf], u_ref[1, :, sf], d_ref[:, sf])
                dyn = jnp.where(has_next, du(half, un_ref[0, :, sf], un_ref[1, :, sf], dn_ref[:, sf]), 0.0)
                xp = jnp.where(has_prev, xp_ref[:, sc], 0.0)
                dx_ref[:, sc] = _conv_accum(dy, dyn, x_ref[:, sc], xp, w_ref, dw_ref, db_ref, sc, K).astype(dx_ref.dtype)

    am, _, an = _row_specs(T, S, F)
    xm, xp_, _ = _row_specs(T, S, C)
    u_main = pl.BlockSpec((2, T, F), lambda i: (0, i, 0))
    u_next = pl.BlockSpec((2, 8, F), lambda i: (0, jnp.minimum((i + 1) * (T // 8), S // 8 - 1), 0))
    dx, dw, db = pl.pallas_call(
        body, grid=(nrow,), in_specs=[u_main, u_next, am, an, xm, xp_, _full((K, 1, C))],
        out_specs=[xm, _full((K, 1, C)), _full((1, C))],
        out_shape=[jax.ShapeDtypeStruct((S, C), BF16), jax.ShapeDtypeStruct((K, 1, C), F32),
                   jax.ShapeDtypeStruct((1, C), F32)],
        compiler_params=_cp("arbitrary"), name=name)(u2, u2, dact, dact, up, up, w.reshape(K, 1, C))
    return dx, dw.reshape(K, C), db


def _cumsum_rows(v):
    n = v.shape[0]
    row = lax.broadcasted_iota(jnp.int32, v.shape, 0)
    k = 1
    while k < n:
        v = v + jnp.where(row >= k, pltpu.roll(v, k, 0), 0.0)
        k *= 2
    return v


def _rev_cumsum_rows(v):
    n = v.shape[0]
    row = lax.broadcasted_iota(jnp.int32, v.shape, 0)
    k = 1
    while k < n:
        v = v + jnp.where(row < n - k, pltpu.roll(v, n - k, 0), 0.0)
        k *= 2
    return v


def _ssd_common(x_ref, dtr_ref, bias_ref, alog_ref, gw):
    Q = SSM_CHUNK
    X = _silu(x_ref[...])
    pre = dtr_ref[...] + bias_ref[...]
    dt = _softplus(pre)
    a = -jnp.exp(alog_ref[...])
    cs = _cumsum_rows(dt * a)
    row = lax.broadcasted_iota(jnp.int32, (Q, gw), 0)
    cs_last = jnp.sum(jnp.where(row == Q - 1, cs, 0.0), axis=0, keepdims=True)
    return X, pre, dt, a, cs, cs_last, row


def _head_decay(cs, head_mask):
    Q = SSM_CHUNK
    col = jnp.max(jnp.where(head_mask, cs, NEG), axis=1, keepdims=True)
    acol = jnp.broadcast_to(col, (Q, Q))
    arow = acol.T
    ii = lax.broadcasted_iota(jnp.int32, (Q, Q), 0)
    jj = lax.broadcasted_iota(jnp.int32, (Q, Q), 1)
    tril = ii >= jj
    return jnp.where(tril, jnp.exp(jnp.where(tril, acol - arow, 0.0)), 0.0), tril


def _ssd_specs(S, d_inner, gw, nc, rev):
    Q, N, G = SSM_CHUNK, SSM_STATE, SSM_GROUPS
    ch = (lambda c: nc - 1 - c) if rev else (lambda c: c)
    x_spec = pl.BlockSpec((Q, gw), lambda g, c: (ch(c), g))
    b_spec = pl.BlockSpec((Q, N), lambda g, c: (ch(c), d_inner // N + g))
    c_spec = pl.BlockSpec((Q, N), lambda g, c: (ch(c), d_inner // N + G + g))
    p_spec = pl.BlockSpec((None, 1, gw), lambda g, c: (g, 0, 0))
    s_spec = pl.BlockSpec((None, None, gw, N), lambda g, c: (ch(c), g, 0, 0))
    return x_spec, b_spec, c_spec, p_spec, s_spec


def _ssd_fwd(xbc, dtr, bias, alog, dsk, d_inner, name):
    S = xbc.shape[0]
    Q, N, G, P = SSM_CHUNK, SSM_STATE, SSM_GROUPS, SSM_HEAD_DIM
    gw = d_inner // G
    R = gw // P
    nc = S // Q

    def body(x_ref, b_ref, c_ref, dtr_ref, bias_ref, alog_ref, d_ref, y_ref, sp_ref, s_scr):
        @pl.when(pl.program_id(1) == 0)
        def _():
            s_scr[...] = jnp.zeros_like(s_scr)

        X, _, dt, a, cs, cs_last, row = _ssd_common(x_ref, dtr_ref, bias_ref, alog_ref, gw)
        Bm, Cm = _silu(b_ref[...]), _silu(c_ref[...])
        xdt = X * dt
        lane = lax.broadcasted_iota(jnp.int32, (Q, gw), 1)
        sprev = s_scr[...]
        sp_ref[...] = sprev
        cb = _dot_nt(Cm, Bm)
        y = jnp.exp(cs) * _dot_nt(Cm, sprev)
        for r in range(R):
            hm = (lane >= r * P) & (lane < (r + 1) * P)
            dec_l, _ = _head_decay(cs, hm)
            y = y + _dot_nn(cb * dec_l, jnp.where(hm, xdt, 0.0))
        dec = jnp.exp(cs_last - cs)
        cd = jnp.exp(jnp.broadcast_to(cs_last, (Q, gw)).T)
        s_scr[...] = sprev * cd + _dot_tn(xdt * dec, Bm)
        y_ref[...] = y + d_ref[...] * X

    x_spec, b_spec, c_spec, p_spec, s_spec = _ssd_specs(S, d_inner, gw, nc, False)
    return pl.pallas_call(
        body, grid=(G, nc), in_specs=[x_spec, b_spec, c_spec, x_spec, p_spec, p_spec, p_spec],
        out_specs=[x_spec, s_spec],
        out_shape=[jax.ShapeDtypeStruct((S, d_inner), F32), jax.ShapeDtypeStruct((nc, G, gw, N), F32)],
        scratch_shapes=[pltpu.VMEM((gw, N), F32)],
        compiler_params=_cp("parallel", "arbitrary"), name=name)(xbc, xbc, xbc, dtr, bias, alog, dsk)


def _ssd_bwd(xbc, dtr, bias, alog, dsk, sprev_all, dy, d_inner, name):
    S = xbc.shape[0]
    Q, N, G, P = SSM_CHUNK, SSM_STATE, SSM_GROUPS, SSM_HEAD_DIM
    gw = d_inner // G
    R = gw // P
    nc = S // Q

    def body(x_ref, b_ref, c_ref, dtr_ref, bias_ref, alog_ref, d_ref, sp_ref, dy_ref,
             dx_ref, db_ref, dc_ref, ddt_ref, dbias_ref, dalog_ref, dd_ref, ds_scr):
        @pl.when(pl.program_id(1) == 0)
        def _():
            ds_scr[...] = jnp.zeros_like(ds_scr)
            dbias_ref[...] = jnp.zeros_like(dbias_ref)
            dalog_ref[...] = jnp.zeros_like(dalog_ref)
            dd_ref[...] = jnp.zeros_like(dd_ref)

        X, pre, dt, a, cs, cs_last, row = _ssd_common(x_ref, dtr_ref, bias_ref, alog_ref, gw)
        Bm, Cm = _silu(b_ref[...]), _silu(c_ref[...])
        dY = dy_ref[...]
        sprev = sp_ref[...]
        dsn = ds_scr[...]
        xdt = X * dt
        lane = lax.broadcasted_iota(jnp.int32, (Q, gw), 1)
        lane1 = lax.broadcasted_iota(jnp.int32, (1, gw), 1)
        srow = lax.broadcasted_iota(jnp.int32, (gw, N), 0)
        ecs = jnp.exp(cs)
        dec = jnp.exp(cs_last - cs)
        cd = jnp.exp(jnp.broadcast_to(cs_last, (Q, gw)).T)
        dd_ref[...] += jnp.sum(dY * X, axis=0, keepdims=True)
        dX = d_ref[...] * dY
        ey = ecs * dY
        dcs = ey * _dot_nt(Cm, sprev)
        dC = _dot_nn(ey, sprev)
        ds_scr[...] = cd * dsn + _dot_tn(ey, Cm)
        wmat = _dot_nt(Bm, dsn)
        dxdt = dec * wmat
        xd = xdt * dec
        dB = _dot_nn(xd, dsn)
        ddec = xdt * wmat * dec
        dcs = dcs - ddec
        dlast = jnp.sum(ddec, axis=0, keepdims=True)
        qmat = dsn * sprev * cd
        cb = _dot_nt(Cm, Bm)
        dcb = jnp.zeros((Q, Q), F32)
        dcs_rep = jnp.zeros((Q, gw), F32)
        dtx_rep = jnp.zeros((Q, gw), F32)
        for r in range(R):
            hm = (lane >= r * P) & (lane < (r + 1) * P)
            dec_l, tril = _head_decay(cs, hm)
            dyr = jnp.where(hm, dY, 0.0)
            gmat = jnp.where(tril, _dot_nt(dyr, xdt), 0.0)
            dcb = dcb + gmat * dec_l
            e = gmat * cb * dec_l
            v = (jnp.sum(e, axis=1, keepdims=True) - jnp.sum(e.T, axis=1, keepdims=True)
                 + jnp.sum(jnp.where(hm, dcs, 0.0), axis=1, keepdims=True))
            dxdt = dxdt + _dot_tn(cb * dec_l, dyr)
            hm1 = (lane1 >= r * P) & (lane1 < (r + 1) * P)
            t_last = (jnp.sum(jnp.where(hm1, dlast, 0.0), axis=1, keepdims=True)
                      + jnp.sum(jnp.where((srow >= r * P) & (srow < (r + 1) * P), qmat, 0.0), keepdims=True))
            dcs_rep = dcs_rep + jnp.where(hm, v, 0.0) + jnp.where(hm & (row == Q - 1), t_last, 0.0)
        for r in range(R):
            hm = (lane >= r * P) & (lane < (r + 1) * P)
            w_r = jnp.sum(jnp.where(hm, dxdt * X, 0.0), axis=1, keepdims=True)
            dtx_rep = dtx_rep + jnp.where(hm, w_r, 0.0)
        dadt = _rev_cumsum_rows(dcs_rep)
        ddt = a * dadt + dtx_rep
        dalog_ref[...] += jnp.sum(dt * dadt, axis=0, keepdims=True) * a
        draw = ddt * _sigmoid(pre)
        ddt_ref[...] = draw
        dbias_ref[...] += jnp.sum(draw, axis=0, keepdims=True)
        dx_ref[...] = dX + dxdt * dt
        db_ref[...] = dB + _dot_tn(dcb, Cm)
        dc_ref[...] = dC + _dot_nn(dcb, Bm)

    x_spec, b_spec, c_spec, p_spec, s_spec = _ssd_specs(S, d_inner, gw, nc, True)
    n_spec = pl.BlockSpec((Q, N), lambda g, c: (nc - 1 - c, g))
    gshape = jax.ShapeDtypeStruct((G, 1, gw), F32)
    return pl.pallas_call(
        body, grid=(G, nc),
        in_specs=[x_spec, b_spec, c_spec, x_spec, p_spec, p_spec, p_spec, s_spec, x_spec],
        out_specs=[x_spec, n_spec, n_spec, x_spec, p_spec, p_spec, p_spec],
        out_shape=[jax.ShapeDtypeStruct((S, d_inner), F32), jax.ShapeDtypeStruct((S, G * N), F32),
                   jax.ShapeDtypeStruct((S, G * N), F32), jax.ShapeDtypeStruct((S, d_inner), F32),
                   gshape, gshape, gshape],
        scratch_shapes=[pltpu.VMEM((gw, N), F32)],
        compiler_params=_cp("parallel", "arbitrary"), name=name)(
            xbc, xbc, xbc, dtr, bias, alog, dsk, sprev_all, dy)


def _gnorm_fwd(y, zx, w, name):
    S, d_inner = y.shape
    G = SSM_GROUPS
    gw = d_inner // G
    T = _pick(S, 256, 8)

    def body(y_ref, z_ref, w_ref, o_ref):
        for k in range(G):
            sl = slice(k * gw, (k + 1) * gw)
            z = z_ref[:, sl]
            gk = y_ref[:, sl] * z * _sigmoid(z)
            r = lax.rsqrt(jnp.mean(gk * gk, axis=-1, keepdims=True) + NORM_EPS)
            o_ref[:, sl] = (gk * r * w_ref[:, sl]).astype(o_ref.dtype)

    row = pl.BlockSpec((T, d_inner), lambda i: (i, 0))
    vec = pl.BlockSpec((1, d_inner), lambda i: (0, 0))
    return pl.pallas_call(body, grid=(S // T,), in_specs=[row, row, vec], out_specs=row,
                          out_shape=jax.ShapeDtypeStruct((S, d_inner), BF16),
                          compiler_params=_cp("parallel"), name=name)(y, zx, w.reshape(1, d_inner))


def _gnorm_bwd(y, zx, w, dout, name):
    S, d_inner = y.shape
    G = SSM_GROUPS
    gw = d_inner // G
    T = _pick(S, 256, 8)

    def body(y_ref, z_ref, w_ref, d_ref, dy_ref, dz_ref, dw_ref):
        @pl.when(pl.program_id(0) == 0)
        def _():
            dw_ref[...] = jnp.zeros_like(dw_ref)

        for k in range(G):
            sl = slice(k * gw, (k + 1) * gw)
            z, yv, d = z_ref[:, sl], y_ref[:, sl], d_ref[:, sl]
            sg = _sigmoid(z)
            sz = z * sg
            gk = yv * sz
            r = lax.rsqrt(jnp.mean(gk * gk, axis=-1, keepdims=True) + NORM_EPS)
            gh = gk * r
            dw_ref[:, sl] += jnp.sum(d * gh, axis=0, keepdims=True)
            dg = d * w_ref[:, sl]
            dgk = r * (dg - gh * jnp.mean(dg * gh, axis=-1, keepdims=True))
            dy_ref[:, sl] = dgk * sz
            dz_ref[:, sl] = dgk * yv * sg * (1.0 + z * (1.0 - sg))

    row = pl.BlockSpec((T, d_inner), lambda i: (i, 0))
    vec = pl.BlockSpec((1, d_inner), lambda i: (0, 0))
    return pl.pallas_call(
        body, grid=(S // T,), in_specs=[row, row, vec, row], out_specs=[row, row, vec],
        out_shape=[jax.ShapeDtypeStruct((S, d_inner), F32)] * 2 + [jax.ShapeDtypeStruct((1, d_inner), F32)],
        compiler_params=_cp("arbitrary"), name=name)(y, zx, w.reshape(1, d_inner), dout)


def _adam_math(g, w, m, v):
    m = ADAM_B1 * m + (1.0 - ADAM_B1) * g
    v = ADAM_B2 * v + (1.0 - ADAM_B2) * (g * g)
    m_hat = m / (1.0 - ADAM_B1 ** ADAM_STEP)
    v_hat = v / (1.0 - ADAM_B2 ** ADAM_STEP)
    delta = -ADAM_LR * (m_hat / (jnp.sqrt(v_hat) + ADAM_EPS) + ADAM_WD * w)
    return delta, m, v


def _adamw_big(own, sib, w, m, v, layer, prev, name):
    L, A, Bc = w.shape
    T = _pick(A, max(8, (1 << 19) // (4 * Bc)), 16)

    def body(o_ref, s_ref, w_ref, m_ref, v_ref, *rest):
        g_ref, d_ref, nm_ref, nv_ref = rest[-4:]
        so = o_ref[0].astype(F32)
        ss = s_ref[0].astype(F32)
        for k in range(1, N_CHIPS):
            so = so + o_ref[k].astype(F32)
            ss = ss + s_ref[k].astype(F32)
        g = so + ss
        delta, nm, nv = _adam_math(g, w_ref[...], m_ref[...], v_ref[...])
        g_ref[...] = g
        d_ref[...] = delta
        nm_ref[...] = nm
        nv_ref[...] = nv

    part = pl.BlockSpec((N_CHIPS, T, Bc), lambda i: (0, i, 0))
    blk = pl.BlockSpec((None, T, Bc), lambda i: (layer, i, 0))
    shp = jax.ShapeDtypeStruct(w.shape, F32)
    in_specs, args, kwargs = [part, part, blk, blk, blk], [own, sib, w, m, v], {}
    if prev is not None:
        in_specs += [pl.BlockSpec(memory_space=pl.ANY)] * 4
        args += list(prev)
        kwargs["input_output_aliases"] = {5 + q: q for q in range(4)}
    return pl.pallas_call(body, grid=(A // T,), in_specs=in_specs, out_specs=[blk] * 4, out_shape=[shp] * 4,
                          compiler_params=_cp("parallel"), name=name, **kwargs)(*args)


def _sum_devices(parts, name):
    _, R, C = parts.shape

    def body(p_ref, o_ref):
        acc = p_ref[0]
        for k in range(1, N_DEV):
            acc = acc + p_ref[k]
        o_ref[...] = acc

    return pl.pallas_call(body, out_shape=jax.ShapeDtypeStruct((R, C), F32), name=name)(parts)


def _adamw_small(g, w, m, v, name):
    def body(g_ref, w_ref, m_ref, v_ref, d_ref, nm_ref, nv_ref):
        delta, nm, nv = _adam_math(g_ref[...], w_ref[...], m_ref[...], v_ref[...])
        d_ref[...] = delta
        nm_ref[...] = nm
        nv_ref[...] = nv

    shp = jax.ShapeDtypeStruct(g.shape, F32)
    return pl.pallas_call(body, out_shape=[shp] * 3, name=name)(g, w, m, v)


PACK_COLS = 1024


def _pack(arrs):
    flat = jnp.concatenate([a.reshape(-1).astype(F32) for a in arrs])
    n = flat.shape[0]
    rows = -(-n // (8 * PACK_COLS)) * 8
    return jnp.pad(flat, (0, rows * PACK_COLS - n)).reshape(rows, PACK_COLS)


def _unpack(packed, shapes):
    flat = packed.reshape(-1)
    out, off = [], 0
    for s in shapes:
        n = math.prod(s)
        out.append(flat[off:off + n].reshape(s))
        off += n
    return out


def _shard_ref(ref, kind, k, n):
    if kind == "col":
        return ref.at[:, pl.ds(pl.multiple_of(k * n, 128), n)]
    if kind == "row":
        return ref.at[pl.ds(pl.multiple_of(k * n, 16), n), :]
    return ref.at[k]


def _chip_peers():
    x, y, c = lax.axis_index("x"), lax.axis_index("y"), lax.axis_index("c")
    return x, y, c, [(1 - x, y), (x, 1 - y), (1 - x, 1 - y)]


class _Exchange:
    def __init__(self, mode, items):
        self.mode, self.items = mode, items
        self.arrays = []
        for it in items:
            if not any(it[0] is a for a in self.arrays):
                self.arrays.append(it[0])
        self.src_idx = [next(i for i, a in enumerate(self.arrays) if a is it[0]) for it in items]
        self.out_shapes = [jax.ShapeDtypeStruct(it[-1], it[0].dtype) for it in items]
        n = len(items)
        if mode == "swap":
            self.scratch = [pltpu.SemaphoreType.DMA((n,)), pltpu.SemaphoreType.DMA((n,))]
        else:
            self.scratch = [pltpu.SemaphoreType.DMA((3 * n,)), pltpu.SemaphoreType.DMA((3 * n,)),
                            pltpu.SemaphoreType.DMA((n,))]

    def _copies(self, ins, outs, sems):
        if self.mode == "swap":
            send_sems, recv_sems = sems
            x, y, c = lax.axis_index("x"), lax.axis_index("y"), lax.axis_index("c")
            sent = [pltpu.make_async_remote_copy(
                src_ref=ins[self.src_idx[t]], dst_ref=outs[t], send_sem=send_sems.at[t], recv_sem=recv_sems.at[t],
                device_id=(x, y, 1 - c), device_id_type=MESH) for t in range(len(self.items))]
            return [], sent, sent
        send_sems, recv_sems, loc_sems = sems
        x, y, c, peers = _chip_peers()
        me = 2 * x + y
        local, sent, arriving = [], [], []
        for t, it in enumerate(self.items):
            src_arr = ins[self.src_idx[t]]
            if self.mode == "gather":
                _, layer, kind, n, _ = it
                src = src_arr if layer is None else src_arr.at[layer]
                src_for = lambda k: src
                dst_from = lambda k: _shard_ref(outs[t], kind, k, n)
            else:
                _, kind, n, _ = it
                src_for = lambda k: _shard_ref(src_arr, kind, k, n)
                dst_from = lambda k: outs[t].at[k]
            local.append(pltpu.make_async_copy(src_for(me), dst_from(me), loc_sems.at[t]))
            for j, (px, py) in enumerate(peers):
                pk = 2 * px + py
                args = dict(send_sem=send_sems.at[3 * t + j], recv_sem=recv_sems.at[3 * t + j],
                            device_id=(px, py, c), device_id_type=MESH)
                sent.append(pltpu.make_async_remote_copy(src_ref=src_for(pk), dst_ref=dst_from(me), **args))
                arriving.append(pltpu.make_async_remote_copy(src_ref=src_for(pk), dst_ref=dst_from(pk), **args))
        return local, sent, arriving

    def start(self, ins, outs, sems):
        local, sent, arriving = self._copies(ins, outs, sems)
        for cp in local + sent:
            cp.start()
        for cp in arriving:
            cp._used = True

    def wait(self, ins, outs, sems):
        local, sent, arriving = self._copies(ins, outs, sems)
        for cp in arriving:
            cp.wait_recv()
        for cp in sent:
            cp.wait_send()
        for cp in local:
            cp.wait()


class _Multi:
    def __init__(self, parts):
        self.parts = parts
        self.arrays = [a for p in parts for a in p.arrays]
        self.out_shapes = [s for p in parts for s in p.out_shapes]
        self.scratch = [s for p in parts for s in p.scratch]

    def _split(self, ins, outs, sems):
        i = o = s = 0
        for p in self.parts:
            ni, no, ns = len(p.arrays), len(p.out_shapes), len(p.scratch)
            yield p, ins[i:i + ni], outs[o:o + no], sems[s:s + ns]
            i, o, s = i + ni, o + no, s + ns

    def start(self, ins, outs, sems):
        for p, a, b, c in self._split(ins, outs, sems):
            p.start(a, b, c)

    def wait(self, ins, outs, sems):
        for p, a, b, c in self._split(ins, outs, sems):
            p.wait(a, b, c)


def _run_exchange(ex, name):
    nin, nout = len(ex.arrays), len(ex.out_shapes)

    def body(*refs):
        ins, outs, sems = refs[:nin], refs[nin:nin + nout], refs[nin + nout:]
        ex.start(ins, outs, sems)
        ex.wait(ins, outs, sems)

    anyspec = pl.BlockSpec(memory_space=pl.ANY)
    return pl.pallas_call(body, in_specs=[anyspec] * nin, out_specs=[anyspec] * nout, out_shape=ex.out_shapes,
                          scratch_shapes=ex.scratch, name=name)(*ex.arrays)


def _all_gather_devices(v, name):
    def body(v_ref, o_ref, send_sems, recv_sems, loc_sem):
        x, y, c = lax.axis_index("x"), lax.axis_index("y"), lax.axis_index("c")
        me = 4 * x + 2 * y + c
        lc = pltpu.make_async_copy(v_ref, o_ref.at[me], loc_sem)
        lc.start()
        rel = [(bx, by, bc) for bx in (0, 1) for by in (0, 1) for bc in (0, 1)][1:]
        copies = []
        for j, (bx, by, bc) in enumerate(rel):
            px, py, pc = x ^ bx, y ^ by, c ^ bc
            copies.append((pltpu.make_async_remote_copy(
                src_ref=v_ref, dst_ref=o_ref.at[me], send_sem=send_sems.at[j], recv_sem=recv_sems.at[j],
                device_id=(px, py, pc), device_id_type=MESH), 4 * px + 2 * py + pc))
        for cp, _ in copies:
            cp.start()
        for j, (cp, pid) in enumerate(copies):
            pltpu.make_async_remote_copy(
                src_ref=v_ref, dst_ref=o_ref.at[pid], send_sem=send_sems.at[j], recv_sem=recv_sems.at[j],
                device_id=(x, y, c), device_id_type=MESH).wait_recv()
        for cp, _ in copies:
            cp.wait_send()
        lc.wait()

    anyspec = pl.BlockSpec(memory_space=pl.ANY)
    return pl.pallas_call(
        body, in_specs=[anyspec], out_specs=anyspec,
        out_shape=jax.ShapeDtypeStruct((N_DEV,) + v.shape, v.dtype),
        scratch_shapes=[pltpu.SemaphoreType.DMA((N_DEV - 1,)), pltpu.SemaphoreType.DMA((N_DEV - 1,)),
                        pltpu.SemaphoreType.DMA(())],
        name=name)(v)


BIG = ("attn_w_qkv", "attn_w_o", "ssm_w_in", "ssm_w_out", "ffn_w_up", "ffn_w_down")
BIG_KIND = {"attn_w_qkv": "col", "attn_w_o": "row", "ssm_w_in": "lead", "ssm_w_out": "row",
            "ffn_w_up": "col", "ffn_w_down": "row"}
SMALL_SHARDED = {"ssm_conv_w": 2, "ssm_conv_b": 1, "ssm_norm_w": 1, "ffn_conv_w": 2}
SMALL = ("mix_norm_w", "ssm_conv_w", "ssm_conv_b", "ssm_dt_bias", "ssm_a_log", "ssm_d", "ssm_norm_w",
         "ffn_norm_w", "ffn_conv_w", "ffn_conv_b", "final_norm_w")
WEIGHTS = ("mix_norm_w", "attn_w_qkv", "attn_w_o", "ssm_w_in", "ssm_conv_w", "ssm_conv_b", "ssm_dt_bias",
           "ssm_a_log", "ssm_d", "ssm_norm_w", "ssm_w_out", "ffn_norm_w", "ffn_w_up", "ffn_conv_w",
           "ffn_conv_b", "ffn_w_down", "final_norm_w")


def _shard_extent(name, shape):
    _, a, b = shape
    return {"col": b, "row": a, "lead": 1}[BIG_KIND[name]]


def _gather_item(w16, name, layer):
    _, a, b = w16.shape
    kind = BIG_KIND[name]
    full = {"col": (a, N_CHIPS * b), "row": (N_CHIPS * a, b), "lead": (N_CHIPS, a, b)}[kind]
    return (w16, layer, kind, _shard_extent(name, w16.shape), full)


IN, OUT, UP, DOWN = range(4)
FETCH_PLAN = {
    ("A", 0): [(0, OUT), (0, UP), (0, DOWN), (1, OUT)], ("U", 0): [(1, IN)], ("D", 0): [(1, UP)],
    ("A", 1): [(2, IN)], ("U", 1): [(1, DOWN), (2, OUT)], ("D", 1): [(2, UP)],
    ("A", 2): [(3, IN), (2, DOWN), (3, OUT)], ("U", 2): [(3, UP)], ("D", 2): [(3, DOWN)],
}


def _layer_weights(i):
    j = i // 2
    mixer = [("attn_w_qkv", j), ("attn_w_o", j)] if i % 2 == 0 else [("ssm_w_in", j), ("ssm_w_out", j)]
    return mixer + [("ffn_w_up", i), ("ffn_w_down", i)]


def kernel(x, mix_norm_w, attn_w_qkv, attn_w_o, ssm_w_in, ssm_conv_w, ssm_conv_b, ssm_dt_bias, ssm_a_log, ssm_d, ssm_norm_w, ssm_w_out, ffn_norm_w, ffn_w_up, ffn_conv_w, ffn_conv_b, ffn_w_down, final_norm_w, loss_target, m_mix_norm_w, m_attn_w_qkv, m_attn_w_o, m_ssm_w_in, m_ssm_conv_w, m_ssm_conv_b, m_ssm_dt_bias, m_ssm_a_log, m_ssm_d, m_ssm_norm_w, m_ssm_w_out, m_ffn_norm_w, m_ffn_w_up, m_ffn_conv_w, m_ffn_conv_b, m_ffn_w_down, m_final_norm_w, v_mix_norm_w, v_attn_w_qkv, v_attn_w_o, v_ssm_w_in, v_ssm_conv_w, v_ssm_conv_b, v_ssm_dt_bias, v_ssm_a_log, v_ssm_d, v_ssm_norm_w, v_ssm_w_out, v_ffn_norm_w, v_ffn_w_up, v_ffn_conv_w, v_ffn_conv_b, v_ffn_w_down, v_final_norm_w):
    W = dict(mix_norm_w=mix_norm_w, attn_w_qkv=attn_w_qkv, attn_w_o=attn_w_o, ssm_w_in=ssm_w_in,
             ssm_conv_w=ssm_conv_w, ssm_conv_b=ssm_conv_b, ssm_dt_bias=ssm_dt_bias, ssm_a_log=ssm_a_log,
             ssm_d=ssm_d, ssm_norm_w=ssm_norm_w, ssm_w_out=ssm_w_out, ffn_norm_w=ffn_norm_w, ffn_w_up=ffn_w_up,
             ffn_conv_w=ffn_conv_w, ffn_conv_b=ffn_conv_b, ffn_w_down=ffn_w_down, final_norm_w=final_norm_w)
    M = dict(mix_norm_w=m_mix_norm_w, attn_w_qkv=m_attn_w_qkv, attn_w_o=m_attn_w_o, ssm_w_in=m_ssm_w_in,
             ssm_conv_w=m_ssm_conv_w, ssm_conv_b=m_ssm_conv_b, ssm_dt_bias=m_ssm_dt_bias, ssm_a_log=m_ssm_a_log,
             ssm_d=m_ssm_d, ssm_norm_w=m_ssm_norm_w, ssm_w_out=m_ssm_w_out, ffn_norm_w=m_ffn_norm_w,
             ffn_w_up=m_ffn_w_up, ffn_conv_w=m_ffn_conv_w, ffn_conv_b=m_ffn_conv_b, ffn_w_down=m_ffn_w_down,
             final_norm_w=m_final_norm_w)
    V = dict(mix_norm_w=v_mix_norm_w, attn_w_qkv=v_attn_w_qkv, attn_w_o=v_attn_w_o, ssm_w_in=v_ssm_w_in,
             ssm_conv_w=v_ssm_conv_w, ssm_conv_b=v_ssm_conv_b, ssm_dt_bias=v_ssm_dt_bias, ssm_a_log=v_ssm_a_log,
             ssm_d=v_ssm_d, ssm_norm_w=v_ssm_norm_w, ssm_w_out=v_ssm_w_out, ffn_norm_w=v_ffn_norm_w,
             ffn_w_up=v_ffn_w_up, ffn_conv_w=v_ffn_conv_w, ffn_conv_b=v_ffn_conv_b, ffn_w_down=v_ffn_w_down,
             final_norm_w=v_final_norm_w)

    S, D = x.shape[1], x.shape[2]
    xs = x.reshape(S, D)
    tgt = loss_target.reshape(S, D)
    depth = mix_norm_w.shape[0]
    heads = attn_w_o.shape[1] * N_CHIPS // HEAD_DIM
    AW = heads * HEAD_DIM
    d_inner = ssm_w_out.shape[1] * N_CHIPS
    ssm_heads = d_inner // SSM_HEAD_DIM
    G, P, N = SSM_GROUPS, SSM_HEAD_DIM, SSM_STATE
    gw = d_inner // G
    conv_dim = d_inner + 2 * G * N
    in_w = d_inner + conv_dim + ssm_heads
    in_pad = -(-in_w // 128) * 128
    shard_in = ssm_w_in.shape[2]
    xi, yi = lax.axis_index("x"), lax.axis_index("y")
    chip = 2 * xi + yi

    W16 = {n: W[n].astype(BF16) for n in BIG}
    full = {}

    def land(keys, outs):
        for (n, l), o in zip(keys, outs):
            if n == "ssm_w_in":
                o = jnp.pad(jnp.concatenate([o[k] for k in range(N_CHIPS)], axis=1), ((0, 0), (0, in_pad - in_w)))
            full[(n, l)] = o

    def fwd_mm(a, b, name, fetch=(), resid=None):
        if not fetch:
            return _matmul(a, b, "nn", F32, name, resid=resid)
        ex = _Exchange("gather", [_gather_item(W16[n], n, l) for n, l in fetch])
        out, got = _matmul(a, b, "nn", F32, name, resid=resid, carry=ex)
        land(fetch, got)
        return out

    sm_names = list(SMALL_SHARDED)
    packed = _pack([W[n] for n in sm_names])
    assert depth == 4, "FETCH_PLAN is written for four layers"
    lw = [_layer_weights(i) for i in range(depth)]
    first = lw[0][:1]
    got = _run_exchange(_Exchange("gather", [_gather_item(W16[n], n, l) for n, l in first]
                                  + [(packed, None, "lead", 1, (N_CHIPS,) + packed.shape)]), "gather_first")
    land(first, got[:-1])
    per_chip = [_unpack(got[-1][k], [W[n].shape for n in sm_names]) for k in range(N_CHIPS)]
    for q, n in enumerate(sm_names):
        full[n] = jnp.concatenate([per_chip[k][q] for k in range(N_CHIPS)], axis=SMALL_SHARDED[n])
    tab = _perm_tokens(_rope_table(S))

    def rep_heads(p):
        return jnp.repeat(p, P).reshape(G, 1, gw)

    saved = []
    cur = xs
    for i in range(depth):
        j = i // 2
        fetch = {c: [lw[l][r] for l, r in FETCH_PLAN.get((c, i), ())] for c in "AUD"}
        sv = {"x_in": cur}
        h = _rms_fwd(cur, mix_norm_w[i], f"mix_norm_fwd_{i}")
        sv["h"] = h
        if i % 2 == 0:
            h = _perm_tokens(h)
            sv["h"] = h
            qkv = fwd_mm(h, full[("attn_w_qkv", j)], f"qkv_fwd_{i}", fetch=fetch["A"])
            og = [_attn_fwd(qkv, tab, g, heads, f"attn_fwd_{i}_{g}") for g in range(3)]
            o, lse = _attn_combine([a for a, _ in og], [b for _, b in og], f"attn_combine_{i}")
            mixed = _unperm_tokens(fwd_mm(o, full[("attn_w_o", j)], f"attn_out_fwd_{i}"))
            sv.update(qkv=qkv, o=o, lse=lse)
        else:
            zx = fwd_mm(h, full[("ssm_w_in", j)], f"ssm_in_fwd_{i}", fetch=fetch["A"])
            conv = _ssm_conv_fwd(zx, full["ssm_conv_w"][j], full["ssm_conv_b"][j], d_inner, conv_dim,
                                 f"ssm_conv_fwd_{i}")
            dtr = jnp.repeat(zx[:, d_inner + conv_dim:in_w], P, axis=1)
            prm = [rep_heads(p[j]) for p in (ssm_dt_bias, ssm_a_log, ssm_d)]
            y, sprev = _ssd_fwd(conv, dtr, *prm, d_inner, f"ssd_fwd_{i}")
            gated = _gnorm_fwd(y, zx, full["ssm_norm_w"][j], f"ssm_norm_fwd_{i}")
            cur = fwd_mm(gated, full[("ssm_w_out", j)], f"ssm_out_fwd_{i}", resid=cur)
            sv.update(zx=zx, conv=conv, dtr=dtr, prm=prm, y=y, sprev=sprev, gated=gated)
            mixed = None
        if mixed is None:
            h2 = _rms_fwd(cur, ffn_norm_w[i], f"ffn_norm_fwd_{i}")
        else:
            cur, h2 = _rms_fwd(cur, ffn_norm_w[i], f"ffn_norm_fwd_{i}", add=mixed)
        sv["x_mid"] = cur
        up = fwd_mm(h2, full[("ffn_w_up", i)], f"ffn_up_fwd_{i}", fetch=fetch["U"])
        u2, act = _ffn_conv_fwd(up, full["ffn_conv_w"][i], ffn_conv_b[i], f"ffn_conv_fwd_{i}")
        cur = fwd_mm(act, full[("ffn_w_down", i)], f"ffn_down_fwd_{i}", fetch=fetch["D"], resid=cur)
        sv.update(h2=h2, up=up, u2=u2, act=act)
        saved.append(sv)

    dx, d_final, loss_part = _loss_head(cur, final_norm_w, tgt, "loss_head")
    gbig, recv = {}, {}
    gs = {n: [None] * W[n].shape[0] for n in SMALL if n != "final_norm_w"}

    def scatter_ex(keys):
        return _Exchange("scatter", [(gbig[(n, l)], BIG_KIND[n], _shard_extent(n, W[n].shape),
                                      (N_CHIPS,) + W[n].shape[1:]) for n, l in keys])

    sib = {}

    def swap_ex(keys):
        return _Exchange("swap", [(recv[k], recv[k].shape) for k in keys])

    def bwd_mm(a, b, mode, dtype, name, send=(), swap=()):
        if not send and not swap:
            return _matmul(a, b, mode, dtype, name)
        parts = ([scatter_ex(send)] if send else []) + ([swap_ex(swap)] if swap else [])
        out, got = _matmul(a, b, mode, dtype, name, carry=_Multi(parts))
        recv.update(zip(send, got[:len(send)]))
        sib.update(zip(swap, got[len(send):]))
        return out

    late = []
    for i in reversed(range(depth)):
        j = i // 2
        sv = saved[i]
        k_in, k_out, k_up, k_down = _layer_weights(i)
        dact = _matmul(dx, full[k_down], "nt", F32, f"ffn_down_dgrad_{i}")
        gbig[k_down] = bwd_mm(sv["act"], dx, "tn", BF16, f"ffn_down_wgrad_{i}", swap=late)
        dup, dcw, dcb = _ffn_conv_bwd(sv["u2"], dact, sv["up"], full["ffn_conv_w"][i], f"ffn_conv_bwd_{i}")
        gs["ffn_conv_w"][i], gs["ffn_conv_b"][i] = dcw, dcb[0]
        dh2 = bwd_mm(dup, full[k_up], "nt", F32, f"ffn_up_dgrad_{i}", send=[k_down])
        gbig[k_up] = bwd_mm(sv["h2"], dup, "tn", BF16, f"ffn_up_wgrad_{i}", swap=[k_down])
        dx, dnw = _rms_bwd(sv["x_mid"], ffn_norm_w[i], dh2, dx, f"ffn_norm_bwd_{i}")
        gs["ffn_norm_w"][i] = dnw[0]
        if i % 2 == 0:
            dxp = _perm_tokens(dx)
            do = _matmul(dxp, full[k_out], "nt", F32, f"attn_out_dgrad_{i}")
            gbig[k_out] = _matmul(sv["o"], dxp, "tn", BF16, f"attn_out_wgrad_{i}")
            dqkv = None
            for g in range(3):
                dqkv = _attn_bwd(sv["qkv"], tab, sv["o"], sv["lse"], do, dqkv, g, heads, f"attn_bwd_{i}_{g}")
            gbig[k_in] = bwd_mm(sv["h"], dqkv, "tn", BF16, f"qkv_wgrad_{i}", send=[k_up, k_out])
            dh = _unperm_tokens(bwd_mm(dqkv, full[k_in], "nt", F32, f"qkv_dgrad_{i}", send=[k_in],
                                       swap=[k_up, k_out]))
        else:
            dgated = _matmul(dx, full[k_out], "nt", F32, f"ssm_out_dgrad_{i}")
            gbig[k_out] = _matmul(sv["gated"], dx, "tn", BF16, f"ssm_out_wgrad_{i}")
            dy, dz, dgw = _gnorm_bwd(sv["y"], sv["zx"], full["ssm_norm_w"][j], dgated, f"ssm_norm_bwd_{i}")
            gs["ssm_norm_w"][j] = dgw[0]
            dxs_, dbm, dcm, ddtr, dbias, dalog, ddsk = _ssd_bwd(
                sv["conv"], sv["dtr"], *sv["prm"], sv["sprev"], dy, d_inner, f"ssd_bwd_{i}")
            gs["ssm_dt_bias"][j] = dbias.reshape(-1)[::P]
            gs["ssm_a_log"][j] = dalog.reshape(-1)[::P]
            gs["ssm_d"][j] = ddsk.reshape(ssm_heads, P).sum(axis=1)
            ddt = jnp.pad(ddtr[:, ::P], ((0, 0), (0, in_pad - in_w)))
            dzx, dcw, dcb = _ssm_conv_bwd(dxs_, dbm, dcm, sv["conv"], sv["zx"], dz, ddt, full["ssm_conv_w"][j],
                                          d_inner, f"ssm_conv_bwd_{i}")
            gs["ssm_conv_w"][j], gs["ssm_conv_b"][j] = dcw, dcb[0]
            dwin = bwd_mm(sv["h"], dzx, "tn", BF16, f"ssm_in_wgrad_{i}", send=[k_up, k_out])
            gbig[k_in] = jnp.stack([dwin[:, k * shard_in:(k + 1) * shard_in] for k in range(N_CHIPS)])
            dh = bwd_mm(dzx, full[k_in], "nt", F32, f"ssm_in_dgrad_{i}", send=[k_in], swap=[k_up, k_out])
        late = [k_in]
        dx, dnw = _rms_bwd(sv["x_in"], mix_norm_w[i], dh, dx, f"mix_norm_bwd_{i}")
        gs["mix_norm_w"][i] = dnw[0]
    grad_x = dx.reshape(x.shape)

    sib.update(zip(late, _run_exchange(swap_ex(late), "swap_last")))

    small_full = [jnp.stack(gs[n]) if n != "final_norm_w" else d_final[0] for n in SMALL]
    small_full.append(loss_part[0, 0:1])
    small_shapes = [a.shape for a in small_full]
    summed = _sum_devices(_all_gather_devices(_pack(small_full), "gather_small_grads"), "sum_small_grads")
    small_g = _unpack(summed, small_shapes)
    loss = small_g[-1][0]
    gsm = {}
    for n, g in zip(SMALL, small_g[:-1]):
        if n in SMALL_SHARDED:
            ax = SMALL_SHARDED[n]
            ext = W[n].shape[ax]
            g = lax.dynamic_slice_in_dim(g, chip * ext, ext, axis=ax)
        gsm[n] = g

    out_g, out_d, out_m, out_v = {}, {}, {}, {}
    for name in BIG:
        outs = None
        for l in range(W[name].shape[0]):
            outs = _adamw_big(recv[(name, l)], sib[(name, l)], W[name], M[name], V[name], l, outs,
                              f"adamw_{name}_{l}")
        out_g[name], out_d[name], out_m[name], out_v[name] = outs
    shapes = [W[n].shape for n in SMALL]
    pd, pm, pv = _adamw_small(_pack([gsm[n] for n in SMALL]), _pack([W[n] for n in SMALL]),
                              _pack([M[n] for n in SMALL]), _pack([V[n] for n in SMALL]), "adamw_small")
    for n, d_, m_, v_ in zip(SMALL, _unpack(pd, shapes), _unpack(pm, shapes), _unpack(pv, shapes)):
        out_g[n], out_d[n], out_m[n], out_v[n] = gsm[n], d_, m_, v_

    return (loss, grad_x, *[out_g[n] for n in WEIGHTS], *[out_d[n] for n in WEIGHTS],
            *[out_m[n] for n in WEIGHTS], *[out_v[n] for n in WEIGHTS])
```

```python
import functools
import math

import jax
import jax.numpy as jnp
from jax import lax
from jax.experimental import pallas as pl
from jax.experimental.pallas import tpu as pltpu

F32 = jnp.float32
BF16 = jnp.bfloat16
MESH = pl.DeviceIdType.MESH

NORM_EPS = 1e-5
HEAD_DIM = 128
ATTN_BLOCK = 128
ATTN_DILATIONS = (1, 4, 16)
ATTN_WINDOWS = (128, 512, 2048)
PERM = 16
ROPE_THETA = 500000.0
ROPE_HALF = HEAD_DIM // 8
SSM_HEAD_DIM = 64
SSM_STATE = 128
SSM_GROUPS = 8
SSM_CHUNK = 128
NEG = -1e30

ADAM_LR = 0.001
ADAM_B1 = 0.9
ADAM_B2 = 0.999
ADAM_EPS = 1e-08
ADAM_WD = 0.01
ADAM_STEP = 10

VMEM_LIMIT_BYTES = 48 * 1024 * 1024
N_CHIPS = 4
N_DEV = 8


def _cp(*sem):
    return pltpu.CompilerParams(dimension_semantics=sem, vmem_limit_bytes=VMEM_LIMIT_BYTES)


def _pick(n, pref, mult=128):
    best = None
    t = mult
    while t <= min(n, pref):
        if n % t == 0:
            best = t
        t += mult
    return n if best is None else best


def _sigmoid(x):
    return 1.0 / (1.0 + jnp.exp(-x))


def _silu(x):
    return x * _sigmoid(x)


def _softplus(x):
    u = jnp.exp(-jnp.abs(x))
    w = 1.0 + u
    log1p = jnp.where(w == 1.0, u, jnp.log(w) * (u / jnp.where(w == 1.0, 1.0, w - 1.0)))
    return jnp.maximum(x, 0.0) + log1p


def _dot(a, b, dims):
    return lax.dot_general(a.astype(BF16), b.astype(BF16), (dims, ((), ())),
                           preferred_element_type=F32)


def _dot_nn(a, b):
    return _dot(a, b, ((1,), (0,)))


def _dot_nt(a, b):
    return _dot(a, b, ((1,), (1,)))


def _dot_tn(a, b):
    return _dot(a, b, ((0,), (0,)))


MATMUL_VMEM_BYTES = 36 * 1024 * 1024
MATMUL_TILES = (2048, 1536, 1408, 1152, 1024, 896, 768, 640, 512, 384, 256, 128)


def _matmul_tiles(M, N, K, a_bytes, b_bytes, o_bytes, has_resid):
    best = None
    for tm in [t for t in MATMUL_TILES if M % t == 0] or [M]:
        for tn in [t for t in MATMUL_TILES if N % t == 0] or [N]:
            for tk in [t for t in MATMUL_TILES if K % t == 0 and t <= 1408] or [K]:
                nk, gm, gn = K // tk, M // tm, N // tn
                vmem = 2 * (tm * tk * a_bytes + tk * tn * b_bytes + tm * tn * o_bytes)
                vmem += (2 * tm * tn * 4 if has_resid else 0) + (tm * tn * 4 if nk > 1 else 0)
                if vmem > MATMUL_VMEM_BYTES:
                    continue
                a_reads = 1 if nk == 1 else gn
                b_reads = 1 if (nk == 1 and gn == 1) else gm
                traffic = M * K * a_bytes * a_reads + K * N * b_bytes * b_reads + M * N * o_bytes
                key = (traffic, gm * gn * nk)
                if best is None or key < best[0]:
                    best = (key, (tm, tn, tk))
    assert best is not None, (M, N, K)
    return best[1]


def _matmul(a, b, mode, out_dtype, name, resid=None, carry=None):
    if mode == "nn":
        (M, K), (K2, N) = a.shape, b.shape
    elif mode == "nt":
        (M, K), (N, K2) = a.shape, b.shape
    else:
        (K, M), (K2, N) = a.shape, b.shape
    assert K == K2, (a.shape, b.shape, mode)
    tm, tn, tk = _matmul_tiles(M, N, K, a.dtype.itemsize, b.dtype.itemsize, jnp.dtype(out_dtype).itemsize,
                               resid is not None)
    nk = K // tk
    gm, gn = M // tm, N // tn
    dims = {"nn": ((1,), (0,)), "nt": ((1,), (1,)), "tn": ((0,), (0,))}[mode]
    has_resid = resid is not None
    nci = len(carry.arrays) if carry else 0
    nco = len(carry.out_shapes) if carry else 0

    def body(a_ref, b_ref, *rest):
        r_ref = rest[0] if has_resid else None
        rest = rest[has_resid:]
        c_ins, o_ref, c_outs, scratch = rest[:nci], rest[nci], rest[nci + 1:nci + 1 + nco], rest[nci + 1 + nco:]
        acc_ref = scratch[0] if nk > 1 else None
        sems = scratch[nk > 1:]
        i, j, k = pl.program_id(0), pl.program_id(1), pl.program_id(2)
        if carry:
            @pl.when((i == 0) & (j == 0) & (k == 0))
            def _():
                carry.start(c_ins, c_outs, sems)

        if nk == 1:
            r = _dot(a_ref[...], b_ref[...], dims)
            if has_resid:
                r = r + r_ref[...]
            o_ref[...] = r.astype(o_ref.dtype)
        else:
            @pl.when(k == 0)
            def _():
                acc_ref[...] = jnp.zeros_like(acc_ref)

            acc_ref[...] += _dot(a_ref[...], b_ref[...], dims)

            @pl.when(k == nk - 1)
            def _():
                r = acc_ref[...]
                if has_resid:
                    r = r + r_ref[...]
                o_ref[...] = r.astype(o_ref.dtype)

        if carry:
            @pl.when((i == gm - 1) & (j == gn - 1) & (k == nk - 1))
            def _():
                carry.wait(c_ins, c_outs, sems)

    if mode == "nn":
        a_spec = pl.BlockSpec((tm, tk), lambda i, j, k: (i, k))
        b_spec = pl.BlockSpec((tk, tn), lambda i, j, k: (k, j))
    elif mode == "nt":
        a_spec = pl.BlockSpec((tm, tk), lambda i, j, k: (i, k))
        b_spec = pl.BlockSpec((tn, tk), lambda i, j, k: (j, k))
    else:
        a_spec = pl.BlockSpec((tk, tm), lambda i, j, k: (k, i))
        b_spec = pl.BlockSpec((tk, tn), lambda i, j, k: (k, j))
    o_spec = pl.BlockSpec((tm, tn), lambda i, j, k: (i, j))
    anyspec = pl.BlockSpec(memory_space=pl.ANY)
    in_specs = [a_spec, b_spec] + ([o_spec] if has_resid else []) + [anyspec] * nci
    args = (a, b) + ((resid,) if has_resid else ()) + (tuple(carry.arrays) if carry else ())
    out_shape = [jax.ShapeDtypeStruct((M, N), out_dtype)] + (carry.out_shapes if carry else [])
    scratch = ([] if nk == 1 else [pltpu.VMEM((tm, tn), F32)]) + (carry.scratch if carry else [])
    sem = ("arbitrary",) * 3 if carry else ("parallel", "parallel", "arbitrary")
    outs = pl.pallas_call(
        body, grid=(gm, gn, nk), in_specs=in_specs, out_specs=[o_spec] + [anyspec] * nco,
        out_shape=out_shape, scratch_shapes=scratch, compiler_params=_cp(*sem), name=name)(*args)
    return (outs[0], outs[1:]) if carry else outs[0]


def _rms_fwd(x, w, name, add=None):
    S, D = x.shape
    t = _pick(S, 512, 8)
    has_add = add is not None

    def body(x_ref, w_ref, *rest):
        xv = x_ref[...]
        if has_add:
            xv = xv + rest[0][...]
            rest[1][...] = xv
        r = lax.rsqrt(jnp.mean(xv * xv, axis=-1, keepdims=True) + NORM_EPS)
        rest[-1][...] = (xv * r * w_ref[...]).astype(rest[-1].dtype)

    row = pl.BlockSpec((t, D), lambda i: (i, 0))
    vec = pl.BlockSpec((1, D), lambda i: (0, 0))
    normed = jax.ShapeDtypeStruct((S, D), BF16)
    if not has_add:
        return pl.pallas_call(body, grid=(S // t,), in_specs=[row, vec], out_specs=row, out_shape=normed,
                              compiler_params=_cp("parallel"), name=name)(x, w.reshape(1, D))
    return pl.pallas_call(body, grid=(S // t,), in_specs=[row, vec, row], out_specs=[row, row],
                          out_shape=[jax.ShapeDtypeStruct((S, D), F32), normed],
                          compiler_params=_cp("parallel"), name=name)(x, w.reshape(1, D), add)


def _rms_bwd(x, w, dh, dres, name):
    S, D = x.shape
    t = _pick(S, 512, 8)

    def body(x_ref, w_ref, dh_ref, dr_ref, dx_ref, dw_ref):
        @pl.when(pl.program_id(0) == 0)
        def _():
            dw_ref[...] = jnp.zeros_like(dw_ref)

        xv = x_ref[...]
        r = lax.rsqrt(jnp.mean(xv * xv, axis=-1, keepdims=True) + NORM_EPS)
        xh = xv * r
        dh_v = dh_ref[...]
        g = dh_v * w_ref[...]
        dx_ref[...] = dr_ref[...] + r * (g - xh * jnp.mean(g * xh, axis=-1, keepdims=True))
        dw_ref[...] += jnp.sum(dh_v * xh, axis=0, keepdims=True)

    row = pl.BlockSpec((t, D), lambda i: (i, 0))
    vec = pl.BlockSpec((1, D), lambda i: (0, 0))
    return pl.pallas_call(
        body, grid=(S // t,), in_specs=[row, vec, row, row], out_specs=[row, vec],
        out_shape=[jax.ShapeDtypeStruct((S, D), F32), jax.ShapeDtypeStruct((1, D), F32)],
        compiler_params=_cp("arbitrary"), name=name)(x, w.reshape(1, D), dh, dres)


def _loss_head(x, w, tgt, name):
    S, D = x.shape
    t = _pick(S, 512, 8)

    def body(x_ref, w_ref, t_ref, dx_ref, dw_ref, l_ref):
        @pl.when(pl.program_id(0) == 0)
        def _():
            dw_ref[...] = jnp.zeros_like(dw_ref)
            l_ref[...] = jnp.zeros_like(l_ref)

        xv = x_ref[...]
        wv = w_ref[...]
        r = lax.rsqrt(jnp.mean(xv * xv, axis=-1, keepdims=True) + NORM_EPS)
        xh = xv * r
        err = xh * wv - t_ref[...]
        per_tok = jnp.mean(err * err, axis=-1, keepdims=True)
        l_ref[...] += 0.5 * jnp.sum(per_tok, axis=0, keepdims=True)
        dy = err * (1.0 / D)
        g = dy * wv
        dx_ref[...] = r * (g - xh * jnp.mean(g * xh, axis=-1, keepdims=True))
        dw_ref[...] += jnp.sum(dy * xh, axis=0, keepdims=True)

    row = pl.BlockSpec((t, D), lambda i: (i, 0))
    vec = pl.BlockSpec((1, D), lambda i: (0, 0))
    lspec = pl.BlockSpec((1, 128), lambda i: (0, 0))
    return pl.pallas_call(
        body, grid=(S // t,), in_specs=[row, vec, row], out_specs=[row, vec, lspec],
        out_shape=[jax.ShapeDtypeStruct((S, D), F32), jax.ShapeDtypeStruct((1, D), F32),
                   jax.ShapeDtypeStruct((1, 128), F32)],
        compiler_params=_cp("arbitrary"), name=name)(x, w.reshape(1, D), tgt)


def _rope_table(seq):
    pos = jnp.arange(seq, dtype=F32)
    inv_freq = ROPE_THETA ** (-jnp.arange(0, 2 * ROPE_HALF, 2, dtype=F32) / (2 * ROPE_HALF))
    ang = pos[:, None] * inv_freq[None, :]
    cos, sin = jnp.cos(ang), jnp.sin(ang)
    pad = HEAD_DIM - 2 * ROPE_HALF
    cos_p = jnp.concatenate([cos, cos, jnp.ones((seq, pad), F32)], axis=1)
    sin_a = jnp.concatenate([-sin, jnp.zeros((seq, HEAD_DIM - ROPE_HALF), F32)], axis=1)
    sin_b = jnp.concatenate([jnp.zeros((seq, ROPE_HALF), F32), sin, jnp.zeros((seq, pad), F32)], axis=1)
    return jnp.concatenate([cos_p, sin_a, sin_b], axis=1)


def _rope(t, tab, sign):
    cos_p = tab[:, 0:HEAD_DIM]
    sin_a = tab[:, HEAD_DIM:2 * HEAD_DIM]
    sin_b = tab[:, 2 * HEAD_DIM:3 * HEAD_DIM]
    up = pltpu.roll(t, HEAD_DIM - ROPE_HALF, 1)
    down = pltpu.roll(t, ROPE_HALF, 1)
    return t * cos_p + sign * (up * sin_a + down * sin_b)


def _perm_tokens(a):
    S = a.shape[0]
    return a.reshape(S // PERM, PERM, -1).transpose(1, 0, 2).reshape(S, -1)


def _unperm_tokens(a):
    S = a.shape[0]
    return a.reshape(PERM, S // PERM, -1).transpose(1, 0, 2).reshape(S, -1)


class _Strided:
    def __init__(self, S, dil):
        self.dil, self.m = dil, PERM // dil
        self.c = ATTN_BLOCK // self.m
        self.rows = S // PERM
        self.nb = S // (dil * ATTN_BLOCK)

    def view(self, a):
        return a.reshape(self.m, self.dil, self.rows, a.shape[-1])

    def spec(self, width, col, f=lambda n: n):
        return pl.BlockSpec((self.m, None, self.c, width), lambda r, n: (0, r, f(n), col))

    def load(self, ref, sl=slice(None)):
        if self.m == 1:
            return ref[0, :, sl]
        return jnp.concatenate([ref[q, :, sl] for q in range(self.m)], axis=0)

    def store(self, ref, sl, val):
        for q in range(self.m):
            ref[q, :, sl] = val[q * self.c:(q + 1) * self.c, :]

    def member(self, i):
        shift = self.c.bit_length() - 1
        return (i & (self.c - 1)) * self.m + (i >> shift)


def _attn_fwd(qkv, tab, g, heads, name):
    S = qkv.shape[0]
    W = heads * HEAD_DIM
    dil = ATTN_DILATIONS[g]
    steps = ATTN_WINDOWS[g] // dil
    B = ATTN_BLOCK
    scale = HEAD_DIM ** -0.5
    st = _Strided(S, dil)

    def body(q_ref, k_ref, kp_ref, v_ref, vp_ref, t_ref, tp_ref, o_ref, l_ref):
        n = pl.program_id(1)
        ii = lax.broadcasted_iota(jnp.int32, (B, 2 * B), 0)
        jj = lax.broadcasted_iota(jnp.int32, (B, 2 * B), 1)
        delta = st.member(ii) - st.member(jj & (B - 1)) + jnp.where(jj >= B, 0, B)
        ok = (delta >= 0) & (delta <= steps) & ((jj >= B) | (n > 0))
        tb = st.load(t_ref)
        tpv = st.load(tp_ref)
        for h in range(heads):
            sl = slice(h * HEAD_DIM, (h + 1) * HEAD_DIM)
            q = _rope(st.load(q_ref, sl), tb, 1.0)
            kc = jnp.concatenate([_rope(st.load(kp_ref, sl), tpv, 1.0), _rope(st.load(k_ref, sl), tb, 1.0)], axis=0)
            vc = jnp.concatenate([st.load(vp_ref, sl), st.load(v_ref, sl)], axis=0)
            s = jnp.where(ok, _dot_nt(q, kc) * scale, NEG)
            m = jnp.max(s, axis=-1, keepdims=True)
            p = jnp.exp(s - m)
            den = jnp.sum(p, axis=-1, keepdims=True)
            st.store(o_ref, sl, _dot_nn(p, vc) / den)
            st.store(l_ref, sl, jnp.broadcast_to(m + jnp.log(den), (B, HEAD_DIM)))

    prv = lambda n: jnp.maximum(n - 1, 0)
    qv, tv = st.view(qkv), st.view(tab)
    o_spec = st.spec(W, 0)
    o, lse = pl.pallas_call(
        body, grid=(dil, st.nb),
        in_specs=[st.spec(W, g * 3), st.spec(W, g * 3 + 1), st.spec(W, g * 3 + 1, prv),
                  st.spec(W, g * 3 + 2), st.spec(W, g * 3 + 2, prv),
                  st.spec(3 * HEAD_DIM, 0), st.spec(3 * HEAD_DIM, 0, prv)],
        out_specs=[o_spec, o_spec],
        out_shape=[jax.ShapeDtypeStruct((st.m, dil, st.rows, W), F32)] * 2,
        compiler_params=_cp("parallel", "parallel"), name=name)(qv, qv, qv, qv, qv, tv, tv)
    return o.reshape(S, W), lse.reshape(S, W)


def _attn_combine(os_, ls_, name):
    S, W = os_[0].shape
    t = _pick(S, 256, 8)

    def body(o0, o1, o2, l0, l1, l2, o_ref, l_ref):
        a, b, c = l0[...], l1[...], l2[...]
        m = jnp.maximum(jnp.maximum(a, b), c)
        ea, eb, ec = jnp.exp(a - m), jnp.exp(b - m), jnp.exp(c - m)
        tot = ea + eb + ec
        o_ref[...] = (ea * o0[...] + eb * o1[...] + ec * o2[...]) / tot
        l_ref[...] = m + jnp.log(tot)

    row = pl.BlockSpec((t, W), lambda i: (i, 0))
    return pl.pallas_call(body, grid=(S // t,), in_specs=[row] * 6, out_specs=[row, row],
                          out_shape=[jax.ShapeDtypeStruct((S, W), F32)] * 2,
                          compiler_params=_cp("parallel"), name=name)(*os_, *ls_)


def _attn_bwd(qkv, tab, o, lse, do, dqkv_prev, g, heads, name):
    S = qkv.shape[0]
    W = heads * HEAD_DIM
    dil = ATTN_DILATIONS[g]
    steps = ATTN_WINDOWS[g] // dil
    B = ATTN_BLOCK
    scale = HEAD_DIM ** -0.5
    st = _Strided(S, dil)
    nb = st.nb
    aliased = dqkv_prev is not None

    def body(q_ref, qn_ref, k_ref, kp_ref, v_ref, vp_ref, do_ref, don_ref, o_ref, on_ref,
             l_ref, ln_ref, t_ref, tp_ref, tn_ref, *rest):
        out_ref = rest[-1]
        n = pl.program_id(1)
        has_next = n < nb - 1
        ia = lax.broadcasted_iota(jnp.int32, (B, 2 * B), 0)
        ja = lax.broadcasted_iota(jnp.int32, (B, 2 * B), 1)
        da = st.member(ia) - st.member(ja & (B - 1)) + jnp.where(ja >= B, 0, B)
        ok_a = (da >= 0) & (da <= steps) & ((ja >= B) | (n > 0))
        ib = lax.broadcasted_iota(jnp.int32, (2 * B, B), 0)
        jb = lax.broadcasted_iota(jnp.int32, (2 * B, B), 1)
        db = st.member(ib & (B - 1)) + jnp.where(ib >= B, B, 0) - st.member(jb)
        ok_b = (db >= 0) & (db <= steps) & ((ib < B) | has_next)
        tb, tpv, tnv = st.load(t_ref), st.load(tp_ref), st.load(tn_ref)
        for h in range(heads):
            sl = slice(h * HEAD_DIM, (h + 1) * HEAD_DIM)
            qr = _rope(st.load(q_ref, sl), tb, 1.0)
            qnr = _rope(st.load(qn_ref, sl), tnv, 1.0)
            kr = _rope(st.load(k_ref, sl), tb, 1.0)
            kpr = _rope(st.load(kp_ref, sl), tpv, 1.0)
            v = st.load(v_ref, sl)
            dov_ = st.load(do_ref, sl)
            donv = st.load(don_ref, sl)
            dl = jnp.sum(dov_ * st.load(o_ref, sl), axis=-1, keepdims=True)
            dln = jnp.sum(donv * st.load(on_ref, sl), axis=-1, keepdims=True)
            ls = st.load(l_ref, sl)
            kc = jnp.concatenate([kpr, kr], axis=0)
            vc = jnp.concatenate([st.load(vp_ref, sl), v], axis=0)
            s = _dot_nt(qr, kc) * scale
            p = jnp.where(ok_a, jnp.exp(jnp.minimum(s - jnp.concatenate([ls, ls], axis=1), 30.0)), 0.0)
            ds = p * (_dot_nt(dov_, vc) - dl) * scale
            st.store(out_ref, sl, _rope(_dot_nn(ds, kc), tb, -1.0))
            qc = jnp.concatenate([qr, qnr], axis=0)
            doc = jnp.concatenate([dov_, donv], axis=0)
            lc = jnp.concatenate([ls, st.load(ln_ref, sl)], axis=0)
            dlc = jnp.concatenate([dl, dln], axis=0)
            s2 = _dot_nt(qc, kr) * scale
            p2 = jnp.where(ok_b, jnp.exp(jnp.minimum(s2 - lc, 30.0)), 0.0)
            ds2 = p2 * (_dot_nt(doc, v) - dlc) * scale
            st.store(out_ref, slice(W + h * HEAD_DIM, W + (h + 1) * HEAD_DIM), _rope(_dot_tn(ds2, qc), tb, -1.0))
            st.store(out_ref, slice(2 * W + h * HEAD_DIM, 2 * W + (h + 1) * HEAD_DIM), _dot_tn(p2, doc))

    nxt = lambda n: jnp.minimum(n + 1, nb - 1)
    prv = lambda n: jnp.maximum(n - 1, 0)
    same = lambda n: n
    q0, q1, q2, tw = g * 3, g * 3 + 1, g * 3 + 2, 3 * HEAD_DIM
    in_specs = [st.spec(W, q0), st.spec(W, q0, nxt), st.spec(W, q1), st.spec(W, q1, prv),
                st.spec(W, q2), st.spec(W, q2, prv)]
    in_specs += [st.spec(W, 0, f) for f in (same, nxt, same, nxt, same, nxt)]
    in_specs += [st.spec(tw, 0, f) for f in (same, prv, nxt)]
    qv, tv, ov, lv, dov = (st.view(a) for a in (qkv, tab, o, lse, do))
    args = [qv, qv, qv, qv, qv, qv, dov, dov, ov, ov, lv, lv, tv, tv, tv]
    kwargs = {}
    if aliased:
        in_specs.append(pl.BlockSpec(memory_space=pl.ANY))
        args.append(st.view(dqkv_prev))
        kwargs["input_output_aliases"] = {len(args) - 1: 0}
    out = pl.pallas_call(
        body, grid=(dil, nb), in_specs=in_specs, out_specs=st.spec(3 * W, g),
        out_shape=jax.ShapeDtypeStruct((st.m, dil, st.rows, 9 * W), F32),
        compiler_params=_cp("parallel", "parallel"), name=name, **kwargs)(*args)
    return out.reshape(S, 9 * W)


def _shift_down(x, halo, s):
    if s == 0:
        return x
    T = x.shape[0]
    xs = pltpu.roll(x, s, 0)
    hs = pltpu.roll(halo, s, 0)
    row8 = lax.broadcasted_iota(jnp.int32, hs.shape, 0)
    top = jnp.where(row8 < s, hs, xs[0:8])
    return jnp.concatenate([top, xs[8:T]], axis=0)


def _shift_up(x, halo, s):
    if s == 0:
        return x
    T = x.shape[0]
    xs = pltpu.roll(x, T - s, 0)
    hs = pltpu.roll(halo, 8 - s, 0)
    row8 = lax.broadcasted_iota(jnp.int32, hs.shape, 0)
    bot = jnp.where(row8 >= 8 - s, hs, xs[T - 8:T])
    return jnp.concatenate([xs[0:T - 8], bot], axis=0)


CONV_ROWS = 128
CONV_LANES = 512


def _conv_apply(x, halo, w_ref, wsl, b, K):
    acc = x * w_ref[K - 1, :, wsl] + b
    for s in range(1, K):
        acc = acc + _shift_down(x, halo, s) * w_ref[K - 1 - s, :, wsl]
    return acc


def _conv_accum(dy, dyn, xv, xp, w_ref, dw_ref, db_ref, wsl, K):
    acc = dy * w_ref[K - 1, :, wsl]
    dw_ref[K - 1, :, wsl] += jnp.sum(dy * xv, axis=0, keepdims=True)
    for s in range(1, K):
        acc = acc + _shift_up(dy, dyn, s) * w_ref[K - 1 - s, :, wsl]
        dw_ref[K - 1 - s, :, wsl] += jnp.sum(dy * _shift_down(xv, xp, s), axis=0, keepdims=True)
    db_ref[:, wsl] += jnp.sum(dy, axis=0, keepdims=True)
    return acc


def _row_specs(T, S, width):
    main = pl.BlockSpec((T, width), lambda i: (i, 0))
    prev = pl.BlockSpec((8, width), lambda i: (jnp.maximum(i * (T // 8) - 1, 0), 0))
    nxt = pl.BlockSpec((8, width), lambda i: (jnp.minimum((i + 1) * (T // 8), S // 8 - 1), 0))
    return main, prev, nxt


def _full(shape):
    return pl.BlockSpec(shape, lambda i: (0,) * len(shape))


def _silu_grad(y):
    sg = _sigmoid(y)
    return sg * (1.0 + y * (1.0 - sg))


def _ssm_conv_fwd(zx, w, b, d_inner, conv_dim, name):
    S, wz = zx.shape
    K = w.shape[0]
    T = _pick(S, CONV_ROWS, 8)
    cw = _pick(conv_dim, CONV_LANES)

    def body(x_ref, h_ref, w_ref, b_ref, c_ref):
        has_prev = pl.program_id(0) > 0
        for cs in range(0, conv_dim, cw):
            so, sx = slice(cs, cs + cw), slice(d_inner + cs, d_inner + cs + cw)
            halo = jnp.where(has_prev, h_ref[:, sx], 0.0)
            c_ref[:, so] = _conv_apply(x_ref[:, sx], halo, w_ref, so, b_ref[:, so], K)

    main, prev, _ = _row_specs(T, S, wz)
    return pl.pallas_call(
        body, grid=(S // T,), in_specs=[main, prev, _full((K, 1, conv_dim)), _full((1, conv_dim))],
        out_specs=pl.BlockSpec((T, conv_dim), lambda i: (i, 0)),
        out_shape=jax.ShapeDtypeStruct((S, conv_dim), F32),
        compiler_params=_cp("parallel"), name=name)(zx, zx, w.reshape(K, 1, conv_dim), b.reshape(1, conv_dim))


def _ssm_conv_bwd(dxs, dbm, dcm, conv, zx, dz, ddt, w, d_inner, name):
    S, wz = zx.shape
    K, conv_dim = w.shape
    gn = dbm.shape[1]
    T = _pick(S, CONV_ROWS, 8)
    cw = _pick(math.gcd(d_inner, gn), CONV_LANES)
    nrow = S // T
    tail = wz - d_inner - conv_dim
    assert ddt.shape[1] == tail

    def body(dx_ref, dxn_ref, db_ref_, dbn_ref, dc_ref, dcn_ref, y_ref, yn_ref, x_ref, xp_ref, dz_ref, ddt_ref,
             w_ref, o_ref, dw_ref, dbias_ref):
        i = pl.program_id(0)

        @pl.when(i == 0)
        def _():
            dw_ref[...] = jnp.zeros_like(dw_ref)
            dbias_ref[...] = jnp.zeros_like(dbias_ref)

        has_prev, has_next = i > 0, i < nrow - 1
        for cs in range(0, d_inner, cw):
            o_ref[:, cs:cs + cw] = dz_ref[:, cs:cs + cw].astype(o_ref.dtype)
        o_ref[:, d_inner + conv_dim:wz] = ddt_ref[...].astype(o_ref.dtype)
        for cs in range(0, conv_dim, cw):
            so, sx = slice(cs, cs + cw), slice(d_inner + cs, d_inner + cs + cw)
            if cs < d_inner:
                src, srcn, ss = dx_ref, dxn_ref, slice(cs, cs + cw)
            elif cs < d_inner + gn:
                src, srcn, ss = db_ref_, dbn_ref, slice(cs - d_inner, cs - d_inner + cw)
            else:
                src, srcn, ss = dc_ref, dcn_ref, slice(cs - d_inner - gn, cs - d_inner - gn + cw)
            dy = src[:, ss] * _silu_grad(y_ref[:, so])
            dyn = jnp.where(has_next, srcn[:, ss] * _silu_grad(yn_ref[:, so]), 0.0)
            xp = jnp.where(has_prev, xp_ref[:, sx], 0.0)
            o_ref[:, sx] = _conv_accum(dy, dyn, x_ref[:, sx], xp, w_ref, dw_ref, dbias_ref, so, K).astype(o_ref.dtype)

    xm, _, xn = _row_specs(T, S, d_inner)
    gm, _, gnx = _row_specs(T, S, gn)
    cm, _, cn = _row_specs(T, S, conv_dim)
    zm, zp, _ = _row_specs(T, S, wz)
    tm_, _, _ = _row_specs(T, S, tail)
    dzx, dw, db = pl.pallas_call(
        body, grid=(nrow,),
        in_specs=[xm, xn, gm, gnx, gm, gnx, cm, cn, zm, zp, xm, tm_, _full((K, 1, conv_dim))],
        out_specs=[zm, _full((K, 1, conv_dim)), _full((1, conv_dim))],
        out_shape=[jax.ShapeDtypeStruct((S, wz), BF16), jax.ShapeDtypeStruct((K, 1, conv_dim), F32),
                   jax.ShapeDtypeStruct((1, conv_dim), F32)],
        compiler_params=_cp("arbitrary"), name=name)(
            dxs, dxs, dbm, dbm, dcm, dcm, conv, conv, zx, zx, dz, ddt, w.reshape(K, 1, conv_dim))
    return dzx, dw.reshape(K, conv_dim), db


def _ffn_conv_fwd(up, w, b, name):
    S, C = up.shape
    F = C // 2
    K = w.shape[0]
    T = _pick(S, CONV_ROWS, 8)
    cw = _pick(F, CONV_LANES)

    def body(x_ref, h_ref, w_ref, b_ref, u_ref, a_ref):
        has_prev = pl.program_id(0) > 0
        for cs in range(0, F, cw):
            sg, su = slice(cs, cs + cw), slice(F + cs, F + cs + cw)
            gate = _conv_apply(x_ref[:, sg], jnp.where(has_prev, h_ref[:, sg], 0.0), w_ref, sg, b_ref[:, sg], K)
            upv = _conv_apply(x_ref[:, su], jnp.where(has_prev, h_ref[:, su], 0.0), w_ref, su, b_ref[:, su], K)
            u_ref[0, :, sg] = gate
            u_ref[1, :, sg] = upv
            a_ref[:, sg] = (gate * _sigmoid(gate) * upv).astype(a_ref.dtype)

    main, prev, _ = _row_specs(T, S, C)
    return pl.pallas_call(
        body, grid=(S // T,), in_specs=[main, prev, _full((K, 1, C)), _full((1, C))],
        out_specs=[pl.BlockSpec((2, T, F), lambda i: (0, i, 0)), pl.BlockSpec((T, F), lambda i: (i, 0))],
        out_shape=[jax.ShapeDtypeStruct((2, S, F), F32), jax.ShapeDtypeStruct((S, F), BF16)],
        compiler_params=_cp("parallel"), name=name)(up, up, w.reshape(K, 1, C), b.reshape(1, C))


def _ffn_conv_bwd(u2, dact, up, w, name):
    S, C = up.shape
    F = C // 2
    K = w.shape[0]
    T = _pick(S, CONV_ROWS, 8)
    cw = _pick(F, CONV_LANES)
    nrow = S // T

    def du(half, gate, upv, d):
        sg = _sigmoid(gate)
        return d * upv * sg * (1.0 + gate * (1.0 - sg)) if half == 0 else d * gate * sg

    def body(u_ref, un_ref, d_ref, dn_ref, x_ref, xp_ref, w_ref, dx_ref, dw_ref, db_ref):
        i = pl.program_id(0)

        @pl.when(i == 0)
        def _():
            dw_ref[...] = jnp.zeros_like(dw_ref)
            db_ref[...] = jnp.zeros_like(db_ref)

        has_prev, has_next = i > 0, i < nrow - 1
        for half in range(2):
            for cs in range(0, F, cw):
                sf, sc = slice(cs, cs + cw), slice(half * F + cs, half * F + cs + cw)
                dy = du(half, u_ref[0, :, sf], u_ref[1, :, sf], d_ref[:, sf])
                dyn = jnp.where(has_next, du(half, un_ref[0, :, sf], un_ref[1, :, sf], dn_ref[:, sf]), 0.0)
                xp = jnp.where(has_prev, xp_ref[:, sc], 0.0)
                dx_ref[:, sc] = _conv_accum(dy, dyn, x_ref[:, sc], xp, w_ref, dw_ref, db_ref, sc, K).astype(dx_ref.dtype)

    am, _, an = _row_specs(T, S, F)
    xm, xp_, _ = _row_specs(T, S, C)
    u_main = pl.BlockSpec((2, T, F), lambda i: (0, i, 0))
    u_next = pl.BlockSpec((2, 8, F), lambda i: (0, jnp.minimum((i + 1) * (T // 8), S // 8 - 1), 0))
    dx, dw, db = pl.pallas_call(
        body, grid=(nrow,), in_specs=[u_main, u_next, am, an, xm, xp_, _full((K, 1, C))],
        out_specs=[xm, _full((K, 1, C)), _full((1, C))],
        out_shape=[jax.ShapeDtypeStruct((S, C), BF16), jax.ShapeDtypeStruct((K, 1, C), F32),
                   jax.ShapeDtypeStruct((1, C), F32)],
        compiler_params=_cp("arbitrary"), name=name)(u2, u2, dact, dact, up, up, w.reshape(K, 1, C))
    return dx, dw.reshape(K, C), db


def _cumsum_rows(v):
    n = v.shape[0]
    row = lax.broadcasted_iota(jnp.int32, v.shape, 0)
    k = 1
    while k < n:
        v = v + jnp.where(row >= k, pltpu.roll(v, k, 0), 0.0)
        k *= 2
    return v


def _rev_cumsum_rows(v):
    n = v.shape[0]
    row = lax.broadcasted_iota(jnp.int32, v.shape, 0)
    k = 1
    while k < n:
        v = v + jnp.where(row < n - k, pltpu.roll(v, n - k, 0), 0.0)
        k *= 2
    return v


def _ssd_common(x_ref, dtr_ref, bias_ref, alog_ref, gw):
    Q = SSM_CHUNK
    X = _silu(x_ref[...])
    pre = dtr_ref[...] + bias_ref[...]
    dt = _softplus(pre)
    a = -jnp.exp(alog_ref[...])
    cs = _cumsum_rows(dt * a)
    row = lax.broadcasted_iota(jnp.int32, (Q, gw), 0)
    cs_last = jnp.sum(jnp.where(row == Q - 1, cs, 0.0), axis=0, keepdims=True)
    return X, pre, dt, a, cs, cs_last, row


def _head_decay(cs, head_mask):
    Q = SSM_CHUNK
    col = jnp.max(jnp.where(head_mask, cs, NEG), axis=1, keepdims=True)
    acol = jnp.broadcast_to(col, (Q, Q))
    arow = acol.T
    ii = lax.broadcasted_iota(jnp.int32, (Q, Q), 0)
    jj = lax.broadcasted_iota(jnp.int32, (Q, Q), 1)
    tril = ii >= jj
    return jnp.where(tril, jnp.exp(jnp.where(tril, acol - arow, 0.0)), 0.0), tril


def _ssd_specs(S, d_inner, gw, nc, rev):
    Q, N, G = SSM_CHUNK, SSM_STATE, SSM_GROUPS
    ch = (lambda c: nc - 1 - c) if rev else (lambda c: c)
    x_spec = pl.BlockSpec((Q, gw), lambda g, c: (ch(c), g))
    b_spec = pl.BlockSpec((Q, N), lambda g, c: (ch(c), d_inner // N + g))
    c_spec = pl.BlockSpec((Q, N), lambda g, c: (ch(c), d_inner // N + G + g))
    p_spec = pl.BlockSpec((None, 1, gw), lambda g, c: (g, 0, 0))
    s_spec = pl.BlockSpec((None, None, gw, N), lambda g, c: (ch(c), g, 0, 0))
    return x_spec, b_spec, c_spec, p_spec, s_spec


def _ssd_fwd(xbc, dtr, bias, alog, dsk, d_inner, name):
    S = xbc.shape[0]
    Q, N, G, P = SSM_CHUNK, SSM_STATE, SSM_GROUPS, SSM_HEAD_DIM
    gw = d_inner // G
    R = gw // P
    nc = S // Q

    def body(x_ref, b_ref, c_ref, dtr_ref, bias_ref, alog_ref, d_ref, y_ref, sp_ref, s_scr):
        @pl.when(pl.program_id(1) == 0)
        def _():
            s_scr[...] = jnp.zeros_like(s_scr)

        X, _, dt, a, cs, cs_last, row = _ssd_common(x_ref, dtr_ref, bias_ref, alog_ref, gw)
        Bm, Cm = _silu(b_ref[...]), _silu(c_ref[...])
        xdt = X * dt
        lane = lax.broadcasted_iota(jnp.int32, (Q, gw), 1)
        sprev = s_scr[...]
        sp_ref[...] = sprev
        cb = _dot_nt(Cm, Bm)
        y = jnp.exp(cs) * _dot_nt(Cm, sprev)
        for r in range(R):
            hm = (lane >= r * P) & (lane < (r + 1) * P)
            dec_l, _ = _head_decay(cs, hm)
            y = y + _dot_nn(cb * dec_l, jnp.where(hm, xdt, 0.0))
        dec = jnp.exp(cs_last - cs)
        cd = jnp.exp(jnp.broadcast_to(cs_last, (Q, gw)).T)
        s_scr[...] = sprev * cd + _dot_tn(xdt * dec, Bm)
        y_ref[...] = y + d_ref[...] * X

    x_spec, b_spec, c_spec, p_spec, s_spec = _ssd_specs(S, d_inner, gw, nc, False)
    return pl.pallas_call(
        body, grid=(G, nc), in_specs=[x_spec, b_spec, c_spec, x_spec, p_spec, p_spec, p_spec],
        out_specs=[x_spec, s_spec],
        out_shape=[jax.ShapeDtypeStruct((S, d_inner), F32), jax.ShapeDtypeStruct((nc, G, gw, N), F32)],
        scratch_shapes=[pltpu.VMEM((gw, N), F32)],
        compiler_params=_cp("parallel", "arbitrary"), name=name)(xbc, xbc, xbc, dtr, bias, alog, dsk)


def _ssd_bwd(xbc, dtr, bias, alog, dsk, sprev_all, dy, d_inner, name):
    S = xbc.shape[0]
    Q, N, G, P = SSM_CHUNK, SSM_STATE, SSM_GROUPS, SSM_HEAD_DIM
    gw = d_inner // G
    R = gw // P
    nc = S // Q

    def body(x_ref, b_ref, c_ref, dtr_ref, bias_ref, alog_ref, d_ref, sp_ref, dy_ref,
             dx_ref, db_ref, dc_ref, ddt_ref, dbias_ref, dalog_ref, dd_ref, ds_scr):
        @pl.when(pl.program_id(1) == 0)
        def _():
            ds_scr[...] = jnp.zeros_like(ds_scr)
            dbias_ref[...] = jnp.zeros_like(dbias_ref)
            dalog_ref[...] = jnp.zeros_like(dalog_ref)
            dd_ref[...] = jnp.zeros_like(dd_ref)

        X, pre, dt, a, cs, cs_last, row = _ssd_common(x_ref, dtr_ref, bias_ref, alog_ref, gw)
        Bm, Cm = _silu(b_ref[...]), _silu(c_ref[...])
        dY = dy_ref[...]
        sprev = sp_ref[...]
        dsn = ds_scr[...]
        xdt = X * dt
        lane = lax.broadcasted_iota(jnp.int32, (Q, gw), 1)
        lane1 = lax.broadcasted_iota(jnp.int32, (1, gw), 1)
        srow = lax.broadcasted_iota(jnp.int32, (gw, N), 0)
        ecs = jnp.exp(cs)
        dec = jnp.exp(cs_last - cs)
        cd = jnp.exp(jnp.broadcast_to(cs_last, (Q, gw)).T)
        dd_ref[...] += jnp.sum(dY * X, axis=0, keepdims=True)
        dX = d_ref[...] * dY
        ey = ecs * dY
        dcs = ey * _dot_nt(Cm, sprev)
        dC = _dot_nn(ey, sprev)
        ds_scr[...] = cd * dsn + _dot_tn(ey, Cm)
        wmat = _dot_nt(Bm, dsn)
        dxdt = dec * wmat
        xd = xdt * dec
        dB = _dot_nn(xd, dsn)
        ddec = xdt * wmat * dec
        dcs = dcs - ddec
        dlast = jnp.sum(ddec, axis=0, keepdims=True)
        qmat = dsn * sprev * cd
        cb = _dot_nt(Cm, Bm)
        dcb = jnp.zeros((Q, Q), F32)
        dcs_rep = jnp.zeros((Q, gw), F32)
        dtx_rep = jnp.zeros((Q, gw), F32)
        for r in range(R):
            hm = (lane >= r * P) & (lane < (r + 1) * P)
            dec_l, tril = _head_decay(cs, hm)
            dyr = jnp.where(hm, dY, 0.0)
            gmat = jnp.where(tril, _dot_nt(dyr, xdt), 0.0)
            dcb = dcb + gmat * dec_l
            e = gmat * cb * dec_l
            v = (jnp.sum(e, axis=1, keepdims=True) - jnp.sum(e.T, axis=1, keepdims=True)
                 + jnp.sum(jnp.where(hm, dcs, 0.0), axis=1, keepdims=True))
            dxdt = dxdt + _dot_tn(cb * dec_l, dyr)
            hm1 = (lane1 >= r * P) & (lane1 < (r + 1) * P)
            t_last = (jnp.sum(jnp.where(hm1, dlast, 0.0), axis=1, keepdims=True)
                      + jnp.sum(jnp.where((srow >= r * P) & (srow < (r + 1) * P), qmat, 0.0), keepdims=True))
            dcs_rep = dcs_rep + jnp.where(hm, v, 0.0) + jnp.where(hm & (row == Q - 1), t_last, 0.0)
        for r in range(R):
            hm = (lane >= r * P) & (lane < (r + 1) * P)
            w_r = jnp.sum(jnp.where(hm, dxdt * X, 0.0), axis=1, keepdims=True)
            dtx_rep = dtx_rep + jnp.where(hm, w_r, 0.0)
        dadt = _rev_cumsum_rows(dcs_rep)
        ddt = a * dadt + dtx_rep
        dalog_ref[...] += jnp.sum(dt * dadt, axis=0, keepdims=True) * a
        draw = ddt * _sigmoid(pre)
        ddt_ref[...] = draw
        dbias_ref[...] += jnp.sum(draw, axis=0, keepdims=True)
        dx_ref[...] = dX + dxdt * dt
        db_ref[...] = dB + _dot_tn(dcb, Cm)
        dc_ref[...] = dC + _dot_nn(dcb, Bm)

    x_spec, b_spec, c_spec, p_spec, s_spec = _ssd_specs(S, d_inner, gw, nc, True)
    n_spec = pl.BlockSpec((Q, N), lambda g, c: (nc - 1 - c, g))
    gshape = jax.ShapeDtypeStruct((G, 1, gw), F32)
    return pl.pallas_call(
        body, grid=(G, nc),
        in_specs=[x_spec, b_spec, c_spec, x_spec, p_spec, p_spec, p_spec, s_spec, x_spec],
        out_specs=[x_spec, n_spec, n_spec, x_spec, p_spec, p_spec, p_spec],
        out_shape=[jax.ShapeDtypeStruct((S, d_inner), F32), jax.ShapeDtypeStruct((S, G * N), F32),
                   jax.ShapeDtypeStruct((S, G * N), F32), jax.ShapeDtypeStruct((S, d_inner), F32),
                   gshape, gshape, gshape],
        scratch_shapes=[pltpu.VMEM((gw, N), F32)],
        compiler_params=_cp("parallel", "arbitrary"), name=name)(
            xbc, xbc, xbc, dtr, bias, alog, dsk, sprev_all, dy)


def _gnorm_fwd(y, zx, w, name):
    S, d_inner = y.shape
    G = SSM_GROUPS
    gw = d_inner // G
    T = _pick(S, 256, 8)

    def body(y_ref, z_ref, w_ref, o_ref):
        for k in range(G):
            sl = slice(k * gw, (k + 1) * gw)
            z = z_ref[:, sl]
            gk = y_ref[:, sl] * z * _sigmoid(z)
            r = lax.rsqrt(jnp.mean(gk * gk, axis=-1, keepdims=True) + NORM_EPS)
            o_ref[:, sl] = (gk * r * w_ref[:, sl]).astype(o_ref.dtype)

    row = pl.BlockSpec((T, d_inner), lambda i: (i, 0))
    vec = pl.BlockSpec((1, d_inner), lambda i: (0, 0))
    return pl.pallas_call(body, grid=(S // T,), in_specs=[row, row, vec], out_specs=row,
                          out_shape=jax.ShapeDtypeStruct((S, d_inner), BF16),
                          compiler_params=_cp("parallel"), name=name)(y, zx, w.reshape(1, d_inner))


def _gnorm_bwd(y, zx, w, dout, name):
    S, d_inner = y.shape
    G = SSM_GROUPS
    gw = d_inner // G
    T = _pick(S, 256, 8)

    def body(y_ref, z_ref, w_ref, d_ref, dy_ref, dz_ref, dw_ref):
        @pl.when(pl.program_id(0) == 0)
        def _():
            dw_ref[...] = jnp.zeros_like(dw_ref)

        for k in range(G):
            sl = slice(k * gw, (k + 1) * gw)
            z, yv, d = z_ref[:, sl], y_ref[:, sl], d_ref[:, sl]
            sg = _sigmoid(z)
            sz = z * sg
            gk = yv * sz
            r = lax.rsqrt(jnp.mean(gk * gk, axis=-1, keepdims=True) + NORM_EPS)
            gh = gk * r
            dw_ref[:, sl] += jnp.sum(d * gh, axis=0, keepdims=True)
            dg = d * w_ref[:, sl]
            dgk = r * (dg - gh * jnp.mean(dg * gh, axis=-1, keepdims=True))
            dy_ref[:, sl] = dgk * sz
            dz_ref[:, sl] = dgk * yv * sg * (1.0 + z * (1.0 - sg))

    row = pl.BlockSpec((T, d_inner), lambda i: (i, 0))
    vec = pl.BlockSpec((1, d_inner), lambda i: (0, 0))
    return pl.pallas_call(
        body, grid=(S // T,), in_specs=[row, row, vec, row], out_specs=[row, row, vec],
        out_shape=[jax.ShapeDtypeStruct((S, d_inner), F32)] * 2 + [jax.ShapeDtypeStruct((1, d_inner), F32)],
        compiler_params=_cp("arbitrary"), name=name)(y, zx, w.reshape(1, d_inner), dout)


def _adam_math(g, w, m, v):
    m = ADAM_B1 * m + (1.0 - ADAM_B1) * g
    v = ADAM_B2 * v + (1.0 - ADAM_B2) * (g * g)
    m_hat = m / (1.0 - ADAM_B1 ** ADAM_STEP)
    v_hat = v / (1.0 - ADAM_B2 ** ADAM_STEP)
    delta = -ADAM_LR * (m_hat / (jnp.sqrt(v_hat) + ADAM_EPS) + ADAM_WD * w)
    return delta, m, v


def _adamw_big(own, sib, w, m, v, layer, prev, name):
    L, A, Bc = w.shape
    T = _pick(A, max(8, (1 << 19) // (4 * Bc)), 16)

    def body(o_ref, s_ref, w_ref, m_ref, v_ref, *rest):
        g_ref, d_ref, nm_ref, nv_ref = rest[-4:]
        so = o_ref[0].astype(F32)
        ss = s_ref[0].astype(F32)
        for k in range(1, N_CHIPS):
            so = so + o_ref[k].astype(F32)
            ss = ss + s_ref[k].astype(F32)
        g = so + ss
        delta, nm, nv = _adam_math(g, w_ref[...], m_ref[...], v_ref[...])
        g_ref[...] = g
        d_ref[...] = delta
        nm_ref[...] = nm
        nv_ref[...] = nv

    part = pl.BlockSpec((N_CHIPS, T, Bc), lambda i: (0, i, 0))
    blk = pl.BlockSpec((None, T, Bc), lambda i: (layer, i, 0))
    shp = jax.ShapeDtypeStruct(w.shape, F32)
    in_specs, args, kwargs = [part, part, blk, blk, blk], [own, sib, w, m, v], {}
    if prev is not None:
        in_specs += [pl.BlockSpec(memory_space=pl.ANY)] * 4
        args += list(prev)
        kwargs["input_output_aliases"] = {5 + q: q for q in range(4)}
    return pl.pallas_call(body, grid=(A // T,), in_specs=in_specs, out_specs=[blk] * 4, out_shape=[shp] * 4,
                          compiler_params=_cp("parallel"), name=name, **kwargs)(*args)


def _sum_devices(parts, name):
    _, R, C = parts.shape

    def body(p_ref, o_ref):
        acc = p_ref[0]
        for k in range(1, N_DEV):
            acc = acc + p_ref[k]
        o_ref[...] = acc

    return pl.pallas_call(body, out_shape=jax.ShapeDtypeStruct((R, C), F32), name=name)(parts)


def _adamw_small(g, w, m, v, name):
    def body(g_ref, w_ref, m_ref, v_ref, d_ref, nm_ref, nv_ref):
        delta, nm, nv = _adam_math(g_ref[...], w_ref[...], m_ref[...], v_ref[...])
        d_ref[...] = delta
        nm_ref[...] = nm
        nv_ref[...] = nv

    shp = jax.ShapeDtypeStruct(g.shape, F32)
    return pl.pallas_call(body, out_shape=[shp] * 3, name=name)(g, w, m, v)


PACK_COLS = 1024


def _pack(arrs):
    flat = jnp.concatenate([a.reshape(-1).astype(F32) for a in arrs])
    n = flat.shape[0]
    rows = -(-n // (8 * PACK_COLS)) * 8
    return jnp.pad(flat, (0, rows * PACK_COLS - n)).reshape(rows, PACK_COLS)


def _unpack(packed, shapes):
    flat = packed.reshape(-1)
    out, off = [], 0
    for s in shapes:
        n = math.prod(s)
        out.append(flat[off:off + n].reshape(s))
        off += n
    return out


def _shard_ref(ref, kind, k, n):
    if kind == "col":
        return ref.at[:, pl.ds(pl.multiple_of(k * n, 128), n)]
    if kind == "row":
        return ref.at[pl.ds(pl.multiple_of(k * n, 16), n), :]
    return ref.at[k]


def _chip_peers():
    x, y, c = lax.axis_index("x"), lax.axis_index("y"), lax.axis_index("c")
    return x, y, c, [(1 - x, y), (x, 1 - y), (1 - x, 1 - y)]


class _Exchange:
    def __init__(self, mode, items):
        self.mode, self.items = mode, items
        self.arrays = []
        for it in items:
            if not any(it[0] is a for a in self.arrays):
                self.arrays.append(it[0])
        self.src_idx = [next(i for i, a in enumerate(self.arrays) if a is it[0]) for it in items]
        self.out_shapes = [jax.ShapeDtypeStruct(it[-1], it[0].dtype) for it in items]
        n = len(items)
        if mode == "swap":
            self.scratch = [pltpu.SemaphoreType.DMA((n,)), pltpu.SemaphoreType.DMA((n,))]
        else:
            self.scratch = [pltpu.SemaphoreType.DMA((3 * n,)), pltpu.SemaphoreType.DMA((3 * n,)),
                            pltpu.SemaphoreType.DMA((n,))]

    def _copies(self, ins, outs, sems):
        if self.mode == "swap":
            send_sems, recv_sems = sems
            x, y, c = lax.axis_index("x"), lax.axis_index("y"), lax.axis_index("c")
            sent = [pltpu.make_async_remote_copy(
                src_ref=ins[self.src_idx[t]], dst_ref=outs[t], send_sem=send_sems.at[t], recv_sem=recv_sems.at[t],
                device_id=(x, y, 1 - c), device_id_type=MESH) for t in range(len(self.items))]
            return [], sent, sent
        send_sems, recv_sems, loc_sems = sems
        x, y, c, peers = _chip_peers()
        me = 2 * x + y
        local, sent, arriving = [], [], []
        for t, it in enumerate(self.items):
            src_arr = ins[self.src_idx[t]]
            if self.mode == "gather":
                _, layer, kind, n, _ = it
                src = src_arr if layer is None else src_arr.at[layer]
                src_for = lambda k: src
                dst_from = lambda k: _shard_ref(outs[t], kind, k, n)
            else:
                _, kind, n, _ = it
                src_for = lambda k: _shard_ref(src_arr, kind, k, n)
                dst_from = lambda k: outs[t].at[k]
            local.append(pltpu.make_async_copy(src_for(me), dst_from(me), loc_sems.at[t]))
            for j, (px, py) in enumerate(peers):
                pk = 2 * px + py
                args = dict(send_sem=send_sems.at[3 * t + j], recv_sem=recv_sems.at[3 * t + j],
                            device_id=(px, py, c), device_id_type=MESH)
                sent.append(pltpu.make_async_remote_copy(src_ref=src_for(pk), dst_ref=dst_from(me), **args))
                arriving.append(pltpu.make_async_remote_copy(src_ref=src_for(pk), dst_ref=dst_from(pk), **args))
        return local, sent, arriving

    def start(self, ins, outs, sems):
        local, sent, arriving = self._copies(ins, outs, sems)
        for cp in local + sent:
            cp.start()
        for cp in arriving:
            cp._used = True

    def wait(self, ins, outs, sems):
        local, sent, arriving = self._copies(ins, outs, sems)
        for cp in arriving:
            cp.wait_recv()
        for cp in sent:
            cp.wait_send()
        for cp in local:
            cp.wait()


class _Multi:
    def __init__(self, parts):
        self.parts = parts
        self.arrays = [a for p in parts for a in p.arrays]
        self.out_shapes = [s for p in parts for s in p.out_shapes]
        self.scratch = [s for p in parts for s in p.scratch]

    def _split(self, ins, outs, sems):
        i = o = s = 0
        for p in self.parts:
            ni, no, ns = len(p.arrays), len(p.out_shapes), len(p.scratch)
            yield p, ins[i:i + ni], outs[o:o + no], sems[s:s + ns]
            i, o, s = i + ni, o + no, s + ns

    def start(self, ins, outs, sems):
        for p, a, b, c in self._split(ins, outs, sems):
            p.start(a, b, c)

    def wait(self, ins, outs, sems):
        for p, a, b, c in self._split(ins, outs, sems):
            p.wait(a, b, c)


def _run_exchange(ex, name):
    nin, nout = len(ex.arrays), len(ex.out_shapes)

    def body(*refs):
        ins, outs, sems = refs[:nin], refs[nin:nin + nout], refs[nin + nout:]
        ex.start(ins, outs, sems)
        ex.wait(ins, outs, sems)

    anyspec = pl.BlockSpec(memory_space=pl.ANY)
    return pl.pallas_call(body, in_specs=[anyspec] * nin, out_specs=[anyspec] * nout, out_shape=ex.out_shapes,
                          scratch_shapes=ex.scratch, name=name)(*ex.arrays)


def _start_exchange(ex, name):
    nin, nout, nsem = len(ex.arrays), len(ex.out_shapes), len(ex.scratch)
    hbm = pl.BlockSpec(memory_space=pltpu.HBM)
    sem = pl.BlockSpec(memory_space=pltpu.SEMAPHORE)

    def body(*refs):
        ins, lands = refs[:nin], refs[nin:nin + nout]
        sems = refs[nin + nout:nin + nout + nsem]
        ex.start(ins, lands, sems)
        refs[-1][...] = jnp.zeros_like(refs[-1])

    args = [pltpu.with_memory_space_constraint(a, pltpu.HBM) for a in ex.arrays]
    args += [pltpu.with_memory_space_constraint(lax.empty(s.shape, s.dtype), pltpu.HBM) for s in ex.out_shapes]
    thru = [pltpu.HBM(a.shape, a.dtype) for a in ex.arrays] + [pltpu.HBM(s.shape, s.dtype) for s in ex.out_shapes]
    return pl.pallas_call(
        body, name=name, in_specs=[hbm] * (nin + nout),
        out_shape=tuple(ex.scratch) + tuple(thru) + (jax.ShapeDtypeStruct((8, 128), F32),),
        out_specs=tuple([sem] * nsem + [hbm] * (nin + nout) + [pl.BlockSpec(memory_space=pltpu.VMEM)]),
        input_output_aliases={q: nsem + q for q in range(nin + nout)},
        compiler_params=pltpu.CompilerParams(has_side_effects=pltpu.SideEffectType.DATAFLOW_SIDE_EFFECTING))(*args)


def _finish_exchange(ex, handles, after, name):
    nin, nout, nsem = len(ex.arrays), len(ex.out_shapes), len(ex.scratch)
    hbm = pl.BlockSpec(memory_space=pltpu.HBM)
    sem = pl.BlockSpec(memory_space=pltpu.SEMAPHORE)
    sems, thru = handles[:nsem], handles[nsem:nsem + nin + nout]

    def body(*refs):
        ins, lands = refs[:nin], refs[nin:nin + nout]
        ex.wait(ins, lands, refs[nin + nout:nin + nout + nsem])

    outs = pl.pallas_call(
        body, name=name, in_specs=[hbm] * (nin + nout) + [sem] * nsem + [pl.BlockSpec(memory_space=pl.ANY)],
        out_shape=tuple(pltpu.HBM(t.shape, t.dtype) for t in thru), out_specs=tuple([hbm] * (nin + nout)),
        input_output_aliases={q: q for q in range(nin + nout)},
        compiler_params=pltpu.CompilerParams(has_side_effects=pltpu.SideEffectType.DATAFLOW_SIDE_EFFECTING))(
            *thru, *sems, after)
    return outs[nin:]


def _all_gather_devices(v, name):
    def body(v_ref, o_ref, send_sems, recv_sems, loc_sem):
        x, y, c = lax.axis_index("x"), lax.axis_index("y"), lax.axis_index("c")
        me = 4 * x + 2 * y + c
        lc = pltpu.make_async_copy(v_ref, o_ref.at[me], loc_sem)
        lc.start()
        rel = [(bx, by, bc) for bx in (0, 1) for by in (0, 1) for bc in (0, 1)][1:]
        copies = []
        for j, (bx, by, bc) in enumerate(rel):
            px, py, pc = x ^ bx, y ^ by, c ^ bc
            copies.append((pltpu.make_async_remote_copy(
                src_ref=v_ref, dst_ref=o_ref.at[me], send_sem=send_sems.at[j], recv_sem=recv_sems.at[j],
                device_id=(px, py, pc), device_id_type=MESH), 4 * px + 2 * py + pc))
        for cp, _ in copies:
            cp.start()
        for j, (cp, pid) in enumerate(copies):
            pltpu.make_async_remote_copy(
                src_ref=v_ref, dst_ref=o_ref.at[pid], send_sem=send_sems.at[j], recv_sem=recv_sems.at[j],
                device_id=(x, y, c), device_id_type=MESH).wait_recv()
        for cp, _ in copies:
            cp.wait_send()
        lc.wait()

    anyspec = pl.BlockSpec(memory_space=pl.ANY)
    return pl.pallas_call(
        body, in_specs=[anyspec], out_specs=anyspec,
        out_shape=jax.ShapeDtypeStruct((N_DEV,) + v.shape, v.dtype),
        scratch_shapes=[pltpu.SemaphoreType.DMA((N_DEV - 1,)), pltpu.SemaphoreType.DMA((N_DEV - 1,)),
                        pltpu.SemaphoreType.DMA(())],
        name=name)(v)


BIG = ("attn_w_qkv", "attn_w_o", "ssm_w_in", "ssm_w_out", "ffn_w_up", "ffn_w_down")
BIG_KIND = {"attn_w_qkv": "col", "attn_w_o": "row", "ssm_w_in": "lead", "ssm_w_out": "row",
            "ffn_w_up": "col", "ffn_w_down": "row"}
SMALL_SHARDED = {"ssm_conv_w": 2, "ssm_conv_b": 1, "ssm_norm_w": 1, "ffn_conv_w": 2}
SMALL = ("mix_norm_w", "ssm_conv_w", "ssm_conv_b", "ssm_dt_bias", "ssm_a_log", "ssm_d", "ssm_norm_w",
         "ffn_norm_w", "ffn_conv_w", "ffn_conv_b", "final_norm_w")
WEIGHTS = ("mix_norm_w", "attn_w_qkv", "attn_w_o", "ssm_w_in", "ssm_conv_w", "ssm_conv_b", "ssm_dt_bias",
           "ssm_a_log", "ssm_d", "ssm_norm_w", "ssm_w_out", "ffn_norm_w", "ffn_w_up", "ffn_conv_w",
           "ffn_conv_b", "ffn_w_down", "final_norm_w")


def _shard_extent(name, shape):
    _, a, b = shape
    return {"col": b, "row": a, "lead": 1}[BIG_KIND[name]]


def _gather_item(w16, name):
    a, b = w16.shape
    kind = BIG_KIND[name]
    full = {"col": (a, N_CHIPS * b), "row": (N_CHIPS * a, b), "lead": (N_CHIPS, a, b)}[kind]
    return (w16, None, kind, _shard_extent(name, (1, a, b)), full)


def _layer_weights(i):
    j = i // 2
    mixer = [("attn_w_qkv", j), ("attn_w_o", j)] if i % 2 == 0 else [("ssm_w_in", j), ("ssm_w_out", j)]
    return mixer + [("ffn_w_up", i), ("ffn_w_down", i)]


def kernel(x, mix_norm_w, attn_w_qkv, attn_w_o, ssm_w_in, ssm_conv_w, ssm_conv_b, ssm_dt_bias, ssm_a_log, ssm_d, ssm_norm_w, ssm_w_out, ffn_norm_w, ffn_w_up, ffn_conv_w, ffn_conv_b, ffn_w_down, final_norm_w, loss_target, m_mix_norm_w, m_attn_w_qkv, m_attn_w_o, m_ssm_w_in, m_ssm_conv_w, m_ssm_conv_b, m_ssm_dt_bias, m_ssm_a_log, m_ssm_d, m_ssm_norm_w, m_ssm_w_out, m_ffn_norm_w, m_ffn_w_up, m_ffn_conv_w, m_ffn_conv_b, m_ffn_w_down, m_final_norm_w, v_mix_norm_w, v_attn_w_qkv, v_attn_w_o, v_ssm_w_in, v_ssm_conv_w, v_ssm_conv_b, v_ssm_dt_bias, v_ssm_a_log, v_ssm_d, v_ssm_norm_w, v_ssm_w_out, v_ffn_norm_w, v_ffn_w_up, v_ffn_conv_w, v_ffn_conv_b, v_ffn_w_down, v_final_norm_w):
    W = dict(mix_norm_w=mix_norm_w, attn_w_qkv=attn_w_qkv, attn_w_o=attn_w_o, ssm_w_in=ssm_w_in,
             ssm_conv_w=ssm_conv_w, ssm_conv_b=ssm_conv_b, ssm_dt_bias=ssm_dt_bias, ssm_a_log=ssm_a_log,
             ssm_d=ssm_d, ssm_norm_w=ssm_norm_w, ssm_w_out=ssm_w_out, ffn_norm_w=ffn_norm_w, ffn_w_up=ffn_w_up,
             ffn_conv_w=ffn_conv_w, ffn_conv_b=ffn_conv_b, ffn_w_down=ffn_w_down, final_norm_w=final_norm_w)
    M = dict(mix_norm_w=m_mix_norm_w, attn_w_qkv=m_attn_w_qkv, attn_w_o=m_attn_w_o, ssm_w_in=m_ssm_w_in,
             ssm_conv_w=m_ssm_conv_w, ssm_conv_b=m_ssm_conv_b, ssm_dt_bias=m_ssm_dt_bias, ssm_a_log=m_ssm_a_log,
             ssm_d=m_ssm_d, ssm_norm_w=m_ssm_norm_w, ssm_w_out=m_ssm_w_out, ffn_norm_w=m_ffn_norm_w,
             ffn_w_up=m_ffn_w_up, ffn_conv_w=m_ffn_conv_w, ffn_conv_b=m_ffn_conv_b, ffn_w_down=m_ffn_w_down,
             final_norm_w=m_final_norm_w)
    V = dict(mix_norm_w=v_mix_norm_w, attn_w_qkv=v_attn_w_qkv, attn_w_o=v_attn_w_o, ssm_w_in=v_ssm_w_in,
             ssm_conv_w=v_ssm_conv_w, ssm_conv_b=v_ssm_conv_b, ssm_dt_bias=v_ssm_dt_bias, ssm_a_log=v_ssm_a_log,
             ssm_d=v_ssm_d, ssm_norm_w=v_ssm_norm_w, ssm_w_out=v_ssm_w_out, ffn_norm_w=v_ffn_norm_w,
             ffn_w_up=v_ffn_w_up, ffn_conv_w=v_ffn_conv_w, ffn_conv_b=v_ffn_conv_b, ffn_w_down=v_ffn_w_down,
             final_norm_w=v_final_norm_w)

    S, D = x.shape[1], x.shape[2]
    xs = x.reshape(S, D)
    tgt = loss_target.reshape(S, D)
    depth = mix_norm_w.shape[0]
    heads = attn_w_o.shape[1] * N_CHIPS // HEAD_DIM
    AW = heads * HEAD_DIM
    d_inner = ssm_w_out.shape[1] * N_CHIPS
    ssm_heads = d_inner // SSM_HEAD_DIM
    G, P, N = SSM_GROUPS, SSM_HEAD_DIM, SSM_STATE
    gw = d_inner // G
    conv_dim = d_inner + 2 * G * N
    in_w = d_inner + conv_dim + ssm_heads
    in_pad = -(-in_w // 128) * 128
    shard_in = ssm_w_in.shape[2]
    xi, yi = lax.axis_index("x"), lax.axis_index("y")
    chip = 2 * xi + yi

    full = {}

    def land(keys, outs):
        for (n, l), o in zip(keys, outs):
            if n == "ssm_w_in":
                o = jnp.pad(jnp.concatenate([o[k] for k in range(N_CHIPS)], axis=1), ((0, 0), (0, in_pad - in_w)))
            full[(n, l)] = o

    def gather_ex(keys, extra=()):
        return _Exchange("gather", [_gather_item(W[n][l].astype(BF16), n) for n, l in keys] + list(extra))

    def fwd_mm(a, b, name, resid=None):
        return _matmul(a, b, "nn", F32, name, resid=resid)

    sm_names = list(SMALL_SHARDED)
    packed = _pack([W[n] for n in sm_names])
    lw = [_layer_weights(i) for i in range(depth)]
    first = lw[0][:1]
    got = _run_exchange(gather_ex(first, [(packed, None, "lead", 1, (N_CHIPS,) + packed.shape)]), "gather_first")
    land(first, got[:-1])
    batches = [lw[0][1:]] + lw[1:]
    pending = []
    for q, keys in enumerate(batches):
        ex = gather_ex(keys)
        pending.append((keys, ex, _start_exchange(ex, f"gather_start_{q}")))

    def arrive(q, after):
        keys, ex, handles = pending[q]
        land(keys, _finish_exchange(ex, handles, after, f"gather_wait_{q}"))
    per_chip = [_unpack(got[-1][k], [W[n].shape for n in sm_names]) for k in range(N_CHIPS)]
    for q, n in enumerate(sm_names):
        full[n] = jnp.concatenate([per_chip[k][q] for k in range(N_CHIPS)], axis=SMALL_SHARDED[n])
    tab = _perm_tokens(_rope_table(S))

    def rep_heads(p):
        return jnp.repeat(p, P).reshape(G, 1, gw)

    saved = []
    cur = xs
    for i in range(depth):
        j = i // 2
        if i > 0:
            arrive(i, cur)
        sv = {"x_in": cur}
        h = _rms_fwd(cur, mix_norm_w[i], f"mix_norm_fwd_{i}")
        sv["h"] = h
        if i % 2 == 0:
            h = _perm_tokens(h)
            sv["h"] = h
            qkv = fwd_mm(h, full[("attn_w_qkv", j)], f"qkv_fwd_{i}")
            if i == 0:
                arrive(0, qkv)
            og = [_attn_fwd(qkv, tab, g, heads, f"attn_fwd_{i}_{g}") for g in range(3)]
            o, lse = _attn_combine([a for a, _ in og], [b for _, b in og], f"attn_combine_{i}")
            mixed = _unperm_tokens(fwd_mm(o, full[("attn_w_o", j)], f"attn_out_fwd_{i}"))
            sv.update(qkv=qkv, o=o, lse=lse)
        else:
            zx = fwd_mm(h, full[("ssm_w_in", j)], f"ssm_in_fwd_{i}")
            conv = _ssm_conv_fwd(zx, full["ssm_conv_w"][j], full["ssm_conv_b"][j], d_inner, conv_dim,
                                 f"ssm_conv_fwd_{i}")
            dtr = jnp.repeat(zx[:, d_inner + conv_dim:in_w], P, axis=1)
            prm = [rep_heads(p[j]) for p in (ssm_dt_bias, ssm_a_log, ssm_d)]
            y, sprev = _ssd_fwd(conv, dtr, *prm, d_inner, f"ssd_fwd_{i}")
            gated = _gnorm_fwd(y, zx, full["ssm_norm_w"][j], f"ssm_norm_fwd_{i}")
            cur = fwd_mm(gated, full[("ssm_w_out", j)], f"ssm_out_fwd_{i}", resid=cur)
            sv.update(zx=zx, conv=conv, dtr=dtr, prm=prm, y=y, sprev=sprev, gated=gated)
            mixed = None
        if mixed is None:
            h2 = _rms_fwd(cur, ffn_norm_w[i], f"ffn_norm_fwd_{i}")
        else:
            cur, h2 = _rms_fwd(cur, ffn_norm_w[i], f"ffn_norm_fwd_{i}", add=mixed)
        sv["x_mid"] = cur
        up = fwd_mm(h2, full[("ffn_w_up", i)], f"ffn_up_fwd_{i}")
        u2, act = _ffn_conv_fwd(up, full["ffn_conv_w"][i], ffn_conv_b[i], f"ffn_conv_fwd_{i}")
        cur = fwd_mm(act, full[("ffn_w_down", i)], f"ffn_down_fwd_{i}", resid=cur)
        sv.update(h2=h2, up=up, u2=u2, act=act)
        saved.append(sv)

    dx, d_final, loss_part = _loss_head(cur, final_norm_w, tgt, "loss_head")
    gbig, recv = {}, {}
    gs = {n: [None] * W[n].shape[0] for n in SMALL if n != "final_norm_w"}

    def scatter_ex(keys):
        return _Exchange("scatter", [(gbig[(n, l)], BIG_KIND[n], _shard_extent(n, W[n].shape),
                                      (N_CHIPS,) + W[n].shape[1:]) for n, l in keys])

    sib = {}

    def swap_ex(keys):
        return _Exchange("swap", [(recv[k], recv[k].shape) for k in keys])

    def bwd_mm(a, b, mode, dtype, name, send=(), swap=()):
        if not send and not swap:
            return _matmul(a, b, mode, dtype, name)
        parts = ([scatter_ex(send)] if send else []) + ([swap_ex(swap)] if swap else [])
        out, got = _matmul(a, b, mode, dtype, name, carry=_Multi(parts))
        recv.update(zip(send, got[:len(send)]))
        sib.update(zip(swap, got[len(send):]))
        return out

    sending = None
    to_swap = []
    for i in reversed(range(depth)):
        j = i // 2
        sv = saved[i]
        k_in, k_out, k_up, k_down = _layer_weights(i)
        own = i == 0

        def now(keys):
            return keys if own else []
        dact = _matmul(dx, full[k_down], "nt", F32, f"ffn_down_dgrad_{i}")
        gbig[k_down] = bwd_mm(sv["act"], dx, "tn", BF16, f"ffn_down_wgrad_{i}", swap=to_swap)
        to_swap = []
        dup, dcw, dcb = _ffn_conv_bwd(sv["u2"], dact, sv["up"], full["ffn_conv_w"][i], f"ffn_conv_bwd_{i}")
        gs["ffn_conv_w"][i], gs["ffn_conv_b"][i] = dcw, dcb[0]
        dh2 = bwd_mm(dup, full[k_up], "nt", F32, f"ffn_up_dgrad_{i}", send=now([k_down]))
        gbig[k_up] = bwd_mm(sv["h2"], dup, "tn", BF16, f"ffn_up_wgrad_{i}", swap=now([k_down]))
        dx, dnw = _rms_bwd(sv["x_mid"], ffn_norm_w[i], dh2, dx, f"ffn_norm_bwd_{i}")
        gs["ffn_norm_w"][i] = dnw[0]
        if i % 2 == 0:
            dxp = _perm_tokens(dx)
            do = _matmul(dxp, full[k_out], "nt", F32, f"attn_out_dgrad_{i}")
            gbig[k_out] = _matmul(sv["o"], dxp, "tn", BF16, f"attn_out_wgrad_{i}")
            dqkv = None
            for g in range(3):
                dqkv = _attn_bwd(sv["qkv"], tab, sv["o"], sv["lse"], do, dqkv, g, heads, f"attn_bwd_{i}_{g}")
            gbig[k_in] = bwd_mm(sv["h"], dqkv, "tn", BF16, f"qkv_wgrad_{i}", send=now([k_up, k_out]))
            dh = _unperm_tokens(bwd_mm(dqkv, full[k_in], "nt", F32, f"qkv_dgrad_{i}", send=now([k_in]),
                                       swap=now([k_up, k_out])))
        else:
            dgated = _matmul(dx, full[k_out], "nt", F32, f"ssm_out_dgrad_{i}")
            gbig[k_out] = _matmul(sv["gated"], dx, "tn", BF16, f"ssm_out_wgrad_{i}")
            dy, dz, dgw = _gnorm_bwd(sv["y"], sv["zx"], full["ssm_norm_w"][j], dgated, f"ssm_norm_bwd_{i}")
            gs["ssm_norm_w"][j] = dgw[0]
            dxs_, dbm, dcm, ddtr, dbias, dalog, ddsk = _ssd_bwd(
                sv["conv"], sv["dtr"], *sv["prm"], sv["sprev"], dy, d_inner, f"ssd_bwd_{i}")
            gs["ssm_dt_bias"][j] = dbias.reshape(-1)[::P]
            gs["ssm_a_log"][j] = dalog.reshape(-1)[::P]
            gs["ssm_d"][j] = ddsk.reshape(ssm_heads, P).sum(axis=1)
            ddt = jnp.pad(ddtr[:, ::P], ((0, 0), (0, in_pad - in_w)))
            dzx, dcw, dcb = _ssm_conv_bwd(dxs_, dbm, dcm, sv["conv"], sv["zx"], dz, ddt, full["ssm_conv_w"][j],
                                          d_inner, f"ssm_conv_bwd_{i}")
            gs["ssm_conv_w"][j], gs["ssm_conv_b"][j] = dcw, dcb[0]
            dwin = bwd_mm(sv["h"], dzx, "tn", BF16, f"ssm_in_wgrad_{i}", send=now([k_up, k_out]))
            gbig[k_in] = jnp.stack([dwin[:, k * shard_in:(k + 1) * shard_in] for k in range(N_CHIPS)])
            dh = bwd_mm(dzx, full[k_in], "nt", F32, f"ssm_in_dgrad_{i}", send=now([k_in]), swap=now([k_up, k_out]))
        if sending is not None:
            keys, ex, handles = sending
            recv.update(zip(keys, _finish_exchange(ex, handles, dh, f"scatter_wait_{i + 1}")))
            to_swap, sending = keys, None
        if own:
            to_swap = to_swap + [k_in]
        else:
            keys = [k_in, k_out, k_up, k_down]
            ex = scatter_ex(keys)
            sending = (keys, ex, _start_exchange(ex, f"scatter_start_{i}"))
        dx, dnw = _rms_bwd(sv["x_in"], mix_norm_w[i], dh, dx, f"mix_norm_bwd_{i}")
        gs["mix_norm_w"][i] = dnw[0]
    grad_x = dx.reshape(x.shape)

    sib.update(zip(to_swap, _run_exchange(swap_ex(to_swap), "swap_last")))

    small_full = [jnp.stack(gs[n]) if n != "final_norm_w" else d_final[0] for n in SMALL]
    small_full.append(loss_part[0, 0:1])
    small_shapes = [a.shape for a in small_full]
    summed = _sum_devices(_all_gather_devices(_pack(small_full), "gather_small_grads"), "sum_small_grads")
    small_g = _unpack(summed, small_shapes)
    loss = small_g[-1][0]
    gsm = {}
    for n, g in zip(SMALL, small_g[:-1]):
        if n in SMALL_SHARDED:
            ax = SMALL_SHARDED[n]
            ext = W[n].shape[ax]
            g = lax.dynamic_slice_in_dim(g, chip * ext, ext, axis=ax)
        gsm[n] = g

    out_g, out_d, out_m, out_v = {}, {}, {}, {}
    for name in BIG:
        outs = None
        for l in range(W[name].shape[0]):
            outs = _adamw_big(recv[(name, l)], sib[(name, l)], W[name], M[name], V[name], l, outs,
                              f"adamw_{name}_{l}")
        out_g[name], out_d[name], out_m[name], out_v[name] = outs
    shapes = [W[n].shape for n in SMALL]
    pd, pm, pv = _adamw_small(_pack([gsm[n] for n in SMALL]), _pack([W[n] for n in SMALL]),
                              _pack([M[n] for n in SMALL]), _pack([V[n] for n in SMALL]), "adamw_small")
    for n, d_, m_, v_ in zip(SMALL, _unpack(pd, shapes), _unpack(pm, shapes), _unpack(pv, shapes)):
        out_g[n], out_d[n], out_m[n], out_v[n] = gsm[n], d_, m_, v_

    return (loss, grad_x, *[out_g[n] for n in WEIGHTS], *[out_d[n] for n in WEIGHTS],
            *[out_m[n] for n in WEIGHTS], *[out_v[n] for n in WEIGHTS])
```

```python
import functools
import math

import jax
import jax.numpy as jnp
from jax import lax
from jax.experimental import pallas as pl
from jax.experimental.pallas import tpu as pltpu

F32 = jnp.float32
BF16 = jnp.bfloat16
MESH = pl.DeviceIdType.MESH

NORM_EPS = 1e-5
HEAD_DIM = 128
ATTN_BLOCK = 128
ATTN_DILATIONS = (1, 4, 16)
ATTN_WINDOWS = (128, 512, 2048)
PERM = 16
ROPE_THETA = 500000.0
ROPE_HALF = HEAD_DIM // 8
SSM_HEAD_DIM = 64
SSM_STATE = 128
SSM_GROUPS = 8
SSM_CHUNK = 128
NEG = -1e30

ADAM_LR = 0.001
ADAM_B1 = 0.9
ADAM_B2 = 0.999
ADAM_EPS = 1e-08
ADAM_WD = 0.01
ADAM_STEP = 10

VMEM_LIMIT_BYTES = 48 * 1024 * 1024
N_CHIPS = 4
N_DEV = 8


def _cp(*sem):
    return pltpu.CompilerParams(dimension_semantics=sem, vmem_limit_bytes=VMEM_LIMIT_BYTES)


def _pick(n, pref, mult=128):
    best = None
    t = mult
    while t <= min(n, pref):
        if n % t == 0:
            best = t
        t += mult
    return n if best is None else best


def _sigmoid(x):
    return 1.0 / (1.0 + jnp.exp(-x))


def _silu(x):
    return x * _sigmoid(x)


def _softplus(x):
    u = jnp.exp(-jnp.abs(x))
    w = 1.0 + u
    log1p = jnp.where(w == 1.0, u, jnp.log(w) * (u / jnp.where(w == 1.0, 1.0, w - 1.0)))
    return jnp.maximum(x, 0.0) + log1p


def _dot(a, b, dims):
    return lax.dot_general(a.astype(BF16), b.astype(BF16), (dims, ((), ())),
                           preferred_element_type=F32)


def _dot_nn(a, b):
    return _dot(a, b, ((1,), (0,)))


def _dot_nt(a, b):
    return _dot(a, b, ((1,), (1,)))


def _dot_tn(a, b):
    return _dot(a, b, ((0,), (0,)))


MATMUL_VMEM_BYTES = 36 * 1024 * 1024
MATMUL_TILES = (2048, 1536, 1408, 1152, 1024, 896, 768, 640, 512, 384, 256, 128)


def _matmul_tiles(M, N, K, a_bytes, b_bytes, o_bytes, has_resid):
    best = None
    for tm in [t for t in MATMUL_TILES if M % t == 0] or [M]:
        for tn in [t for t in MATMUL_TILES if N % t == 0] or [N]:
            for tk in [t for t in MATMUL_TILES if K % t == 0 and t <= 1408] or [K]:
                nk, gm, gn = K // tk, M // tm, N // tn
                vmem = 2 * (tm * tk * a_bytes + tk * tn * b_bytes + tm * tn * o_bytes)
                vmem += (2 * tm * tn * 4 if has_resid else 0) + (tm * tn * 4 if nk > 1 else 0)
                if vmem > MATMUL_VMEM_BYTES:
                    continue
                a_reads = 1 if nk == 1 else gn
                b_reads = 1 if (nk == 1 and gn == 1) else gm
                traffic = M * K * a_bytes * a_reads + K * N * b_bytes * b_reads + M * N * o_bytes
                key = (traffic, gm * gn * nk)
                if best is None or key < best[0]:
                    best = (key, (tm, tn, tk))
    assert best is not None, (M, N, K)
    return best[1]


def _matmul(a, b, mode, out_dtype, name, resid=None, carry=None):
    if mode == "nn":
        (M, K), (K2, N) = a.shape, b.shape
    elif mode == "nt":
        (M, K), (N, K2) = a.shape, b.shape
    else:
        (K, M), (K2, N) = a.shape, b.shape
    assert K == K2, (a.shape, b.shape, mode)
    tm, tn, tk = _matmul_tiles(M, N, K, a.dtype.itemsize, b.dtype.itemsize, jnp.dtype(out_dtype).itemsize,
                               resid is not None)
    nk = K // tk
    gm, gn = M // tm, N // tn
    dims = {"nn": ((1,), (0,)), "nt": ((1,), (1,)), "tn": ((0,), (0,))}[mode]
    has_resid = resid is not None
    nci = len(carry.arrays) if carry else 0
    nco = len(carry.out_shapes) if carry else 0

    def body(a_ref, b_ref, *rest):
        r_ref = rest[0] if has_resid else None
        rest = rest[has_resid:]
        c_ins, o_ref, c_outs, scratch = rest[:nci], rest[nci], rest[nci + 1:nci + 1 + nco], rest[nci + 1 + nco:]
        acc_ref = scratch[0] if nk > 1 else None
        sems = scratch[nk > 1:]
        i, j, k = pl.program_id(0), pl.program_id(1), pl.program_id(2)
        if carry:
            @pl.when((i == 0) & (j == 0) & (k == 0))
            def _():
                carry.start(c_ins, c_outs, sems)

        if nk == 1:
            r = _dot(a_ref[...], b_ref[...], dims)
            if has_resid:
                r = r + r_ref[...]
            o_ref[...] = r.astype(o_ref.dtype)
        else:
            @pl.when(k == 0)
            def _():
                acc_ref[...] = jnp.zeros_like(acc_ref)

            acc_ref[...] += _dot(a_ref[...], b_ref[...], dims)

            @pl.when(k == nk - 1)
            def _():
                r = acc_ref[...]
                if has_resid:
                    r = r + r_ref[...]
                o_ref[...] = r.astype(o_ref.dtype)

        if carry:
            @pl.when((i == gm - 1) & (j == gn - 1) & (k == nk - 1))
            def _():
                carry.wait(c_ins, c_outs, sems)

    if mode == "nn":
        a_spec = pl.BlockSpec((tm, tk), lambda i, j, k: (i, k))
        b_spec = pl.BlockSpec((tk, tn), lambda i, j, k: (k, j))
    elif mode == "nt":
        a_spec = pl.BlockSpec((tm, tk), lambda i, j, k: (i, k))
        b_spec = pl.BlockSpec((tn, tk), lambda i, j, k: (j, k))
    else:
        a_spec = pl.BlockSpec((tk, tm), lambda i, j, k: (k, i))
        b_spec = pl.BlockSpec((tk, tn), lambda i, j, k: (k, j))
    o_spec = pl.BlockSpec((tm, tn), lambda i, j, k: (i, j))
    anyspec = pl.BlockSpec(memory_space=pl.ANY)
    in_specs = [a_spec, b_spec] + ([o_spec] if has_resid else []) + [anyspec] * nci
    args = (a, b) + ((resid,) if has_resid else ()) + (tuple(carry.arrays) if carry else ())
    out_shape = [jax.ShapeDtypeStruct((M, N), out_dtype)] + (carry.out_shapes if carry else [])
    scratch = ([] if nk == 1 else [pltpu.VMEM((tm, tn), F32)]) + (carry.scratch if carry else [])
    sem = ("arbitrary",) * 3 if carry else ("parallel", "parallel", "arbitrary")
    outs = pl.pallas_call(
        body, grid=(gm, gn, nk), in_specs=in_specs, out_specs=[o_spec] + [anyspec] * nco,
        out_shape=out_shape, scratch_shapes=scratch, compiler_params=_cp(*sem), name=name)(*args)
    return (outs[0], outs[1:]) if carry else outs[0]


def _rms_fwd(x, w, name, add=None):
    S, D = x.shape
    t = _pick(S, 512, 8)
    has_add = add is not None

    def body(x_ref, w_ref, *rest):
        xv = x_ref[...]
        if has_add:
            xv = xv + rest[0][...]
            rest[1][...] = xv
        r = lax.rsqrt(jnp.mean(xv * xv, axis=-1, keepdims=True) + NORM_EPS)
        rest[-1][...] = (xv * r * w_ref[...]).astype(rest[-1].dtype)

    row = pl.BlockSpec((t, D), lambda i: (i, 0))
    vec = pl.BlockSpec((1, D), lambda i: (0, 0))
    normed = jax.ShapeDtypeStruct((S, D), BF16)
    if not has_add:
        return pl.pallas_call(body, grid=(S // t,), in_specs=[row, vec], out_specs=row, out_shape=normed,
                              compiler_params=_cp("parallel"), name=name)(x, w.reshape(1, D))
    return pl.pallas_call(body, grid=(S // t,), in_specs=[row, vec, row], out_specs=[row, row],
                          out_shape=[jax.ShapeDtypeStruct((S, D), F32), normed],
                          compiler_params=_cp("parallel"), name=name)(x, w.reshape(1, D), add)


def _rms_bwd(x, w, dh, dres, name):
    S, D = x.shape
    t = _pick(S, 512, 8)

    def body(x_ref, w_ref, dh_ref, dr_ref, dx_ref, dw_ref):
        @pl.when(pl.program_id(0) == 0)
        def _():
            dw_ref[...] = jnp.zeros_like(dw_ref)

        xv = x_ref[...]
        r = lax.rsqrt(jnp.mean(xv * xv, axis=-1, keepdims=True) + NORM_EPS)
        xh = xv * r
        dh_v = dh_ref[...]
        g = dh_v * w_ref[...]
        dx_ref[...] = dr_ref[...] + r * (g - xh * jnp.mean(g * xh, axis=-1, keepdims=True))
        dw_ref[...] += jnp.sum(dh_v * xh, axis=0, keepdims=True)

    row = pl.BlockSpec((t, D), lambda i: (i, 0))
    vec = pl.BlockSpec((1, D), lambda i: (0, 0))
    return pl.pallas_call(
        body, grid=(S // t,), in_specs=[row, vec, row, row], out_specs=[row, vec],
        out_shape=[jax.ShapeDtypeStruct((S, D), F32), jax.ShapeDtypeStruct((1, D), F32)],
        compiler_params=_cp("arbitrary"), name=name)(x, w.reshape(1, D), dh, dres)


def _loss_head(x, w, tgt, name):
    S, D = x.shape
    t = _pick(S, 512, 8)

    def body(x_ref, w_ref, t_ref, dx_ref, dw_ref, l_ref):
        @pl.when(pl.program_id(0) == 0)
        def _():
            dw_ref[...] = jnp.zeros_like(dw_ref)
            l_ref[...] = jnp.zeros_like(l_ref)

        xv = x_ref[...]
        wv = w_ref[...]
        r = lax.rsqrt(jnp.mean(xv * xv, axis=-1, keepdims=True) + NORM_EPS)
        xh = xv * r
        err = xh * wv - t_ref[...]
        per_tok = jnp.mean(err * err, axis=-1, keepdims=True)
        l_ref[...] += 0.5 * jnp.sum(per_tok, axis=0, keepdims=True)
        dy = err * (1.0 / D)
        g = dy * wv
        dx_ref[...] = r * (g - xh * jnp.mean(g * xh, axis=-1, keepdims=True))
        dw_ref[...] += jnp.sum(dy * xh, axis=0, keepdims=True)

    row = pl.BlockSpec((t, D), lambda i: (i, 0))
    vec = pl.BlockSpec((1, D), lambda i: (0, 0))
    lspec = pl.BlockSpec((1, 128), lambda i: (0, 0))
    return pl.pallas_call(
        body, grid=(S // t,), in_specs=[row, vec, row], out_specs=[row, vec, lspec],
        out_shape=[jax.ShapeDtypeStruct((S, D), F32), jax.ShapeDtypeStruct((1, D), F32),
                   jax.ShapeDtypeStruct((1, 128), F32)],
        compiler_params=_cp("arbitrary"), name=name)(x, w.reshape(1, D), tgt)


def _rope_table(seq):
    pos = jnp.arange(seq, dtype=F32)
    inv_freq = ROPE_THETA ** (-jnp.arange(0, 2 * ROPE_HALF, 2, dtype=F32) / (2 * ROPE_HALF))
    ang = pos[:, None] * inv_freq[None, :]
    cos, sin = jnp.cos(ang), jnp.sin(ang)
    pad = HEAD_DIM - 2 * ROPE_HALF
    cos_p = jnp.concatenate([cos, cos, jnp.ones((seq, pad), F32)], axis=1)
    sin_a = jnp.concatenate([-sin, jnp.zeros((seq, HEAD_DIM - ROPE_HALF), F32)], axis=1)
    sin_b = jnp.concatenate([jnp.zeros((seq, ROPE_HALF), F32), sin, jnp.zeros((seq, pad), F32)], axis=1)
    return jnp.concatenate([cos_p, sin_a, sin_b], axis=1)


def _rope(t, tab, sign):
    cos_p = tab[:, 0:HEAD_DIM]
    sin_a = tab[:, HEAD_DIM:2 * HEAD_DIM]
    sin_b = tab[:, 2 * HEAD_DIM:3 * HEAD_DIM]
    up = pltpu.roll(t, HEAD_DIM - ROPE_HALF, 1)
    down = pltpu.roll(t, ROPE_HALF, 1)
    return t * cos_p + sign * (up * sin_a + down * sin_b)


def _perm_tokens(a):
    S = a.shape[0]
    return a.reshape(S // PERM, PERM, -1).transpose(1, 0, 2).reshape(S, -1)


def _unperm_tokens(a):
    S = a.shape[0]
    return a.reshape(PERM, S // PERM, -1).transpose(1, 0, 2).reshape(S, -1)


class _Strided:
    def __init__(self, S, dil):
        self.dil, self.m = dil, PERM // dil
        self.c = ATTN_BLOCK // self.m
        self.rows = S // PERM
        self.nb = S // (dil * ATTN_BLOCK)

    def view(self, a):
        return a.reshape(self.m, self.dil, self.rows, a.shape[-1])

    def spec(self, width, col, f=lambda n: n):
        return pl.BlockSpec((self.m, None, self.c, width), lambda r, n: (0, r, f(n), col))

    def load(self, ref, sl=slice(None)):
        if self.m == 1:
            return ref[0, :, sl]
        return jnp.concatenate([ref[q, :, sl] for q in range(self.m)], axis=0)

    def store(self, ref, sl, val):
        for q in range(self.m):
            ref[q, :, sl] = val[q * self.c:(q + 1) * self.c, :]

    def member(self, i):
        shift = self.c.bit_length() - 1
        return (i & (self.c - 1)) * self.m + (i >> shift)


def _attn_fwd(qkv, tab, g, heads, name):
    S = qkv.shape[0]
    W = heads * HEAD_DIM
    dil = ATTN_DILATIONS[g]
    steps = ATTN_WINDOWS[g] // dil
    B = ATTN_BLOCK
    scale = HEAD_DIM ** -0.5
    st = _Strided(S, dil)

    def body(q_ref, k_ref, kp_ref, v_ref, vp_ref, t_ref, tp_ref, o_ref, l_ref):
        n = pl.program_id(1)
        ii = lax.broadcasted_iota(jnp.int32, (B, 2 * B), 0)
        jj = lax.broadcasted_iota(jnp.int32, (B, 2 * B), 1)
        delta = st.member(ii) - st.member(jj & (B - 1)) + jnp.where(jj >= B, 0, B)
        ok = (delta >= 0) & (delta <= steps) & ((jj >= B) | (n > 0))
        tb = st.load(t_ref)
        tpv = st.load(tp_ref)
        for h in range(heads):
            sl = slice(h * HEAD_DIM, (h + 1) * HEAD_DIM)
            q = _rope(st.load(q_ref, sl), tb, 1.0)
            kc = jnp.concatenate([_rope(st.load(kp_ref, sl), tpv, 1.0), _rope(st.load(k_ref, sl), tb, 1.0)], axis=0)
            vc = jnp.concatenate([st.load(vp_ref, sl), st.load(v_ref, sl)], axis=0)
            s = jnp.where(ok, _dot_nt(q, kc) * scale, NEG)
            m = jnp.max(s, axis=-1, keepdims=True)
            p = jnp.exp(s - m)
            den = jnp.sum(p, axis=-1, keepdims=True)
            st.store(o_ref, sl, _dot_nn(p, vc) / den)
            st.store(l_ref, sl, jnp.broadcast_to(m + jnp.log(den), (B, HEAD_DIM)))

    prv = lambda n: jnp.maximum(n - 1, 0)
    qv, tv = st.view(qkv), st.view(tab)
    o_spec = st.spec(W, 0)
    o, lse = pl.pallas_call(
        body, grid=(dil, st.nb),
        in_specs=[st.spec(W, g * 3), st.spec(W, g * 3 + 1), st.spec(W, g * 3 + 1, prv),
                  st.spec(W, g * 3 + 2), st.spec(W, g * 3 + 2, prv),
                  st.spec(3 * HEAD_DIM, 0), st.spec(3 * HEAD_DIM, 0, prv)],
        out_specs=[o_spec, o_spec],
        out_shape=[jax.ShapeDtypeStruct((st.m, dil, st.rows, W), F32)] * 2,
        compiler_params=_cp("parallel", "parallel"), name=name)(qv, qv, qv, qv, qv, tv, tv)
    return o.reshape(S, W), lse.reshape(S, W)


def _attn_combine(os_, ls_, name):
    S, W = os_[0].shape
    t = _pick(S, 256, 8)

    def body(o0, o1, o2, l0, l1, l2, o_ref, l_ref):
        a, b, c = l0[...], l1[...], l2[...]
        m = jnp.maximum(jnp.maximum(a, b), c)
        ea, eb, ec = jnp.exp(a - m), jnp.exp(b - m), jnp.exp(c - m)
        tot = ea + eb + ec
        o_ref[...] = (ea * o0[...] + eb * o1[...] + ec * o2[...]) / tot
        l_ref[...] = m + jnp.log(tot)

    row = pl.BlockSpec((t, W), lambda i: (i, 0))
    return pl.pallas_call(body, grid=(S // t,), in_specs=[row] * 6, out_specs=[row, row],
                          out_shape=[jax.ShapeDtypeStruct((S, W), F32)] * 2,
                          compiler_params=_cp("parallel"), name=name)(*os_, *ls_)


def _attn_bwd(qkv, tab, o, lse, do, dqkv_prev, g, heads, name):
    S = qkv.shape[0]
    W = heads * HEAD_DIM
    dil = ATTN_DILATIONS[g]
    steps = ATTN_WINDOWS[g] // dil
    B = ATTN_BLOCK
    scale = HEAD_DIM ** -0.5
    st = _Strided(S, dil)
    nb = st.nb
    aliased = dqkv_prev is not None

    def body(q_ref, qn_ref, k_ref, kp_ref, v_ref, vp_ref, do_ref, don_ref, o_ref, on_ref,
             l_ref, ln_ref, t_ref, tp_ref, tn_ref, *rest):
        out_ref = rest[-1]
        n = pl.program_id(1)
        has_next = n < nb - 1
        ia = lax.broadcasted_iota(jnp.int32, (B, 2 * B), 0)
        ja = lax.broadcasted_iota(jnp.int32, (B, 2 * B), 1)
        da = st.member(ia) - st.member(ja & (B - 1)) + jnp.where(ja >= B, 0, B)
        ok_a = (da >= 0) & (da <= steps) & ((ja >= B) | (n > 0))
        ib = lax.broadcasted_iota(jnp.int32, (2 * B, B), 0)
        jb = lax.broadcasted_iota(jnp.int32, (2 * B, B), 1)
        db = st.member(ib & (B - 1)) + jnp.where(ib >= B, B, 0) - st.member(jb)
        ok_b = (db >= 0) & (db <= steps) & ((ib < B) | has_next)
        tb, tpv, tnv = st.load(t_ref), st.load(tp_ref), st.load(tn_ref)
        for h in range(heads):
            sl = slice(h * HEAD_DIM, (h + 1) * HEAD_DIM)
            qr = _rope(st.load(q_ref, sl), tb, 1.0)
            qnr = _rope(st.load(qn_ref, sl), tnv, 1.0)
            kr = _rope(st.load(k_ref, sl), tb, 1.0)
            kpr = _rope(st.load(kp_ref, sl), tpv, 1.0)
            v = st.load(v_ref, sl)
            dov_ = st.load(do_ref, sl)
            donv = st.load(don_ref, sl)
            dl = jnp.sum(dov_ * st.load(o_ref, sl), axis=-1, keepdims=True)
            dln = jnp.sum(donv * st.load(on_ref, sl), axis=-1, keepdims=True)
            ls = st.load(l_ref, sl)
            kc = jnp.concatenate([kpr, kr], axis=0)
            vc = jnp.concatenate([st.load(vp_ref, sl), v], axis=0)
            s = _dot_nt(qr, kc) * scale
            p = jnp.where(ok_a, jnp.exp(jnp.minimum(s - jnp.concatenate([ls, ls], axis=1), 30.0)), 0.0)
            ds = p * (_dot_nt(dov_, vc) - dl) * scale
            st.store(out_ref, sl, _rope(_dot_nn(ds, kc), tb, -1.0))
            qc = jnp.concatenate([qr, qnr], axis=0)
            doc = jnp.concatenate([dov_, donv], axis=0)
            lc = jnp.concatenate([ls, st.load(ln_ref, sl)], axis=0)
            dlc = jnp.concatenate([dl, dln], axis=0)
            s2 = _dot_nt(qc, kr) * scale
            p2 = jnp.where(ok_b, jnp.exp(jnp.minimum(s2 - lc, 30.0)), 0.0)
            ds2 = p2 * (_dot_nt(doc, v) - dlc) * scale
            st.store(out_ref, slice(W + h * HEAD_DIM, W + (h + 1) * HEAD_DIM), _rope(_dot_tn(ds2, qc), tb, -1.0))
            st.store(out_ref, slice(2 * W + h * HEAD_DIM, 2 * W + (h + 1) * HEAD_DIM), _dot_tn(p2, doc))

    nxt = lambda n: jnp.minimum(n + 1, nb - 1)
    prv = lambda n: jnp.maximum(n - 1, 0)
    same = lambda n: n
    q0, q1, q2, tw = g * 3, g * 3 + 1, g * 3 + 2, 3 * HEAD_DIM
    in_specs = [st.spec(W, q0), st.spec(W, q0, nxt), st.spec(W, q1), st.spec(W, q1, prv),
                st.spec(W, q2), st.spec(W, q2, prv)]
    in_specs += [st.spec(W, 0, f) for f in (same, nxt, same, nxt, same, nxt)]
    in_specs += [st.spec(tw, 0, f) for f in (same, prv, nxt)]
    qv, tv, ov, lv, dov = (st.view(a) for a in (qkv, tab, o, lse, do))
    args = [qv, qv, qv, qv, qv, qv, dov, dov, ov, ov, lv, lv, tv, tv, tv]
    kwargs = {}
    if aliased:
        in_specs.append(pl.BlockSpec(memory_space=pl.ANY))
        args.append(st.view(dqkv_prev))
        kwargs["input_output_aliases"] = {len(args) - 1: 0}
    out = pl.pallas_call(
        body, grid=(dil, nb), in_specs=in_specs, out_specs=st.spec(3 * W, g),
        out_shape=jax.ShapeDtypeStruct((st.m, dil, st.rows, 9 * W), F32),
        compiler_params=_cp("parallel", "parallel"), name=name, **kwargs)(*args)
    return out.reshape(S, 9 * W)


def _shift_down(x, halo, s):
    if s == 0:
        return x
    T = x.shape[0]
    xs = pltpu.roll(x, s, 0)
    hs = pltpu.roll(halo, s, 0)
    row8 = lax.broadcasted_iota(jnp.int32, hs.shape, 0)
    top = jnp.where(row8 < s, hs, xs[0:8])
    return top if T == 8 else jnp.concatenate([top, xs[8:T]], axis=0)


def _shift_up(x, halo, s):
    if s == 0:
        return x
    T = x.shape[0]
    xs = pltpu.roll(x, T - s, 0)
    hs = pltpu.roll(halo, 8 - s, 0)
    row8 = lax.broadcasted_iota(jnp.int32, hs.shape, 0)
    bot = jnp.where(row8 >= 8 - s, hs, xs[T - 8:T])
    return jnp.concatenate([xs[0:T - 8], bot], axis=0)


CONV_ROWS = 128
CONV_LANES = 512


def _conv_apply(x, halo, w_ref, wsl, b, K):
    acc = x * w_ref[K - 1, :, wsl] + b
    for s in range(1, K):
        acc = acc + _shift_down(x, halo, s) * w_ref[K - 1 - s, :, wsl]
    return acc


def _conv_accum(dy, dyn, xv, xp, w_ref, dw_ref, db_ref, wsl, K):
    acc = dy * w_ref[K - 1, :, wsl]
    dw_ref[K - 1, :, wsl] += jnp.sum(dy * xv, axis=0, keepdims=True)
    for s in range(1, K):
        acc = acc + _shift_up(dy, dyn, s) * w_ref[K - 1 - s, :, wsl]
        dw_ref[K - 1 - s, :, wsl] += jnp.sum(dy * _shift_down(xv, xp, s), axis=0, keepdims=True)
    db_ref[:, wsl] += jnp.sum(dy, axis=0, keepdims=True)
    return acc


def _row_specs(T, S, width):
    main = pl.BlockSpec((T, width), lambda i: (i, 0))
    prev = pl.BlockSpec((8, width), lambda i: (jnp.maximum(i * (T // 8) - 1, 0), 0))
    nxt = pl.BlockSpec((8, width), lambda i: (jnp.minimum((i + 1) * (T // 8), S // 8 - 1), 0))
    return main, prev, nxt


def _full(shape):
    return pl.BlockSpec(shape, lambda i: (0,) * len(shape))


def _silu_grad(y):
    sg = _sigmoid(y)
    return sg * (1.0 + y * (1.0 - sg))


def _ssm_conv_fwd(zx, w, b, d_inner, conv_dim, name):
    S, wz = zx.shape
    K = w.shape[0]
    T = _pick(S, CONV_ROWS, 8)
    cw = _pick(conv_dim, CONV_LANES)

    def body(x_ref, h_ref, w_ref, b_ref, c_ref):
        has_prev = pl.program_id(0) > 0
        for cs in range(0, conv_dim, cw):
            so, sx = slice(cs, cs + cw), slice(d_inner + cs, d_inner + cs + cw)
            halo = jnp.where(has_prev, h_ref[:, sx], 0.0)
            c_ref[:, so] = _conv_apply(x_ref[:, sx], halo, w_ref, so, b_ref[:, so], K)

    main, prev, _ = _row_specs(T, S, wz)
    return pl.pallas_call(
        body, grid=(S // T,), in_specs=[main, prev, _full((K, 1, conv_dim)), _full((1, conv_dim))],
        out_specs=pl.BlockSpec((T, conv_dim), lambda i: (i, 0)),
        out_shape=jax.ShapeDtypeStruct((S, conv_dim), F32),
        compiler_params=_cp("parallel"), name=name)(zx, zx, w.reshape(K, 1, conv_dim), b.reshape(1, conv_dim))


def _ssm_conv_bwd(dxs, dbm, dcm, conv, zx, dz, ddt, w, d_inner, name):
    S, wz = zx.shape
    K, conv_dim = w.shape
    gn = dbm.shape[1]
    T = _pick(S, CONV_ROWS, 8)
    cw = _pick(math.gcd(d_inner, gn), CONV_LANES)
    nrow = S // T
    tail = wz - d_inner - conv_dim
    assert ddt.shape[1] == tail

    def body(dx_ref, dxn_ref, db_ref_, dbn_ref, dc_ref, dcn_ref, y_ref, yn_ref, x_ref, xp_ref, dz_ref, ddt_ref,
             w_ref, o_ref, dw_ref, dbias_ref):
        i = pl.program_id(0)

        @pl.when(i == 0)
        def _():
            dw_ref[...] = jnp.zeros_like(dw_ref)
            dbias_ref[...] = jnp.zeros_like(dbias_ref)

        has_prev, has_next = i > 0, i < nrow - 1
        for cs in range(0, d_inner, cw):
            o_ref[:, cs:cs + cw] = dz_ref[:, cs:cs + cw].astype(o_ref.dtype)
        o_ref[:, d_inner + conv_dim:wz] = ddt_ref[...].astype(o_ref.dtype)
        for cs in range(0, conv_dim, cw):
            so, sx = slice(cs, cs + cw), slice(d_inner + cs, d_inner + cs + cw)
            if cs < d_inner:
                src, srcn, ss = dx_ref, dxn_ref, slice(cs, cs + cw)
            elif cs < d_inner + gn:
                src, srcn, ss = db_ref_, dbn_ref, slice(cs - d_inner, cs - d_inner + cw)
            else:
                src, srcn, ss = dc_ref, dcn_ref, slice(cs - d_inner - gn, cs - d_inner - gn + cw)
            dy = src[:, ss] * _silu_grad(y_ref[:, so])
            dyn = jnp.where(has_next, srcn[:, ss] * _silu_grad(yn_ref[:, so]), 0.0)
            xp = jnp.where(has_prev, xp_ref[:, sx], 0.0)
            o_ref[:, sx] = _conv_accum(dy, dyn, x_ref[:, sx], xp, w_ref, dw_ref, dbias_ref, so, K).astype(o_ref.dtype)

    xm, _, xn = _row_specs(T, S, d_inner)
    gm, _, gnx = _row_specs(T, S, gn)
    cm, _, cn = _row_specs(T, S, conv_dim)
    zm, zp, _ = _row_specs(T, S, wz)
    tm_, _, _ = _row_specs(T, S, tail)
    dzx, dw, db = pl.pallas_call(
        body, grid=(nrow,),
        in_specs=[xm, xn, gm, gnx, gm, gnx, cm, cn, zm, zp, xm, tm_, _full((K, 1, conv_dim))],
        out_specs=[zm, _full((K, 1, conv_dim)), _full((1, conv_dim))],
        out_shape=[jax.ShapeDtypeStruct((S, wz), BF16), jax.ShapeDtypeStruct((K, 1, conv_dim), F32),
                   jax.ShapeDtypeStruct((1, conv_dim), F32)],
        compiler_params=_cp("arbitrary"), name=name)(
            dxs, dxs, dbm, dbm, dcm, dcm, conv, conv, zx, zx, dz, ddt, w.reshape(K, 1, conv_dim))
    return dzx, dw.reshape(K, conv_dim), db


def _ffn_conv_fwd(up, w, b, name):
    S, C = up.shape
    F = C // 2
    K = w.shape[0]
    T = _pick(S, CONV_ROWS, 8)
    cw = _pick(F, CONV_LANES)

    def body(x_ref, h_ref, w_ref, b_ref, a_ref):
        has_prev = pl.program_id(0) > 0
        for cs in range(0, F, cw):
            sg, su = slice(cs, cs + cw), slice(F + cs, F + cs + cw)
            gate = _conv_apply(x_ref[:, sg], jnp.where(has_prev, h_ref[:, sg], 0.0), w_ref, sg, b_ref[:, sg], K)
            upv = _conv_apply(x_ref[:, su], jnp.where(has_prev, h_ref[:, su], 0.0), w_ref, su, b_ref[:, su], K)
            a_ref[:, sg] = (gate * _sigmoid(gate) * upv).astype(a_ref.dtype)

    main, prev, _ = _row_specs(T, S, C)
    return pl.pallas_call(
        body, grid=(S // T,), in_specs=[main, prev, _full((K, 1, C)), _full((1, C))],
        out_specs=pl.BlockSpec((T, F), lambda i: (i, 0)), out_shape=jax.ShapeDtypeStruct((S, F), BF16),
        compiler_params=_cp("parallel"), name=name)(up, up, w.reshape(K, 1, C), b.reshape(1, C))


def _ffn_conv_bwd(dact, up, w, b, name):
    S, C = up.shape
    F = C // 2
    K = w.shape[0]
    T = _pick(S, CONV_ROWS, 8)
    cw = _pick(F, CONV_LANES)
    nrow = S // T

    def du(gate, upv, d):
        sg = _sigmoid(gate)
        return d * upv * sg * (1.0 + gate * (1.0 - sg)), d * gate * sg

    def body(d_ref, dn_ref, x_ref, xp_ref, xn_ref, w_ref, b_ref, dx_ref, dw_ref, db_ref):
        i = pl.program_id(0)

        @pl.when(i == 0)
        def _():
            dw_ref[...] = jnp.zeros_like(dw_ref)
            db_ref[...] = jnp.zeros_like(db_ref)

        has_prev, has_next = i > 0, i < nrow - 1
        for cs in range(0, F, cw):
            sf = slice(cs, cs + cw)
            cols = [slice(half * F + cs, half * F + cs + cw) for half in range(2)]
            xs = [x_ref[:, sc] for sc in cols]
            xps = [jnp.where(has_prev, xp_ref[:, sc], 0.0) for sc in cols]
            u = [_conv_apply(xs[q], xps[q], w_ref, cols[q], b_ref[:, cols[q]], K) for q in range(2)]
            un = [_conv_apply(xn_ref[:, cols[q]], xs[q][T - 8:T], w_ref, cols[q], b_ref[:, cols[q]], K)
                  for q in range(2)]
            dys = du(u[0], u[1], d_ref[:, sf])
            dyns = du(un[0], un[1], dn_ref[:, sf])
            for q in range(2):
                dyn = jnp.where(has_next, dyns[q], 0.0)
                dx_ref[:, cols[q]] = _conv_accum(dys[q], dyn, xs[q], xps[q], w_ref, dw_ref, db_ref, cols[q],
                                                 K).astype(dx_ref.dtype)

    am, _, an = _row_specs(T, S, F)
    xm, xp_, xn_ = _row_specs(T, S, C)
    dx, dw, db = pl.pallas_call(
        body, grid=(nrow,), in_specs=[am, an, xm, xp_, xn_, _full((K, 1, C)), _full((1, C))],
        out_specs=[xm, _full((K, 1, C)), _full((1, C))],
        out_shape=[jax.ShapeDtypeStruct((S, C), BF16), jax.ShapeDtypeStruct((K, 1, C), F32),
                   jax.ShapeDtypeStruct((1, C), F32)],
        compiler_params=_cp("arbitrary"), name=name)(dact, dact, up, up, up, w.reshape(K, 1, C), b.reshape(1, C))
    return dx, dw.reshape(K, C), db


def _cumsum_rows(v):
    n = v.shape[0]
    row = lax.broadcasted_iota(jnp.int32, v.shape, 0)
    k = 1
    while k < n:
        v = v + jnp.where(row >= k, pltpu.roll(v, k, 0), 0.0)
        k *= 2
    return v


def _rev_cumsum_rows(v):
    n = v.shape[0]
    row = lax.broadcasted_iota(jnp.int32, v.shape, 0)
    k = 1
    while k < n:
        v = v + jnp.where(row < n - k, pltpu.roll(v, n - k, 0), 0.0)
        k *= 2
    return v


def _ssd_common(x_ref, dtr_ref, bias_ref, alog_ref, gw):
    Q = SSM_CHUNK
    X = _silu(x_ref[...])
    pre = dtr_ref[...] + bias_ref[...]
    dt = _softplus(pre)
    a = -jnp.exp(alog_ref[...])
    cs = _cumsum_rows(dt * a)
    row = lax.broadcasted_iota(jnp.int32, (Q, gw), 0)
    cs_last = jnp.sum(jnp.where(row == Q - 1, cs, 0.0), axis=0, keepdims=True)
    return X, pre, dt, a, cs, cs_last, row


def _head_decay(cs, head_mask):
    Q = SSM_CHUNK
    col = jnp.max(jnp.where(head_mask, cs, NEG), axis=1, keepdims=True)
    acol = jnp.broadcast_to(col, (Q, Q))
    arow = acol.T
    ii = lax.broadcasted_iota(jnp.int32, (Q, Q), 0)
    jj = lax.broadcasted_iota(jnp.int32, (Q, Q), 1)
    tril = ii >= jj
    return jnp.where(tril, jnp.exp(jnp.where(tril, acol - arow, 0.0)), 0.0), tril


def _ssd_specs(S, d_inner, gw, nc, rev):
    Q, N, G = SSM_CHUNK, SSM_STATE, SSM_GROUPS
    ch = (lambda c: nc - 1 - c) if rev else (lambda c: c)
    x_spec = pl.BlockSpec((Q, gw), lambda g, c: (ch(c), g))
    b_spec = pl.BlockSpec((Q, N), lambda g, c: (ch(c), d_inner // N + g))
    c_spec = pl.BlockSpec((Q, N), lambda g, c: (ch(c), d_inner // N + G + g))
    p_spec = pl.BlockSpec((None, 1, gw), lambda g, c: (g, 0, 0))
    s_spec = pl.BlockSpec((None, None, gw, N), lambda g, c: (ch(c), g, 0, 0))
    return x_spec, b_spec, c_spec, p_spec, s_spec


def _ssd_fwd(xbc, dtr, bias, alog, dsk, d_inner, name):
    S = xbc.shape[0]
    Q, N, G, P = SSM_CHUNK, SSM_STATE, SSM_GROUPS, SSM_HEAD_DIM
    gw = d_inner // G
    R = gw // P
    nc = S // Q

    def body(x_ref, b_ref, c_ref, dtr_ref, bias_ref, alog_ref, d_ref, y_ref, sp_ref, s_scr):
        @pl.when(pl.program_id(1) == 0)
        def _():
            s_scr[...] = jnp.zeros_like(s_scr)

        X, _, dt, a, cs, cs_last, row = _ssd_common(x_ref, dtr_ref, bias_ref, alog_ref, gw)
        Bm, Cm = _silu(b_ref[...]), _silu(c_ref[...])
        xdt = X * dt
        lane = lax.broadcasted_iota(jnp.int32, (Q, gw), 1)
        sprev = s_scr[...]
        sp_ref[...] = sprev
        cb = _dot_nt(Cm, Bm)
        y = jnp.exp(cs) * _dot_nt(Cm, sprev)
        for r in range(R):
            hm = (lane >= r * P) & (lane < (r + 1) * P)
            dec_l, _ = _head_decay(cs, hm)
            y = y + _dot_nn(cb * dec_l, jnp.where(hm, xdt, 0.0))
        dec = jnp.exp(cs_last - cs)
        cd = jnp.exp(jnp.broadcast_to(cs_last, (Q, gw)).T)
        s_scr[...] = sprev * cd + _dot_tn(xdt * dec, Bm)
        y_ref[...] = y + d_ref[...] * X

    x_spec, b_spec, c_spec, p_spec, s_spec = _ssd_specs(S, d_inner, gw, nc, False)
    return pl.pallas_call(
        body, grid=(G, nc), in_specs=[x_spec, b_spec, c_spec, x_spec, p_spec, p_spec, p_spec],
        out_specs=[x_spec, s_spec],
        out_shape=[jax.ShapeDtypeStruct((S, d_inner), F32), jax.ShapeDtypeStruct((nc, G, gw, N), F32)],
        scratch_shapes=[pltpu.VMEM((gw, N), F32)],
        compiler_params=_cp("parallel", "arbitrary"), name=name)(xbc, xbc, xbc, dtr, bias, alog, dsk)


def _ssd_bwd(xbc, dtr, bias, alog, dsk, sprev_all, dy, d_inner, name):
    S = xbc.shape[0]
    Q, N, G, P = SSM_CHUNK, SSM_STATE, SSM_GROUPS, SSM_HEAD_DIM
    gw = d_inner // G
    R = gw // P
    nc = S // Q

    def body(x_ref, b_ref, c_ref, dtr_ref, bias_ref, alog_ref, d_ref, sp_ref, dy_ref,
             dx_ref, db_ref, dc_ref, ddt_ref, dbias_ref, dalog_ref, dd_ref, ds_scr):
        @pl.when(pl.program_id(1) == 0)
        def _():
            ds_scr[...] = jnp.zeros_like(ds_scr)
            dbias_ref[...] = jnp.zeros_like(dbias_ref)
            dalog_ref[...] = jnp.zeros_like(dalog_ref)
            dd_ref[...] = jnp.zeros_like(dd_ref)

        X, pre, dt, a, cs, cs_last, row = _ssd_common(x_ref, dtr_ref, bias_ref, alog_ref, gw)
        Bm, Cm = _silu(b_ref[...]), _silu(c_ref[...])
        dY = dy_ref[...]
        sprev = sp_ref[...]
        dsn = ds_scr[...]
        xdt = X * dt
        lane = lax.broadcasted_iota(jnp.int32, (Q, gw), 1)
        lane1 = lax.broadcasted_iota(jnp.int32, (1, gw), 1)
        srow = lax.broadcasted_iota(jnp.int32, (gw, N), 0)
        ecs = jnp.exp(cs)
        dec = jnp.exp(cs_last - cs)
        cd = jnp.exp(jnp.broadcast_to(cs_last, (Q, gw)).T)
        dd_ref[...] += jnp.sum(dY * X, axis=0, keepdims=True)
        dX = d_ref[...] * dY
        ey = ecs * dY
        dcs = ey * _dot_nt(Cm, sprev)
        dC = _dot_nn(ey, sprev)
        ds_scr[...] = cd * dsn + _dot_tn(ey, Cm)
        wmat = _dot_nt(Bm, dsn)
        dxdt = dec * wmat
        xd = xdt * dec
        dB = _dot_nn(xd, dsn)
        ddec = xdt * wmat * dec
        dcs = dcs - ddec
        dlast = jnp.sum(ddec, axis=0, keepdims=True)
        qmat = dsn * sprev * cd
        cb = _dot_nt(Cm, Bm)
        dcb = jnp.zeros((Q, Q), F32)
        dcs_rep = jnp.zeros((Q, gw), F32)
        dtx_rep = jnp.zeros((Q, gw), F32)
        for r in range(R):
            hm = (lane >= r * P) & (lane < (r + 1) * P)
            dec_l, tril = _head_decay(cs, hm)
            dyr = jnp.where(hm, dY, 0.0)
            gmat = jnp.where(tril, _dot_nt(dyr, xdt), 0.0)
            dcb = dcb + gmat * dec_l
            e = gmat * cb * dec_l
            v = (jnp.sum(e, axis=1, keepdims=True) - jnp.sum(e.T, axis=1, keepdims=True)
                 + jnp.sum(jnp.where(hm, dcs, 0.0), axis=1, keepdims=True))
            dxdt = dxdt + _dot_tn(cb * dec_l, dyr)
            hm1 = (lane1 >= r * P) & (lane1 < (r + 1) * P)
            t_last = (jnp.sum(jnp.where(hm1, dlast, 0.0), axis=1, keepdims=True)
                      + jnp.sum(jnp.where((srow >= r * P) & (srow < (r + 1) * P), qmat, 0.0), keepdims=True))
            dcs_rep = dcs_rep + jnp.where(hm, v, 0.0) + jnp.where(hm & (row == Q - 1), t_last, 0.0)
        for r in range(R):
            hm = (lane >= r * P) & (lane < (r + 1) * P)
            w_r = jnp.sum(jnp.where(hm, dxdt * X, 0.0), axis=1, keepdims=True)
            dtx_rep = dtx_rep + jnp.where(hm, w_r, 0.0)
        dadt = _rev_cumsum_rows(dcs_rep)
        ddt = a * dadt + dtx_rep
        dalog_ref[...] += jnp.sum(dt * dadt, axis=0, keepdims=True) * a
        draw = ddt * _sigmoid(pre)
        ddt_ref[...] = draw
        dbias_ref[...] += jnp.sum(draw, axis=0, keepdims=True)
        dx_ref[...] = dX + dxdt * dt
        db_ref[...] = dB + _dot_tn(dcb, Cm)
        dc_ref[...] = dC + _dot_nn(dcb, Bm)

    x_spec, b_spec, c_spec, p_spec, s_spec = _ssd_specs(S, d_inner, gw, nc, True)
    n_spec = pl.BlockSpec((Q, N), lambda g, c: (nc - 1 - c, g))
    gshape = jax.ShapeDtypeStruct((G, 1, gw), F32)
    return pl.pallas_call(
        body, grid=(G, nc),
        in_specs=[x_spec, b_spec, c_spec, x_spec, p_spec, p_spec, p_spec, s_spec, x_spec],
        out_specs=[x_spec, n_spec, n_spec, x_spec, p_spec, p_spec, p_spec],
        out_shape=[jax.ShapeDtypeStruct((S, d_inner), F32), jax.ShapeDtypeStruct((S, G * N), F32),
                   jax.ShapeDtypeStruct((S, G * N), F32), jax.ShapeDtypeStruct((S, d_inner), F32),
                   gshape, gshape, gshape],
        scratch_shapes=[pltpu.VMEM((gw, N), F32)],
        compiler_params=_cp("parallel", "arbitrary"), name=name)(
            xbc, xbc, xbc, dtr, bias, alog, dsk, sprev_all, dy)


def _gnorm_fwd(y, zx, w, name):
    S, d_inner = y.shape
    G = SSM_GROUPS
    gw = d_inner // G
    T = _pick(S, 256, 8)

    def body(y_ref, z_ref, w_ref, o_ref):
        for k in range(G):
            sl = slice(k * gw, (k + 1) * gw)
            z = z_ref[:, sl]
            gk = y_ref[:, sl] * z * _sigmoid(z)
            r = lax.rsqrt(jnp.mean(gk * gk, axis=-1, keepdims=True) + NORM_EPS)
            o_ref[:, sl] = (gk * r * w_ref[:, sl]).astype(o_ref.dtype)

    row = pl.BlockSpec((T, d_inner), lambda i: (i, 0))
    vec = pl.BlockSpec((1, d_inner), lambda i: (0, 0))
    return pl.pallas_call(body, grid=(S // T,), in_specs=[row, row, vec], out_specs=row,
                          out_shape=jax.ShapeDtypeStruct((S, d_inner), BF16),
                          compiler_params=_cp("parallel"), name=name)(y, zx, w.reshape(1, d_inner))


def _gnorm_bwd(y, zx, w, dout, name):
    S, d_inner = y.shape
    G = SSM_GROUPS
    gw = d_inner // G
    T = _pick(S, 256, 8)

    def body(y_ref, z_ref, w_ref, d_ref, dy_ref, dz_ref, dw_ref):
        @pl.when(pl.program_id(0) == 0)
        def _():
            dw_ref[...] = jnp.zeros_like(dw_ref)

        for k in range(G):
            sl = slice(k * gw, (k + 1) * gw)
            z, yv, d = z_ref[:, sl], y_ref[:, sl], d_ref[:, sl]
            sg = _sigmoid(z)
            sz = z * sg
            gk = yv * sz
            r = lax.rsqrt(jnp.mean(gk * gk, axis=-1, keepdims=True) + NORM_EPS)
            gh = gk * r
            dw_ref[:, sl] += jnp.sum(d * gh, axis=0, keepdims=True)
            dg = d * w_ref[:, sl]
            dgk = r * (dg - gh * jnp.mean(dg * gh, axis=-1, keepdims=True))
            dy_ref[:, sl] = dgk * sz
            dz_ref[:, sl] = dgk * yv * sg * (1.0 + z * (1.0 - sg))

    row = pl.BlockSpec((T, d_inner), lambda i: (i, 0))
    vec = pl.BlockSpec((1, d_inner), lambda i: (0, 0))
    return pl.pallas_call(
        body, grid=(S // T,), in_specs=[row, row, vec, row], out_specs=[row, row, vec],
        out_shape=[jax.ShapeDtypeStruct((S, d_inner), F32)] * 2 + [jax.ShapeDtypeStruct((1, d_inner), F32)],
        compiler_params=_cp("arbitrary"), name=name)(y, zx, w.reshape(1, d_inner), dout)


def _adam_math(g, w, m, v):
    m = ADAM_B1 * m + (1.0 - ADAM_B1) * g
    v = ADAM_B2 * v + (1.0 - ADAM_B2) * (g * g)
    m_hat = m / (1.0 - ADAM_B1 ** ADAM_STEP)
    v_hat = v / (1.0 - ADAM_B2 ** ADAM_STEP)
    delta = -ADAM_LR * (m_hat / (jnp.sqrt(v_hat) + ADAM_EPS) + ADAM_WD * w)
    return delta, m, v


def _adamw_big(own, sib, w, m, v, layer, prev, name):
    L, A, Bc = w.shape
    T = _pick(A, max(8, (1 << 19) // (4 * Bc)), 16)

    def body(o_ref, s_ref, w_ref, m_ref, v_ref, *rest):
        g_ref, d_ref, nm_ref, nv_ref = rest[-4:]
        so = o_ref[0].astype(F32)
        ss = s_ref[0].astype(F32)
        for k in range(1, N_CHIPS):
            so = so + o_ref[k].astype(F32)
            ss = ss + s_ref[k].astype(F32)
        g = so + ss
        delta, nm, nv = _adam_math(g, w_ref[...], m_ref[...], v_ref[...])
        g_ref[...] = g
        d_ref[...] = delta
        nm_ref[...] = nm
        nv_ref[...] = nv

    part = pl.BlockSpec((N_CHIPS, T, Bc), lambda i: (0, i, 0))
    blk = pl.BlockSpec((None, T, Bc), lambda i: (layer, i, 0))
    shp = jax.ShapeDtypeStruct(w.shape, F32)
    in_specs, args, kwargs = [part, part, blk, blk, blk], [own, sib, w, m, v], {}
    if prev is not None:
        in_specs += [pl.BlockSpec(memory_space=pl.ANY)] * 4
        args += list(prev)
        kwargs["input_output_aliases"] = {5 + q: q for q in range(4)}
    return pl.pallas_call(body, grid=(A // T,), in_specs=in_specs, out_specs=[blk] * 4, out_shape=[shp] * 4,
                          compiler_params=_cp("parallel"), name=name, **kwargs)(*args)


def _sum_devices(parts, name):
    _, R, C = parts.shape

    def body(p_ref, o_ref):
        acc = p_ref[0]
        for k in range(1, N_DEV):
            acc = acc + p_ref[k]
        o_ref[...] = acc

    return pl.pallas_call(body, out_shape=jax.ShapeDtypeStruct((R, C), F32), name=name)(parts)


def _adamw_small(g, w, m, v, name):
    def body(g_ref, w_ref, m_ref, v_ref, d_ref, nm_ref, nv_ref):
        delta, nm, nv = _adam_math(g_ref[...], w_ref[...], m_ref[...], v_ref[...])
        d_ref[...] = delta
        nm_ref[...] = nm
        nv_ref[...] = nv

    shp = jax.ShapeDtypeStruct(g.shape, F32)
    return pl.pallas_call(body, out_shape=[shp] * 3, name=name)(g, w, m, v)


PACK_COLS = 1024


def _pack(arrs):
    flat = jnp.concatenate([a.reshape(-1).astype(F32) for a in arrs])
    n = flat.shape[0]
    rows = -(-n // (8 * PACK_COLS)) * 8
    return jnp.pad(flat, (0, rows * PACK_COLS - n)).reshape(rows, PACK_COLS)


def _unpack(packed, shapes):
    flat = packed.reshape(-1)
    out, off = [], 0
    for s in shapes:
        n = math.prod(s)
        out.append(flat[off:off + n].reshape(s))
        off += n
    return out


def _shard_ref(ref, kind, k, n):
    if kind == "col":
        return ref.at[:, pl.ds(pl.multiple_of(k * n, 128), n)]
    if kind == "row":
        return ref.at[pl.ds(pl.multiple_of(k * n, 16), n), :]
    return ref.at[k]


def _chip_peers():
    x, y, c = lax.axis_index("x"), lax.axis_index("y"), lax.axis_index("c")
    return x, y, c, [(1 - x, y), (x, 1 - y), (1 - x, 1 - y)]


class _Exchange:
    def __init__(self, mode, items):
        self.mode, self.items = mode, items
        self.arrays = []
        for it in items:
            if not any(it[0] is a for a in self.arrays):
                self.arrays.append(it[0])
        self.src_idx = [next(i for i, a in enumerate(self.arrays) if a is it[0]) for it in items]
        self.out_shapes = [jax.ShapeDtypeStruct(it[-1], it[0].dtype) for it in items]
        n = len(items)
        if mode == "swap":
            self.scratch = [pltpu.SemaphoreType.DMA((n,)), pltpu.SemaphoreType.DMA((n,))]
        else:
            self.scratch = [pltpu.SemaphoreType.DMA((3 * n,)), pltpu.SemaphoreType.DMA((3 * n,)),
                            pltpu.SemaphoreType.DMA((n,))]

    def _copies(self, ins, outs, sems):
        if self.mode == "swap":
            send_sems, recv_sems = sems
            x, y, c = lax.axis_index("x"), lax.axis_index("y"), lax.axis_index("c")
            sent = [pltpu.make_async_remote_copy(
                src_ref=ins[self.src_idx[t]], dst_ref=outs[t], send_sem=send_sems.at[t], recv_sem=recv_sems.at[t],
                device_id=(x, y, 1 - c), device_id_type=MESH) for t in range(len(self.items))]
            return [], sent, sent
        send_sems, recv_sems, loc_sems = sems
        x, y, c, peers = _chip_peers()
        me = 2 * x + y
        local, sent, arriving = [], [], []
        for t, it in enumerate(self.items):
            src_arr = ins[self.src_idx[t]]
            if self.mode == "gather":
                _, layer, kind, n, _ = it
                src = src_arr if layer is None else src_arr.at[layer]
                src_for = lambda k: src
                dst_from = lambda k: _shard_ref(outs[t], kind, k, n)
            else:
                _, kind, n, _ = it
                src_for = lambda k: _shard_ref(src_arr, kind, k, n)
                dst_from = lambda k: outs[t].at[k]
            local.append(pltpu.make_async_copy(src_for(me), dst_from(me), loc_sems.at[t]))
            for j, (px, py) in enumerate(peers):
                pk = 2 * px + py
                args = dict(send_sem=send_sems.at[3 * t + j], recv_sem=recv_sems.at[3 * t + j],
                            device_id=(px, py, c), device_id_type=MESH)
                sent.append(pltpu.make_async_remote_copy(src_ref=src_for(pk), dst_ref=dst_from(me), **args))
                arriving.append(pltpu.make_async_remote_copy(src_ref=src_for(pk), dst_ref=dst_from(pk), **args))
        return local, sent, arriving

    def start(self, ins, outs, sems):
        local, sent, arriving = self._copies(ins, outs, sems)
        for cp in local + sent:
            cp.start()
        for cp in arriving:
            cp._used = True

    def wait(self, ins, outs, sems):
        local, sent, arriving = self._copies(ins, outs, sems)
        for cp in arriving:
            cp.wait_recv()
        for cp in sent:
            cp.wait_send()
        for cp in local:
            cp.wait()


class _Multi:
    def __init__(self, parts):
        self.parts = parts
        self.arrays = [a for p in parts for a in p.arrays]
        self.out_shapes = [s for p in parts for s in p.out_shapes]
        self.scratch = [s for p in parts for s in p.scratch]

    def _split(self, ins, outs, sems):
        i = o = s = 0
        for p in self.parts:
            ni, no, ns = len(p.arrays), len(p.out_shapes), len(p.scratch)
            yield p, ins[i:i + ni], outs[o:o + no], sems[s:s + ns]
            i, o, s = i + ni, o + no, s + ns

    def start(self, ins, outs, sems):
        for p, a, b, c in self._split(ins, outs, sems):
            p.start(a, b, c)

    def wait(self, ins, outs, sems):
        for p, a, b, c in self._split(ins, outs, sems):
            p.wait(a, b, c)


def _run_exchange(ex, name):
    nin, nout = len(ex.arrays), len(ex.out_shapes)

    def body(*refs):
        ins, outs, sems = refs[:nin], refs[nin:nin + nout], refs[nin + nout:]
        ex.start(ins, outs, sems)
        ex.wait(ins, outs, sems)

    anyspec = pl.BlockSpec(memory_space=pl.ANY)
    return pl.pallas_call(body, in_specs=[anyspec] * nin, out_specs=[anyspec] * nout, out_shape=ex.out_shapes,
                          scratch_shapes=ex.scratch, name=name)(*ex.arrays)


def _start_exchange(ex, name):
    nin, nout, nsem = len(ex.arrays), len(ex.out_shapes), len(ex.scratch)
    hbm = pl.BlockSpec(memory_space=pltpu.HBM)
    sem = pl.BlockSpec(memory_space=pltpu.SEMAPHORE)

    def body(*refs):
        ins, lands = refs[:nin], refs[nin:nin + nout]
        sems = refs[nin + nout:nin + nout + nsem]
        ex.start(ins, lands, sems)
        refs[-1][...] = jnp.zeros_like(refs[-1])

    args = [pltpu.with_memory_space_constraint(a, pltpu.HBM) for a in ex.arrays]
    args += [pltpu.with_memory_space_constraint(lax.empty(s.shape, s.dtype), pltpu.HBM) for s in ex.out_shapes]
    thru = [pltpu.HBM(a.shape, a.dtype) for a in ex.arrays] + [pltpu.HBM(s.shape, s.dtype) for s in ex.out_shapes]
    return pl.pallas_call(
        body, name=name, in_specs=[hbm] * (nin + nout),
        out_shape=tuple(ex.scratch) + tuple(thru) + (jax.ShapeDtypeStruct((8, 128), F32),),
        out_specs=tuple([sem] * nsem + [hbm] * (nin + nout) + [pl.BlockSpec(memory_space=pltpu.VMEM)]),
        input_output_aliases={q: nsem + q for q in range(nin + nout)},
        compiler_params=pltpu.CompilerParams(has_side_effects=pltpu.SideEffectType.DATAFLOW_SIDE_EFFECTING))(*args)


def _finish_exchange(ex, handles, after, name):
    nin, nout, nsem = len(ex.arrays), len(ex.out_shapes), len(ex.scratch)
    hbm = pl.BlockSpec(memory_space=pltpu.HBM)
    sem = pl.BlockSpec(memory_space=pltpu.SEMAPHORE)
    sems, thru = handles[:nsem], handles[nsem:nsem + nin + nout]

    def body(*refs):
        ins, lands = refs[:nin], refs[nin:nin + nout]
        ex.wait(ins, lands, refs[nin + nout:nin + nout + nsem])

    outs = pl.pallas_call(
        body, name=name, in_specs=[hbm] * (nin + nout) + [sem] * nsem + [pl.BlockSpec(memory_space=pl.ANY)],
        out_shape=tuple(pltpu.HBM(t.shape, t.dtype) for t in thru), out_specs=tuple([hbm] * (nin + nout)),
        input_output_aliases={q: q for q in range(nin + nout)},
        compiler_params=pltpu.CompilerParams(has_side_effects=pltpu.SideEffectType.DATAFLOW_SIDE_EFFECTING))(
            *thru, *sems, after)
    return outs[nin:]


def _all_gather_devices(v, name):
    def body(v_ref, o_ref, send_sems, recv_sems, loc_sem):
        x, y, c = lax.axis_index("x"), lax.axis_index("y"), lax.axis_index("c")
        me = 4 * x + 2 * y + c
        lc = pltpu.make_async_copy(v_ref, o_ref.at[me], loc_sem)
        lc.start()
        rel = [(bx, by, bc) for bx in (0, 1) for by in (0, 1) for bc in (0, 1)][1:]
        copies = []
        for j, (bx, by, bc) in enumerate(rel):
            px, py, pc = x ^ bx, y ^ by, c ^ bc
            copies.append((pltpu.make_async_remote_copy(
                src_ref=v_ref, dst_ref=o_ref.at[me], send_sem=send_sems.at[j], recv_sem=recv_sems.at[j],
                device_id=(px, py, pc), device_id_type=MESH), 4 * px + 2 * py + pc))
        for cp, _ in copies:
            cp.start()
        for j, (cp, pid) in enumerate(copies):
            pltpu.make_async_remote_copy(
                src_ref=v_ref, dst_ref=o_ref.at[pid], send_sem=send_sems.at[j], recv_sem=recv_sems.at[j],
                device_id=(x, y, c), device_id_type=MESH).wait_recv()
        for cp, _ in copies:
            cp.wait_send()
        lc.wait()

    anyspec = pl.BlockSpec(memory_space=pl.ANY)
    return pl.pallas_call(
        body, in_specs=[anyspec], out_specs=anyspec,
        out_shape=jax.ShapeDtypeStruct((N_DEV,) + v.shape, v.dtype),
        scratch_shapes=[pltpu.SemaphoreType.DMA((N_DEV - 1,)), pltpu.SemaphoreType.DMA((N_DEV - 1,)),
                        pltpu.SemaphoreType.DMA(())],
        name=name)(v)


BIG = ("attn_w_qkv", "attn_w_o", "ssm_w_in", "ssm_w_out", "ffn_w_up", "ffn_w_down")
BIG_KIND = {"attn_w_qkv": "col", "attn_w_o": "row", "ssm_w_in": "lead", "ssm_w_out": "row",
            "ffn_w_up": "col", "ffn_w_down": "row"}
SMALL_SHARDED = {"ssm_conv_w": 2, "ssm_conv_b": 1, "ssm_norm_w": 1, "ffn_conv_w": 2}
SMALL = ("mix_norm_w", "ssm_conv_w", "ssm_conv_b", "ssm_dt_bias", "ssm_a_log", "ssm_d", "ssm_norm_w",
         "ffn_norm_w", "ffn_conv_w", "ffn_conv_b", "final_norm_w")
WEIGHTS = ("mix_norm_w", "attn_w_qkv", "attn_w_o", "ssm_w_in", "ssm_conv_w", "ssm_conv_b", "ssm_dt_bias",
           "ssm_a_log", "ssm_d", "ssm_norm_w", "ssm_w_out", "ffn_norm_w", "ffn_w_up", "ffn_conv_w",
           "ffn_conv_b", "ffn_w_down", "final_norm_w")


def _shard_extent(name, shape):
    _, a, b = shape
    return {"col": b, "row": a, "lead": 1}[BIG_KIND[name]]


def _gather_item(w16, name):
    a, b = w16.shape
    kind = BIG_KIND[name]
    full = {"col": (a, N_CHIPS * b), "row": (N_CHIPS * a, b), "lead": (N_CHIPS, a, b)}[kind]
    return (w16, None, kind, _shard_extent(name, (1, a, b)), full)


def _layer_weights(i):
    j = i // 2
    mixer = [("attn_w_qkv", j), ("attn_w_o", j)] if i % 2 == 0 else [("ssm_w_in", j), ("ssm_w_out", j)]
    return mixer + [("ffn_w_up", i), ("ffn_w_down", i)]


def kernel(x, mix_norm_w, attn_w_qkv, attn_w_o, ssm_w_in, ssm_conv_w, ssm_conv_b, ssm_dt_bias, ssm_a_log, ssm_d, ssm_norm_w, ssm_w_out, ffn_norm_w, ffn_w_up, ffn_conv_w, ffn_conv_b, ffn_w_down, final_norm_w, loss_target, m_mix_norm_w, m_attn_w_qkv, m_attn_w_o, m_ssm_w_in, m_ssm_conv_w, m_ssm_conv_b, m_ssm_dt_bias, m_ssm_a_log, m_ssm_d, m_ssm_norm_w, m_ssm_w_out, m_ffn_norm_w, m_ffn_w_up, m_ffn_conv_w, m_ffn_conv_b, m_ffn_w_down, m_final_norm_w, v_mix_norm_w, v_attn_w_qkv, v_attn_w_o, v_ssm_w_in, v_ssm_conv_w, v_ssm_conv_b, v_ssm_dt_bias, v_ssm_a_log, v_ssm_d, v_ssm_norm_w, v_ssm_w_out, v_ffn_norm_w, v_ffn_w_up, v_ffn_conv_w, v_ffn_conv_b, v_ffn_w_down, v_final_norm_w):
    W = dict(mix_norm_w=mix_norm_w, attn_w_qkv=attn_w_qkv, attn_w_o=attn_w_o, ssm_w_in=ssm_w_in,
             ssm_conv_w=ssm_conv_w, ssm_conv_b=ssm_conv_b, ssm_dt_bias=ssm_dt_bias, ssm_a_log=ssm_a_log,
             ssm_d=ssm_d, ssm_norm_w=ssm_norm_w, ssm_w_out=ssm_w_out, ffn_norm_w=ffn_norm_w, ffn_w_up=ffn_w_up,
             ffn_conv_w=ffn_conv_w, ffn_conv_b=ffn_conv_b, ffn_w_down=ffn_w_down, final_norm_w=final_norm_w)
    M = dict(mix_norm_w=m_mix_norm_w, attn_w_qkv=m_attn_w_qkv, attn_w_o=m_attn_w_o, ssm_w_in=m_ssm_w_in,
             ssm_conv_w=m_ssm_conv_w, ssm_conv_b=m_ssm_conv_b, ssm_dt_bias=m_ssm_dt_bias, ssm_a_log=m_ssm_a_log,
             ssm_d=m_ssm_d, ssm_norm_w=m_ssm_norm_w, ssm_w_out=m_ssm_w_out, ffn_norm_w=m_ffn_norm_w,
             ffn_w_up=m_ffn_w_up, ffn_conv_w=m_ffn_conv_w, ffn_conv_b=m_ffn_conv_b, ffn_w_down=m_ffn_w_down,
             final_norm_w=m_final_norm_w)
    V = dict(mix_norm_w=v_mix_norm_w, attn_w_qkv=v_attn_w_qkv, attn_w_o=v_attn_w_o, ssm_w_in=v_ssm_w_in,
             ssm_conv_w=v_ssm_conv_w, ssm_conv_b=v_ssm_conv_b, ssm_dt_bias=v_ssm_dt_bias, ssm_a_log=v_ssm_a_log,
             ssm_d=v_ssm_d, ssm_norm_w=v_ssm_norm_w, ssm_w_out=v_ssm_w_out, ffn_norm_w=v_ffn_norm_w,
             ffn_w_up=v_ffn_w_up, ffn_conv_w=v_ffn_conv_w, ffn_conv_b=v_ffn_conv_b, ffn_w_down=v_ffn_w_down,
             final_norm_w=v_final_norm_w)

    S, D = x.shape[1], x.shape[2]
    xs = x.reshape(S, D)
    tgt = loss_target.reshape(S, D)
    depth = mix_norm_w.shape[0]
    heads = attn_w_o.shape[1] * N_CHIPS // HEAD_DIM
    AW = heads * HEAD_DIM
    d_inner = ssm_w_out.shape[1] * N_CHIPS
    ssm_heads = d_inner // SSM_HEAD_DIM
    G, P, N = SSM_GROUPS, SSM_HEAD_DIM, SSM_STATE
    gw = d_inner // G
    conv_dim = d_inner + 2 * G * N
    in_w = d_inner + conv_dim + ssm_heads
    in_pad = -(-in_w // 128) * 128
    shard_in = ssm_w_in.shape[2]
    xi, yi = lax.axis_index("x"), lax.axis_index("y")
    chip = 2 * xi + yi

    full = {}

    def land(keys, outs):
        for (n, l), o in zip(keys, outs):
            if n == "ssm_w_in":
                o = jnp.pad(jnp.concatenate([o[k] for k in range(N_CHIPS)], axis=1), ((0, 0), (0, in_pad - in_w)))
            full[(n, l)] = o

    def gather_ex(keys, extra=()):
        return _Exchange("gather", [_gather_item(W[n][l].astype(BF16), n) for n, l in keys] + list(extra))

    def fwd_mm(a, b, name, resid=None):
        return _matmul(a, b, "nn", F32, name, resid=resid)

    sm_names = list(SMALL_SHARDED)
    packed = _pack([W[n] for n in sm_names])
    lw = [_layer_weights(i) for i in range(depth)]
    first = lw[0][:1]
    got = _run_exchange(gather_ex(first, [(packed, None, "lead", 1, (N_CHIPS,) + packed.shape)]), "gather_first")
    land(first, got[:-1])
    batches = [lw[0][1:]] + lw[1:]
    pending = []
    for q, keys in enumerate(batches):
        ex = gather_ex(keys)
        pending.append((keys, ex, _start_exchange(ex, f"gather_start_{q}")))
    issued = sum(handles[-1][0, 0] for _, _, handles in pending)

    def arrive(q, after):
        keys, ex, handles = pending[q]
        land(keys, _finish_exchange(ex, handles, after, f"gather_wait_{q}"))
    per_chip = [_unpack(got[-1][k], [W[n].shape for n in sm_names]) for k in range(N_CHIPS)]
    for q, n in enumerate(sm_names):
        full[n] = jnp.concatenate([per_chip[k][q] for k in range(N_CHIPS)], axis=SMALL_SHARDED[n])
    tab = _perm_tokens(_rope_table(S))

    def rep_heads(p):
        return jnp.repeat(p, P).reshape(G, 1, gw)

    saved = []
    cur = xs
    for i in range(depth):
        j = i // 2
        if i > 0:
            arrive(i, cur)
        sv = {"x_in": cur}
        h = _rms_fwd(cur, mix_norm_w[i] + issued if i == 0 else mix_norm_w[i], f"mix_norm_fwd_{i}")
        sv["h"] = h
        if i % 2 == 0:
            h = _perm_tokens(h)
            sv["h"] = h
            qkv = fwd_mm(h, full[("attn_w_qkv", j)], f"qkv_fwd_{i}")
            if i == 0:
                arrive(0, qkv)
            og = [_attn_fwd(qkv, tab, g, heads, f"attn_fwd_{i}_{g}") for g in range(3)]
            o, lse = _attn_combine([a for a, _ in og], [b for _, b in og], f"attn_combine_{i}")
            mixed = _unperm_tokens(fwd_mm(o, full[("attn_w_o", j)], f"attn_out_fwd_{i}"))
            sv.update(qkv=qkv, o=o, lse=lse)
        else:
            zx = fwd_mm(h, full[("ssm_w_in", j)], f"ssm_in_fwd_{i}")
            conv = _ssm_conv_fwd(zx, full["ssm_conv_w"][j], full["ssm_conv_b"][j], d_inner, conv_dim,
                                 f"ssm_conv_fwd_{i}")
            dtr = jnp.repeat(zx[:, d_inner + conv_dim:in_w], P, axis=1)
            prm = [rep_heads(p[j]) for p in (ssm_dt_bias, ssm_a_log, ssm_d)]
            y, sprev = _ssd_fwd(conv, dtr, *prm, d_inner, f"ssd_fwd_{i}")
            gated = _gnorm_fwd(y, zx, full["ssm_norm_w"][j], f"ssm_norm_fwd_{i}")
            cur = fwd_mm(gated, full[("ssm_w_out", j)], f"ssm_out_fwd_{i}", resid=cur)
            sv.update(zx=zx, conv=conv, dtr=dtr, prm=prm, y=y, sprev=sprev, gated=gated)
            mixed = None
        if mixed is None:
            h2 = _rms_fwd(cur, ffn_norm_w[i], f"ffn_norm_fwd_{i}")
        else:
            cur, h2 = _rms_fwd(cur, ffn_norm_w[i], f"ffn_norm_fwd_{i}", add=mixed)
        sv["x_mid"] = cur
        up = fwd_mm(h2, full[("ffn_w_up", i)], f"ffn_up_fwd_{i}")
        act = _ffn_conv_fwd(up, full["ffn_conv_w"][i], ffn_conv_b[i], f"ffn_conv_fwd_{i}")
        cur = fwd_mm(act, full[("ffn_w_down", i)], f"ffn_down_fwd_{i}", resid=cur)
        sv.update(h2=h2, up=up, act=act)
        saved.append(sv)

    dx, d_final, loss_part = _loss_head(cur, final_norm_w, tgt, "loss_head")
    gbig, recv = {}, {}
    gs = {n: [None] * W[n].shape[0] for n in SMALL if n != "final_norm_w"}

    def scatter_ex(keys):
        return _Exchange("scatter", [(gbig[(n, l)], BIG_KIND[n], _shard_extent(n, W[n].shape),
                                      (N_CHIPS,) + W[n].shape[1:]) for n, l in keys])

    sib = {}

    def swap_ex(keys):
        return _Exchange("swap", [(recv[k], recv[k].shape) for k in keys])

    def bwd_mm(a, b, mode, dtype, name, send=(), swap=()):
        if not send and not swap:
            return _matmul(a, b, mode, dtype, name)
        parts = ([scatter_ex(send)] if send else []) + ([swap_ex(swap)] if swap else [])
        out, got = _matmul(a, b, mode, dtype, name, carry=_Multi(parts))
        recv.update(zip(send, got[:len(send)]))
        sib.update(zip(swap, got[len(send):]))
        return out

    sending = None
    to_swap = []
    for i in reversed(range(depth)):
        j = i // 2
        sv = saved[i]
        k_in, k_out, k_up, k_down = _layer_weights(i)
        own = i == 0

        def now(keys):
            return keys if own else []
        dact = _matmul(dx, full[k_down], "nt", F32, f"ffn_down_dgrad_{i}")
        gbig[k_down] = bwd_mm(sv["act"], dx, "tn", BF16, f"ffn_down_wgrad_{i}", swap=to_swap)
        to_swap = []
        dup, dcw, dcb = _ffn_conv_bwd(dact, sv["up"], full["ffn_conv_w"][i], ffn_conv_b[i], f"ffn_conv_bwd_{i}")
        gs["ffn_conv_w"][i], gs["ffn_conv_b"][i] = dcw, dcb[0]
        dh2 = bwd_mm(dup, full[k_up], "nt", F32, f"ffn_up_dgrad_{i}", send=now([k_down]))
        gbig[k_up] = bwd_mm(sv["h2"], dup, "tn", BF16, f"ffn_up_wgrad_{i}", swap=now([k_down]))
        dx, dnw = _rms_bwd(sv["x_mid"], ffn_norm_w[i], dh2, dx, f"ffn_norm_bwd_{i}")
        gs["ffn_norm_w"][i] = dnw[0]
        if i % 2 == 0:
            dxp = _perm_tokens(dx)
            do = _matmul(dxp, full[k_out], "nt", F32, f"attn_out_dgrad_{i}")
            gbig[k_out] = _matmul(sv["o"], dxp, "tn", BF16, f"attn_out_wgrad_{i}")
            dqkv = None
            for g in range(3):
                dqkv = _attn_bwd(sv["qkv"], tab, sv["o"], sv["lse"], do, dqkv, g, heads, f"attn_bwd_{i}_{g}")
            gbig[k_in] = bwd_mm(sv["h"], dqkv, "tn", BF16, f"qkv_wgrad_{i}", send=now([k_up, k_out]))
            dh = _unperm_tokens(bwd_mm(dqkv, full[k_in], "nt", F32, f"qkv_dgrad_{i}", send=now([k_in]),
                                       swap=now([k_up, k_out])))
        else:
            dgated = _matmul(dx, full[k_out], "nt", F32, f"ssm_out_dgrad_{i}")
            gbig[k_out] = _matmul(sv["gated"], dx, "tn", BF16, f"ssm_out_wgrad_{i}")
            dy, dz, dgw = _gnorm_bwd(sv["y"], sv["zx"], full["ssm_norm_w"][j], dgated, f"ssm_norm_bwd_{i}")
            gs["ssm_norm_w"][j] = dgw[0]
            dxs_, dbm, dcm, ddtr, dbias, dalog, ddsk = _ssd_bwd(
                sv["conv"], sv["dtr"], *sv["prm"], sv["sprev"], dy, d_inner, f"ssd_bwd_{i}")
            gs["ssm_dt_bias"][j] = dbias.reshape(-1)[::P]
            gs["ssm_a_log"][j] = dalog.reshape(-1)[::P]
            gs["ssm_d"][j] = ddsk.reshape(ssm_heads, P).sum(axis=1)
            ddt = jnp.pad(ddtr[:, ::P], ((0, 0), (0, in_pad - in_w)))
            dzx, dcw, dcb = _ssm_conv_bwd(dxs_, dbm, dcm, sv["conv"], sv["zx"], dz, ddt, full["ssm_conv_w"][j],
                                          d_inner, f"ssm_conv_bwd_{i}")
            gs["ssm_conv_w"][j], gs["ssm_conv_b"][j] = dcw, dcb[0]
            dwin = bwd_mm(sv["h"], dzx, "tn", BF16, f"ssm_in_wgrad_{i}", send=now([k_up, k_out]))
            gbig[k_in] = jnp.stack([dwin[:, k * shard_in:(k + 1) * shard_in] for k in range(N_CHIPS)])
            dh = bwd_mm(dzx, full[k_in], "nt", F32, f"ssm_in_dgrad_{i}", send=now([k_in]), swap=now([k_up, k_out]))
        if sending is not None:
            keys, ex, handles = sending
            recv.update(zip(keys, _finish_exchange(ex, handles, dh, f"scatter_wait_{i + 1}")))
            to_swap, sending = keys, None
        if own:
            to_swap = to_swap + [k_in]
        else:
            keys = [k_in, k_out, k_up, k_down]
            ex = scatter_ex(keys)
            sending = (keys, ex, _start_exchange(ex, f"scatter_start_{i}"))
        issued = 0.0 if own else sending[2][-1][0, 0]
        dx, dnw = _rms_bwd(sv["x_in"], mix_norm_w[i] + issued, dh, dx, f"mix_norm_bwd_{i}")
        gs["mix_norm_w"][i] = dnw[0]
    grad_x = dx.reshape(x.shape)

    sib.update(zip(to_swap, _run_exchange(swap_ex(to_swap), "swap_last")))

    small_full = [jnp.stack(gs[n]) if n != "final_norm_w" else d_final[0] for n in SMALL]
    small_full.append(loss_part[0, 0:1])
    small_shapes = [a.shape for a in small_full]
    summed = _sum_devices(_all_gather_devices(_pack(small_full), "gather_small_grads"), "sum_small_grads")
    small_g = _unpack(summed, small_shapes)
    loss = small_g[-1][0]
    gsm = {}
    for n, g in zip(SMALL, small_g[:-1]):
        if n in SMALL_SHARDED:
            ax = SMALL_SHARDED[n]
            ext = W[n].shape[ax]
            g = lax.dynamic_slice_in_dim(g, chip * ext, ext, axis=ax)
        gsm[n] = g

    out_g, out_d, out_m, out_v = {}, {}, {}, {}
    for name in BIG:
        outs = None
        for l in range(W[name].shape[0]):
            outs = _adamw_big(recv[(name, l)], sib[(name, l)], W[name], M[name], V[name], l, outs,
                              f"adamw_{name}_{l}")
        out_g[name], out_d[name], out_m[name], out_v[name] = outs
    shapes = [W[n].shape for n in SMALL]
    pd, pm, pv = _adamw_small(_pack([gsm[n] for n in SMALL]), _pack([W[n] for n in SMALL]),
                              _pack([M[n] for n in SMALL]), _pack([V[n] for n in SMALL]), "adamw_small")
    for n, d_, m_, v_ in zip(SMALL, _unpack(pd, shapes), _unpack(pm, shapes), _unpack(pv, shapes)):
        out_g[n], out_d[n], out_m[n], out_v[n] = gsm[n], d_, m_, v_

    return (loss, grad_x, *[out_g[n] for n in WEIGHTS], *[out_d[n] for n in WEIGHTS],
            *[out_m[n] for n in WEIGHTS], *[out_v[n] for n in WEIGHTS])
```

```python
import functools
import math

import jax
import jax.numpy as jnp
from jax import lax
from jax.experimental import pallas as pl
from jax.experimental.pallas import tpu as pltpu

F32 = jnp.float32
BF16 = jnp.bfloat16
MESH = pl.DeviceIdType.MESH

NORM_EPS = 1e-5
HEAD_DIM = 128
ATTN_BLOCK = 128
ATTN_DILATIONS = (1, 4, 16)
ATTN_WINDOWS = (128, 512, 2048)
PERM = 16
ROPE_THETA = 500000.0
ROPE_HALF = HEAD_DIM // 8
SSM_HEAD_DIM = 64
SSM_STATE = 128
SSM_GROUPS = 8
SSM_CHUNK = 128
NEG = -1e30

ADAM_LR = 0.001
ADAM_B1 = 0.9
ADAM_B2 = 0.999
ADAM_EPS = 1e-08
ADAM_WD = 0.01
ADAM_STEP = 10

VMEM_LIMIT_BYTES = 48 * 1024 * 1024
N_CHIPS = 4
N_DEV = 8


def _cp(*sem):
    return pltpu.CompilerParams(dimension_semantics=sem, vmem_limit_bytes=VMEM_LIMIT_BYTES)


def _pick(n, pref, mult=128):
    best = None
    t = mult
    while t <= min(n, pref):
        if n % t == 0:
            best = t
        t += mult
    return n if best is None else best


def _sigmoid(x):
    return 1.0 / (1.0 + jnp.exp(-x))


def _silu(x):
    return x * _sigmoid(x)


def _softplus(x):
    u = jnp.exp(-jnp.abs(x))
    w = 1.0 + u
    log1p = jnp.where(w == 1.0, u, jnp.log(w) * (u / jnp.where(w == 1.0, 1.0, w - 1.0)))
    return jnp.maximum(x, 0.0) + log1p


def _dot(a, b, dims):
    return lax.dot_general(a.astype(BF16), b.astype(BF16), (dims, ((), ())),
                           preferred_element_type=F32)


def _dot_nn(a, b):
    return _dot(a, b, ((1,), (0,)))


def _dot_nt(a, b):
    return _dot(a, b, ((1,), (1,)))


def _dot_tn(a, b):
    return _dot(a, b, ((0,), (0,)))


MATMUL_VMEM_BYTES = 36 * 1024 * 1024
MATMUL_TILES = (2048, 1536, 1408, 1152, 1024, 896, 768, 640, 512, 384, 256, 128)


def _matmul_tiles(M, N, K, a_bytes, b_bytes, o_bytes, has_resid):
    best = None
    for tm in [t for t in MATMUL_TILES if M % t == 0] or [M]:
        for tn in [t for t in MATMUL_TILES if N % t == 0] or [N]:
            for tk in [t for t in MATMUL_TILES if K % t == 0 and t <= 1408] or [K]:
                nk, gm, gn = K // tk, M // tm, N // tn
                vmem = 2 * (tm * tk * a_bytes + tk * tn * b_bytes + tm * tn * o_bytes)
                vmem += (2 * tm * tn * 4 if has_resid else 0) + (tm * tn * 4 if nk > 1 else 0)
                if vmem > MATMUL_VMEM_BYTES:
                    continue
                a_reads = 1 if nk == 1 else gn
                b_reads = 1 if (nk == 1 and gn == 1) else gm
                traffic = M * K * a_bytes * a_reads + K * N * b_bytes * b_reads + M * N * o_bytes
                key = (traffic, gm * gn * nk)
                if best is None or key < best[0]:
                    best = (key, (tm, tn, tk))
    assert best is not None, (M, N, K)
    return best[1]


def _matmul(a, b, mode, out_dtype, name, resid=None, carry=None):
    if mode == "nn":
        (M, K), (K2, N) = a.shape, b.shape
    elif mode == "nt":
        (M, K), (N, K2) = a.shape, b.shape
    else:
        (K, M), (K2, N) = a.shape, b.shape
    assert K == K2, (a.shape, b.shape, mode)
    tm, tn, tk = _matmul_tiles(M, N, K, a.dtype.itemsize, b.dtype.itemsize, jnp.dtype(out_dtype).itemsize,
                               resid is not None)
    nk = K // tk
    gm, gn = M // tm, N // tn
    dims = {"nn": ((1,), (0,)), "nt": ((1,), (1,)), "tn": ((0,), (0,))}[mode]
    has_resid = resid is not None
    nci = len(carry.arrays) if carry else 0
    nco = len(carry.out_shapes) if carry else 0

    def body(a_ref, b_ref, *rest):
        r_ref = rest[0] if has_resid else None
        rest = rest[has_resid:]
        c_ins, o_ref, c_outs, scratch = rest[:nci], rest[nci], rest[nci + 1:nci + 1 + nco], rest[nci + 1 + nco:]
        acc_ref = scratch[0] if nk > 1 else None
        sems = scratch[nk > 1:]
        i, j, k = pl.program_id(0), pl.program_id(1), pl.program_id(2)
        if carry:
            @pl.when((i == 0) & (j == 0) & (k == 0))
            def _():
                carry.start(c_ins, c_outs, sems)

        if nk == 1:
            r = _dot(a_ref[...], b_ref[...], dims)
            if has_resid:
                r = r + r_ref[...]
            o_ref[...] = r.astype(o_ref.dtype)
        else:
            @pl.when(k == 0)
            def _():
                acc_ref[...] = jnp.zeros_like(acc_ref)

            acc_ref[...] += _dot(a_ref[...], b_ref[...], dims)

            @pl.when(k == nk - 1)
            def _():
                r = acc_ref[...]
                if has_resid:
                    r = r + r_ref[...]
                o_ref[...] = r.astype(o_ref.dtype)

        if carry:
            @pl.when((i == gm - 1) & (j == gn - 1) & (k == nk - 1))
            def _():
                carry.wait(c_ins, c_outs, sems)

    if mode == "nn":
        a_spec = pl.BlockSpec((tm, tk), lambda i, j, k: (i, k))
        b_spec = pl.BlockSpec((tk, tn), lambda i, j, k: (k, j))
    elif mode == "nt":
        a_spec = pl.BlockSpec((tm, tk), lambda i, j, k: (i, k))
        b_spec = pl.BlockSpec((tn, tk), lambda i, j, k: (j, k))
    else:
        a_spec = pl.BlockSpec((tk, tm), lambda i, j, k: (k, i))
        b_spec = pl.BlockSpec((tk, tn), lambda i, j, k: (k, j))
    o_spec = pl.BlockSpec((tm, tn), lambda i, j, k: (i, j))
    anyspec = pl.BlockSpec(memory_space=pl.ANY)
    in_specs = [a_spec, b_spec] + ([o_spec] if has_resid else []) + [anyspec] * nci
    args = (a, b) + ((resid,) if has_resid else ()) + (tuple(carry.arrays) if carry else ())
    out_shape = [jax.ShapeDtypeStruct((M, N), out_dtype)] + (carry.out_shapes if carry else [])
    scratch = ([] if nk == 1 else [pltpu.VMEM((tm, tn), F32)]) + (carry.scratch if carry else [])
    sem = ("arbitrary",) * 3 if carry else ("parallel", "parallel", "arbitrary")
    outs = pl.pallas_call(
        body, grid=(gm, gn, nk), in_specs=in_specs, out_specs=[o_spec] + [anyspec] * nco,
        out_shape=out_shape, scratch_shapes=scratch, compiler_params=_cp(*sem), name=name)(*args)
    return (outs[0], outs[1:]) if carry else outs[0]


def _rms_fwd(x, w, name, add=None):
    S, D = x.shape
    t = _pick(S, 512, 8)
    has_add = add is not None

    def body(x_ref, w_ref, *rest):
        xv = x_ref[...]
        if has_add:
            xv = xv + rest[0][...]
            rest[1][...] = xv
        r = lax.rsqrt(jnp.mean(xv * xv, axis=-1, keepdims=True) + NORM_EPS)
        rest[-1][...] = (xv * r * w_ref[...]).astype(rest[-1].dtype)

    row = pl.BlockSpec((t, D), lambda i: (i, 0))
    vec = pl.BlockSpec((1, D), lambda i: (0, 0))
    normed = jax.ShapeDtypeStruct((S, D), BF16)
    if not has_add:
        return pl.pallas_call(body, grid=(S // t,), in_specs=[row, vec], out_specs=row, out_shape=normed,
                              compiler_params=_cp("parallel"), name=name)(x, w.reshape(1, D))
    return pl.pallas_call(body, grid=(S // t,), in_specs=[row, vec, row], out_specs=[row, row],
                          out_shape=[jax.ShapeDtypeStruct((S, D), F32), normed],
                          compiler_params=_cp("parallel"), name=name)(x, w.reshape(1, D), add)


def _rms_bwd(x, w, dh, dres, name):
    S, D = x.shape
    t = _pick(S, 512, 8)

    def body(x_ref, w_ref, dh_ref, dr_ref, dx_ref, dw_ref):
        @pl.when(pl.program_id(0) == 0)
        def _():
            dw_ref[...] = jnp.zeros_like(dw_ref)

        xv = x_ref[...]
        r = lax.rsqrt(jnp.mean(xv * xv, axis=-1, keepdims=True) + NORM_EPS)
        xh = xv * r
        dh_v = dh_ref[...]
        g = dh_v * w_ref[...]
        dx_ref[...] = dr_ref[...] + r * (g - xh * jnp.mean(g * xh, axis=-1, keepdims=True))
        dw_ref[...] += jnp.sum(dh_v * xh, axis=0, keepdims=True)

    row = pl.BlockSpec((t, D), lambda i: (i, 0))
    vec = pl.BlockSpec((1, D), lambda i: (0, 0))
    return pl.pallas_call(
        body, grid=(S // t,), in_specs=[row, vec, row, row], out_specs=[row, vec],
        out_shape=[jax.ShapeDtypeStruct((S, D), F32), jax.ShapeDtypeStruct((1, D), F32)],
        compiler_params=_cp("arbitrary"), name=name)(x, w.reshape(1, D), dh, dres)


def _loss_head(x, w, tgt, name):
    S, D = x.shape
    t = _pick(S, 512, 8)

    def body(x_ref, w_ref, t_ref, dx_ref, dw_ref, l_ref):
        @pl.when(pl.program_id(0) == 0)
        def _():
            dw_ref[...] = jnp.zeros_like(dw_ref)
            l_ref[...] = jnp.zeros_like(l_ref)

        xv = x_ref[...]
        wv = w_ref[...]
        r = lax.rsqrt(jnp.mean(xv * xv, axis=-1, keepdims=True) + NORM_EPS)
        xh = xv * r
        err = xh * wv - t_ref[...]
        per_tok = jnp.mean(err * err, axis=-1, keepdims=True)
        l_ref[...] += 0.5 * jnp.sum(per_tok, axis=0, keepdims=True)
        dy = err * (1.0 / D)
        g = dy * wv
        dx_ref[...] = r * (g - xh * jnp.mean(g * xh, axis=-1, keepdims=True))
        dw_ref[...] += jnp.sum(dy * xh, axis=0, keepdims=True)

    row = pl.BlockSpec((t, D), lambda i: (i, 0))
    vec = pl.BlockSpec((1, D), lambda i: (0, 0))
    lspec = pl.BlockSpec((1, 128), lambda i: (0, 0))
    return pl.pallas_call(
        body, grid=(S // t,), in_specs=[row, vec, row], out_specs=[row, vec, lspec],
        out_shape=[jax.ShapeDtypeStruct((S, D), F32), jax.ShapeDtypeStruct((1, D), F32),
                   jax.ShapeDtypeStruct((1, 128), F32)],
        compiler_params=_cp("arbitrary"), name=name)(x, w.reshape(1, D), tgt)


def _rope_table(seq):
    pos = jnp.arange(seq, dtype=F32)
    inv_freq = ROPE_THETA ** (-jnp.arange(0, 2 * ROPE_HALF, 2, dtype=F32) / (2 * ROPE_HALF))
    ang = pos[:, None] * inv_freq[None, :]
    cos, sin = jnp.cos(ang), jnp.sin(ang)
    pad = HEAD_DIM - 2 * ROPE_HALF
    cos_p = jnp.concatenate([cos, cos, jnp.ones((seq, pad), F32)], axis=1)
    sin_a = jnp.concatenate([-sin, jnp.zeros((seq, HEAD_DIM - ROPE_HALF), F32)], axis=1)
    sin_b = jnp.concatenate([jnp.zeros((seq, ROPE_HALF), F32), sin, jnp.zeros((seq, pad), F32)], axis=1)
    return jnp.concatenate([cos_p, sin_a, sin_b], axis=1)


def _rope(t, tab, sign):
    cos_p = tab[:, 0:HEAD_DIM]
    sin_a = tab[:, HEAD_DIM:2 * HEAD_DIM]
    sin_b = tab[:, 2 * HEAD_DIM:3 * HEAD_DIM]
    up = pltpu.roll(t, HEAD_DIM - ROPE_HALF, 1)
    down = pltpu.roll(t, ROPE_HALF, 1)
    return t * cos_p + sign * (up * sin_a + down * sin_b)


def _perm_tokens(a):
    S = a.shape[0]
    return a.reshape(S // PERM, PERM, -1).transpose(1, 0, 2).reshape(S, -1)


def _unperm_tokens(a):
    S = a.shape[0]
    return a.reshape(PERM, S // PERM, -1).transpose(1, 0, 2).reshape(S, -1)


class _Strided:
    def __init__(self, S, dil):
        self.dil, self.m = dil, PERM // dil
        self.c = ATTN_BLOCK // self.m
        self.rows = S // PERM
        self.nb = S // (dil * ATTN_BLOCK)

    def view(self, a):
        return a.reshape(self.m, self.dil, self.rows, a.shape[-1])

    def spec(self, width, col, f=lambda n: n):
        return pl.BlockSpec((self.m, None, self.c, width), lambda r, n: (0, r, f(n), col))

    def load(self, ref, sl=slice(None)):
        if self.m == 1:
            return ref[0, :, sl]
        return jnp.concatenate([ref[q, :, sl] for q in range(self.m)], axis=0)

    def store(self, ref, sl, val):
        for q in range(self.m):
            ref[q, :, sl] = val[q * self.c:(q + 1) * self.c, :]

    def member(self, i):
        shift = self.c.bit_length() - 1
        return (i & (self.c - 1)) * self.m + (i >> shift)


def _attn_fwd(qkv, tab, g, heads, name):
    S = qkv.shape[0]
    W = heads * HEAD_DIM
    dil = ATTN_DILATIONS[g]
    steps = ATTN_WINDOWS[g] // dil
    B = ATTN_BLOCK
    scale = HEAD_DIM ** -0.5
    st = _Strided(S, dil)

    def body(q_ref, k_ref, kp_ref, v_ref, vp_ref, t_ref, tp_ref, o_ref, l_ref):
        n = pl.program_id(1)
        ii = lax.broadcasted_iota(jnp.int32, (B, 2 * B), 0)
        jj = lax.broadcasted_iota(jnp.int32, (B, 2 * B), 1)
        delta = st.member(ii) - st.member(jj & (B - 1)) + jnp.where(jj >= B, 0, B)
        ok = (delta >= 0) & (delta <= steps) & ((jj >= B) | (n > 0))
        tb = st.load(t_ref)
        tpv = st.load(tp_ref)
        for h in range(heads):
            sl = slice(h * HEAD_DIM, (h + 1) * HEAD_DIM)
            q = _rope(st.load(q_ref, sl), tb, 1.0)
            kc = jnp.concatenate([_rope(st.load(kp_ref, sl), tpv, 1.0), _rope(st.load(k_ref, sl), tb, 1.0)], axis=0)
            vc = jnp.concatenate([st.load(vp_ref, sl), st.load(v_ref, sl)], axis=0)
            s = jnp.where(ok, _dot_nt(q, kc) * scale, NEG)
            m = jnp.max(s, axis=-1, keepdims=True)
            p = jnp.exp(s - m)
            den = jnp.sum(p, axis=-1, keepdims=True)
            st.store(o_ref, sl, _dot_nn(p, vc) / den)
            st.store(l_ref, sl, jnp.broadcast_to(m + jnp.log(den), (B, HEAD_DIM)))

    prv = lambda n: jnp.maximum(n - 1, 0)
    qv, tv = st.view(qkv), st.view(tab)
    o_spec = st.spec(W, 0)
    o, lse = pl.pallas_call(
        body, grid=(dil, st.nb),
        in_specs=[st.spec(W, g * 3), st.spec(W, g * 3 + 1), st.spec(W, g * 3 + 1, prv),
                  st.spec(W, g * 3 + 2), st.spec(W, g * 3 + 2, prv),
                  st.spec(3 * HEAD_DIM, 0), st.spec(3 * HEAD_DIM, 0, prv)],
        out_specs=[o_spec, o_spec],
        out_shape=[jax.ShapeDtypeStruct((st.m, dil, st.rows, W), F32)] * 2,
        compiler_params=_cp("parallel", "parallel"), name=name)(qv, qv, qv, qv, qv, tv, tv)
    return o.reshape(S, W), lse.reshape(S, W)


def _attn_combine(os_, ls_, name):
    S, W = os_[0].shape
    t = _pick(S, 256, 8)

    def body(o0, o1, o2, l0, l1, l2, o_ref, l_ref):
        a, b, c = l0[...], l1[...], l2[...]
        m = jnp.maximum(jnp.maximum(a, b), c)
        ea, eb, ec = jnp.exp(a - m), jnp.exp(b - m), jnp.exp(c - m)
        tot = ea + eb + ec
        o_ref[...] = (ea * o0[...] + eb * o1[...] + ec * o2[...]) / tot
        l_ref[...] = m + jnp.log(tot)

    row = pl.BlockSpec((t, W), lambda i: (i, 0))
    return pl.pallas_call(body, grid=(S // t,), in_specs=[row] * 6, out_specs=[row, row],
                          out_shape=[jax.ShapeDtypeStruct((S, W), F32)] * 2,
                          compiler_params=_cp("parallel"), name=name)(*os_, *ls_)


def _attn_bwd(qkv, tab, o, lse, do, dqkv_prev, g, heads, name):
    S = qkv.shape[0]
    W = heads * HEAD_DIM
    dil = ATTN_DILATIONS[g]
    steps = ATTN_WINDOWS[g] // dil
    B = ATTN_BLOCK
    scale = HEAD_DIM ** -0.5
    st = _Strided(S, dil)
    nb = st.nb
    aliased = dqkv_prev is not None

    def body(q_ref, qn_ref, k_ref, kp_ref, v_ref, vp_ref, do_ref, don_ref, o_ref, on_ref,
             l_ref, ln_ref, t_ref, tp_ref, tn_ref, *rest):
        out_ref = rest[-1]
        n = pl.program_id(1)
        has_next = n < nb - 1
        ia = lax.broadcasted_iota(jnp.int32, (B, 2 * B), 0)
        ja = lax.broadcasted_iota(jnp.int32, (B, 2 * B), 1)
        da = st.member(ia) - st.member(ja & (B - 1)) + jnp.where(ja >= B, 0, B)
        ok_a = (da >= 0) & (da <= steps) & ((ja >= B) | (n > 0))
        ib = lax.broadcasted_iota(jnp.int32, (2 * B, B), 0)
        jb = lax.broadcasted_iota(jnp.int32, (2 * B, B), 1)
        db = st.member(ib & (B - 1)) + jnp.where(ib >= B, B, 0) - st.member(jb)
        ok_b = (db >= 0) & (db <= steps) & ((ib < B) | has_next)
        tb, tpv, tnv = st.load(t_ref), st.load(tp_ref), st.load(tn_ref)
        for h in range(heads):
            sl = slice(h * HEAD_DIM, (h + 1) * HEAD_DIM)
            qr = _rope(st.load(q_ref, sl), tb, 1.0)
            qnr = _rope(st.load(qn_ref, sl), tnv, 1.0)
            kr = _rope(st.load(k_ref, sl), tb, 1.0)
            kpr = _rope(st.load(kp_ref, sl), tpv, 1.0)
            v = st.load(v_ref, sl)
            dov_ = st.load(do_ref, sl)
            donv = st.load(don_ref, sl)
            dl = jnp.sum(dov_ * st.load(o_ref, sl), axis=-1, keepdims=True)
            dln = jnp.sum(donv * st.load(on_ref, sl), axis=-1, keepdims=True)
            ls = st.load(l_ref, sl)
            kc = jnp.concatenate([kpr, kr], axis=0)
            vc = jnp.concatenate([st.load(vp_ref, sl), v], axis=0)
            s = _dot_nt(qr, kc) * scale
            p = jnp.where(ok_a, jnp.exp(jnp.minimum(s - jnp.concatenate([ls, ls], axis=1), 30.0)), 0.0)
            ds = p * (_dot_nt(dov_, vc) - dl) * scale
            st.store(out_ref, sl, _rope(_dot_nn(ds, kc), tb, -1.0))
            qc = jnp.concatenate([qr, qnr], axis=0)
            doc = jnp.concatenate([dov_, donv], axis=0)
            lc = jnp.concatenate([ls, st.load(ln_ref, sl)], axis=0)
            dlc = jnp.concatenate([dl, dln], axis=0)
            s2 = _dot_nt(qc, kr) * scale
            p2 = jnp.where(ok_b, jnp.exp(jnp.minimum(s2 - lc, 30.0)), 0.0)
            ds2 = p2 * (_dot_nt(doc, v) - dlc) * scale
            st.store(out_ref, slice(W + h * HEAD_DIM, W + (h + 1) * HEAD_DIM), _rope(_dot_tn(ds2, qc), tb, -1.0))
            st.store(out_ref, slice(2 * W + h * HEAD_DIM, 2 * W + (h + 1) * HEAD_DIM), _dot_tn(p2, doc))

    nxt = lambda n: jnp.minimum(n + 1, nb - 1)
    prv = lambda n: jnp.maximum(n - 1, 0)
    same = lambda n: n
    q0, q1, q2, tw = g * 3, g * 3 + 1, g * 3 + 2, 3 * HEAD_DIM
    in_specs = [st.spec(W, q0), st.spec(W, q0, nxt), st.spec(W, q1), st.spec(W, q1, prv),
                st.spec(W, q2), st.spec(W, q2, prv)]
    in_specs += [st.spec(W, 0, f) for f in (same, nxt, same, nxt, same, nxt)]
    in_specs += [st.spec(tw, 0, f) for f in (same, prv, nxt)]
    qv, tv, ov, lv, dov = (st.view(a) for a in (qkv, tab, o, lse, do))
    args = [qv, qv, qv, qv, qv, qv, dov, dov, ov, ov, lv, lv, tv, tv, tv]
    kwargs = {}
    if aliased:
        in_specs.append(pl.BlockSpec(memory_space=pl.ANY))
        args.append(st.view(dqkv_prev))
        kwargs["input_output_aliases"] = {len(args) - 1: 0}
    out = pl.pallas_call(
        body, grid=(dil, nb), in_specs=in_specs, out_specs=st.spec(3 * W, g),
        out_shape=jax.ShapeDtypeStruct((st.m, dil, st.rows, 9 * W), F32),
        compiler_params=_cp("parallel", "parallel"), name=name, **kwargs)(*args)
    return out.reshape(S, 9 * W)


def _shift_down(x, halo, s):
    if s == 0:
        return x
    T = x.shape[0]
    xs = pltpu.roll(x, s, 0)
    hs = pltpu.roll(halo, s, 0)
    row8 = lax.broadcasted_iota(jnp.int32, hs.shape, 0)
    top = jnp.where(row8 < s, hs, xs[0:8])
    return top if T == 8 else jnp.concatenate([top, xs[8:T]], axis=0)


def _shift_up(x, halo, s):
    if s == 0:
        return x
    T = x.shape[0]
    xs = pltpu.roll(x, T - s, 0)
    hs = pltpu.roll(halo, 8 - s, 0)
    row8 = lax.broadcasted_iota(jnp.int32, hs.shape, 0)
    bot = jnp.where(row8 >= 8 - s, hs, xs[T - 8:T])
    return jnp.concatenate([xs[0:T - 8], bot], axis=0)


CONV_ROWS = 128
CONV_LANES = 512


def _conv_apply(x, halo, w_ref, wsl, b, K):
    acc = x * w_ref[K - 1, :, wsl] + b
    for s in range(1, K):
        acc = acc + _shift_down(x, halo, s) * w_ref[K - 1 - s, :, wsl]
    return acc


def _conv_accum(dy, dyn, xv, xp, w_ref, dw_ref, db_ref, wsl, K):
    acc = dy * w_ref[K - 1, :, wsl]
    dw_ref[K - 1, :, wsl] += jnp.sum(dy * xv, axis=0, keepdims=True)
    for s in range(1, K):
        acc = acc + _shift_up(dy, dyn, s) * w_ref[K - 1 - s, :, wsl]
        dw_ref[K - 1 - s, :, wsl] += jnp.sum(dy * _shift_down(xv, xp, s), axis=0, keepdims=True)
    db_ref[:, wsl] += jnp.sum(dy, axis=0, keepdims=True)
    return acc


def _row_specs(T, S, width):
    main = pl.BlockSpec((T, width), lambda i: (i, 0))
    prev = pl.BlockSpec((8, width), lambda i: (jnp.maximum(i * (T // 8) - 1, 0), 0))
    nxt = pl.BlockSpec((8, width), lambda i: (jnp.minimum((i + 1) * (T // 8), S // 8 - 1), 0))
    return main, prev, nxt


def _full(shape):
    return pl.BlockSpec(shape, lambda i: (0,) * len(shape))


def _silu_grad(y):
    sg = _sigmoid(y)
    return sg * (1.0 + y * (1.0 - sg))


def _ssm_conv_fwd(zx, w, b, d_inner, conv_dim, name):
    S, wz = zx.shape
    K = w.shape[0]
    T = _pick(S, CONV_ROWS, 8)
    cw = _pick(conv_dim, CONV_LANES)

    def body(x_ref, h_ref, w_ref, b_ref, c_ref):
        has_prev = pl.program_id(0) > 0
        for cs in range(0, conv_dim, cw):
            so, sx = slice(cs, cs + cw), slice(d_inner + cs, d_inner + cs + cw)
            halo = jnp.where(has_prev, h_ref[:, sx], 0.0)
            c_ref[:, so] = _conv_apply(x_ref[:, sx], halo, w_ref, so, b_ref[:, so], K)

    main, prev, _ = _row_specs(T, S, wz)
    return pl.pallas_call(
        body, grid=(S // T,), in_specs=[main, prev, _full((K, 1, conv_dim)), _full((1, conv_dim))],
        out_specs=pl.BlockSpec((T, conv_dim), lambda i: (i, 0)),
        out_shape=jax.ShapeDtypeStruct((S, conv_dim), F32),
        compiler_params=_cp("parallel"), name=name)(zx, zx, w.reshape(K, 1, conv_dim), b.reshape(1, conv_dim))


def _ssm_conv_bwd(dxs, dbm, dcm, conv, zx, dz, ddt, w, d_inner, name):
    S, wz = zx.shape
    K, conv_dim = w.shape
    gn = dbm.shape[1]
    T = _pick(S, CONV_ROWS, 8)
    cw = _pick(math.gcd(d_inner, gn), CONV_LANES)
    nrow = S // T
    tail = wz - d_inner - conv_dim
    assert ddt.shape[1] == tail

    def body(dx_ref, dxn_ref, db_ref_, dbn_ref, dc_ref, dcn_ref, y_ref, yn_ref, x_ref, xp_ref, dz_ref, ddt_ref,
             w_ref, o_ref, dw_ref, dbias_ref):
        i = pl.program_id(0)

        @pl.when(i == 0)
        def _():
            dw_ref[...] = jnp.zeros_like(dw_ref)
            dbias_ref[...] = jnp.zeros_like(dbias_ref)

        has_prev, has_next = i > 0, i < nrow - 1
        for cs in range(0, d_inner, cw):
            o_ref[:, cs:cs + cw] = dz_ref[:, cs:cs + cw].astype(o_ref.dtype)
        o_ref[:, d_inner + conv_dim:wz] = ddt_ref[...].astype(o_ref.dtype)
        for cs in range(0, conv_dim, cw):
            so, sx = slice(cs, cs + cw), slice(d_inner + cs, d_inner + cs + cw)
            if cs < d_inner:
                src, srcn, ss = dx_ref, dxn_ref, slice(cs, cs + cw)
            elif cs < d_inner + gn:
                src, srcn, ss = db_ref_, dbn_ref, slice(cs - d_inner, cs - d_inner + cw)
            else:
                src, srcn, ss = dc_ref, dcn_ref, slice(cs - d_inner - gn, cs - d_inner - gn + cw)
            dy = src[:, ss] * _silu_grad(y_ref[:, so])
            dyn = jnp.where(has_next, srcn[:, ss] * _silu_grad(yn_ref[:, so]), 0.0)
            xp = jnp.where(has_prev, xp_ref[:, sx], 0.0)
            o_ref[:, sx] = _conv_accum(dy, dyn, x_ref[:, sx], xp, w_ref, dw_ref, dbias_ref, so, K).astype(o_ref.dtype)

    xm, _, xn = _row_specs(T, S, d_inner)
    gm, _, gnx = _row_specs(T, S, gn)
    cm, _, cn = _row_specs(T, S, conv_dim)
    zm, zp, _ = _row_specs(T, S, wz)
    tm_, _, _ = _row_specs(T, S, tail)
    dzx, dw, db = pl.pallas_call(
        body, grid=(nrow,),
        in_specs=[xm, xn, gm, gnx, gm, gnx, cm, cn, zm, zp, xm, tm_, _full((K, 1, conv_dim))],
        out_specs=[zm, _full((K, 1, conv_dim)), _full((1, conv_dim))],
        out_shape=[jax.ShapeDtypeStruct((S, wz), BF16), jax.ShapeDtypeStruct((K, 1, conv_dim), F32),
                   jax.ShapeDtypeStruct((1, conv_dim), F32)],
        compiler_params=_cp("arbitrary"), name=name)(
            dxs, dxs, dbm, dbm, dcm, dcm, conv, conv, zx, zx, dz, ddt, w.reshape(K, 1, conv_dim))
    return dzx, dw.reshape(K, conv_dim), db


def _ffn_conv_fwd(up, w, b, name):
    S, C = up.shape
    F = C // 2
    K = w.shape[0]
    T = _pick(S, CONV_ROWS, 8)
    cw = _pick(F, CONV_LANES)

    def body(x_ref, h_ref, w_ref, b_ref, a_ref):
        has_prev = pl.program_id(0) > 0
        for cs in range(0, F, cw):
            sg, su = slice(cs, cs + cw), slice(F + cs, F + cs + cw)
            gate = _conv_apply(x_ref[:, sg], jnp.where(has_prev, h_ref[:, sg], 0.0), w_ref, sg, b_ref[:, sg], K)
            upv = _conv_apply(x_ref[:, su], jnp.where(has_prev, h_ref[:, su], 0.0), w_ref, su, b_ref[:, su], K)
            a_ref[:, sg] = (gate * _sigmoid(gate) * upv).astype(a_ref.dtype)

    main, prev, _ = _row_specs(T, S, C)
    return pl.pallas_call(
        body, grid=(S // T,), in_specs=[main, prev, _full((K, 1, C)), _full((1, C))],
        out_specs=pl.BlockSpec((T, F), lambda i: (i, 0)), out_shape=jax.ShapeDtypeStruct((S, F), BF16),
        compiler_params=_cp("parallel"), name=name)(up, up, w.reshape(K, 1, C), b.reshape(1, C))


def _ffn_conv_bwd(dact, up, w, b, name):
    S, C = up.shape
    F = C // 2
    K = w.shape[0]
    T = _pick(S, CONV_ROWS, 8)
    cw = _pick(F, CONV_LANES)
    nrow = S // T

    def du(gate, upv, d):
        sg = _sigmoid(gate)
        return d * upv * sg * (1.0 + gate * (1.0 - sg)), d * gate * sg

    def body(d_ref, dn_ref, x_ref, xp_ref, xn_ref, w_ref, b_ref, dx_ref, dw_ref, db_ref):
        i = pl.program_id(0)

        @pl.when(i == 0)
        def _():
            dw_ref[...] = jnp.zeros_like(dw_ref)
            db_ref[...] = jnp.zeros_like(db_ref)

        has_prev, has_next = i > 0, i < nrow - 1
        for cs in range(0, F, cw):
            sf = slice(cs, cs + cw)
            cols = [slice(half * F + cs, half * F + cs + cw) for half in range(2)]
            xs = [x_ref[:, sc] for sc in cols]
            xps = [jnp.where(has_prev, xp_ref[:, sc], 0.0) for sc in cols]
            u = [_conv_apply(xs[q], xps[q], w_ref, cols[q], b_ref[:, cols[q]], K) for q in range(2)]
            un = [_conv_apply(xn_ref[:, cols[q]], xs[q][T - 8:T], w_ref, cols[q], b_ref[:, cols[q]], K)
                  for q in range(2)]
            dys = du(u[0], u[1], d_ref[:, sf])
            dyns = du(un[0], un[1], dn_ref[:, sf])
            for q in range(2):
                dyn = jnp.where(has_next, dyns[q], 0.0)
                dx_ref[:, cols[q]] = _conv_accum(dys[q], dyn, xs[q], xps[q], w_ref, dw_ref, db_ref, cols[q],
                                                 K).astype(dx_ref.dtype)

    am, _, an = _row_specs(T, S, F)
    xm, xp_, xn_ = _row_specs(T, S, C)
    dx, dw, db = pl.pallas_call(
        body, grid=(nrow,), in_specs=[am, an, xm, xp_, xn_, _full((K, 1, C)), _full((1, C))],
        out_specs=[xm, _full((K, 1, C)), _full((1, C))],
        out_shape=[jax.ShapeDtypeStruct((S, C), BF16), jax.ShapeDtypeStruct((K, 1, C), F32),
                   jax.ShapeDtypeStruct((1, C), F32)],
        compiler_params=_cp("arbitrary"), name=name)(dact, dact, up, up, up, w.reshape(K, 1, C), b.reshape(1, C))
    return dx, dw.reshape(K, C), db


def _cumsum_rows(v):
    n = v.shape[0]
    row = lax.broadcasted_iota(jnp.int32, v.shape, 0)
    k = 1
    while k < n:
        v = v + jnp.where(row >= k, pltpu.roll(v, k, 0), 0.0)
        k *= 2
    return v


def _rev_cumsum_rows(v):
    n = v.shape[0]
    row = lax.broadcasted_iota(jnp.int32, v.shape, 0)
    k = 1
    while k < n:
        v = v + jnp.where(row < n - k, pltpu.roll(v, n - k, 0), 0.0)
        k *= 2
    return v


def _ssd_common(x_ref, dtr_ref, bias_ref, alog_ref, gw):
    Q = SSM_CHUNK
    X = _silu(x_ref[...])
    pre = dtr_ref[...] + bias_ref[...]
    dt = _softplus(pre)
    a = -jnp.exp(alog_ref[...])
    cs = _cumsum_rows(dt * a)
    row = lax.broadcasted_iota(jnp.int32, (Q, gw), 0)
    cs_last = jnp.sum(jnp.where(row == Q - 1, cs, 0.0), axis=0, keepdims=True)
    return X, pre, dt, a, cs, cs_last, row


def _head_decay(cs, head_mask):
    Q = SSM_CHUNK
    col = jnp.max(jnp.where(head_mask, cs, NEG), axis=1, keepdims=True)
    acol = jnp.broadcast_to(col, (Q, Q))
    arow = acol.T
    ii = lax.broadcasted_iota(jnp.int32, (Q, Q), 0)
    jj = lax.broadcasted_iota(jnp.int32, (Q, Q), 1)
    tril = ii >= jj
    return jnp.where(tril, jnp.exp(jnp.where(tril, acol - arow, 0.0)), 0.0), tril


def _ssd_specs(S, d_inner, gw, nc, rev):
    Q, N, G = SSM_CHUNK, SSM_STATE, SSM_GROUPS
    ch = (lambda c: nc - 1 - c) if rev else (lambda c: c)
    x_spec = pl.BlockSpec((Q, gw), lambda g, c: (ch(c), g))
    b_spec = pl.BlockSpec((Q, N), lambda g, c: (ch(c), d_inner // N + g))
    c_spec = pl.BlockSpec((Q, N), lambda g, c: (ch(c), d_inner // N + G + g))
    p_spec = pl.BlockSpec((None, 1, gw), lambda g, c: (g, 0, 0))
    s_spec = pl.BlockSpec((None, None, gw, N), lambda g, c: (ch(c), g, 0, 0))
    return x_spec, b_spec, c_spec, p_spec, s_spec


def _ssd_fwd(xbc, dtr, bias, alog, dsk, d_inner, name):
    S = xbc.shape[0]
    Q, N, G, P = SSM_CHUNK, SSM_STATE, SSM_GROUPS, SSM_HEAD_DIM
    gw = d_inner // G
    R = gw // P
    nc = S // Q

    def body(x_ref, b_ref, c_ref, dtr_ref, bias_ref, alog_ref, d_ref, y_ref, sp_ref, s_scr):
        @pl.when(pl.program_id(1) == 0)
        def _():
            s_scr[...] = jnp.zeros_like(s_scr)

        X, _, dt, a, cs, cs_last, row = _ssd_common(x_ref, dtr_ref, bias_ref, alog_ref, gw)
        Bm, Cm = _silu(b_ref[...]), _silu(c_ref[...])
        xdt = X * dt
        lane = lax.broadcasted_iota(jnp.int32, (Q, gw), 1)
        sprev = s_scr[...]
        sp_ref[...] = sprev
        cb = _dot_nt(Cm, Bm)
        y = jnp.exp(cs) * _dot_nt(Cm, sprev)
        for r in range(R):
            hm = (lane >= r * P) & (lane < (r + 1) * P)
            dec_l, _ = _head_decay(cs, hm)
            y = y + _dot_nn(cb * dec_l, jnp.where(hm, xdt, 0.0))
        dec = jnp.exp(cs_last - cs)
        cd = jnp.exp(jnp.broadcast_to(cs_last, (Q, gw)).T)
        s_scr[...] = sprev * cd + _dot_tn(xdt * dec, Bm)
        y_ref[...] = y + d_ref[...] * X

    x_spec, b_spec, c_spec, p_spec, s_spec = _ssd_specs(S, d_inner, gw, nc, False)
    return pl.pallas_call(
        body, grid=(G, nc), in_specs=[x_spec, b_spec, c_spec, x_spec, p_spec, p_spec, p_spec],
        out_specs=[x_spec, s_spec],
        out_shape=[jax.ShapeDtypeStruct((S, d_inner), F32), jax.ShapeDtypeStruct((nc, G, gw, N), F32)],
        scratch_shapes=[pltpu.VMEM((gw, N), F32)],
        compiler_params=_cp("parallel", "arbitrary"), name=name)(xbc, xbc, xbc, dtr, bias, alog, dsk)


def _ssd_bwd(xbc, dtr, bias, alog, dsk, sprev_all, dy, d_inner, name):
    S = xbc.shape[0]
    Q, N, G, P = SSM_CHUNK, SSM_STATE, SSM_GROUPS, SSM_HEAD_DIM
    gw = d_inner // G
    R = gw // P
    nc = S // Q

    def body(x_ref, b_ref, c_ref, dtr_ref, bias_ref, alog_ref, d_ref, sp_ref, dy_ref,
             dx_ref, db_ref, dc_ref, ddt_ref, dbias_ref, dalog_ref, dd_ref, ds_scr):
        @pl.when(pl.program_id(1) == 0)
        def _():
            ds_scr[...] = jnp.zeros_like(ds_scr)
            dbias_ref[...] = jnp.zeros_like(dbias_ref)
            dalog_ref[...] = jnp.zeros_like(dalog_ref)
            dd_ref[...] = jnp.zeros_like(dd_ref)

        X, pre, dt, a, cs, cs_last, row = _ssd_common(x_ref, dtr_ref, bias_ref, alog_ref, gw)
        Bm, Cm = _silu(b_ref[...]), _silu(c_ref[...])
        dY = dy_ref[...]
        sprev = sp_ref[...]
        dsn = ds_scr[...]
        xdt = X * dt
        lane = lax.broadcasted_iota(jnp.int32, (Q, gw), 1)
        lane1 = lax.broadcasted_iota(jnp.int32, (1, gw), 1)
        srow = lax.broadcasted_iota(jnp.int32, (gw, N), 0)
        ecs = jnp.exp(cs)
        dec = jnp.exp(cs_last - cs)
        cd = jnp.exp(jnp.broadcast_to(cs_last, (Q, gw)).T)
        dd_ref[...] += jnp.sum(dY * X, axis=0, keepdims=True)
        dX = d_ref[...] * dY
        ey = ecs * dY
        dcs = ey * _dot_nt(Cm, sprev)
        dC = _dot_nn(ey, sprev)
        ds_scr[...] = cd * dsn + _dot_tn(ey, Cm)
        wmat = _dot_nt(Bm, dsn)
        dxdt = dec * wmat
        xd = xdt * dec
        dB = _dot_nn(xd, dsn)
        ddec = xdt * wmat * dec
        dcs = dcs - ddec
        dlast = jnp.sum(ddec, axis=0, keepdims=True)
        qmat = dsn * sprev * cd
        cb = _dot_nt(Cm, Bm)
        dcb = jnp.zeros((Q, Q), F32)
        dcs_rep = jnp.zeros((Q, gw), F32)
        dtx_rep = jnp.zeros((Q, gw), F32)
        for r in range(R):
            hm = (lane >= r * P) & (lane < (r + 1) * P)
            dec_l, tril = _head_decay(cs, hm)
            dyr = jnp.where(hm, dY, 0.0)
            gmat = jnp.where(tril, _dot_nt(dyr, xdt), 0.0)
            dcb = dcb + gmat * dec_l
            e = gmat * cb * dec_l
            v = (jnp.sum(e, axis=1, keepdims=True) - jnp.sum(e.T, axis=1, keepdims=True)
                 + jnp.sum(jnp.where(hm, dcs, 0.0), axis=1, keepdims=True))
            dxdt = dxdt + _dot_tn(cb * dec_l, dyr)
            hm1 = (lane1 >= r * P) & (lane1 < (r + 1) * P)
            t_last = (jnp.sum(jnp.where(hm1, dlast, 0.0), axis=1, keepdims=True)
                      + jnp.sum(jnp.where((srow >= r * P) & (srow < (r + 1) * P), qmat, 0.0), keepdims=True))
            dcs_rep = dcs_rep + jnp.where(hm, v, 0.0) + jnp.where(hm & (row == Q - 1), t_last, 0.0)
        for r in range(R):
            hm = (lane >= r * P) & (lane < (r + 1) * P)
            w_r = jnp.sum(jnp.where(hm, dxdt * X, 0.0), axis=1, keepdims=True)
            dtx_rep = dtx_rep + jnp.where(hm, w_r, 0.0)
        dadt = _rev_cumsum_rows(dcs_rep)
        ddt = a * dadt + dtx_rep
        dalog_ref[...] += jnp.sum(dt * dadt, axis=0, keepdims=True) * a
        draw = ddt * _sigmoid(pre)
        ddt_ref[...] = draw
        dbias_ref[...] += jnp.sum(draw, axis=0, keepdims=True)
        dx_ref[...] = dX + dxdt * dt
        db_ref[...] = dB + _dot_tn(dcb, Cm)
        dc_ref[...] = dC + _dot_nn(dcb, Bm)

    x_spec, b_spec, c_spec, p_spec, s_spec = _ssd_specs(S, d_inner, gw, nc, True)
    n_spec = pl.BlockSpec((Q, N), lambda g, c: (nc - 1 - c, g))
    gshape = jax.ShapeDtypeStruct((G, 1, gw), F32)
    return pl.pallas_call(
        body, grid=(G, nc),
        in_specs=[x_spec, b_spec, c_spec, x_spec, p_spec, p_spec, p_spec, s_spec, x_spec],
        out_specs=[x_spec, n_spec, n_spec, x_spec, p_spec, p_spec, p_spec],
        out_shape=[jax.ShapeDtypeStruct((S, d_inner), F32), jax.ShapeDtypeStruct((S, G * N), F32),
                   jax.ShapeDtypeStruct((S, G * N), F32), jax.ShapeDtypeStruct((S, d_inner), F32),
                   gshape, gshape, gshape],
        scratch_shapes=[pltpu.VMEM((gw, N), F32)],
        compiler_params=_cp("parallel", "arbitrary"), name=name)(
            xbc, xbc, xbc, dtr, bias, alog, dsk, sprev_all, dy)


def _gnorm_fwd(y, zx, w, name):
    S, d_inner = y.shape
    G = SSM_GROUPS
    gw = d_inner // G
    T = _pick(S, 256, 8)

    def body(y_ref, z_ref, w_ref, o_ref):
        for k in range(G):
            sl = slice(k * gw, (k + 1) * gw)
            z = z_ref[:, sl]
            gk = y_ref[:, sl] * z * _sigmoid(z)
            r = lax.rsqrt(jnp.mean(gk * gk, axis=-1, keepdims=True) + NORM_EPS)
            o_ref[:, sl] = (gk * r * w_ref[:, sl]).astype(o_ref.dtype)

    row = pl.BlockSpec((T, d_inner), lambda i: (i, 0))
    vec = pl.BlockSpec((1, d_inner), lambda i: (0, 0))
    return pl.pallas_call(body, grid=(S // T,), in_specs=[row, row, vec], out_specs=row,
                          out_shape=jax.ShapeDtypeStruct((S, d_inner), BF16),
                          compiler_params=_cp("parallel"), name=name)(y, zx, w.reshape(1, d_inner))


def _gnorm_bwd(y, zx, w, dout, name):
    S, d_inner = y.shape
    G = SSM_GROUPS
    gw = d_inner // G
    T = _pick(S, 256, 8)

    def body(y_ref, z_ref, w_ref, d_ref, dy_ref, dz_ref, dw_ref):
        @pl.when(pl.program_id(0) == 0)
        def _():
            dw_ref[...] = jnp.zeros_like(dw_ref)

        for k in range(G):
            sl = slice(k * gw, (k + 1) * gw)
            z, yv, d = z_ref[:, sl], y_ref[:, sl], d_ref[:, sl]
            sg = _sigmoid(z)
            sz = z * sg
            gk = yv * sz
            r = lax.rsqrt(jnp.mean(gk * gk, axis=-1, keepdims=True) + NORM_EPS)
            gh = gk * r
            dw_ref[:, sl] += jnp.sum(d * gh, axis=0, keepdims=True)
            dg = d * w_ref[:, sl]
            dgk = r * (dg - gh * jnp.mean(dg * gh, axis=-1, keepdims=True))
            dy_ref[:, sl] = dgk * sz
            dz_ref[:, sl] = dgk * yv * sg * (1.0 + z * (1.0 - sg))

    row = pl.BlockSpec((T, d_inner), lambda i: (i, 0))
    vec = pl.BlockSpec((1, d_inner), lambda i: (0, 0))
    return pl.pallas_call(
        body, grid=(S // T,), in_specs=[row, row, vec, row], out_specs=[row, row, vec],
        out_shape=[jax.ShapeDtypeStruct((S, d_inner), F32)] * 2 + [jax.ShapeDtypeStruct((1, d_inner), F32)],
        compiler_params=_cp("arbitrary"), name=name)(y, zx, w.reshape(1, d_inner), dout)


def _adam_math(g, w, m, v):
    m = ADAM_B1 * m + (1.0 - ADAM_B1) * g
    v = ADAM_B2 * v + (1.0 - ADAM_B2) * (g * g)
    m_hat = m / (1.0 - ADAM_B1 ** ADAM_STEP)
    v_hat = v / (1.0 - ADAM_B2 ** ADAM_STEP)
    delta = -ADAM_LR * (m_hat / (jnp.sqrt(v_hat) + ADAM_EPS) + ADAM_WD * w)
    return delta, m, v


def _adamw_big(own, sib, w, m, v, layer, prev, name):
    L, A, Bc = w.shape
    T = _pick(A, max(8, (1 << 19) // (4 * Bc)), 16)

    def body(o_ref, s_ref, w_ref, m_ref, v_ref, *rest):
        g_ref, d_ref, nm_ref, nv_ref = rest[-4:]
        so = o_ref[0].astype(F32)
        ss = s_ref[0].astype(F32)
        for k in range(1, N_CHIPS):
            so = so + o_ref[k].astype(F32)
            ss = ss + s_ref[k].astype(F32)
        g = so + ss
        delta, nm, nv = _adam_math(g, w_ref[...], m_ref[...], v_ref[...])
        g_ref[...] = g
        d_ref[...] = delta
        nm_ref[...] = nm
        nv_ref[...] = nv

    part = pl.BlockSpec((N_CHIPS, T, Bc), lambda i: (0, i, 0))
    blk = pl.BlockSpec((None, T, Bc), lambda i: (layer, i, 0))
    shp = jax.ShapeDtypeStruct(w.shape, F32)
    in_specs, args, kwargs = [part, part, blk, blk, blk], [own, sib, w, m, v], {}
    if prev is not None:
        in_specs += [pl.BlockSpec(memory_space=pl.ANY)] * 4
        args += list(prev)
        kwargs["input_output_aliases"] = {5 + q: q for q in range(4)}
    return pl.pallas_call(body, grid=(A // T,), in_specs=in_specs, out_specs=[blk] * 4, out_shape=[shp] * 4,
                          compiler_params=_cp("parallel"), name=name, **kwargs)(*args)


def _sum_devices(parts, name):
    _, R, C = parts.shape

    def body(p_ref, o_ref):
        acc = p_ref[0]
        for k in range(1, N_DEV):
            acc = acc + p_ref[k]
        o_ref[...] = acc

    return pl.pallas_call(body, out_shape=jax.ShapeDtypeStruct((R, C), F32), name=name)(parts)


def _adamw_small(g, w, m, v, name):
    def body(g_ref, w_ref, m_ref, v_ref, d_ref, nm_ref, nv_ref):
        delta, nm, nv = _adam_math(g_ref[...], w_ref[...], m_ref[...], v_ref[...])
        d_ref[...] = delta
        nm_ref[...] = nm
        nv_ref[...] = nv

    shp = jax.ShapeDtypeStruct(g.shape, F32)
    return pl.pallas_call(body, out_shape=[shp] * 3, name=name)(g, w, m, v)


PACK_COLS = 1024


def _pack(arrs):
    flat = jnp.concatenate([a.reshape(-1).astype(F32) for a in arrs])
    n = flat.shape[0]
    rows = -(-n // (8 * PACK_COLS)) * 8
    return jnp.pad(flat, (0, rows * PACK_COLS - n)).reshape(rows, PACK_COLS)


def _unpack(packed, shapes):
    flat = packed.reshape(-1)
    out, off = [], 0
    for s in shapes:
        n = math.prod(s)
        out.append(flat[off:off + n].reshape(s))
        off += n
    return out


def _shard_ref(ref, kind, k, n):
    if kind == "col":
        return ref.at[:, pl.ds(pl.multiple_of(k * n, 128), n)]
    if kind == "row":
        return ref.at[pl.ds(pl.multiple_of(k * n, 16), n), :]
    return ref.at[k]


def _chip_peers():
    x, y, c = lax.axis_index("x"), lax.axis_index("y"), lax.axis_index("c")
    return x, y, c, [(1 - x, y), (x, 1 - y), (1 - x, 1 - y)]


class _Exchange:
    def __init__(self, mode, items):
        self.mode, self.items = mode, items
        self.arrays = []
        for it in items:
            if not any(it[0] is a for a in self.arrays):
                self.arrays.append(it[0])
        self.src_idx = [next(i for i, a in enumerate(self.arrays) if a is it[0]) for it in items]
        self.out_shapes = [jax.ShapeDtypeStruct(it[-1], it[0].dtype) for it in items]
        n = len(items)
        if mode == "swap":
            self.scratch = [pltpu.SemaphoreType.DMA((n,)), pltpu.SemaphoreType.DMA((n,))]
        else:
            self.scratch = [pltpu.SemaphoreType.DMA((3 * n,)), pltpu.SemaphoreType.DMA((3 * n,)),
                            pltpu.SemaphoreType.DMA((n,))]

    def _copies(self, ins, outs, sems):
        if self.mode == "swap":
            send_sems, recv_sems = sems
            x, y, c = lax.axis_index("x"), lax.axis_index("y"), lax.axis_index("c")
            sent = [pltpu.make_async_remote_copy(
                src_ref=ins[self.src_idx[t]], dst_ref=outs[t], send_sem=send_sems.at[t], recv_sem=recv_sems.at[t],
                device_id=(x, y, 1 - c), device_id_type=MESH) for t in range(len(self.items))]
            return [], sent, sent
        send_sems, recv_sems, loc_sems = sems
        x, y, c, peers = _chip_peers()
        me = 2 * x + y
        local, sent, arriving = [], [], []
        for t, it in enumerate(self.items):
            src_arr = ins[self.src_idx[t]]
            if self.mode == "gather":
                _, layer, kind, n, _ = it
                src = src_arr if layer is None else src_arr.at[layer]
                src_for = lambda k: src
                dst_from = lambda k: _shard_ref(outs[t], kind, k, n)
            else:
                _, kind, n, _ = it
                src_for = lambda k: _shard_ref(src_arr, kind, k, n)
                dst_from = lambda k: outs[t].at[k]
            local.append(pltpu.make_async_copy(src_for(me), dst_from(me), loc_sems.at[t]))
            for j, (px, py) in enumerate(peers):
                pk = 2 * px + py
                args = dict(send_sem=send_sems.at[3 * t + j], recv_sem=recv_sems.at[3 * t + j],
                            device_id=(px, py, c), device_id_type=MESH)
                sent.append(pltpu.make_async_remote_copy(src_ref=src_for(pk), dst_ref=dst_from(me), **args))
                arriving.append(pltpu.make_async_remote_copy(src_ref=src_for(pk), dst_ref=dst_from(pk), **args))
        return local, sent, arriving

    def start(self, ins, outs, sems):
        local, sent, arriving = self._copies(ins, outs, sems)
        for cp in local + sent:
            cp.start()
        for cp in arriving:
            cp._used = True

    def wait(self, ins, outs, sems):
        local, sent, arriving = self._copies(ins, outs, sems)
        for cp in arriving:
            cp.wait_recv()
        for cp in sent:
            cp.wait_send()
        for cp in local:
            cp.wait()


class _Multi:
    def __init__(self, parts):
        self.parts = parts
        self.arrays = [a for p in parts for a in p.arrays]
        self.out_shapes = [s for p in parts for s in p.out_shapes]
        self.scratch = [s for p in parts for s in p.scratch]

    def _split(self, ins, outs, sems):
        i = o = s = 0
        for p in self.parts:
            ni, no, ns = len(p.arrays), len(p.out_shapes), len(p.scratch)
            yield p, ins[i:i + ni], outs[o:o + no], sems[s:s + ns]
            i, o, s = i + ni, o + no, s + ns

    def start(self, ins, outs, sems):
        for p, a, b, c in self._split(ins, outs, sems):
            p.start(a, b, c)

    def wait(self, ins, outs, sems):
        for p, a, b, c in self._split(ins, outs, sems):
            p.wait(a, b, c)


def _run_exchange(ex, name):
    nin, nout = len(ex.arrays), len(ex.out_shapes)

    def body(*refs):
        ins, outs, sems = refs[:nin], refs[nin:nin + nout], refs[nin + nout:]
        ex.start(ins, outs, sems)
        ex.wait(ins, outs, sems)

    anyspec = pl.BlockSpec(memory_space=pl.ANY)
    return pl.pallas_call(body, in_specs=[anyspec] * nin, out_specs=[anyspec] * nout, out_shape=ex.out_shapes,
                          scratch_shapes=ex.scratch, name=name)(*ex.arrays)


def _start_exchange(ex, name, after=None):
    nin, nout, nsem = len(ex.arrays), len(ex.out_shapes), len(ex.scratch)
    hbm = pl.BlockSpec(memory_space=pltpu.HBM)
    sem = pl.BlockSpec(memory_space=pltpu.SEMAPHORE)

    n_operands = nin + nout + (after is not None)

    def body(*refs):
        ins, lands = refs[:nin], refs[nin:nin + nout]
        sems = refs[n_operands:n_operands + nsem]
        ex.start(ins, lands, sems)
        refs[-1][...] = jnp.zeros_like(refs[-1])

    args = [pltpu.with_memory_space_constraint(a, pltpu.HBM) for a in ex.arrays]
    args += [pltpu.with_memory_space_constraint(lax.empty(s.shape, s.dtype), pltpu.HBM) for s in ex.out_shapes]
    thru = [pltpu.HBM(a.shape, a.dtype) for a in ex.arrays] + [pltpu.HBM(s.shape, s.dtype) for s in ex.out_shapes]
    extra = [] if after is None else [pl.BlockSpec(memory_space=pl.ANY)]
    args += [] if after is None else [after]
    return pl.pallas_call(
        body, name=name, in_specs=[hbm] * (nin + nout) + extra,
        out_shape=tuple(ex.scratch) + tuple(thru) + (jax.ShapeDtypeStruct((8, 128), F32),),
        out_specs=tuple([sem] * nsem + [hbm] * (nin + nout) + [pl.BlockSpec(memory_space=pltpu.VMEM)]),
        input_output_aliases={q: nsem + q for q in range(nin + nout)},
        compiler_params=pltpu.CompilerParams(has_side_effects=pltpu.SideEffectType.DATAFLOW_SIDE_EFFECTING))(*args)


def _finish_exchange(ex, handles, after, name):
    nin, nout, nsem = len(ex.arrays), len(ex.out_shapes), len(ex.scratch)
    hbm = pl.BlockSpec(memory_space=pltpu.HBM)
    sem = pl.BlockSpec(memory_space=pltpu.SEMAPHORE)
    sems, thru = handles[:nsem], handles[nsem:nsem + nin + nout]

    def body(*refs):
        ins, lands = refs[:nin], refs[nin:nin + nout]
        ex.wait(ins, lands, refs[nin + nout:nin + nout + nsem])

    outs = pl.pallas_call(
        body, name=name, in_specs=[hbm] * (nin + nout) + [sem] * nsem + [pl.BlockSpec(memory_space=pl.ANY)],
        out_shape=tuple(pltpu.HBM(t.shape, t.dtype) for t in thru), out_specs=tuple([hbm] * (nin + nout)),
        input_output_aliases={q: q for q in range(nin + nout)},
        compiler_params=pltpu.CompilerParams(has_side_effects=pltpu.SideEffectType.DATAFLOW_SIDE_EFFECTING))(
            *thru, *sems, after)
    return outs[nin:]


def _all_gather_devices(v, name):
    def body(v_ref, o_ref, send_sems, recv_sems, loc_sem):
        x, y, c = lax.axis_index("x"), lax.axis_index("y"), lax.axis_index("c")
        me = 4 * x + 2 * y + c
        lc = pltpu.make_async_copy(v_ref, o_ref.at[me], loc_sem)
        lc.start()
        rel = [(bx, by, bc) for bx in (0, 1) for by in (0, 1) for bc in (0, 1)][1:]
        copies = []
        for j, (bx, by, bc) in enumerate(rel):
            px, py, pc = x ^ bx, y ^ by, c ^ bc
            copies.append((pltpu.make_async_remote_copy(
                src_ref=v_ref, dst_ref=o_ref.at[me], send_sem=send_sems.at[j], recv_sem=recv_sems.at[j],
                device_id=(px, py, pc), device_id_type=MESH), 4 * px + 2 * py + pc))
        for cp, _ in copies:
            cp.start()
        for j, (cp, pid) in enumerate(copies):
            pltpu.make_async_remote_copy(
                src_ref=v_ref, dst_ref=o_ref.at[pid], send_sem=send_sems.at[j], recv_sem=recv_sems.at[j],
                device_id=(x, y, c), device_id_type=MESH).wait_recv()
        for cp, _ in copies:
            cp.wait_send()
        lc.wait()

    anyspec = pl.BlockSpec(memory_space=pl.ANY)
    return pl.pallas_call(
        body, in_specs=[anyspec], out_specs=anyspec,
        out_shape=jax.ShapeDtypeStruct((N_DEV,) + v.shape, v.dtype),
        scratch_shapes=[pltpu.SemaphoreType.DMA((N_DEV - 1,)), pltpu.SemaphoreType.DMA((N_DEV - 1,)),
                        pltpu.SemaphoreType.DMA(())],
        name=name)(v)


BIG = ("attn_w_qkv", "attn_w_o", "ssm_w_in", "ssm_w_out", "ffn_w_up", "ffn_w_down")
BIG_KIND = {"attn_w_qkv": "col", "attn_w_o": "row", "ssm_w_in": "lead", "ssm_w_out": "row",
            "ffn_w_up": "col", "ffn_w_down": "row"}
SMALL_SHARDED = {"ssm_conv_w": 2, "ssm_conv_b": 1, "ssm_norm_w": 1, "ffn_conv_w": 2}
SMALL = ("mix_norm_w", "ssm_conv_w", "ssm_conv_b", "ssm_dt_bias", "ssm_a_log", "ssm_d", "ssm_norm_w",
         "ffn_norm_w", "ffn_conv_w", "ffn_conv_b", "final_norm_w")
WEIGHTS = ("mix_norm_w", "attn_w_qkv", "attn_w_o", "ssm_w_in", "ssm_conv_w", "ssm_conv_b", "ssm_dt_bias",
           "ssm_a_log", "ssm_d", "ssm_norm_w", "ssm_w_out", "ffn_norm_w", "ffn_w_up", "ffn_conv_w",
           "ffn_conv_b", "ffn_w_down", "final_norm_w")


def _shard_extent(name, shape):
    _, a, b = shape
    return {"col": b, "row": a, "lead": 1}[BIG_KIND[name]]


def _gather_item(w16, name):
    a, b = w16.shape
    kind = BIG_KIND[name]
    full = {"col": (a, N_CHIPS * b), "row": (N_CHIPS * a, b), "lead": (N_CHIPS, a, b)}[kind]
    return (w16, None, kind, _shard_extent(name, (1, a, b)), full)


def _layer_weights(i):
    j = i // 2
    mixer = [("attn_w_qkv", j), ("attn_w_o", j)] if i % 2 == 0 else [("ssm_w_in", j), ("ssm_w_out", j)]
    return mixer + [("ffn_w_up", i), ("ffn_w_down", i)]


def kernel(x, mix_norm_w, attn_w_qkv, attn_w_o, ssm_w_in, ssm_conv_w, ssm_conv_b, ssm_dt_bias, ssm_a_log, ssm_d, ssm_norm_w, ssm_w_out, ffn_norm_w, ffn_w_up, ffn_conv_w, ffn_conv_b, ffn_w_down, final_norm_w, loss_target, m_mix_norm_w, m_attn_w_qkv, m_attn_w_o, m_ssm_w_in, m_ssm_conv_w, m_ssm_conv_b, m_ssm_dt_bias, m_ssm_a_log, m_ssm_d, m_ssm_norm_w, m_ssm_w_out, m_ffn_norm_w, m_ffn_w_up, m_ffn_conv_w, m_ffn_conv_b, m_ffn_w_down, m_final_norm_w, v_mix_norm_w, v_attn_w_qkv, v_attn_w_o, v_ssm_w_in, v_ssm_conv_w, v_ssm_conv_b, v_ssm_dt_bias, v_ssm_a_log, v_ssm_d, v_ssm_norm_w, v_ssm_w_out, v_ffn_norm_w, v_ffn_w_up, v_ffn_conv_w, v_ffn_conv_b, v_ffn_w_down, v_final_norm_w):
    W = dict(mix_norm_w=mix_norm_w, attn_w_qkv=attn_w_qkv, attn_w_o=attn_w_o, ssm_w_in=ssm_w_in,
             ssm_conv_w=ssm_conv_w, ssm_conv_b=ssm_conv_b, ssm_dt_bias=ssm_dt_bias, ssm_a_log=ssm_a_log,
             ssm_d=ssm_d, ssm_norm_w=ssm_norm_w, ssm_w_out=ssm_w_out, ffn_norm_w=ffn_norm_w, ffn_w_up=ffn_w_up,
             ffn_conv_w=ffn_conv_w, ffn_conv_b=ffn_conv_b, ffn_w_down=ffn_w_down, final_norm_w=final_norm_w)
    M = dict(mix_norm_w=m_mix_norm_w, attn_w_qkv=m_attn_w_qkv, attn_w_o=m_attn_w_o, ssm_w_in=m_ssm_w_in,
             ssm_conv_w=m_ssm_conv_w, ssm_conv_b=m_ssm_conv_b, ssm_dt_bias=m_ssm_dt_bias, ssm_a_log=m_ssm_a_log,
             ssm_d=m_ssm_d, ssm_norm_w=m_ssm_norm_w, ssm_w_out=m_ssm_w_out, ffn_norm_w=m_ffn_norm_w,
             ffn_w_up=m_ffn_w_up, ffn_conv_w=m_ffn_conv_w, ffn_conv_b=m_ffn_conv_b, ffn_w_down=m_ffn_w_down,
             final_norm_w=m_final_norm_w)
    V = dict(mix_norm_w=v_mix_norm_w, attn_w_qkv=v_attn_w_qkv, attn_w_o=v_attn_w_o, ssm_w_in=v_ssm_w_in,
             ssm_conv_w=v_ssm_conv_w, ssm_conv_b=v_ssm_conv_b, ssm_dt_bias=v_ssm_dt_bias, ssm_a_log=v_ssm_a_log,
             ssm_d=v_ssm_d, ssm_norm_w=v_ssm_norm_w, ssm_w_out=v_ssm_w_out, ffn_norm_w=v_ffn_norm_w,
             ffn_w_up=v_ffn_w_up, ffn_conv_w=v_ffn_conv_w, ffn_conv_b=v_ffn_conv_b, ffn_w_down=v_ffn_w_down,
             final_norm_w=v_final_norm_w)

    S, D = x.shape[1], x.shape[2]
    xs = x.reshape(S, D)
    tgt = loss_target.reshape(S, D)
    depth = mix_norm_w.shape[0]
    heads = attn_w_o.shape[1] * N_CHIPS // HEAD_DIM
    AW = heads * HEAD_DIM
    d_inner = ssm_w_out.shape[1] * N_CHIPS
    ssm_heads = d_inner // SSM_HEAD_DIM
    G, P, N = SSM_GROUPS, SSM_HEAD_DIM, SSM_STATE
    gw = d_inner // G
    conv_dim = d_inner + 2 * G * N
    in_w = d_inner + conv_dim + ssm_heads
    in_pad = -(-in_w // 128) * 128
    shard_in = ssm_w_in.shape[2]
    xi, yi = lax.axis_index("x"), lax.axis_index("y")
    chip = 2 * xi + yi

    full = {}

    def land(keys, outs):
        for (n, l), o in zip(keys, outs):
            if n == "ssm_w_in":
                o = jnp.pad(jnp.concatenate([o[k] for k in range(N_CHIPS)], axis=1), ((0, 0), (0, in_pad - in_w)))
            full[(n, l)] = o

    def gather_ex(keys, extra=()):
        return _Exchange("gather", [_gather_item(W[n][l].astype(BF16), n) for n, l in keys] + list(extra))

    def fwd_mm(a, b, name, resid=None):
        return _matmul(a, b, "nn", F32, name, resid=resid)

    sm_names = list(SMALL_SHARDED)
    packed = _pack([W[n] for n in sm_names])
    lw = [_layer_weights(i) for i in range(depth)]
    first = lw[0][:1]
    got = _run_exchange(gather_ex(first, [(packed, None, "lead", 1, (N_CHIPS,) + packed.shape)]), "gather_first")
    land(first, got[:-1])
    batches = [lw[0][1:]] + lw[1:]
    pending = []
    for q, keys in enumerate(batches):
        ex = gather_ex(keys)
        pending.append((keys, ex, _start_exchange(ex, f"gather_start_{q}", after=got[-1])))
    issued = sum(handles[-1][0, 0] for _, _, handles in pending)

    def arrive(q, after):
        keys, ex, handles = pending[q]
        land(keys, _finish_exchange(ex, handles, after, f"gather_wait_{q}"))
    per_chip = [_unpack(got[-1][k], [W[n].shape for n in sm_names]) for k in range(N_CHIPS)]
    for q, n in enumerate(sm_names):
        full[n] = jnp.concatenate([per_chip[k][q] for k in range(N_CHIPS)], axis=SMALL_SHARDED[n])
    tab = _perm_tokens(_rope_table(S))

    def rep_heads(p):
        return jnp.repeat(p, P).reshape(G, 1, gw)

    saved = []
    cur = xs
    for i in range(depth):
        j = i // 2
        if i > 0:
            arrive(i, cur)
        sv = {"x_in": cur}
        h = _rms_fwd(cur, mix_norm_w[i] + issued if i == 0 else mix_norm_w[i], f"mix_norm_fwd_{i}")
        sv["h"] = h
        if i % 2 == 0:
            h = _perm_tokens(h)
            sv["h"] = h
            qkv = fwd_mm(h, full[("attn_w_qkv", j)], f"qkv_fwd_{i}")
            if i == 0:
                arrive(0, qkv)
            og = [_attn_fwd(qkv, tab, g, heads, f"attn_fwd_{i}_{g}") for g in range(3)]
            o, lse = _attn_combine([a for a, _ in og], [b for _, b in og], f"attn_combine_{i}")
            mixed = _unperm_tokens(fwd_mm(o, full[("attn_w_o", j)], f"attn_out_fwd_{i}"))
            sv.update(qkv=qkv, o=o, lse=lse)
        else:
            zx = fwd_mm(h, full[("ssm_w_in", j)], f"ssm_in_fwd_{i}")
            conv = _ssm_conv_fwd(zx, full["ssm_conv_w"][j], full["ssm_conv_b"][j], d_inner, conv_dim,
                                 f"ssm_conv_fwd_{i}")
            dtr = jnp.repeat(zx[:, d_inner + conv_dim:in_w], P, axis=1)
            prm = [rep_heads(p[j]) for p in (ssm_dt_bias, ssm_a_log, ssm_d)]
            y, sprev = _ssd_fwd(conv, dtr, *prm, d_inner, f"ssd_fwd_{i}")
            gated = _gnorm_fwd(y, zx, full["ssm_norm_w"][j], f"ssm_norm_fwd_{i}")
            cur = fwd_mm(gated, full[("ssm_w_out", j)], f"ssm_out_fwd_{i}", resid=cur)
            sv.update(zx=zx, conv=conv, dtr=dtr, prm=prm, y=y, sprev=sprev, gated=gated)
            mixed = None
        if mixed is None:
            h2 = _rms_fwd(cur, ffn_norm_w[i], f"ffn_norm_fwd_{i}")
        else:
            cur, h2 = _rms_fwd(cur, ffn_norm_w[i], f"ffn_norm_fwd_{i}", add=mixed)
        sv["x_mid"] = cur
        up = fwd_mm(h2, full[("ffn_w_up", i)], f"ffn_up_fwd_{i}")
        act = _ffn_conv_fwd(up, full["ffn_conv_w"][i], ffn_conv_b[i], f"ffn_conv_fwd_{i}")
        cur = fwd_mm(act, full[("ffn_w_down", i)], f"ffn_down_fwd_{i}", resid=cur)
        sv.update(h2=h2, up=up, act=act)
        saved.append(sv)

    dx, d_final, loss_part = _loss_head(cur, final_norm_w, tgt, "loss_head")
    gbig, recv = {}, {}
    gs = {n: [None] * W[n].shape[0] for n in SMALL if n != "final_norm_w"}

    def scatter_ex(keys):
        return _Exchange("scatter", [(gbig[(n, l)], BIG_KIND[n], _shard_extent(n, W[n].shape),
                                      (N_CHIPS,) + W[n].shape[1:]) for n, l in keys])

    sib = {}

    def swap_ex(keys):
        return _Exchange("swap", [(recv[k], recv[k].shape) for k in keys])

    def bwd_mm(a, b, mode, dtype, name, send=(), swap=()):
        if not send and not swap:
            return _matmul(a, b, mode, dtype, name)
        parts = ([scatter_ex(send)] if send else []) + ([swap_ex(swap)] if swap else [])
        out, got = _matmul(a, b, mode, dtype, name, carry=_Multi(parts))
        recv.update(zip(send, got[:len(send)]))
        sib.update(zip(swap, got[len(send):]))
        return out

    sending = None
    to_swap = []
    for i in reversed(range(depth)):
        j = i // 2
        sv = saved[i]
        k_in, k_out, k_up, k_down = _layer_weights(i)
        own = i == 0

        def now(keys):
            return keys if own else []
        dact = _matmul(dx, full[k_down], "nt", F32, f"ffn_down_dgrad_{i}")
        gbig[k_down] = bwd_mm(sv["act"], dx, "tn", BF16, f"ffn_down_wgrad_{i}", swap=to_swap)
        to_swap = []
        dup, dcw, dcb = _ffn_conv_bwd(dact, sv["up"], full["ffn_conv_w"][i], ffn_conv_b[i], f"ffn_conv_bwd_{i}")
        gs["ffn_conv_w"][i], gs["ffn_conv_b"][i] = dcw, dcb[0]
        dh2 = bwd_mm(dup, full[k_up], "nt", F32, f"ffn_up_dgrad_{i}", send=now([k_down]))
        gbig[k_up] = bwd_mm(sv["h2"], dup, "tn", BF16, f"ffn_up_wgrad_{i}", swap=now([k_down]))
        dx, dnw = _rms_bwd(sv["x_mid"], ffn_norm_w[i], dh2, dx, f"ffn_norm_bwd_{i}")
        gs["ffn_norm_w"][i] = dnw[0]
        if i % 2 == 0:
            dxp = _perm_tokens(dx)
            do = _matmul(dxp, full[k_out], "nt", F32, f"attn_out_dgrad_{i}")
            gbig[k_out] = _matmul(sv["o"], dxp, "tn", BF16, f"attn_out_wgrad_{i}")
            dqkv = None
            for g in range(3):
                dqkv = _attn_bwd(sv["qkv"], tab, sv["o"], sv["lse"], do, dqkv, g, heads, f"attn_bwd_{i}_{g}")
            gbig[k_in] = bwd_mm(sv["h"], dqkv, "tn", BF16, f"qkv_wgrad_{i}", send=now([k_up, k_out]))
            dh = _unperm_tokens(bwd_mm(dqkv, full[k_in], "nt", F32, f"qkv_dgrad_{i}", send=now([k_in]),
                                       swap=now([k_up, k_out])))
        else:
            dgated = _matmul(dx, full[k_out], "nt", F32, f"ssm_out_dgrad_{i}")
            gbig[k_out] = _matmul(sv["gated"], dx, "tn", BF16, f"ssm_out_wgrad_{i}")
            dy, dz, dgw = _gnorm_bwd(sv["y"], sv["zx"], full["ssm_norm_w"][j], dgated, f"ssm_norm_bwd_{i}")
            gs["ssm_norm_w"][j] = dgw[0]
            dxs_, dbm, dcm, ddtr, dbias, dalog, ddsk = _ssd_bwd(
                sv["conv"], sv["dtr"], *sv["prm"], sv["sprev"], dy, d_inner, f"ssd_bwd_{i}")
            gs["ssm_dt_bias"][j] = dbias.reshape(-1)[::P]
            gs["ssm_a_log"][j] = dalog.reshape(-1)[::P]
            gs["ssm_d"][j] = ddsk.reshape(ssm_heads, P).sum(axis=1)
            ddt = jnp.pad(ddtr[:, ::P], ((0, 0), (0, in_pad - in_w)))
            dzx, dcw, dcb = _ssm_conv_bwd(dxs_, dbm, dcm, sv["conv"], sv["zx"], dz, ddt, full["ssm_conv_w"][j],
                                          d_inner, f"ssm_conv_bwd_{i}")
            gs["ssm_conv_w"][j], gs["ssm_conv_b"][j] = dcw, dcb[0]
            dwin = bwd_mm(sv["h"], dzx, "tn", BF16, f"ssm_in_wgrad_{i}", send=now([k_up, k_out]))
            gbig[k_in] = jnp.stack([dwin[:, k * shard_in:(k + 1) * shard_in] for k in range(N_CHIPS)])
            dh = bwd_mm(dzx, full[k_in], "nt", F32, f"ssm_in_dgrad_{i}", send=now([k_in]), swap=now([k_up, k_out]))
        if sending is not None:
            keys, ex, handles = sending
            recv.update(zip(keys, _finish_exchange(ex, handles, dh, f"scatter_wait_{i + 1}")))
            to_swap, sending = keys, None
        if own:
            to_swap = to_swap + [k_in]
        else:
            keys = [k_in, k_out, k_up, k_down]
            ex = scatter_ex(keys)
            sending = (keys, ex, _start_exchange(ex, f"scatter_start_{i}"))
        issued = 0.0 if own else sending[2][-1][0, 0]
        dx, dnw = _rms_bwd(sv["x_in"], mix_norm_w[i] + issued, dh, dx, f"mix_norm_bwd_{i}")
        gs["mix_norm_w"][i] = dnw[0]
    grad_x = dx.reshape(x.shape)

    sib.update(zip(to_swap, _run_exchange(swap_ex(to_swap), "swap_last")))

    small_full = [jnp.stack(gs[n]) if n != "final_norm_w" else d_final[0] for n in SMALL]
    small_full.append(loss_part[0, 0:1])
    small_shapes = [a.shape for a in small_full]
    summed = _sum_devices(_all_gather_devices(_pack(small_full), "gather_small_grads"), "sum_small_grads")
    small_g = _unpack(summed, small_shapes)
    loss = small_g[-1][0]
    gsm = {}
    for n, g in zip(SMALL, small_g[:-1]):
        if n in SMALL_SHARDED:
            ax = SMALL_SHARDED[n]
            ext = W[n].shape[ax]
            g = lax.dynamic_slice_in_dim(g, chip * ext, ext, axis=ax)
        gsm[n] = g

    out_g, out_d, out_m, out_v = {}, {}, {}, {}
    for name in BIG:
        outs = None
        for l in range(W[name].shape[0]):
            outs = _adamw_big(recv[(name, l)], sib[(name, l)], W[name], M[name], V[name], l, outs,
                              f"adamw_{name}_{l}")
        out_g[name], out_d[name], out_m[name], out_v[name] = outs
    shapes = [W[n].shape for n in SMALL]
    pd, pm, pv = _adamw_small(_pack([gsm[n] for n in SMALL]), _pack([W[n] for n in SMALL]),
                              _pack([M[n] for n in SMALL]), _pack([V[n] for n in SMALL]), "adamw_small")
    for n, d_, m_, v_ in zip(SMALL, _unpack(pd, shapes), _unpack(pm, shapes), _unpack(pv, shapes)):
        out_g[n], out_d[n], out_m[n], out_v[n] = gsm[n], d_, m_, v_

    return (loss, grad_x, *[out_g[n] for n in WEIGHTS], *[out_d[n] for n in WEIGHTS],
            *[out_m[n] for n in WEIGHTS], *[out_v[n] for n in WEIGHTS])
```

```python
import functools
import math

import jax
import jax.numpy as jnp
from jax import lax
from jax.experimental import pallas as pl
from jax.experimental.pallas import tpu as pltpu

F32 = jnp.float32
BF16 = jnp.bfloat16
MESH = pl.DeviceIdType.MESH

NORM_EPS = 1e-5
HEAD_DIM = 128
ATTN_BLOCK = 128
ATTN_DILATIONS = (1, 4, 16)
ATTN_WINDOWS = (128, 512, 2048)
PERM = 16
ROPE_THETA = 500000.0
ROPE_HALF = HEAD_DIM // 8
SSM_HEAD_DIM = 64
SSM_STATE = 128
SSM_GROUPS = 8
SSM_CHUNK = 128
NEG = -1e30

ADAM_LR = 0.001
ADAM_B1 = 0.9
ADAM_B2 = 0.999
ADAM_EPS = 1e-08
ADAM_WD = 0.01
ADAM_STEP = 10

VMEM_LIMIT_BYTES = 48 * 1024 * 1024
N_CHIPS = 4
N_DEV = 8


def _cp(*sem):
    return pltpu.CompilerParams(dimension_semantics=sem, vmem_limit_bytes=VMEM_LIMIT_BYTES)


def _pick(n, pref, mult=128):
    best = None
    t = mult
    while t <= min(n, pref):
        if n % t == 0:
            best = t
        t += mult
    return n if best is None else best


def _sigmoid(x):
    return 1.0 / (1.0 + jnp.exp(-x))


def _silu(x):
    return x * _sigmoid(x)


def _softplus(x):
    u = jnp.exp(-jnp.abs(x))
    w = 1.0 + u
    log1p = jnp.where(w == 1.0, u, jnp.log(w) * (u / jnp.where(w == 1.0, 1.0, w - 1.0)))
    return jnp.maximum(x, 0.0) + log1p


def _dot(a, b, dims):
    return lax.dot_general(a.astype(BF16), b.astype(BF16), (dims, ((), ())),
                           preferred_element_type=F32)


def _dot_nn(a, b):
    return _dot(a, b, ((1,), (0,)))


def _dot_nt(a, b):
    return _dot(a, b, ((1,), (1,)))


def _dot_tn(a, b):
    return _dot(a, b, ((0,), (0,)))


MATMUL_VMEM_BYTES = 36 * 1024 * 1024
MATMUL_TILES = (2048, 1536, 1408, 1152, 1024, 896, 768, 640, 512, 384, 256, 128)


def _matmul_tiles(M, N, K, a_bytes, b_bytes, o_bytes, has_resid):
    best = None
    for tm in [t for t in MATMUL_TILES if M % t == 0] or [M]:
        for tn in [t for t in MATMUL_TILES if N % t == 0] or [N]:
            for tk in [t for t in MATMUL_TILES if K % t == 0 and t <= 1408] or [K]:
                nk, gm, gn = K // tk, M // tm, N // tn
                vmem = 2 * (tm * tk * a_bytes + tk * tn * b_bytes + tm * tn * o_bytes)
                vmem += (2 * tm * tn * 4 if has_resid else 0) + (tm * tn * 4 if nk > 1 else 0)
                if vmem > MATMUL_VMEM_BYTES:
                    continue
                a_reads = 1 if nk == 1 else gn
                b_reads = 1 if (nk == 1 and gn == 1) else gm
                traffic = M * K * a_bytes * a_reads + K * N * b_bytes * b_reads + M * N * o_bytes
                key = (traffic, gm * gn * nk)
                if best is None or key < best[0]:
                    best = (key, (tm, tn, tk))
    assert best is not None, (M, N, K)
    return best[1]


def _matmul(a, b, mode, out_dtype, name, resid=None, carry=None):
    if mode == "nn":
        (M, K), (K2, N) = a.shape, b.shape
    elif mode == "nt":
        (M, K), (N, K2) = a.shape, b.shape
    else:
        (K, M), (K2, N) = a.shape, b.shape
    assert K == K2, (a.shape, b.shape, mode)
    tm, tn, tk = _matmul_tiles(M, N, K, a.dtype.itemsize, b.dtype.itemsize, jnp.dtype(out_dtype).itemsize,
                               resid is not None)
    nk = K // tk
    gm, gn = M // tm, N // tn
    dims = {"nn": ((1,), (0,)), "nt": ((1,), (1,)), "tn": ((0,), (0,))}[mode]
    has_resid = resid is not None
    nci = len(carry.arrays) if carry else 0
    nco = len(carry.out_shapes) if carry else 0

    def body(a_ref, b_ref, *rest):
        r_ref = rest[0] if has_resid else None
        rest = rest[has_resid:]
        c_ins, o_ref, c_outs, scratch = rest[:nci], rest[nci], rest[nci + 1:nci + 1 + nco], rest[nci + 1 + nco:]
        acc_ref = scratch[0] if nk > 1 else None
        sems = scratch[nk > 1:]
        i, j, k = pl.program_id(0), pl.program_id(1), pl.program_id(2)
        if carry:
            @pl.when((i == 0) & (j == 0) & (k == 0))
            def _():
                carry.start(c_ins, c_outs, sems)

        if nk == 1:
            r = _dot(a_ref[...], b_ref[...], dims)
            if has_resid:
                r = r + r_ref[...]
            o_ref[...] = r.astype(o_ref.dtype)
        else:
            @pl.when(k == 0)
            def _():
                acc_ref[...] = jnp.zeros_like(acc_ref)

            acc_ref[...] += _dot(a_ref[...], b_ref[...], dims)

            @pl.when(k == nk - 1)
            def _():
                r = acc_ref[...]
                if has_resid:
                    r = r + r_ref[...]
                o_ref[...] = r.astype(o_ref.dtype)

        if carry:
            @pl.when((i == gm - 1) & (j == gn - 1) & (k == nk - 1))
            def _():
                carry.wait(c_ins, c_outs, sems)

    if mode == "nn":
        a_spec = pl.BlockSpec((tm, tk), lambda i, j, k: (i, k))
        b_spec = pl.BlockSpec((tk, tn), lambda i, j, k: (k, j))
    elif mode == "nt":
        a_spec = pl.BlockSpec((tm, tk), lambda i, j, k: (i, k))
        b_spec = pl.BlockSpec((tn, tk), lambda i, j, k: (j, k))
    else:
        a_spec = pl.BlockSpec((tk, tm), lambda i, j, k: (k, i))
        b_spec = pl.BlockSpec((tk, tn), lambda i, j, k: (k, j))
    o_spec = pl.BlockSpec((tm, tn), lambda i, j, k: (i, j))
    anyspec = pl.BlockSpec(memory_space=pl.ANY)
    in_specs = [a_spec, b_spec] + ([o_spec] if has_resid else []) + [anyspec] * nci
    args = (a, b) + ((resid,) if has_resid else ()) + (tuple(carry.arrays) if carry else ())
    out_shape = [jax.ShapeDtypeStruct((M, N), out_dtype)] + (carry.out_shapes if carry else [])
    scratch = ([] if nk == 1 else [pltpu.VMEM((tm, tn), F32)]) + (carry.scratch if carry else [])
    sem = ("arbitrary",) * 3 if carry else ("parallel", "parallel", "arbitrary")
    outs = pl.pallas_call(
        body, grid=(gm, gn, nk), in_specs=in_specs, out_specs=[o_spec] + [anyspec] * nco,
        out_shape=out_shape, scratch_shapes=scratch, compiler_params=_cp(*sem), name=name)(*args)
    return (outs[0], outs[1:]) if carry else outs[0]


def _rms_fwd(x, w, name, add=None):
    S, D = x.shape
    t = _pick(S, 512, 8)
    has_add = add is not None

    def body(x_ref, w_ref, *rest):
        xv = x_ref[...]
        if has_add:
            xv = xv + rest[0][...]
            rest[1][...] = xv
        r = lax.rsqrt(jnp.mean(xv * xv, axis=-1, keepdims=True) + NORM_EPS)
        rest[-1][...] = (xv * r * w_ref[...]).astype(rest[-1].dtype)

    row = pl.BlockSpec((t, D), lambda i: (i, 0))
    vec = pl.BlockSpec((1, D), lambda i: (0, 0))
    normed = jax.ShapeDtypeStruct((S, D), BF16)
    if not has_add:
        return pl.pallas_call(body, grid=(S // t,), in_specs=[row, vec], out_specs=row, out_shape=normed,
                              compiler_params=_cp("parallel"), name=name)(x, w.reshape(1, D))
    return pl.pallas_call(body, grid=(S // t,), in_specs=[row, vec, row], out_specs=[row, row],
                          out_shape=[jax.ShapeDtypeStruct((S, D), F32), normed],
                          compiler_params=_cp("parallel"), name=name)(x, w.reshape(1, D), add)


def _rms_bwd(x, w, dh, dres, name):
    S, D = x.shape
    t = _pick(S, 512, 8)

    def body(x_ref, w_ref, dh_ref, dr_ref, dx_ref, dw_ref):
        @pl.when(pl.program_id(0) == 0)
        def _():
            dw_ref[...] = jnp.zeros_like(dw_ref)

        xv = x_ref[...]
        r = lax.rsqrt(jnp.mean(xv * xv, axis=-1, keepdims=True) + NORM_EPS)
        xh = xv * r
        dh_v = dh_ref[...]
        g = dh_v * w_ref[...]
        dx_ref[...] = dr_ref[...] + r * (g - xh * jnp.mean(g * xh, axis=-1, keepdims=True))
        dw_ref[...] += jnp.sum(dh_v * xh, axis=0, keepdims=True)

    row = pl.BlockSpec((t, D), lambda i: (i, 0))
    vec = pl.BlockSpec((1, D), lambda i: (0, 0))
    return pl.pallas_call(
        body, grid=(S // t,), in_specs=[row, vec, row, row], out_specs=[row, vec],
        out_shape=[jax.ShapeDtypeStruct((S, D), F32), jax.ShapeDtypeStruct((1, D), F32)],
        compiler_params=_cp("arbitrary"), name=name)(x, w.reshape(1, D), dh, dres)


def _loss_head(x, w, tgt, name):
    S, D = x.shape
    t = _pick(S, 512, 8)

    def body(x_ref, w_ref, t_ref, dx_ref, dw_ref, l_ref):
        @pl.when(pl.program_id(0) == 0)
        def _():
            dw_ref[...] = jnp.zeros_like(dw_ref)
            l_ref[...] = jnp.zeros_like(l_ref)

        xv = x_ref[...]
        wv = w_ref[...]
        r = lax.rsqrt(jnp.mean(xv * xv, axis=-1, keepdims=True) + NORM_EPS)
        xh = xv * r
        err = xh * wv - t_ref[...]
        per_tok = jnp.mean(err * err, axis=-1, keepdims=True)
        l_ref[...] += 0.5 * jnp.sum(per_tok, axis=0, keepdims=True)
        dy = err * (1.0 / D)
        g = dy * wv
        dx_ref[...] = r * (g - xh * jnp.mean(g * xh, axis=-1, keepdims=True))
        dw_ref[...] += jnp.sum(dy * xh, axis=0, keepdims=True)

    row = pl.BlockSpec((t, D), lambda i: (i, 0))
    vec = pl.BlockSpec((1, D), lambda i: (0, 0))
    lspec = pl.BlockSpec((1, 128), lambda i: (0, 0))
    return pl.pallas_call(
        body, grid=(S // t,), in_specs=[row, vec, row], out_specs=[row, vec, lspec],
        out_shape=[jax.ShapeDtypeStruct((S, D), F32), jax.ShapeDtypeStruct((1, D), F32),
                   jax.ShapeDtypeStruct((1, 128), F32)],
        compiler_params=_cp("arbitrary"), name=name)(x, w.reshape(1, D), tgt)


def _rope_table(seq):
    pos = jnp.arange(seq, dtype=F32)
    inv_freq = ROPE_THETA ** (-jnp.arange(0, 2 * ROPE_HALF, 2, dtype=F32) / (2 * ROPE_HALF))
    ang = pos[:, None] * inv_freq[None, :]
    cos, sin = jnp.cos(ang), jnp.sin(ang)
    pad = HEAD_DIM - 2 * ROPE_HALF
    cos_p = jnp.concatenate([cos, cos, jnp.ones((seq, pad), F32)], axis=1)
    sin_a = jnp.concatenate([-sin, jnp.zeros((seq, HEAD_DIM - ROPE_HALF), F32)], axis=1)
    sin_b = jnp.concatenate([jnp.zeros((seq, ROPE_HALF), F32), sin, jnp.zeros((seq, pad), F32)], axis=1)
    return jnp.concatenate([cos_p, sin_a, sin_b], axis=1)


def _rope(t, tab, sign):
    cos_p = tab[:, 0:HEAD_DIM]
    sin_a = tab[:, HEAD_DIM:2 * HEAD_DIM]
    sin_b = tab[:, 2 * HEAD_DIM:3 * HEAD_DIM]
    up = pltpu.roll(t, HEAD_DIM - ROPE_HALF, 1)
    down = pltpu.roll(t, ROPE_HALF, 1)
    return t * cos_p + sign * (up * sin_a + down * sin_b)


def _perm_tokens(a):
    S = a.shape[0]
    return a.reshape(S // PERM, PERM, -1).transpose(1, 0, 2).reshape(S, -1)


def _unperm_tokens(a):
    S = a.shape[0]
    return a.reshape(PERM, S // PERM, -1).transpose(1, 0, 2).reshape(S, -1)


class _Strided:
    def __init__(self, S, dil):
        self.dil, self.m = dil, PERM // dil
        self.c = ATTN_BLOCK // self.m
        self.rows = S // PERM
        self.nb = S // (dil * ATTN_BLOCK)

    def view(self, a):
        return a.reshape(self.m, self.dil, self.rows, a.shape[-1])

    def spec(self, width, col, f=lambda n: n):
        return pl.BlockSpec((self.m, None, self.c, width), lambda r, n: (0, r, f(n), col))

    def load(self, ref, sl=slice(None)):
        if self.m == 1:
            return ref[0, :, sl]
        return jnp.concatenate([ref[q, :, sl] for q in range(self.m)], axis=0)

    def store(self, ref, sl, val):
        for q in range(self.m):
            ref[q, :, sl] = val[q * self.c:(q + 1) * self.c, :]

    def member(self, i):
        shift = self.c.bit_length() - 1
        return (i & (self.c - 1)) * self.m + (i >> shift)


def _attn_fwd(qkv, tab, g, heads, name):
    S = qkv.shape[0]
    W = heads * HEAD_DIM
    dil = ATTN_DILATIONS[g]
    steps = ATTN_WINDOWS[g] // dil
    B = ATTN_BLOCK
    scale = HEAD_DIM ** -0.5
    st = _Strided(S, dil)

    def body(q_ref, k_ref, kp_ref, v_ref, vp_ref, t_ref, tp_ref, o_ref, l_ref):
        n = pl.program_id(1)
        ii = lax.broadcasted_iota(jnp.int32, (B, 2 * B), 0)
        jj = lax.broadcasted_iota(jnp.int32, (B, 2 * B), 1)
        delta = st.member(ii) - st.member(jj & (B - 1)) + jnp.where(jj >= B, 0, B)
        ok = (delta >= 0) & (delta <= steps) & ((jj >= B) | (n > 0))
        tb = st.load(t_ref)
        tpv = st.load(tp_ref)
        sls = [slice(h * HEAD_DIM, (h + 1) * HEAD_DIM) for h in range(heads)]
        qs = [_rope(st.load(q_ref, sl), tb, 1.0) for sl in sls]
        kcs = [jnp.concatenate([_rope(st.load(kp_ref, sl), tpv, 1.0), _rope(st.load(k_ref, sl), tb, 1.0)], axis=0)
               for sl in sls]
        ss = [jnp.where(ok, _dot_nt(q, kc) * scale, NEG) for q, kc in zip(qs, kcs)]
        ms = [jnp.max(s, axis=-1, keepdims=True) for s in ss]
        ps = [jnp.exp(s - m) for s, m in zip(ss, ms)]
        dens = [jnp.sum(p, axis=-1, keepdims=True) for p in ps]
        vcs = [jnp.concatenate([st.load(vp_ref, sl), st.load(v_ref, sl)], axis=0) for sl in sls]
        outs = [_dot_nn(p, vc) for p, vc in zip(ps, vcs)]
        for sl, o, m, den in zip(sls, outs, ms, dens):
            st.store(o_ref, sl, o / den)
            st.store(l_ref, sl, jnp.broadcast_to(m + jnp.log(den), (B, HEAD_DIM)))

    prv = lambda n: jnp.maximum(n - 1, 0)
    qv, tv = st.view(qkv), st.view(tab)
    o_spec = st.spec(W, 0)
    o, lse = pl.pallas_call(
        body, grid=(dil, st.nb),
        in_specs=[st.spec(W, g * 3), st.spec(W, g * 3 + 1), st.spec(W, g * 3 + 1, prv),
                  st.spec(W, g * 3 + 2), st.spec(W, g * 3 + 2, prv),
                  st.spec(3 * HEAD_DIM, 0), st.spec(3 * HEAD_DIM, 0, prv)],
        out_specs=[o_spec, o_spec],
        out_shape=[jax.ShapeDtypeStruct((st.m, dil, st.rows, W), F32)] * 2,
        compiler_params=_cp("parallel", "parallel"), name=name)(qv, qv, qv, qv, qv, tv, tv)
    return o.reshape(S, W), lse.reshape(S, W)


def _attn_combine(os_, ls_, name):
    S, W = os_[0].shape
    t = _pick(S, 256, 8)

    def body(o0, o1, o2, l0, l1, l2, o_ref, l_ref):
        a, b, c = l0[...], l1[...], l2[...]
        m = jnp.maximum(jnp.maximum(a, b), c)
        ea, eb, ec = jnp.exp(a - m), jnp.exp(b - m), jnp.exp(c - m)
        tot = ea + eb + ec
        o_ref[...] = (ea * o0[...] + eb * o1[...] + ec * o2[...]) / tot
        l_ref[...] = m + jnp.log(tot)

    row = pl.BlockSpec((t, W), lambda i: (i, 0))
    return pl.pallas_call(body, grid=(S // t,), in_specs=[row] * 6, out_specs=[row, row],
                          out_shape=[jax.ShapeDtypeStruct((S, W), F32)] * 2,
                          compiler_params=_cp("parallel"), name=name)(*os_, *ls_)


def _attn_bwd(qkv, tab, o, lse, do, dqkv_prev, g, heads, name):
    S = qkv.shape[0]
    W = heads * HEAD_DIM
    dil = ATTN_DILATIONS[g]
    steps = ATTN_WINDOWS[g] // dil
    B = ATTN_BLOCK
    scale = HEAD_DIM ** -0.5
    st = _Strided(S, dil)
    nb = st.nb
    aliased = dqkv_prev is not None

    def body(q_ref, qn_ref, k_ref, kp_ref, v_ref, vp_ref, do_ref, don_ref, o_ref, on_ref,
             l_ref, ln_ref, t_ref, tp_ref, tn_ref, *rest):
        out_ref = rest[-1]
        n = pl.program_id(1)
        has_next = n < nb - 1
        ia = lax.broadcasted_iota(jnp.int32, (B, 2 * B), 0)
        ja = lax.broadcasted_iota(jnp.int32, (B, 2 * B), 1)
        da = st.member(ia) - st.member(ja & (B - 1)) + jnp.where(ja >= B, 0, B)
        ok_a = (da >= 0) & (da <= steps) & ((ja >= B) | (n > 0))
        ib = lax.broadcasted_iota(jnp.int32, (2 * B, B), 0)
        jb = lax.broadcasted_iota(jnp.int32, (2 * B, B), 1)
        db = st.member(ib & (B - 1)) + jnp.where(ib >= B, B, 0) - st.member(jb)
        ok_b = (db >= 0) & (db <= steps) & ((ib < B) | has_next)
        tb, tpv, tnv = st.load(t_ref), st.load(tp_ref), st.load(tn_ref)
        hs = range(heads)
        sls = [slice(h * HEAD_DIM, (h + 1) * HEAD_DIM) for h in hs]
        qr = [_rope(st.load(q_ref, sl), tb, 1.0) for sl in sls]
        qnr = [_rope(st.load(qn_ref, sl), tnv, 1.0) for sl in sls]
        kr = [_rope(st.load(k_ref, sl), tb, 1.0) for sl in sls]
        kpr = [_rope(st.load(kp_ref, sl), tpv, 1.0) for sl in sls]
        v = [st.load(v_ref, sl) for sl in sls]
        dov_ = [st.load(do_ref, sl) for sl in sls]
        donv = [st.load(don_ref, sl) for sl in sls]
        dl = [jnp.sum(dov_[h] * st.load(o_ref, sls[h]), axis=-1, keepdims=True) for h in hs]
        dln = [jnp.sum(donv[h] * st.load(on_ref, sls[h]), axis=-1, keepdims=True) for h in hs]
        ls = [st.load(l_ref, sl) for sl in sls]
        kc = [jnp.concatenate([kpr[h], kr[h]], axis=0) for h in hs]
        vc = [jnp.concatenate([st.load(vp_ref, sls[h]), v[h]], axis=0) for h in hs]
        qc = [jnp.concatenate([qr[h], qnr[h]], axis=0) for h in hs]
        doc = [jnp.concatenate([dov_[h], donv[h]], axis=0) for h in hs]
        lc = [jnp.concatenate([ls[h], st.load(ln_ref, sls[h])], axis=0) for h in hs]
        dlc = [jnp.concatenate([dl[h], dln[h]], axis=0) for h in hs]
        s = [_dot_nt(qr[h], kc[h]) * scale for h in hs]
        dp = [_dot_nt(dov_[h], vc[h]) for h in hs]
        s2 = [_dot_nt(qc[h], kr[h]) * scale for h in hs]
        dp2 = [_dot_nt(doc[h], v[h]) for h in hs]
        p = [jnp.where(ok_a, jnp.exp(jnp.minimum(s[h] - jnp.concatenate([ls[h], ls[h]], axis=1), 30.0)), 0.0)
             for h in hs]
        ds = [p[h] * (dp[h] - dl[h]) * scale for h in hs]
        p2 = [jnp.where(ok_b, jnp.exp(jnp.minimum(s2[h] - lc[h], 30.0)), 0.0) for h in hs]
        ds2 = [p2[h] * (dp2[h] - dlc[h]) * scale for h in hs]
        dq = [_dot_nn(ds[h], kc[h]) for h in hs]
        dk = [_dot_tn(ds2[h], qc[h]) for h in hs]
        dv = [_dot_tn(p2[h], doc[h]) for h in hs]
        for h in hs:
            st.store(out_ref, sls[h], _rope(dq[h], tb, -1.0))
            st.store(out_ref, slice(W + h * HEAD_DIM, W + (h + 1) * HEAD_DIM), _rope(dk[h], tb, -1.0))
            st.store(out_ref, slice(2 * W + h * HEAD_DIM, 2 * W + (h + 1) * HEAD_DIM), dv[h])

    nxt = lambda n: jnp.minimum(n + 1, nb - 1)
    prv = lambda n: jnp.maximum(n - 1, 0)
    same = lambda n: n
    q0, q1, q2, tw = g * 3, g * 3 + 1, g * 3 + 2, 3 * HEAD_DIM
    in_specs = [st.spec(W, q0), st.spec(W, q0, nxt), st.spec(W, q1), st.spec(W, q1, prv),
                st.spec(W, q2), st.spec(W, q2, prv)]
    in_specs += [st.spec(W, 0, f) for f in (same, nxt, same, nxt, same, nxt)]
    in_specs += [st.spec(tw, 0, f) for f in (same, prv, nxt)]
    qv, tv, ov, lv, dov = (st.view(a) for a in (qkv, tab, o, lse, do))
    args = [qv, qv, qv, qv, qv, qv, dov, dov, ov, ov, lv, lv, tv, tv, tv]
    kwargs = {}
    if aliased:
        in_specs.append(pl.BlockSpec(memory_space=pl.ANY))
        args.append(st.view(dqkv_prev))
        kwargs["input_output_aliases"] = {len(args) - 1: 0}
    out = pl.pallas_call(
        body, grid=(dil, nb), in_specs=in_specs, out_specs=st.spec(3 * W, g),
        out_shape=jax.ShapeDtypeStruct((st.m, dil, st.rows, 9 * W), F32),
        compiler_params=_cp("parallel", "parallel"), name=name, **kwargs)(*args)
    return out.reshape(S, 9 * W)


def _shift_down(x, halo, s):
    if s == 0:
        return x
    T = x.shape[0]
    xs = pltpu.roll(x, s, 0)
    hs = pltpu.roll(halo, s, 0)
    row8 = lax.broadcasted_iota(jnp.int32, hs.shape, 0)
    top = jnp.where(row8 < s, hs, xs[0:8])
    return top if T == 8 else jnp.concatenate([top, xs[8:T]], axis=0)


def _shift_up(x, halo, s):
    if s == 0:
        return x
    T = x.shape[0]
    xs = pltpu.roll(x, T - s, 0)
    hs = pltpu.roll(halo, 8 - s, 0)
    row8 = lax.broadcasted_iota(jnp.int32, hs.shape, 0)
    bot = jnp.where(row8 >= 8 - s, hs, xs[T - 8:T])
    return jnp.concatenate([xs[0:T - 8], bot], axis=0)


CONV_ROWS = 128
CONV_LANES = 512


def _conv_apply(x, halo, w_ref, wsl, b, K):
    acc = x * w_ref[K - 1, :, wsl] + b
    for s in range(1, K):
        acc = acc + _shift_down(x, halo, s) * w_ref[K - 1 - s, :, wsl]
    return acc


def _conv_accum(dy, dyn, xv, w_ref, dw_ref, db_ref, wsl, K):
    acc = dy * w_ref[K - 1, :, wsl]
    dw_ref[K - 1, :, wsl] += jnp.sum(dy * xv, axis=0, keepdims=True)
    for s in range(1, K):
        ahead = _shift_up(dy, dyn, s)
        acc = acc + ahead * w_ref[K - 1 - s, :, wsl]
        dw_ref[K - 1 - s, :, wsl] += jnp.sum(ahead * xv, axis=0, keepdims=True)
    db_ref[:, wsl] += jnp.sum(dy, axis=0, keepdims=True)
    return acc


def _row_specs(T, S, width):
    main = pl.BlockSpec((T, width), lambda i: (i, 0))
    prev = pl.BlockSpec((8, width), lambda i: (jnp.maximum(i * (T // 8) - 1, 0), 0))
    nxt = pl.BlockSpec((8, width), lambda i: (jnp.minimum((i + 1) * (T // 8), S // 8 - 1), 0))
    return main, prev, nxt


def _full(shape):
    return pl.BlockSpec(shape, lambda i: (0,) * len(shape))


def _silu_grad(y):
    sg = _sigmoid(y)
    return sg * (1.0 + y * (1.0 - sg))


def _ssm_conv_fwd(zx, w, b, d_inner, conv_dim, name):
    S, wz = zx.shape
    K = w.shape[0]
    T = _pick(S, CONV_ROWS, 8)
    cw = _pick(conv_dim, CONV_LANES)

    def body(x_ref, h_ref, w_ref, b_ref, c_ref):
        has_prev = pl.program_id(0) > 0
        for cs in range(0, conv_dim, cw):
            so, sx = slice(cs, cs + cw), slice(d_inner + cs, d_inner + cs + cw)
            halo = jnp.where(has_prev, h_ref[:, sx], 0.0)
            c_ref[:, so] = _conv_apply(x_ref[:, sx], halo, w_ref, so, b_ref[:, so], K)

    main, prev, _ = _row_specs(T, S, wz)
    return pl.pallas_call(
        body, grid=(S // T,), in_specs=[main, prev, _full((K, 1, conv_dim)), _full((1, conv_dim))],
        out_specs=pl.BlockSpec((T, conv_dim), lambda i: (i, 0)),
        out_shape=jax.ShapeDtypeStruct((S, conv_dim), F32),
        compiler_params=_cp("parallel"), name=name)(zx, zx, w.reshape(K, 1, conv_dim), b.reshape(1, conv_dim))


def _ssm_conv_bwd(dxs, dbm, dcm, conv, zx, dz, ddt, w, d_inner, name):
    S, wz = zx.shape
    K, conv_dim = w.shape
    gn = dbm.shape[1]
    T = _pick(S, CONV_ROWS, 8)
    cw = _pick(math.gcd(d_inner, gn), CONV_LANES)
    nrow = S // T
    tail = wz - d_inner - conv_dim
    assert ddt.shape[1] == tail

    def body(dx_ref, dxn_ref, db_ref_, dbn_ref, dc_ref, dcn_ref, y_ref, yn_ref, x_ref, dz_ref, ddt_ref,
             w_ref, o_ref, dw_ref, dbias_ref):
        i = pl.program_id(0)

        @pl.when(i == 0)
        def _():
            dw_ref[...] = jnp.zeros_like(dw_ref)
            dbias_ref[...] = jnp.zeros_like(dbias_ref)

        has_prev, has_next = i > 0, i < nrow - 1
        for cs in range(0, d_inner, cw):
            o_ref[:, cs:cs + cw] = dz_ref[:, cs:cs + cw].astype(o_ref.dtype)
        o_ref[:, d_inner + conv_dim:wz] = ddt_ref[...].astype(o_ref.dtype)
        for cs in range(0, conv_dim, cw):
            so, sx = slice(cs, cs + cw), slice(d_inner + cs, d_inner + cs + cw)
            if cs < d_inner:
                src, srcn, ss = dx_ref, dxn_ref, slice(cs, cs + cw)
            elif cs < d_inner + gn:
                src, srcn, ss = db_ref_, dbn_ref, slice(cs - d_inner, cs - d_inner + cw)
            else:
                src, srcn, ss = dc_ref, dcn_ref, slice(cs - d_inner - gn, cs - d_inner - gn + cw)
            dy = src[:, ss] * _silu_grad(y_ref[:, so])
            dyn = jnp.where(has_next, srcn[:, ss] * _silu_grad(yn_ref[:, so]), 0.0)
            o_ref[:, sx] = _conv_accum(dy, dyn, x_ref[:, sx], w_ref, dw_ref, dbias_ref, so, K).astype(o_ref.dtype)

    xm, _, xn = _row_specs(T, S, d_inner)
    gm, _, gnx = _row_specs(T, S, gn)
    cm, _, cn = _row_specs(T, S, conv_dim)
    zm, _, _ = _row_specs(T, S, wz)
    tm_, _, _ = _row_specs(T, S, tail)
    dzx, dw, db = pl.pallas_call(
        body, grid=(nrow,),
        in_specs=[xm, xn, gm, gnx, gm, gnx, cm, cn, zm, xm, tm_, _full((K, 1, conv_dim))],
        out_specs=[zm, _full((K, 1, conv_dim)), _full((1, conv_dim))],
        out_shape=[jax.ShapeDtypeStruct((S, wz), BF16), jax.ShapeDtypeStruct((K, 1, conv_dim), F32),
                   jax.ShapeDtypeStruct((1, conv_dim), F32)],
        compiler_params=_cp("arbitrary"), name=name)(
            dxs, dxs, dbm, dbm, dcm, dcm, conv, conv, zx, dz, ddt, w.reshape(K, 1, conv_dim))
    return dzx, dw.reshape(K, conv_dim), db


def _ffn_conv_fwd(up, w, b, name):
    S, C = up.shape
    F = C // 2
    K = w.shape[0]
    T = _pick(S, CONV_ROWS, 8)
    cw = _pick(F, CONV_LANES)

    def body(x_ref, h_ref, w_ref, b_ref, a_ref):
        has_prev = pl.program_id(0) > 0
        for cs in range(0, F, cw):
            sg, su = slice(cs, cs + cw), slice(F + cs, F + cs + cw)
            gate = _conv_apply(x_ref[:, sg], jnp.where(has_prev, h_ref[:, sg], 0.0), w_ref, sg, b_ref[:, sg], K)
            upv = _conv_apply(x_ref[:, su], jnp.where(has_prev, h_ref[:, su], 0.0), w_ref, su, b_ref[:, su], K)
            a_ref[:, sg] = (gate * _sigmoid(gate) * upv).astype(a_ref.dtype)

    main, prev, _ = _row_specs(T, S, C)
    return pl.pallas_call(
        body, grid=(S // T,), in_specs=[main, prev, _full((K, 1, C)), _full((1, C))],
        out_specs=pl.BlockSpec((T, F), lambda i: (i, 0)), out_shape=jax.ShapeDtypeStruct((S, F), BF16),
        compiler_params=_cp("parallel"), name=name)(up, up, w.reshape(K, 1, C), b.reshape(1, C))


def _ffn_conv_bwd(dact, up, w, b, name):
    S, C = up.shape
    F = C // 2
    K = w.shape[0]
    T = _pick(S, CONV_ROWS, 8)
    cw = _pick(F, CONV_LANES)
    nrow = S // T

    def du(gate, upv, d):
        sg = _sigmoid(gate)
        return d * upv * sg * (1.0 + gate * (1.0 - sg)), d * gate * sg

    def body(d_ref, dn_ref, x_ref, xp_ref, xn_ref, w_ref, b_ref, dx_ref, dw_ref, db_ref):
        i = pl.program_id(0)

        @pl.when(i == 0)
        def _():
            dw_ref[...] = jnp.zeros_like(dw_ref)
            db_ref[...] = jnp.zeros_like(db_ref)

        has_prev, has_next = i > 0, i < nrow - 1
        for cs in range(0, F, cw):
            sf = slice(cs, cs + cw)
            cols = [slice(half * F + cs, half * F + cs + cw) for half in range(2)]
            xs = [x_ref[:, sc] for sc in cols]
            xps = [jnp.where(has_prev, xp_ref[:, sc], 0.0) for sc in cols]
            u = [_conv_apply(xs[q], xps[q], w_ref, cols[q], b_ref[:, cols[q]], K) for q in range(2)]
            un = [_conv_apply(xn_ref[:, cols[q]], xs[q][T - 8:T], w_ref, cols[q], b_ref[:, cols[q]], K)
                  for q in range(2)]
            dys = du(u[0], u[1], d_ref[:, sf])
            dyns = du(un[0], un[1], dn_ref[:, sf])
            for q in range(2):
                dyn = jnp.where(has_next, dyns[q], 0.0)
                dx_ref[:, cols[q]] = _conv_accum(dys[q], dyn, xs[q], w_ref, dw_ref, db_ref, cols[q],
                                                 K).astype(dx_ref.dtype)

    am, _, an = _row_specs(T, S, F)
    xm, xp_, xn_ = _row_specs(T, S, C)
    dx, dw, db = pl.pallas_call(
        body, grid=(nrow,), in_specs=[am, an, xm, xp_, xn_, _full((K, 1, C)), _full((1, C))],
        out_specs=[xm, _full((K, 1, C)), _full((1, C))],
        out_shape=[jax.ShapeDtypeStruct((S, C), BF16), jax.ShapeDtypeStruct((K, 1, C), F32),
                   jax.ShapeDtypeStruct((1, C), F32)],
        compiler_params=_cp("arbitrary"), name=name)(dact, dact, up, up, up, w.reshape(K, 1, C), b.reshape(1, C))
    return dx, dw.reshape(K, C), db


def _cumsum_rows(v):
    n = v.shape[0]
    row = lax.broadcasted_iota(jnp.int32, v.shape, 0)
    k = 1
    while k < n:
        v = v + jnp.where(row >= k, pltpu.roll(v, k, 0), 0.0)
        k *= 2
    return v


def _rev_cumsum_rows(v):
    n = v.shape[0]
    row = lax.broadcasted_iota(jnp.int32, v.shape, 0)
    k = 1
    while k < n:
        v = v + jnp.where(row < n - k, pltpu.roll(v, n - k, 0), 0.0)
        k *= 2
    return v


def _ssd_common(x_ref, dtr_ref, bias_ref, alog_ref, gw):
    Q = SSM_CHUNK
    X = _silu(x_ref[...])
    pre = dtr_ref[...] + bias_ref[...]
    dt = _softplus(pre)
    a = -jnp.exp(alog_ref[...])
    cs = _cumsum_rows(dt * a)
    row = lax.broadcasted_iota(jnp.int32, (Q, gw), 0)
    cs_last = jnp.sum(jnp.where(row == Q - 1, cs, 0.0), axis=0, keepdims=True)
    return X, pre, dt, a, cs, cs_last, row


def _head_decay(cs, head_mask):
    Q = SSM_CHUNK
    col = jnp.max(jnp.where(head_mask, cs, NEG), axis=1, keepdims=True)
    acol = jnp.broadcast_to(col, (Q, Q))
    arow = acol.T
    ii = lax.broadcasted_iota(jnp.int32, (Q, Q), 0)
    jj = lax.broadcasted_iota(jnp.int32, (Q, Q), 1)
    tril = ii >= jj
    return jnp.where(tril, jnp.exp(jnp.where(tril, acol - arow, 0.0)), 0.0), tril


def _ssd_specs(S, d_inner, gw, nc, rev):
    Q, N, G = SSM_CHUNK, SSM_STATE, SSM_GROUPS
    ch = (lambda c: nc - 1 - c) if rev else (lambda c: c)
    x_spec = pl.BlockSpec((Q, gw), lambda g, c: (ch(c), g))
    b_spec = pl.BlockSpec((Q, N), lambda g, c: (ch(c), d_inner // N + g))
    c_spec = pl.BlockSpec((Q, N), lambda g, c: (ch(c), d_inner // N + G + g))
    p_spec = pl.BlockSpec((None, 1, gw), lambda g, c: (g, 0, 0))
    s_spec = pl.BlockSpec((None, None, gw, N), lambda g, c: (ch(c), g, 0, 0))
    return x_spec, b_spec, c_spec, p_spec, s_spec


def _ssd_fwd(xbc, dtr, bias, alog, dsk, d_inner, name):
    S = xbc.shape[0]
    Q, N, G, P = SSM_CHUNK, SSM_STATE, SSM_GROUPS, SSM_HEAD_DIM
    gw = d_inner // G
    R = gw // P
    nc = S // Q

    def body(x_ref, b_ref, c_ref, dtr_ref, bias_ref, alog_ref, d_ref, y_ref, sp_ref, s_scr):
        @pl.when(pl.program_id(1) == 0)
        def _():
            s_scr[...] = jnp.zeros_like(s_scr)

        X, _, dt, a, cs, cs_last, row = _ssd_common(x_ref, dtr_ref, bias_ref, alog_ref, gw)
        Bm, Cm = _silu(b_ref[...]), _silu(c_ref[...])
        xdt = X * dt
        lane = lax.broadcasted_iota(jnp.int32, (Q, gw), 1)
        sprev = s_scr[...]
        sp_ref[...] = sprev
        cb = _dot_nt(Cm, Bm)
        y = jnp.exp(cs) * _dot_nt(Cm, sprev)
        hms = [(lane >= r * P) & (lane < (r + 1) * P) for r in range(R)]
        dec_ls = [_head_decay(cs, hm)[0] for hm in hms]
        for part in [_dot_nn(cb * dec_l, jnp.where(hm, xdt, 0.0)) for dec_l, hm in zip(dec_ls, hms)]:
            y = y + part
        dec = jnp.exp(cs_last - cs)
        cd = jnp.exp(jnp.broadcast_to(cs_last, (Q, gw)).T)
        s_scr[...] = sprev * cd + _dot_tn(xdt * dec, Bm)
        y_ref[...] = y + d_ref[...] * X

    x_spec, b_spec, c_spec, p_spec, s_spec = _ssd_specs(S, d_inner, gw, nc, False)
    return pl.pallas_call(
        body, grid=(G, nc), in_specs=[x_spec, b_spec, c_spec, x_spec, p_spec, p_spec, p_spec],
        out_specs=[x_spec, s_spec],
        out_shape=[jax.ShapeDtypeStruct((S, d_inner), F32), jax.ShapeDtypeStruct((nc, G, gw, N), F32)],
        scratch_shapes=[pltpu.VMEM((gw, N), F32)],
        compiler_params=_cp("parallel", "arbitrary"), name=name)(xbc, xbc, xbc, dtr, bias, alog, dsk)


def _ssd_bwd(xbc, dtr, bias, alog, dsk, sprev_all, dy, d_inner, name):
    S = xbc.shape[0]
    Q, N, G, P = SSM_CHUNK, SSM_STATE, SSM_GROUPS, SSM_HEAD_DIM
    gw = d_inner // G
    R = gw // P
    nc = S // Q

    def body(x_ref, b_ref, c_ref, dtr_ref, bias_ref, alog_ref, d_ref, sp_ref, dy_ref,
             dx_ref, db_ref, dc_ref, ddt_ref, dbias_ref, dalog_ref, dd_ref, ds_scr):
        @pl.when(pl.program_id(1) == 0)
        def _():
            ds_scr[...] = jnp.zeros_like(ds_scr)
            dbias_ref[...] = jnp.zeros_like(dbias_ref)
            dalog_ref[...] = jnp.zeros_like(dalog_ref)
            dd_ref[...] = jnp.zeros_like(dd_ref)

        X, pre, dt, a, cs, cs_last, row = _ssd_common(x_ref, dtr_ref, bias_ref, alog_ref, gw)
        Bm, Cm = _silu(b_ref[...]), _silu(c_ref[...])
        dY = dy_ref[...]
        sprev = sp_ref[...]
        dsn = ds_scr[...]
        xdt = X * dt
        lane = lax.broadcasted_iota(jnp.int32, (Q, gw), 1)
        lane1 = lax.broadcasted_iota(jnp.int32, (1, gw), 1)
        srow = lax.broadcasted_iota(jnp.int32, (gw, N), 0)
        ecs = jnp.exp(cs)
        dec = jnp.exp(cs_last - cs)
        cd = jnp.exp(jnp.broadcast_to(cs_last, (Q, gw)).T)
        dd_ref[...] += jnp.sum(dY * X, axis=0, keepdims=True)
        dX = d_ref[...] * dY
        ey = ecs * dY
        dcs = ey * _dot_nt(Cm, sprev)
        dC = _dot_nn(ey, sprev)
        ds_scr[...] = cd * dsn + _dot_tn(ey, Cm)
        wmat = _dot_nt(Bm, dsn)
        dxdt = dec * wmat
        xd = xdt * dec
        dB = _dot_nn(xd, dsn)
        ddec = xdt * wmat * dec
        dcs = dcs - ddec
        dlast = jnp.sum(ddec, axis=0, keepdims=True)
        qmat = dsn * sprev * cd
        cb = _dot_nt(Cm, Bm)
        dcb = jnp.zeros((Q, Q), F32)
        dcs_rep = jnp.zeros((Q, gw), F32)
        dtx_rep = jnp.zeros((Q, gw), F32)
        hms = [(lane >= r * P) & (lane < (r + 1) * P) for r in range(R)]
        decs = [_head_decay(cs, hm) for hm in hms]
        dyrs = [jnp.where(hm, dY, 0.0) for hm in hms]
        graws = [_dot_nt(dyr, xdt) for dyr in dyrs]
        backs = [_dot_tn(cb * dec_l, dyr) for (dec_l, _), dyr in zip(decs, dyrs)]
        for r in range(R):
            hm, (dec_l, tril) = hms[r], decs[r]
            gmat = jnp.where(tril, graws[r], 0.0)
            dcb = dcb + gmat * dec_l
            e = gmat * cb * dec_l
            v = (jnp.sum(e, axis=1, keepdims=True) - jnp.sum(e.T, axis=1, keepdims=True)
                 + jnp.sum(jnp.where(hm, dcs, 0.0), axis=1, keepdims=True))
            dxdt = dxdt + backs[r]
            hm1 = (lane1 >= r * P) & (lane1 < (r + 1) * P)
            t_last = (jnp.sum(jnp.where(hm1, dlast, 0.0), axis=1, keepdims=True)
                      + jnp.sum(jnp.where((srow >= r * P) & (srow < (r + 1) * P), qmat, 0.0), keepdims=True))
            dcs_rep = dcs_rep + jnp.where(hm, v, 0.0) + jnp.where(hm & (row == Q - 1), t_last, 0.0)
        for r in range(R):
            hm = (lane >= r * P) & (lane < (r + 1) * P)
            w_r = jnp.sum(jnp.where(hm, dxdt * X, 0.0), axis=1, keepdims=True)
            dtx_rep = dtx_rep + jnp.where(hm, w_r, 0.0)
        dadt = _rev_cumsum_rows(dcs_rep)
        ddt = a * dadt + dtx_rep
        dalog_ref[...] += jnp.sum(dt * dadt, axis=0, keepdims=True) * a
        draw = ddt * _sigmoid(pre)
        ddt_ref[...] = draw
        dbias_ref[...] += jnp.sum(draw, axis=0, keepdims=True)
        dx_ref[...] = dX + dxdt * dt
        db_ref[...] = dB + _dot_tn(dcb, Cm)
        dc_ref[...] = dC + _dot_nn(dcb, Bm)

    x_spec, b_spec, c_spec, p_spec, s_spec = _ssd_specs(S, d_inner, gw, nc, True)
    n_spec = pl.BlockSpec((Q, N), lambda g, c: (nc - 1 - c, g))
    gshape = jax.ShapeDtypeStruct((G, 1, gw), F32)
    return pl.pallas_call(
        body, grid=(G, nc),
        in_specs=[x_spec, b_spec, c_spec, x_spec, p_spec, p_spec, p_spec, s_spec, x_spec],
        out_specs=[x_spec, n_spec, n_spec, x_spec, p_spec, p_spec, p_spec],
        out_shape=[jax.ShapeDtypeStruct((S, d_inner), F32), jax.ShapeDtypeStruct((S, G * N), F32),
                   jax.ShapeDtypeStruct((S, G * N), F32), jax.ShapeDtypeStruct((S, d_inner), F32),
                   gshape, gshape, gshape],
        scratch_shapes=[pltpu.VMEM((gw, N), F32)],
        compiler_params=_cp("parallel", "arbitrary"), name=name)(
            xbc, xbc, xbc, dtr, bias, alog, dsk, sprev_all, dy)


def _gnorm_fwd(y, zx, w, name):
    S, d_inner = y.shape
    G = SSM_GROUPS
    gw = d_inner // G
    T = _pick(S, 256, 8)

    def body(y_ref, z_ref, w_ref, o_ref):
        for k in range(G):
            sl = slice(k * gw, (k + 1) * gw)
            z = z_ref[:, sl]
            gk = y_ref[:, sl] * z * _sigmoid(z)
            r = lax.rsqrt(jnp.mean(gk * gk, axis=-1, keepdims=True) + NORM_EPS)
            o_ref[:, sl] = (gk * r * w_ref[:, sl]).astype(o_ref.dtype)

    row = pl.BlockSpec((T, d_inner), lambda i: (i, 0))
    vec = pl.BlockSpec((1, d_inner), lambda i: (0, 0))
    return pl.pallas_call(body, grid=(S // T,), in_specs=[row, row, vec], out_specs=row,
                          out_shape=jax.ShapeDtypeStruct((S, d_inner), BF16),
                          compiler_params=_cp("parallel"), name=name)(y, zx, w.reshape(1, d_inner))


def _gnorm_bwd(y, zx, w, dout, name):
    S, d_inner = y.shape
    G = SSM_GROUPS
    gw = d_inner // G
    T = _pick(S, 256, 8)

    def body(y_ref, z_ref, w_ref, d_ref, dy_ref, dz_ref, dw_ref):
        @pl.when(pl.program_id(0) == 0)
        def _():
            dw_ref[...] = jnp.zeros_like(dw_ref)

        for k in range(G):
            sl = slice(k * gw, (k + 1) * gw)
            z, yv, d = z_ref[:, sl], y_ref[:, sl], d_ref[:, sl]
            sg = _sigmoid(z)
            sz = z * sg
            gk = yv * sz
            r = lax.rsqrt(jnp.mean(gk * gk, axis=-1, keepdims=True) + NORM_EPS)
            gh = gk * r
            dw_ref[:, sl] += jnp.sum(d * gh, axis=0, keepdims=True)
            dg = d * w_ref[:, sl]
            dgk = r * (dg - gh * jnp.mean(dg * gh, axis=-1, keepdims=True))
            dy_ref[:, sl] = dgk * sz
            dz_ref[:, sl] = dgk * yv * sg * (1.0 + z * (1.0 - sg))

    row = pl.BlockSpec((T, d_inner), lambda i: (i, 0))
    vec = pl.BlockSpec((1, d_inner), lambda i: (0, 0))
    return pl.pallas_call(
        body, grid=(S // T,), in_specs=[row, row, vec, row], out_specs=[row, row, vec],
        out_shape=[jax.ShapeDtypeStruct((S, d_inner), F32)] * 2 + [jax.ShapeDtypeStruct((1, d_inner), F32)],
        compiler_params=_cp("arbitrary"), name=name)(y, zx, w.reshape(1, d_inner), dout)


def _adam_math(g, w, m, v):
    m = ADAM_B1 * m + (1.0 - ADAM_B1) * g
    v = ADAM_B2 * v + (1.0 - ADAM_B2) * (g * g)
    m_hat = m / (1.0 - ADAM_B1 ** ADAM_STEP)
    v_hat = v / (1.0 - ADAM_B2 ** ADAM_STEP)
    delta = -ADAM_LR * (m_hat / (jnp.sqrt(v_hat) + ADAM_EPS) + ADAM_WD * w)
    return delta, m, v


def _adamw_big(own, sib, w, m, v, layer, prev, name):
    L, A, Bc = w.shape
    T = _pick(A, max(8, (1 << 19) // (4 * Bc)), 16)

    def body(o_ref, s_ref, w_ref, m_ref, v_ref, *rest):
        g_ref, d_ref, nm_ref, nv_ref = rest[-4:]
        so = o_ref[0].astype(F32)
        ss = s_ref[0].astype(F32)
        for k in range(1, N_CHIPS):
            so = so + o_ref[k].astype(F32)
            ss = ss + s_ref[k].astype(F32)
        g = so + ss
        delta, nm, nv = _adam_math(g, w_ref[...], m_ref[...], v_ref[...])
        g_ref[...] = g
        d_ref[...] = delta
        nm_ref[...] = nm
        nv_ref[...] = nv

    part = pl.BlockSpec((N_CHIPS, T, Bc), lambda i: (0, i, 0))
    blk = pl.BlockSpec((None, T, Bc), lambda i: (layer, i, 0))
    shp = jax.ShapeDtypeStruct(w.shape, F32)
    in_specs, args, kwargs = [part, part, blk, blk, blk], [own, sib, w, m, v], {}
    if prev is not None:
        in_specs += [pl.BlockSpec(memory_space=pl.ANY)] * 4
        args += list(prev)
        kwargs["input_output_aliases"] = {5 + q: q for q in range(4)}
    return pl.pallas_call(body, grid=(A // T,), in_specs=in_specs, out_specs=[blk] * 4, out_shape=[shp] * 4,
                          compiler_params=_cp("parallel"), name=name, **kwargs)(*args)


def _sum_devices(parts, name):
    _, R, C = parts.shape

    def body(p_ref, o_ref):
        acc = p_ref[0]
        for k in range(1, N_DEV):
            acc = acc + p_ref[k]
        o_ref[...] = acc

    return pl.pallas_call(body, out_shape=jax.ShapeDtypeStruct((R, C), F32), name=name)(parts)


def _adamw_small(g, w, m, v, name):
    def body(g_ref, w_ref, m_ref, v_ref, d_ref, nm_ref, nv_ref):
        delta, nm, nv = _adam_math(g_ref[...], w_ref[...], m_ref[...], v_ref[...])
        d_ref[...] = delta
        nm_ref[...] = nm
        nv_ref[...] = nv

    shp = jax.ShapeDtypeStruct(g.shape, F32)
    return pl.pallas_call(body, out_shape=[shp] * 3, name=name)(g, w, m, v)


PACK_COLS = 1024


def _pack(arrs):
    flat = jnp.concatenate([a.reshape(-1).astype(F32) for a in arrs])
    n = flat.shape[0]
    rows = -(-n // (8 * PACK_COLS)) * 8
    return jnp.pad(flat, (0, rows * PACK_COLS - n)).reshape(rows, PACK_COLS)


def _unpack(packed, shapes):
    flat = packed.reshape(-1)
    out, off = [], 0
    for s in shapes:
        n = math.prod(s)
        out.append(flat[off:off + n].reshape(s))
        off += n
    return out


def _shard_ref(ref, kind, k, n):
    if kind == "col":
        return ref.at[:, pl.ds(pl.multiple_of(k * n, 128), n)]
    if kind == "row":
        return ref.at[pl.ds(pl.multiple_of(k * n, 16), n), :]
    return ref.at[k]


def _chip_peers():
    x, y, c = lax.axis_index("x"), lax.axis_index("y"), lax.axis_index("c")
    return x, y, c, [(1 - x, y), (x, 1 - y), (1 - x, 1 - y)]


class _Exchange:
    def __init__(self, mode, items):
        self.mode, self.items = mode, items
        self.arrays = []
        for it in items:
            if not any(it[0] is a for a in self.arrays):
                self.arrays.append(it[0])
        self.src_idx = [next(i for i, a in enumerate(self.arrays) if a is it[0]) for it in items]
        self.out_shapes = [jax.ShapeDtypeStruct(it[-1], it[0].dtype) for it in items]
        n = len(items)
        if mode == "swap":
            self.scratch = [pltpu.SemaphoreType.DMA((n,)), pltpu.SemaphoreType.DMA((n,))]
        else:
            self.scratch = [pltpu.SemaphoreType.DMA((3 * n,)), pltpu.SemaphoreType.DMA((3 * n,)),
                            pltpu.SemaphoreType.DMA((n,))]

    def _copies(self, ins, outs, sems):
        if self.mode == "swap":
            send_sems, recv_sems = sems
            x, y, c = lax.axis_index("x"), lax.axis_index("y"), lax.axis_index("c")
            sent = [pltpu.make_async_remote_copy(
                src_ref=ins[self.src_idx[t]], dst_ref=outs[t], send_sem=send_sems.at[t], recv_sem=recv_sems.at[t],
                device_id=(x, y, 1 - c), device_id_type=MESH) for t in range(len(self.items))]
            return [], sent, sent
        send_sems, recv_sems, loc_sems = sems
        x, y, c, peers = _chip_peers()
        me = 2 * x + y
        local, sent, arriving = [], [], []
        for t, it in enumerate(self.items):
            src_arr = ins[self.src_idx[t]]
            if self.mode == "gather":
                _, layer, kind, n, _ = it
                src = src_arr if layer is None else src_arr.at[layer]
                src_for = lambda k: src
                dst_from = lambda k: _shard_ref(outs[t], kind, k, n)
            else:
                _, kind, n, _ = it
                src_for = lambda k: _shard_ref(src_arr, kind, k, n)
                dst_from = lambda k: outs[t].at[k]
            local.append(pltpu.make_async_copy(src_for(me), dst_from(me), loc_sems.at[t]))
            for j, (px, py) in enumerate(peers):
                pk = 2 * px + py
                args = dict(send_sem=send_sems.at[3 * t + j], recv_sem=recv_sems.at[3 * t + j],
                            device_id=(px, py, c), device_id_type=MESH)
                sent.append(pltpu.make_async_remote_copy(src_ref=src_for(pk), dst_ref=dst_from(me), **args))
                arriving.append(pltpu.make_async_remote_copy(src_ref=src_for(pk), dst_ref=dst_from(pk), **args))
        return local, sent, arriving

    def start(self, ins, outs, sems):
        local, sent, arriving = self._copies(ins, outs, sems)
        for cp in local + sent:
            cp.start()
        for cp in arriving:
            cp._used = True

    def wait(self, ins, outs, sems):
        local, sent, arriving = self._copies(ins, outs, sems)
        for cp in arriving:
            cp.wait_recv()
        for cp in sent:
            cp.wait_send()
        for cp in local:
            cp.wait()


class _Multi:
    def __init__(self, parts):
        self.parts = parts
        self.arrays = [a for p in parts for a in p.arrays]
        self.out_shapes = [s for p in parts for s in p.out_shapes]
        self.scratch = [s for p in parts for s in p.scratch]

    def _split(self, ins, outs, sems):
        i = o = s = 0
        for p in self.parts:
            ni, no, ns = len(p.arrays), len(p.out_shapes), len(p.scratch)
            yield p, ins[i:i + ni], outs[o:o + no], sems[s:s + ns]
            i, o, s = i + ni, o + no, s + ns

    def start(self, ins, outs, sems):
        for p, a, b, c in self._split(ins, outs, sems):
            p.start(a, b, c)

    def wait(self, ins, outs, sems):
        for p, a, b, c in self._split(ins, outs, sems):
            p.wait(a, b, c)


def _run_exchange(ex, name):
    nin, nout = len(ex.arrays), len(ex.out_shapes)

    def body(*refs):
        ins, outs, sems = refs[:nin], refs[nin:nin + nout], refs[nin + nout:]
        ex.start(ins, outs, sems)
        ex.wait(ins, outs, sems)

    anyspec = pl.BlockSpec(memory_space=pl.ANY)
    return pl.pallas_call(body, in_specs=[anyspec] * nin, out_specs=[anyspec] * nout, out_shape=ex.out_shapes,
                          scratch_shapes=ex.scratch, name=name)(*ex.arrays)


def _start_exchange(ex, name, after=None):
    nin, nout, nsem = len(ex.arrays), len(ex.out_shapes), len(ex.scratch)
    hbm = pl.BlockSpec(memory_space=pltpu.HBM)
    sem = pl.BlockSpec(memory_space=pltpu.SEMAPHORE)

    n_operands = nin + nout + (after is not None)

    def body(*refs):
        ins, lands = refs[:nin], refs[nin:nin + nout]
        sems = refs[n_operands:n_operands + nsem]
        ex.start(ins, lands, sems)
        refs[-1][...] = jnp.zeros_like(refs[-1])

    args = [pltpu.with_memory_space_constraint(a, pltpu.HBM) for a in ex.arrays]
    args += [pltpu.with_memory_space_constraint(lax.empty(s.shape, s.dtype), pltpu.HBM) for s in ex.out_shapes]
    thru = [pltpu.HBM(a.shape, a.dtype) for a in ex.arrays] + [pltpu.HBM(s.shape, s.dtype) for s in ex.out_shapes]
    extra = [] if after is None else [pl.BlockSpec(memory_space=pl.ANY)]
    args += [] if after is None else [after]
    return pl.pallas_call(
        body, name=name, in_specs=[hbm] * (nin + nout) + extra,
        out_shape=tuple(ex.scratch) + tuple(thru) + (jax.ShapeDtypeStruct((8, 128), F32),),
        out_specs=tuple([sem] * nsem + [hbm] * (nin + nout) + [pl.BlockSpec(memory_space=pltpu.VMEM)]),
        input_output_aliases={q: nsem + q for q in range(nin + nout)},
        compiler_params=pltpu.CompilerParams(has_side_effects=pltpu.SideEffectType.DATAFLOW_SIDE_EFFECTING))(*args)


def _finish_exchange(ex, handles, after, name):
    nin, nout, nsem = len(ex.arrays), len(ex.out_shapes), len(ex.scratch)
    hbm = pl.BlockSpec(memory_space=pltpu.HBM)
    sem = pl.BlockSpec(memory_space=pltpu.SEMAPHORE)
    sems, thru = handles[:nsem], handles[nsem:nsem + nin + nout]

    def body(*refs):
        ins, lands = refs[:nin], refs[nin:nin + nout]
        ex.wait(ins, lands, refs[nin + nout:nin + nout + nsem])

    outs = pl.pallas_call(
        body, name=name, in_specs=[hbm] * (nin + nout) + [sem] * nsem + [pl.BlockSpec(memory_space=pl.ANY)],
        out_shape=tuple(pltpu.HBM(t.shape, t.dtype) for t in thru), out_specs=tuple([hbm] * (nin + nout)),
        input_output_aliases={q: q for q in range(nin + nout)},
        compiler_params=pltpu.CompilerParams(has_side_effects=pltpu.SideEffectType.DATAFLOW_SIDE_EFFECTING))(
            *thru, *sems, after)
    return outs[nin:]


def _all_gather_devices(v, name):
    def body(v_ref, o_ref, send_sems, recv_sems, loc_sem):
        x, y, c = lax.axis_index("x"), lax.axis_index("y"), lax.axis_index("c")
        me = 4 * x + 2 * y + c
        lc = pltpu.make_async_copy(v_ref, o_ref.at[me], loc_sem)
        lc.start()
        rel = [(bx, by, bc) for bx in (0, 1) for by in (0, 1) for bc in (0, 1)][1:]
        copies = []
        for j, (bx, by, bc) in enumerate(rel):
            px, py, pc = x ^ bx, y ^ by, c ^ bc
            copies.append((pltpu.make_async_remote_copy(
                src_ref=v_ref, dst_ref=o_ref.at[me], send_sem=send_sems.at[j], recv_sem=recv_sems.at[j],
                device_id=(px, py, pc), device_id_type=MESH), 4 * px + 2 * py + pc))
        for cp, _ in copies:
            cp.start()
        for j, (cp, pid) in enumerate(copies):
            pltpu.make_async_remote_copy(
                src_ref=v_ref, dst_ref=o_ref.at[pid], send_sem=send_sems.at[j], recv_sem=recv_sems.at[j],
                device_id=(x, y, c), device_id_type=MESH).wait_recv()
        for cp, _ in copies:
            cp.wait_send()
        lc.wait()

    anyspec = pl.BlockSpec(memory_space=pl.ANY)
    return pl.pallas_call(
        body, in_specs=[anyspec], out_specs=anyspec,
        out_shape=jax.ShapeDtypeStruct((N_DEV,) + v.shape, v.dtype),
        scratch_shapes=[pltpu.SemaphoreType.DMA((N_DEV - 1,)), pltpu.SemaphoreType.DMA((N_DEV - 1,)),
                        pltpu.SemaphoreType.DMA(())],
        name=name)(v)


BIG = ("attn_w_qkv", "attn_w_o", "ssm_w_in", "ssm_w_out", "ffn_w_up", "ffn_w_down")
BIG_KIND = {"attn_w_qkv": "col", "attn_w_o": "row", "ssm_w_in": "lead", "ssm_w_out": "row",
            "ffn_w_up": "col", "ffn_w_down": "row"}
SMALL_SHARDED = {"ssm_conv_w": 2, "ssm_conv_b": 1, "ssm_norm_w": 1, "ffn_conv_w": 2}
SMALL = ("mix_norm_w", "ssm_conv_w", "ssm_conv_b", "ssm_dt_bias", "ssm_a_log", "ssm_d", "ssm_norm_w",
         "ffn_norm_w", "ffn_conv_w", "ffn_conv_b", "final_norm_w")
WEIGHTS = ("mix_norm_w", "attn_w_qkv", "attn_w_o", "ssm_w_in", "ssm_conv_w", "ssm_conv_b", "ssm_dt_bias",
           "ssm_a_log", "ssm_d", "ssm_norm_w", "ssm_w_out", "ffn_norm_w", "ffn_w_up", "ffn_conv_w",
           "ffn_conv_b", "ffn_w_down", "final_norm_w")


def _shard_extent(name, shape):
    _, a, b = shape
    return {"col": b, "row": a, "lead": 1}[BIG_KIND[name]]


def _gather_item(w16, name):
    a, b = w16.shape
    kind = BIG_KIND[name]
    full = {"col": (a, N_CHIPS * b), "row": (N_CHIPS * a, b), "lead": (N_CHIPS, a, b)}[kind]
    return (w16, None, kind, _shard_extent(name, (1, a, b)), full)


def _layer_weights(i):
    j = i // 2
    mixer = [("attn_w_qkv", j), ("attn_w_o", j)] if i % 2 == 0 else [("ssm_w_in", j), ("ssm_w_out", j)]
    return mixer + [("ffn_w_up", i), ("ffn_w_down", i)]


def kernel(x, mix_norm_w, attn_w_qkv, attn_w_o, ssm_w_in, ssm_conv_w, ssm_conv_b, ssm_dt_bias, ssm_a_log, ssm_d, ssm_norm_w, ssm_w_out, ffn_norm_w, ffn_w_up, ffn_conv_w, ffn_conv_b, ffn_w_down, final_norm_w, loss_target, m_mix_norm_w, m_attn_w_qkv, m_attn_w_o, m_ssm_w_in, m_ssm_conv_w, m_ssm_conv_b, m_ssm_dt_bias, m_ssm_a_log, m_ssm_d, m_ssm_norm_w, m_ssm_w_out, m_ffn_norm_w, m_ffn_w_up, m_ffn_conv_w, m_ffn_conv_b, m_ffn_w_down, m_final_norm_w, v_mix_norm_w, v_attn_w_qkv, v_attn_w_o, v_ssm_w_in, v_ssm_conv_w, v_ssm_conv_b, v_ssm_dt_bias, v_ssm_a_log, v_ssm_d, v_ssm_norm_w, v_ssm_w_out, v_ffn_norm_w, v_ffn_w_up, v_ffn_conv_w, v_ffn_conv_b, v_ffn_w_down, v_final_norm_w):
    W = dict(mix_norm_w=mix_norm_w, attn_w_qkv=attn_w_qkv, attn_w_o=attn_w_o, ssm_w_in=ssm_w_in,
             ssm_conv_w=ssm_conv_w, ssm_conv_b=ssm_conv_b, ssm_dt_bias=ssm_dt_bias, ssm_a_log=ssm_a_log,
             ssm_d=ssm_d, ssm_norm_w=ssm_norm_w, ssm_w_out=ssm_w_out, ffn_norm_w=ffn_norm_w, ffn_w_up=ffn_w_up,
             ffn_conv_w=ffn_conv_w, ffn_conv_b=ffn_conv_b, ffn_w_down=ffn_w_down, final_norm_w=final_norm_w)
    M = dict(mix_norm_w=m_mix_norm_w, attn_w_qkv=m_attn_w_qkv, attn_w_o=m_attn_w_o, ssm_w_in=m_ssm_w_in,
             ssm_conv_w=m_ssm_conv_w, ssm_conv_b=m_ssm_conv_b, ssm_dt_bias=m_ssm_dt_bias, ssm_a_log=m_ssm_a_log,
             ssm_d=m_ssm_d, ssm_norm_w=m_ssm_norm_w, ssm_w_out=m_ssm_w_out, ffn_norm_w=m_ffn_norm_w,
             ffn_w_up=m_ffn_w_up, ffn_conv_w=m_ffn_conv_w, ffn_conv_b=m_ffn_conv_b, ffn_w_down=m_ffn_w_down,
             final_norm_w=m_final_norm_w)
    V = dict(mix_norm_w=v_mix_norm_w, attn_w_qkv=v_attn_w_qkv, attn_w_o=v_attn_w_o, ssm_w_in=v_ssm_w_in,
             ssm_conv_w=v_ssm_conv_w, ssm_conv_b=v_ssm_conv_b, ssm_dt_bias=v_ssm_dt_bias, ssm_a_log=v_ssm_a_log,
             ssm_d=v_ssm_d, ssm_norm_w=v_ssm_norm_w, ssm_w_out=v_ssm_w_out, ffn_norm_w=v_ffn_norm_w,
             ffn_w_up=v_ffn_w_up, ffn_conv_w=v_ffn_conv_w, ffn_conv_b=v_ffn_conv_b, ffn_w_down=v_ffn_w_down,
             final_norm_w=v_final_norm_w)

    S, D = x.shape[1], x.shape[2]
    xs = x.reshape(S, D)
    tgt = loss_target.reshape(S, D)
    depth = mix_norm_w.shape[0]
    heads = attn_w_o.shape[1] * N_CHIPS // HEAD_DIM
    AW = heads * HEAD_DIM
    d_inner = ssm_w_out.shape[1] * N_CHIPS
    ssm_heads = d_inner // SSM_HEAD_DIM
    G, P, N = SSM_GROUPS, SSM_HEAD_DIM, SSM_STATE
    gw = d_inner // G
    conv_dim = d_inner + 2 * G * N
    in_w = d_inner + conv_dim + ssm_heads
    in_pad = -(-in_w // 128) * 128
    shard_in = ssm_w_in.shape[2]
    xi, yi = lax.axis_index("x"), lax.axis_index("y")
    chip = 2 * xi + yi

    full = {}

    def land(keys, outs):
        for (n, l), o in zip(keys, outs):
            if n == "ssm_w_in":
                o = jnp.pad(jnp.concatenate([o[k] for k in range(N_CHIPS)], axis=1), ((0, 0), (0, in_pad - in_w)))
            full[(n, l)] = o

    def gather_ex(keys, extra=()):
        return _Exchange("gather", [_gather_item(W[n][l].astype(BF16), n) for n, l in keys] + list(extra))

    def fwd_mm(a, b, name, resid=None):
        return _matmul(a, b, "nn", F32, name, resid=resid)

    sm_names = list(SMALL_SHARDED)
    packed = _pack([W[n] for n in sm_names])
    lw = [_layer_weights(i) for i in range(depth)]
    first = lw[0][:1]
    got = _run_exchange(gather_ex(first, [(packed, None, "lead", 1, (N_CHIPS,) + packed.shape)]), "gather_first")
    land(first, got[:-1])
    batches = [lw[0][1:]] + lw[1:]
    pending = []
    for q, keys in enumerate(batches):
        ex = gather_ex(keys)
        pending.append((keys, ex, _start_exchange(ex, f"gather_start_{q}", after=got[-1])))
    issued = sum(handles[-1][0, 0] for _, _, handles in pending)

    def arrive(q, after):
        keys, ex, handles = pending[q]
        land(keys, _finish_exchange(ex, handles, after, f"gather_wait_{q}"))
    per_chip = [_unpack(got[-1][k], [W[n].shape for n in sm_names]) for k in range(N_CHIPS)]
    for q, n in enumerate(sm_names):
        full[n] = jnp.concatenate([per_chip[k][q] for k in range(N_CHIPS)], axis=SMALL_SHARDED[n])
    tab = _perm_tokens(_rope_table(S))

    def rep_heads(p):
        return jnp.repeat(p, P).reshape(G, 1, gw)

    saved = []
    cur = xs
    for i in range(depth):
        j = i // 2
        if i > 0:
            arrive(i, cur)
        sv = {"x_in": cur}
        h = _rms_fwd(cur, mix_norm_w[i] + issued if i == 0 else mix_norm_w[i], f"mix_norm_fwd_{i}")
        sv["h"] = h
        if i % 2 == 0:
            h = _perm_tokens(h)
            sv["h"] = h
            qkv = fwd_mm(h, full[("attn_w_qkv", j)], f"qkv_fwd_{i}")
            if i == 0:
                arrive(0, qkv)
            og = [_attn_fwd(qkv, tab, g, heads, f"attn_fwd_{i}_{g}") for g in range(3)]
            o, lse = _attn_combine([a for a, _ in og], [b for _, b in og], f"attn_combine_{i}")
            mixed = _unperm_tokens(fwd_mm(o, full[("attn_w_o", j)], f"attn_out_fwd_{i}"))
            sv.update(qkv=qkv, o=o, lse=lse)
        else:
            zx = fwd_mm(h, full[("ssm_w_in", j)], f"ssm_in_fwd_{i}")
            conv = _ssm_conv_fwd(zx, full["ssm_conv_w"][j], full["ssm_conv_b"][j], d_inner, conv_dim,
                                 f"ssm_conv_fwd_{i}")
            dtr = jnp.repeat(zx[:, d_inner + conv_dim:in_w], P, axis=1)
            prm = [rep_heads(p[j]) for p in (ssm_dt_bias, ssm_a_log, ssm_d)]
            y, sprev = _ssd_fwd(conv, dtr, *prm, d_inner, f"ssd_fwd_{i}")
            gated = _gnorm_fwd(y, zx, full["ssm_norm_w"][j], f"ssm_norm_fwd_{i}")
            cur = fwd_mm(gated, full[("ssm_w_out", j)], f"ssm_out_fwd_{i}", resid=cur)
            sv.update(zx=zx, conv=conv, dtr=dtr, prm=prm, y=y, sprev=sprev, gated=gated)
            mixed = None
        if mixed is None:
            h2 = _rms_fwd(cur, ffn_norm_w[i], f"ffn_norm_fwd_{i}")
        else:
            cur, h2 = _rms_fwd(cur, ffn_norm_w[i], f"ffn_norm_fwd_{i}", add=mixed)
        sv["x_mid"] = cur
        up = fwd_mm(h2, full[("ffn_w_up", i)], f"ffn_up_fwd_{i}")
        act = _ffn_conv_fwd(up, full["ffn_conv_w"][i], ffn_conv_b[i], f"ffn_conv_fwd_{i}")
        cur = fwd_mm(act, full[("ffn_w_down", i)], f"ffn_down_fwd_{i}", resid=cur)
        sv.update(h2=h2, up=up, act=act)
        saved.append(sv)

    dx, d_final, loss_part = _loss_head(cur, final_norm_w, tgt, "loss_head")
    gbig, recv = {}, {}
    gs = {n: [None] * W[n].shape[0] for n in SMALL if n != "final_norm_w"}

    def scatter_ex(keys):
        return _Exchange("scatter", [(gbig[(n, l)], BIG_KIND[n], _shard_extent(n, W[n].shape),
                                      (N_CHIPS,) + W[n].shape[1:]) for n, l in keys])

    sib = {}

    def swap_ex(keys):
        return _Exchange("swap", [(recv[k], recv[k].shape) for k in keys])

    def bwd_mm(a, b, mode, dtype, name, send=(), swap=()):
        if not send and not swap:
            return _matmul(a, b, mode, dtype, name)
        parts = ([scatter_ex(send)] if send else []) + ([swap_ex(swap)] if swap else [])
        out, got = _matmul(a, b, mode, dtype, name, carry=_Multi(parts))
        recv.update(zip(send, got[:len(send)]))
        sib.update(zip(swap, got[len(send):]))
        return out

    sending = None
    to_swap = []
    for i in reversed(range(depth)):
        j = i // 2
        sv = saved[i]
        k_in, k_out, k_up, k_down = _layer_weights(i)
        own = i == 0

        def now(keys):
            return keys if own else []
        dact = _matmul(dx, full[k_down], "nt", F32, f"ffn_down_dgrad_{i}")
        gbig[k_down] = bwd_mm(sv["act"], dx, "tn", BF16, f"ffn_down_wgrad_{i}", swap=to_swap)
        to_swap = []
        dup, dcw, dcb = _ffn_conv_bwd(dact, sv["up"], full["ffn_conv_w"][i], ffn_conv_b[i], f"ffn_conv_bwd_{i}")
        gs["ffn_conv_w"][i], gs["ffn_conv_b"][i] = dcw, dcb[0]
        dh2 = bwd_mm(dup, full[k_up], "nt", F32, f"ffn_up_dgrad_{i}", send=now([k_down]))
        gbig[k_up] = bwd_mm(sv["h2"], dup, "tn", BF16, f"ffn_up_wgrad_{i}", swap=now([k_down]))
        dx, dnw = _rms_bwd(sv["x_mid"], ffn_norm_w[i], dh2, dx, f"ffn_norm_bwd_{i}")
        gs["ffn_norm_w"][i] = dnw[0]
        if i % 2 == 0:
            dxp = _perm_tokens(dx)
            do = _matmul(dxp, full[k_out], "nt", F32, f"attn_out_dgrad_{i}")
            gbig[k_out] = _matmul(sv["o"], dxp, "tn", BF16, f"attn_out_wgrad_{i}")
            dqkv = None
            for g in range(3):
                dqkv = _attn_bwd(sv["qkv"], tab, sv["o"], sv["lse"], do, dqkv, g, heads, f"attn_bwd_{i}_{g}")
            gbig[k_in] = bwd_mm(sv["h"], dqkv, "tn", BF16, f"qkv_wgrad_{i}", send=now([k_up, k_out]))
            dh = _unperm_tokens(bwd_mm(dqkv, full[k_in], "nt", F32, f"qkv_dgrad_{i}", send=now([k_in]),
                                       swap=now([k_up, k_out])))
        else:
            dgated = _matmul(dx, full[k_out], "nt", F32, f"ssm_out_dgrad_{i}")
            gbig[k_out] = _matmul(sv["gated"], dx, "tn", BF16, f"ssm_out_wgrad_{i}")
            dy, dz, dgw = _gnorm_bwd(sv["y"], sv["zx"], full["ssm_norm_w"][j], dgated, f"ssm_norm_bwd_{i}")
            gs["ssm_norm_w"][j] = dgw[0]
            dxs_, dbm, dcm, ddtr, dbias, dalog, ddsk = _ssd_bwd(
                sv["conv"], sv["dtr"], *sv["prm"], sv["sprev"], dy, d_inner, f"ssd_bwd_{i}")
            gs["ssm_dt_bias"][j] = dbias.reshape(-1)[::P]
            gs["ssm_a_log"][j] = dalog.reshape(-1)[::P]
            gs["ssm_d"][j] = ddsk.reshape(ssm_heads, P).sum(axis=1)
            ddt = jnp.pad(ddtr[:, ::P], ((0, 0), (0, in_pad - in_w)))
            dzx, dcw, dcb = _ssm_conv_bwd(dxs_, dbm, dcm, sv["conv"], sv["zx"], dz, ddt, full["ssm_conv_w"][j],
                                          d_inner, f"ssm_conv_bwd_{i}")
            gs["ssm_conv_w"][j], gs["ssm_conv_b"][j] = dcw, dcb[0]
            dwin = bwd_mm(sv["h"], dzx, "tn", BF16, f"ssm_in_wgrad_{i}", send=now([k_up, k_out]))
            gbig[k_in] = jnp.stack([dwin[:, k * shard_in:(k + 1) * shard_in] for k in range(N_CHIPS)])
            dh = bwd_mm(dzx, full[k_in], "nt", F32, f"ssm_in_dgrad_{i}", send=now([k_in]), swap=now([k_up, k_out]))
        if sending is not None:
            keys, ex, handles = sending
            recv.update(zip(keys, _finish_exchange(ex, handles, dh, f"scatter_wait_{i + 1}")))
            to_swap, sending = keys, None
        if own:
            to_swap = to_swap + [k_in]
        else:
            keys = [k_in, k_out, k_up, k_down]
            ex = scatter_ex(keys)
            sending = (keys, ex, _start_exchange(ex, f"scatter_start_{i}"))
        issued = 0.0 if own else sending[2][-1][0, 0]
        dx, dnw = _rms_bwd(sv["x_in"], mix_norm_w[i] + issued, dh, dx, f"mix_norm_bwd_{i}")
        gs["mix_norm_w"][i] = dnw[0]
    grad_x = dx.reshape(x.shape)

    sib.update(zip(to_swap, _run_exchange(swap_ex(to_swap), "swap_last")))

    small_full = [jnp.stack(gs[n]) if n != "final_norm_w" else d_final[0] for n in SMALL]
    small_full.append(loss_part[0, 0:1])
    small_shapes = [a.shape for a in small_full]
    summed = _sum_devices(_all_gather_devices(_pack(small_full), "gather_small_grads"), "sum_small_grads")
    small_g = _unpack(summed, small_shapes)
    loss = small_g[-1][0]
    gsm = {}
    for n, g in zip(SMALL, small_g[:-1]):
        if n in SMALL_SHARDED:
            ax = SMALL_SHARDED[n]
            ext = W[n].shape[ax]
            g = lax.dynamic_slice_in_dim(g, chip * ext, ext, axis=ax)
        gsm[n] = g

    out_g, out_d, out_m, out_v = {}, {}, {}, {}
    for name in BIG:
        outs = None
        for l in range(W[name].shape[0]):
            outs = _adamw_big(recv[(name, l)], sib[(name, l)], W[name], M[name], V[name], l, outs,
                              f"adamw_{name}_{l}")
        out_g[name], out_d[name], out_m[name], out_v[name] = outs
    shapes = [W[n].shape for n in SMALL]
    pd, pm, pv = _adamw_small(_pack([gsm[n] for n in SMALL]), _pack([W[n] for n in SMALL]),
                              _pack([M[n] for n in SMALL]), _pack([V[n] for n in SMALL]), "adamw_small")
    for n, d_, m_, v_ in zip(SMALL, _unpack(pd, shapes), _unpack(pm, shapes), _unpack(pv, shapes)):
        out_g[n], out_d[n], out_m[n], out_v[n] = gsm[n], d_, m_, v_

    return (loss, grad_x, *[out_g[n] for n in WEIGHTS], *[out_d[n] for n in WEIGHTS],
            *[out_m[n] for n in WEIGHTS], *[out_v[n] for n in WEIGHTS])
```

```python
import functools
import math

import jax
import jax.numpy as jnp
from jax import lax
from jax.experimental import pallas as pl
from jax.experimental.pallas import tpu as pltpu

F32 = jnp.float32
BF16 = jnp.bfloat16
MESH = pl.DeviceIdType.MESH

NORM_EPS = 1e-5
HEAD_DIM = 128
ATTN_BLOCK = 128
ATTN_DILATIONS = (1, 4, 16)
ATTN_WINDOWS = (128, 512, 2048)
PERM = 16
ROPE_THETA = 500000.0
ROPE_HALF = HEAD_DIM // 8
SSM_HEAD_DIM = 64
SSM_STATE = 128
SSM_GROUPS = 8
SSM_CHUNK = 128
NEG = -1e30

ADAM_LR = 0.001
ADAM_B1 = 0.9
ADAM_B2 = 0.999
ADAM_EPS = 1e-08
ADAM_WD = 0.01
ADAM_STEP = 10

VMEM_LIMIT_BYTES = 48 * 1024 * 1024
N_CHIPS = 4
N_DEV = 8


def _cp(*sem):
    return pltpu.CompilerParams(dimension_semantics=sem, vmem_limit_bytes=VMEM_LIMIT_BYTES)


def _pick(n, pref, mult=128):
    best = None
    t = mult
    while t <= min(n, pref):
        if n % t == 0:
            best = t
        t += mult
    return n if best is None else best


def _sigmoid(x):
    return 1.0 / (1.0 + jnp.exp(-x))


def _silu(x):
    return x * _sigmoid(x)


def _softplus(x):
    u = jnp.exp(-jnp.abs(x))
    w = 1.0 + u
    log1p = jnp.where(w == 1.0, u, jnp.log(w) * (u / jnp.where(w == 1.0, 1.0, w - 1.0)))
    return jnp.maximum(x, 0.0) + log1p


def _dot(a, b, dims):
    return lax.dot_general(a.astype(BF16), b.astype(BF16), (dims, ((), ())),
                           preferred_element_type=F32)


def _dot_nn(a, b):
    return _dot(a, b, ((1,), (0,)))


def _dot_nt(a, b):
    return _dot(a, b, ((1,), (1,)))


def _dot_tn(a, b):
    return _dot(a, b, ((0,), (0,)))


MATMUL_VMEM_BYTES = 36 * 1024 * 1024
MATMUL_TILES = (2048, 1536, 1408, 1152, 1024, 896, 768, 640, 512, 384, 256, 128)


def _matmul_tiles(M, N, K, a_bytes, b_bytes, o_bytes, has_resid, fix_tn=None):
    best = None
    for tm in [t for t in MATMUL_TILES if M % t == 0] or [M]:
        for tn in [fix_tn] if fix_tn else [t for t in MATMUL_TILES if N % t == 0] or [N]:
            for tk in [t for t in MATMUL_TILES if K % t == 0 and t <= 1408] or [K]:
                nk, gm, gn = K // tk, M // tm, N // tn
                vmem = 2 * (tm * tk * a_bytes + tk * tn * b_bytes + tm * tn * o_bytes)
                vmem += (2 * tm * tn * 4 if has_resid else 0) + (tm * tn * 4 if nk > 1 else 0)
                if vmem > MATMUL_VMEM_BYTES:
                    continue
                a_reads = 1 if nk == 1 else gn
                b_reads = 1 if (nk == 1 and gn == 1) else gm
                traffic = M * K * a_bytes * a_reads + K * N * b_bytes * b_reads + M * N * o_bytes
                key = (traffic, gm * gn * nk)
                if best is None or key < best[0]:
                    best = (key, (tm, tn, tk))
    assert best is not None, (M, N, K)
    return best[1]


def _matmul(a, b, mode, out_dtype, name, resid=None, carry=None, rope=None):
    if mode == "nn":
        (M, K), (K2, N) = a.shape, b.shape
    elif mode == "nt":
        (M, K), (N, K2) = a.shape, b.shape
    else:
        (K, M), (K2, N) = a.shape, b.shape
    assert K == K2, (a.shape, b.shape, mode)
    tm, tn, tk = _matmul_tiles(M, N, K, a.dtype.itemsize, b.dtype.itemsize, jnp.dtype(out_dtype).itemsize,
                               resid is not None or rope is not None, fix_tn=rope[1] if rope else None)
    nk = K // tk
    assert not (rope and (resid is not None or nk > 1))
    gm, gn = M // tm, N // tn
    dims = {"nn": ((1,), (0,)), "nt": ((1,), (1,)), "tn": ((0,), (0,))}[mode]
    has_resid = resid is not None or rope is not None
    nci = len(carry.arrays) if carry else 0
    nco = len(carry.out_shapes) if carry else 0

    def body(a_ref, b_ref, *rest):
        r_ref = rest[0] if has_resid else None
        rest = rest[has_resid:]
        c_ins, o_ref, c_outs, scratch = rest[:nci], rest[nci], rest[nci + 1:nci + 1 + nco], rest[nci + 1 + nco:]
        acc_ref = scratch[0] if nk > 1 else None
        sems = scratch[nk > 1:]
        i, j, k = pl.program_id(0), pl.program_id(1), pl.program_id(2)
        if carry:
            @pl.when((i == 0) & (j == 0) & (k == 0))
            def _():
                carry.start(c_ins, c_outs, sems)

        if nk == 1:
            r = _dot(a_ref[...], b_ref[...], dims)
            if rope:
                @pl.when(j % 3 < 2)
                def _():
                    for h in range(tn // HEAD_DIM):
                        hs = slice(h * HEAD_DIM, (h + 1) * HEAD_DIM)
                        o_ref[:, hs] = _rope(r[:, hs], r_ref[...], 1.0).astype(o_ref.dtype)

                @pl.when(j % 3 == 2)
                def _():
                    o_ref[...] = r.astype(o_ref.dtype)
            else:
                if has_resid:
                    r = r + r_ref[...]
                o_ref[...] = r.astype(o_ref.dtype)
        else:
            @pl.when(k == 0)
            def _():
                acc_ref[...] = jnp.zeros_like(acc_ref)

            acc_ref[...] += _dot(a_ref[...], b_ref[...], dims)

            @pl.when(k == nk - 1)
            def _():
                r = acc_ref[...]
                if has_resid:
                    r = r + r_ref[...]
                o_ref[...] = r.astype(o_ref.dtype)

        if carry:
            @pl.when((i == gm - 1) & (j == gn - 1) & (k == nk - 1))
            def _():
                carry.wait(c_ins, c_outs, sems)

    if mode == "nn":
        a_spec = pl.BlockSpec((tm, tk), lambda i, j, k: (i, k))
        b_spec = pl.BlockSpec((tk, tn), lambda i, j, k: (k, j))
    elif mode == "nt":
        a_spec = pl.BlockSpec((tm, tk), lambda i, j, k: (i, k))
        b_spec = pl.BlockSpec((tn, tk), lambda i, j, k: (j, k))
    else:
        a_spec = pl.BlockSpec((tk, tm), lambda i, j, k: (k, i))
        b_spec = pl.BlockSpec((tk, tn), lambda i, j, k: (k, j))
    o_spec = pl.BlockSpec((tm, tn), lambda i, j, k: (i, j))
    anyspec = pl.BlockSpec(memory_space=pl.ANY)
    extra_spec = pl.BlockSpec((tm, 3 * HEAD_DIM), lambda i, j, k: (i, 0)) if rope else o_spec
    in_specs = [a_spec, b_spec] + ([extra_spec] if has_resid else []) + [anyspec] * nci
    extra = (rope[0],) if rope else ((resid,) if has_resid else ())
    args = (a, b) + extra + (tuple(carry.arrays) if carry else ())
    out_shape = [jax.ShapeDtypeStruct((M, N), out_dtype)] + (carry.out_shapes if carry else [])
    scratch = ([] if nk == 1 else [pltpu.VMEM((tm, tn), F32)]) + (carry.scratch if carry else [])
    sem = ("arbitrary",) * 3 if carry else ("parallel", "parallel", "arbitrary")
    outs = pl.pallas_call(
        body, grid=(gm, gn, nk), in_specs=in_specs, out_specs=[o_spec] + [anyspec] * nco,
        out_shape=out_shape, scratch_shapes=scratch, compiler_params=_cp(*sem), name=name)(*args)
    return (outs[0], outs[1:]) if carry else outs[0]


def _rms_fwd(x, w, name, add=None):
    S, D = x.shape
    t = _pick(S, 512, 8)
    has_add = add is not None

    def body(x_ref, w_ref, *rest):
        xv = x_ref[...]
        if has_add:
            xv = xv + rest[0][...]
            rest[1][...] = xv
        r = lax.rsqrt(jnp.mean(xv * xv, axis=-1, keepdims=True) + NORM_EPS)
        rest[-1][...] = (xv * r * w_ref[...]).astype(rest[-1].dtype)

    row = pl.BlockSpec((t, D), lambda i: (i, 0))
    vec = pl.BlockSpec((1, D), lambda i: (0, 0))
    normed = jax.ShapeDtypeStruct((S, D), BF16)
    if not has_add:
        return pl.pallas_call(body, grid=(S // t,), in_specs=[row, vec], out_specs=row, out_shape=normed,
                              compiler_params=_cp("parallel"), name=name)(x, w.reshape(1, D))
    return pl.pallas_call(body, grid=(S // t,), in_specs=[row, vec, row], out_specs=[row, row],
                          out_shape=[jax.ShapeDtypeStruct((S, D), F32), normed],
                          compiler_params=_cp("parallel"), name=name)(x, w.reshape(1, D), add)


def _rms_bwd(x, w, dh, dres, name):
    S, D = x.shape
    t = _pick(S, 512, 8)

    def body(x_ref, w_ref, dh_ref, dr_ref, dx_ref, dw_ref):
        @pl.when(pl.program_id(0) == 0)
        def _():
            dw_ref[...] = jnp.zeros_like(dw_ref)

        xv = x_ref[...]
        r = lax.rsqrt(jnp.mean(xv * xv, axis=-1, keepdims=True) + NORM_EPS)
        xh = xv * r
        dh_v = dh_ref[...]
        g = dh_v * w_ref[...]
        dx_ref[...] = dr_ref[...] + r * (g - xh * jnp.mean(g * xh, axis=-1, keepdims=True))
        dw_ref[...] += jnp.sum(dh_v * xh, axis=0, keepdims=True)

    row = pl.BlockSpec((t, D), lambda i: (i, 0))
    vec = pl.BlockSpec((1, D), lambda i: (0, 0))
    return pl.pallas_call(
        body, grid=(S // t,), in_specs=[row, vec, row, row], out_specs=[row, vec],
        out_shape=[jax.ShapeDtypeStruct((S, D), F32), jax.ShapeDtypeStruct((1, D), F32)],
        compiler_params=_cp("arbitrary"), name=name)(x, w.reshape(1, D), dh, dres)


def _loss_head(x, w, tgt, name):
    S, D = x.shape
    t = _pick(S, 512, 8)

    def body(x_ref, w_ref, t_ref, dx_ref, dw_ref, l_ref):
        @pl.when(pl.program_id(0) == 0)
        def _():
            dw_ref[...] = jnp.zeros_like(dw_ref)
            l_ref[...] = jnp.zeros_like(l_ref)

        xv = x_ref[...]
        wv = w_ref[...]
        r = lax.rsqrt(jnp.mean(xv * xv, axis=-1, keepdims=True) + NORM_EPS)
        xh = xv * r
        err = xh * wv - t_ref[...]
        per_tok = jnp.mean(err * err, axis=-1, keepdims=True)
        l_ref[...] += 0.5 * jnp.sum(per_tok, axis=0, keepdims=True)
        dy = err * (1.0 / D)
        g = dy * wv
        dx_ref[...] = r * (g - xh * jnp.mean(g * xh, axis=-1, keepdims=True))
        dw_ref[...] += jnp.sum(dy * xh, axis=0, keepdims=True)

    row = pl.BlockSpec((t, D), lambda i: (i, 0))
    vec = pl.BlockSpec((1, D), lambda i: (0, 0))
    lspec = pl.BlockSpec((1, 128), lambda i: (0, 0))
    return pl.pallas_call(
        body, grid=(S // t,), in_specs=[row, vec, row], out_specs=[row, vec, lspec],
        out_shape=[jax.ShapeDtypeStruct((S, D), F32), jax.ShapeDtypeStruct((1, D), F32),
                   jax.ShapeDtypeStruct((1, 128), F32)],
        compiler_params=_cp("arbitrary"), name=name)(x, w.reshape(1, D), tgt)


def _rope_table(seq):
    pos = jnp.arange(seq, dtype=F32)
    inv_freq = ROPE_THETA ** (-jnp.arange(0, 2 * ROPE_HALF, 2, dtype=F32) / (2 * ROPE_HALF))
    ang = pos[:, None] * inv_freq[None, :]
    cos, sin = jnp.cos(ang), jnp.sin(ang)
    pad = HEAD_DIM - 2 * ROPE_HALF
    cos_p = jnp.concatenate([cos, cos, jnp.ones((seq, pad), F32)], axis=1)
    sin_a = jnp.concatenate([-sin, jnp.zeros((seq, HEAD_DIM - ROPE_HALF), F32)], axis=1)
    sin_b = jnp.concatenate([jnp.zeros((seq, ROPE_HALF), F32), sin, jnp.zeros((seq, pad), F32)], axis=1)
    return jnp.concatenate([cos_p, sin_a, sin_b], axis=1)


def _rope(t, tab, sign):
    cos_p = tab[:, 0:HEAD_DIM]
    sin_a = tab[:, HEAD_DIM:2 * HEAD_DIM]
    sin_b = tab[:, 2 * HEAD_DIM:3 * HEAD_DIM]
    up = pltpu.roll(t, HEAD_DIM - ROPE_HALF, 1)
    down = pltpu.roll(t, ROPE_HALF, 1)
    return t * cos_p + sign * (up * sin_a + down * sin_b)


def _perm_tokens(a):
    S = a.shape[0]
    return a.reshape(S // PERM, PERM, -1).transpose(1, 0, 2).reshape(S, -1)


def _unperm_tokens(a):
    S = a.shape[0]
    return a.reshape(PERM, S // PERM, -1).transpose(1, 0, 2).reshape(S, -1)


class _Strided:
    def __init__(self, S, dil):
        self.dil, self.m = dil, PERM // dil
        self.c = ATTN_BLOCK // self.m
        self.rows = S // PERM
        self.nb = S // (dil * ATTN_BLOCK)

    def view(self, a):
        return a.reshape(self.m, self.dil, self.rows, a.shape[-1])

    def spec(self, width, col, f=lambda n: n):
        return pl.BlockSpec((self.m, None, self.c, width), lambda r, n: (0, r, f(n), col))

    def load(self, ref, sl=slice(None)):
        if self.m == 1:
            return ref[0, :, sl]
        return jnp.concatenate([ref[q, :, sl] for q in range(self.m)], axis=0)

    def store(self, ref, sl, val):
        for q in range(self.m):
            ref[q, :, sl] = val[q * self.c:(q + 1) * self.c, :]

    def member(self, i):
        shift = self.c.bit_length() - 1
        return (i & (self.c - 1)) * self.m + (i >> shift)


def _attn_fwd(qkv, g, heads, name):
    S = qkv.shape[0]
    W = heads * HEAD_DIM
    dil = ATTN_DILATIONS[g]
    steps = ATTN_WINDOWS[g] // dil
    B = ATTN_BLOCK
    scale = HEAD_DIM ** -0.5
    st = _Strided(S, dil)

    def body(q_ref, k_ref, kp_ref, v_ref, vp_ref, o_ref, l_ref):
        n = pl.program_id(1)
        ii = lax.broadcasted_iota(jnp.int32, (B, 2 * B), 0)
        jj = lax.broadcasted_iota(jnp.int32, (B, 2 * B), 1)
        delta = st.member(ii) - st.member(jj & (B - 1)) + jnp.where(jj >= B, 0, B)
        ok = (delta >= 0) & (delta <= steps) & ((jj >= B) | (n > 0))
        sls = [slice(h * HEAD_DIM, (h + 1) * HEAD_DIM) for h in range(heads)]
        qs = [st.load(q_ref, sl) for sl in sls]
        kcs = [jnp.concatenate([st.load(kp_ref, sl), st.load(k_ref, sl)], axis=0) for sl in sls]
        ss = [jnp.where(ok, _dot_nt(q, kc) * scale, NEG) for q, kc in zip(qs, kcs)]
        ms = [jnp.max(s, axis=-1, keepdims=True) for s in ss]
        ps = [jnp.exp(s - m) for s, m in zip(ss, ms)]
        dens = [jnp.sum(p, axis=-1, keepdims=True) for p in ps]
        vcs = [jnp.concatenate([st.load(vp_ref, sl), st.load(v_ref, sl)], axis=0) for sl in sls]
        outs = [_dot_nn(p, vc) for p, vc in zip(ps, vcs)]
        for sl, o, m, den in zip(sls, outs, ms, dens):
            st.store(o_ref, sl, o / den)
            st.store(l_ref, sl, jnp.broadcast_to(m + jnp.log(den), (B, HEAD_DIM)))

    prv = lambda n: jnp.maximum(n - 1, 0)
    qv = st.view(qkv)
    o_spec = st.spec(W, 0)
    o, lse = pl.pallas_call(
        body, grid=(dil, st.nb),
        in_specs=[st.spec(W, g * 3), st.spec(W, g * 3 + 1), st.spec(W, g * 3 + 1, prv),
                  st.spec(W, g * 3 + 2), st.spec(W, g * 3 + 2, prv)],
        out_specs=[o_spec, o_spec],
        out_shape=[jax.ShapeDtypeStruct((st.m, dil, st.rows, W), F32)] * 2,
        compiler_params=_cp("parallel", "parallel"), name=name)(qv, qv, qv, qv, qv)
    return o.reshape(S, W), lse.reshape(S, W)


def _attn_combine(os_, ls_, name):
    S, W = os_[0].shape
    t = _pick(S, 256, 8)

    def body(o0, o1, o2, l0, l1, l2, o_ref, l_ref):
        a, b, c = l0[...], l1[...], l2[...]
        m = jnp.maximum(jnp.maximum(a, b), c)
        ea, eb, ec = jnp.exp(a - m), jnp.exp(b - m), jnp.exp(c - m)
        tot = ea + eb + ec
        o_ref[...] = (ea * o0[...] + eb * o1[...] + ec * o2[...]) / tot
        l_ref[...] = m + jnp.log(tot)

    row = pl.BlockSpec((t, W), lambda i: (i, 0))
    return pl.pallas_call(body, grid=(S // t,), in_specs=[row] * 6, out_specs=[row, row],
                          out_shape=[jax.ShapeDtypeStruct((S, W), F32)] * 2,
                          compiler_params=_cp("parallel"), name=name)(*os_, *ls_)


def _attn_bwd(qkv, tab, o, lse, do, dqkv_prev, g, heads, name):
    S = qkv.shape[0]
    W = heads * HEAD_DIM
    dil = ATTN_DILATIONS[g]
    steps = ATTN_WINDOWS[g] // dil
    B = ATTN_BLOCK
    scale = HEAD_DIM ** -0.5
    st = _Strided(S, dil)
    nb = st.nb
    aliased = dqkv_prev is not None

    def body(q_ref, qn_ref, k_ref, kp_ref, v_ref, vp_ref, do_ref, don_ref, o_ref, on_ref,
             l_ref, ln_ref, t_ref, *rest):
        out_ref = rest[-1]
        n = pl.program_id(1)
        has_next = n < nb - 1
        ia = lax.broadcasted_iota(jnp.int32, (B, 2 * B), 0)
        ja = lax.broadcasted_iota(jnp.int32, (B, 2 * B), 1)
        da = st.member(ia) - st.member(ja & (B - 1)) + jnp.where(ja >= B, 0, B)
        ok_a = (da >= 0) & (da <= steps) & ((ja >= B) | (n > 0))
        ib = lax.broadcasted_iota(jnp.int32, (2 * B, B), 0)
        jb = lax.broadcasted_iota(jnp.int32, (2 * B, B), 1)
        db = st.member(ib & (B - 1)) + jnp.where(ib >= B, B, 0) - st.member(jb)
        ok_b = (db >= 0) & (db <= steps) & ((ib < B) | has_next)
        tb = st.load(t_ref)
        hs = range(heads)
        sls = [slice(h * HEAD_DIM, (h + 1) * HEAD_DIM) for h in hs]
        qr = [st.load(q_ref, sl) for sl in sls]
        qnr = [st.load(qn_ref, sl) for sl in sls]
        kr = [st.load(k_ref, sl) for sl in sls]
        kpr = [st.load(kp_ref, sl) for sl in sls]
        v = [st.load(v_ref, sl) for sl in sls]
        dov_ = [st.load(do_ref, sl) for sl in sls]
        donv = [st.load(don_ref, sl) for sl in sls]
        dl = [jnp.sum(dov_[h] * st.load(o_ref, sls[h]), axis=-1, keepdims=True) for h in hs]
        dln = [jnp.sum(donv[h] * st.load(on_ref, sls[h]), axis=-1, keepdims=True) for h in hs]
        ls = [st.load(l_ref, sl) for sl in sls]
        kc = [jnp.concatenate([kpr[h], kr[h]], axis=0) for h in hs]
        vc = [jnp.concatenate([st.load(vp_ref, sls[h]), v[h]], axis=0) for h in hs]
        qc = [jnp.concatenate([qr[h], qnr[h]], axis=0) for h in hs]
        doc = [jnp.concatenate([dov_[h], donv[h]], axis=0) for h in hs]
        lc = [jnp.concatenate([ls[h], st.load(ln_ref, sls[h])], axis=0) for h in hs]
        dlc = [jnp.concatenate([dl[h], dln[h]], axis=0) for h in hs]
        s = [_dot_nt(qr[h], kc[h]) * scale for h in hs]
        dp = [_dot_nt(dov_[h], vc[h]) for h in hs]
        s2 = [_dot_nt(qc[h], kr[h]) * scale for h in hs]
        dp2 = [_dot_nt(doc[h], v[h]) for h in hs]
        p = [jnp.where(ok_a, jnp.exp(jnp.minimum(s[h] - jnp.concatenate([ls[h], ls[h]], axis=1), 30.0)), 0.0)
             for h in hs]
        ds = [p[h] * (dp[h] - dl[h]) * scale for h in hs]
        p2 = [jnp.where(ok_b, jnp.exp(jnp.minimum(s2[h] - lc[h], 30.0)), 0.0) for h in hs]
        ds2 = [p2[h] * (dp2[h] - dlc[h]) * scale for h in hs]
        dq = [_dot_nn(ds[h], kc[h]) for h in hs]
        dk = [_dot_tn(ds2[h], qc[h]) for h in hs]
        dv = [_dot_tn(p2[h], doc[h]) for h in hs]
        for h in hs:
            st.store(out_ref, sls[h], _rope(dq[h], tb, -1.0))
            st.store(out_ref, slice(W + h * HEAD_DIM, W + (h + 1) * HEAD_DIM), _rope(dk[h], tb, -1.0))
            st.store(out_ref, slice(2 * W + h * HEAD_DIM, 2 * W + (h + 1) * HEAD_DIM), dv[h])

    nxt = lambda n: jnp.minimum(n + 1, nb - 1)
    prv = lambda n: jnp.maximum(n - 1, 0)
    same = lambda n: n
    q0, q1, q2, tw = g * 3, g * 3 + 1, g * 3 + 2, 3 * HEAD_DIM
    in_specs = [st.spec(W, q0), st.spec(W, q0, nxt), st.spec(W, q1), st.spec(W, q1, prv),
                st.spec(W, q2), st.spec(W, q2, prv)]
    in_specs += [st.spec(W, 0, f) for f in (same, nxt, same, nxt, same, nxt)]
    in_specs += [st.spec(tw, 0)]
    qv, tv, ov, lv, dov = (st.view(a) for a in (qkv, tab, o, lse, do))
    args = [qv, qv, qv, qv, qv, qv, dov, dov, ov, ov, lv, lv, tv]
    kwargs = {}
    if aliased:
        in_specs.append(pl.BlockSpec(memory_space=pl.ANY))
        args.append(st.view(dqkv_prev))
        kwargs["input_output_aliases"] = {len(args) - 1: 0}
    out = pl.pallas_call(
        body, grid=(dil, nb), in_specs=in_specs, out_specs=st.spec(3 * W, g),
        out_shape=jax.ShapeDtypeStruct((st.m, dil, st.rows, 9 * W), F32),
        compiler_params=_cp("parallel", "parallel"), name=name, **kwargs)(*args)
    return out.reshape(S, 9 * W)


def _shift_down(x, halo, s):
    if s == 0:
        return x
    T = x.shape[0]
    xs = pltpu.roll(x, s, 0)
    hs = pltpu.roll(halo, s, 0)
    row8 = lax.broadcasted_iota(jnp.int32, hs.shape, 0)
    top = jnp.where(row8 < s, hs, xs[0:8])
    return top if T == 8 else jnp.concatenate([top, xs[8:T]], axis=0)


def _shift_up(x, halo, s):
    if s == 0:
        return x
    T = x.shape[0]
    xs = pltpu.roll(x, T - s, 0)
    hs = pltpu.roll(halo, 8 - s, 0)
    row8 = lax.broadcasted_iota(jnp.int32, hs.shape, 0)
    bot = jnp.where(row8 >= 8 - s, hs, xs[T - 8:T])
    return jnp.concatenate([xs[0:T - 8], bot], axis=0)


CONV_ROWS = 128
CONV_LANES = 512


def _conv_apply(x, halo, w_ref, wsl, b, K):
    acc = x * w_ref[K - 1, :, wsl] + b
    for s in range(1, K):
        acc = acc + _shift_down(x, halo, s) * w_ref[K - 1 - s, :, wsl]
    return acc


def _conv_accum(dy, dyn, xv, w_ref, dw_ref, db_ref, wsl, K):
    acc = dy * w_ref[K - 1, :, wsl]
    dw_ref[K - 1, :, wsl] += jnp.sum(dy * xv, axis=0, keepdims=True)
    for s in range(1, K):
        ahead = _shift_up(dy, dyn, s)
        acc = acc + ahead * w_ref[K - 1 - s, :, wsl]
        dw_ref[K - 1 - s, :, wsl] += jnp.sum(ahead * xv, axis=0, keepdims=True)
    db_ref[:, wsl] += jnp.sum(dy, axis=0, keepdims=True)
    return acc


def _row_specs(T, S, width):
    main = pl.BlockSpec((T, width), lambda i: (i, 0))
    prev = pl.BlockSpec((8, width), lambda i: (jnp.maximum(i * (T // 8) - 1, 0), 0))
    nxt = pl.BlockSpec((8, width), lambda i: (jnp.minimum((i + 1) * (T // 8), S // 8 - 1), 0))
    return main, prev, nxt


def _full(shape):
    return pl.BlockSpec(shape, lambda i: (0,) * len(shape))


def _silu_grad(y):
    sg = _sigmoid(y)
    return sg * (1.0 + y * (1.0 - sg))


def _ssm_conv_fwd(zx, w, b, d_inner, conv_dim, name):
    S, wz = zx.shape
    K = w.shape[0]
    T = _pick(S, CONV_ROWS, 8)
    cw = _pick(conv_dim, CONV_LANES)

    def body(x_ref, h_ref, w_ref, b_ref, c_ref):
        has_prev = pl.program_id(0) > 0
        for cs in range(0, conv_dim, cw):
            so, sx = slice(cs, cs + cw), slice(d_inner + cs, d_inner + cs + cw)
            halo = jnp.where(has_prev, h_ref[:, sx], 0.0)
            c_ref[:, so] = _conv_apply(x_ref[:, sx], halo, w_ref, so, b_ref[:, so], K)

    main, prev, _ = _row_specs(T, S, wz)
    return pl.pallas_call(
        body, grid=(S // T,), in_specs=[main, prev, _full((K, 1, conv_dim)), _full((1, conv_dim))],
        out_specs=pl.BlockSpec((T, conv_dim), lambda i: (i, 0)),
        out_shape=jax.ShapeDtypeStruct((S, conv_dim), F32),
        compiler_params=_cp("parallel"), name=name)(zx, zx, w.reshape(K, 1, conv_dim), b.reshape(1, conv_dim))


def _ssm_conv_bwd(dxs, dbm, dcm, conv, zx, dz, ddt, w, d_inner, name):
    S, wz = zx.shape
    K, conv_dim = w.shape
    gn = dbm.shape[1]
    T = _pick(S, CONV_ROWS, 8)
    cw = _pick(math.gcd(d_inner, gn), CONV_LANES)
    nrow = S // T
    tail = wz - d_inner - conv_dim
    assert ddt.shape[1] == tail

    def body(dx_ref, dxn_ref, db_ref_, dbn_ref, dc_ref, dcn_ref, y_ref, yn_ref, x_ref, dz_ref, ddt_ref,
             w_ref, o_ref, dw_ref, dbias_ref):
        i = pl.program_id(0)

        @pl.when(i == 0)
        def _():
            dw_ref[...] = jnp.zeros_like(dw_ref)
            dbias_ref[...] = jnp.zeros_like(dbias_ref)

        has_prev, has_next = i > 0, i < nrow - 1
        for cs in range(0, d_inner, cw):
            o_ref[:, cs:cs + cw] = dz_ref[:, cs:cs + cw].astype(o_ref.dtype)
        o_ref[:, d_inner + conv_dim:wz] = ddt_ref[...].astype(o_ref.dtype)
        for cs in range(0, conv_dim, cw):
            so, sx = slice(cs, cs + cw), slice(d_inner + cs, d_inner + cs + cw)
            if cs < d_inner:
                src, srcn, ss = dx_ref, dxn_ref, slice(cs, cs + cw)
            elif cs < d_inner + gn:
                src, srcn, ss = db_ref_, dbn_ref, slice(cs - d_inner, cs - d_inner + cw)
            else:
                src, srcn, ss = dc_ref, dcn_ref, slice(cs - d_inner - gn, cs - d_inner - gn + cw)
            dy = src[:, ss] * _silu_grad(y_ref[:, so])
            dyn = jnp.where(has_next, srcn[:, ss] * _silu_grad(yn_ref[:, so]), 0.0)
            o_ref[:, sx] = _conv_accum(dy, dyn, x_ref[:, sx], w_ref, dw_ref, dbias_ref, so, K).astype(o_ref.dtype)

    xm, _, xn = _row_specs(T, S, d_inner)
    gm, _, gnx = _row_specs(T, S, gn)
    cm, _, cn = _row_specs(T, S, conv_dim)
    zm, _, _ = _row_specs(T, S, wz)
    tm_, _, _ = _row_specs(T, S, tail)
    dzx, dw, db = pl.pallas_call(
        body, grid=(nrow,),
        in_specs=[xm, xn, gm, gnx, gm, gnx, cm, cn, zm, xm, tm_, _full((K, 1, conv_dim))],
        out_specs=[zm, _full((K, 1, conv_dim)), _full((1, conv_dim))],
        out_shape=[jax.ShapeDtypeStruct((S, wz), BF16), jax.ShapeDtypeStruct((K, 1, conv_dim), F32),
                   jax.ShapeDtypeStruct((1, conv_dim), F32)],
        compiler_params=_cp("arbitrary"), name=name)(
            dxs, dxs, dbm, dbm, dcm, dcm, conv, conv, zx, dz, ddt, w.reshape(K, 1, conv_dim))
    return dzx, dw.reshape(K, conv_dim), db


def _ffn_conv_fwd(up, w, b, name):
    S, C = up.shape
    F = C // 2
    K = w.shape[0]
    T = _pick(S, CONV_ROWS, 8)
    cw = _pick(F, CONV_LANES)

    def body(x_ref, h_ref, w_ref, b_ref, a_ref):
        has_prev = pl.program_id(0) > 0
        for cs in range(0, F, cw):
            sg, su = slice(cs, cs + cw), slice(F + cs, F + cs + cw)
            gate = _conv_apply(x_ref[:, sg], jnp.where(has_prev, h_ref[:, sg], 0.0), w_ref, sg, b_ref[:, sg], K)
            upv = _conv_apply(x_ref[:, su], jnp.where(has_prev, h_ref[:, su], 0.0), w_ref, su, b_ref[:, su], K)
            a_ref[:, sg] = (gate * _sigmoid(gate) * upv).astype(a_ref.dtype)

    main, prev, _ = _row_specs(T, S, C)
    return pl.pallas_call(
        body, grid=(S // T,), in_specs=[main, prev, _full((K, 1, C)), _full((1, C))],
        out_specs=pl.BlockSpec((T, F), lambda i: (i, 0)), out_shape=jax.ShapeDtypeStruct((S, F), BF16),
        compiler_params=_cp("parallel"), name=name)(up, up, w.reshape(K, 1, C), b.reshape(1, C))


def _ffn_conv_bwd(dact, up, w, b, name):
    S, C = up.shape
    F = C // 2
    K = w.shape[0]
    T = _pick(S, CONV_ROWS, 8)
    cw = _pick(F, CONV_LANES)
    nrow = S // T

    def du(gate, upv, d):
        sg = _sigmoid(gate)
        return d * upv * sg * (1.0 + gate * (1.0 - sg)), d * gate * sg

    def body(d_ref, dn_ref, x_ref, xp_ref, xn_ref, w_ref, b_ref, dx_ref, dw_ref, db_ref):
        i = pl.program_id(0)

        @pl.when(i == 0)
        def _():
            dw_ref[...] = jnp.zeros_like(dw_ref)
            db_ref[...] = jnp.zeros_like(db_ref)

        has_prev, has_next = i > 0, i < nrow - 1
        for cs in range(0, F, cw):
            sf = slice(cs, cs + cw)
            cols = [slice(half * F + cs, half * F + cs + cw) for half in range(2)]
            xs = [x_ref[:, sc] for sc in cols]
            xps = [jnp.where(has_prev, xp_ref[:, sc], 0.0) for sc in cols]
            u = [_conv_apply(xs[q], xps[q], w_ref, cols[q], b_ref[:, cols[q]], K) for q in range(2)]
            un = [_conv_apply(xn_ref[:, cols[q]], xs[q][T - 8:T], w_ref, cols[q], b_ref[:, cols[q]], K)
                  for q in range(2)]
            dys = du(u[0], u[1], d_ref[:, sf])
            dyns = du(un[0], un[1], dn_ref[:, sf])
            for q in range(2):
                dyn = jnp.where(has_next, dyns[q], 0.0)
                dx_ref[:, cols[q]] = _conv_accum(dys[q], dyn, xs[q], w_ref, dw_ref, db_ref, cols[q],
                                                 K).astype(dx_ref.dtype)

    am, _, an = _row_specs(T, S, F)
    xm, xp_, xn_ = _row_specs(T, S, C)
    dx, dw, db = pl.pallas_call(
        body, grid=(nrow,), in_specs=[am, an, xm, xp_, xn_, _full((K, 1, C)), _full((1, C))],
        out_specs=[xm, _full((K, 1, C)), _full((1, C))],
        out_shape=[jax.ShapeDtypeStruct((S, C), BF16), jax.ShapeDtypeStruct((K, 1, C), F32),
                   jax.ShapeDtypeStruct((1, C), F32)],
        compiler_params=_cp("arbitrary"), name=name)(dact, dact, up, up, up, w.reshape(K, 1, C), b.reshape(1, C))
    return dx, dw.reshape(K, C), db


def _cumsum_rows(v):
    n = v.shape[0]
    row = lax.broadcasted_iota(jnp.int32, v.shape, 0)
    k = 1
    while k < n:
        v = v + jnp.where(row >= k, pltpu.roll(v, k, 0), 0.0)
        k *= 2
    return v


def _rev_cumsum_rows(v):
    n = v.shape[0]
    row = lax.broadcasted_iota(jnp.int32, v.shape, 0)
    k = 1
    while k < n:
        v = v + jnp.where(row < n - k, pltpu.roll(v, n - k, 0), 0.0)
        k *= 2
    return v


def _ssd_common(x_ref, dtr_ref, bias_ref, alog_ref, gw):
    Q = SSM_CHUNK
    X = _silu(x_ref[...])
    pre = dtr_ref[...] + bias_ref[...]
    dt = _softplus(pre)
    a = -jnp.exp(alog_ref[...])
    cs = _cumsum_rows(dt * a)
    row = lax.broadcasted_iota(jnp.int32, (Q, gw), 0)
    cs_last = jnp.sum(jnp.where(row == Q - 1, cs, 0.0), axis=0, keepdims=True)
    return X, pre, dt, a, cs, cs_last, row


def _head_decay(cs, head_mask):
    Q = SSM_CHUNK
    col = jnp.max(jnp.where(head_mask, cs, NEG), axis=1, keepdims=True)
    acol = jnp.broadcast_to(col, (Q, Q))
    arow = acol.T
    ii = lax.broadcasted_iota(jnp.int32, (Q, Q), 0)
    jj = lax.broadcasted_iota(jnp.int32, (Q, Q), 1)
    tril = ii >= jj
    return jnp.where(tril, jnp.exp(jnp.where(tril, acol - arow, 0.0)), 0.0), tril


class _Win:
    def __init__(self, ref, idx):
        self.ref, self.idx = ref, idx

    def __getitem__(self, _):
        return self.ref[self.idx]

    def __setitem__(self, _, value):
        self.ref[self.idx] = value

    @property
    def dtype(self):
        return self.ref.dtype

    @property
    def shape(self):
        return jax.eval_shape(lambda a: a[self.idx], jax.ShapeDtypeStruct(self.ref.shape, self.ref.dtype)).shape


def _ssd_windows(g, d_inner, gw):
    N, G = SSM_STATE, SSM_GROUPS
    rows = slice(None)
    x = (rows, slice(g * gw, (g + 1) * gw))
    b = (rows, slice(d_inner + g * N, d_inner + (g + 1) * N))
    c = (rows, slice(d_inner + (G + g) * N, d_inner + (G + g + 1) * N))
    n = (rows, slice(g * N, (g + 1) * N))
    return x, b, c, n


def _ssd_fwd(xbc, dtr, bias, alog, dsk, d_inner, name):
    S = xbc.shape[0]
    Q, N, G, P = SSM_CHUNK, SSM_STATE, SSM_GROUPS, SSM_HEAD_DIM
    gw = d_inner // G
    R = gw // P
    nc = S // Q

    def body(conv_ref, dtr_ref, bias_ref, alog_ref, d_ref, y_ref, sp_ref, s_scr):
        first = pl.program_id(0) == 0
        for g in range(G):
            x, b, c, _ = _ssd_windows(g, d_inner, gw)
            one_group(first, _Win(conv_ref, x), _Win(conv_ref, b), _Win(conv_ref, c), _Win(dtr_ref, x),
                      _Win(bias_ref, g), _Win(alog_ref, g), _Win(d_ref, g), _Win(y_ref, x), _Win(sp_ref, g),
                      _Win(s_scr, g))

    def one_group(first, x_ref, b_ref, c_ref, dtr_ref, bias_ref, alog_ref, d_ref, y_ref, sp_ref, s_scr):
        @pl.when(first)
        def _():
            s_scr[...] = jnp.zeros((gw, N), F32)

        X, _, dt, a, cs, cs_last, row = _ssd_common(x_ref, dtr_ref, bias_ref, alog_ref, gw)
        Bm, Cm = _silu(b_ref[...]), _silu(c_ref[...])
        xdt = X * dt
        lane = lax.broadcasted_iota(jnp.int32, (Q, gw), 1)
        sprev = s_scr[...]
        sp_ref[...] = sprev
        cb = _dot_nt(Cm, Bm)
        y = jnp.exp(cs) * _dot_nt(Cm, sprev)
        hms = [(lane >= r * P) & (lane < (r + 1) * P) for r in range(R)]
        dec_ls = [_head_decay(cs, hm)[0] for hm in hms]
        for part in [_dot_nn(cb * dec_l, jnp.where(hm, xdt, 0.0)) for dec_l, hm in zip(dec_ls, hms)]:
            y = y + part
        dec = jnp.exp(cs_last - cs)
        cd = jnp.exp(jnp.broadcast_to(cs_last, (Q, gw)).T)
        s_scr[...] = sprev * cd + _dot_tn(xdt * dec, Bm)
        y_ref[...] = y + d_ref[...] * X

    conv_dim = xbc.shape[1]
    row = pl.BlockSpec((Q, d_inner), lambda c: (c, 0))
    p_spec = _full((G, 1, gw))
    return pl.pallas_call(
        body, grid=(nc,),
        in_specs=[pl.BlockSpec((Q, conv_dim), lambda c: (c, 0)), row, p_spec, p_spec, p_spec],
        out_specs=[row, pl.BlockSpec((None, G, gw, N), lambda c: (c, 0, 0, 0))],
        out_shape=[jax.ShapeDtypeStruct((S, d_inner), F32), jax.ShapeDtypeStruct((nc, G, gw, N), F32)],
        scratch_shapes=[pltpu.VMEM((G, gw, N), F32)],
        compiler_params=_cp("arbitrary"), name=name)(xbc, dtr, bias, alog, dsk)


def _ssd_bwd(xbc, dtr, bias, alog, dsk, sprev_all, dy, d_inner, name):
    S = xbc.shape[0]
    Q, N, G, P = SSM_CHUNK, SSM_STATE, SSM_GROUPS, SSM_HEAD_DIM
    gw = d_inner // G
    R = gw // P
    nc = S // Q

    def body(conv_ref, dtr_ref, bias_ref, alog_ref, d_ref, sp_ref, dy_ref,
             dx_ref, db_ref, dc_ref, ddt_ref, dbias_ref, dalog_ref, dd_ref, ds_scr):
        first = pl.program_id(0) == 0
        for g in range(G):
            x, b, c, n = _ssd_windows(g, d_inner, gw)
            one_group(first, _Win(conv_ref, x), _Win(conv_ref, b), _Win(conv_ref, c), _Win(dtr_ref, x),
                      _Win(bias_ref, g), _Win(alog_ref, g), _Win(d_ref, g), _Win(sp_ref, g), _Win(dy_ref, x),
                      _Win(dx_ref, x), _Win(db_ref, n), _Win(dc_ref, n), _Win(ddt_ref, x), _Win(dbias_ref, g),
                      _Win(dalog_ref, g), _Win(dd_ref, g), _Win(ds_scr, g))

    def one_group(first, x_ref, b_ref, c_ref, dtr_ref, bias_ref, alog_ref, d_ref, sp_ref, dy_ref,
                  dx_ref, db_ref, dc_ref, ddt_ref, dbias_ref, dalog_ref, dd_ref, ds_scr):
        @pl.when(first)
        def _():
            ds_scr[...] = jnp.zeros((gw, N), F32)
            dbias_ref[...] = jnp.zeros((1, gw), F32)
            dalog_ref[...] = jnp.zeros((1, gw), F32)
            dd_ref[...] = jnp.zeros((1, gw), F32)

        X, pre, dt, a, cs, cs_last, row = _ssd_common(x_ref, dtr_ref, bias_ref, alog_ref, gw)
        Bm, Cm = _silu(b_ref[...]), _silu(c_ref[...])
        dY = dy_ref[...]
        sprev = sp_ref[...]
        dsn = ds_scr[...]
        xdt = X * dt
        lane = lax.broadcasted_iota(jnp.int32, (Q, gw), 1)
        lane1 = lax.broadcasted_iota(jnp.int32, (1, gw), 1)
        srow = lax.broadcasted_iota(jnp.int32, (gw, N), 0)
        ecs = jnp.exp(cs)
        dec = jnp.exp(cs_last - cs)
        cd = jnp.exp(jnp.broadcast_to(cs_last, (Q, gw)).T)
        dd_ref[...] += jnp.sum(dY * X, axis=0, keepdims=True)
        dX = d_ref[...] * dY
        ey = ecs * dY
        dcs = ey * _dot_nt(Cm, sprev)
        dC = _dot_nn(ey, sprev)
        ds_scr[...] = cd * dsn + _dot_tn(ey, Cm)
        wmat = _dot_nt(Bm, dsn)
        dxdt = dec * wmat
        xd = xdt * dec
        dB = _dot_nn(xd, dsn)
        ddec = xdt * wmat * dec
        dcs = dcs - ddec
        dlast = jnp.sum(ddec, axis=0, keepdims=True)
        qmat = dsn * sprev * cd
        cb = _dot_nt(Cm, Bm)
        dcb = jnp.zeros((Q, Q), F32)
        dcs_rep = jnp.zeros((Q, gw), F32)
        dtx_rep = jnp.zeros((Q, gw), F32)
        hms = [(lane >= r * P) & (lane < (r + 1) * P) for r in range(R)]
        decs = [_head_decay(cs, hm) for hm in hms]
        dyrs = [jnp.where(hm, dY, 0.0) for hm in hms]
        graws = [_dot_nt(dyr, xdt) for dyr in dyrs]
        backs = [_dot_tn(cb * dec_l, dyr) for (dec_l, _), dyr in zip(decs, dyrs)]
        for r in range(R):
            hm, (dec_l, tril) = hms[r], decs[r]
            gmat = jnp.where(tril, graws[r], 0.0)
            dcb = dcb + gmat * dec_l
            e = gmat * cb * dec_l
            v = (jnp.sum(e, axis=1, keepdims=True) - jnp.sum(e.T, axis=1, keepdims=True)
                 + jnp.sum(jnp.where(hm, dcs, 0.0), axis=1, keepdims=True))
            dxdt = dxdt + backs[r]
            hm1 = (lane1 >= r * P) & (lane1 < (r + 1) * P)
            t_last = (jnp.sum(jnp.where(hm1, dlast, 0.0), axis=1, keepdims=True)
                      + jnp.sum(jnp.where((srow >= r * P) & (srow < (r + 1) * P), qmat, 0.0), keepdims=True))
            dcs_rep = dcs_rep + jnp.where(hm, v, 0.0) + jnp.where(hm & (row == Q - 1), t_last, 0.0)
        for r in range(R):
            hm = (lane >= r * P) & (lane < (r + 1) * P)
            w_r = jnp.sum(jnp.where(hm, dxdt * X, 0.0), axis=1, keepdims=True)
            dtx_rep = dtx_rep + jnp.where(hm, w_r, 0.0)
        dadt = _rev_cumsum_rows(dcs_rep)
        ddt = a * dadt + dtx_rep
        dalog_ref[...] += jnp.sum(dt * dadt, axis=0, keepdims=True) * a
        draw = ddt * _sigmoid(pre)
        ddt_ref[...] = draw
        dbias_ref[...] += jnp.sum(draw, axis=0, keepdims=True)
        dx_ref[...] = dX + dxdt * dt
        db_ref[...] = dB + _dot_tn(dcb, Cm)
        dc_ref[...] = dC + _dot_nn(dcb, Bm)

    conv_dim = xbc.shape[1]
    rev = lambda c: nc - 1 - c
    row = pl.BlockSpec((Q, d_inner), lambda c: (rev(c), 0))
    n_spec = pl.BlockSpec((Q, G * N), lambda c: (rev(c), 0))
    s_spec = pl.BlockSpec((None, G, gw, N), lambda c: (rev(c), 0, 0, 0))
    p_spec = _full((G, 1, gw))
    gshape = jax.ShapeDtypeStruct((G, 1, gw), F32)
    return pl.pallas_call(
        body, grid=(nc,),
        in_specs=[pl.BlockSpec((Q, conv_dim), lambda c: (rev(c), 0)), row, p_spec, p_spec, p_spec, s_spec, row],
        out_specs=[row, n_spec, n_spec, row, p_spec, p_spec, p_spec],
        out_shape=[jax.ShapeDtypeStruct((S, d_inner), F32), jax.ShapeDtypeStruct((S, G * N), F32),
                   jax.ShapeDtypeStruct((S, G * N), F32), jax.ShapeDtypeStruct((S, d_inner), F32),
                   gshape, gshape, gshape],
        scratch_shapes=[pltpu.VMEM((G, gw, N), F32)],
        compiler_params=_cp("arbitrary"), name=name)(xbc, dtr, bias, alog, dsk, sprev_all, dy)


def _gnorm_fwd(y, zx, w, name):
    S, d_inner = y.shape
    G = SSM_GROUPS
    gw = d_inner // G
    T = _pick(S, 256, 8)

    def body(y_ref, z_ref, w_ref, o_ref):
        for k in range(G):
            sl = slice(k * gw, (k + 1) * gw)
            z = z_ref[:, sl]
            gk = y_ref[:, sl] * z * _sigmoid(z)
            r = lax.rsqrt(jnp.mean(gk * gk, axis=-1, keepdims=True) + NORM_EPS)
            o_ref[:, sl] = (gk * r * w_ref[:, sl]).astype(o_ref.dtype)

    row = pl.BlockSpec((T, d_inner), lambda i: (i, 0))
    vec = pl.BlockSpec((1, d_inner), lambda i: (0, 0))
    return pl.pallas_call(body, grid=(S // T,), in_specs=[row, row, vec], out_specs=row,
                          out_shape=jax.ShapeDtypeStruct((S, d_inner), BF16),
                          compiler_params=_cp("parallel"), name=name)(y, zx, w.reshape(1, d_inner))


def _gnorm_bwd(y, zx, w, dout, name):
    S, d_inner = y.shape
    G = SSM_GROUPS
    gw = d_inner // G
    T = _pick(S, 256, 8)

    def body(y_ref, z_ref, w_ref, d_ref, dy_ref, dz_ref, dw_ref):
        @pl.when(pl.program_id(0) == 0)
        def _():
            dw_ref[...] = jnp.zeros_like(dw_ref)

        for k in range(G):
            sl = slice(k * gw, (k + 1) * gw)
            z, yv, d = z_ref[:, sl], y_ref[:, sl], d_ref[:, sl]
            sg = _sigmoid(z)
            sz = z * sg
            gk = yv * sz
            r = lax.rsqrt(jnp.mean(gk * gk, axis=-1, keepdims=True) + NORM_EPS)
            gh = gk * r
            dw_ref[:, sl] += jnp.sum(d * gh, axis=0, keepdims=True)
            dg = d * w_ref[:, sl]
            dgk = r * (dg - gh * jnp.mean(dg * gh, axis=-1, keepdims=True))
            dy_ref[:, sl] = dgk * sz
            dz_ref[:, sl] = dgk * yv * sg * (1.0 + z * (1.0 - sg))

    row = pl.BlockSpec((T, d_inner), lambda i: (i, 0))
    vec = pl.BlockSpec((1, d_inner), lambda i: (0, 0))
    return pl.pallas_call(
        body, grid=(S // T,), in_specs=[row, row, vec, row], out_specs=[row, row, vec],
        out_shape=[jax.ShapeDtypeStruct((S, d_inner), F32)] * 2 + [jax.ShapeDtypeStruct((1, d_inner), F32)],
        compiler_params=_cp("arbitrary"), name=name)(y, zx, w.reshape(1, d_inner), dout)


def _adam_math(g, w, m, v):
    m = ADAM_B1 * m + (1.0 - ADAM_B1) * g
    v = ADAM_B2 * v + (1.0 - ADAM_B2) * (g * g)
    m_hat = m / (1.0 - ADAM_B1 ** ADAM_STEP)
    v_hat = v / (1.0 - ADAM_B2 ** ADAM_STEP)
    delta = -ADAM_LR * (m_hat / (jnp.sqrt(v_hat) + ADAM_EPS) + ADAM_WD * w)
    return delta, m, v


def _adamw_big(own, sib, w, m, v, layer, prev, name):
    L, A, Bc = w.shape
    T = _pick(A, max(8, (1 << 19) // (4 * Bc)), 16)

    def body(o_ref, s_ref, w_ref, m_ref, v_ref, *rest):
        g_ref, d_ref, nm_ref, nv_ref = rest[-4:]
        so = o_ref[0].astype(F32)
        ss = s_ref[0].astype(F32)
        for k in range(1, N_CHIPS):
            so = so + o_ref[k].astype(F32)
            ss = ss + s_ref[k].astype(F32)
        g = so + ss
        delta, nm, nv = _adam_math(g, w_ref[...], m_ref[...], v_ref[...])
        g_ref[...] = g
        d_ref[...] = delta
        nm_ref[...] = nm
        nv_ref[...] = nv

    part = pl.BlockSpec((N_CHIPS, T, Bc), lambda i: (0, i, 0))
    blk = pl.BlockSpec((None, T, Bc), lambda i: (layer, i, 0))
    shp = jax.ShapeDtypeStruct(w.shape, F32)
    in_specs, args, kwargs = [part, part, blk, blk, blk], [own, sib, w, m, v], {}
    if prev is not None:
        in_specs += [pl.BlockSpec(memory_space=pl.ANY)] * 4
        args += list(prev)
        kwargs["input_output_aliases"] = {5 + q: q for q in range(4)}
    return pl.pallas_call(body, grid=(A // T,), in_specs=in_specs, out_specs=[blk] * 4, out_shape=[shp] * 4,
                          compiler_params=_cp("parallel"), name=name, **kwargs)(*args)


def _sum_devices(parts, name):
    _, R, C = parts.shape

    def body(p_ref, o_ref):
        acc = p_ref[0]
        for k in range(1, N_DEV):
            acc = acc + p_ref[k]
        o_ref[...] = acc

    return pl.pallas_call(body, out_shape=jax.ShapeDtypeStruct((R, C), F32), name=name)(parts)


def _adamw_small(g, w, m, v, name):
    def body(g_ref, w_ref, m_ref, v_ref, d_ref, nm_ref, nv_ref):
        delta, nm, nv = _adam_math(g_ref[...], w_ref[...], m_ref[...], v_ref[...])
        d_ref[...] = delta
        nm_ref[...] = nm
        nv_ref[...] = nv

    shp = jax.ShapeDtypeStruct(g.shape, F32)
    return pl.pallas_call(body, out_shape=[shp] * 3, name=name)(g, w, m, v)


PACK_COLS = 1024


def _pack(arrs):
    flat = jnp.concatenate([a.reshape(-1).astype(F32) for a in arrs])
    n = flat.shape[0]
    rows = -(-n // (8 * PACK_COLS)) * 8
    return jnp.pad(flat, (0, rows * PACK_COLS - n)).reshape(rows, PACK_COLS)


def _unpack(packed, shapes):
    flat = packed.reshape(-1)
    out, off = [], 0
    for s in shapes:
        n = math.prod(s)
        out.append(flat[off:off + n].reshape(s))
        off += n
    return out


def _shard_ref(ref, kind, k, n):
    if kind == "col":
        return ref.at[:, pl.ds(pl.multiple_of(k * n, 128), n)]
    if kind == "row":
        return ref.at[pl.ds(pl.multiple_of(k * n, 16), n), :]
    return ref.at[k]


def _chip_peers():
    x, y, c = lax.axis_index("x"), lax.axis_index("y"), lax.axis_index("c")
    return x, y, c, [(1 - x, y), (x, 1 - y), (1 - x, 1 - y)]


class _Exchange:
    def __init__(self, mode, items):
        self.mode, self.items = mode, items
        self.arrays = []
        for it in items:
            if not any(it[0] is a for a in self.arrays):
                self.arrays.append(it[0])
        self.src_idx = [next(i for i, a in enumerate(self.arrays) if a is it[0]) for it in items]
        self.out_shapes = [jax.ShapeDtypeStruct(it[-1], it[0].dtype) for it in items]
        n = len(items)
        if mode == "swap":
            self.scratch = [pltpu.SemaphoreType.DMA((n,)), pltpu.SemaphoreType.DMA((n,))]
        else:
            self.scratch = [pltpu.SemaphoreType.DMA((3 * n,)), pltpu.SemaphoreType.DMA((3 * n,)),
                            pltpu.SemaphoreType.DMA((n,))]

    def _copies(self, ins, outs, sems):
        if self.mode == "swap":
            send_sems, recv_sems = sems
            x, y, c = lax.axis_index("x"), lax.axis_index("y"), lax.axis_index("c")
            sent = [pltpu.make_async_remote_copy(
                src_ref=ins[self.src_idx[t]], dst_ref=outs[t], send_sem=send_sems.at[t], recv_sem=recv_sems.at[t],
                device_id=(x, y, 1 - c), device_id_type=MESH) for t in range(len(self.items))]
            return [], sent, sent
        send_sems, recv_sems, loc_sems = sems
        x, y, c, peers = _chip_peers()
        me = 2 * x + y
        local, sent, arriving = [], [], []
        for t, it in enumerate(self.items):
            src_arr = ins[self.src_idx[t]]
            if self.mode == "gather":
                _, layer, kind, n, _ = it
                src = src_arr if layer is None else src_arr.at[layer]
                src_for = lambda k: src
                dst_from = lambda k: _shard_ref(outs[t], kind, k, n)
            else:
                _, kind, n, _ = it
                src_for = lambda k: _shard_ref(src_arr, kind, k, n)
                dst_from = lambda k: outs[t].at[k]
            local.append(pltpu.make_async_copy(src_for(me), dst_from(me), loc_sems.at[t]))
            for j, (px, py) in enumerate(peers):
                pk = 2 * px + py
                args = dict(send_sem=send_sems.at[3 * t + j], recv_sem=recv_sems.at[3 * t + j],
                            device_id=(px, py, c), device_id_type=MESH)
                sent.append(pltpu.make_async_remote_copy(src_ref=src_for(pk), dst_ref=dst_from(me), **args))
                arriving.append(pltpu.make_async_remote_copy(src_ref=src_for(pk), dst_ref=dst_from(pk), **args))
        return local, sent, arriving

    def start(self, ins, outs, sems):
        local, sent, arriving = self._copies(ins, outs, sems)
        for cp in local + sent:
            cp.start()
        for cp in arriving:
            cp._used = True

    def wait(self, ins, outs, sems):
        local, sent, arriving = self._copies(ins, outs, sems)
        for cp in arriving:
            cp.wait_recv()
        for cp in sent:
            cp.wait_send()
        for cp in local:
            cp.wait()


class _Multi:
    def __init__(self, parts):
        self.parts = parts
        self.arrays = [a for p in parts for a in p.arrays]
        self.out_shapes = [s for p in parts for s in p.out_shapes]
        self.scratch = [s for p in parts for s in p.scratch]

    def _split(self, ins, outs, sems):
        i = o = s = 0
        for p in self.parts:
            ni, no, ns = len(p.arrays), len(p.out_shapes), len(p.scratch)
            yield p, ins[i:i + ni], outs[o:o + no], sems[s:s + ns]
            i, o, s = i + ni, o + no, s + ns

    def start(self, ins, outs, sems):
        for p, a, b, c in self._split(ins, outs, sems):
            p.start(a, b, c)

    def wait(self, ins, outs, sems):
        for p, a, b, c in self._split(ins, outs, sems):
            p.wait(a, b, c)


def _run_exchange(ex, name):
    nin, nout = len(ex.arrays), len(ex.out_shapes)

    def body(*refs):
        ins, outs, sems = refs[:nin], refs[nin:nin + nout], refs[nin + nout:]
        ex.start(ins, outs, sems)
        ex.wait(ins, outs, sems)

    anyspec = pl.BlockSpec(memory_space=pl.ANY)
    return pl.pallas_call(body, in_specs=[anyspec] * nin, out_specs=[anyspec] * nout, out_shape=ex.out_shapes,
                          scratch_shapes=ex.scratch, name=name)(*ex.arrays)


def _start_exchange(ex, name, after=None):
    nin, nout, nsem = len(ex.arrays), len(ex.out_shapes), len(ex.scratch)
    hbm = pl.BlockSpec(memory_space=pltpu.HBM)
    sem = pl.BlockSpec(memory_space=pltpu.SEMAPHORE)

    n_operands = nin + nout + (after is not None)

    def body(*refs):
        ins, lands = refs[:nin], refs[nin:nin + nout]
        sems = refs[n_operands:n_operands + nsem]
        ex.start(ins, lands, sems)
        refs[-1][...] = jnp.zeros_like(refs[-1])

    args = [pltpu.with_memory_space_constraint(a, pltpu.HBM) for a in ex.arrays]
    args += [pltpu.with_memory_space_constraint(lax.empty(s.shape, s.dtype), pltpu.HBM) for s in ex.out_shapes]
    thru = [pltpu.HBM(a.shape, a.dtype) for a in ex.arrays] + [pltpu.HBM(s.shape, s.dtype) for s in ex.out_shapes]
    extra = [] if after is None else [pl.BlockSpec(memory_space=pl.ANY)]
    args += [] if after is None else [after]
    return pl.pallas_call(
        body, name=name, in_specs=[hbm] * (nin + nout) + extra,
        out_shape=tuple(ex.scratch) + tuple(thru) + (jax.ShapeDtypeStruct((8, 128), F32),),
        out_specs=tuple([sem] * nsem + [hbm] * (nin + nout) + [pl.BlockSpec(memory_space=pltpu.VMEM)]),
        input_output_aliases={q: nsem + q for q in range(nin + nout)},
        compiler_params=pltpu.CompilerParams(has_side_effects=pltpu.SideEffectType.DATAFLOW_SIDE_EFFECTING))(*args)


def _finish_exchange(ex, handles, after, name):
    nin, nout, nsem = len(ex.arrays), len(ex.out_shapes), len(ex.scratch)
    hbm = pl.BlockSpec(memory_space=pltpu.HBM)
    sem = pl.BlockSpec(memory_space=pltpu.SEMAPHORE)
    sems, thru = handles[:nsem], handles[nsem:nsem + nin + nout]

    def body(*refs):
        ins, lands = refs[:nin], refs[nin:nin + nout]
        ex.wait(ins, lands, refs[nin + nout:nin + nout + nsem])

    outs = pl.pallas_call(
        body, name=name, in_specs=[hbm] * (nin + nout) + [sem] * nsem + [pl.BlockSpec(memory_space=pl.ANY)],
        out_shape=tuple(pltpu.HBM(t.shape, t.dtype) for t in thru), out_specs=tuple([hbm] * (nin + nout)),
        input_output_aliases={q: q for q in range(nin + nout)},
        compiler_params=pltpu.CompilerParams(has_side_effects=pltpu.SideEffectType.DATAFLOW_SIDE_EFFECTING))(
            *thru, *sems, after)
    return outs[nin:]


def _all_gather_devices(v, name):
    def body(v_ref, o_ref, send_sems, recv_sems, loc_sem):
        x, y, c = lax.axis_index("x"), lax.axis_index("y"), lax.axis_index("c")
        me = 4 * x + 2 * y + c
        lc = pltpu.make_async_copy(v_ref, o_ref.at[me], loc_sem)
        lc.start()
        rel = [(bx, by, bc) for bx in (0, 1) for by in (0, 1) for bc in (0, 1)][1:]
        copies = []
        for j, (bx, by, bc) in enumerate(rel):
            px, py, pc = x ^ bx, y ^ by, c ^ bc
            copies.append((pltpu.make_async_remote_copy(
                src_ref=v_ref, dst_ref=o_ref.at[me], send_sem=send_sems.at[j], recv_sem=recv_sems.at[j],
                device_id=(px, py, pc), device_id_type=MESH), 4 * px + 2 * py + pc))
        for cp, _ in copies:
            cp.start()
        for j, (cp, pid) in enumerate(copies):
            pltpu.make_async_remote_copy(
                src_ref=v_ref, dst_ref=o_ref.at[pid], send_sem=send_sems.at[j], recv_sem=recv_sems.at[j],
                device_id=(x, y, c), device_id_type=MESH).wait_recv()
        for cp, _ in copies:
            cp.wait_send()
        lc.wait()

    anyspec = pl.BlockSpec(memory_space=pl.ANY)
    return pl.pallas_call(
        body, in_specs=[anyspec], out_specs=anyspec,
        out_shape=jax.ShapeDtypeStruct((N_DEV,) + v.shape, v.dtype),
        scratch_shapes=[pltpu.SemaphoreType.DMA((N_DEV - 1,)), pltpu.SemaphoreType.DMA((N_DEV - 1,)),
                        pltpu.SemaphoreType.DMA(())],
        name=name)(v)


BIG = ("attn_w_qkv", "attn_w_o", "ssm_w_in", "ssm_w_out", "ffn_w_up", "ffn_w_down")
BIG_KIND = {"attn_w_qkv": "col", "attn_w_o": "row", "ssm_w_in": "lead", "ssm_w_out": "row",
            "ffn_w_up": "col", "ffn_w_down": "row"}
SMALL_SHARDED = {"ssm_conv_w": 2, "ssm_conv_b": 1, "ssm_norm_w": 1, "ffn_conv_w": 2}
SMALL = ("mix_norm_w", "ssm_conv_w", "ssm_conv_b", "ssm_dt_bias", "ssm_a_log", "ssm_d", "ssm_norm_w",
         "ffn_norm_w", "ffn_conv_w", "ffn_conv_b", "final_norm_w")
WEIGHTS = ("mix_norm_w", "attn_w_qkv", "attn_w_o", "ssm_w_in", "ssm_conv_w", "ssm_conv_b", "ssm_dt_bias",
           "ssm_a_log", "ssm_d", "ssm_norm_w", "ssm_w_out", "ffn_norm_w", "ffn_w_up", "ffn_conv_w",
           "ffn_conv_b", "ffn_w_down", "final_norm_w")


def _shard_extent(name, shape):
    _, a, b = shape
    return {"col": b, "row": a, "lead": 1}[BIG_KIND[name]]


def _gather_item(w16, name):
    a, b = w16.shape
    kind = BIG_KIND[name]
    full = {"col": (a, N_CHIPS * b), "row": (N_CHIPS * a, b), "lead": (N_CHIPS, a, b)}[kind]
    return (w16, None, kind, _shard_extent(name, (1, a, b)), full)


def _layer_weights(i):
    j = i // 2
    mixer = [("attn_w_qkv", j), ("attn_w_o", j)] if i % 2 == 0 else [("ssm_w_in", j), ("ssm_w_out", j)]
    return mixer + [("ffn_w_up", i), ("ffn_w_down", i)]


def kernel(x, mix_norm_w, attn_w_qkv, attn_w_o, ssm_w_in, ssm_conv_w, ssm_conv_b, ssm_dt_bias, ssm_a_log, ssm_d, ssm_norm_w, ssm_w_out, ffn_norm_w, ffn_w_up, ffn_conv_w, ffn_conv_b, ffn_w_down, final_norm_w, loss_target, m_mix_norm_w, m_attn_w_qkv, m_attn_w_o, m_ssm_w_in, m_ssm_conv_w, m_ssm_conv_b, m_ssm_dt_bias, m_ssm_a_log, m_ssm_d, m_ssm_norm_w, m_ssm_w_out, m_ffn_norm_w, m_ffn_w_up, m_ffn_conv_w, m_ffn_conv_b, m_ffn_w_down, m_final_norm_w, v_mix_norm_w, v_attn_w_qkv, v_attn_w_o, v_ssm_w_in, v_ssm_conv_w, v_ssm_conv_b, v_ssm_dt_bias, v_ssm_a_log, v_ssm_d, v_ssm_norm_w, v_ssm_w_out, v_ffn_norm_w, v_ffn_w_up, v_ffn_conv_w, v_ffn_conv_b, v_ffn_w_down, v_final_norm_w):
    W = dict(mix_norm_w=mix_norm_w, attn_w_qkv=attn_w_qkv, attn_w_o=attn_w_o, ssm_w_in=ssm_w_in,
             ssm_conv_w=ssm_conv_w, ssm_conv_b=ssm_conv_b, ssm_dt_bias=ssm_dt_bias, ssm_a_log=ssm_a_log,
             ssm_d=ssm_d, ssm_norm_w=ssm_norm_w, ssm_w_out=ssm_w_out, ffn_norm_w=ffn_norm_w, ffn_w_up=ffn_w_up,
             ffn_conv_w=ffn_conv_w, ffn_conv_b=ffn_conv_b, ffn_w_down=ffn_w_down, final_norm_w=final_norm_w)
    M = dict(mix_norm_w=m_mix_norm_w, attn_w_qkv=m_attn_w_qkv, attn_w_o=m_attn_w_o, ssm_w_in=m_ssm_w_in,
             ssm_conv_w=m_ssm_conv_w, ssm_conv_b=m_ssm_conv_b, ssm_dt_bias=m_ssm_dt_bias, ssm_a_log=m_ssm_a_log,
             ssm_d=m_ssm_d, ssm_norm_w=m_ssm_norm_w, ssm_w_out=m_ssm_w_out, ffn_norm_w=m_ffn_norm_w,
             ffn_w_up=m_ffn_w_up, ffn_conv_w=m_ffn_conv_w, ffn_conv_b=m_ffn_conv_b, ffn_w_down=m_ffn_w_down,
             final_norm_w=m_final_norm_w)
    V = dict(mix_norm_w=v_mix_norm_w, attn_w_qkv=v_attn_w_qkv, attn_w_o=v_attn_w_o, ssm_w_in=v_ssm_w_in,
             ssm_conv_w=v_ssm_conv_w, ssm_conv_b=v_ssm_conv_b, ssm_dt_bias=v_ssm_dt_bias, ssm_a_log=v_ssm_a_log,
             ssm_d=v_ssm_d, ssm_norm_w=v_ssm_norm_w, ssm_w_out=v_ssm_w_out, ffn_norm_w=v_ffn_norm_w,
             ffn_w_up=v_ffn_w_up, ffn_conv_w=v_ffn_conv_w, ffn_conv_b=v_ffn_conv_b, ffn_w_down=v_ffn_w_down,
             final_norm_w=v_final_norm_w)

    S, D = x.shape[1], x.shape[2]
    xs = x.reshape(S, D)
    tgt = loss_target.reshape(S, D)
    depth = mix_norm_w.shape[0]
    heads = attn_w_o.shape[1] * N_CHIPS // HEAD_DIM
    AW = heads * HEAD_DIM
    d_inner = ssm_w_out.shape[1] * N_CHIPS
    ssm_heads = d_inner // SSM_HEAD_DIM
    G, P, N = SSM_GROUPS, SSM_HEAD_DIM, SSM_STATE
    gw = d_inner // G
    conv_dim = d_inner + 2 * G * N
    in_w = d_inner + conv_dim + ssm_heads
    in_pad = -(-in_w // 128) * 128
    shard_in = ssm_w_in.shape[2]
    xi, yi = lax.axis_index("x"), lax.axis_index("y")
    chip = 2 * xi + yi

    full = {}

    def land(keys, outs):
        for (n, l), o in zip(keys, outs):
            if n == "ssm_w_in":
                o = jnp.pad(jnp.concatenate([o[k] for k in range(N_CHIPS)], axis=1), ((0, 0), (0, in_pad - in_w)))
            full[(n, l)] = o

    def gather_ex(keys, extra=()):
        return _Exchange("gather", [_gather_item(W[n][l].astype(BF16), n) for n, l in keys] + list(extra))

    def fwd_mm(a, b, name, resid=None):
        return _matmul(a, b, "nn", F32, name, resid=resid)

    sm_names = list(SMALL_SHARDED)
    packed = _pack([W[n] for n in sm_names])
    lw = [_layer_weights(i) for i in range(depth)]
    first = lw[0][:1]
    got = _run_exchange(gather_ex(first, [(packed, None, "lead", 1, (N_CHIPS,) + packed.shape)]), "gather_first")
    land(first, got[:-1])
    batches = [lw[0][1:]] + lw[1:]
    pending = []
    for q, keys in enumerate(batches):
        ex = gather_ex(keys)
        pending.append((keys, ex, _start_exchange(ex, f"gather_start_{q}", after=got[-1])))
    issued = sum(handles[-1][0, 0] for _, _, handles in pending)

    def arrive(q, after):
        keys, ex, handles = pending[q]
        land(keys, _finish_exchange(ex, handles, after, f"gather_wait_{q}"))
    per_chip = [_unpack(got[-1][k], [W[n].shape for n in sm_names]) for k in range(N_CHIPS)]
    for q, n in enumerate(sm_names):
        full[n] = jnp.concatenate([per_chip[k][q] for k in range(N_CHIPS)], axis=SMALL_SHARDED[n])
    tab = _perm_tokens(_rope_table(S))

    def rep_heads(p):
        return jnp.repeat(p, P).reshape(G, 1, gw)

    saved = []
    cur = xs
    for i in range(depth):
        j = i // 2
        if i > 0:
            arrive(i, cur)
        sv = {"x_in": cur}
        h = _rms_fwd(cur, mix_norm_w[i] + issued if i == 0 else mix_norm_w[i], f"mix_norm_fwd_{i}")
        sv["h"] = h
        if i % 2 == 0:
            h = _perm_tokens(h)
            sv["h"] = h
            qkv = _matmul(h, full[("attn_w_qkv", j)], "nn", F32, f"qkv_fwd_{i}", rope=(tab, AW))
            if i == 0:
                arrive(0, qkv)
            og = [_attn_fwd(qkv, g, heads, f"attn_fwd_{i}_{g}") for g in range(3)]
            o, lse = _attn_combine([a for a, _ in og], [b for _, b in og], f"attn_combine_{i}")
            mixed = _unperm_tokens(fwd_mm(o, full[("attn_w_o", j)], f"attn_out_fwd_{i}"))
            sv.update(qkv=qkv, o=o, lse=lse)
        else:
            zx = fwd_mm(h, full[("ssm_w_in", j)], f"ssm_in_fwd_{i}")
            conv = _ssm_conv_fwd(zx, full["ssm_conv_w"][j], full["ssm_conv_b"][j], d_inner, conv_dim,
                                 f"ssm_conv_fwd_{i}")
            dtr = jnp.repeat(zx[:, d_inner + conv_dim:in_w], P, axis=1)
            prm = [rep_heads(p[j]) for p in (ssm_dt_bias, ssm_a_log, ssm_d)]
            y, sprev = _ssd_fwd(conv, dtr, *prm, d_inner, f"ssd_fwd_{i}")
            gated = _gnorm_fwd(y, zx, full["ssm_norm_w"][j], f"ssm_norm_fwd_{i}")
            cur = fwd_mm(gated, full[("ssm_w_out", j)], f"ssm_out_fwd_{i}", resid=cur)
            sv.update(zx=zx, conv=conv, dtr=dtr, prm=prm, y=y, sprev=sprev, gated=gated)
            mixed = None
        if mixed is None:
            h2 = _rms_fwd(cur, ffn_norm_w[i], f"ffn_norm_fwd_{i}")
        else:
            cur, h2 = _rms_fwd(cur, ffn_norm_w[i], f"ffn_norm_fwd_{i}", add=mixed)
        sv["x_mid"] = cur
        up = fwd_mm(h2, full[("ffn_w_up", i)], f"ffn_up_fwd_{i}")
        act = _ffn_conv_fwd(up, full["ffn_conv_w"][i], ffn_conv_b[i], f"ffn_conv_fwd_{i}")
        cur = fwd_mm(act, full[("ffn_w_down", i)], f"ffn_down_fwd_{i}", resid=cur)
        sv.update(h2=h2, up=up, act=act)
        saved.append(sv)

    dx, d_final, loss_part = _loss_head(cur, final_norm_w, tgt, "loss_head")
    gbig, recv = {}, {}
    gs = {n: [None] * W[n].shape[0] for n in SMALL if n != "final_norm_w"}

    def scatter_ex(keys):
        return _Exchange("scatter", [(gbig[(n, l)], BIG_KIND[n], _shard_extent(n, W[n].shape),
                                      (N_CHIPS,) + W[n].shape[1:]) for n, l in keys])

    sib = {}

    def swap_ex(keys):
        return _Exchange("swap", [(recv[k], recv[k].shape) for k in keys])

    def bwd_mm(a, b, mode, dtype, name, send=(), swap=()):
        if not send and not swap:
            return _matmul(a, b, mode, dtype, name)
        parts = ([scatter_ex(send)] if send else []) + ([swap_ex(swap)] if swap else [])
        out, got = _matmul(a, b, mode, dtype, name, carry=_Multi(parts))
        recv.update(zip(send, got[:len(send)]))
        sib.update(zip(swap, got[len(send):]))
        return out

    sending = None
    to_swap = []
    for i in reversed(range(depth)):
        j = i // 2
        sv = saved[i]
        k_in, k_out, k_up, k_down = _layer_weights(i)
        own = i == 0

        def now(keys):
            return keys if own else []
        dact = _matmul(dx, full[k_down], "nt", F32, f"ffn_down_dgrad_{i}")
        gbig[k_down] = bwd_mm(sv["act"], dx, "tn", BF16, f"ffn_down_wgrad_{i}", swap=to_swap)
        to_swap = []
        dup, dcw, dcb = _ffn_conv_bwd(dact, sv["up"], full["ffn_conv_w"][i], ffn_conv_b[i], f"ffn_conv_bwd_{i}")
        gs["ffn_conv_w"][i], gs["ffn_conv_b"][i] = dcw, dcb[0]
        dh2 = bwd_mm(dup, full[k_up], "nt", F32, f"ffn_up_dgrad_{i}", send=now([k_down]))
        gbig[k_up] = bwd_mm(sv["h2"], dup, "tn", BF16, f"ffn_up_wgrad_{i}", swap=now([k_down]))
        dx, dnw = _rms_bwd(sv["x_mid"], ffn_norm_w[i], dh2, dx, f"ffn_norm_bwd_{i}")
        gs["ffn_norm_w"][i] = dnw[0]
        if i % 2 == 0:
            dxp = _perm_tokens(dx)
            do = _matmul(dxp, full[k_out], "nt", F32, f"attn_out_dgrad_{i}")
            gbig[k_out] = _matmul(sv["o"], dxp, "tn", BF16, f"attn_out_wgrad_{i}")
            dqkv = None
            for g in range(3):
                dqkv = _attn_bwd(sv["qkv"], tab, sv["o"], sv["lse"], do, dqkv, g, heads, f"attn_bwd_{i}_{g}")
            gbig[k_in] = bwd_mm(sv["h"], dqkv, "tn", BF16, f"qkv_wgrad_{i}", send=now([k_up, k_out]))
            dh = _unperm_tokens(bwd_mm(dqkv, full[k_in], "nt", F32, f"qkv_dgrad_{i}", send=now([k_in]),
                                       swap=now([k_up, k_out])))
        else:
            dgated = _matmul(dx, full[k_out], "nt", F32, f"ssm_out_dgrad_{i}")
            gbig[k_out] = _matmul(sv["gated"], dx, "tn", BF16, f"ssm_out_wgrad_{i}")
            dy, dz, dgw = _gnorm_bwd(sv["y"], sv["zx"], full["ssm_norm_w"][j], dgated, f"ssm_norm_bwd_{i}")
            gs["ssm_norm_w"][j] = dgw[0]
            dxs_, dbm, dcm, ddtr, dbias, dalog, ddsk = _ssd_bwd(
                sv["conv"], sv["dtr"], *sv["prm"], sv["sprev"], dy, d_inner, f"ssd_bwd_{i}")
            gs["ssm_dt_bias"][j] = dbias.reshape(-1)[::P]
            gs["ssm_a_log"][j] = dalog.reshape(-1)[::P]
            gs["ssm_d"][j] = ddsk.reshape(ssm_heads, P).sum(axis=1)
            ddt = jnp.pad(ddtr[:, ::P], ((0, 0), (0, in_pad - in_w)))
            dzx, dcw, dcb = _ssm_conv_bwd(dxs_, dbm, dcm, sv["conv"], sv["zx"], dz, ddt, full["ssm_conv_w"][j],
                                          d_inner, f"ssm_conv_bwd_{i}")
            gs["ssm_conv_w"][j], gs["ssm_conv_b"][j] = dcw, dcb[0]
            dwin = bwd_mm(sv["h"], dzx, "tn", BF16, f"ssm_in_wgrad_{i}", send=now([k_up, k_out]))
            gbig[k_in] = jnp.stack([dwin[:, k * shard_in:(k + 1) * shard_in] for k in range(N_CHIPS)])
            dh = bwd_mm(dzx, full[k_in], "nt", F32, f"ssm_in_dgrad_{i}", send=now([k_in]), swap=now([k_up, k_out]))
        if sending is not None:
            keys, ex, handles = sending
            recv.update(zip(keys, _finish_exchange(ex, handles, dh, f"scatter_wait_{i + 1}")))
            to_swap, sending = keys, None
        if own:
            to_swap = to_swap + [k_in]
        else:
            keys = [k_in, k_out, k_up, k_down]
            ex = scatter_ex(keys)
            sending = (keys, ex, _start_exchange(ex, f"scatter_start_{i}"))
        issued = 0.0 if own else sending[2][-1][0, 0]
        dx, dnw = _rms_bwd(sv["x_in"], mix_norm_w[i] + issued, dh, dx, f"mix_norm_bwd_{i}")
        gs["mix_norm_w"][i] = dnw[0]
    grad_x = dx.reshape(x.shape)

    sib.update(zip(to_swap, _run_exchange(swap_ex(to_swap), "swap_last")))

    small_full = [jnp.stack(gs[n]) if n != "final_norm_w" else d_final[0] for n in SMALL]
    small_full.append(loss_part[0, 0:1])
    small_shapes = [a.shape for a in small_full]
    summed = _sum_devices(_all_gather_devices(_pack(small_full), "gather_small_grads"), "sum_small_grads")
    small_g = _unpack(summed, small_shapes)
    loss = small_g[-1][0]
    gsm = {}
    for n, g in zip(SMALL, small_g[:-1]):
        if n in SMALL_SHARDED:
            ax = SMALL_SHARDED[n]
            ext = W[n].shape[ax]
            g = lax.dynamic_slice_in_dim(g, chip * ext, ext, axis=ax)
        gsm[n] = g

    out_g, out_d, out_m, out_v = {}, {}, {}, {}
    for name in BIG:
        outs = None
        for l in range(W[name].shape[0]):
            outs = _adamw_big(recv[(name, l)], sib[(name, l)], W[name], M[name], V[name], l, outs,
                              f"adamw_{name}_{l}")
        out_g[name], out_d[name], out_m[name], out_v[name] = outs
    shapes = [W[n].shape for n in SMALL]
    pd, pm, pv = _adamw_small(_pack([gsm[n] for n in SMALL]), _pack([W[n] for n in SMALL]),
                              _pack([M[n] for n in SMALL]), _pack([V[n] for n in SMALL]), "adamw_small")
    for n, d_, m_, v_ in zip(SMALL, _unpack(pd, shapes), _unpack(pm, shapes), _unpack(pv, shapes)):
        out_g[n], out_d[n], out_m[n], out_v[n] = gsm[n], d_, m_, v_

    return (loss, grad_x, *[out_g[n] for n in WEIGHTS], *[out_d[n] for n in WEIGHTS],
            *[out_m[n] for n in WEIGHTS], *[out_v[n] for n in WEIGHTS])
```

```python
import functools
import math

import jax
import jax.numpy as jnp
from jax import lax
from jax.experimental import pallas as pl
from jax.experimental.pallas import tpu as pltpu

F32 = jnp.float32
BF16 = jnp.bfloat16
MESH = pl.DeviceIdType.MESH

NORM_EPS = 1e-5
HEAD_DIM = 128
ATTN_BLOCK = 128
ATTN_DILATIONS = (1, 4, 16)
ATTN_WINDOWS = (128, 512, 2048)
PERM = 16
ROPE_THETA = 500000.0
ROPE_HALF = HEAD_DIM // 8
SSM_HEAD_DIM = 64
SSM_STATE = 128
SSM_GROUPS = 8
SSM_CHUNK = 128
NEG = -1e30

ADAM_LR = 0.001
ADAM_B1 = 0.9
ADAM_B2 = 0.999
ADAM_EPS = 1e-08
ADAM_WD = 0.01
ADAM_STEP = 10

VMEM_LIMIT_BYTES = 48 * 1024 * 1024
N_CHIPS = 4
N_DEV = 8


def _cp(*sem):
    return pltpu.CompilerParams(dimension_semantics=sem, vmem_limit_bytes=VMEM_LIMIT_BYTES)


def _pick(n, pref, mult=128):
    best = None
    t = mult
    while t <= min(n, pref):
        if n % t == 0:
            best = t
        t += mult
    return n if best is None else best


def _sigmoid(x):
    return 1.0 / (1.0 + jnp.exp(-x))


def _silu(x):
    return x * _sigmoid(x)


def _softplus(x):
    u = jnp.exp(-jnp.abs(x))
    w = 1.0 + u
    log1p = jnp.where(w == 1.0, u, jnp.log(w) * (u / jnp.where(w == 1.0, 1.0, w - 1.0)))
    return jnp.maximum(x, 0.0) + log1p


def _dot(a, b, dims):
    return lax.dot_general(a.astype(BF16), b.astype(BF16), (dims, ((), ())),
                           preferred_element_type=F32)


def _dot_nn(a, b):
    return _dot(a, b, ((1,), (0,)))


def _dot_nt(a, b):
    return _dot(a, b, ((1,), (1,)))


def _dot_tn(a, b):
    return _dot(a, b, ((0,), (0,)))


MATMUL_VMEM_BYTES = 36 * 1024 * 1024
MATMUL_TILES = (2048, 1536, 1408, 1152, 1024, 896, 768, 640, 512, 384, 256, 128)


def _matmul_tiles(M, N, K, a_bytes, b_bytes, o_bytes, has_resid, fix_tn=None):
    best = None
    for tm in [t for t in MATMUL_TILES if M % t == 0] or [M]:
        for tn in [fix_tn] if fix_tn else [t for t in MATMUL_TILES if N % t == 0] or [N]:
            for tk in [t for t in MATMUL_TILES if K % t == 0 and t <= 1408] or [K]:
                nk, gm, gn = K // tk, M // tm, N // tn
                vmem = 2 * (tm * tk * a_bytes + tk * tn * b_bytes + tm * tn * o_bytes)
                vmem += (2 * tm * tn * 4 if has_resid else 0) + (tm * tn * 4 if nk > 1 else 0)
                if vmem > MATMUL_VMEM_BYTES:
                    continue
                a_reads = 1 if nk == 1 else gn
                b_reads = 1 if (nk == 1 and gn == 1) else gm
                traffic = M * K * a_bytes * a_reads + K * N * b_bytes * b_reads + M * N * o_bytes
                key = (traffic, gm * gn * nk)
                if best is None or key < best[0]:
                    best = (key, (tm, tn, tk))
    assert best is not None, (M, N, K)
    return best[1]


def _matmul(a, b, mode, out_dtype, name, resid=None, carry=None, rope=None):
    if mode == "nn":
        (M, K), (K2, N) = a.shape, b.shape
    elif mode == "nt":
        (M, K), (N, K2) = a.shape, b.shape
    else:
        (K, M), (K2, N) = a.shape, b.shape
    assert K == K2, (a.shape, b.shape, mode)
    tm, tn, tk = _matmul_tiles(M, N, K, a.dtype.itemsize, b.dtype.itemsize, jnp.dtype(out_dtype).itemsize,
                               resid is not None or rope is not None, fix_tn=rope[1] if rope else None)
    nk = K // tk
    assert not (rope and (resid is not None or nk > 1))
    gm, gn = M // tm, N // tn
    dims = {"nn": ((1,), (0,)), "nt": ((1,), (1,)), "tn": ((0,), (0,))}[mode]
    has_resid = resid is not None or rope is not None
    nci = len(carry.arrays) if carry else 0
    nco = len(carry.out_shapes) if carry else 0

    def body(a_ref, b_ref, *rest):
        r_ref = rest[0] if has_resid else None
        rest = rest[has_resid:]
        c_ins, o_ref, c_outs, scratch = rest[:nci], rest[nci], rest[nci + 1:nci + 1 + nco], rest[nci + 1 + nco:]
        acc_ref = scratch[0] if nk > 1 else None
        sems = scratch[nk > 1:]
        i, j, k = pl.program_id(0), pl.program_id(1), pl.program_id(2)
        if carry:
            @pl.when((i == 0) & (j == 0) & (k == 0))
            def _():
                carry.start(c_ins, c_outs, sems)

        if nk == 1:
            r = _dot(a_ref[...], b_ref[...], dims)
            if rope:
                @pl.when(j % 3 < 2)
                def _():
                    table = r_ref[...]
                    for h in range(tn // HEAD_DIM):
                        hs = slice(h * HEAD_DIM, (h + 1) * HEAD_DIM)
                        o_ref[:, hs] = _rope(r[:, hs], table, 1.0).astype(o_ref.dtype)

                @pl.when(j % 3 == 2)
                def _():
                    o_ref[...] = r.astype(o_ref.dtype)
            else:
                if has_resid:
                    r = r + r_ref[...]
                o_ref[...] = r.astype(o_ref.dtype)
        else:
            @pl.when(k == 0)
            def _():
                acc_ref[...] = jnp.zeros_like(acc_ref)

            acc_ref[...] += _dot(a_ref[...], b_ref[...], dims)

            @pl.when(k == nk - 1)
            def _():
                r = acc_ref[...]
                if has_resid:
                    r = r + r_ref[...]
                o_ref[...] = r.astype(o_ref.dtype)

        if carry:
            @pl.when((i == gm - 1) & (j == gn - 1) & (k == nk - 1))
            def _():
                carry.wait(c_ins, c_outs, sems)

    if mode == "nn":
        a_spec = pl.BlockSpec((tm, tk), lambda i, j, k: (i, k))
        b_spec = pl.BlockSpec((tk, tn), lambda i, j, k: (k, j))
    elif mode == "nt":
        a_spec = pl.BlockSpec((tm, tk), lambda i, j, k: (i, k))
        b_spec = pl.BlockSpec((tn, tk), lambda i, j, k: (j, k))
    else:
        a_spec = pl.BlockSpec((tk, tm), lambda i, j, k: (k, i))
        b_spec = pl.BlockSpec((tk, tn), lambda i, j, k: (k, j))
    o_spec = pl.BlockSpec((tm, tn), lambda i, j, k: (i, j))
    anyspec = pl.BlockSpec(memory_space=pl.ANY)
    extra_spec = pl.BlockSpec((tm, 3 * HEAD_DIM), lambda i, j, k: (i, 0)) if rope else o_spec
    in_specs = [a_spec, b_spec] + ([extra_spec] if has_resid else []) + [anyspec] * nci
    extra = (rope[0],) if rope else ((resid,) if has_resid else ())
    args = (a, b) + extra + (tuple(carry.arrays) if carry else ())
    out_shape = [jax.ShapeDtypeStruct((M, N), out_dtype)] + (carry.out_shapes if carry else [])
    scratch = ([] if nk == 1 else [pltpu.VMEM((tm, tn), F32)]) + (carry.scratch if carry else [])
    sem = ("arbitrary",) * 3 if carry else ("parallel", "parallel", "arbitrary")
    outs = pl.pallas_call(
        body, grid=(gm, gn, nk), in_specs=in_specs, out_specs=[o_spec] + [anyspec] * nco,
        out_shape=out_shape, scratch_shapes=scratch, compiler_params=_cp(*sem), name=name)(*args)
    return (outs[0], outs[1:]) if carry else outs[0]


def _rms_fwd(x, w, name, add=None):
    S, D = x.shape
    t = _pick(S, 512, 8)
    has_add = add is not None

    def body(x_ref, w_ref, *rest):
        xv = x_ref[...]
        if has_add:
            xv = xv + rest[0][...]
            rest[1][...] = xv
        r = lax.rsqrt(jnp.mean(xv * xv, axis=-1, keepdims=True) + NORM_EPS)
        rest[-1][...] = (xv * r * w_ref[...]).astype(rest[-1].dtype)

    row = pl.BlockSpec((t, D), lambda i: (i, 0))
    vec = pl.BlockSpec((1, D), lambda i: (0, 0))
    normed = jax.ShapeDtypeStruct((S, D), BF16)
    if not has_add:
        return pl.pallas_call(body, grid=(S // t,), in_specs=[row, vec], out_specs=row, out_shape=normed,
                              compiler_params=_cp("parallel"), name=name)(x, w.reshape(1, D))
    return pl.pallas_call(body, grid=(S // t,), in_specs=[row, vec, row], out_specs=[row, row],
                          out_shape=[jax.ShapeDtypeStruct((S, D), F32), normed],
                          compiler_params=_cp("parallel"), name=name)(x, w.reshape(1, D), add)


def _rms_bwd(x, w, dh, dres, name):
    S, D = x.shape
    t = _pick(S, 512, 8)

    def body(x_ref, w_ref, dh_ref, dr_ref, dx_ref, dw_ref):
        @pl.when(pl.program_id(0) == 0)
        def _():
            dw_ref[...] = jnp.zeros_like(dw_ref)

        xv = x_ref[...]
        r = lax.rsqrt(jnp.mean(xv * xv, axis=-1, keepdims=True) + NORM_EPS)
        xh = xv * r
        dh_v = dh_ref[...]
        g = dh_v * w_ref[...]
        dx_ref[...] = dr_ref[...] + r * (g - xh * jnp.mean(g * xh, axis=-1, keepdims=True))
        dw_ref[...] += jnp.sum(dh_v * xh, axis=0, keepdims=True)

    row = pl.BlockSpec((t, D), lambda i: (i, 0))
    vec = pl.BlockSpec((1, D), lambda i: (0, 0))
    return pl.pallas_call(
        body, grid=(S // t,), in_specs=[row, vec, row, row], out_specs=[row, vec],
        out_shape=[jax.ShapeDtypeStruct((S, D), F32), jax.ShapeDtypeStruct((1, D), F32)],
        compiler_params=_cp("arbitrary"), name=name)(x, w.reshape(1, D), dh, dres)


def _loss_head(x, w, tgt, name):
    S, D = x.shape
    t = _pick(S, 512, 8)

    def body(x_ref, w_ref, t_ref, dx_ref, dw_ref, l_ref):
        @pl.when(pl.program_id(0) == 0)
        def _():
            dw_ref[...] = jnp.zeros_like(dw_ref)
            l_ref[...] = jnp.zeros_like(l_ref)

        xv = x_ref[...]
        wv = w_ref[...]
        r = lax.rsqrt(jnp.mean(xv * xv, axis=-1, keepdims=True) + NORM_EPS)
        xh = xv * r
        err = xh * wv - t_ref[...]
        per_tok = jnp.mean(err * err, axis=-1, keepdims=True)
        l_ref[...] += 0.5 * jnp.sum(per_tok, axis=0, keepdims=True)
        dy = err * (1.0 / D)
        g = dy * wv
        dx_ref[...] = r * (g - xh * jnp.mean(g * xh, axis=-1, keepdims=True))
        dw_ref[...] += jnp.sum(dy * xh, axis=0, keepdims=True)

    row = pl.BlockSpec((t, D), lambda i: (i, 0))
    vec = pl.BlockSpec((1, D), lambda i: (0, 0))
    lspec = pl.BlockSpec((1, 128), lambda i: (0, 0))
    return pl.pallas_call(
        body, grid=(S // t,), in_specs=[row, vec, row], out_specs=[row, vec, lspec],
        out_shape=[jax.ShapeDtypeStruct((S, D), F32), jax.ShapeDtypeStruct((1, D), F32),
                   jax.ShapeDtypeStruct((1, 128), F32)],
        compiler_params=_cp("arbitrary"), name=name)(x, w.reshape(1, D), tgt)


def _rope_table(seq):
    pos = jnp.arange(seq, dtype=F32)
    inv_freq = ROPE_THETA ** (-jnp.arange(0, 2 * ROPE_HALF, 2, dtype=F32) / (2 * ROPE_HALF))
    ang = pos[:, None] * inv_freq[None, :]
    cos, sin = jnp.cos(ang), jnp.sin(ang)
    pad = HEAD_DIM - 2 * ROPE_HALF
    cos_p = jnp.concatenate([cos, cos, jnp.ones((seq, pad), F32)], axis=1)
    sin_a = jnp.concatenate([-sin, jnp.zeros((seq, HEAD_DIM - ROPE_HALF), F32)], axis=1)
    sin_b = jnp.concatenate([jnp.zeros((seq, ROPE_HALF), F32), sin, jnp.zeros((seq, pad), F32)], axis=1)
    return jnp.concatenate([cos_p, sin_a, sin_b], axis=1)


def _rope(t, tab, sign):
    cos_p = tab[:, 0:HEAD_DIM]
    sin_a = tab[:, HEAD_DIM:2 * HEAD_DIM]
    sin_b = tab[:, 2 * HEAD_DIM:3 * HEAD_DIM]
    up = pltpu.roll(t, HEAD_DIM - ROPE_HALF, 1)
    down = pltpu.roll(t, ROPE_HALF, 1)
    return t * cos_p + sign * (up * sin_a + down * sin_b)


def _perm_tokens(a):
    S = a.shape[0]
    return a.reshape(S // PERM, PERM, -1).transpose(1, 0, 2).reshape(S, -1)


def _unperm_tokens(a):
    S = a.shape[0]
    return a.reshape(PERM, S // PERM, -1).transpose(1, 0, 2).reshape(S, -1)


class _Strided:
    def __init__(self, S, dil):
        self.dil, self.m = dil, PERM // dil
        self.c = ATTN_BLOCK // self.m
        self.rows = S // PERM
        self.nb = S // (dil * ATTN_BLOCK)

    def view(self, a):
        return a.reshape(self.m, self.dil, self.rows, a.shape[-1])

    def spec(self, width, col, f=lambda n: n):
        return pl.BlockSpec((self.m, None, self.c, width), lambda r, n: (0, r, f(n), col))

    def load(self, ref, sl=slice(None)):
        if self.m == 1:
            return ref[0, :, sl]
        return jnp.concatenate([ref[q, :, sl] for q in range(self.m)], axis=0)

    def store(self, ref, sl, val):
        for q in range(self.m):
            ref[q, :, sl] = val[q * self.c:(q + 1) * self.c, :]

    def member(self, i):
        shift = self.c.bit_length() - 1
        return (i & (self.c - 1)) * self.m + (i >> shift)


def _attn_fwd(qkv, g, heads, name):
    S = qkv.shape[0]
    W = heads * HEAD_DIM
    dil = ATTN_DILATIONS[g]
    steps = ATTN_WINDOWS[g] // dil
    B = ATTN_BLOCK
    scale = HEAD_DIM ** -0.5
    st = _Strided(S, dil)

    def body(q_ref, k_ref, v_ref, o_ref, l_ref, kp_scr, vp_scr):
        n = pl.program_id(1)

        @pl.when(n == 0)
        def _():
            kp_scr[...] = jnp.zeros_like(kp_scr)
            vp_scr[...] = jnp.zeros_like(vp_scr)

        ii = lax.broadcasted_iota(jnp.int32, (B, 2 * B), 0)
        jj = lax.broadcasted_iota(jnp.int32, (B, 2 * B), 1)
        delta = st.member(ii) - st.member(jj & (B - 1)) + jnp.where(jj >= B, 0, B)
        ok = (delta >= 0) & (delta <= steps) & ((jj >= B) | (n > 0))
        sls = [slice(h * HEAD_DIM, (h + 1) * HEAD_DIM) for h in range(heads)]
        qs = [st.load(q_ref, sl) for sl in sls]
        ks = [st.load(k_ref, sl) for sl in sls]
        vs = [st.load(v_ref, sl) for sl in sls]
        kcs = [jnp.concatenate([kp_scr[:, sl], k], axis=0) for sl, k in zip(sls, ks)]
        ss = [jnp.where(ok, _dot_nt(q, kc) * scale, NEG) for q, kc in zip(qs, kcs)]
        ms = [jnp.max(s, axis=-1, keepdims=True) for s in ss]
        ps = [jnp.exp(s - m) for s, m in zip(ss, ms)]
        dens = [jnp.sum(p, axis=-1, keepdims=True) for p in ps]
        vcs = [jnp.concatenate([vp_scr[:, sl], v], axis=0) for sl, v in zip(sls, vs)]
        outs = [_dot_nn(p, vc) for p, vc in zip(ps, vcs)]
        for sl, o, m, den, k, v in zip(sls, outs, ms, dens, ks, vs):
            st.store(o_ref, sl, o / den)
            st.store(l_ref, sl, jnp.broadcast_to(m + jnp.log(den), (B, HEAD_DIM)))
            kp_scr[:, sl] = k
            vp_scr[:, sl] = v

    qv = st.view(qkv)
    o_spec = st.spec(W, 0)
    o, lse = pl.pallas_call(
        body, grid=(dil, st.nb),
        in_specs=[st.spec(W, g * 3), st.spec(W, g * 3 + 1), st.spec(W, g * 3 + 2)],
        out_specs=[o_spec, o_spec],
        out_shape=[jax.ShapeDtypeStruct((st.m, dil, st.rows, W), F32)] * 2,
        scratch_shapes=[pltpu.VMEM((B, W), F32), pltpu.VMEM((B, W), F32)],
        compiler_params=_cp("parallel", "arbitrary"), name=name)(qv, qv, qv)
    return o.reshape(S, W), lse.reshape(S, W)


def _attn_combine(os_, ls_, name):
    S, W = os_[0].shape
    t = _pick(S, 256, 8)

    def body(o0, o1, o2, l0, l1, l2, o_ref, l_ref):
        a, b, c = l0[...], l1[...], l2[...]
        m = jnp.maximum(jnp.maximum(a, b), c)
        ea, eb, ec = jnp.exp(a - m), jnp.exp(b - m), jnp.exp(c - m)
        tot = ea + eb + ec
        o_ref[...] = (ea * o0[...] + eb * o1[...] + ec * o2[...]) / tot
        l_ref[...] = m + jnp.log(tot)

    row = pl.BlockSpec((t, W), lambda i: (i, 0))
    return pl.pallas_call(body, grid=(S // t,), in_specs=[row] * 6, out_specs=[row, row],
                          out_shape=[jax.ShapeDtypeStruct((S, W), F32)] * 2,
                          compiler_params=_cp("parallel"), name=name)(*os_, *ls_)


def _attn_bwd(qkv, tab, o, lse, do, dqkv_prev, g, heads, name):
    S = qkv.shape[0]
    W = heads * HEAD_DIM
    dil = ATTN_DILATIONS[g]
    steps = ATTN_WINDOWS[g] // dil
    B = ATTN_BLOCK
    scale = HEAD_DIM ** -0.5
    st = _Strided(S, dil)
    nb = st.nb
    aliased = dqkv_prev is not None

    def body(q_ref, qn_ref, k_ref, v_ref, do_ref, don_ref, o_ref, on_ref, l_ref, ln_ref, t_ref, *rest):
        out_ref, kp_scr, vp_scr = rest[-3:]
        n = pl.program_id(1)
        has_next = n < nb - 1

        @pl.when(n == 0)
        def _():
            kp_scr[...] = jnp.zeros_like(kp_scr)
            vp_scr[...] = jnp.zeros_like(vp_scr)

        ia = lax.broadcasted_iota(jnp.int32, (B, 2 * B), 0)
        ja = lax.broadcasted_iota(jnp.int32, (B, 2 * B), 1)
        da = st.member(ia) - st.member(ja & (B - 1)) + jnp.where(ja >= B, 0, B)
        ok_a = (da >= 0) & (da <= steps) & ((ja >= B) | (n > 0))
        ib = lax.broadcasted_iota(jnp.int32, (2 * B, B), 0)
        jb = lax.broadcasted_iota(jnp.int32, (2 * B, B), 1)
        db = st.member(ib & (B - 1)) + jnp.where(ib >= B, B, 0) - st.member(jb)
        ok_b = (db >= 0) & (db <= steps) & ((ib < B) | has_next)
        tb = st.load(t_ref)
        hs = range(heads)
        sls = [slice(h * HEAD_DIM, (h + 1) * HEAD_DIM) for h in hs]
        qr = [st.load(q_ref, sl) for sl in sls]
        qnr = [st.load(qn_ref, sl) for sl in sls]
        kr = [st.load(k_ref, sl) for sl in sls]
        kpr = [kp_scr[:, sl] for sl in sls]
        v = [st.load(v_ref, sl) for sl in sls]
        dov_ = [st.load(do_ref, sl) for sl in sls]
        donv = [st.load(don_ref, sl) for sl in sls]
        dl = [jnp.sum(dov_[h] * st.load(o_ref, sls[h]), axis=-1, keepdims=True) for h in hs]
        dln = [jnp.sum(donv[h] * st.load(on_ref, sls[h]), axis=-1, keepdims=True) for h in hs]
        ls = [st.load(l_ref, sl) for sl in sls]
        kc = [jnp.concatenate([kpr[h], kr[h]], axis=0) for h in hs]
        vc = [jnp.concatenate([vp_scr[:, sls[h]], v[h]], axis=0) for h in hs]
        qc = [jnp.concatenate([qr[h], qnr[h]], axis=0) for h in hs]
        doc = [jnp.concatenate([dov_[h], donv[h]], axis=0) for h in hs]
        lc = [jnp.concatenate([ls[h], st.load(ln_ref, sls[h])], axis=0) for h in hs]
        dlc = [jnp.concatenate([dl[h], dln[h]], axis=0) for h in hs]
        s = [_dot_nt(qr[h], kc[h]) * scale for h in hs]
        dp = [_dot_nt(dov_[h], vc[h]) for h in hs]
        s2 = [_dot_nt(qc[h], kr[h]) * scale for h in hs]
        dp2 = [_dot_nt(doc[h], v[h]) for h in hs]
        p = [jnp.where(ok_a, jnp.exp(jnp.minimum(s[h] - jnp.concatenate([ls[h], ls[h]], axis=1), 30.0)), 0.0)
             for h in hs]
        ds = [p[h] * (dp[h] - dl[h]) * scale for h in hs]
        p2 = [jnp.where(ok_b, jnp.exp(jnp.minimum(s2[h] - lc[h], 30.0)), 0.0) for h in hs]
        ds2 = [p2[h] * (dp2[h] - dlc[h]) * scale for h in hs]
        dq = [_dot_nn(ds[h], kc[h]) for h in hs]
        dk = [_dot_tn(ds2[h], qc[h]) for h in hs]
        dv = [_dot_tn(p2[h], doc[h]) for h in hs]
        for h in hs:
            st.store(out_ref, sls[h], _rope(dq[h], tb, -1.0))
            st.store(out_ref, slice(W + h * HEAD_DIM, W + (h + 1) * HEAD_DIM), _rope(dk[h], tb, -1.0))
            st.store(out_ref, slice(2 * W + h * HEAD_DIM, 2 * W + (h + 1) * HEAD_DIM), dv[h])
            kp_scr[:, sls[h]] = kr[h]
            vp_scr[:, sls[h]] = v[h]

    nxt = lambda n: jnp.minimum(n + 1, nb - 1)
    prv = lambda n: jnp.maximum(n - 1, 0)
    same = lambda n: n
    q0, q1, q2, tw = g * 3, g * 3 + 1, g * 3 + 2, 3 * HEAD_DIM
    in_specs = [st.spec(W, q0), st.spec(W, q0, nxt), st.spec(W, q1), st.spec(W, q2)]
    in_specs += [st.spec(W, 0, f) for f in (same, nxt, same, nxt, same, nxt)]
    in_specs += [st.spec(tw, 0)]
    qv, tv, ov, lv, dov = (st.view(a) for a in (qkv, tab, o, lse, do))
    args = [qv, qv, qv, qv, dov, dov, ov, ov, lv, lv, tv]
    kwargs = {}
    if aliased:
        in_specs.append(pl.BlockSpec(memory_space=pl.ANY))
        args.append(st.view(dqkv_prev))
        kwargs["input_output_aliases"] = {len(args) - 1: 0}
    out = pl.pallas_call(
        body, grid=(dil, nb), in_specs=in_specs, out_specs=st.spec(3 * W, g),
        out_shape=jax.ShapeDtypeStruct((st.m, dil, st.rows, 9 * W), F32),
        scratch_shapes=[pltpu.VMEM((B, W), F32), pltpu.VMEM((B, W), F32)],
        compiler_params=_cp("parallel", "arbitrary"), name=name, **kwargs)(*args)
    return out.reshape(S, 9 * W)


def _shift_down(x, halo, s):
    if s == 0:
        return x
    T = x.shape[0]
    xs = pltpu.roll(x, s, 0)
    hs = pltpu.roll(halo, s, 0)
    row8 = lax.broadcasted_iota(jnp.int32, hs.shape, 0)
    top = jnp.where(row8 < s, hs, xs[0:8])
    return top if T == 8 else jnp.concatenate([top, xs[8:T]], axis=0)


def _shift_up(x, halo, s):
    if s == 0:
        return x
    T = x.shape[0]
    xs = pltpu.roll(x, T - s, 0)
    hs = pltpu.roll(halo, 8 - s, 0)
    row8 = lax.broadcasted_iota(jnp.int32, hs.shape, 0)
    bot = jnp.where(row8 >= 8 - s, hs, xs[T - 8:T])
    return jnp.concatenate([xs[0:T - 8], bot], axis=0)


CONV_ROWS = 128
CONV_LANES = 512


def _conv_apply(x, halo, w_ref, wsl, b, K):
    acc = x * w_ref[K - 1, :, wsl] + b
    for s in range(1, K):
        acc = acc + _shift_down(x, halo, s) * w_ref[K - 1 - s, :, wsl]
    return acc


def _conv_accum(dy, dyn, xv, w_ref, dw_ref, db_ref, wsl, K):
    acc = dy * w_ref[K - 1, :, wsl]
    dw_ref[K - 1, :, wsl] += jnp.sum(dy * xv, axis=0, keepdims=True)
    for s in range(1, K):
        ahead = _shift_up(dy, dyn, s)
        acc = acc + ahead * w_ref[K - 1 - s, :, wsl]
        dw_ref[K - 1 - s, :, wsl] += jnp.sum(ahead * xv, axis=0, keepdims=True)
    db_ref[:, wsl] += jnp.sum(dy, axis=0, keepdims=True)
    return acc


def _row_specs(T, S, width):
    main = pl.BlockSpec((T, width), lambda i: (i, 0))
    prev = pl.BlockSpec((8, width), lambda i: (jnp.maximum(i * (T // 8) - 1, 0), 0))
    nxt = pl.BlockSpec((8, width), lambda i: (jnp.minimum((i + 1) * (T // 8), S // 8 - 1), 0))
    return main, prev, nxt


def _full(shape):
    return pl.BlockSpec(shape, lambda i: (0,) * len(shape))


def _silu_grad(y):
    sg = _sigmoid(y)
    return sg * (1.0 + y * (1.0 - sg))


def _ssm_conv_fwd(zx, w, b, d_inner, conv_dim, name):
    S, wz = zx.shape
    K = w.shape[0]
    T = _pick(S, CONV_ROWS, 8)
    cw = _pick(conv_dim, CONV_LANES)

    def body(x_ref, h_ref, w_ref, b_ref, c_ref):
        has_prev = pl.program_id(0) > 0
        for cs in range(0, conv_dim, cw):
            so, sx = slice(cs, cs + cw), slice(d_inner + cs, d_inner + cs + cw)
            halo = jnp.where(has_prev, h_ref[:, sx], 0.0)
            c_ref[:, so] = _conv_apply(x_ref[:, sx], halo, w_ref, so, b_ref[:, so], K)

    main, prev, _ = _row_specs(T, S, wz)
    return pl.pallas_call(
        body, grid=(S // T,), in_specs=[main, prev, _full((K, 1, conv_dim)), _full((1, conv_dim))],
        out_specs=pl.BlockSpec((T, conv_dim), lambda i: (i, 0)),
        out_shape=jax.ShapeDtypeStruct((S, conv_dim), F32),
        compiler_params=_cp("parallel"), name=name)(zx, zx, w.reshape(K, 1, conv_dim), b.reshape(1, conv_dim))


def _ssm_conv_bwd(dxs, dbm, dcm, conv, zx, dz, ddt, w, d_inner, name):
    S, wz = zx.shape
    K, conv_dim = w.shape
    gn = dbm.shape[1]
    T = _pick(S, CONV_ROWS, 8)
    cw = _pick(math.gcd(d_inner, gn), CONV_LANES)
    nrow = S // T
    tail = wz - d_inner - conv_dim
    assert ddt.shape[1] == tail

    def body(dx_ref, dxn_ref, db_ref_, dbn_ref, dc_ref, dcn_ref, y_ref, yn_ref, x_ref, dz_ref, ddt_ref,
             w_ref, o_ref, dw_ref, dbias_ref):
        i = pl.program_id(0)

        @pl.when(i == 0)
        def _():
            dw_ref[...] = jnp.zeros_like(dw_ref)
            dbias_ref[...] = jnp.zeros_like(dbias_ref)

        has_prev, has_next = i > 0, i < nrow - 1
        for cs in range(0, d_inner, cw):
            o_ref[:, cs:cs + cw] = dz_ref[:, cs:cs + cw].astype(o_ref.dtype)
        o_ref[:, d_inner + conv_dim:wz] = ddt_ref[...].astype(o_ref.dtype)
        for cs in range(0, conv_dim, cw):
            so, sx = slice(cs, cs + cw), slice(d_inner + cs, d_inner + cs + cw)
            if cs < d_inner:
                src, srcn, ss = dx_ref, dxn_ref, slice(cs, cs + cw)
            elif cs < d_inner + gn:
                src, srcn, ss = db_ref_, dbn_ref, slice(cs - d_inner, cs - d_inner + cw)
            else:
                src, srcn, ss = dc_ref, dcn_ref, slice(cs - d_inner - gn, cs - d_inner - gn + cw)
            dy = src[:, ss] * _silu_grad(y_ref[:, so])
            dyn = jnp.where(has_next, srcn[:, ss] * _silu_grad(yn_ref[:, so]), 0.0)
            o_ref[:, sx] = _conv_accum(dy, dyn, x_ref[:, sx], w_ref, dw_ref, dbias_ref, so, K).astype(o_ref.dtype)

    xm, _, xn = _row_specs(T, S, d_inner)
    gm, _, gnx = _row_specs(T, S, gn)
    cm, _, cn = _row_specs(T, S, conv_dim)
    zm, _, _ = _row_specs(T, S, wz)
    tm_, _, _ = _row_specs(T, S, tail)
    dzx, dw, db = pl.pallas_call(
        body, grid=(nrow,),
        in_specs=[xm, xn, gm, gnx, gm, gnx, cm, cn, zm, xm, tm_, _full((K, 1, conv_dim))],
        out_specs=[zm, _full((K, 1, conv_dim)), _full((1, conv_dim))],
        out_shape=[jax.ShapeDtypeStruct((S, wz), BF16), jax.ShapeDtypeStruct((K, 1, conv_dim), F32),
                   jax.ShapeDtypeStruct((1, conv_dim), F32)],
        compiler_params=_cp("arbitrary"), name=name)(
            dxs, dxs, dbm, dbm, dcm, dcm, conv, conv, zx, dz, ddt, w.reshape(K, 1, conv_dim))
    return dzx, dw.reshape(K, conv_dim), db


def _ffn_conv_fwd(up, w, b, name):
    S, C = up.shape
    F = C // 2
    K = w.shape[0]
    T = _pick(S, CONV_ROWS, 8)
    cw = _pick(F, CONV_LANES)

    def body(x_ref, h_ref, w_ref, b_ref, a_ref):
        has_prev = pl.program_id(0) > 0
        for cs in range(0, F, cw):
            sg, su = slice(cs, cs + cw), slice(F + cs, F + cs + cw)
            gate = _conv_apply(x_ref[:, sg], jnp.where(has_prev, h_ref[:, sg], 0.0), w_ref, sg, b_ref[:, sg], K)
            upv = _conv_apply(x_ref[:, su], jnp.where(has_prev, h_ref[:, su], 0.0), w_ref, su, b_ref[:, su], K)
            a_ref[:, sg] = (gate * _sigmoid(gate) * upv).astype(a_ref.dtype)

    main, prev, _ = _row_specs(T, S, C)
    return pl.pallas_call(
        body, grid=(S // T,), in_specs=[main, prev, _full((K, 1, C)), _full((1, C))],
        out_specs=pl.BlockSpec((T, F), lambda i: (i, 0)), out_shape=jax.ShapeDtypeStruct((S, F), BF16),
        compiler_params=_cp("parallel"), name=name)(up, up, w.reshape(K, 1, C), b.reshape(1, C))


def _ffn_conv_bwd(dact, up, w, b, name):
    S, C = up.shape
    F = C // 2
    K = w.shape[0]
    T = _pick(S, CONV_ROWS, 8)
    cw = _pick(F, CONV_LANES)
    nrow = S // T

    def du(gate, upv, d):
        sg = _sigmoid(gate)
        return d * upv * sg * (1.0 + gate * (1.0 - sg)), d * gate * sg

    def body(d_ref, dn_ref, x_ref, xp_ref, xn_ref, w_ref, b_ref, dx_ref, dw_ref, db_ref):
        i = pl.program_id(0)

        @pl.when(i == 0)
        def _():
            dw_ref[...] = jnp.zeros_like(dw_ref)
            db_ref[...] = jnp.zeros_like(db_ref)

        has_prev, has_next = i > 0, i < nrow - 1
        for cs in range(0, F, cw):
            sf = slice(cs, cs + cw)
            cols = [slice(half * F + cs, half * F + cs + cw) for half in range(2)]
            xs = [x_ref[:, sc] for sc in cols]
            xps = [jnp.where(has_prev, xp_ref[:, sc], 0.0) for sc in cols]
            u = [_conv_apply(xs[q], xps[q], w_ref, cols[q], b_ref[:, cols[q]], K) for q in range(2)]
            un = [_conv_apply(xn_ref[:, cols[q]], xs[q][T - 8:T], w_ref, cols[q], b_ref[:, cols[q]], K)
                  for q in range(2)]
            dys = du(u[0], u[1], d_ref[:, sf])
            dyns = du(un[0], un[1], dn_ref[:, sf])
            for q in range(2):
                dyn = jnp.where(has_next, dyns[q], 0.0)
                dx_ref[:, cols[q]] = _conv_accum(dys[q], dyn, xs[q], w_ref, dw_ref, db_ref, cols[q],
                                                 K).astype(dx_ref.dtype)

    am, _, an = _row_specs(T, S, F)
    xm, xp_, xn_ = _row_specs(T, S, C)
    dx, dw, db = pl.pallas_call(
        body, grid=(nrow,), in_specs=[am, an, xm, xp_, xn_, _full((K, 1, C)), _full((1, C))],
        out_specs=[xm, _full((K, 1, C)), _full((1, C))],
        out_shape=[jax.ShapeDtypeStruct((S, C), BF16), jax.ShapeDtypeStruct((K, 1, C), F32),
                   jax.ShapeDtypeStruct((1, C), F32)],
        compiler_params=_cp("arbitrary"), name=name)(dact, dact, up, up, up, w.reshape(K, 1, C), b.reshape(1, C))
    return dx, dw.reshape(K, C), db


def _cumsum_rows(v):
    n = v.shape[0]
    row = lax.broadcasted_iota(jnp.int32, v.shape, 0)
    k = 1
    while k < n:
        v = v + jnp.where(row >= k, pltpu.roll(v, k, 0), 0.0)
        k *= 2
    return v


def _rev_cumsum_rows(v):
    n = v.shape[0]
    row = lax.broadcasted_iota(jnp.int32, v.shape, 0)
    k = 1
    while k < n:
        v = v + jnp.where(row < n - k, pltpu.roll(v, n - k, 0), 0.0)
        k *= 2
    return v


def _ssd_common(x_ref, dtr_ref, bias_ref, alog_ref, gw):
    Q = SSM_CHUNK
    X = _silu(x_ref[...])
    pre = dtr_ref[...] + bias_ref[...]
    dt = _softplus(pre)
    a = -jnp.exp(alog_ref[...])
    cs = _cumsum_rows(dt * a)
    row = lax.broadcasted_iota(jnp.int32, (Q, gw), 0)
    cs_last = jnp.sum(jnp.where(row == Q - 1, cs, 0.0), axis=0, keepdims=True)
    return X, pre, dt, a, cs, cs_last, row


def _head_decay(cs, head_mask):
    Q = SSM_CHUNK
    col = jnp.max(jnp.where(head_mask, cs, NEG), axis=1, keepdims=True)
    acol = jnp.broadcast_to(col, (Q, Q))
    arow = acol.T
    ii = lax.broadcasted_iota(jnp.int32, (Q, Q), 0)
    jj = lax.broadcasted_iota(jnp.int32, (Q, Q), 1)
    tril = ii >= jj
    return jnp.where(tril, jnp.exp(jnp.where(tril, acol - arow, 0.0)), 0.0), tril


class _Win:
    def __init__(self, ref, idx):
        self.ref, self.idx = ref, idx

    def __getitem__(self, _):
        return self.ref[self.idx]

    def __setitem__(self, _, value):
        self.ref[self.idx] = value

    @property
    def dtype(self):
        return self.ref.dtype

    @property
    def shape(self):
        return jax.eval_shape(lambda a: a[self.idx], jax.ShapeDtypeStruct(self.ref.shape, self.ref.dtype)).shape


def _ssd_windows(g, d_inner, gw):
    N, G = SSM_STATE, SSM_GROUPS
    rows = slice(None)
    x = (rows, slice(g * gw, (g + 1) * gw))
    b = (rows, slice(d_inner + g * N, d_inner + (g + 1) * N))
    c = (rows, slice(d_inner + (G + g) * N, d_inner + (G + g + 1) * N))
    n = (rows, slice(g * N, (g + 1) * N))
    return x, b, c, n


def _ssd_fwd(xbc, dtr, bias, alog, dsk, d_inner, name):
    S = xbc.shape[0]
    Q, N, G, P = SSM_CHUNK, SSM_STATE, SSM_GROUPS, SSM_HEAD_DIM
    gw = d_inner // G
    R = gw // P
    nc = S // Q

    def body(conv_ref, dtr_ref, bias_ref, alog_ref, d_ref, y_ref, sp_ref, s_scr):
        first = pl.program_id(0) == 0
        for g in range(G):
            x, b, c, _ = _ssd_windows(g, d_inner, gw)
            one_group(first, _Win(conv_ref, x), _Win(conv_ref, b), _Win(conv_ref, c), _Win(dtr_ref, x),
                      _Win(bias_ref, g), _Win(alog_ref, g), _Win(d_ref, g), _Win(y_ref, x), _Win(sp_ref, g),
                      _Win(s_scr, g))

    def one_group(first, x_ref, b_ref, c_ref, dtr_ref, bias_ref, alog_ref, d_ref, y_ref, sp_ref, s_scr):
        @pl.when(first)
        def _():
            s_scr[...] = jnp.zeros((gw, N), F32)

        X, _, dt, a, cs, cs_last, row = _ssd_common(x_ref, dtr_ref, bias_ref, alog_ref, gw)
        Bm, Cm = _silu(b_ref[...]), _silu(c_ref[...])
        xdt = X * dt
        lane = lax.broadcasted_iota(jnp.int32, (Q, gw), 1)
        sprev = s_scr[...]
        sp_ref[...] = sprev
        cb = _dot_nt(Cm, Bm)
        y = jnp.exp(cs) * _dot_nt(Cm, sprev)
        hms = [(lane >= r * P) & (lane < (r + 1) * P) for r in range(R)]
        dec_ls = [_head_decay(cs, hm)[0] for hm in hms]
        for part in [_dot_nn(cb * dec_l, jnp.where(hm, xdt, 0.0)) for dec_l, hm in zip(dec_ls, hms)]:
            y = y + part
        dec = jnp.exp(cs_last - cs)
        cd = jnp.exp(jnp.broadcast_to(cs_last, (Q, gw)).T)
        s_scr[...] = sprev * cd + _dot_tn(xdt * dec, Bm)
        y_ref[...] = y + d_ref[...] * X

    conv_dim = xbc.shape[1]
    row = pl.BlockSpec((Q, d_inner), lambda c: (c, 0))
    p_spec = _full((G, 1, gw))
    return pl.pallas_call(
        body, grid=(nc,),
        in_specs=[pl.BlockSpec((Q, conv_dim), lambda c: (c, 0)), row, p_spec, p_spec, p_spec],
        out_specs=[row, pl.BlockSpec((None, G, gw, N), lambda c: (c, 0, 0, 0))],
        out_shape=[jax.ShapeDtypeStruct((S, d_inner), F32), jax.ShapeDtypeStruct((nc, G, gw, N), F32)],
        scratch_shapes=[pltpu.VMEM((G, gw, N), F32)],
        compiler_params=_cp("arbitrary"), name=name)(xbc, dtr, bias, alog, dsk)


def _ssd_bwd(xbc, dtr, bias, alog, dsk, sprev_all, dy, d_inner, name):
    S = xbc.shape[0]
    Q, N, G, P = SSM_CHUNK, SSM_STATE, SSM_GROUPS, SSM_HEAD_DIM
    gw = d_inner // G
    R = gw // P
    nc = S // Q

    def body(conv_ref, dtr_ref, bias_ref, alog_ref, d_ref, sp_ref, dy_ref,
             dx_ref, db_ref, dc_ref, ddt_ref, dbias_ref, dalog_ref, dd_ref, ds_scr):
        first = pl.program_id(0) == 0
        for g in range(G):
            x, b, c, n = _ssd_windows(g, d_inner, gw)
            one_group(first, _Win(conv_ref, x), _Win(conv_ref, b), _Win(conv_ref, c), _Win(dtr_ref, x),
                      _Win(bias_ref, g), _Win(alog_ref, g), _Win(d_ref, g), _Win(sp_ref, g), _Win(dy_ref, x),
                      _Win(dx_ref, x), _Win(db_ref, n), _Win(dc_ref, n), _Win(ddt_ref, x), _Win(dbias_ref, g),
                      _Win(dalog_ref, g), _Win(dd_ref, g), _Win(ds_scr, g))

    def one_group(first, x_ref, b_ref, c_ref, dtr_ref, bias_ref, alog_ref, d_ref, sp_ref, dy_ref,
                  dx_ref, db_ref, dc_ref, ddt_ref, dbias_ref, dalog_ref, dd_ref, ds_scr):
        @pl.when(first)
        def _():
            ds_scr[...] = jnp.zeros((gw, N), F32)
            dbias_ref[...] = jnp.zeros((1, gw), F32)
            dalog_ref[...] = jnp.zeros((1, gw), F32)
            dd_ref[...] = jnp.zeros((1, gw), F32)

        X, pre, dt, a, cs, cs_last, row = _ssd_common(x_ref, dtr_ref, bias_ref, alog_ref, gw)
        Bm, Cm = _silu(b_ref[...]), _silu(c_ref[...])
        dY = dy_ref[...]
        sprev = sp_ref[...]
        dsn = ds_scr[...]
        xdt = X * dt
        lane = lax.broadcasted_iota(jnp.int32, (Q, gw), 1)
        lane1 = lax.broadcasted_iota(jnp.int32, (1, gw), 1)
        srow = lax.broadcasted_iota(jnp.int32, (gw, N), 0)
        ecs = jnp.exp(cs)
        dec = jnp.exp(cs_last - cs)
        cd = jnp.exp(jnp.broadcast_to(cs_last, (Q, gw)).T)
        dd_ref[...] += jnp.sum(dY * X, axis=0, keepdims=True)
        dX = d_ref[...] * dY
        ey = ecs * dY
        dcs = ey * _dot_nt(Cm, sprev)
        dC = _dot_nn(ey, sprev)
        ds_scr[...] = cd * dsn + _dot_tn(ey, Cm)
        wmat = _dot_nt(Bm, dsn)
        dxdt = dec * wmat
        xd = xdt * dec
        dB = _dot_nn(xd, dsn)
        ddec = xdt * wmat * dec
        dcs = dcs - ddec
        dlast = jnp.sum(ddec, axis=0, keepdims=True)
        qmat = dsn * sprev * cd
        cb = _dot_nt(Cm, Bm)
        dcb = jnp.zeros((Q, Q), F32)
        dcs_rep = jnp.zeros((Q, gw), F32)
        dtx_rep = jnp.zeros((Q, gw), F32)
        hms = [(lane >= r * P) & (lane < (r + 1) * P) for r in range(R)]
        decs = [_head_decay(cs, hm) for hm in hms]
        dyrs = [jnp.where(hm, dY, 0.0) for hm in hms]
        graws = [_dot_nt(dyr, xdt) for dyr in dyrs]
        backs = [_dot_tn(cb * dec_l, dyr) for (dec_l, _), dyr in zip(decs, dyrs)]
        for r in range(R):
            hm, (dec_l, tril) = hms[r], decs[r]
            gmat = jnp.where(tril, graws[r], 0.0)
            dcb = dcb + gmat * dec_l
            e = gmat * cb * dec_l
            v = (jnp.sum(e, axis=1, keepdims=True) - jnp.sum(e.T, axis=1, keepdims=True)
                 + jnp.sum(jnp.where(hm, dcs, 0.0), axis=1, keepdims=True))
            dxdt = dxdt + backs[r]
            hm1 = (lane1 >= r * P) & (lane1 < (r + 1) * P)
            t_last = (jnp.sum(jnp.where(hm1, dlast, 0.0), axis=1, keepdims=True)
                      + jnp.sum(jnp.where((srow >= r * P) & (srow < (r + 1) * P), qmat, 0.0), keepdims=True))
            dcs_rep = dcs_rep + jnp.where(hm, v, 0.0) + jnp.where(hm & (row == Q - 1), t_last, 0.0)
        for r in range(R):
            hm = (lane >= r * P) & (lane < (r + 1) * P)
            w_r = jnp.sum(jnp.where(hm, dxdt * X, 0.0), axis=1, keepdims=True)
            dtx_rep = dtx_rep + jnp.where(hm, w_r, 0.0)
        dadt = _rev_cumsum_rows(dcs_rep)
        ddt = a * dadt + dtx_rep
        dalog_ref[...] += jnp.sum(dt * dadt, axis=0, keepdims=True) * a
        draw = ddt * _sigmoid(pre)
        ddt_ref[...] = draw
        dbias_ref[...] += jnp.sum(draw, axis=0, keepdims=True)
        dx_ref[...] = dX + dxdt * dt
        db_ref[...] = dB + _dot_tn(dcb, Cm)
        dc_ref[...] = dC + _dot_nn(dcb, Bm)

    conv_dim = xbc.shape[1]
    rev = lambda c: nc - 1 - c
    row = pl.BlockSpec((Q, d_inner), lambda c: (rev(c), 0))
    n_spec = pl.BlockSpec((Q, G * N), lambda c: (rev(c), 0))
    s_spec = pl.BlockSpec((None, G, gw, N), lambda c: (rev(c), 0, 0, 0))
    p_spec = _full((G, 1, gw))
    gshape = jax.ShapeDtypeStruct((G, 1, gw), F32)
    return pl.pallas_call(
        body, grid=(nc,),
        in_specs=[pl.BlockSpec((Q, conv_dim), lambda c: (rev(c), 0)), row, p_spec, p_spec, p_spec, s_spec, row],
        out_specs=[row, n_spec, n_spec, row, p_spec, p_spec, p_spec],
        out_shape=[jax.ShapeDtypeStruct((S, d_inner), F32), jax.ShapeDtypeStruct((S, G * N), F32),
                   jax.ShapeDtypeStruct((S, G * N), F32), jax.ShapeDtypeStruct((S, d_inner), F32),
                   gshape, gshape, gshape],
        scratch_shapes=[pltpu.VMEM((G, gw, N), F32)],
        compiler_params=_cp("arbitrary"), name=name)(xbc, dtr, bias, alog, dsk, sprev_all, dy)


def _gnorm_fwd(y, zx, w, name):
    S, d_inner = y.shape
    G = SSM_GROUPS
    gw = d_inner // G
    T = _pick(S, 256, 8)

    def body(y_ref, z_ref, w_ref, o_ref):
        for k in range(G):
            sl = slice(k * gw, (k + 1) * gw)
            z = z_ref[:, sl]
            gk = y_ref[:, sl] * z * _sigmoid(z)
            r = lax.rsqrt(jnp.mean(gk * gk, axis=-1, keepdims=True) + NORM_EPS)
            o_ref[:, sl] = (gk * r * w_ref[:, sl]).astype(o_ref.dtype)

    row = pl.BlockSpec((T, d_inner), lambda i: (i, 0))
    vec = pl.BlockSpec((1, d_inner), lambda i: (0, 0))
    return pl.pallas_call(body, grid=(S // T,), in_specs=[row, row, vec], out_specs=row,
                          out_shape=jax.ShapeDtypeStruct((S, d_inner), BF16),
                          compiler_params=_cp("parallel"), name=name)(y, zx, w.reshape(1, d_inner))


def _gnorm_bwd(y, zx, w, dout, name):
    S, d_inner = y.shape
    G = SSM_GROUPS
    gw = d_inner // G
    T = _pick(S, 256, 8)

    def body(y_ref, z_ref, w_ref, d_ref, dy_ref, dz_ref, dw_ref):
        @pl.when(pl.program_id(0) == 0)
        def _():
            dw_ref[...] = jnp.zeros_like(dw_ref)

        for k in range(G):
            sl = slice(k * gw, (k + 1) * gw)
            z, yv, d = z_ref[:, sl], y_ref[:, sl], d_ref[:, sl]
            sg = _sigmoid(z)
            sz = z * sg
            gk = yv * sz
            r = lax.rsqrt(jnp.mean(gk * gk, axis=-1, keepdims=True) + NORM_EPS)
            gh = gk * r
            dw_ref[:, sl] += jnp.sum(d * gh, axis=0, keepdims=True)
            dg = d * w_ref[:, sl]
            dgk = r * (dg - gh * jnp.mean(dg * gh, axis=-1, keepdims=True))
            dy_ref[:, sl] = dgk * sz
            dz_ref[:, sl] = dgk * yv * sg * (1.0 + z * (1.0 - sg))

    row = pl.BlockSpec((T, d_inner), lambda i: (i, 0))
    vec = pl.BlockSpec((1, d_inner), lambda i: (0, 0))
    return pl.pallas_call(
        body, grid=(S // T,), in_specs=[row, row, vec, row], out_specs=[row, row, vec],
        out_shape=[jax.ShapeDtypeStruct((S, d_inner), F32)] * 2 + [jax.ShapeDtypeStruct((1, d_inner), F32)],
        compiler_params=_cp("arbitrary"), name=name)(y, zx, w.reshape(1, d_inner), dout)


def _adam_math(g, w, m, v):
    m = ADAM_B1 * m + (1.0 - ADAM_B1) * g
    v = ADAM_B2 * v + (1.0 - ADAM_B2) * (g * g)
    m_hat = m / (1.0 - ADAM_B1 ** ADAM_STEP)
    v_hat = v / (1.0 - ADAM_B2 ** ADAM_STEP)
    delta = -ADAM_LR * (m_hat / (jnp.sqrt(v_hat) + ADAM_EPS) + ADAM_WD * w)
    return delta, m, v


def _adamw_big(own, sib, w, m, v, layer, prev, name):
    L, A, Bc = w.shape
    T = _pick(A, max(8, (1 << 20) // (4 * Bc)), 16)

    def body(o_ref, s_ref, w_ref, m_ref, v_ref, *rest):
        g_ref, d_ref, nm_ref, nv_ref = rest[-4:]
        so = o_ref[0].astype(F32)
        ss = s_ref[0].astype(F32)
        for k in range(1, N_CHIPS):
            so = so + o_ref[k].astype(F32)
            ss = ss + s_ref[k].astype(F32)
        g = so + ss
        delta, nm, nv = _adam_math(g, w_ref[...], m_ref[...], v_ref[...])
        g_ref[...] = g
        d_ref[...] = delta
        nm_ref[...] = nm
        nv_ref[...] = nv

    part = pl.BlockSpec((N_CHIPS, T, Bc), lambda i: (0, i, 0))
    blk = pl.BlockSpec((None, T, Bc), lambda i: (layer, i, 0))
    shp = jax.ShapeDtypeStruct(w.shape, F32)
    in_specs, args, kwargs = [part, part, blk, blk, blk], [own, sib, w, m, v], {}
    if prev is not None:
        in_specs += [pl.BlockSpec(memory_space=pl.ANY)] * 4
        args += list(prev)
        kwargs["input_output_aliases"] = {5 + q: q for q in range(4)}
    return pl.pallas_call(body, grid=(A // T,), in_specs=in_specs, out_specs=[blk] * 4, out_shape=[shp] * 4,
                          compiler_params=_cp("parallel"), name=name, **kwargs)(*args)


def _sum_devices(parts, name):
    _, R, C = parts.shape

    def body(p_ref, o_ref):
        acc = p_ref[0]
        for k in range(1, N_DEV):
            acc = acc + p_ref[k]
        o_ref[...] = acc

    return pl.pallas_call(body, out_shape=jax.ShapeDtypeStruct((R, C), F32), name=name)(parts)


def _adamw_small(g, w, m, v, name):
    def body(g_ref, w_ref, m_ref, v_ref, d_ref, nm_ref, nv_ref):
        delta, nm, nv = _adam_math(g_ref[...], w_ref[...], m_ref[...], v_ref[...])
        d_ref[...] = delta
        nm_ref[...] = nm
        nv_ref[...] = nv

    shp = jax.ShapeDtypeStruct(g.shape, F32)
    return pl.pallas_call(body, out_shape=[shp] * 3, name=name)(g, w, m, v)


PACK_COLS = 1024


def _pack(arrs):
    flat = jnp.concatenate([a.reshape(-1).astype(F32) for a in arrs])
    n = flat.shape[0]
    rows = -(-n // (8 * PACK_COLS)) * 8
    return jnp.pad(flat, (0, rows * PACK_COLS - n)).reshape(rows, PACK_COLS)


def _unpack(packed, shapes):
    flat = packed.reshape(-1)
    out, off = [], 0
    for s in shapes:
        n = math.prod(s)
        out.append(flat[off:off + n].reshape(s))
        off += n
    return out


def _shard_ref(ref, kind, k, n):
    if kind == "col":
        return ref.at[:, pl.ds(pl.multiple_of(k * n, 128), n)]
    if kind == "row":
        return ref.at[pl.ds(pl.multiple_of(k * n, 16), n), :]
    return ref.at[k]


def _chip_peers():
    x, y, c = lax.axis_index("x"), lax.axis_index("y"), lax.axis_index("c")
    return x, y, c, [(1 - x, y), (x, 1 - y), (1 - x, 1 - y)]


class _Exchange:
    def __init__(self, mode, items):
        self.mode, self.items = mode, items
        self.arrays = []
        for it in items:
            if not any(it[0] is a for a in self.arrays):
                self.arrays.append(it[0])
        self.src_idx = [next(i for i, a in enumerate(self.arrays) if a is it[0]) for it in items]
        self.out_shapes = [jax.ShapeDtypeStruct(it[-1], it[0].dtype) for it in items]
        n = len(items)
        if mode == "swap":
            self.scratch = [pltpu.SemaphoreType.DMA((n,)), pltpu.SemaphoreType.DMA((n,))]
        else:
            self.scratch = [pltpu.SemaphoreType.DMA((3 * n,)), pltpu.SemaphoreType.DMA((3 * n,)),
                            pltpu.SemaphoreType.DMA((n,))]

    def _copies(self, ins, outs, sems):
        if self.mode == "swap":
            send_sems, recv_sems = sems
            x, y, c = lax.axis_index("x"), lax.axis_index("y"), lax.axis_index("c")
            sent = [pltpu.make_async_remote_copy(
                src_ref=ins[self.src_idx[t]], dst_ref=outs[t], send_sem=send_sems.at[t], recv_sem=recv_sems.at[t],
                device_id=(x, y, 1 - c), device_id_type=MESH) for t in range(len(self.items))]
            return [], sent, sent
        send_sems, recv_sems, loc_sems = sems
        x, y, c, peers = _chip_peers()
        me = 2 * x + y
        local, sent, arriving = [], [], []
        for t, it in enumerate(self.items):
            src_arr = ins[self.src_idx[t]]
            if self.mode == "gather":
                _, layer, kind, n, _ = it
                src = src_arr if layer is None else src_arr.at[layer]
                src_for = lambda k: src
                dst_from = lambda k: _shard_ref(outs[t], kind, k, n)
            else:
                _, kind, n, _ = it
                src_for = lambda k: _shard_ref(src_arr, kind, k, n)
                dst_from = lambda k: outs[t].at[k]
            local.append(pltpu.make_async_copy(src_for(me), dst_from(me), loc_sems.at[t]))
            for j, (px, py) in enumerate(peers):
                pk = 2 * px + py
                args = dict(send_sem=send_sems.at[3 * t + j], recv_sem=recv_sems.at[3 * t + j],
                            device_id=(px, py, c), device_id_type=MESH)
                sent.append(pltpu.make_async_remote_copy(src_ref=src_for(pk), dst_ref=dst_from(me), **args))
                arriving.append(pltpu.make_async_remote_copy(src_ref=src_for(pk), dst_ref=dst_from(pk), **args))
        return local, sent, arriving

    def start(self, ins, outs, sems):
        local, sent, arriving = self._copies(ins, outs, sems)
        for cp in local + sent:
            cp.start()
        for cp in arriving:
            cp._used = True

    def wait(self, ins, outs, sems):
        local, sent, arriving = self._copies(ins, outs, sems)
        for cp in arriving:
            cp.wait_recv()
        for cp in sent:
            cp.wait_send()
        for cp in local:
            cp.wait()


class _Multi:
    def __init__(self, parts):
        self.parts = parts
        self.arrays = [a for p in parts for a in p.arrays]
        self.out_shapes = [s for p in parts for s in p.out_shapes]
        self.scratch = [s for p in parts for s in p.scratch]

    def _split(self, ins, outs, sems):
        i = o = s = 0
        for p in self.parts:
            ni, no, ns = len(p.arrays), len(p.out_shapes), len(p.scratch)
            yield p, ins[i:i + ni], outs[o:o + no], sems[s:s + ns]
            i, o, s = i + ni, o + no, s + ns

    def start(self, ins, outs, sems):
        for p, a, b, c in self._split(ins, outs, sems):
            p.start(a, b, c)

    def wait(self, ins, outs, sems):
        for p, a, b, c in self._split(ins, outs, sems):
            p.wait(a, b, c)


def _run_exchange(ex, name):
    nin, nout = len(ex.arrays), len(ex.out_shapes)

    def body(*refs):
        ins, outs, sems = refs[:nin], refs[nin:nin + nout], refs[nin + nout:]
        ex.start(ins, outs, sems)
        ex.wait(ins, outs, sems)

    anyspec = pl.BlockSpec(memory_space=pl.ANY)
    return pl.pallas_call(body, in_specs=[anyspec] * nin, out_specs=[anyspec] * nout, out_shape=ex.out_shapes,
                          scratch_shapes=ex.scratch, name=name)(*ex.arrays)


def _start_exchange(ex, name, after=None):
    nin, nout, nsem = len(ex.arrays), len(ex.out_shapes), len(ex.scratch)
    hbm = pl.BlockSpec(memory_space=pltpu.HBM)
    sem = pl.BlockSpec(memory_space=pltpu.SEMAPHORE)

    n_operands = nin + nout + (after is not None)

    def body(*refs):
        ins, lands = refs[:nin], refs[nin:nin + nout]
        sems = refs[n_operands:n_operands + nsem]
        ex.start(ins, lands, sems)
        refs[-1][...] = jnp.zeros_like(refs[-1])

    args = [pltpu.with_memory_space_constraint(a, pltpu.HBM) for a in ex.arrays]
    args += [pltpu.with_memory_space_constraint(lax.empty(s.shape, s.dtype), pltpu.HBM) for s in ex.out_shapes]
    thru = [pltpu.HBM(a.shape, a.dtype) for a in ex.arrays] + [pltpu.HBM(s.shape, s.dtype) for s in ex.out_shapes]
    extra = [] if after is None else [pl.BlockSpec(memory_space=pl.ANY)]
    args += [] if after is None else [after]
    return pl.pallas_call(
        body, name=name, in_specs=[hbm] * (nin + nout) + extra,
        out_shape=tuple(ex.scratch) + tuple(thru) + (jax.ShapeDtypeStruct((8, 128), F32),),
        out_specs=tuple([sem] * nsem + [hbm] * (nin + nout) + [pl.BlockSpec(memory_space=pltpu.VMEM)]),
        input_output_aliases={q: nsem + q for q in range(nin + nout)},
        compiler_params=pltpu.CompilerParams(has_side_effects=pltpu.SideEffectType.DATAFLOW_SIDE_EFFECTING))(*args)


def _finish_exchange(ex, handles, after, name):
    nin, nout, nsem = len(ex.arrays), len(ex.out_shapes), len(ex.scratch)
    hbm = pl.BlockSpec(memory_space=pltpu.HBM)
    sem = pl.BlockSpec(memory_space=pltpu.SEMAPHORE)
    sems, thru = handles[:nsem], handles[nsem:nsem + nin + nout]

    def body(*refs):
        ins, lands = refs[:nin], refs[nin:nin + nout]
        ex.wait(ins, lands, refs[nin + nout:nin + nout + nsem])

    outs = pl.pallas_call(
        body, name=name, in_specs=[hbm] * (nin + nout) + [sem] * nsem + [pl.BlockSpec(memory_space=pl.ANY)],
        out_shape=tuple(pltpu.HBM(t.shape, t.dtype) for t in thru), out_specs=tuple([hbm] * (nin + nout)),
        input_output_aliases={q: q for q in range(nin + nout)},
        compiler_params=pltpu.CompilerParams(has_side_effects=pltpu.SideEffectType.DATAFLOW_SIDE_EFFECTING))(
            *thru, *sems, after)
    return outs[nin:]


def _all_gather_devices(v, name):
    def body(v_ref, o_ref, send_sems, recv_sems, loc_sem):
        x, y, c = lax.axis_index("x"), lax.axis_index("y"), lax.axis_index("c")
        me = 4 * x + 2 * y + c
        lc = pltpu.make_async_copy(v_ref, o_ref.at[me], loc_sem)
        lc.start()
        rel = [(bx, by, bc) for bx in (0, 1) for by in (0, 1) for bc in (0, 1)][1:]
        copies = []
        for j, (bx, by, bc) in enumerate(rel):
            px, py, pc = x ^ bx, y ^ by, c ^ bc
            copies.append((pltpu.make_async_remote_copy(
                src_ref=v_ref, dst_ref=o_ref.at[me], send_sem=send_sems.at[j], recv_sem=recv_sems.at[j],
                device_id=(px, py, pc), device_id_type=MESH), 4 * px + 2 * py + pc))
        for cp, _ in copies:
            cp.start()
        for j, (cp, pid) in enumerate(copies):
            pltpu.make_async_remote_copy(
                src_ref=v_ref, dst_ref=o_ref.at[pid], send_sem=send_sems.at[j], recv_sem=recv_sems.at[j],
                device_id=(x, y, c), device_id_type=MESH).wait_recv()
        for cp, _ in copies:
            cp.wait_send()
        lc.wait()

    anyspec = pl.BlockSpec(memory_space=pl.ANY)
    return pl.pallas_call(
        body, in_specs=[anyspec], out_specs=anyspec,
        out_shape=jax.ShapeDtypeStruct((N_DEV,) + v.shape, v.dtype),
        scratch_shapes=[pltpu.SemaphoreType.DMA((N_DEV - 1,)), pltpu.SemaphoreType.DMA((N_DEV - 1,)),
                        pltpu.SemaphoreType.DMA(())],
        name=name)(v)


BIG = ("attn_w_qkv", "attn_w_o", "ssm_w_in", "ssm_w_out", "ffn_w_up", "ffn_w_down")
BIG_KIND = {"attn_w_qkv": "col", "attn_w_o": "row", "ssm_w_in": "lead", "ssm_w_out": "row",
            "ffn_w_up": "col", "ffn_w_down": "row"}
SMALL_SHARDED = {"ssm_conv_w": 2, "ssm_conv_b": 1, "ssm_norm_w": 1, "ffn_conv_w": 2}
SMALL = ("mix_norm_w", "ssm_conv_w", "ssm_conv_b", "ssm_dt_bias", "ssm_a_log", "ssm_d", "ssm_norm_w",
         "ffn_norm_w", "ffn_conv_w", "ffn_conv_b", "final_norm_w")
WEIGHTS = ("mix_norm_w", "attn_w_qkv", "attn_w_o", "ssm_w_in", "ssm_conv_w", "ssm_conv_b", "ssm_dt_bias",
           "ssm_a_log", "ssm_d", "ssm_norm_w", "ssm_w_out", "ffn_norm_w", "ffn_w_up", "ffn_conv_w",
           "ffn_conv_b", "ffn_w_down", "final_norm_w")


def _shard_extent(name, shape):
    _, a, b = shape
    return {"col": b, "row": a, "lead": 1}[BIG_KIND[name]]


def _gather_item(w16, name):
    a, b = w16.shape
    kind = BIG_KIND[name]
    full = {"col": (a, N_CHIPS * b), "row": (N_CHIPS * a, b), "lead": (N_CHIPS, a, b)}[kind]
    return (w16, None, kind, _shard_extent(name, (1, a, b)), full)


def _layer_weights(i):
    j = i // 2
    mixer = [("attn_w_qkv", j), ("attn_w_o", j)] if i % 2 == 0 else [("ssm_w_in", j), ("ssm_w_out", j)]
    return mixer + [("ffn_w_up", i), ("ffn_w_down", i)]


def kernel(x, mix_norm_w, attn_w_qkv, attn_w_o, ssm_w_in, ssm_conv_w, ssm_conv_b, ssm_dt_bias, ssm_a_log, ssm_d, ssm_norm_w, ssm_w_out, ffn_norm_w, ffn_w_up, ffn_conv_w, ffn_conv_b, ffn_w_down, final_norm_w, loss_target, m_mix_norm_w, m_attn_w_qkv, m_attn_w_o, m_ssm_w_in, m_ssm_conv_w, m_ssm_conv_b, m_ssm_dt_bias, m_ssm_a_log, m_ssm_d, m_ssm_norm_w, m_ssm_w_out, m_ffn_norm_w, m_ffn_w_up, m_ffn_conv_w, m_ffn_conv_b, m_ffn_w_down, m_final_norm_w, v_mix_norm_w, v_attn_w_qkv, v_attn_w_o, v_ssm_w_in, v_ssm_conv_w, v_ssm_conv_b, v_ssm_dt_bias, v_ssm_a_log, v_ssm_d, v_ssm_norm_w, v_ssm_w_out, v_ffn_norm_w, v_ffn_w_up, v_ffn_conv_w, v_ffn_conv_b, v_ffn_w_down, v_final_norm_w):
    W = dict(mix_norm_w=mix_norm_w, attn_w_qkv=attn_w_qkv, attn_w_o=attn_w_o, ssm_w_in=ssm_w_in,
             ssm_conv_w=ssm_conv_w, ssm_conv_b=ssm_conv_b, ssm_dt_bias=ssm_dt_bias, ssm_a_log=ssm_a_log,
             ssm_d=ssm_d, ssm_norm_w=ssm_norm_w, ssm_w_out=ssm_w_out, ffn_norm_w=ffn_norm_w, ffn_w_up=ffn_w_up,
             ffn_conv_w=ffn_conv_w, ffn_conv_b=ffn_conv_b, ffn_w_down=ffn_w_down, final_norm_w=final_norm_w)
    M = dict(mix_norm_w=m_mix_norm_w, attn_w_qkv=m_attn_w_qkv, attn_w_o=m_attn_w_o, ssm_w_in=m_ssm_w_in,
             ssm_conv_w=m_ssm_conv_w, ssm_conv_b=m_ssm_conv_b, ssm_dt_bias=m_ssm_dt_bias, ssm_a_log=m_ssm_a_log,
             ssm_d=m_ssm_d, ssm_norm_w=m_ssm_norm_w, ssm_w_out=m_ssm_w_out, ffn_norm_w=m_ffn_norm_w,
             ffn_w_up=m_ffn_w_up, ffn_conv_w=m_ffn_conv_w, ffn_conv_b=m_ffn_conv_b, ffn_w_down=m_ffn_w_down,
             final_norm_w=m_final_norm_w)
    V = dict(mix_norm_w=v_mix_norm_w, attn_w_qkv=v_attn_w_qkv, attn_w_o=v_attn_w_o, ssm_w_in=v_ssm_w_in,
             ssm_conv_w=v_ssm_conv_w, ssm_conv_b=v_ssm_conv_b, ssm_dt_bias=v_ssm_dt_bias, ssm_a_log=v_ssm_a_log,
             ssm_d=v_ssm_d, ssm_norm_w=v_ssm_norm_w, ssm_w_out=v_ssm_w_out, ffn_norm_w=v_ffn_norm_w,
             ffn_w_up=v_ffn_w_up, ffn_conv_w=v_ffn_conv_w, ffn_conv_b=v_ffn_conv_b, ffn_w_down=v_ffn_w_down,
             final_norm_w=v_final_norm_w)

    S, D = x.shape[1], x.shape[2]
    xs = x.reshape(S, D)
    tgt = loss_target.reshape(S, D)
    depth = mix_norm_w.shape[0]
    heads = attn_w_o.shape[1] * N_CHIPS // HEAD_DIM
    AW = heads * HEAD_DIM
    d_inner = ssm_w_out.shape[1] * N_CHIPS
    ssm_heads = d_inner // SSM_HEAD_DIM
    G, P, N = SSM_GROUPS, SSM_HEAD_DIM, SSM_STATE
    gw = d_inner // G
    conv_dim = d_inner + 2 * G * N
    in_w = d_inner + conv_dim + ssm_heads
    in_pad = -(-in_w // 128) * 128
    shard_in = ssm_w_in.shape[2]
    xi, yi = lax.axis_index("x"), lax.axis_index("y")
    chip = 2 * xi + yi

    full = {}

    def land(keys, outs):
        for (n, l), o in zip(keys, outs):
            if n == "ssm_w_in":
                o = jnp.pad(jnp.concatenate([o[k] for k in range(N_CHIPS)], axis=1), ((0, 0), (0, in_pad - in_w)))
            full[(n, l)] = o

    def gather_ex(keys, extra=()):
        return _Exchange("gather", [_gather_item(W[n][l].astype(BF16), n) for n, l in keys] + list(extra))

    def fwd_mm(a, b, name, resid=None):
        return _matmul(a, b, "nn", F32, name, resid=resid)

    sm_names = list(SMALL_SHARDED)
    packed = _pack([W[n] for n in sm_names])
    lw = [_layer_weights(i) for i in range(depth)]
    first = lw[0][:1]
    got = _run_exchange(gather_ex(first, [(packed, None, "lead", 1, (N_CHIPS,) + packed.shape)]), "gather_first")
    land(first, got[:-1])
    batches = [lw[0][1:]] + lw[1:]
    pending = []
    for q, keys in enumerate(batches):
        ex = gather_ex(keys)
        pending.append((keys, ex, _start_exchange(ex, f"gather_start_{q}", after=got[-1])))
    issued = sum(handles[-1][0, 0] for _, _, handles in pending)

    def arrive(q, after):
        keys, ex, handles = pending[q]
        land(keys, _finish_exchange(ex, handles, after, f"gather_wait_{q}"))
    per_chip = [_unpack(got[-1][k], [W[n].shape for n in sm_names]) for k in range(N_CHIPS)]
    for q, n in enumerate(sm_names):
        full[n] = jnp.concatenate([per_chip[k][q] for k in range(N_CHIPS)], axis=SMALL_SHARDED[n])
    tab = _perm_tokens(_rope_table(S))

    def rep_heads(p):
        return jnp.repeat(p, P).reshape(G, 1, gw)

    saved = []
    cur = xs
    for i in range(depth):
        j = i // 2
        if i > 0:
            arrive(i, cur)
        sv = {"x_in": cur}
        h = _rms_fwd(cur, mix_norm_w[i] + issued if i == 0 else mix_norm_w[i], f"mix_norm_fwd_{i}")
        sv["h"] = h
        if i % 2 == 0:
            h = _perm_tokens(h)
            sv["h"] = h
            qkv = _matmul(h, full[("attn_w_qkv", j)], "nn", F32, f"qkv_fwd_{i}", rope=(tab, AW))
            if i == 0:
                arrive(0, qkv)
            og = [_attn_fwd(qkv, g, heads, f"attn_fwd_{i}_{g}") for g in range(3)]
            o, lse = _attn_combine([a for a, _ in og], [b for _, b in og], f"attn_combine_{i}")
            mixed = _unperm_tokens(fwd_mm(o, full[("attn_w_o", j)], f"attn_out_fwd_{i}"))
            sv.update(qkv=qkv, o=o, lse=lse)
        else:
            zx = fwd_mm(h, full[("ssm_w_in", j)], f"ssm_in_fwd_{i}")
            conv = _ssm_conv_fwd(zx, full["ssm_conv_w"][j], full["ssm_conv_b"][j], d_inner, conv_dim,
                                 f"ssm_conv_fwd_{i}")
            prm = [rep_heads(p[j]) for p in (ssm_dt_bias, ssm_a_log, ssm_d)]
            dtr = jnp.repeat(zx[:, d_inner + conv_dim:in_w], P, axis=1)
            y, sprev = _ssd_fwd(conv, dtr, *prm, d_inner, f"ssd_fwd_{i}")
            gated = _gnorm_fwd(y, zx, full["ssm_norm_w"][j], f"ssm_norm_fwd_{i}")
            cur = fwd_mm(gated, full[("ssm_w_out", j)], f"ssm_out_fwd_{i}", resid=cur)
            sv.update(zx=zx, conv=conv, dtr=dtr, prm=prm, y=y, sprev=sprev, gated=gated)
            mixed = None
        if mixed is None:
            h2 = _rms_fwd(cur, ffn_norm_w[i], f"ffn_norm_fwd_{i}")
        else:
            cur, h2 = _rms_fwd(cur, ffn_norm_w[i], f"ffn_norm_fwd_{i}", add=mixed)
        sv["x_mid"] = cur
        up = fwd_mm(h2, full[("ffn_w_up", i)], f"ffn_up_fwd_{i}")
        act = _ffn_conv_fwd(up, full["ffn_conv_w"][i], ffn_conv_b[i], f"ffn_conv_fwd_{i}")
        cur = fwd_mm(act, full[("ffn_w_down", i)], f"ffn_down_fwd_{i}", resid=cur)
        sv.update(h2=h2, up=up, act=act)
        saved.append(sv)

    dx, d_final, loss_part = _loss_head(cur, final_norm_w, tgt, "loss_head")
    gbig, recv = {}, {}
    gs = {n: [None] * W[n].shape[0] for n in SMALL if n != "final_norm_w"}

    def scatter_ex(keys):
        return _Exchange("scatter", [(gbig[(n, l)], BIG_KIND[n], _shard_extent(n, W[n].shape),
                                      (N_CHIPS,) + W[n].shape[1:]) for n, l in keys])

    sib = {}

    def swap_ex(keys):
        return _Exchange("swap", [(recv[k], recv[k].shape) for k in keys])

    def bwd_mm(a, b, mode, dtype, name, send=(), swap=()):
        if not send and not swap:
            return _matmul(a, b, mode, dtype, name)
        parts = ([scatter_ex(send)] if send else []) + ([swap_ex(swap)] if swap else [])
        out, got = _matmul(a, b, mode, dtype, name, carry=_Multi(parts))
        recv.update(zip(send, got[:len(send)]))
        sib.update(zip(swap, got[len(send):]))
        return out

    sending = None
    to_swap = []
    for i in reversed(range(depth)):
        j = i // 2
        sv = saved[i]
        k_in, k_out, k_up, k_down = _layer_weights(i)
        own = i == 0

        def now(keys):
            return keys if own else []
        dact = _matmul(dx, full[k_down], "nt", F32, f"ffn_down_dgrad_{i}")
        gbig[k_down] = bwd_mm(sv["act"], dx, "tn", BF16, f"ffn_down_wgrad_{i}", swap=to_swap)
        to_swap = []
        dup, dcw, dcb = _ffn_conv_bwd(dact, sv["up"], full["ffn_conv_w"][i], ffn_conv_b[i], f"ffn_conv_bwd_{i}")
        gs["ffn_conv_w"][i], gs["ffn_conv_b"][i] = dcw, dcb[0]
        dh2 = bwd_mm(dup, full[k_up], "nt", F32, f"ffn_up_dgrad_{i}", send=now([k_down]))
        gbig[k_up] = bwd_mm(sv["h2"], dup, "tn", BF16, f"ffn_up_wgrad_{i}", swap=now([k_down]))
        dx, dnw = _rms_bwd(sv["x_mid"], ffn_norm_w[i], dh2, dx, f"ffn_norm_bwd_{i}")
        gs["ffn_norm_w"][i] = dnw[0]
        if i % 2 == 0:
            dxp = _perm_tokens(dx)
            do = _matmul(dxp, full[k_out], "nt", F32, f"attn_out_dgrad_{i}")
            gbig[k_out] = _matmul(sv["o"], dxp, "tn", BF16, f"attn_out_wgrad_{i}")
            dqkv = None
            for g in range(3):
                dqkv = _attn_bwd(sv["qkv"], tab, sv["o"], sv["lse"], do, dqkv, g, heads, f"attn_bwd_{i}_{g}")
            gbig[k_in] = bwd_mm(sv["h"], dqkv, "tn", BF16, f"qkv_wgrad_{i}", send=now([k_up, k_out]))
            dh = _unperm_tokens(bwd_mm(dqkv, full[k_in], "nt", F32, f"qkv_dgrad_{i}", send=now([k_in]),
                                       swap=now([k_up, k_out])))
        else:
            dgated = _matmul(dx, full[k_out], "nt", F32, f"ssm_out_dgrad_{i}")
            gbig[k_out] = _matmul(sv["gated"], dx, "tn", BF16, f"ssm_out_wgrad_{i}")
            dy, dz, dgw = _gnorm_bwd(sv["y"], sv["zx"], full["ssm_norm_w"][j], dgated, f"ssm_norm_bwd_{i}")
            gs["ssm_norm_w"][j] = dgw[0]
            dxs_, dbm, dcm, ddtr, dbias, dalog, ddsk = _ssd_bwd(
                sv["conv"], sv["dtr"], *sv["prm"], sv["sprev"], dy, d_inner, f"ssd_bwd_{i}")
            gs["ssm_dt_bias"][j] = dbias.reshape(-1)[::P]
            gs["ssm_a_log"][j] = dalog.reshape(-1)[::P]
            gs["ssm_d"][j] = ddsk.reshape(ssm_heads, P).sum(axis=1)
            ddt = jnp.pad(ddtr[:, ::P], ((0, 0), (0, in_pad - in_w)))
            dzx, dcw, dcb = _ssm_conv_bwd(dxs_, dbm, dcm, sv["conv"], sv["zx"], dz, ddt, full["ssm_conv_w"][j],
                                          d_inner, f"ssm_conv_bwd_{i}")
            gs["ssm_conv_w"][j], gs["ssm_conv_b"][j] = dcw, dcb[0]
            dwin = bwd_mm(sv["h"], dzx, "tn", BF16, f"ssm_in_wgrad_{i}", send=now([k_up, k_out]))
            gbig[k_in] = jnp.stack([dwin[:, k * shard_in:(k + 1) * shard_in] for k in range(N_CHIPS)])
            dh = bwd_mm(dzx, full[k_in], "nt", F32, f"ssm_in_dgrad_{i}", send=now([k_in]), swap=now([k_up, k_out]))
        if sending is not None:
            keys, ex, handles = sending
            recv.update(zip(keys, _finish_exchange(ex, handles, dh, f"scatter_wait_{i + 1}")))
            to_swap, sending = keys, None
        if own:
            to_swap = to_swap + [k_in]
        else:
            keys = [k_in, k_out, k_up, k_down]
            ex = scatter_ex(keys)
            sending = (keys, ex, _start_exchange(ex, f"scatter_start_{i}"))
        issued = 0.0 if own else sending[2][-1][0, 0]
        dx, dnw = _rms_bwd(sv["x_in"], mix_norm_w[i] + issued, dh, dx, f"mix_norm_bwd_{i}")
        gs["mix_norm_w"][i] = dnw[0]
    grad_x = dx.reshape(x.shape)

    sib.update(zip(to_swap, _run_exchange(swap_ex(to_swap), "swap_last")))

    small_full = [jnp.stack(gs[n]) if n != "final_norm_w" else d_final[0] for n in SMALL]
    small_full.append(loss_part[0, 0:1])
    small_shapes = [a.shape for a in small_full]
    summed = _sum_devices(_all_gather_devices(_pack(small_full), "gather_small_grads"), "sum_small_grads")
    small_g = _unpack(summed, small_shapes)
    loss = small_g[-1][0]
    gsm = {}
    for n, g in zip(SMALL, small_g[:-1]):
        if n in SMALL_SHARDED:
            ax = SMALL_SHARDED[n]
            ext = W[n].shape[ax]
            g = lax.dynamic_slice_in_dim(g, chip * ext, ext, axis=ax)
        gsm[n] = g

    out_g, out_d, out_m, out_v = {}, {}, {}, {}
    for name in BIG:
        outs = None
        for l in range(W[name].shape[0]):
            outs = _adamw_big(recv[(name, l)], sib[(name, l)], W[name], M[name], V[name], l, outs,
                              f"adamw_{name}_{l}")
        out_g[name], out_d[name], out_m[name], out_v[name] = outs
    shapes = [W[n].shape for n in SMALL]
    pd, pm, pv = _adamw_small(_pack([gsm[n] for n in SMALL]), _pack([W[n] for n in SMALL]),
                              _pack([M[n] for n in SMALL]), _pack([V[n] for n in SMALL]), "adamw_small")
    for n, d_, m_, v_ in zip(SMALL, _unpack(pd, shapes), _unpack(pm, shapes), _unpack(pv, shapes)):
        out_g[n], out_d[n], out_m[n], out_v[n] = gsm[n], d_, m_, v_

    return (loss, grad_x, *[out_g[n] for n in WEIGHTS], *[out_d[n] for n in WEIGHTS],
            *[out_m[n] for n in WEIGHTS], *[out_v[n] for n in WEIGHTS])
```

```python
import math

import jax
import jax.numpy as jnp
from jax import lax
from jax.experimental import pallas as pl
from jax.experimental.pallas import tpu as pltpu

F32 = jnp.float32
BF16 = jnp.bfloat16
MESH = pl.DeviceIdType.MESH

NORM_EPS = 1e-5
HEAD_DIM = 128
ATTN_BLOCK = 128
ATTN_DILATIONS = (1, 4, 16)
ATTN_WINDOWS = (128, 512, 2048)
PERM = 16
ROPE_THETA = 500000.0
ROPE_HALF = HEAD_DIM // 8
SSM_HEAD_DIM = 64
SSM_STATE = 128
SSM_GROUPS = 8
SSM_CHUNK = 128
NEG = -1e30

ADAM_LR = 0.001
ADAM_B1 = 0.9
ADAM_B2 = 0.999
ADAM_EPS = 1e-08
ADAM_WD = 0.01
ADAM_STEP = 10

VMEM_LIMIT_BYTES = 48 * 1024 * 1024
N_CHIPS = 4
N_DEV = 8


def _cp(*sem):
    return pltpu.CompilerParams(dimension_semantics=sem, vmem_limit_bytes=VMEM_LIMIT_BYTES)


def _pick(n, pref, mult=128):
    best = None
    t = mult
    while t <= min(n, pref):
        if n % t == 0:
            best = t
        t += mult
    return n if best is None else best


def _sigmoid(x):
    return 1.0 / (1.0 + jnp.exp(-x))


def _silu(x):
    return x * _sigmoid(x)


def _softplus(x):
    u = jnp.exp(-jnp.abs(x))
    w = 1.0 + u
    log1p = jnp.where(w == 1.0, u, jnp.log(w) * (u / jnp.where(w == 1.0, 1.0, w - 1.0)))
    return jnp.maximum(x, 0.0) + log1p


def _dot(a, b, dims):
    return lax.dot_general(a.astype(BF16), b.astype(BF16), (dims, ((), ())),
                           preferred_element_type=F32)


def _dot_nn(a, b):
    return _dot(a, b, ((1,), (0,)))


def _dot_nt(a, b):
    return _dot(a, b, ((1,), (1,)))


def _dot_tn(a, b):
    return _dot(a, b, ((0,), (0,)))


MATMUL_VMEM_BYTES = 36 * 1024 * 1024
MATMUL_TILES = (2048, 1536, 1408, 1152, 1024, 896, 768, 640, 512, 384, 256, 128)


def _matmul_tiles(M, N, K, a_bytes, b_bytes, o_bytes, has_resid, fix_tn=None):
    best = None
    for tm in [t for t in MATMUL_TILES if M % t == 0] or [M]:
        for tn in [fix_tn] if fix_tn else [t for t in MATMUL_TILES if N % t == 0] or [N]:
            for tk in [t for t in MATMUL_TILES if K % t == 0 and t <= 1408] or [K]:
                nk, gm, gn = K // tk, M // tm, N // tn
                vmem = 2 * (tm * tk * a_bytes + tk * tn * b_bytes + tm * tn * o_bytes)
                vmem += (2 * tm * tn * 4 if has_resid else 0) + (tm * tn * 4 if nk > 1 else 0)
                if vmem > MATMUL_VMEM_BYTES:
                    continue
                a_reads = 1 if nk == 1 else gn
                b_reads = 1 if (nk == 1 and gn == 1) else gm
                traffic = M * K * a_bytes * a_reads + K * N * b_bytes * b_reads + M * N * o_bytes
                key = (traffic, gm * gn * nk)
                if best is None or key < best[0]:
                    best = (key, (tm, tn, tk))
    assert best is not None, (M, N, K)
    return best[1]


def _matmul(a, b, mode, out_dtype, name, resid=None, carry=None, rope=None):
    if mode == "nn":
        (M, K), (K2, N) = a.shape, b.shape
    elif mode == "nt":
        (M, K), (N, K2) = a.shape, b.shape
    else:
        (K, M), (K2, N) = a.shape, b.shape
    assert K == K2, (a.shape, b.shape, mode)
    tm, tn, tk = _matmul_tiles(M, N, K, a.dtype.itemsize, b.dtype.itemsize, jnp.dtype(out_dtype).itemsize,
                               resid is not None or rope is not None, fix_tn=rope[1] if rope else None)
    nk = K // tk
    assert not (rope and (resid is not None or nk > 1))
    gm, gn = M // tm, N // tn
    dims = {"nn": ((1,), (0,)), "nt": ((1,), (1,)), "tn": ((0,), (0,))}[mode]
    has_resid = resid is not None or rope is not None
    nci = len(carry.arrays) if carry else 0
    nco = len(carry.out_shapes) if carry else 0

    def body(a_ref, b_ref, *rest):
        r_ref = rest[0] if has_resid else None
        rest = rest[has_resid:]
        c_ins, o_ref, c_outs, scratch = rest[:nci], rest[nci], rest[nci + 1:nci + 1 + nco], rest[nci + 1 + nco:]
        acc_ref = scratch[0] if nk > 1 else None
        sems = scratch[nk > 1:]
        i, j, k = pl.program_id(0), pl.program_id(1), pl.program_id(2)
        if carry:
            @pl.when((i == 0) & (j == 0) & (k == 0))
            def _():
                carry.start(c_ins, c_outs, sems)

        if nk == 1:
            r = _dot(a_ref[...], b_ref[...], dims)
            if rope:
                @pl.when(j % 3 < 2)
                def _():
                    table = r_ref[...]
                    for h in range(tn // HEAD_DIM):
                        hs = slice(h * HEAD_DIM, (h + 1) * HEAD_DIM)
                        o_ref[:, hs] = _rope(r[:, hs], table, 1.0).astype(o_ref.dtype)

                @pl.when(j % 3 == 2)
                def _():
                    o_ref[...] = r.astype(o_ref.dtype)
            else:
                if has_resid:
                    r = r + r_ref[...]
                o_ref[...] = r.astype(o_ref.dtype)
        else:
            @pl.when(k == 0)
            def _():
                acc_ref[...] = jnp.zeros_like(acc_ref)

            acc_ref[...] += _dot(a_ref[...], b_ref[...], dims)

            @pl.when(k == nk - 1)
            def _():
                r = acc_ref[...]
                if has_resid:
                    r = r + r_ref[...]
                o_ref[...] = r.astype(o_ref.dtype)

        if carry:
            @pl.when((i == gm - 1) & (j == gn - 1) & (k == nk - 1))
            def _():
                carry.wait(c_ins, c_outs, sems)

    if mode == "nn":
        a_spec = pl.BlockSpec((tm, tk), lambda i, j, k: (i, k))
        b_spec = pl.BlockSpec((tk, tn), lambda i, j, k: (k, j))
    elif mode == "nt":
        a_spec = pl.BlockSpec((tm, tk), lambda i, j, k: (i, k))
        b_spec = pl.BlockSpec((tn, tk), lambda i, j, k: (j, k))
    else:
        a_spec = pl.BlockSpec((tk, tm), lambda i, j, k: (k, i))
        b_spec = pl.BlockSpec((tk, tn), lambda i, j, k: (k, j))
    o_spec = pl.BlockSpec((tm, tn), lambda i, j, k: (i, j))
    anyspec = pl.BlockSpec(memory_space=pl.ANY)
    extra_spec = pl.BlockSpec((tm, 3 * HEAD_DIM), lambda i, j, k: (i, 0)) if rope else o_spec
    in_specs = [a_spec, b_spec] + ([extra_spec] if has_resid else []) + [anyspec] * nci
    extra = (rope[0],) if rope else ((resid,) if has_resid else ())
    args = (a, b) + extra + (tuple(carry.arrays) if carry else ())
    out_shape = [jax.ShapeDtypeStruct((M, N), out_dtype)] + (carry.out_shapes if carry else [])
    scratch = ([] if nk == 1 else [pltpu.VMEM((tm, tn), F32)]) + (carry.scratch if carry else [])
    sem = ("arbitrary",) * 3 if carry else ("parallel", "parallel", "arbitrary")
    outs = pl.pallas_call(
        body, grid=(gm, gn, nk), in_specs=in_specs, out_specs=[o_spec] + [anyspec] * nco,
        out_shape=out_shape, scratch_shapes=scratch, compiler_params=_cp(*sem), name=name)(*args)
    return (outs[0], outs[1:]) if carry else outs[0]


def _rms_fwd(x, w, name, add=None):
    S, D = x.shape
    t = _pick(S, 512, 8)
    has_add = add is not None

    def body(x_ref, w_ref, *rest):
        xv = x_ref[...]
        if has_add:
            xv = xv + rest[0][...]
            rest[1][...] = xv
        r = lax.rsqrt(jnp.mean(xv * xv, axis=-1, keepdims=True) + NORM_EPS)
        rest[-1][...] = (xv * r * w_ref[...]).astype(rest[-1].dtype)

    row = pl.BlockSpec((t, D), lambda i: (i, 0))
    vec = pl.BlockSpec((1, D), lambda i: (0, 0))
    normed = jax.ShapeDtypeStruct((S, D), BF16)
    if not has_add:
        return pl.pallas_call(body, grid=(S // t,), in_specs=[row, vec], out_specs=row, out_shape=normed,
                              compiler_params=_cp("parallel"), name=name)(x, w.reshape(1, D))
    return pl.pallas_call(body, grid=(S // t,), in_specs=[row, vec, row], out_specs=[row, row],
                          out_shape=[jax.ShapeDtypeStruct((S, D), F32), normed],
                          compiler_params=_cp("parallel"), name=name)(x, w.reshape(1, D), add)


def _rms_bwd(x, w, dh, dres, name):
    S, D = x.shape
    t = _pick(S, 512, 8)

    def body(x_ref, w_ref, dh_ref, dr_ref, dx_ref, dw_ref):
        @pl.when(pl.program_id(0) == 0)
        def _():
            dw_ref[...] = jnp.zeros_like(dw_ref)

        xv = x_ref[...]
        r = lax.rsqrt(jnp.mean(xv * xv, axis=-1, keepdims=True) + NORM_EPS)
        xh = xv * r
        dh_v = dh_ref[...]
        g = dh_v * w_ref[...]
        dx_ref[...] = dr_ref[...] + r * (g - xh * jnp.mean(g * xh, axis=-1, keepdims=True))
        dw_ref[...] += jnp.sum(dh_v * xh, axis=0, keepdims=True)

    row = pl.BlockSpec((t, D), lambda i: (i, 0))
    vec = pl.BlockSpec((1, D), lambda i: (0, 0))
    return pl.pallas_call(
        body, grid=(S // t,), in_specs=[row, vec, row, row], out_specs=[row, vec],
        out_shape=[jax.ShapeDtypeStruct((S, D), F32), jax.ShapeDtypeStruct((1, D), F32)],
        compiler_params=_cp("arbitrary"), name=name)(x, w.reshape(1, D), dh, dres)


def _loss_head(x, w, tgt, name):
    S, D = x.shape
    t = _pick(S, 512, 8)

    def body(x_ref, w_ref, t_ref, dx_ref, dw_ref, l_ref):
        @pl.when(pl.program_id(0) == 0)
        def _():
            dw_ref[...] = jnp.zeros_like(dw_ref)
            l_ref[...] = jnp.zeros_like(l_ref)

        xv = x_ref[...]
        wv = w_ref[...]
        r = lax.rsqrt(jnp.mean(xv * xv, axis=-1, keepdims=True) + NORM_EPS)
        xh = xv * r
        err = xh * wv - t_ref[...]
        per_tok = jnp.mean(err * err, axis=-1, keepdims=True)
        l_ref[...] += 0.5 * jnp.sum(per_tok, axis=0, keepdims=True)
        dy = err * (1.0 / D)
        g = dy * wv
        dx_ref[...] = r * (g - xh * jnp.mean(g * xh, axis=-1, keepdims=True))
        dw_ref[...] += jnp.sum(dy * xh, axis=0, keepdims=True)

    row = pl.BlockSpec((t, D), lambda i: (i, 0))
    vec = pl.BlockSpec((1, D), lambda i: (0, 0))
    lspec = pl.BlockSpec((1, 128), lambda i: (0, 0))
    return pl.pallas_call(
        body, grid=(S // t,), in_specs=[row, vec, row], out_specs=[row, vec, lspec],
        out_shape=[jax.ShapeDtypeStruct((S, D), F32), jax.ShapeDtypeStruct((1, D), F32),
                   jax.ShapeDtypeStruct((1, 128), F32)],
        compiler_params=_cp("arbitrary"), name=name)(x, w.reshape(1, D), tgt)


def _rope_table(seq):
    pos = jnp.arange(seq, dtype=F32)
    inv_freq = ROPE_THETA ** (-jnp.arange(0, 2 * ROPE_HALF, 2, dtype=F32) / (2 * ROPE_HALF))
    ang = pos[:, None] * inv_freq[None, :]
    cos, sin = jnp.cos(ang), jnp.sin(ang)
    pad = HEAD_DIM - 2 * ROPE_HALF
    cos_p = jnp.concatenate([cos, cos, jnp.ones((seq, pad), F32)], axis=1)
    sin_a = jnp.concatenate([-sin, jnp.zeros((seq, HEAD_DIM - ROPE_HALF), F32)], axis=1)
    sin_b = jnp.concatenate([jnp.zeros((seq, ROPE_HALF), F32), sin, jnp.zeros((seq, pad), F32)], axis=1)
    return jnp.concatenate([cos_p, sin_a, sin_b], axis=1)


def _rope(t, tab, sign):
    cos_p = tab[:, 0:HEAD_DIM]
    sin_a = tab[:, HEAD_DIM:2 * HEAD_DIM]
    sin_b = tab[:, 2 * HEAD_DIM:3 * HEAD_DIM]
    up = pltpu.roll(t, HEAD_DIM - ROPE_HALF, 1)
    down = pltpu.roll(t, ROPE_HALF, 1)
    return t * cos_p + sign * (up * sin_a + down * sin_b)


def _perm_tokens(a):
    S = a.shape[0]
    return a.reshape(S // PERM, PERM, -1).transpose(1, 0, 2).reshape(S, -1)


def _unperm_tokens(a):
    S = a.shape[0]
    return a.reshape(PERM, S // PERM, -1).transpose(1, 0, 2).reshape(S, -1)


class _Strided:
    def __init__(self, S, dil):
        self.dil, self.m = dil, PERM // dil
        self.c = ATTN_BLOCK // self.m
        self.rows = S // PERM
        self.nb = S // (dil * ATTN_BLOCK)

    def view(self, a):
        return a.reshape(self.m, self.dil, self.rows, a.shape[-1])

    def spec(self, width, col, f=lambda n: n):
        return pl.BlockSpec((self.m, None, self.c, width), lambda r, n: (0, r, f(n), col))

    def load(self, ref, sl=slice(None)):
        if self.m == 1:
            return ref[0, :, sl]
        return jnp.concatenate([ref[q, :, sl] for q in range(self.m)], axis=0)

    def store(self, ref, sl, val):
        for q in range(self.m):
            ref[q, :, sl] = val[q * self.c:(q + 1) * self.c, :]

    def member(self, i):
        shift = self.c.bit_length() - 1
        return (i & (self.c - 1)) * self.m + (i >> shift)


def _attn_fwd(qkv, g, heads, name):
    S = qkv.shape[0]
    W = heads * HEAD_DIM
    dil = ATTN_DILATIONS[g]
    steps = ATTN_WINDOWS[g] // dil
    B = ATTN_BLOCK
    scale = HEAD_DIM ** -0.5
    st = _Strided(S, dil)

    def body(q_ref, k_ref, v_ref, o_ref, l_ref, kp_scr, vp_scr):
        n = pl.program_id(1)

        @pl.when(n == 0)
        def _():
            kp_scr[...] = jnp.zeros_like(kp_scr)
            vp_scr[...] = jnp.zeros_like(vp_scr)

        ii = lax.broadcasted_iota(jnp.int32, (B, 2 * B), 0)
        jj = lax.broadcasted_iota(jnp.int32, (B, 2 * B), 1)
        delta = st.member(ii) - st.member(jj & (B - 1)) + jnp.where(jj >= B, 0, B)
        ok = (delta >= 0) & (delta <= steps) & ((jj >= B) | (n > 0))
        sls = [slice(h * HEAD_DIM, (h + 1) * HEAD_DIM) for h in range(heads)]
        qs = [st.load(q_ref, sl) for sl in sls]
        ks = [st.load(k_ref, sl) for sl in sls]
        vs = [st.load(v_ref, sl) for sl in sls]
        kcs = [jnp.concatenate([kp_scr[:, sl], k], axis=0) for sl, k in zip(sls, ks)]
        ss = [jnp.where(ok, _dot_nt(q, kc) * scale, NEG) for q, kc in zip(qs, kcs)]
        ms = [jnp.max(s, axis=-1, keepdims=True) for s in ss]
        ps = [jnp.exp(s - m) for s, m in zip(ss, ms)]
        dens = [jnp.sum(p, axis=-1, keepdims=True) for p in ps]
        vcs = [jnp.concatenate([vp_scr[:, sl], v], axis=0) for sl, v in zip(sls, vs)]
        outs = [_dot_nn(p, vc) for p, vc in zip(ps, vcs)]
        for sl, o, m, den, k, v in zip(sls, outs, ms, dens, ks, vs):
            st.store(o_ref, sl, o / den)
            st.store(l_ref, sl, jnp.broadcast_to(m + jnp.log(den), (B, HEAD_DIM)))
            kp_scr[:, sl] = k
            vp_scr[:, sl] = v

    qv = st.view(qkv)
    o_spec = st.spec(W, 0)
    o, lse = pl.pallas_call(
        body, grid=(dil, st.nb),
        in_specs=[st.spec(W, g * 3), st.spec(W, g * 3 + 1), st.spec(W, g * 3 + 2)],
        out_specs=[o_spec, o_spec],
        out_shape=[jax.ShapeDtypeStruct((st.m, dil, st.rows, W), F32)] * 2,
        scratch_shapes=[pltpu.VMEM((B, W), F32), pltpu.VMEM((B, W), F32)],
        compiler_params=_cp("parallel", "arbitrary"), name=name)(qv, qv, qv)
    return o.reshape(S, W), lse.reshape(S, W)


def _attn_combine(os_, ls_, name):
    S, W = os_[0].shape
    t = _pick(S, 256, 8)

    def body(o0, o1, o2, l0, l1, l2, o_ref, l_ref):
        a, b, c = l0[...], l1[...], l2[...]
        m = jnp.maximum(jnp.maximum(a, b), c)
        ea, eb, ec = jnp.exp(a - m), jnp.exp(b - m), jnp.exp(c - m)
        tot = ea + eb + ec
        o_ref[...] = (ea * o0[...] + eb * o1[...] + ec * o2[...]) / tot
        l_ref[...] = m + jnp.log(tot)

    row = pl.BlockSpec((t, W), lambda i: (i, 0))
    return pl.pallas_call(body, grid=(S // t,), in_specs=[row] * 6, out_specs=[row, row],
                          out_shape=[jax.ShapeDtypeStruct((S, W), F32)] * 2,
                          compiler_params=_cp("parallel"), name=name)(*os_, *ls_)


def _attn_bwd(qkv, tab, o, lse, do, dqkv_prev, g, heads, name):
    S = qkv.shape[0]
    W = heads * HEAD_DIM
    dil = ATTN_DILATIONS[g]
    steps = ATTN_WINDOWS[g] // dil
    B = ATTN_BLOCK
    scale = HEAD_DIM ** -0.5
    st = _Strided(S, dil)
    nb = st.nb
    aliased = dqkv_prev is not None

    def body(q_ref, qn_ref, k_ref, v_ref, do_ref, don_ref, o_ref, on_ref, l_ref, ln_ref, t_ref, *rest):
        out_ref, kp_scr, vp_scr = rest[-3:]
        n = pl.program_id(1)
        has_next = n < nb - 1

        @pl.when(n == 0)
        def _():
            kp_scr[...] = jnp.zeros_like(kp_scr)
            vp_scr[...] = jnp.zeros_like(vp_scr)

        ia = lax.broadcasted_iota(jnp.int32, (B, 2 * B), 0)
        ja = lax.broadcasted_iota(jnp.int32, (B, 2 * B), 1)
        da = st.member(ia) - st.member(ja & (B - 1)) + jnp.where(ja >= B, 0, B)
        ok_a = (da >= 0) & (da <= steps) & ((ja >= B) | (n > 0))
        ib = lax.broadcasted_iota(jnp.int32, (2 * B, B), 0)
        jb = lax.broadcasted_iota(jnp.int32, (2 * B, B), 1)
        db = st.member(ib & (B - 1)) + jnp.where(ib >= B, B, 0) - st.member(jb)
        ok_b = (db >= 0) & (db <= steps) & ((ib < B) | has_next)
        tb = st.load(t_ref)
        hs = range(heads)
        sls = [slice(h * HEAD_DIM, (h + 1) * HEAD_DIM) for h in hs]
        qr = [st.load(q_ref, sl) for sl in sls]
        qnr = [st.load(qn_ref, sl) for sl in sls]
        kr = [st.load(k_ref, sl) for sl in sls]
        kpr = [kp_scr[:, sl] for sl in sls]
        v = [st.load(v_ref, sl) for sl in sls]
        dov_ = [st.load(do_ref, sl) for sl in sls]
        donv = [st.load(don_ref, sl) for sl in sls]
        dl = [jnp.sum(dov_[h] * st.load(o_ref, sls[h]), axis=-1, keepdims=True) for h in hs]
        dln = [jnp.sum(donv[h] * st.load(on_ref, sls[h]), axis=-1, keepdims=True) for h in hs]
        ls = [st.load(l_ref, sl) for sl in sls]
        kc = [jnp.concatenate([kpr[h], kr[h]], axis=0) for h in hs]
        vc = [jnp.concatenate([vp_scr[:, sls[h]], v[h]], axis=0) for h in hs]
        qc = [jnp.concatenate([qr[h], qnr[h]], axis=0) for h in hs]
        doc = [jnp.concatenate([dov_[h], donv[h]], axis=0) for h in hs]
        lc = [jnp.concatenate([ls[h], st.load(ln_ref, sls[h])], axis=0) for h in hs]
        dlc = [jnp.concatenate([dl[h], dln[h]], axis=0) for h in hs]
        s = [_dot_nt(qr[h], kc[h]) * scale for h in hs]
        dp = [_dot_nt(dov_[h], vc[h]) for h in hs]
        s2 = [_dot_nt(qc[h], kr[h]) * scale for h in hs]
        dp2 = [_dot_nt(doc[h], v[h]) for h in hs]
        p = [jnp.where(ok_a, jnp.exp(jnp.minimum(s[h] - jnp.concatenate([ls[h], ls[h]], axis=1), 30.0)), 0.0)
             for h in hs]
        ds = [p[h] * (dp[h] - dl[h]) * scale for h in hs]
        p2 = [jnp.where(ok_b, jnp.exp(jnp.minimum(s2[h] - lc[h], 30.0)), 0.0) for h in hs]
        ds2 = [p2[h] * (dp2[h] - dlc[h]) * scale for h in hs]
        dq = [_dot_nn(ds[h], kc[h]) for h in hs]
        dk = [_dot_tn(ds2[h], qc[h]) for h in hs]
        dv = [_dot_tn(p2[h], doc[h]) for h in hs]
        for h in hs:
            st.store(out_ref, sls[h], _rope(dq[h], tb, -1.0))
            st.store(out_ref, slice(W + h * HEAD_DIM, W + (h + 1) * HEAD_DIM), _rope(dk[h], tb, -1.0))
            st.store(out_ref, slice(2 * W + h * HEAD_DIM, 2 * W + (h + 1) * HEAD_DIM), dv[h])
            kp_scr[:, sls[h]] = kr[h]
            vp_scr[:, sls[h]] = v[h]

    nxt = lambda n: jnp.minimum(n + 1, nb - 1)
    same = lambda n: n
    q0, q1, q2, tw = g * 3, g * 3 + 1, g * 3 + 2, 3 * HEAD_DIM
    in_specs = [st.spec(W, q0), st.spec(W, q0, nxt), st.spec(W, q1), st.spec(W, q2)]
    in_specs += [st.spec(W, 0, f) for f in (same, nxt, same, nxt, same, nxt)]
    in_specs += [st.spec(tw, 0)]
    qv, tv, ov, lv, dov = (st.view(a) for a in (qkv, tab, o, lse, do))
    args = [qv, qv, qv, qv, dov, dov, ov, ov, lv, lv, tv]
    kwargs = {}
    if aliased:
        in_specs.append(pl.BlockSpec(memory_space=pl.ANY))
        args.append(st.view(dqkv_prev))
        kwargs["input_output_aliases"] = {len(args) - 1: 0}
    out = pl.pallas_call(
        body, grid=(dil, nb), in_specs=in_specs, out_specs=st.spec(3 * W, g),
        out_shape=jax.ShapeDtypeStruct((st.m, dil, st.rows, 9 * W), F32),
        scratch_shapes=[pltpu.VMEM((B, W), F32), pltpu.VMEM((B, W), F32)],
        compiler_params=_cp("parallel", "arbitrary"), name=name, **kwargs)(*args)
    return out.reshape(S, 9 * W)


def _shift_down(x, halo, s):
    if s == 0:
        return x
    T = x.shape[0]
    xs = pltpu.roll(x, s, 0)
    hs = pltpu.roll(halo, s, 0)
    row8 = lax.broadcasted_iota(jnp.int32, hs.shape, 0)
    top = jnp.where(row8 < s, hs, xs[0:8])
    return top if T == 8 else jnp.concatenate([top, xs[8:T]], axis=0)


def _shift_up(x, halo, s):
    if s == 0:
        return x
    T = x.shape[0]
    xs = pltpu.roll(x, T - s, 0)
    hs = pltpu.roll(halo, 8 - s, 0)
    row8 = lax.broadcasted_iota(jnp.int32, hs.shape, 0)
    bot = jnp.where(row8 >= 8 - s, hs, xs[T - 8:T])
    return jnp.concatenate([xs[0:T - 8], bot], axis=0)


CONV_ROWS = 128
FFN_CONV_ROWS = 256
CONV_LANES = 512


def _conv_apply(x, halo, w_ref, wsl, b, K):
    acc = x * w_ref[K - 1, :, wsl] + b
    for s in range(1, K):
        acc = acc + _shift_down(x, halo, s) * w_ref[K - 1 - s, :, wsl]
    return acc


def _conv_accum(dy, dyn, xv, w_ref, dw_ref, db_ref, wsl, K):
    acc = dy * w_ref[K - 1, :, wsl]
    dw_ref[K - 1, :, wsl] += jnp.sum(dy * xv, axis=0, keepdims=True)
    for s in range(1, K):
        ahead = _shift_up(dy, dyn, s)
        acc = acc + ahead * w_ref[K - 1 - s, :, wsl]
        dw_ref[K - 1 - s, :, wsl] += jnp.sum(ahead * xv, axis=0, keepdims=True)
    db_ref[:, wsl] += jnp.sum(dy, axis=0, keepdims=True)
    return acc


def _row_specs(T, S, width):
    main = pl.BlockSpec((T, width), lambda i: (i, 0))
    prev = pl.BlockSpec((8, width), lambda i: (jnp.maximum(i * (T // 8) - 1, 0), 0))
    nxt = pl.BlockSpec((8, width), lambda i: (jnp.minimum((i + 1) * (T // 8), S // 8 - 1), 0))
    return main, prev, nxt


def _full(shape):
    return pl.BlockSpec(shape, lambda i: (0,) * len(shape))


def _silu_grad(y):
    sg = _sigmoid(y)
    return sg * (1.0 + y * (1.0 - sg))


def _ssm_conv_fwd(zx, w, b, d_inner, conv_dim, name):
    S, wz = zx.shape
    K = w.shape[0]
    T = _pick(S, CONV_ROWS, 8)
    cw = _pick(conv_dim, CONV_LANES)

    def body(x_ref, h_ref, w_ref, b_ref, c_ref):
        has_prev = pl.program_id(0) > 0
        for cs in range(0, conv_dim, cw):
            so, sx = slice(cs, cs + cw), slice(d_inner + cs, d_inner + cs + cw)
            halo = jnp.where(has_prev, h_ref[:, sx], 0.0)
            c_ref[:, so] = _conv_apply(x_ref[:, sx], halo, w_ref, so, b_ref[:, so], K)

    main, prev, _ = _row_specs(T, S, wz)
    return pl.pallas_call(
        body, grid=(S // T,), in_specs=[main, prev, _full((K, 1, conv_dim)), _full((1, conv_dim))],
        out_specs=pl.BlockSpec((T, conv_dim), lambda i: (i, 0)),
        out_shape=jax.ShapeDtypeStruct((S, conv_dim), F32),
        compiler_params=_cp("parallel"), name=name)(zx, zx, w.reshape(K, 1, conv_dim), b.reshape(1, conv_dim))


def _ssm_conv_bwd(dxs, dbm, dcm, conv, zx, dz, ddt, w, d_inner, name):
    S, wz = zx.shape
    K, conv_dim = w.shape
    gn = dbm.shape[1]
    T = _pick(S, CONV_ROWS, 8)
    cw = _pick(math.gcd(d_inner, gn), CONV_LANES)
    nrow = S // T
    tail = wz - d_inner - conv_dim
    assert ddt.shape[1] == tail

    def body(dx_ref, dxn_ref, db_ref_, dbn_ref, dc_ref, dcn_ref, y_ref, yn_ref, x_ref, dz_ref, ddt_ref,
             w_ref, o_ref, dw_ref, dbias_ref):
        i = pl.program_id(0)

        @pl.when(i == 0)
        def _():
            dw_ref[...] = jnp.zeros_like(dw_ref)
            dbias_ref[...] = jnp.zeros_like(dbias_ref)

        has_next = i < nrow - 1
        for cs in range(0, d_inner, cw):
            o_ref[:, cs:cs + cw] = dz_ref[:, cs:cs + cw].astype(o_ref.dtype)
        o_ref[:, d_inner + conv_dim:wz] = ddt_ref[...].astype(o_ref.dtype)
        for cs in range(0, conv_dim, cw):
            so, sx = slice(cs, cs + cw), slice(d_inner + cs, d_inner + cs + cw)
            if cs < d_inner:
                src, srcn, ss = dx_ref, dxn_ref, slice(cs, cs + cw)
            elif cs < d_inner + gn:
                src, srcn, ss = db_ref_, dbn_ref, slice(cs - d_inner, cs - d_inner + cw)
            else:
                src, srcn, ss = dc_ref, dcn_ref, slice(cs - d_inner - gn, cs - d_inner - gn + cw)
            dy = src[:, ss] * _silu_grad(y_ref[:, so])
            dyn = jnp.where(has_next, srcn[:, ss] * _silu_grad(yn_ref[:, so]), 0.0)
            o_ref[:, sx] = _conv_accum(dy, dyn, x_ref[:, sx], w_ref, dw_ref, dbias_ref, so, K).astype(o_ref.dtype)

    xm, _, xn = _row_specs(T, S, d_inner)
    gm, _, gnx = _row_specs(T, S, gn)
    cm, _, cn = _row_specs(T, S, conv_dim)
    zm, _, _ = _row_specs(T, S, wz)
    tm_, _, _ = _row_specs(T, S, tail)
    dzx, dw, db = pl.pallas_call(
        body, grid=(nrow,),
        in_specs=[xm, xn, gm, gnx, gm, gnx, cm, cn, zm, xm, tm_, _full((K, 1, conv_dim))],
        out_specs=[zm, _full((K, 1, conv_dim)), _full((1, conv_dim))],
        out_shape=[jax.ShapeDtypeStruct((S, wz), BF16), jax.ShapeDtypeStruct((K, 1, conv_dim), F32),
                   jax.ShapeDtypeStruct((1, conv_dim), F32)],
        compiler_params=_cp("arbitrary"), name=name)(
            dxs, dxs, dbm, dbm, dcm, dcm, conv, conv, zx, dz, ddt, w.reshape(K, 1, conv_dim))
    return dzx, dw.reshape(K, conv_dim), db


def _ffn_conv_fwd(up, w, b, name):
    S, C = up.shape
    F = C // 2
    K = w.shape[0]
    T = _pick(S, FFN_CONV_ROWS, 8)
    cw = _pick(F, CONV_LANES)

    def body(x_ref, h_ref, w_ref, b_ref, a_ref):
        has_prev = pl.program_id(0) > 0
        for cs in range(0, F, cw):
            sg, su = slice(cs, cs + cw), slice(F + cs, F + cs + cw)
            gate = _conv_apply(x_ref[:, sg], jnp.where(has_prev, h_ref[:, sg], 0.0), w_ref, sg, b_ref[:, sg], K)
            upv = _conv_apply(x_ref[:, su], jnp.where(has_prev, h_ref[:, su], 0.0), w_ref, su, b_ref[:, su], K)
            a_ref[:, sg] = (gate * _sigmoid(gate) * upv).astype(a_ref.dtype)

    main, prev, _ = _row_specs(T, S, C)
    return pl.pallas_call(
        body, grid=(S // T,), in_specs=[main, prev, _full((K, 1, C)), _full((1, C))],
        out_specs=pl.BlockSpec((T, F), lambda i: (i, 0)), out_shape=jax.ShapeDtypeStruct((S, F), BF16),
        compiler_params=_cp("parallel"), name=name)(up, up, w.reshape(K, 1, C), b.reshape(1, C))


def _ffn_conv_bwd(dact, up, w, b, name):
    S, C = up.shape
    F = C // 2
    K = w.shape[0]
    T = _pick(S, FFN_CONV_ROWS, 8)
    cw = _pick(F, CONV_LANES)
    nrow = S // T

    def du(gate, upv, d):
        sg = _sigmoid(gate)
        return d * upv * sg * (1.0 + gate * (1.0 - sg)), d * gate * sg

    def body(d_ref, dn_ref, x_ref, xp_ref, xn_ref, w_ref, b_ref, dx_ref, dw_ref, db_ref):
        i = pl.program_id(0)

        @pl.when(i == 0)
        def _():
            dw_ref[...] = jnp.zeros_like(dw_ref)
            db_ref[...] = jnp.zeros_like(db_ref)

        has_prev, has_next = i > 0, i < nrow - 1
        for cs in range(0, F, cw):
            sf = slice(cs, cs + cw)
            cols = [slice(half * F + cs, half * F + cs + cw) for half in range(2)]
            xs = [x_ref[:, sc] for sc in cols]
            xps = [jnp.where(has_prev, xp_ref[:, sc], 0.0) for sc in cols]
            u = [_conv_apply(xs[q], xps[q], w_ref, cols[q], b_ref[:, cols[q]], K) for q in range(2)]
            un = [_conv_apply(xn_ref[:, cols[q]], xs[q][T - 8:T], w_ref, cols[q], b_ref[:, cols[q]], K)
                  for q in range(2)]
            dys = du(u[0], u[1], d_ref[:, sf])
            dyns = du(un[0], un[1], dn_ref[:, sf])
            for q in range(2):
                dyn = jnp.where(has_next, dyns[q], 0.0)
                dx_ref[:, cols[q]] = _conv_accum(dys[q], dyn, xs[q], w_ref, dw_ref, db_ref, cols[q],
                                                 K).astype(dx_ref.dtype)

    am, _, an = _row_specs(T, S, F)
    xm, xp_, xn_ = _row_specs(T, S, C)
    dx, dw, db = pl.pallas_call(
        body, grid=(nrow,), in_specs=[am, an, xm, xp_, xn_, _full((K, 1, C)), _full((1, C))],
        out_specs=[xm, _full((K, 1, C)), _full((1, C))],
        out_shape=[jax.ShapeDtypeStruct((S, C), BF16), jax.ShapeDtypeStruct((K, 1, C), F32),
                   jax.ShapeDtypeStruct((1, C), F32)],
        compiler_params=_cp("arbitrary"), name=name)(dact, dact, up, up, up, w.reshape(K, 1, C), b.reshape(1, C))
    return dx, dw.reshape(K, C), db


def _cumsum_rows(v):
    n = v.shape[0]
    row = lax.broadcasted_iota(jnp.int32, v.shape, 0)
    k = 1
    while k < n:
        v = v + jnp.where(row >= k, pltpu.roll(v, k, 0), 0.0)
        k *= 2
    return v


def _rev_cumsum_rows(v):
    n = v.shape[0]
    row = lax.broadcasted_iota(jnp.int32, v.shape, 0)
    k = 1
    while k < n:
        v = v + jnp.where(row < n - k, pltpu.roll(v, n - k, 0), 0.0)
        k *= 2
    return v


def _ssd_common(x_ref, dtr_ref, bias_ref, alog_ref, gw):
    Q = SSM_CHUNK
    X = _silu(x_ref[...])
    pre = dtr_ref[...] + bias_ref[...]
    dt = _softplus(pre)
    a = -jnp.exp(alog_ref[...])
    cs = _cumsum_rows(dt * a)
    row = lax.broadcasted_iota(jnp.int32, (Q, gw), 0)
    cs_last = jnp.sum(jnp.where(row == Q - 1, cs, 0.0), axis=0, keepdims=True)
    return X, pre, dt, a, cs, cs_last, row


def _head_decay(cs, head_mask):
    Q = SSM_CHUNK
    col = jnp.max(jnp.where(head_mask, cs, NEG), axis=1, keepdims=True)
    acol = jnp.broadcast_to(col, (Q, Q))
    arow = acol.T
    ii = lax.broadcasted_iota(jnp.int32, (Q, Q), 0)
    jj = lax.broadcasted_iota(jnp.int32, (Q, Q), 1)
    tril = ii >= jj
    return jnp.where(tril, jnp.exp(jnp.where(tril, acol - arow, 0.0)), 0.0), tril


class _Win:
    def __init__(self, ref, idx):
        self.ref, self.idx = ref, idx

    def __getitem__(self, _):
        return self.ref[self.idx]

    def __setitem__(self, _, value):
        self.ref[self.idx] = value


def _ssd_windows(g, d_inner, gw):
    N, G = SSM_STATE, SSM_GROUPS
    rows = slice(None)
    x = (rows, slice(g * gw, (g + 1) * gw))
    b = (rows, slice(d_inner + g * N, d_inner + (g + 1) * N))
    c = (rows, slice(d_inner + (G + g) * N, d_inner + (G + g + 1) * N))
    n = (rows, slice(g * N, (g + 1) * N))
    return x, b, c, n


def _ssd_fwd(xbc, dtr, bias, alog, dsk, d_inner, name):
    S = xbc.shape[0]
    Q, N, G, P = SSM_CHUNK, SSM_STATE, SSM_GROUPS, SSM_HEAD_DIM
    gw = d_inner // G
    R = gw // P
    nc = S // Q

    def body(conv_ref, dtr_ref, bias_ref, alog_ref, d_ref, y_ref, sp_ref, s_scr):
        first = pl.program_id(0) == 0
        for g in range(G):
            x, b, c, _ = _ssd_windows(g, d_inner, gw)
            one_group(first, _Win(conv_ref, x), _Win(conv_ref, b), _Win(conv_ref, c), _Win(dtr_ref, x),
                      _Win(bias_ref, g), _Win(alog_ref, g), _Win(d_ref, g), _Win(y_ref, x), _Win(sp_ref, g),
                      _Win(s_scr, g))

    def one_group(first, x_ref, b_ref, c_ref, dtr_ref, bias_ref, alog_ref, d_ref, y_ref, sp_ref, s_scr):
        @pl.when(first)
        def _():
            s_scr[...] = jnp.zeros((gw, N), F32)

        X, _, dt, a, cs, cs_last, row = _ssd_common(x_ref, dtr_ref, bias_ref, alog_ref, gw)
        Bm, Cm = _silu(b_ref[...]), _silu(c_ref[...])
        xdt = X * dt
        lane = lax.broadcasted_iota(jnp.int32, (Q, gw), 1)
        sprev = s_scr[...]
        sp_ref[...] = sprev
        cb = _dot_nt(Cm, Bm)
        y = jnp.exp(cs) * _dot_nt(Cm, sprev)
        hms = [(lane >= r * P) & (lane < (r + 1) * P) for r in range(R)]
        dec_ls = [_head_decay(cs, hm)[0] for hm in hms]
        for part in [_dot_nn(cb * dec_l, jnp.where(hm, xdt, 0.0)) for dec_l, hm in zip(dec_ls, hms)]:
            y = y + part
        dec = jnp.exp(cs_last - cs)
        cd = jnp.exp(jnp.broadcast_to(cs_last, (Q, gw)).T)
        s_scr[...] = sprev * cd + _dot_tn(xdt * dec, Bm)
        y_ref[...] = y + d_ref[...] * X

    conv_dim = xbc.shape[1]
    row = pl.BlockSpec((Q, d_inner), lambda c: (c, 0))
    p_spec = _full((G, 1, gw))
    return pl.pallas_call(
        body, grid=(nc,),
        in_specs=[pl.BlockSpec((Q, conv_dim), lambda c: (c, 0)), row, p_spec, p_spec, p_spec],
        out_specs=[row, pl.BlockSpec((None, G, gw, N), lambda c: (c, 0, 0, 0))],
        out_shape=[jax.ShapeDtypeStruct((S, d_inner), F32), jax.ShapeDtypeStruct((nc, G, gw, N), F32)],
        scratch_shapes=[pltpu.VMEM((G, gw, N), F32)],
        compiler_params=_cp("arbitrary"), name=name)(xbc, dtr, bias, alog, dsk)


def _ssd_bwd(xbc, dtr, bias, alog, dsk, sprev_all, dy, d_inner, name):
    S = xbc.shape[0]
    Q, N, G, P = SSM_CHUNK, SSM_STATE, SSM_GROUPS, SSM_HEAD_DIM
    gw = d_inner // G
    R = gw // P
    nc = S // Q

    def body(conv_ref, dtr_ref, bias_ref, alog_ref, d_ref, sp_ref, dy_ref,
             dx_ref, db_ref, dc_ref, ddt_ref, dbias_ref, dalog_ref, dd_ref, ds_scr):
        first = pl.program_id(0) == 0
        for g in range(G):
            x, b, c, n = _ssd_windows(g, d_inner, gw)
            one_group(first, _Win(conv_ref, x), _Win(conv_ref, b), _Win(conv_ref, c), _Win(dtr_ref, x),
                      _Win(bias_ref, g), _Win(alog_ref, g), _Win(d_ref, g), _Win(sp_ref, g), _Win(dy_ref, x),
                      _Win(dx_ref, x), _Win(db_ref, n), _Win(dc_ref, n), _Win(ddt_ref, x), _Win(dbias_ref, g),
                      _Win(dalog_ref, g), _Win(dd_ref, g), _Win(ds_scr, g))

    def one_group(first, x_ref, b_ref, c_ref, dtr_ref, bias_ref, alog_ref, d_ref, sp_ref, dy_ref,
                  dx_ref, db_ref, dc_ref, ddt_ref, dbias_ref, dalog_ref, dd_ref, ds_scr):
        @pl.when(first)
        def _():
            ds_scr[...] = jnp.zeros((gw, N), F32)
            dbias_ref[...] = jnp.zeros((1, gw), F32)
            dalog_ref[...] = jnp.zeros((1, gw), F32)
            dd_ref[...] = jnp.zeros((1, gw), F32)

        X, pre, dt, a, cs, cs_last, row = _ssd_common(x_ref, dtr_ref, bias_ref, alog_ref, gw)
        Bm, Cm = _silu(b_ref[...]), _silu(c_ref[...])
        dY = dy_ref[...]
        sprev = sp_ref[...]
        dsn = ds_scr[...]
        xdt = X * dt
        lane = lax.broadcasted_iota(jnp.int32, (Q, gw), 1)
        lane1 = lax.broadcasted_iota(jnp.int32, (1, gw), 1)
        srow = lax.broadcasted_iota(jnp.int32, (gw, N), 0)
        ecs = jnp.exp(cs)
        dec = jnp.exp(cs_last - cs)
        cd = jnp.exp(jnp.broadcast_to(cs_last, (Q, gw)).T)
        dd_ref[...] += jnp.sum(dY * X, axis=0, keepdims=True)
        dX = d_ref[...] * dY
        ey = ecs * dY
        dcs = ey * _dot_nt(Cm, sprev)
        dC = _dot_nn(ey, sprev)
        ds_scr[...] = cd * dsn + _dot_tn(ey, Cm)
        wmat = _dot_nt(Bm, dsn)
        dxdt = dec * wmat
        xd = xdt * dec
        dB = _dot_nn(xd, dsn)
        ddec = xdt * wmat * dec
        dcs = dcs - ddec
        dlast = jnp.sum(ddec, axis=0, keepdims=True)
        qmat = dsn * sprev * cd
        cb = _dot_nt(Cm, Bm)
        dcb = jnp.zeros((Q, Q), F32)
        dcs_rep = jnp.zeros((Q, gw), F32)
        dtx_rep = jnp.zeros((Q, gw), F32)
        hms = [(lane >= r * P) & (lane < (r + 1) * P) for r in range(R)]
        decs = [_head_decay(cs, hm) for hm in hms]
        dyrs = [jnp.where(hm, dY, 0.0) for hm in hms]
        graws = [_dot_nt(dyr, xdt) for dyr in dyrs]
        backs = [_dot_tn(cb * dec_l, dyr) for (dec_l, _), dyr in zip(decs, dyrs)]
        for r in range(R):
            hm, (dec_l, tril) = hms[r], decs[r]
            gmat = jnp.where(tril, graws[r], 0.0)
            dcb = dcb + gmat * dec_l
            e = gmat * cb * dec_l
            v = (jnp.sum(e, axis=1, keepdims=True) - jnp.sum(e.T, axis=1, keepdims=True)
                 + jnp.sum(jnp.where(hm, dcs, 0.0), axis=1, keepdims=True))
            dxdt = dxdt + backs[r]
            hm1 = (lane1 >= r * P) & (lane1 < (r + 1) * P)
            t_last = (jnp.sum(jnp.where(hm1, dlast, 0.0), axis=1, keepdims=True)
                      + jnp.sum(jnp.where((srow >= r * P) & (srow < (r + 1) * P), qmat, 0.0), keepdims=True))
            dcs_rep = dcs_rep + jnp.where(hm, v, 0.0) + jnp.where(hm & (row == Q - 1), t_last, 0.0)
        for r in range(R):
            hm = (lane >= r * P) & (lane < (r + 1) * P)
            w_r = jnp.sum(jnp.where(hm, dxdt * X, 0.0), axis=1, keepdims=True)
            dtx_rep = dtx_rep + jnp.where(hm, w_r, 0.0)
        dadt = _rev_cumsum_rows(dcs_rep)
        ddt = a * dadt + dtx_rep
        dalog_ref[...] += jnp.sum(dt * dadt, axis=0, keepdims=True) * a
        draw = ddt * _sigmoid(pre)
        ddt_ref[...] = draw
        dbias_ref[...] += jnp.sum(draw, axis=0, keepdims=True)
        dx_ref[...] = dX + dxdt * dt
        db_ref[...] = dB + _dot_tn(dcb, Cm)
        dc_ref[...] = dC + _dot_nn(dcb, Bm)

    conv_dim = xbc.shape[1]
    rev = lambda c: nc - 1 - c
    row = pl.BlockSpec((Q, d_inner), lambda c: (rev(c), 0))
    n_spec = pl.BlockSpec((Q, G * N), lambda c: (rev(c), 0))
    s_spec = pl.BlockSpec((None, G, gw, N), lambda c: (rev(c), 0, 0, 0))
    p_spec = _full((G, 1, gw))
    gshape = jax.ShapeDtypeStruct((G, 1, gw), F32)
    return pl.pallas_call(
        body, grid=(nc,),
        in_specs=[pl.BlockSpec((Q, conv_dim), lambda c: (rev(c), 0)), row, p_spec, p_spec, p_spec, s_spec, row],
        out_specs=[row, n_spec, n_spec, row, p_spec, p_spec, p_spec],
        out_shape=[jax.ShapeDtypeStruct((S, d_inner), F32), jax.ShapeDtypeStruct((S, G * N), F32),
                   jax.ShapeDtypeStruct((S, G * N), F32), jax.ShapeDtypeStruct((S, d_inner), F32),
                   gshape, gshape, gshape],
        scratch_shapes=[pltpu.VMEM((G, gw, N), F32)],
        compiler_params=_cp("arbitrary"), name=name)(xbc, dtr, bias, alog, dsk, sprev_all, dy)


def _gnorm_fwd(y, zx, w, name):
    S, d_inner = y.shape
    G = SSM_GROUPS
    gw = d_inner // G
    T = _pick(S, 256, 8)

    def body(y_ref, z_ref, w_ref, o_ref):
        for k in range(G):
            sl = slice(k * gw, (k + 1) * gw)
            z = z_ref[:, sl]
            gk = y_ref[:, sl] * z * _sigmoid(z)
            r = lax.rsqrt(jnp.mean(gk * gk, axis=-1, keepdims=True) + NORM_EPS)
            o_ref[:, sl] = (gk * r * w_ref[:, sl]).astype(o_ref.dtype)

    row = pl.BlockSpec((T, d_inner), lambda i: (i, 0))
    vec = pl.BlockSpec((1, d_inner), lambda i: (0, 0))
    return pl.pallas_call(body, grid=(S // T,), in_specs=[row, row, vec], out_specs=row,
                          out_shape=jax.ShapeDtypeStruct((S, d_inner), BF16),
                          compiler_params=_cp("parallel"), name=name)(y, zx, w.reshape(1, d_inner))


def _gnorm_bwd(y, zx, w, dout, name):
    S, d_inner = y.shape
    G = SSM_GROUPS
    gw = d_inner // G
    T = _pick(S, 256, 8)

    def body(y_ref, z_ref, w_ref, d_ref, dy_ref, dz_ref, dw_ref):
        @pl.when(pl.program_id(0) == 0)
        def _():
            dw_ref[...] = jnp.zeros_like(dw_ref)

        for k in range(G):
            sl = slice(k * gw, (k + 1) * gw)
            z, yv, d = z_ref[:, sl], y_ref[:, sl], d_ref[:, sl]
            sg = _sigmoid(z)
            sz = z * sg
            gk = yv * sz
            r = lax.rsqrt(jnp.mean(gk * gk, axis=-1, keepdims=True) + NORM_EPS)
            gh = gk * r
            dw_ref[:, sl] += jnp.sum(d * gh, axis=0, keepdims=True)
            dg = d * w_ref[:, sl]
            dgk = r * (dg - gh * jnp.mean(dg * gh, axis=-1, keepdims=True))
            dy_ref[:, sl] = dgk * sz
            dz_ref[:, sl] = dgk * yv * sg * (1.0 + z * (1.0 - sg))

    row = pl.BlockSpec((T, d_inner), lambda i: (i, 0))
    vec = pl.BlockSpec((1, d_inner), lambda i: (0, 0))
    return pl.pallas_call(
        body, grid=(S // T,), in_specs=[row, row, vec, row], out_specs=[row, row, vec],
        out_shape=[jax.ShapeDtypeStruct((S, d_inner), F32)] * 2 + [jax.ShapeDtypeStruct((1, d_inner), F32)],
        compiler_params=_cp("arbitrary"), name=name)(y, zx, w.reshape(1, d_inner), dout)


def _adam_math(g, w, m, v):
    m = ADAM_B1 * m + (1.0 - ADAM_B1) * g
    v = ADAM_B2 * v + (1.0 - ADAM_B2) * (g * g)
    m_hat = m / (1.0 - ADAM_B1 ** ADAM_STEP)
    v_hat = v / (1.0 - ADAM_B2 ** ADAM_STEP)
    delta = -ADAM_LR * (m_hat / (jnp.sqrt(v_hat) + ADAM_EPS) + ADAM_WD * w)
    return delta, m, v


def _adamw_big(own, sib, w, m, v, layer, prev, name):
    L, A, Bc = w.shape
    T = _pick(A, max(8, (1 << 20) // (4 * Bc)), 16)

    def body(o_ref, s_ref, w_ref, m_ref, v_ref, *rest):
        g_ref, d_ref, nm_ref, nv_ref = rest[-4:]
        so = o_ref[0].astype(F32)
        ss = s_ref[0].astype(F32)
        for k in range(1, N_CHIPS):
            so = so + o_ref[k].astype(F32)
            ss = ss + s_ref[k].astype(F32)
        g = so + ss
        delta, nm, nv = _adam_math(g, w_ref[...], m_ref[...], v_ref[...])
        g_ref[...] = g
        d_ref[...] = delta
        nm_ref[...] = nm
        nv_ref[...] = nv

    part = pl.BlockSpec((N_CHIPS, T, Bc), lambda i: (0, i, 0))
    blk = pl.BlockSpec((None, T, Bc), lambda i: (layer, i, 0))
    shp = jax.ShapeDtypeStruct(w.shape, F32)
    in_specs, args, kwargs = [part, part, blk, blk, blk], [own, sib, w, m, v], {}
    if prev is not None:
        in_specs += [pl.BlockSpec(memory_space=pl.ANY)] * 4
        args += list(prev)
        kwargs["input_output_aliases"] = {5 + q: q for q in range(4)}
    return pl.pallas_call(body, grid=(A // T,), in_specs=in_specs, out_specs=[blk] * 4, out_shape=[shp] * 4,
                          compiler_params=_cp("parallel"), name=name, **kwargs)(*args)


def _sum_devices(parts, name):
    _, R, C = parts.shape

    def body(p_ref, o_ref):
        acc = p_ref[0]
        for k in range(1, N_DEV):
            acc = acc + p_ref[k]
        o_ref[...] = acc

    return pl.pallas_call(body, out_shape=jax.ShapeDtypeStruct((R, C), F32), name=name)(parts)


def _adamw_small(g, w, m, v, name):
    def body(g_ref, w_ref, m_ref, v_ref, d_ref, nm_ref, nv_ref):
        delta, nm, nv = _adam_math(g_ref[...], w_ref[...], m_ref[...], v_ref[...])
        d_ref[...] = delta
        nm_ref[...] = nm
        nv_ref[...] = nv

    shp = jax.ShapeDtypeStruct(g.shape, F32)
    return pl.pallas_call(body, out_shape=[shp] * 3, name=name)(g, w, m, v)


PACK_COLS = 1024


def _pack(arrs):
    flat = jnp.concatenate([a.reshape(-1).astype(F32) for a in arrs])
    n = flat.shape[0]
    rows = -(-n // (8 * PACK_COLS)) * 8
    return jnp.pad(flat, (0, rows * PACK_COLS - n)).reshape(rows, PACK_COLS)


def _unpack(packed, shapes):
    flat = packed.reshape(-1)
    out, off = [], 0
    for s in shapes:
        n = math.prod(s)
        out.append(flat[off:off + n].reshape(s))
        off += n
    return out


def _shard_ref(ref, kind, k, n):
    if kind == "col":
        return ref.at[:, pl.ds(pl.multiple_of(k * n, 128), n)]
    if kind == "row":
        return ref.at[pl.ds(pl.multiple_of(k * n, 16), n), :]
    return ref.at[k]


def _chip_peers():
    x, y, c = lax.axis_index("x"), lax.axis_index("y"), lax.axis_index("c")
    return x, y, c, [(1 - x, y), (x, 1 - y), (1 - x, 1 - y)]


class _Exchange:
    def __init__(self, mode, items):
        self.mode, self.items = mode, items
        self.arrays = []
        for it in items:
            if not any(it[0] is a for a in self.arrays):
                self.arrays.append(it[0])
        self.src_idx = [next(i for i, a in enumerate(self.arrays) if a is it[0]) for it in items]
        self.out_shapes = [jax.ShapeDtypeStruct(it[-1], it[0].dtype) for it in items]
        n = len(items)
        if mode == "swap":
            self.scratch = [pltpu.SemaphoreType.DMA((n,)), pltpu.SemaphoreType.DMA((n,))]
        else:
            self.scratch = [pltpu.SemaphoreType.DMA((3 * n,)), pltpu.SemaphoreType.DMA((3 * n,)),
                            pltpu.SemaphoreType.DMA((n,))]

    def _copies(self, ins, outs, sems):
        if self.mode == "swap":
            send_sems, recv_sems = sems
            x, y, c = lax.axis_index("x"), lax.axis_index("y"), lax.axis_index("c")
            sent = [pltpu.make_async_remote_copy(
                src_ref=ins[self.src_idx[t]], dst_ref=outs[t], send_sem=send_sems.at[t], recv_sem=recv_sems.at[t],
                device_id=(x, y, 1 - c), device_id_type=MESH) for t in range(len(self.items))]
            return [], sent, sent
        send_sems, recv_sems, loc_sems = sems
        x, y, c, peers = _chip_peers()
        me = 2 * x + y
        local, sent, arriving = [], [], []
        for t, it in enumerate(self.items):
            src_arr = ins[self.src_idx[t]]
            if self.mode == "gather":
                _, layer, kind, n, _ = it
                src = src_arr if layer is None else src_arr.at[layer]
                src_for = lambda k: src
                dst_from = lambda k: _shard_ref(outs[t], kind, k, n)
            else:
                _, kind, n, _ = it
                src_for = lambda k: _shard_ref(src_arr, kind, k, n)
                dst_from = lambda k: outs[t].at[k]
            local.append(pltpu.make_async_copy(src_for(me), dst_from(me), loc_sems.at[t]))
            for j, (px, py) in enumerate(peers):
                pk = 2 * px + py
                args = dict(send_sem=send_sems.at[3 * t + j], recv_sem=recv_sems.at[3 * t + j],
                            device_id=(px, py, c), device_id_type=MESH)
                sent.append(pltpu.make_async_remote_copy(src_ref=src_for(pk), dst_ref=dst_from(me), **args))
                arriving.append(pltpu.make_async_remote_copy(src_ref=src_for(pk), dst_ref=dst_from(pk), **args))
        return local, sent, arriving

    def start(self, ins, outs, sems):
        local, sent, arriving = self._copies(ins, outs, sems)
        for cp in local + sent:
            cp.start()
        for cp in arriving:
            cp._used = True

    def wait(self, ins, outs, sems):
        local, sent, arriving = self._copies(ins, outs, sems)
        for cp in arriving:
            cp.wait_recv()
        for cp in sent:
            cp.wait_send()
        for cp in local:
            cp.wait()


class _Multi:
    def __init__(self, parts):
        self.parts = parts
        self.arrays = [a for p in parts for a in p.arrays]
        self.out_shapes = [s for p in parts for s in p.out_shapes]
        self.scratch = [s for p in parts for s in p.scratch]

    def _split(self, ins, outs, sems):
        i = o = s = 0
        for p in self.parts:
            ni, no, ns = len(p.arrays), len(p.out_shapes), len(p.scratch)
            yield p, ins[i:i + ni], outs[o:o + no], sems[s:s + ns]
            i, o, s = i + ni, o + no, s + ns

    def start(self, ins, outs, sems):
        for p, a, b, c in self._split(ins, outs, sems):
            p.start(a, b, c)

    def wait(self, ins, outs, sems):
        for p, a, b, c in self._split(ins, outs, sems):
            p.wait(a, b, c)


def _run_exchange(ex, name):
    nin, nout = len(ex.arrays), len(ex.out_shapes)

    def body(*refs):
        ins, outs, sems = refs[:nin], refs[nin:nin + nout], refs[nin + nout:]
        ex.start(ins, outs, sems)
        ex.wait(ins, outs, sems)

    anyspec = pl.BlockSpec(memory_space=pl.ANY)
    return pl.pallas_call(body, in_specs=[anyspec] * nin, out_specs=[anyspec] * nout, out_shape=ex.out_shapes,
                          scratch_shapes=ex.scratch, name=name)(*ex.arrays)


def _start_exchange(ex, name, after=None):
    nin, nout, nsem = len(ex.arrays), len(ex.out_shapes), len(ex.scratch)
    hbm = pl.BlockSpec(memory_space=pltpu.HBM)
    sem = pl.BlockSpec(memory_space=pltpu.SEMAPHORE)

    n_operands = nin + nout + (after is not None)

    def body(*refs):
        ins, lands = refs[:nin], refs[nin:nin + nout]
        sems = refs[n_operands:n_operands + nsem]
        ex.start(ins, lands, sems)
        refs[-1][...] = jnp.zeros_like(refs[-1])

    args = [pltpu.with_memory_space_constraint(a, pltpu.HBM) for a in ex.arrays]
    args += [pltpu.with_memory_space_constraint(lax.empty(s.shape, s.dtype), pltpu.HBM) for s in ex.out_shapes]
    thru = [pltpu.HBM(a.shape, a.dtype) for a in ex.arrays] + [pltpu.HBM(s.shape, s.dtype) for s in ex.out_shapes]
    extra = [] if after is None else [pl.BlockSpec(memory_space=pl.ANY)]
    args += [] if after is None else [after]
    return pl.pallas_call(
        body, name=name, in_specs=[hbm] * (nin + nout) + extra,
        out_shape=tuple(ex.scratch) + tuple(thru) + (jax.ShapeDtypeStruct((8, 128), F32),),
        out_specs=tuple([sem] * nsem + [hbm] * (nin + nout) + [pl.BlockSpec(memory_space=pltpu.VMEM)]),
        input_output_aliases={q: nsem + q for q in range(nin + nout)},
        compiler_params=pltpu.CompilerParams(has_side_effects=pltpu.SideEffectType.DATAFLOW_SIDE_EFFECTING))(*args)


def _finish_exchange(ex, handles, after, name):
    nin, nout, nsem = len(ex.arrays), len(ex.out_shapes), len(ex.scratch)
    hbm = pl.BlockSpec(memory_space=pltpu.HBM)
    sem = pl.BlockSpec(memory_space=pltpu.SEMAPHORE)
    sems, thru = handles[:nsem], handles[nsem:nsem + nin + nout]

    def body(*refs):
        ins, lands = refs[:nin], refs[nin:nin + nout]
        ex.wait(ins, lands, refs[nin + nout:nin + nout + nsem])

    outs = pl.pallas_call(
        body, name=name, in_specs=[hbm] * (nin + nout) + [sem] * nsem + [pl.BlockSpec(memory_space=pl.ANY)],
        out_shape=tuple(pltpu.HBM(t.shape, t.dtype) for t in thru), out_specs=tuple([hbm] * (nin + nout)),
        input_output_aliases={q: q for q in range(nin + nout)},
        compiler_params=pltpu.CompilerParams(has_side_effects=pltpu.SideEffectType.DATAFLOW_SIDE_EFFECTING))(
            *thru, *sems, after)
    return outs[nin:]


def _all_gather_devices(v, name):
    def body(v_ref, o_ref, send_sems, recv_sems, loc_sem):
        x, y, c = lax.axis_index("x"), lax.axis_index("y"), lax.axis_index("c")
        me = 4 * x + 2 * y + c
        lc = pltpu.make_async_copy(v_ref, o_ref.at[me], loc_sem)
        lc.start()
        rel = [(bx, by, bc) for bx in (0, 1) for by in (0, 1) for bc in (0, 1)][1:]
        copies = []
        for j, (bx, by, bc) in enumerate(rel):
            px, py, pc = x ^ bx, y ^ by, c ^ bc
            copies.append((pltpu.make_async_remote_copy(
                src_ref=v_ref, dst_ref=o_ref.at[me], send_sem=send_sems.at[j], recv_sem=recv_sems.at[j],
                device_id=(px, py, pc), device_id_type=MESH), 4 * px + 2 * py + pc))
        for cp, _ in copies:
            cp.start()
        for j, (cp, pid) in enumerate(copies):
            pltpu.make_async_remote_copy(
                src_ref=v_ref, dst_ref=o_ref.at[pid], send_sem=send_sems.at[j], recv_sem=recv_sems.at[j],
                device_id=(x, y, c), device_id_type=MESH).wait_recv()
        for cp, _ in copies:
            cp.wait_send()
        lc.wait()

    anyspec = pl.BlockSpec(memory_space=pl.ANY)
    return pl.pallas_call(
        body, in_specs=[anyspec], out_specs=anyspec,
        out_shape=jax.ShapeDtypeStruct((N_DEV,) + v.shape, v.dtype),
        scratch_shapes=[pltpu.SemaphoreType.DMA((N_DEV - 1,)), pltpu.SemaphoreType.DMA((N_DEV - 1,)),
                        pltpu.SemaphoreType.DMA(())],
        name=name)(v)


BIG = ("attn_w_qkv", "attn_w_o", "ssm_w_in", "ssm_w_out", "ffn_w_up", "ffn_w_down")
BIG_KIND = {"attn_w_qkv": "col", "attn_w_o": "row", "ssm_w_in": "lead", "ssm_w_out": "row",
            "ffn_w_up": "col", "ffn_w_down": "row"}
SMALL_SHARDED = {"ssm_conv_w": 2, "ssm_conv_b": 1, "ssm_norm_w": 1, "ffn_conv_w": 2}
SMALL = ("mix_norm_w", "ssm_conv_w", "ssm_conv_b", "ssm_dt_bias", "ssm_a_log", "ssm_d", "ssm_norm_w",
         "ffn_norm_w", "ffn_conv_w", "ffn_conv_b", "final_norm_w")
WEIGHTS = ("mix_norm_w", "attn_w_qkv", "attn_w_o", "ssm_w_in", "ssm_conv_w", "ssm_conv_b", "ssm_dt_bias",
           "ssm_a_log", "ssm_d", "ssm_norm_w", "ssm_w_out", "ffn_norm_w", "ffn_w_up", "ffn_conv_w",
           "ffn_conv_b", "ffn_w_down", "final_norm_w")


def _shard_extent(name, shape):
    _, a, b = shape
    return {"col": b, "row": a, "lead": 1}[BIG_KIND[name]]


def _gather_item(w16, name):
    a, b = w16.shape
    kind = BIG_KIND[name]
    full = {"col": (a, N_CHIPS * b), "row": (N_CHIPS * a, b), "lead": (N_CHIPS, a, b)}[kind]
    return (w16, None, kind, _shard_extent(name, (1, a, b)), full)


def _layer_weights(i):
    j = i // 2
    mixer = [("attn_w_qkv", j), ("attn_w_o", j)] if i % 2 == 0 else [("ssm_w_in", j), ("ssm_w_out", j)]
    return mixer + [("ffn_w_up", i), ("ffn_w_down", i)]


def kernel(x, mix_norm_w, attn_w_qkv, attn_w_o, ssm_w_in, ssm_conv_w, ssm_conv_b, ssm_dt_bias, ssm_a_log, ssm_d, ssm_norm_w, ssm_w_out, ffn_norm_w, ffn_w_up, ffn_conv_w, ffn_conv_b, ffn_w_down, final_norm_w, loss_target, m_mix_norm_w, m_attn_w_qkv, m_attn_w_o, m_ssm_w_in, m_ssm_conv_w, m_ssm_conv_b, m_ssm_dt_bias, m_ssm_a_log, m_ssm_d, m_ssm_norm_w, m_ssm_w_out, m_ffn_norm_w, m_ffn_w_up, m_ffn_conv_w, m_ffn_conv_b, m_ffn_w_down, m_final_norm_w, v_mix_norm_w, v_attn_w_qkv, v_attn_w_o, v_ssm_w_in, v_ssm_conv_w, v_ssm_conv_b, v_ssm_dt_bias, v_ssm_a_log, v_ssm_d, v_ssm_norm_w, v_ssm_w_out, v_ffn_norm_w, v_ffn_w_up, v_ffn_conv_w, v_ffn_conv_b, v_ffn_w_down, v_final_norm_w):
    W = dict(mix_norm_w=mix_norm_w, attn_w_qkv=attn_w_qkv, attn_w_o=attn_w_o, ssm_w_in=ssm_w_in,
             ssm_conv_w=ssm_conv_w, ssm_conv_b=ssm_conv_b, ssm_dt_bias=ssm_dt_bias, ssm_a_log=ssm_a_log,
             ssm_d=ssm_d, ssm_norm_w=ssm_norm_w, ssm_w_out=ssm_w_out, ffn_norm_w=ffn_norm_w, ffn_w_up=ffn_w_up,
             ffn_conv_w=ffn_conv_w, ffn_conv_b=ffn_conv_b, ffn_w_down=ffn_w_down, final_norm_w=final_norm_w)
    M = dict(mix_norm_w=m_mix_norm_w, attn_w_qkv=m_attn_w_qkv, attn_w_o=m_attn_w_o, ssm_w_in=m_ssm_w_in,
             ssm_conv_w=m_ssm_conv_w, ssm_conv_b=m_ssm_conv_b, ssm_dt_bias=m_ssm_dt_bias, ssm_a_log=m_ssm_a_log,
             ssm_d=m_ssm_d, ssm_norm_w=m_ssm_norm_w, ssm_w_out=m_ssm_w_out, ffn_norm_w=m_ffn_norm_w,
             ffn_w_up=m_ffn_w_up, ffn_conv_w=m_ffn_conv_w, ffn_conv_b=m_ffn_conv_b, ffn_w_down=m_ffn_w_down,
             final_norm_w=m_final_norm_w)
    V = dict(mix_norm_w=v_mix_norm_w, attn_w_qkv=v_attn_w_qkv, attn_w_o=v_attn_w_o, ssm_w_in=v_ssm_w_in,
             ssm_conv_w=v_ssm_conv_w, ssm_conv_b=v_ssm_conv_b, ssm_dt_bias=v_ssm_dt_bias, ssm_a_log=v_ssm_a_log,
             ssm_d=v_ssm_d, ssm_norm_w=v_ssm_norm_w, ssm_w_out=v_ssm_w_out, ffn_norm_w=v_ffn_norm_w,
             ffn_w_up=v_ffn_w_up, ffn_conv_w=v_ffn_conv_w, ffn_conv_b=v_ffn_conv_b, ffn_w_down=v_ffn_w_down,
             final_norm_w=v_final_norm_w)

    S, D = x.shape[1], x.shape[2]
    xs = x.reshape(S, D)
    tgt = loss_target.reshape(S, D)
    depth = mix_norm_w.shape[0]
    heads = attn_w_o.shape[1] * N_CHIPS // HEAD_DIM
    AW = heads * HEAD_DIM
    d_inner = ssm_w_out.shape[1] * N_CHIPS
    ssm_heads = d_inner // SSM_HEAD_DIM
    G, P, N = SSM_GROUPS, SSM_HEAD_DIM, SSM_STATE
    gw = d_inner // G
    conv_dim = d_inner + 2 * G * N
    in_w = d_inner + conv_dim + ssm_heads
    in_pad = -(-in_w // 128) * 128
    shard_in = ssm_w_in.shape[2]
    xi, yi = lax.axis_index("x"), lax.axis_index("y")
    chip = 2 * xi + yi

    full = {}

    def land(keys, outs):
        for (n, l), o in zip(keys, outs):
            if n == "ssm_w_in":
                o = jnp.pad(jnp.concatenate([o[k] for k in range(N_CHIPS)], axis=1), ((0, 0), (0, in_pad - in_w)))
            full[(n, l)] = o

    def gather_ex(keys, extra=()):
        return _Exchange("gather", [_gather_item(W[n][l].astype(BF16), n) for n, l in keys] + list(extra))

    def fwd_mm(a, b, name, resid=None):
        return _matmul(a, b, "nn", F32, name, resid=resid)

    sm_names = list(SMALL_SHARDED)
    packed = _pack([W[n] for n in sm_names])
    lw = [_layer_weights(i) for i in range(depth)]
    first = lw[0][:1]
    got = _run_exchange(gather_ex(first, [(packed, None, "lead", 1, (N_CHIPS,) + packed.shape)]), "gather_first")
    land(first, got[:-1])
    batches = [lw[0][1:]] + lw[1:]
    pending = []
    for q, keys in enumerate(batches):
        ex = gather_ex(keys)
        pending.append((keys, ex, _start_exchange(ex, f"gather_start_{q}", after=got[-1])))
    issued = sum(handles[-1][0, 0] for _, _, handles in pending)

    def arrive(q, after):
        keys, ex, handles = pending[q]
        land(keys, _finish_exchange(ex, handles, after, f"gather_wait_{q}"))
    per_chip = [_unpack(got[-1][k], [W[n].shape for n in sm_names]) for k in range(N_CHIPS)]
    for q, n in enumerate(sm_names):
        full[n] = jnp.concatenate([per_chip[k][q] for k in range(N_CHIPS)], axis=SMALL_SHARDED[n])
    tab = _perm_tokens(_rope_table(S))

    def rep_heads(p):
        return jnp.repeat(p, P).reshape(G, 1, gw)

    saved = []
    cur = xs
    for i in range(depth):
        j = i // 2
        if i > 0:
            arrive(i, cur)
        sv = {"x_in": cur}
        h = _rms_fwd(cur, mix_norm_w[i] + issued if i == 0 else mix_norm_w[i], f"mix_norm_fwd_{i}")
        sv["h"] = h
        if i % 2 == 0:
            h = _perm_tokens(h)
            sv["h"] = h
            qkv = _matmul(h, full[("attn_w_qkv", j)], "nn", F32, f"qkv_fwd_{i}", rope=(tab, AW))
            if i == 0:
                arrive(0, qkv)
            og = [_attn_fwd(qkv, g, heads, f"attn_fwd_{i}_{g}") for g in range(3)]
            o, lse = _attn_combine([a for a, _ in og], [b for _, b in og], f"attn_combine_{i}")
            mixed = _unperm_tokens(fwd_mm(o, full[("attn_w_o", j)], f"attn_out_fwd_{i}"))
            sv.update(qkv=qkv, o=o, lse=lse)
        else:
            zx = fwd_mm(h, full[("ssm_w_in", j)], f"ssm_in_fwd_{i}")
            conv = _ssm_conv_fwd(zx, full["ssm_conv_w"][j], full["ssm_conv_b"][j], d_inner, conv_dim,
                                 f"ssm_conv_fwd_{i}")
            prm = [rep_heads(p[j]) for p in (ssm_dt_bias, ssm_a_log, ssm_d)]
            dtr = jnp.repeat(zx[:, d_inner + conv_dim:in_w], P, axis=1)
            y, sprev = _ssd_fwd(conv, dtr, *prm, d_inner, f"ssd_fwd_{i}")
            gated = _gnorm_fwd(y, zx, full["ssm_norm_w"][j], f"ssm_norm_fwd_{i}")
            cur = fwd_mm(gated, full[("ssm_w_out", j)], f"ssm_out_fwd_{i}", resid=cur)
            sv.update(zx=zx, conv=conv, dtr=dtr, prm=prm, y=y, sprev=sprev, gated=gated)
            mixed = None
        if mixed is None:
            h2 = _rms_fwd(cur, ffn_norm_w[i], f"ffn_norm_fwd_{i}")
        else:
            cur, h2 = _rms_fwd(cur, ffn_norm_w[i], f"ffn_norm_fwd_{i}", add=mixed)
        sv["x_mid"] = cur
        up = fwd_mm(h2, full[("ffn_w_up", i)], f"ffn_up_fwd_{i}")
        act = _ffn_conv_fwd(up, full["ffn_conv_w"][i], ffn_conv_b[i], f"ffn_conv_fwd_{i}")
        cur = fwd_mm(act, full[("ffn_w_down", i)], f"ffn_down_fwd_{i}", resid=cur)
        sv.update(h2=h2, up=up, act=act)
        saved.append(sv)

    dx, d_final, loss_part = _loss_head(cur, final_norm_w, tgt, "loss_head")
    gbig, recv = {}, {}
    gs = {n: [None] * W[n].shape[0] for n in SMALL if n != "final_norm_w"}

    def scatter_ex(keys):
        return _Exchange("scatter", [(gbig[(n, l)], BIG_KIND[n], _shard_extent(n, W[n].shape),
                                      (N_CHIPS,) + W[n].shape[1:]) for n, l in keys])

    sib = {}

    def swap_ex(keys):
        return _Exchange("swap", [(recv[k], recv[k].shape) for k in keys])

    def bwd_mm(a, b, mode, dtype, name, send=(), swap=()):
        if not send and not swap:
            return _matmul(a, b, mode, dtype, name)
        parts = ([scatter_ex(send)] if send else []) + ([swap_ex(swap)] if swap else [])
        out, got = _matmul(a, b, mode, dtype, name, carry=_Multi(parts))
        recv.update(zip(send, got[:len(send)]))
        sib.update(zip(swap, got[len(send):]))
        return out

    sending = None
    to_swap = []
    for i in reversed(range(depth)):
        j = i // 2
        sv = saved[i]
        k_in, k_out, k_up, k_down = _layer_weights(i)
        own = i == 0

        def now(keys):
            return keys if own else []
        dact = _matmul(dx, full[k_down], "nt", F32, f"ffn_down_dgrad_{i}")
        gbig[k_down] = bwd_mm(sv["act"], dx, "tn", BF16, f"ffn_down_wgrad_{i}", swap=to_swap)
        to_swap = []
        dup, dcw, dcb = _ffn_conv_bwd(dact, sv["up"], full["ffn_conv_w"][i], ffn_conv_b[i], f"ffn_conv_bwd_{i}")
        gs["ffn_conv_w"][i], gs["ffn_conv_b"][i] = dcw, dcb[0]
        dh2 = bwd_mm(dup, full[k_up], "nt", F32, f"ffn_up_dgrad_{i}", send=now([k_down]))
        gbig[k_up] = bwd_mm(sv["h2"], dup, "tn", BF16, f"ffn_up_wgrad_{i}", swap=now([k_down]))
        dx, dnw = _rms_bwd(sv["x_mid"], ffn_norm_w[i], dh2, dx, f"ffn_norm_bwd_{i}")
        gs["ffn_norm_w"][i] = dnw[0]
        if i % 2 == 0:
            dxp = _perm_tokens(dx)
            do = _matmul(dxp, full[k_out], "nt", F32, f"attn_out_dgrad_{i}")
            gbig[k_out] = _matmul(sv["o"], dxp, "tn", BF16, f"attn_out_wgrad_{i}")
            dqkv = None
            for g in range(3):
                dqkv = _attn_bwd(sv["qkv"], tab, sv["o"], sv["lse"], do, dqkv, g, heads, f"attn_bwd_{i}_{g}")
            gbig[k_in] = bwd_mm(sv["h"], dqkv, "tn", BF16, f"qkv_wgrad_{i}", send=now([k_up, k_out]))
            dh = _unperm_tokens(bwd_mm(dqkv, full[k_in], "nt", F32, f"qkv_dgrad_{i}", send=now([k_in]),
                                       swap=now([k_up, k_out])))
        else:
            dgated = _matmul(dx, full[k_out], "nt", F32, f"ssm_out_dgrad_{i}")
            gbig[k_out] = _matmul(sv["gated"], dx, "tn", BF16, f"ssm_out_wgrad_{i}")
            dy, dz, dgw = _gnorm_bwd(sv["y"], sv["zx"], full["ssm_norm_w"][j], dgated, f"ssm_norm_bwd_{i}")
            gs["ssm_norm_w"][j] = dgw[0]
            dxs_, dbm, dcm, ddtr, dbias, dalog, ddsk = _ssd_bwd(
                sv["conv"], sv["dtr"], *sv["prm"], sv["sprev"], dy, d_inner, f"ssd_bwd_{i}")
            gs["ssm_dt_bias"][j] = dbias.reshape(-1)[::P]
            gs["ssm_a_log"][j] = dalog.reshape(-1)[::P]
            gs["ssm_d"][j] = ddsk.reshape(ssm_heads, P).sum(axis=1)
            ddt = jnp.pad(ddtr[:, ::P], ((0, 0), (0, in_pad - in_w)))
            dzx, dcw, dcb = _ssm_conv_bwd(dxs_, dbm, dcm, sv["conv"], sv["zx"], dz, ddt, full["ssm_conv_w"][j],
                                          d_inner, f"ssm_conv_bwd_{i}")
            gs["ssm_conv_w"][j], gs["ssm_conv_b"][j] = dcw, dcb[0]
            dwin = bwd_mm(sv["h"], dzx, "tn", BF16, f"ssm_in_wgrad_{i}", send=now([k_up, k_out]))
            gbig[k_in] = jnp.stack([dwin[:, k * shard_in:(k + 1) * shard_in] for k in range(N_CHIPS)])
            dh = bwd_mm(dzx, full[k_in], "nt", F32, f"ssm_in_dgrad_{i}", send=now([k_in]), swap=now([k_up, k_out]))
        if sending is not None:
            keys, ex, handles = sending
            recv.update(zip(keys, _finish_exchange(ex, handles, dh, f"scatter_wait_{i + 1}")))
            to_swap, sending = keys, None
        if own:
            to_swap = to_swap + [k_in]
        else:
            keys = [k_in, k_out, k_up, k_down]
            ex = scatter_ex(keys)
            sending = (keys, ex, _start_exchange(ex, f"scatter_start_{i}"))
        issued = 0.0 if own else sending[2][-1][0, 0]
        dx, dnw = _rms_bwd(sv["x_in"], mix_norm_w[i] + issued, dh, dx, f"mix_norm_bwd_{i}")
        gs["mix_norm_w"][i] = dnw[0]
    grad_x = dx.reshape(x.shape)

    sib.update(zip(to_swap, _run_exchange(swap_ex(to_swap), "swap_last")))

    small_full = [jnp.stack(gs[n]) if n != "final_norm_w" else d_final[0] for n in SMALL]
    small_full.append(loss_part[0, 0:1])
    small_shapes = [a.shape for a in small_full]
    summed = _sum_devices(_all_gather_devices(_pack(small_full), "gather_small_grads"), "sum_small_grads")
    small_g = _unpack(summed, small_shapes)
    loss = small_g[-1][0]
    gsm = {}
    for n, g in zip(SMALL, small_g[:-1]):
        if n in SMALL_SHARDED:
            ax = SMALL_SHARDED[n]
            ext = W[n].shape[ax]
            g = lax.dynamic_slice_in_dim(g, chip * ext, ext, axis=ax)
        gsm[n] = g

    out_g, out_d, out_m, out_v = {}, {}, {}, {}
    for name in BIG:
        outs = None
        for l in range(W[name].shape[0]):
            outs = _adamw_big(recv[(name, l)], sib[(name, l)], W[name], M[name], V[name], l, outs,
                              f"adamw_{name}_{l}")
        out_g[name], out_d[name], out_m[name], out_v[name] = outs
    shapes = [W[n].shape for n in SMALL]
    pd, pm, pv = _adamw_small(_pack([gsm[n] for n in SMALL]), _pack([W[n] for n in SMALL]),
                              _pack([M[n] for n in SMALL]), _pack([V[n] for n in SMALL]), "adamw_small")
    for n, d_, m_, v_ in zip(SMALL, _unpack(pd, shapes), _unpack(pm, shapes), _unpack(pv, shapes)):
        out_g[n], out_d[n], out_m[n], out_v[n] = gsm[n], d_, m_, v_

    return (loss, grad_x, *[out_g[n] for n in WEIGHTS], *[out_d[n] for n in WEIGHTS],
            *[out_m[n] for n in WEIGHTS], *[out_v[n] for n in WEIGHTS])
```

```python
import math

import jax
import jax.numpy as jnp
from jax import lax
from jax.experimental import pallas as pl
from jax.experimental.pallas import tpu as pltpu

F32 = jnp.float32
BF16 = jnp.bfloat16
MESH = pl.DeviceIdType.MESH

NORM_EPS = 1e-5
HEAD_DIM = 128
ATTN_BLOCK = 128
ATTN_DILATIONS = (1, 4, 16)
ATTN_WINDOWS = (128, 512, 2048)
PERM = 16
ROPE_THETA = 500000.0
ROPE_HALF = HEAD_DIM // 8
SSM_HEAD_DIM = 64
SSM_STATE = 128
SSM_GROUPS = 8
SSM_CHUNK = 128
NEG = -1e30

ADAM_LR = 0.001
ADAM_B1 = 0.9
ADAM_B2 = 0.999
ADAM_EPS = 1e-08
ADAM_WD = 0.01
ADAM_STEP = 10

VMEM_LIMIT_BYTES = 48 * 1024 * 1024
N_CHIPS = 4
N_DEV = 8


def _cp(*sem):
    return pltpu.CompilerParams(dimension_semantics=sem, vmem_limit_bytes=VMEM_LIMIT_BYTES)


def _pick(n, pref, mult=128):
    best = None
    t = mult
    while t <= min(n, pref):
        if n % t == 0:
            best = t
        t += mult
    return n if best is None else best


def _sigmoid(x):
    return 1.0 / (1.0 + jnp.exp(-x))


def _silu(x):
    return x * _sigmoid(x)


def _softplus(x):
    u = jnp.exp(-jnp.abs(x))
    w = 1.0 + u
    log1p = jnp.where(w == 1.0, u, jnp.log(w) * (u / jnp.where(w == 1.0, 1.0, w - 1.0)))
    return jnp.maximum(x, 0.0) + log1p


def _dot(a, b, dims):
    return lax.dot_general(a.astype(BF16), b.astype(BF16), (dims, ((), ())),
                           preferred_element_type=F32)


def _dot_nn(a, b):
    return _dot(a, b, ((1,), (0,)))


def _dot_nt(a, b):
    return _dot(a, b, ((1,), (1,)))


def _dot_tn(a, b):
    return _dot(a, b, ((0,), (0,)))


MATMUL_VMEM_BYTES = 36 * 1024 * 1024
MATMUL_TILES = (2048, 1536, 1408, 1152, 1024, 896, 768, 640, 512, 384, 256, 128)


def _matmul_tiles(M, N, K, a_bytes, b_bytes, o_bytes, has_resid, fix_tn=None):
    best = None
    for tm in [t for t in MATMUL_TILES if M % t == 0] or [M]:
        for tn in [fix_tn] if fix_tn else [t for t in MATMUL_TILES if N % t == 0] or [N]:
            for tk in [t for t in MATMUL_TILES if K % t == 0 and t <= 1408] or [K]:
                nk, gm, gn = K // tk, M // tm, N // tn
                vmem = 2 * (tm * tk * a_bytes + tk * tn * b_bytes + tm * tn * o_bytes)
                vmem += (2 * tm * tn * 4 if has_resid else 0) + (tm * tn * 4 if nk > 1 else 0)
                if vmem > MATMUL_VMEM_BYTES:
                    continue
                a_reads = 1 if nk == 1 else gn
                b_reads = 1 if (nk == 1 and gn == 1) else gm
                traffic = M * K * a_bytes * a_reads + K * N * b_bytes * b_reads + M * N * o_bytes
                key = (traffic, gm * gn * nk)
                if best is None or key < best[0]:
                    best = (key, (tm, tn, tk))
    assert best is not None, (M, N, K)
    return best[1]


def _matmul(a, b, mode, out_dtype, name, resid=None, carry=None, rope=None):
    if mode == "nn":
        (M, K), (K2, N) = a.shape, b.shape
    elif mode == "nt":
        (M, K), (N, K2) = a.shape, b.shape
    else:
        (K, M), (K2, N) = a.shape, b.shape
    assert K == K2, (a.shape, b.shape, mode)
    tm, tn, tk = _matmul_tiles(M, N, K, a.dtype.itemsize, b.dtype.itemsize, jnp.dtype(out_dtype).itemsize,
                               resid is not None or rope is not None, fix_tn=rope[1] if rope else None)
    nk = K // tk
    assert not (rope and (resid is not None or nk > 1))
    gm, gn = M // tm, N // tn
    dims = {"nn": ((1,), (0,)), "nt": ((1,), (1,)), "tn": ((0,), (0,))}[mode]
    has_resid = resid is not None or rope is not None
    nci = len(carry.arrays) if carry else 0
    nco = len(carry.out_shapes) if carry else 0

    def body(a_ref, b_ref, *rest):
        r_ref = rest[0] if has_resid else None
        rest = rest[has_resid:]
        c_ins, o_ref, c_outs, scratch = rest[:nci], rest[nci], rest[nci + 1:nci + 1 + nco], rest[nci + 1 + nco:]
        acc_ref = scratch[0] if nk > 1 else None
        sems = scratch[nk > 1:]
        i, j, k = pl.program_id(0), pl.program_id(1), pl.program_id(2)
        if carry:
            @pl.when((i == 0) & (j == 0) & (k == 0))
            def _():
                carry.start(c_ins, c_outs, sems)

        if nk == 1:
            r = _dot(a_ref[...], b_ref[...], dims)
            if rope:
                @pl.when(j % 3 < 2)
                def _():
                    table = r_ref[...]
                    for h in range(tn // HEAD_DIM):
                        hs = slice(h * HEAD_DIM, (h + 1) * HEAD_DIM)
                        o_ref[:, hs] = _rope(r[:, hs], table, 1.0).astype(o_ref.dtype)

                @pl.when(j % 3 == 2)
                def _():
                    o_ref[...] = r.astype(o_ref.dtype)
            else:
                if has_resid:
                    r = r + r_ref[...]
                o_ref[...] = r.astype(o_ref.dtype)
        else:
            @pl.when(k == 0)
            def _():
                acc_ref[...] = jnp.zeros_like(acc_ref)

            acc_ref[...] += _dot(a_ref[...], b_ref[...], dims)

            @pl.when(k == nk - 1)
            def _():
                r = acc_ref[...]
                if has_resid:
                    r = r + r_ref[...]
                o_ref[...] = r.astype(o_ref.dtype)

        if carry:
            @pl.when((i == gm - 1) & (j == gn - 1) & (k == nk - 1))
            def _():
                carry.wait(c_ins, c_outs, sems)

    if mode == "nn":
        a_spec = pl.BlockSpec((tm, tk), lambda i, j, k: (i, k))
        b_spec = pl.BlockSpec((tk, tn), lambda i, j, k: (k, j))
    elif mode == "nt":
        a_spec = pl.BlockSpec((tm, tk), lambda i, j, k: (i, k))
        b_spec = pl.BlockSpec((tn, tk), lambda i, j, k: (j, k))
    else:
        a_spec = pl.BlockSpec((tk, tm), lambda i, j, k: (k, i))
        b_spec = pl.BlockSpec((tk, tn), lambda i, j, k: (k, j))
    o_spec = pl.BlockSpec((tm, tn), lambda i, j, k: (i, j))
    anyspec = pl.BlockSpec(memory_space=pl.ANY)
    extra_spec = pl.BlockSpec((tm, 3 * HEAD_DIM), lambda i, j, k: (i, 0)) if rope else o_spec
    in_specs = [a_spec, b_spec] + ([extra_spec] if has_resid else []) + [anyspec] * nci
    extra = (rope[0],) if rope else ((resid,) if has_resid else ())
    args = (a, b) + extra + (tuple(carry.arrays) if carry else ())
    out_shape = [jax.ShapeDtypeStruct((M, N), out_dtype)] + (carry.out_shapes if carry else [])
    scratch = ([] if nk == 1 else [pltpu.VMEM((tm, tn), F32)]) + (carry.scratch if carry else [])
    sem = ("arbitrary",) * 3 if carry else ("parallel", "parallel", "arbitrary")
    outs = pl.pallas_call(
        body, grid=(gm, gn, nk), in_specs=in_specs, out_specs=[o_spec] + [anyspec] * nco,
        out_shape=out_shape, scratch_shapes=scratch, compiler_params=_cp(*sem), name=name)(*args)
    return (outs[0], outs[1:]) if carry else outs[0]


def _rms_fwd(x, w, name, add=None):
    S, D = x.shape
    t = _pick(S, 512, 8)
    has_add = add is not None

    def body(x_ref, w_ref, *rest):
        xv = x_ref[...]
        if has_add:
            xv = xv + rest[0][...]
            rest[1][...] = xv
        r = lax.rsqrt(jnp.mean(xv * xv, axis=-1, keepdims=True) + NORM_EPS)
        rest[-1][...] = (xv * r * w_ref[...]).astype(rest[-1].dtype)

    row = pl.BlockSpec((t, D), lambda i: (i, 0))
    vec = pl.BlockSpec((1, D), lambda i: (0, 0))
    normed = jax.ShapeDtypeStruct((S, D), BF16)
    if not has_add:
        return pl.pallas_call(body, grid=(S // t,), in_specs=[row, vec], out_specs=row, out_shape=normed,
                              compiler_params=_cp("parallel"), name=name)(x, w.reshape(1, D))
    return pl.pallas_call(body, grid=(S // t,), in_specs=[row, vec, row], out_specs=[row, row],
                          out_shape=[jax.ShapeDtypeStruct((S, D), F32), normed],
                          compiler_params=_cp("parallel"), name=name)(x, w.reshape(1, D), add)


def _rms_bwd(x, w, dh, dres, name):
    S, D = x.shape
    t = _pick(S, 512, 8)

    def body(x_ref, w_ref, dh_ref, dr_ref, dx_ref, dw_ref):
        @pl.when(pl.program_id(0) == 0)
        def _():
            dw_ref[...] = jnp.zeros_like(dw_ref)

        xv = x_ref[...]
        r = lax.rsqrt(jnp.mean(xv * xv, axis=-1, keepdims=True) + NORM_EPS)
        xh = xv * r
        dh_v = dh_ref[...]
        g = dh_v * w_ref[...]
        dx_ref[...] = dr_ref[...] + r * (g - xh * jnp.mean(g * xh, axis=-1, keepdims=True))
        dw_ref[...] += jnp.sum(dh_v * xh, axis=0, keepdims=True)

    row = pl.BlockSpec((t, D), lambda i: (i, 0))
    vec = pl.BlockSpec((1, D), lambda i: (0, 0))
    return pl.pallas_call(
        body, grid=(S // t,), in_specs=[row, vec, row, row], out_specs=[row, vec],
        out_shape=[jax.ShapeDtypeStruct((S, D), F32), jax.ShapeDtypeStruct((1, D), F32)],
        compiler_params=_cp("arbitrary"), name=name)(x, w.reshape(1, D), dh, dres)


def _loss_head(x, w, tgt, name):
    S, D = x.shape
    t = _pick(S, 512, 8)

    def body(x_ref, w_ref, t_ref, dx_ref, dw_ref, l_ref):
        @pl.when(pl.program_id(0) == 0)
        def _():
            dw_ref[...] = jnp.zeros_like(dw_ref)
            l_ref[...] = jnp.zeros_like(l_ref)

        xv = x_ref[...]
        wv = w_ref[...]
        r = lax.rsqrt(jnp.mean(xv * xv, axis=-1, keepdims=True) + NORM_EPS)
        xh = xv * r
        err = xh * wv - t_ref[...]
        per_tok = jnp.mean(err * err, axis=-1, keepdims=True)
        l_ref[...] += 0.5 * jnp.sum(per_tok, axis=0, keepdims=True)
        dy = err * (1.0 / D)
        g = dy * wv
        dx_ref[...] = r * (g - xh * jnp.mean(g * xh, axis=-1, keepdims=True))
        dw_ref[...] += jnp.sum(dy * xh, axis=0, keepdims=True)

    row = pl.BlockSpec((t, D), lambda i: (i, 0))
    vec = pl.BlockSpec((1, D), lambda i: (0, 0))
    lspec = pl.BlockSpec((1, 128), lambda i: (0, 0))
    return pl.pallas_call(
        body, grid=(S // t,), in_specs=[row, vec, row], out_specs=[row, vec, lspec],
        out_shape=[jax.ShapeDtypeStruct((S, D), F32), jax.ShapeDtypeStruct((1, D), F32),
                   jax.ShapeDtypeStruct((1, 128), F32)],
        compiler_params=_cp("arbitrary"), name=name)(x, w.reshape(1, D), tgt)


def _rope_table(seq):
    pos = jnp.arange(seq, dtype=F32)
    inv_freq = ROPE_THETA ** (-jnp.arange(0, 2 * ROPE_HALF, 2, dtype=F32) / (2 * ROPE_HALF))
    ang = pos[:, None] * inv_freq[None, :]
    cos, sin = jnp.cos(ang), jnp.sin(ang)
    pad = HEAD_DIM - 2 * ROPE_HALF
    cos_p = jnp.concatenate([cos, cos, jnp.ones((seq, pad), F32)], axis=1)
    sin_a = jnp.concatenate([-sin, jnp.zeros((seq, HEAD_DIM - ROPE_HALF), F32)], axis=1)
    sin_b = jnp.concatenate([jnp.zeros((seq, ROPE_HALF), F32), sin, jnp.zeros((seq, pad), F32)], axis=1)
    return jnp.concatenate([cos_p, sin_a, sin_b], axis=1)


def _rope(t, tab, sign):
    cos_p = tab[:, 0:HEAD_DIM]
    sin_a = tab[:, HEAD_DIM:2 * HEAD_DIM]
    sin_b = tab[:, 2 * HEAD_DIM:3 * HEAD_DIM]
    up = pltpu.roll(t, HEAD_DIM - ROPE_HALF, 1)
    down = pltpu.roll(t, ROPE_HALF, 1)
    return t * cos_p + sign * (up * sin_a + down * sin_b)


def _perm_tokens(a):
    S = a.shape[0]
    return a.reshape(S // PERM, PERM, -1).transpose(1, 0, 2).reshape(S, -1)


def _unperm_tokens(a):
    S = a.shape[0]
    return a.reshape(PERM, S // PERM, -1).transpose(1, 0, 2).reshape(S, -1)


class _Strided:
    def __init__(self, S, dil):
        self.dil, self.m = dil, PERM // dil
        self.c = ATTN_BLOCK // self.m
        self.rows = S // PERM
        self.nb = S // (dil * ATTN_BLOCK)

    def view(self, a):
        return a.reshape(self.m, self.dil, self.rows, a.shape[-1])

    def spec(self, width, col, f=lambda n: n):
        return pl.BlockSpec((self.m, None, self.c, width), lambda r, n: (0, r, f(n), col))

    def load(self, ref, sl=slice(None)):
        if self.m == 1:
            return ref[0, :, sl]
        return jnp.concatenate([ref[q, :, sl] for q in range(self.m)], axis=0)

    def store(self, ref, sl, val):
        for q in range(self.m):
            ref[q, :, sl] = val[q * self.c:(q + 1) * self.c, :]

    def member(self, i):
        shift = self.c.bit_length() - 1
        return (i & (self.c - 1)) * self.m + (i >> shift)


def _attn_fwd(qkv, g, heads, name):
    S = qkv.shape[0]
    W = heads * HEAD_DIM
    dil = ATTN_DILATIONS[g]
    steps = ATTN_WINDOWS[g] // dil
    B = ATTN_BLOCK
    scale = HEAD_DIM ** -0.5
    st = _Strided(S, dil)

    def body(q_ref, k_ref, v_ref, o_ref, l_ref, kp_scr, vp_scr):
        n = pl.program_id(1)

        @pl.when(n == 0)
        def _():
            kp_scr[...] = jnp.zeros_like(kp_scr)
            vp_scr[...] = jnp.zeros_like(vp_scr)

        ii = lax.broadcasted_iota(jnp.int32, (B, 2 * B), 0)
        jj = lax.broadcasted_iota(jnp.int32, (B, 2 * B), 1)
        delta = st.member(ii) - st.member(jj & (B - 1)) + jnp.where(jj >= B, 0, B)
        ok = (delta >= 0) & (delta <= steps) & ((jj >= B) | (n > 0))
        sls = [slice(h * HEAD_DIM, (h + 1) * HEAD_DIM) for h in range(heads)]
        qs = [st.load(q_ref, sl) for sl in sls]
        ks = [st.load(k_ref, sl) for sl in sls]
        vs = [st.load(v_ref, sl) for sl in sls]
        kcs = [jnp.concatenate([kp_scr[:, sl], k], axis=0) for sl, k in zip(sls, ks)]
        ss = [jnp.where(ok, _dot_nt(q, kc) * scale, NEG) for q, kc in zip(qs, kcs)]
        ms = [jnp.max(s, axis=-1, keepdims=True) for s in ss]
        ps = [jnp.exp(s - m) for s, m in zip(ss, ms)]
        dens = [jnp.sum(p, axis=-1, keepdims=True) for p in ps]
        vcs = [jnp.concatenate([vp_scr[:, sl], v], axis=0) for sl, v in zip(sls, vs)]
        outs = [_dot_nn(p, vc) for p, vc in zip(ps, vcs)]
        for sl, o, m, den, k, v in zip(sls, outs, ms, dens, ks, vs):
            st.store(o_ref, sl, o / den)
            st.store(l_ref, sl, jnp.broadcast_to(m + jnp.log(den), (B, HEAD_DIM)))
            kp_scr[:, sl] = k
            vp_scr[:, sl] = v

    qv = st.view(qkv)
    o_spec = st.spec(W, 0)
    o, lse = pl.pallas_call(
        body, grid=(dil, st.nb),
        in_specs=[st.spec(W, g * 3), st.spec(W, g * 3 + 1), st.spec(W, g * 3 + 2)],
        out_specs=[o_spec, o_spec],
        out_shape=[jax.ShapeDtypeStruct((st.m, dil, st.rows, W), F32)] * 2,
        scratch_shapes=[pltpu.VMEM((B, W), F32), pltpu.VMEM((B, W), F32)],
        compiler_params=_cp("parallel", "arbitrary"), name=name)(qv, qv, qv)
    return o.reshape(S, W), lse.reshape(S, W)


def _attn_combine(os_, ls_, name):
    S, W = os_[0].shape
    t = _pick(S, 256, 8)

    def body(o0, o1, o2, l0, l1, l2, o_ref, l_ref):
        a, b, c = l0[...], l1[...], l2[...]
        m = jnp.maximum(jnp.maximum(a, b), c)
        ea, eb, ec = jnp.exp(a - m), jnp.exp(b - m), jnp.exp(c - m)
        tot = ea + eb + ec
        o_ref[...] = (ea * o0[...] + eb * o1[...] + ec * o2[...]) / tot
        l_ref[...] = m + jnp.log(tot)

    row = pl.BlockSpec((t, W), lambda i: (i, 0))
    return pl.pallas_call(body, grid=(S // t,), in_specs=[row] * 6, out_specs=[row, row],
                          out_shape=[jax.ShapeDtypeStruct((S, W), F32)] * 2,
                          compiler_params=_cp("parallel"), name=name)(*os_, *ls_)


def _attn_bwd(qkv, tab, o, lse, do, dqkv_prev, g, heads, name):
    S = qkv.shape[0]
    W = heads * HEAD_DIM
    dil = ATTN_DILATIONS[g]
    steps = ATTN_WINDOWS[g] // dil
    B = ATTN_BLOCK
    scale = HEAD_DIM ** -0.5
    st = _Strided(S, dil)
    nb = st.nb
    aliased = dqkv_prev is not None

    def body(q_ref, qn_ref, k_ref, v_ref, do_ref, don_ref, o_ref, on_ref, l_ref, ln_ref, t_ref, *rest):
        out_ref, kp_scr, vp_scr = rest[-3:]
        n = pl.program_id(1)
        has_next = n < nb - 1

        @pl.when(n == 0)
        def _():
            kp_scr[...] = jnp.zeros_like(kp_scr)
            vp_scr[...] = jnp.zeros_like(vp_scr)

        ia = lax.broadcasted_iota(jnp.int32, (B, 2 * B), 0)
        ja = lax.broadcasted_iota(jnp.int32, (B, 2 * B), 1)
        da = st.member(ia) - st.member(ja & (B - 1)) + jnp.where(ja >= B, 0, B)
        ok_a = (da >= 0) & (da <= steps) & ((ja >= B) | (n > 0))
        ib = lax.broadcasted_iota(jnp.int32, (2 * B, B), 0)
        jb = lax.broadcasted_iota(jnp.int32, (2 * B, B), 1)
        db = st.member(ib & (B - 1)) + jnp.where(ib >= B, B, 0) - st.member(jb)
        ok_b = (db >= 0) & (db <= steps) & ((ib < B) | has_next)
        tb = st.load(t_ref)
        hs = range(heads)
        sls = [slice(h * HEAD_DIM, (h + 1) * HEAD_DIM) for h in hs]
        qr = [st.load(q_ref, sl) for sl in sls]
        qnr = [st.load(qn_ref, sl) for sl in sls]
        kr = [st.load(k_ref, sl) for sl in sls]
        kpr = [kp_scr[:, sl] for sl in sls]
        v = [st.load(v_ref, sl) for sl in sls]
        dov_ = [st.load(do_ref, sl) for sl in sls]
        donv = [st.load(don_ref, sl) for sl in sls]
        dl = [jnp.sum(dov_[h] * st.load(o_ref, sls[h]), axis=-1, keepdims=True) for h in hs]
        dln = [jnp.sum(donv[h] * st.load(on_ref, sls[h]), axis=-1, keepdims=True) for h in hs]
        ls = [st.load(l_ref, sl) for sl in sls]
        kc = [jnp.concatenate([kpr[h], kr[h]], axis=0) for h in hs]
        vc = [jnp.concatenate([vp_scr[:, sls[h]], v[h]], axis=0) for h in hs]
        qc = [jnp.concatenate([qr[h], qnr[h]], axis=0) for h in hs]
        doc = [jnp.concatenate([dov_[h], donv[h]], axis=0) for h in hs]
        lc = [jnp.concatenate([ls[h], st.load(ln_ref, sls[h])], axis=0) for h in hs]
        dlc = [jnp.concatenate([dl[h], dln[h]], axis=0) for h in hs]
        s = [_dot_nt(qr[h], kc[h]) * scale for h in hs]
        dp = [_dot_nt(dov_[h], vc[h]) for h in hs]
        s2 = [_dot_nt(qc[h], kr[h]) * scale for h in hs]
        dp2 = [_dot_nt(doc[h], v[h]) for h in hs]
        p = [jnp.where(ok_a, jnp.exp(jnp.minimum(s[h] - jnp.concatenate([ls[h], ls[h]], axis=1), 30.0)), 0.0)
             for h in hs]
        ds = [p[h] * (dp[h] - dl[h]) * scale for h in hs]
        p2 = [jnp.where(ok_b, jnp.exp(jnp.minimum(s2[h] - lc[h], 30.0)), 0.0) for h in hs]
        ds2 = [p2[h] * (dp2[h] - dlc[h]) * scale for h in hs]
        dq = [_dot_nn(ds[h], kc[h]) for h in hs]
        dk = [_dot_tn(ds2[h], qc[h]) for h in hs]
        dv = [_dot_tn(p2[h], doc[h]) for h in hs]
        for h in hs:
            st.store(out_ref, sls[h], _rope(dq[h], tb, -1.0))
            st.store(out_ref, slice(W + h * HEAD_DIM, W + (h + 1) * HEAD_DIM), _rope(dk[h], tb, -1.0))
            st.store(out_ref, slice(2 * W + h * HEAD_DIM, 2 * W + (h + 1) * HEAD_DIM), dv[h])
            kp_scr[:, sls[h]] = kr[h]
            vp_scr[:, sls[h]] = v[h]

    nxt = lambda n: jnp.minimum(n + 1, nb - 1)
    same = lambda n: n
    q0, q1, q2, tw = g * 3, g * 3 + 1, g * 3 + 2, 3 * HEAD_DIM
    in_specs = [st.spec(W, q0), st.spec(W, q0, nxt), st.spec(W, q1), st.spec(W, q2)]
    in_specs += [st.spec(W, 0, f) for f in (same, nxt, same, nxt, same, nxt)]
    in_specs += [st.spec(tw, 0)]
    qv, tv, ov, lv, dov = (st.view(a) for a in (qkv, tab, o, lse, do))
    args = [qv, qv, qv, qv, dov, dov, ov, ov, lv, lv, tv]
    kwargs = {}
    if aliased:
        in_specs.append(pl.BlockSpec(memory_space=pl.ANY))
        args.append(st.view(dqkv_prev))
        kwargs["input_output_aliases"] = {len(args) - 1: 0}
    out = pl.pallas_call(
        body, grid=(dil, nb), in_specs=in_specs, out_specs=st.spec(3 * W, g),
        out_shape=jax.ShapeDtypeStruct((st.m, dil, st.rows, 9 * W), F32),
        scratch_shapes=[pltpu.VMEM((B, W), F32), pltpu.VMEM((B, W), F32)],
        compiler_params=_cp("parallel", "arbitrary"), name=name, **kwargs)(*args)
    return out.reshape(S, 9 * W)


def _shift_down(x, halo, s):
    if s == 0:
        return x
    T = x.shape[0]
    xs = pltpu.roll(x, s, 0)
    hs = pltpu.roll(halo, s, 0)
    row8 = lax.broadcasted_iota(jnp.int32, hs.shape, 0)
    top = jnp.where(row8 < s, hs, xs[0:8])
    return top if T == 8 else jnp.concatenate([top, xs[8:T]], axis=0)


def _shift_up(x, halo, s):
    if s == 0:
        return x
    T = x.shape[0]
    xs = pltpu.roll(x, T - s, 0)
    hs = pltpu.roll(halo, 8 - s, 0)
    row8 = lax.broadcasted_iota(jnp.int32, hs.shape, 0)
    bot = jnp.where(row8 >= 8 - s, hs, xs[T - 8:T])
    return jnp.concatenate([xs[0:T - 8], bot], axis=0)


CONV_ROWS = 128
FFN_CONV_ROWS = 256
CONV_LANES = 512


def _conv_apply(x, halo, w_ref, wsl, b, K):
    acc = x * w_ref[K - 1, :, wsl] + b
    for s in range(1, K):
        acc = acc + _shift_down(x, halo, s) * w_ref[K - 1 - s, :, wsl]
    return acc


def _conv_accum(dy, dyn, xv, w_ref, dw_ref, db_ref, wsl, K):
    acc = dy * w_ref[K - 1, :, wsl]
    dw_ref[K - 1, :, wsl] += jnp.sum(dy * xv, axis=0, keepdims=True)
    for s in range(1, K):
        ahead = _shift_up(dy, dyn, s)
        acc = acc + ahead * w_ref[K - 1 - s, :, wsl]
        dw_ref[K - 1 - s, :, wsl] += jnp.sum(ahead * xv, axis=0, keepdims=True)
    db_ref[:, wsl] += jnp.sum(dy, axis=0, keepdims=True)
    return acc


def _row_specs(T, S, width):
    main = pl.BlockSpec((T, width), lambda i: (i, 0))
    prev = pl.BlockSpec((8, width), lambda i: (jnp.maximum(i * (T // 8) - 1, 0), 0))
    nxt = pl.BlockSpec((8, width), lambda i: (jnp.minimum((i + 1) * (T // 8), S // 8 - 1), 0))
    return main, prev, nxt


def _full(shape):
    return pl.BlockSpec(shape, lambda i: (0,) * len(shape))


def _silu_grad(y):
    sg = _sigmoid(y)
    return sg * (1.0 + y * (1.0 - sg))


def _ssm_conv_fwd(zx, w, b, d_inner, conv_dim, name):
    S, wz = zx.shape
    K = w.shape[0]
    T = _pick(S, CONV_ROWS, 8)
    cw = _pick(conv_dim, CONV_LANES)

    def body(x_ref, h_ref, w_ref, b_ref, c_ref):
        has_prev = pl.program_id(0) > 0
        for cs in range(0, conv_dim, cw):
            so, sx = slice(cs, cs + cw), slice(d_inner + cs, d_inner + cs + cw)
            halo = jnp.where(has_prev, h_ref[:, sx], 0.0)
            c_ref[:, so] = _conv_apply(x_ref[:, sx], halo, w_ref, so, b_ref[:, so], K)

    main, prev, _ = _row_specs(T, S, wz)
    return pl.pallas_call(
        body, grid=(S // T,), in_specs=[main, prev, _full((K, 1, conv_dim)), _full((1, conv_dim))],
        out_specs=pl.BlockSpec((T, conv_dim), lambda i: (i, 0)),
        out_shape=jax.ShapeDtypeStruct((S, conv_dim), F32),
        compiler_params=_cp("parallel"), name=name)(zx, zx, w.reshape(K, 1, conv_dim), b.reshape(1, conv_dim))


def _ssm_conv_bwd(dxs, dbm, dcm, conv, zx, dz, ddt, w, d_inner, name):
    S, wz = zx.shape
    K, conv_dim = w.shape
    gn = dbm.shape[1]
    T = _pick(S, CONV_ROWS, 8)
    cw = _pick(math.gcd(d_inner, gn), CONV_LANES)
    nrow = S // T
    tail = wz - d_inner - conv_dim
    assert ddt.shape[1] == tail

    def body(dx_ref, dxn_ref, db_ref_, dbn_ref, dc_ref, dcn_ref, y_ref, yn_ref, x_ref, dz_ref, ddt_ref,
             w_ref, o_ref, dw_ref, dbias_ref):
        i = pl.program_id(0)

        @pl.when(i == 0)
        def _():
            dw_ref[...] = jnp.zeros_like(dw_ref)
            dbias_ref[...] = jnp.zeros_like(dbias_ref)

        has_next = i < nrow - 1
        for cs in range(0, d_inner, cw):
            o_ref[:, cs:cs + cw] = dz_ref[:, cs:cs + cw].astype(o_ref.dtype)
        o_ref[:, d_inner + conv_dim:wz] = ddt_ref[...].astype(o_ref.dtype)
        for cs in range(0, conv_dim, cw):
            so, sx = slice(cs, cs + cw), slice(d_inner + cs, d_inner + cs + cw)
            if cs < d_inner:
                src, srcn, ss = dx_ref, dxn_ref, slice(cs, cs + cw)
            elif cs < d_inner + gn:
                src, srcn, ss = db_ref_, dbn_ref, slice(cs - d_inner, cs - d_inner + cw)
            else:
                src, srcn, ss = dc_ref, dcn_ref, slice(cs - d_inner - gn, cs - d_inner - gn + cw)
            dy = src[:, ss] * _silu_grad(y_ref[:, so])
            dyn = jnp.where(has_next, srcn[:, ss] * _silu_grad(yn_ref[:, so]), 0.0)
            o_ref[:, sx] = _conv_accum(dy, dyn, x_ref[:, sx], w_ref, dw_ref, dbias_ref, so, K).astype(o_ref.dtype)

    xm, _, xn = _row_specs(T, S, d_inner)
    gm, _, gnx = _row_specs(T, S, gn)
    cm, _, cn = _row_specs(T, S, conv_dim)
    zm, _, _ = _row_specs(T, S, wz)
    tm_, _, _ = _row_specs(T, S, tail)
    dzx, dw, db = pl.pallas_call(
        body, grid=(nrow,),
        in_specs=[xm, xn, gm, gnx, gm, gnx, cm, cn, zm, xm, tm_, _full((K, 1, conv_dim))],
        out_specs=[zm, _full((K, 1, conv_dim)), _full((1, conv_dim))],
        out_shape=[jax.ShapeDtypeStruct((S, wz), BF16), jax.ShapeDtypeStruct((K, 1, conv_dim), F32),
                   jax.ShapeDtypeStruct((1, conv_dim), F32)],
        compiler_params=_cp("arbitrary"), name=name)(
            dxs, dxs, dbm, dbm, dcm, dcm, conv, conv, zx, dz, ddt, w.reshape(K, 1, conv_dim))
    return dzx, dw.reshape(K, conv_dim), db


def _ffn_conv_fwd(up, w, b, name):
    S, C = up.shape
    F = C // 2
    K = w.shape[0]
    T = _pick(S, FFN_CONV_ROWS, 8)
    cw = _pick(F, CONV_LANES)

    def body(x_ref, h_ref, w_ref, b_ref, a_ref):
        has_prev = pl.program_id(0) > 0
        for cs in range(0, F, cw):
            sg, su = slice(cs, cs + cw), slice(F + cs, F + cs + cw)
            gate = _conv_apply(x_ref[:, sg], jnp.where(has_prev, h_ref[:, sg], 0.0), w_ref, sg, b_ref[:, sg], K)
            upv = _conv_apply(x_ref[:, su], jnp.where(has_prev, h_ref[:, su], 0.0), w_ref, su, b_ref[:, su], K)
            a_ref[:, sg] = (gate * _sigmoid(gate) * upv).astype(a_ref.dtype)

    main, prev, _ = _row_specs(T, S, C)
    return pl.pallas_call(
        body, grid=(S // T,), in_specs=[main, prev, _full((K, 1, C)), _full((1, C))],
        out_specs=pl.BlockSpec((T, F), lambda i: (i, 0)), out_shape=jax.ShapeDtypeStruct((S, F), BF16),
        compiler_params=_cp("parallel"), name=name)(up, up, w.reshape(K, 1, C), b.reshape(1, C))


def _ffn_conv_bwd(dact, up, w, b, name):
    S, C = up.shape
    F = C // 2
    K = w.shape[0]
    T = _pick(S, FFN_CONV_ROWS, 8)
    cw = _pick(F, CONV_LANES)
    nrow = S // T

    def du(gate, upv, d):
        sg = _sigmoid(gate)
        return d * upv * sg * (1.0 + gate * (1.0 - sg)), d * gate * sg

    def body(d_ref, dn_ref, x_ref, xp_ref, xn_ref, w_ref, b_ref, dx_ref, dw_ref, db_ref):
        i = pl.program_id(0)

        @pl.when(i == 0)
        def _():
            dw_ref[...] = jnp.zeros_like(dw_ref)
            db_ref[...] = jnp.zeros_like(db_ref)

        has_prev, has_next = i > 0, i < nrow - 1
        for cs in range(0, F, cw):
            sf = slice(cs, cs + cw)
            cols = [slice(half * F + cs, half * F + cs + cw) for half in range(2)]
            xs = [x_ref[:, sc] for sc in cols]
            xps = [jnp.where(has_prev, xp_ref[:, sc], 0.0) for sc in cols]
            u = [_conv_apply(xs[q], xps[q], w_ref, cols[q], b_ref[:, cols[q]], K) for q in range(2)]
            un = [_conv_apply(xn_ref[:, cols[q]], xs[q][T - 8:T], w_ref, cols[q], b_ref[:, cols[q]], K)
                  for q in range(2)]
            dys = du(u[0], u[1], d_ref[:, sf])
            dyns = du(un[0], un[1], dn_ref[:, sf])
            for q in range(2):
                dyn = jnp.where(has_next, dyns[q], 0.0)
                dx_ref[:, cols[q]] = _conv_accum(dys[q], dyn, xs[q], w_ref, dw_ref, db_ref, cols[q],
                                                 K).astype(dx_ref.dtype)

    am, _, an = _row_specs(T, S, F)
    xm, xp_, xn_ = _row_specs(T, S, C)
    dx, dw, db = pl.pallas_call(
        body, grid=(nrow,), in_specs=[am, an, xm, xp_, xn_, _full((K, 1, C)), _full((1, C))],
        out_specs=[xm, _full((K, 1, C)), _full((1, C))],
        out_shape=[jax.ShapeDtypeStruct((S, C), BF16), jax.ShapeDtypeStruct((K, 1, C), F32),
                   jax.ShapeDtypeStruct((1, C), F32)],
        compiler_params=_cp("arbitrary"), name=name)(dact, dact, up, up, up, w.reshape(K, 1, C), b.reshape(1, C))
    return dx, dw.reshape(K, C), db


def _cumsum_rows(v):
    n = v.shape[0]
    row = lax.broadcasted_iota(jnp.int32, v.shape, 0)
    k = 1
    while k < n:
        v = v + jnp.where(row >= k, pltpu.roll(v, k, 0), 0.0)
        k *= 2
    return v


def _rev_cumsum_rows(v):
    n = v.shape[0]
    row = lax.broadcasted_iota(jnp.int32, v.shape, 0)
    k = 1
    while k < n:
        v = v + jnp.where(row < n - k, pltpu.roll(v, n - k, 0), 0.0)
        k *= 2
    return v


def _ssd_common(x_ref, dtr_ref, bias_ref, alog_ref, gw):
    Q = SSM_CHUNK
    X = _silu(x_ref[...])
    pre = dtr_ref[...] + bias_ref[...]
    dt = _softplus(pre)
    a = -jnp.exp(alog_ref[...])
    cs = _cumsum_rows(dt * a)
    row = lax.broadcasted_iota(jnp.int32, (Q, gw), 0)
    cs_last = jnp.sum(jnp.where(row == Q - 1, cs, 0.0), axis=0, keepdims=True)
    return X, pre, dt, a, cs, cs_last, row


def _head_decay(cs, head_mask):
    Q = SSM_CHUNK
    col = jnp.max(jnp.where(head_mask, cs, NEG), axis=1, keepdims=True)
    acol = jnp.broadcast_to(col, (Q, Q))
    arow = acol.T
    ii = lax.broadcasted_iota(jnp.int32, (Q, Q), 0)
    jj = lax.broadcasted_iota(jnp.int32, (Q, Q), 1)
    tril = ii >= jj
    return jnp.where(tril, jnp.exp(jnp.where(tril, acol - arow, 0.0)), 0.0), tril


class _Win:
    def __init__(self, ref, idx):
        self.ref, self.idx = ref, idx

    def __getitem__(self, _):
        return self.ref[self.idx]

    def __setitem__(self, _, value):
        self.ref[self.idx] = value


def _ssd_windows(g, d_inner, gw):
    N, G = SSM_STATE, SSM_GROUPS
    rows = slice(None)
    x = (rows, slice(g * gw, (g + 1) * gw))
    b = (rows, slice(d_inner + g * N, d_inner + (g + 1) * N))
    c = (rows, slice(d_inner + (G + g) * N, d_inner + (G + g + 1) * N))
    n = (rows, slice(g * N, (g + 1) * N))
    return x, b, c, n


def _ssd_fwd(xbc, dtr, bias, alog, dsk, d_inner, name):
    S = xbc.shape[0]
    Q, N, G, P = SSM_CHUNK, SSM_STATE, SSM_GROUPS, SSM_HEAD_DIM
    gw = d_inner // G
    R = gw // P
    nc = S // Q

    def body(conv_ref, dtr_ref, bias_ref, alog_ref, d_ref, y_ref, sp_ref, s_scr):
        first = pl.program_id(0) == 0
        for g in range(G):
            x, b, c, _ = _ssd_windows(g, d_inner, gw)
            one_group(first, _Win(conv_ref, x), _Win(conv_ref, b), _Win(conv_ref, c), _Win(dtr_ref, x),
                      _Win(bias_ref, g), _Win(alog_ref, g), _Win(d_ref, g), _Win(y_ref, x), _Win(sp_ref, g),
                      _Win(s_scr, g))

    def one_group(first, x_ref, b_ref, c_ref, dtr_ref, bias_ref, alog_ref, d_ref, y_ref, sp_ref, s_scr):
        @pl.when(first)
        def _():
            s_scr[...] = jnp.zeros((gw, N), F32)

        X, _, dt, a, cs, cs_last, row = _ssd_common(x_ref, dtr_ref, bias_ref, alog_ref, gw)
        Bm, Cm = _silu(b_ref[...]), _silu(c_ref[...])
        xdt = X * dt
        lane = lax.broadcasted_iota(jnp.int32, (Q, gw), 1)
        sprev = s_scr[...]
        sp_ref[...] = sprev
        cb = _dot_nt(Cm, Bm)
        y = jnp.exp(cs) * _dot_nt(Cm, sprev)
        hms = [(lane >= r * P) & (lane < (r + 1) * P) for r in range(R)]
        dec_ls = [_head_decay(cs, hm)[0] for hm in hms]
        for part in [_dot_nn(cb * dec_l, jnp.where(hm, xdt, 0.0)) for dec_l, hm in zip(dec_ls, hms)]:
            y = y + part
        dec = jnp.exp(cs_last - cs)
        cd = jnp.exp(jnp.broadcast_to(cs_last, (Q, gw)).T)
        s_scr[...] = sprev * cd + _dot_tn(xdt * dec, Bm)
        y_ref[...] = y + d_ref[...] * X

    conv_dim = xbc.shape[1]
    row = pl.BlockSpec((Q, d_inner), lambda c: (c, 0))
    p_spec = _full((G, 1, gw))
    return pl.pallas_call(
        body, grid=(nc,),
        in_specs=[pl.BlockSpec((Q, conv_dim), lambda c: (c, 0)), row, p_spec, p_spec, p_spec],
        out_specs=[row, pl.BlockSpec((None, G, gw, N), lambda c: (c, 0, 0, 0))],
        out_shape=[jax.ShapeDtypeStruct((S, d_inner), F32), jax.ShapeDtypeStruct((nc, G, gw, N), F32)],
        scratch_shapes=[pltpu.VMEM((G, gw, N), F32)],
        compiler_params=_cp("arbitrary"), name=name)(xbc, dtr, bias, alog, dsk)


def _ssd_bwd(xbc, dtr, bias, alog, dsk, sprev_all, dy, d_inner, name):
    S = xbc.shape[0]
    Q, N, G, P = SSM_CHUNK, SSM_STATE, SSM_GROUPS, SSM_HEAD_DIM
    gw = d_inner // G
    R = gw // P
    nc = S // Q

    def body(conv_ref, dtr_ref, bias_ref, alog_ref, d_ref, sp_ref, dy_ref,
             dx_ref, db_ref, dc_ref, ddt_ref, dbias_ref, dalog_ref, dd_ref, ds_scr):
        first = pl.program_id(0) == 0
        for g in range(G):
            x, b, c, n = _ssd_windows(g, d_inner, gw)
            one_group(first, _Win(conv_ref, x), _Win(conv_ref, b), _Win(conv_ref, c), _Win(dtr_ref, x),
                      _Win(bias_ref, g), _Win(alog_ref, g), _Win(d_ref, g), _Win(sp_ref, g), _Win(dy_ref, x),
                      _Win(dx_ref, x), _Win(db_ref, n), _Win(dc_ref, n), _Win(ddt_ref, x), _Win(dbias_ref, g),
                      _Win(dalog_ref, g), _Win(dd_ref, g), _Win(ds_scr, g))

    def one_group(first, x_ref, b_ref, c_ref, dtr_ref, bias_ref, alog_ref, d_ref, sp_ref, dy_ref,
                  dx_ref, db_ref, dc_ref, ddt_ref, dbias_ref, dalog_ref, dd_ref, ds_scr):
        @pl.when(first)
        def _():
            ds_scr[...] = jnp.zeros((gw, N), F32)
            dbias_ref[...] = jnp.zeros((1, gw), F32)
            dalog_ref[...] = jnp.zeros((1, gw), F32)
            dd_ref[...] = jnp.zeros((1, gw), F32)

        X, pre, dt, a, cs, cs_last, row = _ssd_common(x_ref, dtr_ref, bias_ref, alog_ref, gw)
        Bm, Cm = _silu(b_ref[...]), _silu(c_ref[...])
        dY = dy_ref[...]
        sprev = sp_ref[...]
        dsn = ds_scr[...]
        xdt = X * dt
        lane = lax.broadcasted_iota(jnp.int32, (Q, gw), 1)
        lane1 = lax.broadcasted_iota(jnp.int32, (1, gw), 1)
        srow = lax.broadcasted_iota(jnp.int32, (gw, N), 0)
        ecs = jnp.exp(cs)
        dec = jnp.exp(cs_last - cs)
        cd = jnp.exp(jnp.broadcast_to(cs_last, (Q, gw)).T)
        dd_ref[...] += jnp.sum(dY * X, axis=0, keepdims=True)
        dX = d_ref[...] * dY
        ey = ecs * dY
        dcs = ey * _dot_nt(Cm, sprev)
        dC = _dot_nn(ey, sprev)
        ds_scr[...] = cd * dsn + _dot_tn(ey, Cm)
        wmat = _dot_nt(Bm, dsn)
        dxdt = dec * wmat
        xd = xdt * dec
        dB = _dot_nn(xd, dsn)
        ddec = xdt * wmat * dec
        dcs = dcs - ddec
        dlast = jnp.sum(ddec, axis=0, keepdims=True)
        qmat = dsn * sprev * cd
        cb = _dot_nt(Cm, Bm)
        dcb = jnp.zeros((Q, Q), F32)
        dcs_rep = jnp.zeros((Q, gw), F32)
        dtx_rep = jnp.zeros((Q, gw), F32)
        hms = [(lane >= r * P) & (lane < (r + 1) * P) for r in range(R)]
        decs = [_head_decay(cs, hm) for hm in hms]
        dyrs = [jnp.where(hm, dY, 0.0) for hm in hms]
        graws = [_dot_nt(dyr, xdt) for dyr in dyrs]
        backs = [_dot_tn(cb * dec_l, dyr) for (dec_l, _), dyr in zip(decs, dyrs)]
        for r in range(R):
            hm, (dec_l, tril) = hms[r], decs[r]
            gmat = jnp.where(tril, graws[r], 0.0)
            dcb = dcb + gmat * dec_l
            e = gmat * cb * dec_l
            v = (jnp.sum(e, axis=1, keepdims=True) - jnp.sum(e.T, axis=1, keepdims=True)
                 + jnp.sum(jnp.where(hm, dcs, 0.0), axis=1, keepdims=True))
            dxdt = dxdt + backs[r]
            hm1 = (lane1 >= r * P) & (lane1 < (r + 1) * P)
            t_last = (jnp.sum(jnp.where(hm1, dlast, 0.0), axis=1, keepdims=True)
                      + jnp.sum(jnp.where((srow >= r * P) & (srow < (r + 1) * P), qmat, 0.0), keepdims=True))
            dcs_rep = dcs_rep + jnp.where(hm, v, 0.0) + jnp.where(hm & (row == Q - 1), t_last, 0.0)
        for r in range(R):
            hm = (lane >= r * P) & (lane < (r + 1) * P)
            w_r = jnp.sum(jnp.where(hm, dxdt * X, 0.0), axis=1, keepdims=True)
            dtx_rep = dtx_rep + jnp.where(hm, w_r, 0.0)
        dadt = _rev_cumsum_rows(dcs_rep)
        ddt = a * dadt + dtx_rep
        dalog_ref[...] += jnp.sum(dt * dadt, axis=0, keepdims=True) * a
        draw = ddt * _sigmoid(pre)
        ddt_ref[...] = draw
        dbias_ref[...] += jnp.sum(draw, axis=0, keepdims=True)
        dx_ref[...] = dX + dxdt * dt
        db_ref[...] = dB + _dot_tn(dcb, Cm)
        dc_ref[...] = dC + _dot_nn(dcb, Bm)

    conv_dim = xbc.shape[1]
    rev = lambda c: nc - 1 - c
    row = pl.BlockSpec((Q, d_inner), lambda c: (rev(c), 0))
    n_spec = pl.BlockSpec((Q, G * N), lambda c: (rev(c), 0))
    s_spec = pl.BlockSpec((None, G, gw, N), lambda c: (rev(c), 0, 0, 0))
    p_spec = _full((G, 1, gw))
    gshape = jax.ShapeDtypeStruct((G, 1, gw), F32)
    return pl.pallas_call(
        body, grid=(nc,),
        in_specs=[pl.BlockSpec((Q, conv_dim), lambda c: (rev(c), 0)), row, p_spec, p_spec, p_spec, s_spec, row],
        out_specs=[row, n_spec, n_spec, row, p_spec, p_spec, p_spec],
        out_shape=[jax.ShapeDtypeStruct((S, d_inner), F32), jax.ShapeDtypeStruct((S, G * N), F32),
                   jax.ShapeDtypeStruct((S, G * N), F32), jax.ShapeDtypeStruct((S, d_inner), F32),
                   gshape, gshape, gshape],
        scratch_shapes=[pltpu.VMEM((G, gw, N), F32)],
        compiler_params=_cp("arbitrary"), name=name)(xbc, dtr, bias, alog, dsk, sprev_all, dy)


def _gnorm_fwd(y, zx, w, name):
    S, d_inner = y.shape
    G = SSM_GROUPS
    gw = d_inner // G
    T = _pick(S, 256, 8)

    def body(y_ref, z_ref, w_ref, o_ref):
        for k in range(G):
            sl = slice(k * gw, (k + 1) * gw)
            z = z_ref[:, sl]
            gk = y_ref[:, sl] * z * _sigmoid(z)
            r = lax.rsqrt(jnp.mean(gk * gk, axis=-1, keepdims=True) + NORM_EPS)
            o_ref[:, sl] = (gk * r * w_ref[:, sl]).astype(o_ref.dtype)

    row = pl.BlockSpec((T, d_inner), lambda i: (i, 0))
    vec = pl.BlockSpec((1, d_inner), lambda i: (0, 0))
    return pl.pallas_call(body, grid=(S // T,), in_specs=[row, row, vec], out_specs=row,
                          out_shape=jax.ShapeDtypeStruct((S, d_inner), BF16),
                          compiler_params=_cp("parallel"), name=name)(y, zx, w.reshape(1, d_inner))


def _gnorm_bwd(y, zx, w, dout, name):
    S, d_inner = y.shape
    G = SSM_GROUPS
    gw = d_inner // G
    T = _pick(S, 256, 8)

    def body(y_ref, z_ref, w_ref, d_ref, dy_ref, dz_ref, dw_ref):
        @pl.when(pl.program_id(0) == 0)
        def _():
            dw_ref[...] = jnp.zeros_like(dw_ref)

        for k in range(G):
            sl = slice(k * gw, (k + 1) * gw)
            z, yv, d = z_ref[:, sl], y_ref[:, sl], d_ref[:, sl]
            sg = _sigmoid(z)
            sz = z * sg
            gk = yv * sz
            r = lax.rsqrt(jnp.mean(gk * gk, axis=-1, keepdims=True) + NORM_EPS)
            gh = gk * r
            dw_ref[:, sl] += jnp.sum(d * gh, axis=0, keepdims=True)
            dg = d * w_ref[:, sl]
            dgk = r * (dg - gh * jnp.mean(dg * gh, axis=-1, keepdims=True))
            dy_ref[:, sl] = dgk * sz
            dz_ref[:, sl] = dgk * yv * sg * (1.0 + z * (1.0 - sg))

    row = pl.BlockSpec((T, d_inner), lambda i: (i, 0))
    vec = pl.BlockSpec((1, d_inner), lambda i: (0, 0))
    return pl.pallas_call(
        body, grid=(S // T,), in_specs=[row, row, vec, row], out_specs=[row, row, vec],
        out_shape=[jax.ShapeDtypeStruct((S, d_inner), F32)] * 2 + [jax.ShapeDtypeStruct((1, d_inner), F32)],
        compiler_params=_cp("arbitrary"), name=name)(y, zx, w.reshape(1, d_inner), dout)


def _adam_math(g, w, m, v):
    m = ADAM_B1 * m + (1.0 - ADAM_B1) * g
    v = ADAM_B2 * v + (1.0 - ADAM_B2) * (g * g)
    m_hat = m / (1.0 - ADAM_B1 ** ADAM_STEP)
    v_hat = v / (1.0 - ADAM_B2 ** ADAM_STEP)
    delta = -ADAM_LR * (m_hat / (jnp.sqrt(v_hat) + ADAM_EPS) + ADAM_WD * w)
    return delta, m, v


def _adamw_big(own, sib, w, m, v, layer, prev, name):
    L, A, Bc = w.shape
    T = _pick(A, max(8, (1 << 20) // (4 * Bc)), 16)

    def body(o_ref, s_ref, w_ref, m_ref, v_ref, *rest):
        g_ref, d_ref, nm_ref, nv_ref = rest[-4:]
        so = o_ref[0].astype(F32)
        ss = s_ref[0].astype(F32)
        for k in range(1, N_CHIPS):
            so = so + o_ref[k].astype(F32)
            ss = ss + s_ref[k].astype(F32)
        g = so + ss
        delta, nm, nv = _adam_math(g, w_ref[...], m_ref[...], v_ref[...])
        g_ref[...] = g
        d_ref[...] = delta
        nm_ref[...] = nm
        nv_ref[...] = nv

    part = pl.BlockSpec((N_CHIPS, T, Bc), lambda i: (0, i, 0))
    blk = pl.BlockSpec((None, T, Bc), lambda i: (layer, i, 0))
    shp = jax.ShapeDtypeStruct(w.shape, F32)
    in_specs, args, kwargs = [part, part, blk, blk, blk], [own, sib, w, m, v], {}
    if prev is not None:
        in_specs += [pl.BlockSpec(memory_space=pl.ANY)] * 4
        args += list(prev)
        kwargs["input_output_aliases"] = {5 + q: q for q in range(4)}
    return pl.pallas_call(body, grid=(A // T,), in_specs=in_specs, out_specs=[blk] * 4, out_shape=[shp] * 4,
                          compiler_params=_cp("parallel"), name=name, **kwargs)(*args)


def _sum_devices(parts, name):
    _, R, C = parts.shape

    def body(p_ref, o_ref):
        acc = p_ref[0]
        for k in range(1, N_DEV):
            acc = acc + p_ref[k]
        o_ref[...] = acc

    return pl.pallas_call(body, out_shape=jax.ShapeDtypeStruct((R, C), F32), name=name)(parts)


def _adamw_small(g, w, m, v, name):
    def body(g_ref, w_ref, m_ref, v_ref, d_ref, nm_ref, nv_ref):
        delta, nm, nv = _adam_math(g_ref[...], w_ref[...], m_ref[...], v_ref[...])
        d_ref[...] = delta
        nm_ref[...] = nm
        nv_ref[...] = nv

    shp = jax.ShapeDtypeStruct(g.shape, F32)
    return pl.pallas_call(body, out_shape=[shp] * 3, name=name)(g, w, m, v)


PACK_COLS = 1024


def _pack(arrs):
    flat = jnp.concatenate([a.reshape(-1).astype(F32) for a in arrs])
    n = flat.shape[0]
    rows = -(-n // (8 * PACK_COLS)) * 8
    return jnp.pad(flat, (0, rows * PACK_COLS - n)).reshape(rows, PACK_COLS)


def _unpack(packed, shapes):
    flat = packed.reshape(-1)
    out, off = [], 0
    for s in shapes:
        n = math.prod(s)
        out.append(flat[off:off + n].reshape(s))
        off += n
    return out


def _shard_ref(ref, kind, k, n):
    if kind == "col":
        return ref.at[:, pl.ds(pl.multiple_of(k * n, 128), n)]
    if kind == "row":
        return ref.at[pl.ds(pl.multiple_of(k * n, 16), n), :]
    return ref.at[k]


def _chip_peers():
    x, y, c = lax.axis_index("x"), lax.axis_index("y"), lax.axis_index("c")
    return x, y, c, [(1 - x, y), (x, 1 - y), (1 - x, 1 - y)]


class _Exchange:
    def __init__(self, mode, items):
        self.mode, self.items = mode, items
        self.arrays = []
        for it in items:
            if not any(it[0] is a for a in self.arrays):
                self.arrays.append(it[0])
        self.src_idx = [next(i for i, a in enumerate(self.arrays) if a is it[0]) for it in items]
        self.out_shapes = [jax.ShapeDtypeStruct(it[-1], it[0].dtype) for it in items]
        n = len(items)
        if mode == "swap":
            self.scratch = [pltpu.SemaphoreType.DMA((n,)), pltpu.SemaphoreType.DMA((n,))]
        else:
            self.scratch = [pltpu.SemaphoreType.DMA((3 * n,)), pltpu.SemaphoreType.DMA((3 * n,)),
                            pltpu.SemaphoreType.DMA((n,))]

    def _copies(self, ins, outs, sems):
        if self.mode == "swap":
            send_sems, recv_sems = sems
            x, y, c = lax.axis_index("x"), lax.axis_index("y"), lax.axis_index("c")
            sent = [pltpu.make_async_remote_copy(
                src_ref=ins[self.src_idx[t]], dst_ref=outs[t], send_sem=send_sems.at[t], recv_sem=recv_sems.at[t],
                device_id=(x, y, 1 - c), device_id_type=MESH) for t in range(len(self.items))]
            return [], sent, sent
        send_sems, recv_sems, loc_sems = sems
        x, y, c, peers = _chip_peers()
        me = 2 * x + y
        local, sent, arriving = [], [], []
        for t, it in enumerate(self.items):
            src_arr = ins[self.src_idx[t]]
            if self.mode == "gather":
                _, layer, kind, n, _ = it
                src = src_arr if layer is None else src_arr.at[layer]
                src_for = lambda k: src
                dst_from = lambda k: _shard_ref(outs[t], kind, k, n)
            else:
                _, kind, n, _ = it
                src_for = lambda k: _shard_ref(src_arr, kind, k, n)
                dst_from = lambda k: outs[t].at[k]
            local.append(pltpu.make_async_copy(src_for(me), dst_from(me), loc_sems.at[t]))
            for j, (px, py) in enumerate(peers):
                pk = 2 * px + py
                args = dict(send_sem=send_sems.at[3 * t + j], recv_sem=recv_sems.at[3 * t + j],
                            device_id=(px, py, c), device_id_type=MESH)
                sent.append(pltpu.make_async_remote_copy(src_ref=src_for(pk), dst_ref=dst_from(me), **args))
                arriving.append(pltpu.make_async_remote_copy(src_ref=src_for(pk), dst_ref=dst_from(pk), **args))
        return local, sent, arriving

    def start(self, ins, outs, sems):
        local, sent, arriving = self._copies(ins, outs, sems)
        for cp in local + sent:
            cp.start()
        for cp in arriving:
            cp._used = True

    def wait(self, ins, outs, sems):
        local, sent, arriving = self._copies(ins, outs, sems)
        for cp in arriving:
            cp.wait_recv()
        for cp in sent:
            cp.wait_send()
        for cp in local:
            cp.wait()


class _Multi:
    def __init__(self, parts):
        self.parts = parts
        self.arrays = [a for p in parts for a in p.arrays]
        self.out_shapes = [s for p in parts for s in p.out_shapes]
        self.scratch = [s for p in parts for s in p.scratch]

    def _split(self, ins, outs, sems):
        i = o = s = 0
        for p in self.parts:
            ni, no, ns = len(p.arrays), len(p.out_shapes), len(p.scratch)
            yield p, ins[i:i + ni], outs[o:o + no], sems[s:s + ns]
            i, o, s = i + ni, o + no, s + ns

    def start(self, ins, outs, sems):
        for p, a, b, c in self._split(ins, outs, sems):
            p.start(a, b, c)

    def wait(self, ins, outs, sems):
        for p, a, b, c in self._split(ins, outs, sems):
            p.wait(a, b, c)


def _run_exchange(ex, name):
    nin, nout = len(ex.arrays), len(ex.out_shapes)

    def body(*refs):
        ins, outs, sems = refs[:nin], refs[nin:nin + nout], refs[nin + nout:]
        ex.start(ins, outs, sems)
        ex.wait(ins, outs, sems)

    anyspec = pl.BlockSpec(memory_space=pl.ANY)
    return pl.pallas_call(body, in_specs=[anyspec] * nin, out_specs=[anyspec] * nout, out_shape=ex.out_shapes,
                          scratch_shapes=ex.scratch, name=name)(*ex.arrays)


def _start_exchange(ex, name, after=None):
    nin, nout, nsem = len(ex.arrays), len(ex.out_shapes), len(ex.scratch)
    hbm = pl.BlockSpec(memory_space=pltpu.HBM)
    sem = pl.BlockSpec(memory_space=pltpu.SEMAPHORE)

    n_operands = nin + nout + (after is not None)

    def body(*refs):
        ins, lands = refs[:nin], refs[nin:nin + nout]
        sems = refs[n_operands:n_operands + nsem]
        ex.start(ins, lands, sems)
        refs[-1][...] = jnp.zeros_like(refs[-1])

    args = [pltpu.with_memory_space_constraint(a, pltpu.HBM) for a in ex.arrays]
    args += [pltpu.with_memory_space_constraint(lax.empty(s.shape, s.dtype), pltpu.HBM) for s in ex.out_shapes]
    thru = [pltpu.HBM(a.shape, a.dtype) for a in ex.arrays] + [pltpu.HBM(s.shape, s.dtype) for s in ex.out_shapes]
    extra = [] if after is None else [pl.BlockSpec(memory_space=pl.ANY)]
    args += [] if after is None else [after]
    return pl.pallas_call(
        body, name=name, in_specs=[hbm] * (nin + nout) + extra,
        out_shape=tuple(ex.scratch) + tuple(thru) + (jax.ShapeDtypeStruct((8, 128), F32),),
        out_specs=tuple([sem] * nsem + [hbm] * (nin + nout) + [pl.BlockSpec(memory_space=pltpu.VMEM)]),
        input_output_aliases={q: nsem + q for q in range(nin + nout)},
        compiler_params=pltpu.CompilerParams(has_side_effects=pltpu.SideEffectType.DATAFLOW_SIDE_EFFECTING))(*args)


def _finish_exchange(ex, handles, after, name):
    nin, nout, nsem = len(ex.arrays), len(ex.out_shapes), len(ex.scratch)
    hbm = pl.BlockSpec(memory_space=pltpu.HBM)
    sem = pl.BlockSpec(memory_space=pltpu.SEMAPHORE)
    sems, thru = handles[:nsem], handles[nsem:nsem + nin + nout]

    def body(*refs):
        ins, lands = refs[:nin], refs[nin:nin + nout]
        ex.wait(ins, lands, refs[nin + nout:nin + nout + nsem])

    outs = pl.pallas_call(
        body, name=name, in_specs=[hbm] * (nin + nout) + [sem] * nsem + [pl.BlockSpec(memory_space=pl.ANY)],
        out_shape=tuple(pltpu.HBM(t.shape, t.dtype) for t in thru), out_specs=tuple([hbm] * (nin + nout)),
        input_output_aliases={q: q for q in range(nin + nout)},
        compiler_params=pltpu.CompilerParams(has_side_effects=pltpu.SideEffectType.DATAFLOW_SIDE_EFFECTING))(
            *thru, *sems, after)
    return outs[nin:]


def _all_gather_devices(v, name):
    def body(v_ref, o_ref, send_sems, recv_sems, loc_sem):
        x, y, c = lax.axis_index("x"), lax.axis_index("y"), lax.axis_index("c")
        me = 4 * x + 2 * y + c
        lc = pltpu.make_async_copy(v_ref, o_ref.at[me], loc_sem)
        lc.start()
        rel = [(bx, by, bc) for bx in (0, 1) for by in (0, 1) for bc in (0, 1)][1:]
        copies = []
        for j, (bx, by, bc) in enumerate(rel):
            px, py, pc = x ^ bx, y ^ by, c ^ bc
            copies.append((pltpu.make_async_remote_copy(
                src_ref=v_ref, dst_ref=o_ref.at[me], send_sem=send_sems.at[j], recv_sem=recv_sems.at[j],
                device_id=(px, py, pc), device_id_type=MESH), 4 * px + 2 * py + pc))
        for cp, _ in copies:
            cp.start()
        for j, (cp, pid) in enumerate(copies):
            pltpu.make_async_remote_copy(
                src_ref=v_ref, dst_ref=o_ref.at[pid], send_sem=send_sems.at[j], recv_sem=recv_sems.at[j],
                device_id=(x, y, c), device_id_type=MESH).wait_recv()
        for cp, _ in copies:
            cp.wait_send()
        lc.wait()

    anyspec = pl.BlockSpec(memory_space=pl.ANY)
    return pl.pallas_call(
        body, in_specs=[anyspec], out_specs=anyspec,
        out_shape=jax.ShapeDtypeStruct((N_DEV,) + v.shape, v.dtype),
        scratch_shapes=[pltpu.SemaphoreType.DMA((N_DEV - 1,)), pltpu.SemaphoreType.DMA((N_DEV - 1,)),
                        pltpu.SemaphoreType.DMA(())],
        name=name)(v)


BIG = ("attn_w_qkv", "attn_w_o", "ssm_w_in", "ssm_w_out", "ffn_w_up", "ffn_w_down")
BIG_KIND = {"attn_w_qkv": "col", "attn_w_o": "row", "ssm_w_in": "lead", "ssm_w_out": "row",
            "ffn_w_up": "col", "ffn_w_down": "row"}
SMALL_SHARDED = {"ssm_conv_w": 2, "ssm_conv_b": 1, "ssm_norm_w": 1, "ffn_conv_w": 2}
SMALL = ("mix_norm_w", "ssm_conv_w", "ssm_conv_b", "ssm_dt_bias", "ssm_a_log", "ssm_d", "ssm_norm_w",
         "ffn_norm_w", "ffn_conv_w", "ffn_conv_b", "final_norm_w")
WEIGHTS = ("mix_norm_w", "attn_w_qkv", "attn_w_o", "ssm_w_in", "ssm_conv_w", "ssm_conv_b", "ssm_dt_bias",
           "ssm_a_log", "ssm_d", "ssm_norm_w", "ssm_w_out", "ffn_norm_w", "ffn_w_up", "ffn_conv_w",
           "ffn_conv_b", "ffn_w_down", "final_norm_w")


def _shard_extent(name, shape):
    _, a, b = shape
    return {"col": b, "row": a, "lead": 1}[BIG_KIND[name]]


def _gather_item(w16, name):
    a, b = w16.shape
    kind = BIG_KIND[name]
    full = {"col": (a, N_CHIPS * b), "row": (N_CHIPS * a, b), "lead": (N_CHIPS, a, b)}[kind]
    return (w16, None, kind, _shard_extent(name, (1, a, b)), full)


def _layer_weights(i):
    j = i // 2
    mixer = [("attn_w_qkv", j), ("attn_w_o", j)] if i % 2 == 0 else [("ssm_w_in", j), ("ssm_w_out", j)]
    return mixer + [("ffn_w_up", i), ("ffn_w_down", i)]


def kernel(x, mix_norm_w, attn_w_qkv, attn_w_o, ssm_w_in, ssm_conv_w, ssm_conv_b, ssm_dt_bias, ssm_a_log, ssm_d, ssm_norm_w, ssm_w_out, ffn_norm_w, ffn_w_up, ffn_conv_w, ffn_conv_b, ffn_w_down, final_norm_w, loss_target, m_mix_norm_w, m_attn_w_qkv, m_attn_w_o, m_ssm_w_in, m_ssm_conv_w, m_ssm_conv_b, m_ssm_dt_bias, m_ssm_a_log, m_ssm_d, m_ssm_norm_w, m_ssm_w_out, m_ffn_norm_w, m_ffn_w_up, m_ffn_conv_w, m_ffn_conv_b, m_ffn_w_down, m_final_norm_w, v_mix_norm_w, v_attn_w_qkv, v_attn_w_o, v_ssm_w_in, v_ssm_conv_w, v_ssm_conv_b, v_ssm_dt_bias, v_ssm_a_log, v_ssm_d, v_ssm_norm_w, v_ssm_w_out, v_ffn_norm_w, v_ffn_w_up, v_ffn_conv_w, v_ffn_conv_b, v_ffn_w_down, v_final_norm_w):
    W = dict(mix_norm_w=mix_norm_w, attn_w_qkv=attn_w_qkv, attn_w_o=attn_w_o, ssm_w_in=ssm_w_in,
             ssm_conv_w=ssm_conv_w, ssm_conv_b=ssm_conv_b, ssm_dt_bias=ssm_dt_bias, ssm_a_log=ssm_a_log,
             ssm_d=ssm_d, ssm_norm_w=ssm_norm_w, ssm_w_out=ssm_w_out, ffn_norm_w=ffn_norm_w, ffn_w_up=ffn_w_up,
             ffn_conv_w=ffn_conv_w, ffn_conv_b=ffn_conv_b, ffn_w_down=ffn_w_down, final_norm_w=final_norm_w)
    M = dict(mix_norm_w=m_mix_norm_w, attn_w_qkv=m_attn_w_qkv, attn_w_o=m_attn_w_o, ssm_w_in=m_ssm_w_in,
             ssm_conv_w=m_ssm_conv_w, ssm_conv_b=m_ssm_conv_b, ssm_dt_bias=m_ssm_dt_bias, ssm_a_log=m_ssm_a_log,
             ssm_d=m_ssm_d, ssm_norm_w=m_ssm_norm_w, ssm_w_out=m_ssm_w_out, ffn_norm_w=m_ffn_norm_w,
             ffn_w_up=m_ffn_w_up, ffn_conv_w=m_ffn_conv_w, ffn_conv_b=m_ffn_conv_b, ffn_w_down=m_ffn_w_down,
             final_norm_w=m_final_norm_w)
    V = dict(mix_norm_w=v_mix_norm_w, attn_w_qkv=v_attn_w_qkv, attn_w_o=v_attn_w_o, ssm_w_in=v_ssm_w_in,
             ssm_conv_w=v_ssm_conv_w, ssm_conv_b=v_ssm_conv_b, ssm_dt_bias=v_ssm_dt_bias, ssm_a_log=v_ssm_a_log,
             ssm_d=v_ssm_d, ssm_norm_w=v_ssm_norm_w, ssm_w_out=v_ssm_w_out, ffn_norm_w=v_ffn_norm_w,
             ffn_w_up=v_ffn_w_up, ffn_conv_w=v_ffn_conv_w, ffn_conv_b=v_ffn_conv_b, ffn_w_down=v_ffn_w_down,
             final_norm_w=v_final_norm_w)

    S, D = x.shape[1], x.shape[2]
    xs = x.reshape(S, D)
    tgt = loss_target.reshape(S, D)
    depth = mix_norm_w.shape[0]
    heads = attn_w_o.shape[1] * N_CHIPS // HEAD_DIM
    AW = heads * HEAD_DIM
    d_inner = ssm_w_out.shape[1] * N_CHIPS
    ssm_heads = d_inner // SSM_HEAD_DIM
    G, P, N = SSM_GROUPS, SSM_HEAD_DIM, SSM_STATE
    gw = d_inner // G
    conv_dim = d_inner + 2 * G * N
    in_w = d_inner + conv_dim + ssm_heads
    in_pad = -(-in_w // 128) * 128
    shard_in = ssm_w_in.shape[2]
    xi, yi = lax.axis_index("x"), lax.axis_index("y")
    chip = 2 * xi + yi

    full = {}

    def land(keys, outs):
        for (n, l), o in zip(keys, outs):
            if n == "ssm_w_in":
                o = jnp.pad(jnp.concatenate([o[k] for k in range(N_CHIPS)], axis=1), ((0, 0), (0, in_pad - in_w)))
            full[(n, l)] = o

    def gather_ex(keys, extra=()):
        return _Exchange("gather", [_gather_item(W[n][l].astype(BF16), n) for n, l in keys] + list(extra))

    def fwd_mm(a, b, name, resid=None):
        return _matmul(a, b, "nn", F32, name, resid=resid)

    sm_names = list(SMALL_SHARDED)
    packed = _pack([W[n] for n in sm_names])
    lw = [_layer_weights(i) for i in range(depth)]
    first = lw[0][:1]
    first_ex = gather_ex(first, [(packed, None, "lead", 1, (N_CHIPS,) + packed.shape)])
    first_handles = _start_exchange(first_ex, "gather_first_start")
    batches = [lw[0][1:]] + lw[1:]
    pending = []
    for q, keys in enumerate(batches):
        ex = gather_ex(keys)
        pending.append((keys, ex, _start_exchange(ex, f"gather_start_{q}", after=first_handles[-1])))
    issued = first_handles[-1][0, 0] + sum(handles[-1][0, 0] for _, _, handles in pending)

    def arrive(q, after):
        keys, ex, handles = pending[q]
        land(keys, _finish_exchange(ex, handles, after, f"gather_wait_{q}"))

    def first_arrive(after):
        got = _finish_exchange(first_ex, first_handles, after, "gather_first_wait")
        land(first, got[:-1])
        per_chip = [_unpack(got[-1][k], [W[n].shape for n in sm_names]) for k in range(N_CHIPS)]
        for q, n in enumerate(sm_names):
            full[n] = jnp.concatenate([per_chip[k][q] for k in range(N_CHIPS)], axis=SMALL_SHARDED[n])

    tab = _perm_tokens(_rope_table(S))

    def rep_heads(p):
        return jnp.repeat(p, P).reshape(G, 1, gw)

    saved = []
    cur = xs
    for i in range(depth):
        j = i // 2
        if i > 0:
            arrive(i, cur)
        sv = {"x_in": cur}
        h = _rms_fwd(cur, mix_norm_w[i] + issued if i == 0 else mix_norm_w[i], f"mix_norm_fwd_{i}")
        sv["h"] = h
        if i % 2 == 0:
            h = _perm_tokens(h)
            sv["h"] = h
            if i == 0:
                first_arrive(h)
            qkv = _matmul(h, full[("attn_w_qkv", j)], "nn", F32, f"qkv_fwd_{i}", rope=(tab, AW))
            if i == 0:
                arrive(0, qkv)
            og = [_attn_fwd(qkv, g, heads, f"attn_fwd_{i}_{g}") for g in range(3)]
            o, lse = _attn_combine([a for a, _ in og], [b for _, b in og], f"attn_combine_{i}")
            mixed = _unperm_tokens(fwd_mm(o, full[("attn_w_o", j)], f"attn_out_fwd_{i}"))
            sv.update(qkv=qkv, o=o, lse=lse)
        else:
            zx = fwd_mm(h, full[("ssm_w_in", j)], f"ssm_in_fwd_{i}")
            conv = _ssm_conv_fwd(zx, full["ssm_conv_w"][j], full["ssm_conv_b"][j], d_inner, conv_dim,
                                 f"ssm_conv_fwd_{i}")
            prm = [rep_heads(p[j]) for p in (ssm_dt_bias, ssm_a_log, ssm_d)]
            dtr = jnp.repeat(zx[:, d_inner + conv_dim:in_w], P, axis=1)
            y, sprev = _ssd_fwd(conv, dtr, *prm, d_inner, f"ssd_fwd_{i}")
            gated = _gnorm_fwd(y, zx, full["ssm_norm_w"][j], f"ssm_norm_fwd_{i}")
            cur = fwd_mm(gated, full[("ssm_w_out", j)], f"ssm_out_fwd_{i}", resid=cur)
            sv.update(zx=zx, conv=conv, dtr=dtr, prm=prm, y=y, sprev=sprev, gated=gated)
            mixed = None
        if mixed is None:
            h2 = _rms_fwd(cur, ffn_norm_w[i], f"ffn_norm_fwd_{i}")
        else:
            cur, h2 = _rms_fwd(cur, ffn_norm_w[i], f"ffn_norm_fwd_{i}", add=mixed)
        sv["x_mid"] = cur
        up = fwd_mm(h2, full[("ffn_w_up", i)], f"ffn_up_fwd_{i}")
        act = _ffn_conv_fwd(up, full["ffn_conv_w"][i], ffn_conv_b[i], f"ffn_conv_fwd_{i}")
        cur = fwd_mm(act, full[("ffn_w_down", i)], f"ffn_down_fwd_{i}", resid=cur)
        sv.update(h2=h2, up=up, act=act)
        saved.append(sv)

    dx, d_final, loss_part = _loss_head(cur, final_norm_w, tgt, "loss_head")
    gbig, recv = {}, {}
    gs = {n: [None] * W[n].shape[0] for n in SMALL if n != "final_norm_w"}

    def scatter_ex(keys):
        return _Exchange("scatter", [(gbig[(n, l)], BIG_KIND[n], _shard_extent(n, W[n].shape),
                                      (N_CHIPS,) + W[n].shape[1:]) for n, l in keys])

    sib = {}

    def swap_ex(keys):
        return _Exchange("swap", [(recv[k], recv[k].shape) for k in keys])

    def bwd_mm(a, b, mode, dtype, name, send=(), swap=()):
        if not send and not swap:
            return _matmul(a, b, mode, dtype, name)
        parts = ([scatter_ex(send)] if send else []) + ([swap_ex(swap)] if swap else [])
        out, got = _matmul(a, b, mode, dtype, name, carry=_Multi(parts))
        recv.update(zip(send, got[:len(send)]))
        sib.update(zip(swap, got[len(send):]))
        return out

    sending = None
    to_swap = []
    for i in reversed(range(depth)):
        j = i // 2
        sv = saved[i]
        k_in, k_out, k_up, k_down = _layer_weights(i)
        own = i == 0

        def now(keys):
            return keys if own else []
        dact = _matmul(dx, full[k_down], "nt", F32, f"ffn_down_dgrad_{i}")
        gbig[k_down] = bwd_mm(sv["act"], dx, "tn", BF16, f"ffn_down_wgrad_{i}", swap=to_swap)
        to_swap = []
        dup, dcw, dcb = _ffn_conv_bwd(dact, sv["up"], full["ffn_conv_w"][i], ffn_conv_b[i], f"ffn_conv_bwd_{i}")
        gs["ffn_conv_w"][i], gs["ffn_conv_b"][i] = dcw, dcb[0]
        dh2 = bwd_mm(dup, full[k_up], "nt", F32, f"ffn_up_dgrad_{i}", send=now([k_down]))
        gbig[k_up] = bwd_mm(sv["h2"], dup, "tn", BF16, f"ffn_up_wgrad_{i}", swap=now([k_down]))
        dx, dnw = _rms_bwd(sv["x_mid"], ffn_norm_w[i], dh2, dx, f"ffn_norm_bwd_{i}")
        gs["ffn_norm_w"][i] = dnw[0]
        if i % 2 == 0:
            dxp = _perm_tokens(dx)
            do = _matmul(dxp, full[k_out], "nt", F32, f"attn_out_dgrad_{i}")
            gbig[k_out] = _matmul(sv["o"], dxp, "tn", BF16, f"attn_out_wgrad_{i}")
            dqkv = None
            for g in range(3):
                dqkv = _attn_bwd(sv["qkv"], tab, sv["o"], sv["lse"], do, dqkv, g, heads, f"attn_bwd_{i}_{g}")
            gbig[k_in] = bwd_mm(sv["h"], dqkv, "tn", BF16, f"qkv_wgrad_{i}", send=now([k_up, k_out]))
            dh = _unperm_tokens(bwd_mm(dqkv, full[k_in], "nt", F32, f"qkv_dgrad_{i}", send=now([k_in]),
                                       swap=now([k_up, k_out])))
        else:
            dgated = _matmul(dx, full[k_out], "nt", F32, f"ssm_out_dgrad_{i}")
            gbig[k_out] = _matmul(sv["gated"], dx, "tn", BF16, f"ssm_out_wgrad_{i}")
            dy, dz, dgw = _gnorm_bwd(sv["y"], sv["zx"], full["ssm_norm_w"][j], dgated, f"ssm_norm_bwd_{i}")
            gs["ssm_norm_w"][j] = dgw[0]
            dxs_, dbm, dcm, ddtr, dbias, dalog, ddsk = _ssd_bwd(
                sv["conv"], sv["dtr"], *sv["prm"], sv["sprev"], dy, d_inner, f"ssd_bwd_{i}")
            gs["ssm_dt_bias"][j] = dbias.reshape(-1)[::P]
            gs["ssm_a_log"][j] = dalog.reshape(-1)[::P]
            gs["ssm_d"][j] = ddsk.reshape(ssm_heads, P).sum(axis=1)
            ddt = jnp.pad(ddtr[:, ::P], ((0, 0), (0, in_pad - in_w)))
            dzx, dcw, dcb = _ssm_conv_bwd(dxs_, dbm, dcm, sv["conv"], sv["zx"], dz, ddt, full["ssm_conv_w"][j],
                                          d_inner, f"ssm_conv_bwd_{i}")
            gs["ssm_conv_w"][j], gs["ssm_conv_b"][j] = dcw, dcb[0]
            dwin = bwd_mm(sv["h"], dzx, "tn", BF16, f"ssm_in_wgrad_{i}", send=now([k_up, k_out]))
            gbig[k_in] = jnp.stack([dwin[:, k * shard_in:(k + 1) * shard_in] for k in range(N_CHIPS)])
            dh = bwd_mm(dzx, full[k_in], "nt", F32, f"ssm_in_dgrad_{i}", send=now([k_in]), swap=now([k_up, k_out]))
        if sending is not None:
            keys, ex, handles = sending
            recv.update(zip(keys, _finish_exchange(ex, handles, dh, f"scatter_wait_{i + 1}")))
            to_swap, sending = keys, None
        if own:
            to_swap = to_swap + [k_in]
        else:
            keys = [k_in, k_out, k_up, k_down]
            ex = scatter_ex(keys)
            sending = (keys, ex, _start_exchange(ex, f"scatter_start_{i}"))
        issued = 0.0 if own else sending[2][-1][0, 0]
        dx, dnw = _rms_bwd(sv["x_in"], mix_norm_w[i] + issued, dh, dx, f"mix_norm_bwd_{i}")
        gs["mix_norm_w"][i] = dnw[0]
    grad_x = dx.reshape(x.shape)

    sib.update(zip(to_swap, _run_exchange(swap_ex(to_swap), "swap_last")))

    small_full = [jnp.stack(gs[n]) if n != "final_norm_w" else d_final[0] for n in SMALL]
    small_full.append(loss_part[0, 0:1])
    small_shapes = [a.shape for a in small_full]
    summed = _sum_devices(_all_gather_devices(_pack(small_full), "gather_small_grads"), "sum_small_grads")
    small_g = _unpack(summed, small_shapes)
    loss = small_g[-1][0]
    gsm = {}
    for n, g in zip(SMALL, small_g[:-1]):
        if n in SMALL_SHARDED:
            ax = SMALL_SHARDED[n]
            ext = W[n].shape[ax]
            g = lax.dynamic_slice_in_dim(g, chip * ext, ext, axis=ax)
        gsm[n] = g

    out_g, out_d, out_m, out_v = {}, {}, {}, {}
    for name in BIG:
        outs = None
        for l in range(W[name].shape[0]):
            outs = _adamw_big(recv[(name, l)], sib[(name, l)], W[name], M[name], V[name], l, outs,
                              f"adamw_{name}_{l}")
        out_g[name], out_d[name], out_m[name], out_v[name] = outs
    shapes = [W[n].shape for n in SMALL]
    pd, pm, pv = _adamw_small(_pack([gsm[n] for n in SMALL]), _pack([W[n] for n in SMALL]),
                              _pack([M[n] for n in SMALL]), _pack([V[n] for n in SMALL]), "adamw_small")
    for n, d_, m_, v_ in zip(SMALL, _unpack(pd, shapes), _unpack(pm, shapes), _unpack(pv, shapes)):
        out_g[n], out_d[n], out_m[n], out_v[n] = gsm[n], d_, m_, v_

    return (loss, grad_x, *[out_g[n] for n in WEIGHTS], *[out_d[n] for n in WEIGHTS],
            *[out_m[n] for n in WEIGHTS], *[out_v[n] for n in WEIGHTS])
```
